```python
import math
import jax, jax.numpy as jnp
from jax import lax
import numpy as np

D_MODEL = 1024
BATCH = 16
SEQ = 2048
DEPTH = 4

N_A_LAYERS = DEPTH // 2
N_B_LAYERS = DEPTH - N_A_LAYERS
SSM_GROUP = 16
SSM_GROUPS = D_MODEL // SSM_GROUP
SSM_STATE = 64
DT_MIN = 1e-3
DT_MAX = 1e-1
HEAD_DIM = 64
N_HEADS = D_MODEL // HEAD_DIM
DILATED_BRANCHES = ((128, 1), (512, 4), (2048, 16))
N_BRANCHES = len(DILATED_BRANCHES)
BRANCH_WIDTH = N_HEADS * HEAD_DIM
Q_WIDTH = N_BRANCHES * BRANCH_WIDTH
D_FF = 4 * D_MODEL
BLOCK = 128
EPS = 1e-6
NEG = -1e30

kernel_name = "yoco_s5_dilated_attn_hybrid"


def rms_norm(x, g):
    xf = x.astype(jnp.float32)
    y = xf * lax.rsqrt(jnp.mean(xf * xf, axis=-1, keepdims=True) + EPS)
    return (y * g.astype(jnp.float32)).astype(x.dtype)


def ada_chunks(c, w, b, n):
    m = jax.nn.silu(c) @ w + b
    return jnp.split(m[:, None, :], n, axis=-1)


def s5_mixer(u, lam_re, lam_im, log_dt, b_re, b_im, c_re, c_im, d_skip, w_glu):
    bsz, seq, dm = u.shape
    f32 = jnp.float32
    lam = lax.complex(lam_re.astype(f32), lam_im.astype(f32))
    dt = jnp.exp(log_dt.astype(f32))[:, None]
    lam_bar = jnp.exp(lam * dt)
    b_mat = lax.complex(b_re.astype(f32), b_im.astype(f32))
    b_bar = ((lam_bar - 1.0) / lam)[..., None] * b_mat
    c_mat = lax.complex(c_re.astype(f32), c_im.astype(f32))
    uf = u.astype(f32)
    ug = uf.reshape(bsz, seq, SSM_GROUPS, SSM_GROUP).astype(jnp.complex64)
    bu = jnp.einsum('bsgc,gpc->bsgp', ug, b_bar)
    a = jnp.broadcast_to(lam_bar, (1, seq) + lam_bar.shape)

    def combine(left, right):
        a_l, b_l = left
        a_r, b_r = right
        return a_r * a_l, a_r * b_l + b_r

    _, state = lax.associative_scan(combine, (a, bu), axis=1)
    y = jnp.einsum('bsgp,gcp->bsgc', state, c_mat).real.reshape(bsz, seq, dm)
    y = y + d_skip.astype(f32) * uf
    z = jax.nn.gelu(y).astype(u.dtype)
    val, gate = jnp.split(z @ w_glu, 2, axis=-1)
    return val * jax.nn.sigmoid(gate)


def to_dilated_blocks(t, dil):
    bsz, seq = t.shape[:2]
    rest = t.shape[2:]
    sub = seq // dil
    nb = -(-sub // BLOCK)
    t = jnp.swapaxes(t.reshape((bsz, sub, dil) + rest), 1, 2)
    t = jnp.pad(t, [(0, 0), (0, 0), (0, nb * BLOCK - sub)] + [(0, 0)] * len(rest))
    return t.reshape((bsz, dil, nb, BLOCK) + rest)


def from_dilated_blocks(t, seq):
    bsz, dil, nb, blk = t.shape[:4]
    rest = t.shape[4:]
    sub = seq // dil
    t = t.reshape((bsz, dil, nb * blk) + rest)[:, :, :sub]
    return jnp.swapaxes(t, 1, 2).reshape((bsz, seq) + rest)


def band_keys(t):
    prev = jnp.concatenate([jnp.zeros_like(t[:, :, :1]), t[:, :, :-1]], axis=2)
    return jnp.concatenate([prev, t], axis=3)


def band_mask(nb, span):
    qi = jnp.arange(BLOCK)[:, None]
    kj = jnp.arange(2 * BLOCK)[None, :] - BLOCK
    dist = qi - kj
    rel = (dist >= 0) & (dist <= span)
    abs_k = jnp.arange(nb)[:, None, None] * BLOCK + kj[None]
    return rel[None] & (abs_k >= 0)


def dilated_branch(q, k_band, v_band, span, dil):
    f32 = jnp.float32
    seq = q.shape[1]
    qb = to_dilated_blocks(q, dil).astype(f32)
    nb = qb.shape[2]
    s = jnp.einsum('brnqhe,brnkhe->brnhqk', qb, k_band.astype(f32)) * (HEAD_DIM ** -0.5)
    s = jnp.where(band_mask(nb, span)[None, None, :, None], s, NEG)
    m = jnp.max(s, axis=-1, keepdims=True)
    p = jnp.exp(s - m)
    den = jnp.sum(p, axis=-1)
    o = jnp.einsum('brnhqk,brnkhe->brnqhe', p, v_band.astype(f32))
    o = o / jnp.swapaxes(den, 3, 4)[..., None]
    lse = jnp.swapaxes(m[..., 0] + jnp.log(den), 3, 4)
    return from_dilated_blocks(o, seq), from_dilated_blocks(lse, seq)


def shared_kv(h, c, kv_g, kv_ada_w, kv_ada_b, w_kv):
    bsz, seq, _ = h.shape
    shift, scale = ada_chunks(c, kv_ada_w, kv_ada_b, 2)
    u = rms_norm(h, kv_g) * (1.0 + scale) + shift
    kv = (u @ w_kv).reshape(bsz, seq, 2, N_BRANCHES, N_HEADS, HEAD_DIM)
    k_bands, v_bands = [], []
    for i, (win, dil) in enumerate(DILATED_BRANCHES):
        k_bands.append(band_keys(to_dilated_blocks(kv[:, :, 0, i], dil)))
        v_bands.append(band_keys(to_dilated_blocks(kv[:, :, 1, i], dil)))
    return k_bands, v_bands


def dilated_mixer(u, w_q, k_bands, v_bands, w_o):
    bsz, seq, _ = u.shape
    q = (u @ w_q).reshape(bsz, seq, N_BRANCHES, N_HEADS, HEAD_DIM)
    outs, lses = [], []
    for i, (win, dil) in enumerate(DILATED_BRANCHES):
        o, l = dilated_branch(q[:, :, i], k_bands[i], v_bands[i], win // dil, dil)
        outs.append(o)
        lses.append(l)
    weights = jax.nn.softmax(jnp.stack(lses, axis=-1), axis=-1)
    o = jnp.einsum('gbshe,bshg->bshe', jnp.stack(outs), weights)
    return o.reshape(bsz, seq, BRANCH_WIDTH).astype(u.dtype) @ w_o


def _fwd_setup_inputs(seed: int = 0) -> dict:
    key = jax.random.key(seed)
    ks = jax.random.split(key, 24)
    f32 = jnp.float32

    def nrm(k, shape, std):
        return jax.random.normal(k, shape, f32) * std

    n_idx = jnp.arange(SSM_STATE, dtype=f32)
    gp = (N_A_LAYERS, SSM_GROUPS, SSM_STATE)
    return {
        "x": nrm(ks[0], (BATCH, SEQ, D_MODEL), 1.0),
        "c": nrm(ks[1], (BATCH, D_MODEL), 1.0),
        "ln_g": 1.0 + nrm(ks[2], (DEPTH, 2, D_MODEL), 0.02),
        "ada_w": nrm(ks[3], (DEPTH, 2, D_MODEL, 3 * D_MODEL), 0.5 * D_MODEL ** -0.5),
        "ada_b": nrm(ks[4], (DEPTH, 2, 3 * D_MODEL), 0.02),
        "ssm_lam_re": -0.5 + nrm(ks[5], gp, 0.01),
        "ssm_lam_im": math.pi * n_idx + nrm(ks[6], gp, 0.01),
        "ssm_log_dt": jax.random.uniform(ks[7], (N_A_LAYERS, SSM_GROUPS), f32, math.log(DT_MIN), math.log(DT_MAX)),
        "ssm_b_re": nrm(ks[8], gp + (SSM_GROUP,), (2 * SSM_GROUP) ** -0.5),
        "ssm_b_im": nrm(ks[9], gp + (SSM_GROUP,), (2 * SSM_GROUP) ** -0.5),
        "ssm_c_re": nrm(ks[10], (N_A_LAYERS, SSM_GROUPS, SSM_GROUP, SSM_STATE), 0.5),
        "ssm_c_im": nrm(ks[11], (N_A_LAYERS, SSM_GROUPS, SSM_GROUP, SSM_STATE), 0.5),
        "ssm_d": nrm(ks[12], (N_A_LAYERS, D_MODEL), 1.0),
        "ssm_w_glu": nrm(ks[13], (N_A_LAYERS, D_MODEL, 2 * D_MODEL), D_MODEL ** -0.5),
        "kv_g": 1.0 + nrm(ks[14], (D_MODEL,), 0.02),
        "kv_ada_w": nrm(ks[15], (D_MODEL, 2 * D_MODEL), 0.5 * D_MODEL ** -0.5),
        "kv_ada_b": nrm(ks[16], (2 * D_MODEL,), 0.02),
        "w_kv": nrm(ks[17], (D_MODEL, 2 * Q_WIDTH), D_MODEL ** -0.5),
        "attn_w_q": nrm(ks[18], (N_B_LAYERS, D_MODEL, Q_WIDTH), D_MODEL ** -0.5),
        "attn_w_o": nrm(ks[19], (N_B_LAYERS, BRANCH_WIDTH, D_MODEL), BRANCH_WIDTH ** -0.5),
        "mlp_w1": nrm(ks[20], (DEPTH, D_MODEL, D_FF), D_MODEL ** -0.5),
        "mlp_w2": nrm(ks[21], (DEPTH, D_FF, D_MODEL), D_FF ** -0.5),
        "final_g": 1.0 + nrm(ks[22], (D_MODEL,), 0.02),
    }


def _fwd_reference(x, c, ln_g, ada_w, ada_b, ssm_lam_re, ssm_lam_im, ssm_log_dt, ssm_b_re, ssm_b_im,
              ssm_c_re, ssm_c_im, ssm_d, ssm_w_glu, kv_g, kv_ada_w, kv_ada_b, w_kv,
              attn_w_q, attn_w_o, mlp_w1, mlp_w2, final_g):
    h = x
    k_bands, v_bands = None, None
    for layer in range(DEPTH):
        if layer == N_A_LAYERS:
            k_bands, v_bands = shared_kv(h, c, kv_g, kv_ada_w, kv_ada_b, w_kv)
        shift, scale, gate = ada_chunks(c, ada_w[layer, 0], ada_b[layer, 0], 3)
        u = rms_norm(h, ln_g[layer, 0]) * (1.0 + scale) + shift
        if layer < N_A_LAYERS:
            y = s5_mixer(u, ssm_lam_re[layer], ssm_lam_im[layer], ssm_log_dt[layer], ssm_b_re[layer],
                         ssm_b_im[layer], ssm_c_re[layer], ssm_c_im[layer], ssm_d[layer], ssm_w_glu[layer])
        else:
            j = layer - N_A_LAYERS
            y = dilated_mixer(u, attn_w_q[j], k_bands, v_bands, attn_w_o[j])
        h = h + gate * y
        shift, scale, gate = ada_chunks(c, ada_w[layer, 1], ada_b[layer, 1], 3)
        u = rms_norm(h, ln_g[layer, 1]) * (1.0 + scale) + shift
        h = h + gate * (jnp.square(jax.nn.relu(u @ mlp_w1[layer])) @ mlp_w2[layer])
    return rms_norm(h, final_g)


import jax as _jax
import jax.numpy as _jnp

TWIN_FORMAT = 'train_step'
FWD_PARAMS = ['x', 'c', 'ln_g', 'ada_w', 'ada_b', 'ssm_lam_re', 'ssm_lam_im', 'ssm_log_dt', 'ssm_b_re', 'ssm_b_im', 'ssm_c_re', 'ssm_c_im', 'ssm_d', 'ssm_w_glu', 'kv_g', 'kv_ada_w', 'kv_ada_b', 'w_kv', 'attn_w_q', 'attn_w_o', 'mlp_w1', 'mlp_w2', 'final_g']
TWIN_WEIGHTS = ['ln_g', 'ada_w', 'ada_b', 'ssm_lam_re', 'ssm_lam_im', 'ssm_log_dt', 'ssm_b_re', 'ssm_b_im', 'ssm_c_re', 'ssm_c_im', 'ssm_d', 'ssm_w_glu', 'kv_g', 'kv_ada_w', 'kv_ada_b', 'w_kv', 'attn_w_q', 'attn_w_o', 'mlp_w1', 'mlp_w2', 'final_g']
TWIN_DIFF_INPUT = 'x'
TWIN_INPUTS = ['x', 'c', 'ln_g', 'ada_w', 'ada_b', 'ssm_lam_re', 'ssm_lam_im', 'ssm_log_dt', 'ssm_b_re', 'ssm_b_im', 'ssm_c_re', 'ssm_c_im', 'ssm_d', 'ssm_w_glu', 'kv_g', 'kv_ada_w', 'kv_ada_b', 'w_kv', 'attn_w_q', 'attn_w_o', 'mlp_w1', 'mlp_w2', 'final_g', 'loss_target', 'm_ln_g', 'm_ada_w', 'm_ada_b', 'm_ssm_lam_re', 'm_ssm_lam_im', 'm_ssm_log_dt', 'm_ssm_b_re', 'm_ssm_b_im', 'm_ssm_c_re', 'm_ssm_c_im', 'm_ssm_d', 'm_ssm_w_glu', 'm_kv_g', 'm_kv_ada_w', 'm_kv_ada_b', 'm_w_kv', 'm_attn_w_q', 'm_attn_w_o', 'm_mlp_w1', 'm_mlp_w2', 'm_final_g', 'v_ln_g', 'v_ada_w', 'v_ada_b', 'v_ssm_lam_re', 'v_ssm_lam_im', 'v_ssm_log_dt', 'v_ssm_b_re', 'v_ssm_b_im', 'v_ssm_c_re', 'v_ssm_c_im', 'v_ssm_d', 'v_ssm_w_glu', 'v_kv_g', 'v_kv_ada_w', 'v_kv_ada_b', 'v_w_kv', 'v_attn_w_q', 'v_attn_w_o', 'v_mlp_w1', 'v_mlp_w2', 'v_final_g']
TWIN_OUTPUTS = ['loss', 'grad_x', 'grad_ln_g', 'grad_ada_w', 'grad_ada_b', 'grad_ssm_lam_re', 'grad_ssm_lam_im', 'grad_ssm_log_dt', 'grad_ssm_b_re', 'grad_ssm_b_im', 'grad_ssm_c_re', 'grad_ssm_c_im', 'grad_ssm_d', 'grad_ssm_w_glu', 'grad_kv_g', 'grad_kv_ada_w', 'grad_kv_ada_b', 'grad_w_kv', 'grad_attn_w_q', 'grad_attn_w_o', 'grad_mlp_w1', 'grad_mlp_w2', 'grad_final_g', 'delta_ln_g', 'delta_ada_w', 'delta_ada_b', 'delta_ssm_lam_re', 'delta_ssm_lam_im', 'delta_ssm_log_dt', 'delta_ssm_b_re', 'delta_ssm_b_im', 'delta_ssm_c_re', 'delta_ssm_c_im', 'delta_ssm_d', 'delta_ssm_w_glu', 'delta_kv_g', 'delta_kv_ada_w', 'delta_kv_ada_b', 'delta_w_kv', 'delta_attn_w_q', 'delta_attn_w_o', 'delta_mlp_w1', 'delta_mlp_w2', 'delta_final_g', 'new_m_ln_g', 'new_m_ada_w', 'new_m_ada_b', 'new_m_ssm_lam_re', 'new_m_ssm_lam_im', 'new_m_ssm_log_dt', 'new_m_ssm_b_re', 'new_m_ssm_b_im', 'new_m_ssm_c_re', 'new_m_ssm_c_im', 'new_m_ssm_d', 'new_m_ssm_w_glu', 'new_m_kv_g', 'new_m_kv_ada_w', 'new_m_kv_ada_b', 'new_m_w_kv', 'new_m_attn_w_q', 'new_m_attn_w_o', 'new_m_mlp_w1', 'new_m_mlp_w2', 'new_m_final_g', 'new_v_ln_g', 'new_v_ada_w', 'new_v_ada_b', 'new_v_ssm_lam_re', 'new_v_ssm_lam_im', 'new_v_ssm_log_dt', 'new_v_ssm_b_re', 'new_v_ssm_b_im', 'new_v_ssm_c_re', 'new_v_ssm_c_im', 'new_v_ssm_d', 'new_v_ssm_w_glu', 'new_v_kv_g', 'new_v_kv_ada_w', 'new_v_kv_ada_b', 'new_v_w_kv', 'new_v_attn_w_q', 'new_v_attn_w_o', 'new_v_mlp_w1', 'new_v_mlp_w2', 'new_v_final_g']
TWIN_LEAF_KINDS = {'loss': 'loss', 'grad_x': 'grad_x', 'grad_ln_g': 'grad_w', 'grad_ada_w': 'grad_w', 'grad_ada_b': 'grad_w', 'grad_ssm_lam_re': 'grad_w', 'grad_ssm_lam_im': 'grad_w', 'grad_ssm_log_dt': 'grad_w', 'grad_ssm_b_re': 'grad_w', 'grad_ssm_b_im': 'grad_w', 'grad_ssm_c_re': 'grad_w', 'grad_ssm_c_im': 'grad_w', 'grad_ssm_d': 'grad_w', 'grad_ssm_w_glu': 'grad_w', 'grad_kv_g': 'grad_w', 'grad_kv_ada_w': 'grad_w', 'grad_kv_ada_b': 'grad_w', 'grad_w_kv': 'grad_w', 'grad_attn_w_q': 'grad_w', 'grad_attn_w_o': 'grad_w', 'grad_mlp_w1': 'grad_w', 'grad_mlp_w2': 'grad_w', 'grad_final_g': 'grad_w', 'delta_ln_g': 'delta_w', 'delta_ada_w': 'delta_w', 'delta_ada_b': 'delta_w', 'delta_ssm_lam_re': 'delta_w', 'delta_ssm_lam_im': 'delta_w', 'delta_ssm_log_dt': 'delta_w', 'delta_ssm_b_re': 'delta_w', 'delta_ssm_b_im': 'delta_w', 'delta_ssm_c_re': 'delta_w', 'delta_ssm_c_im': 'delta_w', 'delta_ssm_d': 'delta_w', 'delta_ssm_w_glu': 'delta_w', 'delta_kv_g': 'delta_w', 'delta_kv_ada_w': 'delta_w', 'delta_kv_ada_b': 'delta_w', 'delta_w_kv': 'delta_w', 'delta_attn_w_q': 'delta_w', 'delta_attn_w_o': 'delta_w', 'delta_mlp_w1': 'delta_w', 'delta_mlp_w2': 'delta_w', 'delta_final_g': 'delta_w', 'new_m_ln_g': 'new_m', 'new_m_ada_w': 'new_m', 'new_m_ada_b': 'new_m', 'new_m_ssm_lam_re': 'new_m', 'new_m_ssm_lam_im': 'new_m', 'new_m_ssm_log_dt': 'new_m', 'new_m_ssm_b_re': 'new_m', 'new_m_ssm_b_im': 'new_m', 'new_m_ssm_c_re': 'new_m', 'new_m_ssm_c_im': 'new_m', 'new_m_ssm_d': 'new_m', 'new_m_ssm_w_glu': 'new_m', 'new_m_kv_g': 'new_m', 'new_m_kv_ada_w': 'new_m', 'new_m_kv_ada_b': 'new_m', 'new_m_w_kv': 'new_m', 'new_m_attn_w_q': 'new_m', 'new_m_attn_w_o': 'new_m', 'new_m_mlp_w1': 'new_m', 'new_m_mlp_w2': 'new_m', 'new_m_final_g': 'new_m', 'new_v_ln_g': 'new_v', 'new_v_ada_w': 'new_v', 'new_v_ada_b': 'new_v', 'new_v_ssm_lam_re': 'new_v', 'new_v_ssm_lam_im': 'new_v', 'new_v_ssm_log_dt': 'new_v', 'new_v_ssm_b_re': 'new_v', 'new_v_ssm_b_im': 'new_v', 'new_v_ssm_c_re': 'new_v', 'new_v_ssm_c_im': 'new_v', 'new_v_ssm_d': 'new_v', 'new_v_ssm_w_glu': 'new_v', 'new_v_kv_g': 'new_v', 'new_v_kv_ada_w': 'new_v', 'new_v_kv_ada_b': 'new_v', 'new_v_w_kv': 'new_v', 'new_v_attn_w_q': 'new_v', 'new_v_attn_w_o': 'new_v', 'new_v_mlp_w1': 'new_v', 'new_v_mlp_w2': 'new_v', 'new_v_final_g': 'new_v'}


def _forward(args):
    return _fwd_reference(*[args[k] for k in FWD_PARAMS])


def _output_shape():
    out = _jax.eval_shape(lambda: _forward(_fwd_setup_inputs(0)))
    return out.shape, out.dtype

N_MICROBATCH = 1
ADAM_LR = 0.001
ADAM_B1 = 0.9
ADAM_B2 = 0.999
ADAM_EPS = 1e-08
ADAM_WD = 0.01
ADAM_STEP = 10
PER_EXAMPLE_BATCH_AXIS = {'x': 0, 'c': 0, 'loss_target': 0}
SHARED_INPUTS = []
_WEIGHT_DTYPES = {'ln_g': _jnp.float32, 'ada_w': _jnp.float32, 'ada_b': _jnp.float32, 'ssm_lam_re': _jnp.float32, 'ssm_lam_im': _jnp.float32, 'ssm_log_dt': _jnp.float32, 'ssm_b_re': _jnp.float32, 'ssm_b_im': _jnp.float32, 'ssm_c_re': _jnp.float32, 'ssm_c_im': _jnp.float32, 'ssm_d': _jnp.float32, 'ssm_w_glu': _jnp.float32, 'kv_g': _jnp.float32, 'kv_ada_w': _jnp.float32, 'kv_ada_b': _jnp.float32, 'w_kv': _jnp.float32, 'attn_w_q': _jnp.float32, 'attn_w_o': _jnp.float32, 'mlp_w1': _jnp.float32, 'mlp_w2': _jnp.float32, 'final_g': _jnp.float32}
MOMENT_SCALE = {'ln_g': 5.522536e-02, 'ada_w': 7.407667e-02, 'ada_b': 1.244574e-01, 'ssm_lam_re': 1.159820e-02, 'ssm_lam_im': 1.213110e-02, 'ssm_log_dt': 4.206723e+00, 'ssm_b_re': 6.429175e-03, 'ssm_b_im': 5.571230e-03, 'ssm_c_re': 2.267778e-03, 'ssm_c_im': 2.095246e-03, 'ssm_d': 2.463809e-02, 'ssm_w_glu': 1.848995e-02, 'kv_g': 2.370537e-02, 'kv_ada_w': 2.315849e-02, 'kv_ada_b': 3.619338e-02, 'w_kv': 1.147246e-02, 'attn_w_q': 6.020461e-03, 'attn_w_o': 1.702588e-02, 'mlp_w1': 3.959286e-02, 'mlp_w2': 7.435551e-02, 'final_g': 3.258302e+01}


def _to_microbatches(a, axis):
    t = _jnp.moveaxis(a, axis, 0)
    t = t.reshape((N_MICROBATCH, t.shape[0] // N_MICROBATCH) + t.shape[1:])
    return _jnp.moveaxis(t, 1, axis + 1)


def setup_inputs(seed: int = 0) -> dict:
    inp = _fwd_setup_inputs(seed)
    key = _jax.random.fold_in(_jax.random.key(seed), 7919)
    shape, _ = _output_shape()
    out = dict(inp)
    out["loss_target"] = _jax.random.normal(_jax.random.fold_in(key, 0), shape, _jnp.float32)
    for i, name in enumerate(TWIN_WEIGHTS):
        w = inp[name].astype(_jnp.float32)
        if MOMENT_SCALE is None:
            s = _jnp.sqrt(_jnp.mean(_jnp.square(w)) + 1e-30)
        else:
            s = MOMENT_SCALE[name]
        km, kv = _jax.random.split(_jax.random.fold_in(key, i + 1))
        out[name] = w
        out["m_" + name] = s * _jax.random.normal(km, w.shape, _jnp.float32)
        out["v_" + name] = (s * s) * _jax.random.uniform(kv, w.shape, _jnp.float32, 0.5, 1.5)
    if N_MICROBATCH > 1:
        for name, axis in PER_EXAMPLE_BATCH_AXIS.items():
            out[name] = _to_microbatches(out[name], axis)
    return {'x': out['x'], 'c': out['c'], 'ln_g': out['ln_g'], 'ada_w': out['ada_w'], 'ada_b': out['ada_b'], 'ssm_lam_re': out['ssm_lam_re'], 'ssm_lam_im': out['ssm_lam_im'], 'ssm_log_dt': out['ssm_log_dt'], 'ssm_b_re': out['ssm_b_re'], 'ssm_b_im': out['ssm_b_im'], 'ssm_c_re': out['ssm_c_re'], 'ssm_c_im': out['ssm_c_im'], 'ssm_d': out['ssm_d'], 'ssm_w_glu': out['ssm_w_glu'], 'kv_g': out['kv_g'], 'kv_ada_w': out['kv_ada_w'], 'kv_ada_b': out['kv_ada_b'], 'w_kv': out['w_kv'], 'attn_w_q': out['attn_w_q'], 'attn_w_o': out['attn_w_o'], 'mlp_w1': out['mlp_w1'], 'mlp_w2': out['mlp_w2'], 'final_g': out['final_g'], 'loss_target': out['loss_target'], 'm_ln_g': out['m_ln_g'], 'm_ada_w': out['m_ada_w'], 'm_ada_b': out['m_ada_b'], 'm_ssm_lam_re': out['m_ssm_lam_re'], 'm_ssm_lam_im': out['m_ssm_lam_im'], 'm_ssm_log_dt': out['m_ssm_log_dt'], 'm_ssm_b_re': out['m_ssm_b_re'], 'm_ssm_b_im': out['m_ssm_b_im'], 'm_ssm_c_re': out['m_ssm_c_re'], 'm_ssm_c_im': out['m_ssm_c_im'], 'm_ssm_d': out['m_ssm_d'], 'm_ssm_w_glu': out['m_ssm_w_glu'], 'm_kv_g': out['m_kv_g'], 'm_kv_ada_w': out['m_kv_ada_w'], 'm_kv_ada_b': out['m_kv_ada_b'], 'm_w_kv': out['m_w_kv'], 'm_attn_w_q': out['m_attn_w_q'], 'm_attn_w_o': out['m_attn_w_o'], 'm_mlp_w1': out['m_mlp_w1'], 'm_mlp_w2': out['m_mlp_w2'], 'm_final_g': out['m_final_g'], 'v_ln_g': out['v_ln_g'], 'v_ada_w': out['v_ada_w'], 'v_ada_b': out['v_ada_b'], 'v_ssm_lam_re': out['v_ssm_lam_re'], 'v_ssm_lam_im': out['v_ssm_lam_im'], 'v_ssm_log_dt': out['v_ssm_log_dt'], 'v_ssm_b_re': out['v_ssm_b_re'], 'v_ssm_b_im': out['v_ssm_b_im'], 'v_ssm_c_re': out['v_ssm_c_re'], 'v_ssm_c_im': out['v_ssm_c_im'], 'v_ssm_d': out['v_ssm_d'], 'v_ssm_w_glu': out['v_ssm_w_glu'], 'v_kv_g': out['v_kv_g'], 'v_kv_ada_w': out['v_kv_ada_w'], 'v_kv_ada_b': out['v_kv_ada_b'], 'v_w_kv': out['v_w_kv'], 'v_attn_w_q': out['v_attn_w_q'], 'v_attn_w_o': out['v_attn_w_o'], 'v_mlp_w1': out['v_mlp_w1'], 'v_mlp_w2': out['v_mlp_w2'], 'v_final_g': out['v_final_g']}


def _loss(weights, diff, rest, loss_target):
    with _jax.named_scope("forward"):
        args = {**rest, TWIN_DIFF_INPUT: diff, **{k: w.astype(_WEIGHT_DTYPES[k]) for k, w in weights.items()}}
        y = _forward(args)
    with _jax.named_scope("loss_head"):
        err = _jnp.square(y.astype(_jnp.float32) - loss_target)
        return 0.5 * _jnp.sum(_jnp.mean(err, axis=-1)) if err.ndim else 0.5 * err


def _adamw(w, g, m, v):
    m = ADAM_B1 * m + (1.0 - ADAM_B1) * g
    v = ADAM_B2 * v + (1.0 - ADAM_B2) * _jnp.square(g)
    m_hat = m / (1.0 - ADAM_B1 ** ADAM_STEP)
    v_hat = v / (1.0 - ADAM_B2 ** ADAM_STEP)
    delta = -ADAM_LR * (m_hat / (_jnp.sqrt(v_hat) + ADAM_EPS) + ADAM_WD * w)
    return delta, m, v


def reference(x, c, ln_g, ada_w, ada_b, ssm_lam_re, ssm_lam_im, ssm_log_dt, ssm_b_re, ssm_b_im, ssm_c_re, ssm_c_im, ssm_d, ssm_w_glu, kv_g, kv_ada_w, kv_ada_b, w_kv, attn_w_q, attn_w_o, mlp_w1, mlp_w2, final_g, loss_target, m_ln_g, m_ada_w, m_ada_b, m_ssm_lam_re, m_ssm_lam_im, m_ssm_log_dt, m_ssm_b_re, m_ssm_b_im, m_ssm_c_re, m_ssm_c_im, m_ssm_d, m_ssm_w_glu, m_kv_g, m_kv_ada_w, m_kv_ada_b, m_w_kv, m_attn_w_q, m_attn_w_o, m_mlp_w1, m_mlp_w2, m_final_g, v_ln_g, v_ada_w, v_ada_b, v_ssm_lam_re, v_ssm_lam_im, v_ssm_log_dt, v_ssm_b_re, v_ssm_b_im, v_ssm_c_re, v_ssm_c_im, v_ssm_d, v_ssm_w_glu, v_kv_g, v_kv_ada_w, v_kv_ada_b, v_w_kv, v_attn_w_q, v_attn_w_o, v_mlp_w1, v_mlp_w2, v_final_g):
    given = dict(x=x, c=c, ln_g=ln_g, ada_w=ada_w, ada_b=ada_b, ssm_lam_re=ssm_lam_re, ssm_lam_im=ssm_lam_im, ssm_log_dt=ssm_log_dt, ssm_b_re=ssm_b_re, ssm_b_im=ssm_b_im, ssm_c_re=ssm_c_re, ssm_c_im=ssm_c_im, ssm_d=ssm_d, ssm_w_glu=ssm_w_glu, kv_g=kv_g, kv_ada_w=kv_ada_w, kv_ada_b=kv_ada_b, w_kv=w_kv, attn_w_q=attn_w_q, attn_w_o=attn_w_o, mlp_w1=mlp_w1, mlp_w2=mlp_w2, final_g=final_g, loss_target=loss_target, m_ln_g=m_ln_g, m_ada_w=m_ada_w, m_ada_b=m_ada_b, m_ssm_lam_re=m_ssm_lam_re, m_ssm_lam_im=m_ssm_lam_im, m_ssm_log_dt=m_ssm_log_dt, m_ssm_b_re=m_ssm_b_re, m_ssm_b_im=m_ssm_b_im, m_ssm_c_re=m_ssm_c_re, m_ssm_c_im=m_ssm_c_im, m_ssm_d=m_ssm_d, m_ssm_w_glu=m_ssm_w_glu, m_kv_g=m_kv_g, m_kv_ada_w=m_kv_ada_w, m_kv_ada_b=m_kv_ada_b, m_w_kv=m_w_kv, m_attn_w_q=m_attn_w_q, m_attn_w_o=m_attn_w_o, m_mlp_w1=m_mlp_w1, m_mlp_w2=m_mlp_w2, m_final_g=m_final_g, v_ln_g=v_ln_g, v_ada_w=v_ada_w, v_ada_b=v_ada_b, v_ssm_lam_re=v_ssm_lam_re, v_ssm_lam_im=v_ssm_lam_im, v_ssm_log_dt=v_ssm_log_dt, v_ssm_b_re=v_ssm_b_re, v_ssm_b_im=v_ssm_b_im, v_ssm_c_re=v_ssm_c_re, v_ssm_c_im=v_ssm_c_im, v_ssm_d=v_ssm_d, v_ssm_w_glu=v_ssm_w_glu, v_kv_g=v_kv_g, v_kv_ada_w=v_kv_ada_w, v_kv_ada_b=v_kv_ada_b, v_w_kv=v_w_kv, v_attn_w_q=v_attn_w_q, v_attn_w_o=v_attn_w_o, v_mlp_w1=v_mlp_w1, v_mlp_w2=v_mlp_w2, v_final_g=v_final_g)
    weights = {n: given[n] for n in TWIN_WEIGHTS}
    shared = {n: given[n] for n in SHARED_INPUTS}
    per_example = {n: given[n] for n in ['x', 'c']}
    grad_fn = _jax.value_and_grad(_loss, argnums=(0, 1))

    def one_microbatch(ex, loss_target):
        ex = dict(ex)
        diff = ex.pop(TWIN_DIFF_INPUT)
        return grad_fn(weights, diff, {**shared, **ex}, loss_target)

    if N_MICROBATCH == 1:
        loss, (grad_w, grad_x) = one_microbatch(per_example, given["loss_target"])
    else:
        def body(carry, xs):
            loss_sum, grad_sum = carry
            l_k, (gw_k, gx_k) = one_microbatch(xs[0], xs[1])
            with _jax.named_scope("update"):
                return (loss_sum + l_k, _jax.tree.map(_jnp.add, grad_sum, gw_k)), gx_k

        init = (_jnp.zeros((), _jnp.float32), _jax.tree.map(_jnp.zeros_like, weights))
        (loss, grad_w), grad_x = _jax.lax.scan(body, init, (per_example, given["loss_target"]))
    with _jax.named_scope("update"):
        delta_w, new_m, new_v = {}, {}, {}
        for n in TWIN_WEIGHTS:
            delta_w[n], new_m[n], new_v[n] = _adamw(weights[n], grad_w[n], given["m_" + n], given["v_" + n])
    return (loss, grad_x, *[grad_w[n] for n in TWIN_WEIGHTS], *[delta_w[n] for n in TWIN_WEIGHTS],
            *[new_m[n] for n in TWIN_WEIGHTS], *[new_v[n] for n in TWIN_WEIGHTS])
```

```python
import functools
import math

import jax
import jax.numpy as jnp
from jax import lax
from jax.experimental import pallas as pl
from jax.experimental.pallas import tpu as pltpu

F32 = jnp.float32
BF16 = jnp.bfloat16
MESH = pl.DeviceIdType.MESH

EPS = 1e-6
NEG = -1e30
SSM_GROUP = 16
SSM_STATE = 64
HEAD_DIM = 64
ATTN_BLOCK = 128
DILATIONS = (1, 4, 16)
ADAM_LR, ADAM_B1, ADAM_B2, ADAM_EPS, ADAM_WD, ADAM_STEP = 0.001, 0.9, 0.999, 1e-08, 0.01, 10

LANES = 128
SUBLANES = 8
CHUNK_GROUPS = LANES // SSM_GROUP
CHUNK_STATE = CHUNK_GROUPS * SSM_STATE
VMEM_LIMIT = 56 * 1024 * 1024


def _div(dim, pref, mult):
    t = min(pref, dim) // mult * mult
    while t >= mult:
        if dim % t == 0:
            return t
        t -= mult
    return dim


def _params(*sem):
    return pltpu.CompilerParams(dimension_semantics=sem, vmem_limit_bytes=VMEM_LIMIT)


def _mm(name, a, b4, *, mode, M, N, K, b_lay="cs", b_l=0, b_s0=0, b_ns=1, out_dtype=F32, out_lay=None, out4_shape=None,
        out_into=None, out_l=0, out_s0=0, out_ns=1, epi=None, extras=(), rows_per_ex=None, tm=1024, tn=1024, tk=512):
    _, _, bR, bC = b4.shape
    tm = _div(M, tm, SUBLANES if M % 16 else 16)
    brows, bcols = (N, K) if mode == "nt" else (K, N)
    if b_lay == "cs":
        assert bR == brows and bC * b_ns == bcols, (name, b4.shape, brows, bcols)
    else:
        assert bC == bcols and bR * b_ns == brows, (name, b4.shape, brows, bcols)
    n_lim = N
    k_lim = K
    if mode == "nt":
        if b_lay == "cs":
            k_lim = bC
        else:
            n_lim = bR
    else:
        if b_lay == "cs":
            n_lim = bC
        else:
            k_lim = bR
    if out_lay == "cs":
        oR, oC = out4_shape[2], out4_shape[3]
        assert oR == M and oC * out_ns == N, (name, out4_shape, M, N)
        n_lim = math.gcd(n_lim, oC)
    elif out_lay == "rs":
        oR, oC = out4_shape[2], out4_shape[3]
        assert oC == N and oR * out_ns == M, (name, out4_shape, M, N)
        tm = _div(oR, tm, SUBLANES)
    tn = _div(n_lim, tn, LANES)
    tk = _div(k_lim, tk, LANES if mode != "tn" else SUBLANES)
    if mode == "tn":
        tk = _div(k_lim, tk, 16) if k_lim % 16 == 0 else tk
    nk = K // tk
    grid = (M // tm, N // tn, nk)

    if mode == "tn":
        a_spec = pl.BlockSpec((tk, tm), lambda i, j, k: (k, i))
    else:
        a_spec = pl.BlockSpec((tm, tk), lambda i, j, k: (i, k))

    def b_index(ri, ci, br, bc):
        if b_lay == "cs":
            per = bC // bc
            return (b_s0 + ci // per, b_l, ri, ci % per)
        per = bR // br
        return (b_s0 + ri // per, b_l, ri % per, ci)

    if mode == "nt":
        b_spec = pl.BlockSpec((None, None, tn, tk), lambda i, j, k: b_index(j, k, tn, tk))
    else:
        b_spec = pl.BlockSpec((None, None, tk, tn), lambda i, j, k: b_index(k, j, tk, tn))

    in_specs = [a_spec, b_spec]
    operands = [a, b4]
    for kind, arr in extras:
        if kind == "mn":
            in_specs.append(pl.BlockSpec((tm, tn), lambda i, j, k: (i, j)))
        elif kind == "ex":
            per_ex = rows_per_ex // tm
            in_specs.append(pl.BlockSpec((None, 1, tn), lambda i, j, k: (i // per_ex, 0, j)))
        else:
            in_specs.append(pl.BlockSpec((1, tn), lambda i, j, k: (0, j)))
        operands.append(arr)
    n_extra = len(extras)

    if out_lay is None:
        out_shape = jax.ShapeDtypeStruct((M, N), out_dtype)
        out_spec = pl.BlockSpec((tm, tn), lambda i, j, k: (i, j))
    else:
        out_shape = jax.ShapeDtypeStruct(tuple(out4_shape), out_dtype)
        if out_lay == "cs":
            per_o = oC // tn
            out_spec = pl.BlockSpec((None, None, tm, tn), lambda i, j, k: (out_s0 + j // per_o, out_l, i, j % per_o))
        else:
            per_o = oR // tm
            out_spec = pl.BlockSpec((None, None, tm, tn), lambda i, j, k: (out_s0 + i // per_o, out_l, i % per_o, j))
    aliases = {}
    if out_into is not None:
        in_specs.append(pl.BlockSpec(memory_space=pl.ANY))
        operands.append(out_into)
        aliases = {len(operands) - 1: 0}

    dims = {"nn": (((1,), (0,)), ((), ())), "nt": (((1,), (1,)), ((), ())), "tn": (((0,), (0,)), ((), ()))}[mode]

    def body(a_ref, b_ref, *rest):
        extra_refs = rest[:n_extra]
        o_ref, acc = rest[-2], rest[-1]
        k = pl.program_id(2)

        @pl.when(k == 0)
        def _():
            acc[...] = jnp.zeros_like(acc)

        acc[...] += lax.dot_general(a_ref[...].astype(BF16), b_ref[...].astype(BF16), dims, preferred_element_type=F32)

        @pl.when(k == nk - 1)
        def _():
            r = acc[...]
            if epi is not None:
                r = epi(r, *[e[...] for e in extra_refs])
            o_ref[...] = r.astype(o_ref.dtype)

    return pl.pallas_call(
        body, name=name, grid=grid, in_specs=in_specs, out_specs=out_spec, out_shape=out_shape,
        scratch_shapes=[pltpu.VMEM((tm, tn), F32)], input_output_aliases=aliases,
        compiler_params=_params("parallel", "parallel", "arbitrary"),
    )(*operands)


def _as4(w):
    return w.reshape((1, 1) + w.shape)


def _relu2(acc):
    r = jnp.maximum(acc, 0.0)
    return r * r


def _relu2_bwd(acc, r):
    return acc * (2.0 * jnp.sqrt(r.astype(F32)))


def _add(acc, e):
    return acc + e


def _row_tiles(N, B, pref=256):
    S = N // B
    tm = _div(S, pref, SUBLANES)
    return tm, S // tm


def _normmod(h, g, scale, shift, B):
    N, D = h.shape
    tm, per_ex = _row_tiles(N, B)

    def body(h_ref, g_ref, sc_ref, sh_ref, u_ref):
        x = h_ref[...]
        rstd = lax.rsqrt(jnp.mean(x * x, axis=-1, keepdims=True) + EPS)
        y = (x * rstd) * g_ref[...]
        u_ref[...] = (y * (1.0 + sc_ref[...]) + sh_ref[...]).astype(u_ref.dtype)

    tok = pl.BlockSpec((tm, D), lambda i: (i, 0))
    vec = pl.BlockSpec((1, D), lambda i: (0, 0))
    ex = pl.BlockSpec((None, 1, D), lambda i: (i // per_ex, 0, 0))
    return pl.pallas_call(
        body, name="normmod_fwd", grid=(N // tm,), in_specs=[tok, vec, ex, ex], out_specs=tok,
        out_shape=jax.ShapeDtypeStruct((N, D), BF16), compiler_params=_params("parallel"),
    )(h, g, scale, shift)


def _normmod_bwd(du, h, g, scale, dh_in, B):
    N, D = h.shape
    tm, per_ex = _row_tiles(N, B)

    def body(du_ref, h_ref, g_ref, sc_ref, dhin_ref, dh_ref, dg_ref, dsc_ref, dsh_ref):
        i = pl.program_id(0)
        x = h_ref[...]
        gv = g_ref[...]
        d_u = du_ref[...].astype(F32)
        rstd = lax.rsqrt(jnp.mean(x * x, axis=-1, keepdims=True) + EPS)
        xn = x * rstd
        dyg = d_u * (1.0 + sc_ref[...])
        dxn = dyg * gv
        dh_ref[...] = dhin_ref[...] + rstd * (dxn - xn * jnp.mean(dxn * xn, axis=-1, keepdims=True))
        dsh_t = jnp.sum(d_u, axis=0, keepdims=True)
        dsc_t = jnp.sum(d_u * (xn * gv), axis=0, keepdims=True)
        dg_t = jnp.sum(dyg * xn, axis=0, keepdims=True)

        @pl.when(i % per_ex == 0)
        def _():
            dsc_ref[...] = dsc_t
            dsh_ref[...] = dsh_t

        @pl.when(i % per_ex != 0)
        def _():
            dsc_ref[...] += dsc_t
            dsh_ref[...] += dsh_t

        @pl.when(i == 0)
        def _():
            dg_ref[...] = dg_t

        @pl.when(i != 0)
        def _():
            dg_ref[...] += dg_t

    tok = pl.BlockSpec((tm, D), lambda i: (i, 0))
    vec = pl.BlockSpec((1, D), lambda i: (0, 0))
    ex = pl.BlockSpec((None, 1, D), lambda i: (i // per_ex, 0, 0))
    return pl.pallas_call(
        body, name="normmod_bwd", grid=(N // tm,), in_specs=[tok, tok, vec, ex, tok], out_specs=[tok, vec, ex, ex],
        out_shape=[jax.ShapeDtypeStruct((N, D), F32), jax.ShapeDtypeStruct((1, D), F32),
                   jax.ShapeDtypeStruct((B, 1, D), F32), jax.ShapeDtypeStruct((B, 1, D), F32)],
        compiler_params=_params("arbitrary"),
    )(du, h, g, scale, dh_in)


def _residual(h, gate, y, B):
    N, D = h.shape
    tm, per_ex = _row_tiles(N, B)

    def body(h_ref, gt_ref, y_ref, o_ref):
        o_ref[...] = h_ref[...] + gt_ref[...] * y_ref[...]

    tok = pl.BlockSpec((tm, D), lambda i: (i, 0))
    ex = pl.BlockSpec((None, 1, D), lambda i: (i // per_ex, 0, 0))
    return pl.pallas_call(
        body, name="residual_fwd", grid=(N // tm,), in_specs=[tok, ex, tok], out_specs=tok,
        out_shape=jax.ShapeDtypeStruct((N, D), F32), compiler_params=_params("parallel"),
    )(h, gate, y)


def _residual_bwd(dh, gate, y, B):
    N, D = dh.shape
    tm, per_ex = _row_tiles(N, B)

    def body(dh_ref, gt_ref, y_ref, dy_ref, dgt_ref):
        i = pl.program_id(0)
        d = dh_ref[...]
        dy_ref[...] = (gt_ref[...] * d).astype(dy_ref.dtype)
        t = jnp.sum(d * y_ref[...], axis=0, keepdims=True)

        @pl.when(i % per_ex == 0)
        def _():
            dgt_ref[...] = t

        @pl.when(i % per_ex != 0)
        def _():
            dgt_ref[...] += t

    tok = pl.BlockSpec((tm, D), lambda i: (i, 0))
    ex = pl.BlockSpec((None, 1, D), lambda i: (i // per_ex, 0, 0))
    return pl.pallas_call(
        body, name="residual_bwd", grid=(N // tm,), in_specs=[tok, ex, tok], out_specs=[tok, ex],
        out_shape=[jax.ShapeDtypeStruct((N, D), BF16), jax.ShapeDtypeStruct((B, 1, D), F32)],
        compiler_params=_params("arbitrary"),
    )(dh, gate, y)


def _glu(zz):
    N, D2 = zz.shape
    D = D2 // 2
    tm = _div(N, 256, SUBLANES)

    def body(v_ref, g_ref, o_ref):
        o_ref[...] = v_ref[...] * jax.nn.sigmoid(g_ref[...])

    return pl.pallas_call(
        body, name="glu_fwd", grid=(N // tm,),
        in_specs=[pl.BlockSpec((tm, D), lambda i: (i, 0)), pl.BlockSpec((tm, D), lambda i: (i, 1))],
        out_specs=pl.BlockSpec((tm, D), lambda i: (i, 0)), out_shape=jax.ShapeDtypeStruct((N, D), F32),
        compiler_params=_params("parallel"),
    )(zz, zz)


def _glu_bwd(dy, zz):
    N, D2 = zz.shape
    D = D2 // 2
    tm = _div(N, 256, SUBLANES)

    def body(dy_ref, v_ref, g_ref, o_ref):
        d = dy_ref[...].astype(F32)
        s = jax.nn.sigmoid(g_ref[...])
        o_ref[...] = jnp.concatenate([d * s, d * v_ref[...] * s * (1.0 - s)], axis=1).astype(o_ref.dtype)

    return pl.pallas_call(
        body, name="glu_bwd", grid=(N // tm,),
        in_specs=[pl.BlockSpec((tm, D), lambda i: (i, 0)), pl.BlockSpec((tm, D), lambda i: (i, 0)),
                  pl.BlockSpec((tm, D), lambda i: (i, 1))],
        out_specs=pl.BlockSpec((tm, D2), lambda i: (i, 0)), out_shape=jax.ShapeDtypeStruct((N, D2), BF16),
        compiler_params=_params("parallel"),
    )(dy, zz, zz)


def _loss_head(h, g, target):
    N, D = h.shape
    tm = _div(N, 256, SUBLANES)

    def body(h_ref, g_ref, t_ref, loss_ref, dh_ref, dg_ref):
        i = pl.program_id(0)
        x = h_ref[...]
        gv = g_ref[...]
        rstd = lax.rsqrt(jnp.mean(x * x, axis=-1, keepdims=True) + EPS)
        xn = x * rstd
        err = xn * gv - t_ref[...]
        part = 0.5 * jnp.sum(jnp.sum(err * err, axis=-1, keepdims=True) / D, axis=0, keepdims=True)
        dy = err / D
        dxn = dy * gv
        dh_ref[...] = rstd * (dxn - xn * jnp.mean(dxn * xn, axis=-1, keepdims=True))
        dg_t = jnp.sum(dy * xn, axis=0, keepdims=True)
        part = jnp.broadcast_to(part, loss_ref.shape)

        @pl.when(i == 0)
        def _():
            loss_ref[...] = part
            dg_ref[...] = dg_t

        @pl.when(i != 0)
        def _():
            loss_ref[...] += part
            dg_ref[...] += dg_t

    tok = pl.BlockSpec((tm, D), lambda i: (i, 0))
    vec = pl.BlockSpec((1, D), lambda i: (0, 0))
    return pl.pallas_call(
        body, name="loss_head", grid=(N // tm,), in_specs=[tok, vec, tok],
        out_specs=[pl.BlockSpec((SUBLANES, LANES), lambda i: (0, 0)), tok, vec],
        out_shape=[jax.ShapeDtypeStruct((SUBLANES, LANES), F32), jax.ShapeDtypeStruct((N, D), F32),
                   jax.ShapeDtypeStruct((1, D), F32)],
        compiler_params=_params("arbitrary"),
    )(h, g, target)


def _swap_halves(x):
    half = x.shape[-1] // 2
    return jnp.concatenate([x[:, half:], x[:, :half]], axis=1)


def _gelu(y):
    return jax.nn.gelu(y)


def _gelu_grad(y):
    c0 = math.sqrt(2.0 / math.pi)
    inner = c0 * (y + 0.044715 * y * y * y)
    t = jnp.tanh(inner)
    return 0.5 * (1.0 + t) + 0.5 * y * (1.0 - t * t) * c0 * (1.0 + 3.0 * 0.044715 * y * y)


def _s5_discretize(lam_re, lam_im, log_dt, b_re, b_im, c_re, c_im):
    G = lam_re.shape[0]
    nch = G // CHUNK_GROUPS
    dt = jnp.exp(log_dt)[:, None]
    er = jnp.exp(lam_re * dt)
    a_re = er * jnp.cos(lam_im * dt)
    a_im = er * jnp.sin(lam_im * dt)
    den = lam_re * lam_re + lam_im * lam_im
    n_re, n_im = a_re - 1.0, a_im
    f_re = (n_re * lam_re + n_im * lam_im) / den
    f_im = (n_im * lam_re - n_re * lam_im) / den
    bb_re = f_re[..., None] * b_re - f_im[..., None] * b_im
    bb_im = f_re[..., None] * b_im + f_im[..., None] * b_re
    eye = jnp.eye(CHUNK_GROUPS, dtype=F32)

    def pack_b(bb):
        bb = bb.reshape(nch, CHUNK_GROUPS, SSM_STATE, SSM_GROUP)
        return jnp.einsum("jgpc,gh->jgchp", bb, eye).reshape(nch, LANES, CHUNK_STATE)

    def pack_c(cc):
        cc = cc.reshape(nch, CHUNK_GROUPS, SSM_GROUP, SSM_STATE)
        return jnp.einsum("jgcp,gh->jgphc", cc, eye).reshape(nch, CHUNK_STATE, LANES)

    bd = jnp.concatenate([pack_b(bb_re), pack_b(bb_im)], axis=2)
    cd = jnp.concatenate([pack_c(c_re), pack_c(-c_im)], axis=1)
    return bd, cd, a_re, a_im


def _s5_scan_coefs(lam_re, lam_im, log_dt):
    G = lam_re.shape[0]
    nch = G // CHUNK_GROUPS
    dt = jnp.exp(log_dt)[:, None]

    def power(k):
        er = jnp.exp(k * lam_re * dt)
        re = (er * jnp.cos(k * lam_im * dt)).reshape(nch, 1, CHUNK_STATE)
        im = (er * jnp.sin(k * lam_im * dt)).reshape(nch, 1, CHUNK_STATE)
        return re, im

    pw = {k: power(k) for k in range(1, SUBLANES + 1)}
    row = jnp.arange(SUBLANES, dtype=jnp.int32)[None, :, None]

    def table(reverse):
        sign = -1.0 if reverse else 1.0
        tabs = []
        for s in (1, 2, 4):
            re, im = pw[s]
            mask = (row < SUBLANES - s) if reverse else (row >= s)
            ar = jnp.where(mask, jnp.concatenate([re, re], axis=2), 0.0)
            ai = jnp.where(mask, jnp.concatenate([-sign * im, sign * im], axis=2), 0.0)
            tabs += [ar, ai]
        pr_rows, pi_rows = [], []
        for j in range(SUBLANES):
            re, im = pw[SUBLANES - j] if reverse else pw[j + 1]
            pr_rows.append(jnp.concatenate([re, re], axis=2))
            pi_rows.append(jnp.concatenate([-sign * im, sign * im], axis=2))
        tabs += [jnp.concatenate(pr_rows, axis=1), jnp.concatenate(pi_rows, axis=1)]
        return jnp.stack([jnp.broadcast_to(t, (nch, SUBLANES, 2 * CHUNK_STATE)) for t in tabs], axis=1)

    return table(False), table(True)


def _scan_rows(x_ref, coef_ref, carry_ref, n_groups, reverse, prev_ref=None):
    row = lax.broadcasted_iota(jnp.int32, (SUBLANES, x_ref.shape[-1]), 0)

    def group(t, _):
        i = (n_groups - 1 - t) if reverse else t
        rows = pl.ds(pl.multiple_of(i * SUBLANES, SUBLANES), SUBLANES)
        x = x_ref[rows, :]
        for si, s in enumerate((1, 2, 4)):
            xs = pltpu.roll(x, (SUBLANES - s) if reverse else s, 0)
            x = x + coef_ref[2 * si] * xs + coef_ref[2 * si + 1] * _swap_halves(xs)
        c = carry_ref[...]
        x = x + coef_ref[6] * c + coef_ref[7] * _swap_halves(c)
        if prev_ref is not None:
            prev_ref[rows, :] = jnp.where(row == 0, c, pltpu.roll(x, 1, 0))
        edge = x[0:1, :] if reverse else x[SUBLANES - 1:SUBLANES, :]
        carry_ref[...] = jnp.broadcast_to(edge, carry_ref.shape)
        x_ref[rows, :] = x
        return 0

    lax.fori_loop(0, n_groups, group, 0)


def _s5_fwd(u, bd, cd, coef_f, d_skip, B):
    N, D = u.shape
    S = N // B
    nch = D // LANES
    W = 2 * CHUNK_STATE
    tm = _div(S, 256, SUBLANES)
    nt = S // tm

    def body(u_ref, bd_ref, cd_ref, coef_ref, d_ref, z_ref, cin_ref, x_s, carry):
        t = pl.program_id(2)

        @pl.when(t == 0)
        def _():
            carry[...] = jnp.zeros_like(carry)

        cin_ref[...] = carry[...]
        uv = u_ref[...]
        x_s[...] = jnp.dot(uv, bd_ref[...], preferred_element_type=F32)
        _scan_rows(x_s, coef_ref, carry, tm // SUBLANES, False)
        y = jnp.dot(x_s[...].astype(BF16), cd_ref[...], preferred_element_type=F32) + d_ref[...] * uv.astype(F32)
        z_ref[...] = _gelu(y).astype(z_ref.dtype)

    return pl.pallas_call(
        body, name="s5_fwd", grid=(nch, B, nt),
        in_specs=[pl.BlockSpec((tm, LANES), lambda j, b, t: (b * nt + t, j)),
                  pl.BlockSpec((None, LANES, W), lambda j, b, t: (j, 0, 0)),
                  pl.BlockSpec((None, W, LANES), lambda j, b, t: (j, 0, 0)),
                  pl.BlockSpec((None, 8, SUBLANES, W), lambda j, b, t: (j, 0, 0, 0)),
                  pl.BlockSpec((1, LANES), lambda j, b, t: (0, j))],
        out_specs=[pl.BlockSpec((tm, LANES), lambda j, b, t: (b * nt + t, j)),
                   pl.BlockSpec((None, None, SUBLANES, W), lambda j, b, t: (j, b * nt + t, 0, 0))],
        out_shape=[jax.ShapeDtypeStruct((N, D), BF16), jax.ShapeDtypeStruct((nch, B * nt, SUBLANES, W), F32)],
        scratch_shapes=[pltpu.VMEM((tm, W), F32), pltpu.VMEM((SUBLANES, W), F32)],
        compiler_params=_params("parallel", "arbitrary", "arbitrary"),
    )(u, bd, cd, coef_f, d_skip)


def _s5_bwd(u, dz, bd, cd, coef_f, coef_b, d_skip, carries, B):
    N, D = u.shape
    S = N // B
    nch = D // LANES
    W = 2 * CHUNK_STATE
    tm = carries.shape[1] // B
    tm = S // tm
    nt = S // tm
    ng = tm // SUBLANES

    def body(u_ref, dz_ref, bd_ref, cd_ref, cf_ref, cb_ref, d_ref, cin_ref,
             du_ref, dbd_ref, dcd_ref, da_ref, dd_ref, x_s, xp_s, l_s, carry, lcarry):
        b = pl.program_id(1)
        t = pl.program_id(2)

        @pl.when((b == 0) & (t == 0))
        def _():
            dbd_ref[...] = jnp.zeros_like(dbd_ref)
            dcd_ref[...] = jnp.zeros_like(dcd_ref)
            da_ref[...] = jnp.zeros_like(da_ref)
            dd_ref[...] = jnp.zeros_like(dd_ref)

        @pl.when(t == 0)
        def _():
            lcarry[...] = jnp.zeros_like(lcarry)

        uv = u_ref[...]
        uf = uv.astype(F32)
        carry[...] = cin_ref[...]
        x_s[...] = jnp.dot(uv, bd_ref[...], preferred_element_type=F32)
        _scan_rows(x_s, cf_ref, carry, ng, False, prev_ref=xp_s)
        xb = x_s[...].astype(BF16)
        y = jnp.dot(xb, cd_ref[...], preferred_element_type=F32) + d_ref[...] * uf
        dy = dz_ref[...] * _gelu_grad(y)
        dd_ref[...] += jnp.sum(dy * uf, axis=0, keepdims=True)
        dyb = dy.astype(BF16)
        dcd_ref[...] += lax.dot_general(xb, dyb, (((0,), (0,)), ((), ())), preferred_element_type=F32)
        l_s[...] = lax.dot_general(dyb, cd_ref[...], (((1,), (1,)), ((), ())), preferred_element_type=F32)
        _scan_rows(l_s, cb_ref, lcarry, ng, True)
        lam = l_s[...]
        lb = lam.astype(BF16)
        dbd_ref[...] += lax.dot_general(uv, lb, (((0,), (0,)), ((), ())), preferred_element_type=F32)
        du_ref[...] = lax.dot_general(lb, bd_ref[...], (((1,), (1,)), ((), ())), preferred_element_type=F32) + d_ref[...] * dy
        xp = xp_s[...]
        da_ref[0:1, :] += jnp.sum(lam * xp, axis=0, keepdims=True)
        da_ref[1:2, :] += jnp.sum(lam * _swap_halves(xp), axis=0, keepdims=True)

    tile = lambda j, b, t: (b * nt + (nt - 1 - t), j)
    return pl.pallas_call(
        body, name="s5_bwd", grid=(nch, B, nt),
        in_specs=[pl.BlockSpec((tm, LANES), tile), pl.BlockSpec((tm, LANES), tile),
                  pl.BlockSpec((None, LANES, W), lambda j, b, t: (j, 0, 0)),
                  pl.BlockSpec((None, W, LANES), lambda j, b, t: (j, 0, 0)),
                  pl.BlockSpec((None, 8, SUBLANES, W), lambda j, b, t: (j, 0, 0, 0)),
                  pl.BlockSpec((None, 8, SUBLANES, W), lambda j, b, t: (j, 0, 0, 0)),
                  pl.BlockSpec((1, LANES), lambda j, b, t: (0, j)),
                  pl.BlockSpec((None, None, SUBLANES, W), lambda j, b, t: (j, b * nt + (nt - 1 - t), 0, 0))],
        out_specs=[pl.BlockSpec((tm, LANES), tile),
                   pl.BlockSpec((None, LANES, W), lambda j, b, t: (j, 0, 0)),
                   pl.BlockSpec((None, W, LANES), lambda j, b, t: (j, 0, 0)),
                   pl.BlockSpec((None, 2, W), lambda j, b, t: (j, 0, 0)),
                   pl.BlockSpec((1, LANES), lambda j, b, t: (0, j))],
        out_shape=[jax.ShapeDtypeStruct((N, D), F32), jax.ShapeDtypeStruct((nch, LANES, W), F32),
                   jax.ShapeDtypeStruct((nch, W, LANES), F32), jax.ShapeDtypeStruct((nch, 2, W), F32),
                   jax.ShapeDtypeStruct((1, D), F32)],
        scratch_shapes=[pltpu.VMEM((tm, W), F32), pltpu.VMEM((tm, W), F32), pltpu.VMEM((tm, W), F32),
                        pltpu.VMEM((SUBLANES, W), F32), pltpu.VMEM((SUBLANES, W), F32)],
        compiler_params=_params("parallel", "arbitrary", "arbitrary"),
    )(u, dz, bd, cd, coef_f, coef_b, d_skip, carries)


def _attn_masks():
    qi = lax.broadcasted_iota(jnp.int32, (ATTN_BLOCK, ATTN_BLOCK), 0)
    kj = lax.broadcasted_iota(jnp.int32, (ATTN_BLOCK, ATTN_BLOCK), 1)
    return kj <= qi, kj >= qi


def _head_bcast(cols):
    return jnp.concatenate([jnp.broadcast_to(c, (ATTN_BLOCK, HEAD_DIM)) for c in cols], axis=1)


def _block_rows(tb, dil, nb):
    r = tb // nb
    n = tb % nb
    start = r + dil * ATTN_BLOCK * n
    startp = jnp.where(n > 0, start - dil * ATTN_BLOCK, start)
    return n, start, startp


def _attn_fwd(q, k, v, B):
    _, S, D3 = q.shape
    D = D3 // 3
    HP = D // LANES
    scale = HEAD_DIM ** -0.5
    n_blocks = S // ATTN_BLOCK
    heads = LANES // HEAD_DIM

    def branch(gi, dil, q_ref, k_ref, v_ref, acc, m_s, l_s):
        nb = (S // dil) // ATTN_BLOCK
        cur_ok, prev_ok = _attn_masks()

        def blk(tb, _):
            n, start, startp = _block_rows(tb, dil, nb)
            rows = pl.ds(start, ATTN_BLOCK, stride=dil)
            rowsp = pl.ds(startp, ATTN_BLOCK, stride=dil)
            qb = q_ref[rows, :] * scale
            kc, kp, vc, vp = k_ref[rows, :], k_ref[rowsp, :], v_ref[rows, :], v_ref[rowsp, :]
            pmask = prev_ok & (n > 0)
            ms, ls, accs = [], [], []
            for h in range(heads):
                sl = slice(h * HEAD_DIM, (h + 1) * HEAD_DIM)
                qh = qb[:, sl].astype(BF16)
                nt_dims = (((1,), (1,)), ((), ()))
                s_c = lax.dot_general(qh, kc[:, sl].astype(BF16), nt_dims, preferred_element_type=F32)
                s_p = lax.dot_general(qh, kp[:, sl].astype(BF16), nt_dims, preferred_element_type=F32)
                s_c = jnp.where(cur_ok, s_c, NEG)
                s_p = jnp.where(pmask, s_p, NEG)
                mh = jnp.maximum(jnp.max(s_c, axis=-1, keepdims=True), jnp.max(s_p, axis=-1, keepdims=True))
                p_c = jnp.exp(s_c - mh)
                p_p = jnp.exp(s_p - mh)
                ls.append(jnp.sum(p_c, axis=-1, keepdims=True) + jnp.sum(p_p, axis=-1, keepdims=True))
                ms.append(mh)
                accs.append(jnp.dot(p_c.astype(BF16), vc[:, sl].astype(BF16), preferred_element_type=F32)
                            + jnp.dot(p_p.astype(BF16), vp[:, sl].astype(BF16), preferred_element_type=F32))
            m_b, l_b, a_b = _head_bcast(ms), _head_bcast(ls), jnp.concatenate(accs, axis=1)
            if gi > 0:
                m_o = m_s[rows, :]
                m_n = jnp.maximum(m_o, m_b)
                w_o = jnp.exp(m_o - m_n)
                w_b = jnp.exp(m_b - m_n)
                l_b = w_o * l_s[rows, :] + w_b * l_b
                a_b = w_o * acc[rows, :] + w_b * a_b
                m_b = m_n
            m_s[rows, :] = m_b
            l_s[rows, :] = l_b
            acc[rows, :] = a_b
            return 0

        lax.fori_loop(0, n_blocks, blk, 0)

    def body(q_ref, k_ref, v_ref, o_ref, lse_ref, acc, m_s, l_s):
        g = pl.program_id(2)
        for gi, dil in enumerate(DILATIONS):
            pl.when(g == gi)(functools.partial(branch, gi, dil, q_ref, k_ref, v_ref, acc, m_s, l_s))

        @pl.when(g == len(DILATIONS) - 1)
        def _():
            def fin(i, _):
                rows = pl.ds(pl.multiple_of(i * ATTN_BLOCK, ATTN_BLOCK), ATTN_BLOCK)
                den = l_s[rows, :]
                o_ref[rows, :] = acc[rows, :] / den
                lse_ref[rows, :] = m_s[rows, :] + jnp.log(den)
                return 0

            lax.fori_loop(0, n_blocks, fin, 0)

    br = pl.BlockSpec((None, S, LANES), lambda b, hp, g: (b, 0, g * HP + hp))
    hd = pl.BlockSpec((None, S, LANES), lambda b, hp, g: (b, 0, hp))
    return pl.pallas_call(
        body, name="attn_fwd", grid=(B, HP, len(DILATIONS)), in_specs=[br, br, br], out_specs=[hd, hd],
        out_shape=[jax.ShapeDtypeStruct((B, S, D), F32), jax.ShapeDtypeStruct((B, S, D), F32)],
        scratch_shapes=[pltpu.VMEM((S, LANES), F32)] * 3,
        compiler_params=_params("parallel", "parallel", "arbitrary"),
    )(q, k, v)


def _attn_bwd(q, k, v, o, lse, do, dk_prev, dv_prev, B):
    _, S, D3 = q.shape
    D = D3 // 3
    HP = D // LANES
    scale = HEAD_DIM ** -0.5
    n_blocks = S // ATTN_BLOCK
    heads = LANES // HEAD_DIM
    has_prev = dk_prev is not None
    nt_dims = (((1,), (1,)), ((), ()))
    tn_dims = (((0,), (0,)), ((), ()))

    def branch(dil, q_ref, k_ref, v_ref, lse_ref, do_ref, dq_ref, dk_ref, dv_ref, delta):
        nb = (S // dil) // ATTN_BLOCK
        cur_ok, prev_ok = _attn_masks()

        def blk(tb, _):
            n, start, startp = _block_rows(tb, dil, nb)
            rows = pl.ds(start, ATTN_BLOCK, stride=dil)
            rowsp = pl.ds(startp, ATTN_BLOCK, stride=dil)
            qb = q_ref[rows, :] * scale
            dob, lb, db = do_ref[rows, :], lse_ref[rows, :], delta[rows, :]
            kc, kp, vc, vp = k_ref[rows, :], k_ref[rowsp, :], v_ref[rows, :], v_ref[rowsp, :]
            pmask = prev_ok & (n > 0)
            dqs, dkcs, dkps, dvcs, dvps = [], [], [], [], []
            for h in range(heads):
                sl = slice(h * HEAD_DIM, (h + 1) * HEAD_DIM)
                qh = qb[:, sl].astype(BF16)
                doh = dob[:, sl].astype(BF16)
                kch, kph = kc[:, sl].astype(BF16), kp[:, sl].astype(BF16)
                vch, vph = vc[:, sl].astype(BF16), vp[:, sl].astype(BF16)
                lh = lb[:, h * HEAD_DIM:h * HEAD_DIM + 1]
                dh = db[:, h * HEAD_DIM:h * HEAD_DIM + 1]
                s_c = lax.dot_general(qh, kch, nt_dims, preferred_element_type=F32)
                s_p = lax.dot_general(qh, kph, nt_dims, preferred_element_type=F32)
                p_c = jnp.where(cur_ok, jnp.exp(s_c - lh), 0.0)
                p_p = jnp.where(pmask, jnp.exp(s_p - lh), 0.0)
                dvcs.append(lax.dot_general(p_c.astype(BF16), doh, tn_dims, preferred_element_type=F32))
                dvps.append(lax.dot_general(p_p.astype(BF16), doh, tn_dims, preferred_element_type=F32))
                dp_c = lax.dot_general(doh, vch, nt_dims, preferred_element_type=F32)
                dp_p = lax.dot_general(doh, vph, nt_dims, preferred_element_type=F32)
                ds_c = (p_c * (dp_c - dh)).astype(BF16)
                ds_p = (p_p * (dp_p - dh)).astype(BF16)
                dqs.append((jnp.dot(ds_c, kch, preferred_element_type=F32)
                            + jnp.dot(ds_p, kph, preferred_element_type=F32)) * scale)
                dkcs.append(lax.dot_general(ds_c, qh, tn_dims, preferred_element_type=F32))
                dkps.append(lax.dot_general(ds_p, qh, tn_dims, preferred_element_type=F32))
            dq_ref[rows, :] = jnp.concatenate(dqs, axis=1)
            dk_ref[rows, :] += jnp.concatenate(dkcs, axis=1)
            dk_ref[rowsp, :] += jnp.concatenate(dkps, axis=1)
            dv_ref[rows, :] += jnp.concatenate(dvcs, axis=1)
            dv_ref[rowsp, :] += jnp.concatenate(dvps, axis=1)
            return 0

        lax.fori_loop(0, n_blocks, blk, 0)

    def body(*refs):
        q_ref, k_ref, v_ref, o_ref, lse_ref, do_ref = refs[:6]
        n_in = 8 if has_prev else 6
        dq_ref, dk_ref, dv_ref, delta = refs[n_in:n_in + 4]
        g = pl.program_id(2)

        @pl.when(g == 0)
        def _():
            def dl(i, _):
                rows = pl.ds(pl.multiple_of(i * ATTN_BLOCK, ATTN_BLOCK), ATTN_BLOCK)
                prod = do_ref[rows, :] * o_ref[rows, :]
                delta[rows, :] = _head_bcast([jnp.sum(prod[:, h * HEAD_DIM:(h + 1) * HEAD_DIM], axis=-1, keepdims=True)
                                              for h in range(heads)])
                return 0

            lax.fori_loop(0, n_blocks, dl, 0)

        if has_prev:
            dk_ref[...] = refs[6][...]
            dv_ref[...] = refs[7][...]
        else:
            dk_ref[...] = jnp.zeros_like(dk_ref)
            dv_ref[...] = jnp.zeros_like(dv_ref)
        for gi, dil in enumerate(DILATIONS):
            pl.when(g == gi)(functools.partial(branch, dil, q_ref, k_ref, v_ref, lse_ref, do_ref, dq_ref, dk_ref, dv_ref, delta))

    br = pl.BlockSpec((None, S, LANES), lambda b, hp, g: (b, 0, g * HP + hp))
    hd = pl.BlockSpec((None, S, LANES), lambda b, hp, g: (b, 0, hp))
    ins = [q, k, v, o, lse, do] + ([dk_prev, dv_prev] if has_prev else [])
    return pl.pallas_call(
        body, name="attn_bwd", grid=(B, HP, len(DILATIONS)),
        in_specs=[br, br, br, hd, hd, hd] + ([br, br] if has_prev else []), out_specs=[br, br, br],
        out_shape=[jax.ShapeDtypeStruct(q.shape, F32)] * 3, scratch_shapes=[pltpu.VMEM((S, LANES), F32)],
        compiler_params=_params("parallel", "parallel", "arbitrary"),
    )(*ins)


def _adamw(w, grads, m, v):
    R, C = w.shape
    tr = _div(R, 256, SUBLANES)
    ng = len(grads)
    c1 = 1.0 - ADAM_B1 ** ADAM_STEP
    c2 = 1.0 - ADAM_B2 ** ADAM_STEP

    def body(*refs):
        w_ref, m_ref, v_ref = refs[0], refs[1 + ng], refs[2 + ng]
        d_ref, mo_ref, vo_ref = refs[3 + ng:6 + ng]
        g = refs[1][...]
        if ng == 2:
            g = g + refs[2][...]
            refs[6 + ng][...] = g
        mn = ADAM_B1 * m_ref[...] + (1.0 - ADAM_B1) * g
        vn = ADAM_B2 * v_ref[...] + (1.0 - ADAM_B2) * (g * g)
        d_ref[...] = -ADAM_LR * ((mn / c1) / (jnp.sqrt(vn / c2) + ADAM_EPS) + ADAM_WD * w_ref[...])
        mo_ref[...] = mn
        vo_ref[...] = vn

    blk = pl.BlockSpec((tr, C), lambda i: (i, 0))
    n_out = 3 + (ng == 2)
    outs = pl.pallas_call(
        body, name="adamw", grid=(R // tr,), in_specs=[blk] * (3 + ng), out_specs=[blk] * n_out,
        out_shape=[jax.ShapeDtypeStruct((R, C), F32)] * n_out, compiler_params=_params("parallel"),
    )(w, *grads, m, v)
    return (outs[3] if ng == 2 else grads[0],) + tuple(outs[:3])


def _sum_shards(recv):
    n, R, C = recv.shape
    tr = _div(R, 256, SUBLANES if recv.dtype == F32 else 2 * SUBLANES)

    def body(r_ref, o_ref):
        s = r_ref[0].astype(F32)
        for i in range(1, n):
            s = s + r_ref[i].astype(F32)
        o_ref[...] = s

    return pl.pallas_call(
        body, name="sum_shards", grid=(R // tr,), in_specs=[pl.BlockSpec((n, tr, C), lambda i: (0, i, 0))],
        out_specs=pl.BlockSpec((tr, C), lambda i: (i, 0)), out_shape=jax.ShapeDtypeStruct((R, C), F32),
        compiler_params=_params("parallel"),
    )(recv)


N_DEV = 8
N_CHIPS = 4


def _coords():
    return lax.axis_index("x"), lax.axis_index("y"), lax.axis_index("c")


def _all_gather_small(x):
    m_per, n = x.shape

    def body(x_ref, out_ref, send_sems, recv_sems, local_sem):
        cx, cy, cc = _coords()
        me, sibling = (cx, cy, cc), (cx, cy, 1 - cc)
        chips = [(1 - cx, cy), (cx, 1 - cy), (1 - cx, 1 - cy)]

        def rows(px, py, pc):
            return out_ref.at[pl.ds((4 * px + 2 * py + pc) * m_per, m_per), :]

        def copy(k, block, to, src=None):
            return pltpu.make_async_remote_copy(
                src_ref=rows(*block) if src is None else src, dst_ref=rows(*block), send_sem=send_sems.at[k],
                recv_sem=recv_sems.at[k], device_id=to, device_id_type=MESH)

        mine = pltpu.make_async_copy(x_ref, rows(*me), local_sem)
        mine.start()
        first = [copy(0, me, sibling, src=x_ref)]
        first += [copy(1 + j, me, (*chip, cc), src=x_ref) for j, chip in enumerate(chips)]
        for cp in first:
            cp.start()
        passed = [copy(4 + j, (*chip, cc), sibling) for j, chip in enumerate(chips)]
        for j, chip in enumerate(chips):
            copy(1 + j, (*chip, cc), me).wait_recv()
            passed[j].start()
        copy(0, sibling, me).wait_recv()
        for j, chip in enumerate(chips):
            copy(4 + j, (*chip, 1 - cc), me).wait_recv()
        for cp in first + passed:
            cp.wait_send()
        mine.wait()

    return pl.pallas_call(
        body, name="all_gather_small", out_shape=jax.ShapeDtypeStruct((N_DEV * m_per, n), x.dtype),
        in_specs=[pl.BlockSpec(memory_space=pltpu.VMEM)], out_specs=pl.BlockSpec(memory_space=pltpu.VMEM),
        scratch_shapes=[pltpu.SemaphoreType.DMA((7,)), pltpu.SemaphoreType.DMA((7,)), pltpu.SemaphoreType.DMA],
        compiler_params=pltpu.CompilerParams(vmem_limit_bytes=VMEM_LIMIT),
    )(x)


def _hbm_exchange(name, ins, out_shapes, plan):
    n_in = len(ins)
    probe = plan([None] * n_in, [None] * len(out_shapes), None, count_only=True)
    n_remote, n_local = probe

    def body(*refs):
        in_refs, out_refs = refs[:n_in], refs[n_in:n_in + len(out_shapes)]
        send_sems, recv_sems, local_sems = refs[n_in + len(out_shapes):]
        remote, local = plan(in_refs, out_refs, _coords())
        locals_ = [pltpu.make_async_copy(s, d, local_sems.at[i]) for i, (s, d) in enumerate(local)]
        for cp in locals_:
            cp.start()
        remotes = [pltpu.make_async_remote_copy(src_ref=s, dst_ref=d, send_sem=send_sems.at[i], recv_sem=recv_sems.at[i],
                                                device_id=peer, device_id_type=MESH)
                   for i, (s, d, peer) in enumerate(remote)]
        for cp in remotes:
            cp.start()
        for cp in remotes:
            cp.wait_send()
            cp.wait_recv()
        for cp in locals_:
            cp.wait()

    anyspec = pl.BlockSpec(memory_space=pl.ANY)
    return pl.pallas_call(
        body, name=name, out_shape=out_shapes, in_specs=[anyspec] * n_in, out_specs=[anyspec] * len(out_shapes),
        scratch_shapes=[pltpu.SemaphoreType.DMA((n_remote,)), pltpu.SemaphoreType.DMA((n_remote,)),
                        pltpu.SemaphoreType.DMA((max(n_local, 1),))],
    )(*ins)


def _other_chips(cx, cy):
    return [(1 - cx, cy), (cx, 1 - cy), (1 - cx, 1 - cy)]


def _gather_weights(shards):
    outs = [jax.ShapeDtypeStruct((N_CHIPS,) + s.shape, s.dtype) for s in shards]

    def plan(in_refs, out_refs, me, count_only=False):
        if count_only:
            return 3 * len(in_refs), len(in_refs)
        cx, cy, cc = me
        mine = 2 * cx + cy
        remote = [(s, o.at[mine], (px, py, cc)) for s, o in zip(in_refs, out_refs) for px, py in _other_chips(cx, cy)]
        local = [(s, o.at[mine]) for s, o in zip(in_refs, out_refs)]
        return remote, local

    return _hbm_exchange("gather_weights", shards, outs, plan)


def _scatter_grads(grads):
    outs = [jax.ShapeDtypeStruct(g.shape, g.dtype) for g in grads]

    def plan(in_refs, out_refs, me, count_only=False):
        if count_only:
            return 3 * len(in_refs), len(in_refs)
        cx, cy, cc = me
        mine = 2 * cx + cy
        remote = [(g.at[2 * px + py], o.at[mine], (px, py, cc))
                  for g, o in zip(in_refs, out_refs) for px, py in _other_chips(cx, cy)]
        local = [(g.at[mine], o.at[mine]) for g, o in zip(in_refs, out_refs)]
        return remote, local

    return _hbm_exchange("scatter_grads", grads, outs, plan)


def _swap_with_sibling(sums):
    outs = [jax.ShapeDtypeStruct(s.shape, s.dtype) for s in sums]

    def plan(in_refs, out_refs, me, count_only=False):
        if count_only:
            return len(in_refs), 0
        cx, cy, cc = me
        return [(s, o, (cx, cy, 1 - cc)) for s, o in zip(in_refs, out_refs)], []

    return _hbm_exchange("swap_with_sibling", sums, outs, plan)


def _pack(arrs, width):
    parts, layout, row = [], [], 0
    for a in arrs:
        flat = a.reshape(-1).astype(F32)
        rows = -(-flat.shape[0] // width)
        parts.append(jnp.pad(flat, (0, rows * width - flat.shape[0])).reshape(rows, width))
        layout.append((row, rows, a.shape))
        row += rows
    pad = -row % SUBLANES
    if pad:
        parts.append(jnp.zeros((pad, width), F32))
    return jnp.concatenate(parts, axis=0), layout, row + pad


def _unpack(buf, layout, idx):
    row, rows, shape = layout[idx]
    size = math.prod(shape)
    return buf[row:row + rows].reshape(-1)[:size].reshape(shape)


def kernel(x, c, ln_g, ada_w, ada_b, ssm_lam_re, ssm_lam_im, ssm_log_dt, ssm_b_re, ssm_b_im, ssm_c_re, ssm_c_im, ssm_d, ssm_w_glu, kv_g, kv_ada_w, kv_ada_b, w_kv, attn_w_q, attn_w_o, mlp_w1, mlp_w2, final_g, loss_target, m_ln_g, m_ada_w, m_ada_b, m_ssm_lam_re, m_ssm_lam_im, m_ssm_log_dt, m_ssm_b_re, m_ssm_b_im, m_ssm_c_re, m_ssm_c_im, m_ssm_d, m_ssm_w_glu, m_kv_g, m_kv_ada_w, m_kv_ada_b, m_w_kv, m_attn_w_q, m_attn_w_o, m_mlp_w1, m_mlp_w2, m_final_g, v_ln_g, v_ada_w, v_ada_b, v_ssm_lam_re, v_ssm_lam_im, v_ssm_log_dt, v_ssm_b_re, v_ssm_b_im, v_ssm_c_re, v_ssm_c_im, v_ssm_d, v_ssm_w_glu, v_kv_g, v_kv_ada_w, v_kv_ada_b, v_w_kv, v_attn_w_q, v_attn_w_o, v_mlp_w1, v_mlp_w2, v_final_g):
    B, S, D = x.shape
    N = B * S
    depth = ln_g.shape[0]
    n_a = ssm_w_glu.shape[0]
    n_b = attn_w_q.shape[0]
    FF = mlp_w1.shape[2] * N_CHIPS
    cx, cy, cc = _coords()
    chip = 2 * cx + cy
    dev = 4 * cx + 2 * cy + cc
    n_ex = N_DEV * B
    ada_cols = ada_w.shape[-1]
    kv_cols = kv_ada_w.shape[-1]

    wg_glu, wg_kv, wg_q, wg_o, wg_1, wg_2 = _gather_weights([
        ssm_w_glu.astype(BF16), w_kv.astype(BF16)[None], attn_w_q.astype(BF16), attn_w_o.astype(BF16),
        mlp_w1.astype(BF16), mlp_w2.astype(BF16)])

    c_pack, c_layout, _ = _pack([c], D)
    c_all_buf = _all_gather_small(c_pack)
    c_rows = c_pack.shape[0]
    c_all = jnp.concatenate([_unpack(c_all_buf[d * c_rows:(d + 1) * c_rows], c_layout, 0) for d in range(N_DEV)], axis=0)
    sc_all = jax.nn.silu(c_all).astype(BF16)
    n_mod = depth * 2
    ada_w8 = ada_w.reshape(n_mod, 1, D, ada_cols)
    ada_b_row = ada_b.reshape(1, n_mod * ada_cols)
    mod_local = _mm("ada_fwd", sc_all, ada_w8, mode="nn", M=n_ex, N=n_mod * ada_cols, K=D, b_lay="cs", b_ns=n_mod,
                    epi=_add, extras=[("n", ada_b_row)])
    kv_ada_b_local = lax.dynamic_slice(kv_ada_b.reshape(N_CHIPS, kv_cols), (chip, 0), (1, kv_cols))
    kvmod_local = _mm("ada_fwd", sc_all, _as4(kv_ada_w), mode="nn", M=n_ex, N=kv_cols, K=D, epi=_add,
                      extras=[("n", kv_ada_b_local)])
    mod_pack, mod_layout, mod_rows = _pack([mod_local, kvmod_local, ln_g, ssm_d], D)
    mod_buf = _all_gather_small(mod_pack)

    def from_chip(j, idx):
        d = 2 * j
        return _unpack(mod_buf[d * mod_rows:(d + 1) * mod_rows], mod_layout, idx)

    my_rows = lambda a: lax.dynamic_slice_in_dim(a, dev * B, B, axis=0)
    mods = jnp.concatenate([my_rows(from_chip(j, 0)).reshape(B, n_mod, ada_cols) for j in range(N_CHIPS)], axis=2)
    kvmod = jnp.concatenate([my_rows(from_chip(j, 1)) for j in range(N_CHIPS)], axis=1)
    ln_g_full = jnp.concatenate([from_chip(j, 2) for j in range(N_CHIPS)], axis=2)
    ssm_d_full = jnp.concatenate([from_chip(j, 3) for j in range(N_CHIPS)], axis=1)

    def mod3(l, s):
        mrow = mods[:, l * 2 + s]
        return [mrow[:, i * D:(i + 1) * D].reshape(B, 1, D) for i in range(3)]

    kv_shift, kv_scale = kvmod[:, :D].reshape(B, 1, D), kvmod[:, D:].reshape(B, 1, D)

    s5_tabs = []
    for l in range(n_a):
        prm = (ssm_lam_re[l], ssm_lam_im[l], ssm_log_dt[l], ssm_b_re[l], ssm_b_im[l], ssm_c_re[l], ssm_c_im[l])
        (bd, cd, _, _), disc_vjp = jax.vjp(_s5_discretize, *prm)
        coef_f, coef_b = _s5_scan_coefs(ssm_lam_re[l], ssm_lam_im[l], ssm_log_dt[l])
        s5_tabs.append((bd.astype(BF16), cd.astype(BF16), coef_f, coef_b, disc_vjp))

    h = x.reshape(N, D)
    saved = []
    k_all = v_all = None
    for l in range(depth):
        sv = {}
        shift, scale, gate = mod3(l, 0)
        sv["h0"], sv["scale0"], sv["gate0"] = h, scale, gate
        u = _normmod(h, ln_g_full[l, 0].reshape(1, D), scale, shift, B)
        sv["u0"] = u
        if l < n_a:
            bd, cd, coef_f, coef_b, _ = s5_tabs[l]
            z, carries = _s5_fwd(u, bd, cd, coef_f, ssm_d_full[l].reshape(1, D), B)
            zz = _mm("glu_proj", z, wg_glu, mode="nn", M=N, N=2 * D, K=D, b_lay="cs", b_l=l, b_ns=N_CHIPS)
            y = _glu(zz)
            sv["z"], sv["carries"], sv["zz"] = z, carries, zz
        else:
            j = l - n_a
            q = _mm("q_proj", u, wg_q, mode="nn", M=N, N=3 * D, K=D, b_lay="cs", b_l=j, b_ns=N_CHIPS)
            q3 = q.reshape(B, S, 3 * D)
            o, lse = _attn_fwd(q3, k_all, v_all, B)
            o2 = o.reshape(N, D)
            y = _mm("o_proj", o2, wg_o, mode="nn", M=N, N=D, K=D, b_lay="rs", b_l=j, b_ns=N_CHIPS)
            sv["q"], sv["o"], sv["lse"] = q3, o, lse
        sv["y0"] = y
        h = _residual(h, gate, y, B)
        shift, scale, gate = mod3(l, 1)
        sv["h1"], sv["scale1"], sv["gate1"] = h, scale, gate
        u = _normmod(h, ln_g_full[l, 1].reshape(1, D), scale, shift, B)
        r = _mm("mlp_up", u, wg_1, mode="nn", M=N, N=FF, K=D, b_lay="cs", b_l=l, b_ns=N_CHIPS, out_dtype=BF16, epi=_relu2)
        y = _mm("mlp_down", r, wg_2, mode="nn", M=N, N=D, K=FF, b_lay="rs", b_l=l, b_ns=N_CHIPS)
        sv["u1"], sv["r"], sv["y1"] = u, r, y
        h = _residual(h, gate, y, B)
        saved.append(sv)
        if l == n_a - 1:
            h_kv = h
            u_kv = _normmod(h, kv_g.reshape(1, D), kv_scale, kv_shift, B)
            half = N_CHIPS // 2
            k_all = _mm("kv_proj", u_kv, wg_kv, mode="nn", M=N, N=3 * D, K=D, b_lay="cs", b_s0=0, b_ns=half).reshape(B, S, 3 * D)
            v_all = _mm("kv_proj", u_kv, wg_kv, mode="nn", M=N, N=3 * D, K=D, b_lay="cs", b_s0=half, b_ns=half).reshape(B, S, 3 * D)

    loss_buf, dh, d_final_g = _loss_head(h, final_g.reshape(1, D), loss_target.reshape(N, D))
    loss = lax.psum(loss_buf[0, 0], ("x", "y", "c"))

    def grads_like(wg):
        return jnp.zeros(wg.shape, BF16)

    dg_glu, dg_kv, dg_q, dg_o, dg_1, dg_2 = (grads_like(w) for w in (wg_glu, wg_kv, wg_q, wg_o, wg_1, wg_2))
    d_ln_g = [[None, None] for _ in range(depth)]
    d_mods = [[None, None] for _ in range(depth)]
    d_s5 = [None] * n_a
    dk_acc = dv_acc = None
    half = N_CHIPS // 2

    def tn_grad(name, a, d, into, l, Mr, Nc, lay, s0=0, ns=N_CHIPS):
        return _mm(name, a, _as4(d), mode="tn", M=Mr, N=Nc, K=N, b_lay="cs", out_dtype=BF16, out_lay=lay,
                   out4_shape=into.shape, out_into=into, out_l=l, out_s0=s0, out_ns=ns)

    for l in reversed(range(depth)):
        sv = saved[l]
        dy, d_gate1 = _residual_bwd(dh, sv["gate1"], sv["y1"], B)
        dg_2 = tn_grad("mlp_down_dw", sv["r"], dy, dg_2, l, FF, D, "rs")
        da = _mm("mlp_down_dx", dy, wg_2, mode="nt", M=N, N=FF, K=D, b_lay="rs", b_l=l, b_ns=N_CHIPS, out_dtype=BF16,
                 epi=_relu2_bwd, extras=[("mn", sv["r"])])
        dg_1 = tn_grad("mlp_up_dw", sv["u1"], da, dg_1, l, D, FF, "cs")
        du = _mm("mlp_up_dx", da, wg_1, mode="nt", M=N, N=D, K=FF, b_lay="cs", b_l=l, b_ns=N_CHIPS)
        dh, dgv, d_scale1, d_shift1 = _normmod_bwd(du, sv["h1"], ln_g_full[l, 1].reshape(1, D), sv["scale1"], dh, B)
        d_ln_g[l][1] = dgv
        d_mods[l][1] = jnp.concatenate([d_shift1, d_scale1, d_gate1], axis=2)
        dy, d_gate0 = _residual_bwd(dh, sv["gate0"], sv["y0"], B)
        if l < n_a:
            bd, cd, coef_f, coef_b, disc_vjp = s5_tabs[l]
            dzz = _glu_bwd(dy, sv["zz"])
            dg_glu = tn_grad("glu_proj_dw", sv["z"], dzz, dg_glu, l, D, 2 * D, "cs")
            dz = _mm("glu_proj_dx", dzz, wg_glu, mode="nt", M=N, N=D, K=2 * D, b_lay="cs", b_l=l, b_ns=N_CHIPS)
            du, d_bd, d_cd, d_a2, d_dskip = _s5_bwd(sv["u0"], dz, bd, cd, coef_f, coef_b, ssm_d_full[l].reshape(1, D),
                                                    sv["carries"], B)
            d_are = (d_a2[:, 0, :CHUNK_STATE] + d_a2[:, 0, CHUNK_STATE:]).reshape(-1, SSM_STATE)
            d_aim = (d_a2[:, 1, CHUNK_STATE:] - d_a2[:, 1, :CHUNK_STATE]).reshape(-1, SSM_STATE)
            d_s5[l] = disc_vjp((d_bd, d_cd, d_are, d_aim)) + (d_dskip,)
        else:
            j = l - n_a
            dg_o = tn_grad("o_proj_dw", sv["o"].reshape(N, D), dy, dg_o, j, D, D, "rs")
            do = _mm("o_proj_dx", dy, wg_o, mode="nt", M=N, N=D, K=D, b_lay="rs", b_l=j, b_ns=N_CHIPS)
            dq, dk_acc, dv_acc = _attn_bwd(sv["q"], k_all, v_all, sv["o"], sv["lse"], do.reshape(B, S, D), dk_acc, dv_acc, B)
            dq2 = dq.reshape(N, 3 * D)
            dg_q = tn_grad("q_proj_dw", sv["u0"], dq2, dg_q, j, D, 3 * D, "cs")
            du = _mm("q_proj_dx", dq2, wg_q, mode="nt", M=N, N=D, K=3 * D, b_lay="cs", b_l=j, b_ns=N_CHIPS)
        dh, dgv, d_scale0, d_shift0 = _normmod_bwd(du, sv["h0"], ln_g_full[l, 0].reshape(1, D), sv["scale0"], dh, B)
        d_ln_g[l][0] = dgv
        d_mods[l][0] = jnp.concatenate([d_shift0, d_scale0, d_gate0], axis=2)
        if l == n_a:
            dk2, dv2 = dk_acc.reshape(N, 3 * D), dv_acc.reshape(N, 3 * D)
            dg_kv = tn_grad("kv_proj_dw", u_kv, dk2, dg_kv, 0, D, 3 * D, "cs", s0=0, ns=half)
            dg_kv = tn_grad("kv_proj_dw", u_kv, dv2, dg_kv, 0, D, 3 * D, "cs", s0=half, ns=half)
            du_kv = _mm("kv_proj_dx", dk2, wg_kv, mode="nt", M=N, N=D, K=3 * D, b_lay="cs", b_s0=0, b_ns=half)
            du_kv = _mm("kv_proj_dx", dv2, wg_kv, mode="nt", M=N, N=D, K=3 * D, b_lay="cs", b_s0=half, b_ns=half,
                        epi=_add, extras=[("mn", du_kv)])
            dh, d_kv_g, d_kv_scale, d_kv_shift = _normmod_bwd(du_kv, h_kv, kv_g.reshape(1, D), kv_scale, dh, B)
    grad_x = dh.reshape(B, S, D)

    recv = _scatter_grads([dg_glu, dg_kv, dg_q, dg_o, dg_1, dg_2])
    own = [_sum_shards(r.reshape(N_CHIPS, -1, r.shape[-1])) for r in recv]
    other = _swap_with_sibling(own)

    d_kvmod = jnp.concatenate([d_kv_shift, d_kv_scale], axis=2).reshape(B, 2 * D)
    d_mod_all = jnp.concatenate([d_mods[l][s].reshape(B, 3 * D) for l in range(depth) for s in range(2)], axis=1)
    small = [
        d_mod_all, d_kvmod,
        jnp.stack([jnp.stack([d_ln_g[l][0].reshape(D), d_ln_g[l][1].reshape(D)]) for l in range(depth)]),
        jnp.stack([d_s5[l][0] for l in range(n_a)]), jnp.stack([d_s5[l][1] for l in range(n_a)]),
        jnp.stack([d_s5[l][2] for l in range(n_a)]),
        jnp.stack([d_s5[l][3] for l in range(n_a)]), jnp.stack([d_s5[l][4] for l in range(n_a)]),
        jnp.stack([d_s5[l][5] for l in range(n_a)]), jnp.stack([d_s5[l][6] for l in range(n_a)]),
        jnp.stack([d_s5[l][7].reshape(D) for l in range(n_a)]),
        d_kv_g.reshape(D), d_final_g.reshape(D),
    ]
    small_pack, small_layout, small_rows = _pack(small, D)
    small_buf = _all_gather_small(small_pack)
    small_sum = _sum_shards(small_buf.reshape(N_DEV, small_rows, D))
    red = lambda idx: _unpack(small_sum, small_layout, idx)
    per_dev = lambda idx: jnp.concatenate(
        [_unpack(small_buf[d * small_rows:(d + 1) * small_rows], small_layout, idx) for d in range(N_DEV)], axis=0)

    dm_all = per_dev(0).reshape(n_ex, n_mod, 3 * D)
    dm_cols = lax.dynamic_slice_in_dim(dm_all, chip * ada_cols, ada_cols, axis=2).reshape(n_ex, n_mod * ada_cols)
    g_ada_w = _mm("ada_dw", sc_all, _as4(dm_cols), mode="tn", M=D, N=n_mod * ada_cols, K=n_ex, b_lay="cs",
                  out_lay="cs", out4_shape=(n_mod, 1, D, ada_cols), out_ns=n_mod).reshape(ada_w.shape)
    dkvm_all = per_dev(1)
    dkvm_cols = lax.dynamic_slice_in_dim(dkvm_all, chip * kv_cols, kv_cols, axis=1)
    g_kv_ada_w = _mm("ada_dw", sc_all, _as4(dkvm_cols), mode="tn", M=D, N=kv_cols, K=n_ex, b_lay="cs")
    g_ada_b_full = (red(0)[0] + red(0)[1]).reshape(depth, 2, 3 * D) if B == 2 else jnp.sum(red(0), axis=0).reshape(depth, 2, 3 * D)
    g_ada_b = lax.dynamic_slice_in_dim(g_ada_b_full, chip * ada_cols, ada_cols, axis=2)
    g_kv_ada_b = red(1)[0] + red(1)[1] if B == 2 else jnp.sum(red(1), axis=0)
    g_ln_g = lax.dynamic_slice_in_dim(red(2), chip * (D // N_CHIPS), D // N_CHIPS, axis=2)
    g_ssm_d = lax.dynamic_slice_in_dim(red(10), chip * (D // N_CHIPS), D // N_CHIPS, axis=1)
    small_grads = {
        "ln_g": g_ln_g, "ada_b": g_ada_b, "ssm_lam_re": red(3), "ssm_lam_im": red(4), "ssm_log_dt": red(5),
        "ssm_b_re": red(6), "ssm_b_im": red(7), "ssm_c_re": red(8), "ssm_c_im": red(9), "ssm_d": g_ssm_d,
        "kv_g": red(11), "kv_ada_b": g_kv_ada_b, "final_g": red(12),
    }
    small_w = {"ln_g": (ln_g, m_ln_g, v_ln_g), "ada_b": (ada_b, m_ada_b, v_ada_b),
               "ssm_lam_re": (ssm_lam_re, m_ssm_lam_re, v_ssm_lam_re), "ssm_lam_im": (ssm_lam_im, m_ssm_lam_im, v_ssm_lam_im),
               "ssm_log_dt": (ssm_log_dt, m_ssm_log_dt, v_ssm_log_dt), "ssm_b_re": (ssm_b_re, m_ssm_b_re, v_ssm_b_re),
               "ssm_b_im": (ssm_b_im, m_ssm_b_im, v_ssm_b_im), "ssm_c_re": (ssm_c_re, m_ssm_c_re, v_ssm_c_re),
               "ssm_c_im": (ssm_c_im, m_ssm_c_im, v_ssm_c_im), "ssm_d": (ssm_d, m_ssm_d, v_ssm_d),
               "kv_g": (kv_g, m_kv_g, v_kv_g), "kv_ada_b": (kv_ada_b, m_kv_ada_b, v_kv_ada_b),
               "final_g": (final_g, m_final_g, v_final_g)}
    names = list(small_w)
    wp, lay_w, _ = _pack([small_w[n][0] for n in names], D)
    gp, _, _ = _pack([small_grads[n] for n in names], D)
    mp, _, _ = _pack([small_w[n][1] for n in names], D)
    vp, _, _ = _pack([small_w[n][2] for n in names], D)
    _, d_p, m_p, v_p = _adamw(wp, [gp], mp, vp)
    upd = {n: (small_grads[n].reshape(small_w[n][0].shape), _unpack(d_p, lay_w, i), _unpack(m_p, lay_w, i), _unpack(v_p, lay_w, i))
           for i, n in enumerate(names)}

    def big(w, m, v, g_own, g_other=None):
        C = w.shape[-1]
        gs = [g_own.reshape(-1, C)] + ([g_other.reshape(-1, C)] if g_other is not None else [])
        return tuple(t.reshape(w.shape) for t in _adamw(w.reshape(-1, C), gs, m.reshape(-1, C), v.reshape(-1, C)))

    upd["ssm_w_glu"] = big(ssm_w_glu, m_ssm_w_glu, v_ssm_w_glu, own[0], other[0])
    upd["w_kv"] = big(w_kv, m_w_kv, v_w_kv, own[1], other[1])
    upd["attn_w_q"] = big(attn_w_q, m_attn_w_q, v_attn_w_q, own[2], other[2])
    upd["attn_w_o"] = big(attn_w_o, m_attn_w_o, v_attn_w_o, own[3], other[3])
    upd["mlp_w1"] = big(mlp_w1, m_mlp_w1, v_mlp_w1, own[4], other[4])
    upd["mlp_w2"] = big(mlp_w2, m_mlp_w2, v_mlp_w2, own[5], other[5])
    upd["ada_w"] = big(ada_w, m_ada_w, v_ada_w, g_ada_w)
    upd["kv_ada_w"] = big(kv_ada_w, m_kv_ada_w, v_kv_ada_w, g_kv_ada_w)

    order = ["ln_g", "ada_w", "ada_b", "ssm_lam_re", "ssm_lam_im", "ssm_log_dt", "ssm_b_re", "ssm_b_im", "ssm_c_re",
             "ssm_c_im", "ssm_d", "ssm_w_glu", "kv_g", "kv_ada_w", "kv_ada_b", "w_kv", "attn_w_q", "attn_w_o", "mlp_w1",
             "mlp_w2", "final_g"]
    return (loss, grad_x, *[upd[n][0] for n in order], *[upd[n][1] for n in order], *[upd[n][2] for n in order],
            *[upd[n][3] for n in order])
```

```python
import functools
import math

import jax
import jax.numpy as jnp
from jax import lax
from jax.experimental import pallas as pl
from jax.experimental.pallas import tpu as pltpu

F32 = jnp.float32
BF16 = jnp.bfloat16
MESH = pl.DeviceIdType.MESH

EPS = 1e-6
NEG = -1e30
SSM_GROUP = 16
SSM_STATE = 64
HEAD_DIM = 64
ATTN_BLOCK = 128
DILATIONS = (1, 4, 16)
ADAM_LR, ADAM_B1, ADAM_B2, ADAM_EPS, ADAM_WD, ADAM_STEP = 0.001, 0.9, 0.999, 1e-08, 0.01, 10

LANES = 128
SUBLANES = 8
CHUNK_GROUPS = LANES // SSM_GROUP
CHUNK_STATE = CHUNK_GROUPS * SSM_STATE
VMEM_LIMIT = 56 * 1024 * 1024


def _div(dim, pref, mult):
    t = min(pref, dim) // mult * mult
    while t >= mult:
        if dim % t == 0:
            return t
        t -= mult
    return dim


def _params(*sem):
    return pltpu.CompilerParams(dimension_semantics=sem, vmem_limit_bytes=VMEM_LIMIT)


def _mm(name, a, b4, *, mode, M, N, K, b_lay="cs", b_l=0, b_s0=0, b_ns=1, out_dtype=F32, out_lay=None, out4_shape=None,
        out_into=None, out_l=0, out_s0=0, out_ns=1, epi=None, extras=(), rows_per_ex=None, tm=1024, tn=1024, tk=1024):
    _, _, bR, bC = b4.shape
    tm = _div(M, tm, SUBLANES if M % 16 else 16)
    brows, bcols = (N, K) if mode == "nt" else (K, N)
    if b_lay == "cs":
        assert bR == brows and bC * b_ns == bcols, (name, b4.shape, brows, bcols)
    else:
        assert bC == bcols and bR * b_ns == brows, (name, b4.shape, brows, bcols)
    n_lim = N
    k_lim = K
    if mode == "nt":
        if b_lay == "cs":
            k_lim = bC
        else:
            n_lim = bR
    else:
        if b_lay == "cs":
            n_lim = bC
        else:
            k_lim = bR
    if out_lay == "cs":
        oR, oC = out4_shape[2], out4_shape[3]
        assert oR == M and oC * out_ns == N, (name, out4_shape, M, N)
        n_lim = math.gcd(n_lim, oC)
    elif out_lay == "rs":
        oR, oC = out4_shape[2], out4_shape[3]
        assert oC == N and oR * out_ns == M, (name, out4_shape, M, N)
        tm = _div(oR, tm, SUBLANES)
    tn = _div(n_lim, tn, LANES)
    tk = _div(k_lim, tk, LANES if mode != "tn" else SUBLANES)
    if mode == "tn":
        tk = _div(k_lim, tk, 16) if k_lim % 16 == 0 else tk
    nk = K // tk
    grid = (M // tm, N // tn, nk)

    if mode == "tn":
        a_spec = pl.BlockSpec((tk, tm), lambda i, j, k: (k, i))
    else:
        a_spec = pl.BlockSpec((tm, tk), lambda i, j, k: (i, k))

    def b_index(ri, ci, br, bc):
        if b_lay == "cs":
            per = bC // bc
            return (b_s0 + ci // per, b_l, ri, ci % per)
        per = bR // br
        return (b_s0 + ri // per, b_l, ri % per, ci)

    if mode == "nt":
        b_spec = pl.BlockSpec((None, None, tn, tk), lambda i, j, k: b_index(j, k, tn, tk))
    else:
        b_spec = pl.BlockSpec((None, None, tk, tn), lambda i, j, k: b_index(k, j, tk, tn))

    in_specs = [a_spec, b_spec]
    operands = [a, b4]
    for kind, arr in extras:
        if kind == "mn":
            in_specs.append(pl.BlockSpec((tm, tn), lambda i, j, k: (i, j)))
        elif kind == "ex":
            per_ex = rows_per_ex // tm
            in_specs.append(pl.BlockSpec((None, 1, tn), lambda i, j, k: (i // per_ex, 0, j)))
        else:
            in_specs.append(pl.BlockSpec((1, tn), lambda i, j, k: (0, j)))
        operands.append(arr)
    n_extra = len(extras)

    multi = isinstance(out_dtype, tuple)
    n_out = len(out_dtype) if multi else 1
    if out_lay is None:
        out_shape = [jax.ShapeDtypeStruct((M, N), dt) for dt in (out_dtype if multi else (out_dtype,))]
        out_spec = [pl.BlockSpec((tm, tn), lambda i, j, k: (i, j)) for _ in range(n_out)]
    else:
        out_shape = [jax.ShapeDtypeStruct(tuple(out4_shape), out_dtype)]
        if out_lay == "cs":
            per_o = oC // tn
            out_spec = [pl.BlockSpec((None, None, tm, tn), lambda i, j, k: (out_s0 + j // per_o, out_l, i, j % per_o))]
        else:
            per_o = oR // tm
            out_spec = [pl.BlockSpec((None, None, tm, tn), lambda i, j, k: (out_s0 + i // per_o, out_l, i % per_o, j))]
    aliases = {}
    if out_into is not None:
        in_specs.append(pl.BlockSpec(memory_space=pl.ANY))
        operands.append(out_into)
        aliases = {len(operands) - 1: 0}

    dims = {"nn": (((1,), (0,)), ((), ())), "nt": (((1,), (1,)), ((), ())), "tn": (((0,), (0,)), ((), ()))}[mode]

    def body(a_ref, b_ref, *rest):
        extra_refs = rest[:n_extra]
        o_refs = rest[len(rest) - n_out - (nk > 1):len(rest) - (nk > 1)]

        def finish(r):
            if epi is not None:
                r = epi(r, *[e[...] for e in extra_refs])
            for o_ref, val in zip(o_refs, r if multi else (r,)):
                o_ref[...] = val.astype(o_ref.dtype)

        part = lax.dot_general(a_ref[...].astype(BF16), b_ref[...].astype(BF16), dims, preferred_element_type=F32)
        if nk == 1:
            finish(part)
            return
        acc = rest[-1]
        k = pl.program_id(2)

        @pl.when(k == 0)
        def _():
            acc[...] = part

        @pl.when(k != 0)
        def _():
            acc[...] += part

        @pl.when(k == nk - 1)
        def _():
            finish(acc[...])

    outs = pl.pallas_call(
        body, name=name, grid=grid, in_specs=in_specs, out_specs=out_spec, out_shape=out_shape,
        scratch_shapes=[pltpu.VMEM((tm, tn), F32)] if nk > 1 else [], input_output_aliases=aliases,
        compiler_params=_params("parallel", "parallel", "arbitrary"),
    )(*operands)
    return tuple(outs) if multi else outs[0]


def _as4(w):
    return w.reshape((1, 1) + w.shape)


def _relu2(acc):
    r = jnp.maximum(acc, 0.0)
    return r * r


def _relu2_bwd(acc, r):
    return acc * (2.0 * jnp.sqrt(r.astype(F32)))


def _add(acc, e):
    return acc + e


def _gated_residual(acc, h, gate):
    return acc, h + gate * acc


def _row_tiles(N, B, pref=256):
    S = N // B
    tm = _div(S, pref, SUBLANES)
    return tm, S // tm


def _normmod(h, g, scale, shift, B):
    N, D = h.shape
    tm, per_ex = _row_tiles(N, B)

    def body(h_ref, g_ref, sc_ref, sh_ref, u_ref):
        x = h_ref[...]
        rstd = lax.rsqrt(jnp.mean(x * x, axis=-1, keepdims=True) + EPS)
        y = (x * rstd) * g_ref[...]
        u_ref[...] = (y * (1.0 + sc_ref[...]) + sh_ref[...]).astype(u_ref.dtype)

    tok = pl.BlockSpec((tm, D), lambda i: (i, 0))
    vec = pl.BlockSpec((1, D), lambda i: (0, 0))
    ex = pl.BlockSpec((None, 1, D), lambda i: (i // per_ex, 0, 0))
    return pl.pallas_call(
        body, name="normmod_fwd", grid=(N // tm,), in_specs=[tok, vec, ex, ex], out_specs=tok,
        out_shape=jax.ShapeDtypeStruct((N, D), BF16), compiler_params=_params("parallel"),
    )(h, g, scale, shift)


def _normmod_bwd(du, h, g, scale, dh_in, B):
    N, D = h.shape
    tm, per_ex = _row_tiles(N, B)

    def body(du_ref, h_ref, g_ref, sc_ref, dhin_ref, dh_ref, dg_ref, dsc_ref, dsh_ref):
        i = pl.program_id(0)
        x = h_ref[...]
        gv = g_ref[...]
        d_u = du_ref[...].astype(F32)
        rstd = lax.rsqrt(jnp.mean(x * x, axis=-1, keepdims=True) + EPS)
        xn = x * rstd
        dyg = d_u * (1.0 + sc_ref[...])
        dxn = dyg * gv
        dh_ref[...] = dhin_ref[...] + rstd * (dxn - xn * jnp.mean(dxn * xn, axis=-1, keepdims=True))
        dsh_t = jnp.sum(d_u, axis=0, keepdims=True)
        dsc_t = jnp.sum(d_u * (xn * gv), axis=0, keepdims=True)
        dg_t = jnp.sum(dyg * xn, axis=0, keepdims=True)

        @pl.when(i % per_ex == 0)
        def _():
            dsc_ref[...] = dsc_t
            dsh_ref[...] = dsh_t

        @pl.when(i % per_ex != 0)
        def _():
            dsc_ref[...] += dsc_t
            dsh_ref[...] += dsh_t

        @pl.when(i == 0)
        def _():
            dg_ref[...] = dg_t

        @pl.when(i != 0)
        def _():
            dg_ref[...] += dg_t

    tok = pl.BlockSpec((tm, D), lambda i: (i, 0))
    vec = pl.BlockSpec((1, D), lambda i: (0, 0))
    ex = pl.BlockSpec((None, 1, D), lambda i: (i // per_ex, 0, 0))
    return pl.pallas_call(
        body, name="normmod_bwd", grid=(N // tm,), in_specs=[tok, tok, vec, ex, tok], out_specs=[tok, vec, ex, ex],
        out_shape=[jax.ShapeDtypeStruct((N, D), F32), jax.ShapeDtypeStruct((1, D), F32),
                   jax.ShapeDtypeStruct((B, 1, D), F32), jax.ShapeDtypeStruct((B, 1, D), F32)],
        compiler_params=_params("arbitrary"),
    )(du, h, g, scale, dh_in)


def _residual_bwd(dh, gate, y, B):
    N, D = dh.shape
    tm, per_ex = _row_tiles(N, B)

    def body(dh_ref, gt_ref, y_ref, dy_ref, dgt_ref):
        i = pl.program_id(0)
        d = dh_ref[...]
        dy_ref[...] = (gt_ref[...] * d).astype(dy_ref.dtype)
        t = jnp.sum(d * y_ref[...], axis=0, keepdims=True)

        @pl.when(i % per_ex == 0)
        def _():
            dgt_ref[...] = t

        @pl.when(i % per_ex != 0)
        def _():
            dgt_ref[...] += t

    tok = pl.BlockSpec((tm, D), lambda i: (i, 0))
    ex = pl.BlockSpec((None, 1, D), lambda i: (i // per_ex, 0, 0))
    return pl.pallas_call(
        body, name="residual_bwd", grid=(N // tm,), in_specs=[tok, ex, tok], out_specs=[tok, ex],
        out_shape=[jax.ShapeDtypeStruct((N, D), BF16), jax.ShapeDtypeStruct((B, 1, D), F32)],
        compiler_params=_params("arbitrary"),
    )(dh, gate, y)


def _glu_residual(zz, h, gate, B):
    N, D2 = zz.shape
    D = D2 // 2
    tm, per_ex = _row_tiles(N, B)

    def body(v_ref, g_ref, h_ref, gt_ref, y_ref, o_ref):
        y = v_ref[...] * jax.nn.sigmoid(g_ref[...])
        y_ref[...] = y.astype(y_ref.dtype)
        o_ref[...] = h_ref[...] + gt_ref[...] * y

    tok = pl.BlockSpec((tm, D), lambda i: (i, 0))
    return pl.pallas_call(
        body, name="glu_fwd", grid=(N // tm,),
        in_specs=[tok, pl.BlockSpec((tm, D), lambda i: (i, 1)), tok,
                  pl.BlockSpec((None, 1, D), lambda i: (i // per_ex, 0, 0))],
        out_specs=[tok, tok], out_shape=[jax.ShapeDtypeStruct((N, D), BF16), jax.ShapeDtypeStruct((N, D), F32)],
        compiler_params=_params("parallel"),
    )(zz, zz, h, gate)


def _glu_bwd(dy, zz):
    N, D2 = zz.shape
    D = D2 // 2
    tm = _div(N, 256, SUBLANES)

    def body(dy_ref, v_ref, g_ref, o_ref):
        d = dy_ref[...].astype(F32)
        s = jax.nn.sigmoid(g_ref[...])
        o_ref[...] = jnp.concatenate([d * s, d * v_ref[...] * s * (1.0 - s)], axis=1).astype(o_ref.dtype)

    return pl.pallas_call(
        body, name="glu_bwd", grid=(N // tm,),
        in_specs=[pl.BlockSpec((tm, D), lambda i: (i, 0)), pl.BlockSpec((tm, D), lambda i: (i, 0)),
                  pl.BlockSpec((tm, D), lambda i: (i, 1))],
        out_specs=pl.BlockSpec((tm, D2), lambda i: (i, 0)), out_shape=jax.ShapeDtypeStruct((N, D2), BF16),
        compiler_params=_params("parallel"),
    )(dy, zz, zz)


def _loss_head(h, g, target):
    N, D = h.shape
    tm = _div(N, 256, SUBLANES)

    def body(h_ref, g_ref, t_ref, loss_ref, dh_ref, dg_ref):
        i = pl.program_id(0)
        x = h_ref[...]
        gv = g_ref[...]
        rstd = lax.rsqrt(jnp.mean(x * x, axis=-1, keepdims=True) + EPS)
        xn = x * rstd
        err = xn * gv - t_ref[...]
        part = 0.5 * jnp.sum(jnp.sum(err * err, axis=-1, keepdims=True) / D, axis=0, keepdims=True)
        dy = err / D
        dxn = dy * gv
        dh_ref[...] = rstd * (dxn - xn * jnp.mean(dxn * xn, axis=-1, keepdims=True))
        dg_t = jnp.sum(dy * xn, axis=0, keepdims=True)
        part = jnp.broadcast_to(part, loss_ref.shape)

        @pl.when(i == 0)
        def _():
            loss_ref[...] = part
            dg_ref[...] = dg_t

        @pl.when(i != 0)
        def _():
            loss_ref[...] += part
            dg_ref[...] += dg_t

    tok = pl.BlockSpec((tm, D), lambda i: (i, 0))
    vec = pl.BlockSpec((1, D), lambda i: (0, 0))
    return pl.pallas_call(
        body, name="loss_head", grid=(N // tm,), in_specs=[tok, vec, tok],
        out_specs=[pl.BlockSpec((SUBLANES, LANES), lambda i: (0, 0)), tok, vec],
        out_shape=[jax.ShapeDtypeStruct((SUBLANES, LANES), F32), jax.ShapeDtypeStruct((N, D), F32),
                   jax.ShapeDtypeStruct((1, D), F32)],
        compiler_params=_params("arbitrary"),
    )(h, g, target)


def _swap_halves(x):
    half = x.shape[-1] // 2
    return jnp.concatenate([x[:, half:], x[:, :half]], axis=1)


def _gelu(y):
    return jax.nn.gelu(y)


def _gelu_grad(y):
    c0 = math.sqrt(2.0 / math.pi)
    inner = c0 * (y + 0.044715 * y * y * y)
    t = jnp.tanh(inner)
    return 0.5 * (1.0 + t) + 0.5 * y * (1.0 - t * t) * c0 * (1.0 + 3.0 * 0.044715 * y * y)


def _s5_discretize(lam_re, lam_im, log_dt, b_re, b_im, c_re, c_im):
    G = lam_re.shape[0]
    nch = G // CHUNK_GROUPS
    dt = jnp.exp(log_dt)[:, None]
    er = jnp.exp(lam_re * dt)
    a_re = er * jnp.cos(lam_im * dt)
    a_im = er * jnp.sin(lam_im * dt)
    den = lam_re * lam_re + lam_im * lam_im
    n_re, n_im = a_re - 1.0, a_im
    f_re = (n_re * lam_re + n_im * lam_im) / den
    f_im = (n_im * lam_re - n_re * lam_im) / den
    bb_re = f_re[..., None] * b_re - f_im[..., None] * b_im
    bb_im = f_re[..., None] * b_im + f_im[..., None] * b_re
    eye = jnp.eye(CHUNK_GROUPS, dtype=F32)

    def pack_b(bb):
        bb = bb.reshape(nch, CHUNK_GROUPS, SSM_STATE, SSM_GROUP)
        return jnp.einsum("jgpc,gh->jgchp", bb, eye).reshape(nch, LANES, CHUNK_STATE)

    def pack_c(cc):
        cc = cc.reshape(nch, CHUNK_GROUPS, SSM_GROUP, SSM_STATE)
        return jnp.einsum("jgcp,gh->jgphc", cc, eye).reshape(nch, CHUNK_STATE, LANES)

    bd = jnp.concatenate([pack_b(bb_re), pack_b(bb_im)], axis=2)
    cd = jnp.concatenate([pack_c(c_re), pack_c(-c_im)], axis=1)
    return bd, cd, a_re, a_im


def _s5_scan_coefs(lam_re, lam_im, log_dt):
    G = lam_re.shape[0]
    nch = G // CHUNK_GROUPS
    dt = jnp.exp(log_dt)[:, None]

    def power(k):
        er = jnp.exp(k * lam_re * dt)
        re = (er * jnp.cos(k * lam_im * dt)).reshape(nch, 1, CHUNK_STATE)
        im = (er * jnp.sin(k * lam_im * dt)).reshape(nch, 1, CHUNK_STATE)
        return re, im

    pw = {k: power(k) for k in range(1, SUBLANES + 1)}
    row = jnp.arange(SUBLANES, dtype=jnp.int32)[None, :, None]

    def table(reverse):
        sign = -1.0 if reverse else 1.0
        tabs = []
        for s in (1, 2, 4):
            re, im = pw[s]
            mask = (row < SUBLANES - s) if reverse else (row >= s)
            ar = jnp.where(mask, jnp.concatenate([re, re], axis=2), 0.0)
            ai = jnp.where(mask, jnp.concatenate([-sign * im, sign * im], axis=2), 0.0)
            tabs += [ar, ai]
        pr_rows, pi_rows = [], []
        for j in range(SUBLANES):
            re, im = pw[SUBLANES - j] if reverse else pw[j + 1]
            pr_rows.append(jnp.concatenate([re, re], axis=2))
            pi_rows.append(jnp.concatenate([-sign * im, sign * im], axis=2))
        tabs += [jnp.concatenate(pr_rows, axis=1), jnp.concatenate(pi_rows, axis=1)]
        return jnp.stack([jnp.broadcast_to(t, (nch, SUBLANES, 2 * CHUNK_STATE)) for t in tabs], axis=1)

    return table(False), table(True)


def _scan_rows(x_ref, coef_ref, carry_ref, n_groups, reverse, prev_ref=None):
    row = lax.broadcasted_iota(jnp.int32, (SUBLANES, x_ref.shape[-1]), 0)

    def group(t, _):
        i = (n_groups - 1 - t) if reverse else t
        rows = pl.ds(pl.multiple_of(i * SUBLANES, SUBLANES), SUBLANES)
        x = x_ref[rows, :]
        for si, s in enumerate((1, 2, 4)):
            xs = pltpu.roll(x, (SUBLANES - s) if reverse else s, 0)
            x = x + coef_ref[2 * si] * xs + coef_ref[2 * si + 1] * _swap_halves(xs)
        c = carry_ref[...]
        x = x + coef_ref[6] * c + coef_ref[7] * _swap_halves(c)
        if prev_ref is not None:
            prev_ref[rows, :] = jnp.where(row == 0, c, pltpu.roll(x, 1, 0))
        edge = x[0:1, :] if reverse else x[SUBLANES - 1:SUBLANES, :]
        carry_ref[...] = jnp.broadcast_to(edge, carry_ref.shape)
        x_ref[rows, :] = x
        return 0

    lax.fori_loop(0, n_groups, group, 0)


def _s5_fwd(u, bd, cd, coef_f, d_skip, B):
    N, D = u.shape
    S = N // B
    nch = D // LANES
    W = 2 * CHUNK_STATE
    tm = _div(S, 256, SUBLANES)
    nt = S // tm

    def body(u_ref, bd_ref, cd_ref, coef_ref, d_ref, z_ref, cin_ref, x_s, carry):
        t = pl.program_id(2)

        @pl.when(t == 0)
        def _():
            carry[...] = jnp.zeros_like(carry)

        cin_ref[...] = carry[...]
        uv = u_ref[...]
        x_s[...] = jnp.dot(uv, bd_ref[...], preferred_element_type=F32)
        _scan_rows(x_s, coef_ref, carry, tm // SUBLANES, False)
        y = jnp.dot(x_s[...].astype(BF16), cd_ref[...], preferred_element_type=F32) + d_ref[...] * uv.astype(F32)
        z_ref[...] = _gelu(y).astype(z_ref.dtype)

    return pl.pallas_call(
        body, name="s5_fwd", grid=(nch, B, nt),
        in_specs=[pl.BlockSpec((tm, LANES), lambda j, b, t: (b * nt + t, j)),
                  pl.BlockSpec((None, LANES, W), lambda j, b, t: (j, 0, 0)),
                  pl.BlockSpec((None, W, LANES), lambda j, b, t: (j, 0, 0)),
                  pl.BlockSpec((None, 8, SUBLANES, W), lambda j, b, t: (j, 0, 0, 0)),
                  pl.BlockSpec((1, LANES), lambda j, b, t: (0, j))],
        out_specs=[pl.BlockSpec((tm, LANES), lambda j, b, t: (b * nt + t, j)),
                   pl.BlockSpec((None, None, SUBLANES, W), lambda j, b, t: (j, b * nt + t, 0, 0))],
        out_shape=[jax.ShapeDtypeStruct((N, D), BF16), jax.ShapeDtypeStruct((nch, B * nt, SUBLANES, W), F32)],
        scratch_shapes=[pltpu.VMEM((tm, W), F32), pltpu.VMEM((SUBLANES, W), F32)],
        compiler_params=_params("parallel", "arbitrary", "arbitrary"),
    )(u, bd, cd, coef_f, d_skip)


def _s5_bwd(u, dz, bd, cd, coef_f, coef_b, d_skip, carries, B):
    N, D = u.shape
    S = N // B
    nch = D // LANES
    W = 2 * CHUNK_STATE
    tm = carries.shape[1] // B
    tm = S // tm
    nt = S // tm
    ng = tm // SUBLANES

    def body(u_ref, dz_ref, bd_ref, cd_ref, cf_ref, cb_ref, d_ref, cin_ref,
             du_ref, dbd_ref, dcd_ref, da_ref, dd_ref, x_s, xp_s, l_s, carry, lcarry):
        b = pl.program_id(1)
        t = pl.program_id(2)

        @pl.when((b == 0) & (t == 0))
        def _():
            dbd_ref[...] = jnp.zeros_like(dbd_ref)
            dcd_ref[...] = jnp.zeros_like(dcd_ref)
            da_ref[...] = jnp.zeros_like(da_ref)
            dd_ref[...] = jnp.zeros_like(dd_ref)

        @pl.when(t == 0)
        def _():
            lcarry[...] = jnp.zeros_like(lcarry)

        uv = u_ref[...]
        uf = uv.astype(F32)
        carry[...] = cin_ref[...]
        x_s[...] = jnp.dot(uv, bd_ref[...], preferred_element_type=F32)
        _scan_rows(x_s, cf_ref, carry, ng, False, prev_ref=xp_s)
        xb = x_s[...].astype(BF16)
        y = jnp.dot(xb, cd_ref[...], preferred_element_type=F32) + d_ref[...] * uf
        dy = dz_ref[...] * _gelu_grad(y)
        dd_ref[...] += jnp.sum(dy * uf, axis=0, keepdims=True)
        dyb = dy.astype(BF16)
        dcd_ref[...] += lax.dot_general(xb, dyb, (((0,), (0,)), ((), ())), preferred_element_type=F32)
        l_s[...] = lax.dot_general(dyb, cd_ref[...], (((1,), (1,)), ((), ())), preferred_element_type=F32)
        _scan_rows(l_s, cb_ref, lcarry, ng, True)
        lam = l_s[...]
        lb = lam.astype(BF16)
        dbd_ref[...] += lax.dot_general(uv, lb, (((0,), (0,)), ((), ())), preferred_element_type=F32)
        du_ref[...] = lax.dot_general(lb, bd_ref[...], (((1,), (1,)), ((), ())), preferred_element_type=F32) + d_ref[...] * dy
        xp = xp_s[...]
        da_ref[0:1, :] += jnp.sum(lam * xp, axis=0, keepdims=True)
        da_ref[1:2, :] += jnp.sum(lam * _swap_halves(xp), axis=0, keepdims=True)

    tile = lambda j, b, t: (b * nt + (nt - 1 - t), j)
    return pl.pallas_call(
        body, name="s5_bwd", grid=(nch, B, nt),
        in_specs=[pl.BlockSpec((tm, LANES), tile), pl.BlockSpec((tm, LANES), tile),
                  pl.BlockSpec((None, LANES, W), lambda j, b, t: (j, 0, 0)),
                  pl.BlockSpec((None, W, LANES), lambda j, b, t: (j, 0, 0)),
                  pl.BlockSpec((None, 8, SUBLANES, W), lambda j, b, t: (j, 0, 0, 0)),
                  pl.BlockSpec((None, 8, SUBLANES, W), lambda j, b, t: (j, 0, 0, 0)),
                  pl.BlockSpec((1, LANES), lambda j, b, t: (0, j)),
                  pl.BlockSpec((None, None, SUBLANES, W), lambda j, b, t: (j, b * nt + (nt - 1 - t), 0, 0))],
        out_specs=[pl.BlockSpec((tm, LANES), tile),
                   pl.BlockSpec((None, LANES, W), lambda j, b, t: (j, 0, 0)),
                   pl.BlockSpec((None, W, LANES), lambda j, b, t: (j, 0, 0)),
                   pl.BlockSpec((None, 2, W), lambda j, b, t: (j, 0, 0)),
                   pl.BlockSpec((1, LANES), lambda j, b, t: (0, j))],
        out_shape=[jax.ShapeDtypeStruct((N, D), F32), jax.ShapeDtypeStruct((nch, LANES, W), F32),
                   jax.ShapeDtypeStruct((nch, W, LANES), F32), jax.ShapeDtypeStruct((nch, 2, W), F32),
                   jax.ShapeDtypeStruct((1, D), F32)],
        scratch_shapes=[pltpu.VMEM((tm, W), F32), pltpu.VMEM((tm, W), F32), pltpu.VMEM((tm, W), F32),
                        pltpu.VMEM((SUBLANES, W), F32), pltpu.VMEM((SUBLANES, W), F32)],
        compiler_params=_params("parallel", "arbitrary", "arbitrary"),
    )(u, dz, bd, cd, coef_f, coef_b, d_skip, carries)


ATTN_HEADS = LANES // HEAD_DIM
ATTN_UNROLL = 2


def _attn_mask(n):
    qi = lax.broadcasted_iota(jnp.int32, (ATTN_BLOCK, 2 * ATTN_BLOCK), 0)
    kj = lax.broadcasted_iota(jnp.int32, (ATTN_BLOCK, 2 * ATTN_BLOCK), 1)
    prev_ok = (kj < ATTN_BLOCK) & (kj >= qi) & (n > 0)
    return prev_ok | ((kj >= ATTN_BLOCK) & (kj - ATTN_BLOCK <= qi))


def _head_lanes(h):
    lane = lax.broadcasted_iota(jnp.int32, (ATTN_BLOCK, LANES), 1)
    return (lane >= h * HEAD_DIM) & (lane < (h + 1) * HEAD_DIM)


def _per_head(cols):
    out = jnp.broadcast_to(cols[-1], (ATTN_BLOCK, LANES))
    for h in range(len(cols) - 2, -1, -1):
        out = jnp.where(_head_lanes(h), jnp.broadcast_to(cols[h], (ATTN_BLOCK, LANES)), out)
    return out


def _only_head(x, h):
    return jnp.where(_head_lanes(h), x, 0.0).astype(BF16)


def _block_rows(tb, dil, nb):
    r = tb // nb
    n = tb % nb
    start = r + dil * ATTN_BLOCK * n
    startp = jnp.where(n > 0, start - dil * ATTN_BLOCK, start)
    return n, pl.ds(start, ATTN_BLOCK, stride=dil), pl.ds(startp, ATTN_BLOCK, stride=dil)


def _attn_fwd(q, k, v, B):
    _, S, D3 = q.shape
    D = D3 // 3
    HP = D // LANES
    scale = HEAD_DIM ** -0.5
    n_blocks = S // ATTN_BLOCK
    nbr = len(DILATIONS)
    nt_dims = (((1,), (1,)), ((), ()))

    def branch(dil, q_ref, k_ref, v_ref, acc, m_s, l_s):
        nb = (S // dil) // ATTN_BLOCK

        def blk(tb, _):
            n, rows, rowsp = _block_rows(tb, dil, nb)
            qb = q_ref[rows, :] * scale
            kk = jnp.concatenate([k_ref[rowsp, :], k_ref[rows, :]], axis=0).astype(BF16)
            vv = jnp.concatenate([v_ref[rowsp, :], v_ref[rows, :]], axis=0).astype(BF16)
            ok = _attn_mask(n)
            ms, ls, accs = [], [], []
            for h in range(ATTN_HEADS):
                s = lax.dot_general(_only_head(qb, h), kk, nt_dims, preferred_element_type=F32)
                s = jnp.where(ok, s, NEG)
                mh = jnp.max(s, axis=-1, keepdims=True)
                p = jnp.exp(s - mh)
                ms.append(mh)
                ls.append(jnp.sum(p, axis=-1, keepdims=True))
                accs.append(jnp.dot(p.astype(BF16), vv, preferred_element_type=F32))
            m_s[rows, :] = _per_head(ms)
            l_s[rows, :] = _per_head(ls)
            acc[rows, :] = _per_head(accs)
            return 0

        lax.fori_loop(0, n_blocks, blk, 0, unroll=ATTN_UNROLL)

    def body(q_ref, k_ref, v_ref, o_ref, lse_ref, *scratch):
        accs, m_ss, l_ss = scratch[:nbr], scratch[nbr:2 * nbr], scratch[2 * nbr:]
        g = pl.program_id(2)
        for gi, dil in enumerate(DILATIONS):
            pl.when(g == gi)(functools.partial(branch, dil, q_ref, k_ref, v_ref, accs[gi], m_ss[gi], l_ss[gi]))

        @pl.when(g == nbr - 1)
        def _():
            def fin(i, _):
                rows = pl.ds(pl.multiple_of(i * ATTN_BLOCK, ATTN_BLOCK), ATTN_BLOCK)
                ms = [m[rows, :] for m in m_ss]
                m_all = functools.reduce(jnp.maximum, ms)
                ws = [jnp.exp(m - m_all) for m in ms]
                den = sum(w * l[rows, :] for w, l in zip(ws, l_ss))
                o_ref[rows, :] = sum(w * a[rows, :] for w, a in zip(ws, accs)) / den
                lse_ref[rows, :] = m_all + jnp.log(den)
                return 0

            lax.fori_loop(0, n_blocks, fin, 0)

    br = pl.BlockSpec((None, S, LANES), lambda b, hp, g: (b, 0, g * HP + hp))
    hd = pl.BlockSpec((None, S, LANES), lambda b, hp, g: (b, 0, hp))
    return pl.pallas_call(
        body, name="attn_fwd", grid=(B, HP, nbr), in_specs=[br, br, br], out_specs=[hd, hd],
        out_shape=[jax.ShapeDtypeStruct((B, S, D), F32), jax.ShapeDtypeStruct((B, S, D), F32)],
        scratch_shapes=[pltpu.VMEM((S, LANES), F32)] * (3 * nbr),
        compiler_params=_params("parallel", "parallel", "arbitrary"),
    )(q, k, v)


def _attn_bwd(q, k, v, o, lse, do, dk_prev, dv_prev, B):
    _, S, D3 = q.shape
    D = D3 // 3
    HP = D // LANES
    scale = HEAD_DIM ** -0.5
    n_blocks = S // ATTN_BLOCK
    has_prev = dk_prev is not None
    nt_dims = (((1,), (1,)), ((), ()))
    tn_dims = (((0,), (0,)), ((), ()))

    def branch(dil, q_ref, k_ref, v_ref, lse_ref, do_ref, dq_ref, dk_ref, dv_ref, delta, dk_p, dv_p):
        nb = (S // dil) // ATTN_BLOCK

        def blk(tb, _):
            n, rows, rowsp = _block_rows(tb, dil, nb)
            qb = q_ref[rows, :] * scale
            dob, lb, db = do_ref[rows, :], lse_ref[rows, :], delta[rows, :]
            kk = jnp.concatenate([k_ref[rowsp, :], k_ref[rows, :]], axis=0).astype(BF16)
            vv = jnp.concatenate([v_ref[rowsp, :], v_ref[rows, :]], axis=0).astype(BF16)
            ok = _attn_mask(n)
            dqs = []
            dkk = dvv = None
            for h in range(ATTN_HEADS):
                qh, doh = _only_head(qb, h), _only_head(dob, h)
                lh = lb[:, h * HEAD_DIM:h * HEAD_DIM + 1]
                dlt = db[:, h * HEAD_DIM:h * HEAD_DIM + 1]
                s = lax.dot_general(qh, kk, nt_dims, preferred_element_type=F32)
                p = jnp.where(ok, jnp.exp(s - lh), 0.0)
                dp = lax.dot_general(doh, vv, nt_dims, preferred_element_type=F32)
                ds = (p * (dp - dlt)).astype(BF16)
                dqs.append(jnp.dot(ds, kk, preferred_element_type=F32))
                dk_h = lax.dot_general(ds, qh, tn_dims, preferred_element_type=F32)
                dv_h = lax.dot_general(p.astype(BF16), doh, tn_dims, preferred_element_type=F32)
                dkk = dk_h if dkk is None else dkk + dk_h
                dvv = dv_h if dvv is None else dvv + dv_h
            dq_ref[rows, :] = _per_head(dqs) * scale
            dk_p[rowsp, :] = dkk[:ATTN_BLOCK]
            dv_p[rowsp, :] = dvv[:ATTN_BLOCK]
            dk_ref[rows, :] = dkk[ATTN_BLOCK:]
            dv_ref[rows, :] = dvv[ATTN_BLOCK:]
            return 0

        lax.fori_loop(0, n_blocks, blk, 0, unroll=ATTN_UNROLL)

    def body(*refs):
        q_ref, k_ref, v_ref, o_ref, lse_ref, do_ref = refs[:6]
        n_in = 8 if has_prev else 6
        dq_ref, dk_ref, dv_ref, delta, dk_p, dv_p = refs[n_in:n_in + 6]
        g = pl.program_id(2)

        @pl.when(g == 0)
        def _():
            def dl(i, _):
                rows = pl.ds(pl.multiple_of(i * ATTN_BLOCK, ATTN_BLOCK), ATTN_BLOCK)
                prod = do_ref[rows, :] * o_ref[rows, :]
                delta[rows, :] = _per_head([jnp.sum(jnp.where(_head_lanes(h), prod, 0.0), axis=-1, keepdims=True)
                                            for h in range(ATTN_HEADS)])
                return 0

            lax.fori_loop(0, n_blocks, dl, 0)

        dk_p[...] = jnp.zeros_like(dk_p)
        dv_p[...] = jnp.zeros_like(dv_p)
        for gi, dil in enumerate(DILATIONS):
            pl.when(g == gi)(functools.partial(branch, dil, q_ref, k_ref, v_ref, lse_ref, do_ref, dq_ref, dk_ref, dv_ref,
                                               delta, dk_p, dv_p))

        def fin(i, _):
            rows = pl.ds(pl.multiple_of(i * ATTN_BLOCK, ATTN_BLOCK), ATTN_BLOCK)
            dk_t = dk_ref[rows, :] + dk_p[rows, :]
            dv_t = dv_ref[rows, :] + dv_p[rows, :]
            if has_prev:
                dk_t = dk_t + refs[6][rows, :]
                dv_t = dv_t + refs[7][rows, :]
            dk_ref[rows, :] = dk_t
            dv_ref[rows, :] = dv_t
            return 0

        lax.fori_loop(0, n_blocks, fin, 0)

    br = pl.BlockSpec((None, S, LANES), lambda b, hp, g: (b, 0, g * HP + hp))
    hd = pl.BlockSpec((None, S, LANES), lambda b, hp, g: (b, 0, hp))
    ins = [q, k, v, o, lse, do] + ([dk_prev, dv_prev] if has_prev else [])
    return pl.pallas_call(
        body, name="attn_bwd", grid=(B, HP, len(DILATIONS)),
        in_specs=[br, br, br, hd, hd, hd] + ([br, br] if has_prev else []), out_specs=[br, br, br],
        out_shape=[jax.ShapeDtypeStruct(q.shape, F32)] * 3, scratch_shapes=[pltpu.VMEM((S, LANES), F32)] * 3,
        compiler_params=_params("parallel", "parallel", "arbitrary"),
    )(*ins)


def _adamw(w, grads, m, v):
    R, C = w.shape
    tr = _div(R, 256, SUBLANES)
    ng = len(grads)
    c1 = 1.0 - ADAM_B1 ** ADAM_STEP
    c2 = 1.0 - ADAM_B2 ** ADAM_STEP

    def body(*refs):
        w_ref, m_ref, v_ref = refs[0], refs[1 + ng], refs[2 + ng]
        d_ref, mo_ref, vo_ref = refs[3 + ng:6 + ng]
        g = refs[1][...]
        if ng == 2:
            g = g + refs[2][...]
            refs[6 + ng][...] = g
        mn = ADAM_B1 * m_ref[...] + (1.0 - ADAM_B1) * g
        vn = ADAM_B2 * v_ref[...] + (1.0 - ADAM_B2) * (g * g)
        d_ref[...] = -ADAM_LR * ((mn / c1) / (jnp.sqrt(vn / c2) + ADAM_EPS) + ADAM_WD * w_ref[...])
        mo_ref[...] = mn
        vo_ref[...] = vn

    blk = pl.BlockSpec((tr, C), lambda i: (i, 0))
    n_out = 3 + (ng == 2)
    outs = pl.pallas_call(
        body, name="adamw", grid=(R // tr,), in_specs=[blk] * (3 + ng), out_specs=[blk] * n_out,
        out_shape=[jax.ShapeDtypeStruct((R, C), F32)] * n_out, compiler_params=_params("parallel"),
    )(w, *grads, m, v)
    return (outs[3] if ng == 2 else grads[0],) + tuple(outs[:3])


def _sum_shards(recv):
    n, R, C = recv.shape
    tr = _div(R, 256, SUBLANES if recv.dtype == F32 else 2 * SUBLANES)

    def body(r_ref, o_ref):
        s = r_ref[0].astype(F32)
        for i in range(1, n):
            s = s + r_ref[i].astype(F32)
        o_ref[...] = s

    return pl.pallas_call(
        body, name="sum_shards", grid=(R // tr,), in_specs=[pl.BlockSpec((n, tr, C), lambda i: (0, i, 0))],
        out_specs=pl.BlockSpec((tr, C), lambda i: (i, 0)), out_shape=jax.ShapeDtypeStruct((R, C), F32),
        compiler_params=_params("parallel"),
    )(recv)


N_DEV = 8
N_CHIPS = 4


def _coords():
    return lax.axis_index("x"), lax.axis_index("y"), lax.axis_index("c")


def _all_gather_small(x):
    m_per, n = x.shape

    def body(x_ref, out_ref, send_sems, recv_sems, local_sem):
        cx, cy, cc = _coords()
        me, sibling = (cx, cy, cc), (cx, cy, 1 - cc)
        chips = [(1 - cx, cy), (cx, 1 - cy), (1 - cx, 1 - cy)]

        def rows(px, py, pc):
            return out_ref.at[pl.ds((4 * px + 2 * py + pc) * m_per, m_per), :]

        def copy(k, block, to, src=None):
            return pltpu.make_async_remote_copy(
                src_ref=rows(*block) if src is None else src, dst_ref=rows(*block), send_sem=send_sems.at[k],
                recv_sem=recv_sems.at[k], device_id=to, device_id_type=MESH)

        mine = pltpu.make_async_copy(x_ref, rows(*me), local_sem)
        mine.start()
        first = [copy(0, me, sibling, src=x_ref)]
        first += [copy(1 + j, me, (*chip, cc), src=x_ref) for j, chip in enumerate(chips)]
        for cp in first:
            cp.start()
        passed = [copy(4 + j, (*chip, cc), sibling) for j, chip in enumerate(chips)]
        for j, chip in enumerate(chips):
            copy(1 + j, (*chip, cc), me).wait_recv()
            passed[j].start()
        copy(0, sibling, me).wait_recv()
        for j, chip in enumerate(chips):
            copy(4 + j, (*chip, 1 - cc), me).wait_recv()
        for cp in first + passed:
            cp.wait_send()
        mine.wait()

    return pl.pallas_call(
        body, name="all_gather_small", out_shape=jax.ShapeDtypeStruct((N_DEV * m_per, n), x.dtype),
        in_specs=[pl.BlockSpec(memory_space=pltpu.VMEM)], out_specs=pl.BlockSpec(memory_space=pltpu.VMEM),
        scratch_shapes=[pltpu.SemaphoreType.DMA((7,)), pltpu.SemaphoreType.DMA((7,)), pltpu.SemaphoreType.DMA],
        compiler_params=pltpu.CompilerParams(vmem_limit_bytes=VMEM_LIMIT),
    )(x)


def _hbm_exchange(name, ins, out_shapes, plan):
    n_in = len(ins)
    probe = plan([None] * n_in, [None] * len(out_shapes), None, count_only=True)
    n_remote, n_local = probe

    def body(*refs):
        in_refs, out_refs = refs[:n_in], refs[n_in:n_in + len(out_shapes)]
        send_sems, recv_sems, local_sems = refs[n_in + len(out_shapes):]
        remote, local = plan(in_refs, out_refs, _coords())
        locals_ = [pltpu.make_async_copy(s, d, local_sems.at[i]) for i, (s, d) in enumerate(local)]
        for cp in locals_:
            cp.start()
        remotes = [pltpu.make_async_remote_copy(src_ref=s, dst_ref=d, send_sem=send_sems.at[i], recv_sem=recv_sems.at[i],
                                                device_id=peer, device_id_type=MESH)
                   for i, (s, d, peer) in enumerate(remote)]
        for cp in remotes:
            cp.start()
        for cp in remotes:
            cp.wait_send()
            cp.wait_recv()
        for cp in locals_:
            cp.wait()

    anyspec = pl.BlockSpec(memory_space=pl.ANY)
    return pl.pallas_call(
        body, name=name, out_shape=out_shapes, in_specs=[anyspec] * n_in, out_specs=[anyspec] * len(out_shapes),
        scratch_shapes=[pltpu.SemaphoreType.DMA((n_remote,)), pltpu.SemaphoreType.DMA((n_remote,)),
                        pltpu.SemaphoreType.DMA((max(n_local, 1),))],
    )(*ins)


def _other_chips(cx, cy):
    return [(1 - cx, cy), (cx, 1 - cy), (1 - cx, 1 - cy)]


def _gather_weights(shards):
    n_w = len(shards)
    n_peer = N_CHIPS - 1
    outs = [jax.ShapeDtypeStruct((N_CHIPS,) + s.shape, s.dtype) for s in shards]

    def body(*refs):
        in_refs, out_refs = refs[:n_w], refs[n_w:2 * n_w]
        send_sems, recv_sems, local_sems = refs[2 * n_w:]
        cx, cy, cc = _coords()
        mine = 2 * cx + cy
        sibling = (cx, cy, 1 - cc)
        chips = _other_chips(cx, cy)

        def half(ref, c):
            rows = ref.shape[1] // 2
            return ref.at[:, pl.ds(c * rows, rows), :]

        def copy(k, src, dst, to):
            return pltpu.make_async_remote_copy(src_ref=src, dst_ref=dst, send_sem=send_sems.at[k], recv_sem=recv_sems.at[k],
                                                device_id=to, device_id_type=MESH)

        locals_ = [pltpu.make_async_copy(s, o.at[mine], local_sems.at[w]) for w, (s, o) in enumerate(zip(in_refs, out_refs))]
        for cp in locals_:
            cp.start()
        over_ici, passed, from_ici, from_sibling = [], [], [], []
        for w, (s, o) in enumerate(zip(in_refs, out_refs)):
            for j, (px, py) in enumerate(chips):
                k = w * n_peer + j
                theirs = 2 * px + py
                over_ici.append(copy(k, half(s, cc), half(o.at[mine], cc), (px, py, cc)))
                from_ici.append(copy(k, half(s, cc), half(o.at[theirs], cc), (px, py, cc)))
                passed.append(copy(n_w * n_peer + k, half(o.at[theirs], cc), half(o.at[theirs], cc), sibling))
                from_sibling.append(copy(n_w * n_peer + k, half(s, 1 - cc), half(o.at[theirs], 1 - cc), sibling))
        for cp in over_ici:
            cp.start()
        for arrived, onward in zip(from_ici, passed):
            arrived.wait_recv()
            onward.start()
        for cp in from_sibling:
            cp.wait_recv()
        for cp in over_ici + passed:
            cp.wait_send()
        for cp in locals_:
            cp.wait()

    anyspec = pl.BlockSpec(memory_space=pl.ANY)
    n_remote = 2 * n_w * n_peer
    return pl.pallas_call(
        body, name="gather_weights", out_shape=outs, in_specs=[anyspec] * n_w, out_specs=[anyspec] * n_w,
        scratch_shapes=[pltpu.SemaphoreType.DMA((n_remote,)), pltpu.SemaphoreType.DMA((n_remote,)),
                        pltpu.SemaphoreType.DMA((n_w,))],
    )(*shards)


def _scatter_grads(grads):
    outs = [jax.ShapeDtypeStruct(g.shape, g.dtype) for g in grads]

    def plan(in_refs, out_refs, me, count_only=False):
        if count_only:
            return 3 * len(in_refs), len(in_refs)
        cx, cy, cc = me
        mine = 2 * cx + cy
        remote = [(g.at[2 * px + py], o.at[mine], (px, py, cc))
                  for g, o in zip(in_refs, out_refs) for px, py in _other_chips(cx, cy)]
        local = [(g.at[mine], o.at[mine]) for g, o in zip(in_refs, out_refs)]
        return remote, local

    return _hbm_exchange("scatter_grads", grads, outs, plan)


def _swap_with_sibling(sums):
    outs = [jax.ShapeDtypeStruct(s.shape, s.dtype) for s in sums]

    def plan(in_refs, out_refs, me, count_only=False):
        if count_only:
            return len(in_refs), 0
        cx, cy, cc = me
        return [(s, o, (cx, cy, 1 - cc)) for s, o in zip(in_refs, out_refs)], []

    return _hbm_exchange("swap_with_sibling", sums, outs, plan)


def _pack(arrs, width):
    parts, layout, row = [], [], 0
    for a in arrs:
        flat = a.reshape(-1).astype(F32)
        rows = -(-flat.shape[0] // width)
        parts.append(jnp.pad(flat, (0, rows * width - flat.shape[0])).reshape(rows, width))
        layout.append((row, rows, a.shape))
        row += rows
    pad = -row % SUBLANES
    if pad:
        parts.append(jnp.zeros((pad, width), F32))
    return jnp.concatenate(parts, axis=0), layout, row + pad


def _unpack(buf, layout, idx):
    row, rows, shape = layout[idx]
    size = math.prod(shape)
    return buf[row:row + rows].reshape(-1)[:size].reshape(shape)


def kernel(x, c, ln_g, ada_w, ada_b, ssm_lam_re, ssm_lam_im, ssm_log_dt, ssm_b_re, ssm_b_im, ssm_c_re, ssm_c_im, ssm_d, ssm_w_glu, kv_g, kv_ada_w, kv_ada_b, w_kv, attn_w_q, attn_w_o, mlp_w1, mlp_w2, final_g, loss_target, m_ln_g, m_ada_w, m_ada_b, m_ssm_lam_re, m_ssm_lam_im, m_ssm_log_dt, m_ssm_b_re, m_ssm_b_im, m_ssm_c_re, m_ssm_c_im, m_ssm_d, m_ssm_w_glu, m_kv_g, m_kv_ada_w, m_kv_ada_b, m_w_kv, m_attn_w_q, m_attn_w_o, m_mlp_w1, m_mlp_w2, m_final_g, v_ln_g, v_ada_w, v_ada_b, v_ssm_lam_re, v_ssm_lam_im, v_ssm_log_dt, v_ssm_b_re, v_ssm_b_im, v_ssm_c_re, v_ssm_c_im, v_ssm_d, v_ssm_w_glu, v_kv_g, v_kv_ada_w, v_kv_ada_b, v_w_kv, v_attn_w_q, v_attn_w_o, v_mlp_w1, v_mlp_w2, v_final_g):
    B, S, D = x.shape
    N = B * S
    depth = ln_g.shape[0]
    n_a = ssm_w_glu.shape[0]
    n_b = attn_w_q.shape[0]
    FF = mlp_w1.shape[2] * N_CHIPS
    cx, cy, cc = _coords()
    chip = 2 * cx + cy
    dev = 4 * cx + 2 * cy + cc
    n_ex = N_DEV * B
    ada_cols = ada_w.shape[-1]
    kv_cols = kv_ada_w.shape[-1]

    wg_glu, wg_kv, wg_q, wg_o, wg_1, wg_2 = _gather_weights([
        ssm_w_glu.astype(BF16), w_kv.astype(BF16)[None], attn_w_q.astype(BF16), attn_w_o.astype(BF16),
        mlp_w1.astype(BF16), mlp_w2.astype(BF16)])

    c_pack, c_layout, _ = _pack([c], D)
    c_all_buf = _all_gather_small(c_pack)
    c_rows = c_pack.shape[0]
    c_all = jnp.concatenate([_unpack(c_all_buf[d * c_rows:(d + 1) * c_rows], c_layout, 0) for d in range(N_DEV)], axis=0)
    sc_all = jax.nn.silu(c_all).astype(BF16)
    n_mod = depth * 2
    ada_w8 = ada_w.reshape(n_mod, 1, D, ada_cols)
    ada_b_row = ada_b.reshape(1, n_mod * ada_cols)
    mod_local = _mm("ada_fwd", sc_all, ada_w8, mode="nn", M=n_ex, N=n_mod * ada_cols, K=D, b_lay="cs", b_ns=n_mod,
                    epi=_add, extras=[("n", ada_b_row)])
    kv_ada_b_local = lax.dynamic_slice(kv_ada_b.reshape(N_CHIPS, kv_cols), (chip, 0), (1, kv_cols))
    kvmod_local = _mm("ada_fwd", sc_all, _as4(kv_ada_w), mode="nn", M=n_ex, N=kv_cols, K=D, epi=_add,
                      extras=[("n", kv_ada_b_local)])
    mod_pack, mod_layout, mod_rows = _pack([mod_local, kvmod_local, ln_g, ssm_d], D)
    mod_buf = _all_gather_small(mod_pack)

    def from_chip(j, idx):
        d = 2 * j
        return _unpack(mod_buf[d * mod_rows:(d + 1) * mod_rows], mod_layout, idx)

    my_rows = lambda a: lax.dynamic_slice_in_dim(a, dev * B, B, axis=0)
    mods = jnp.concatenate([my_rows(from_chip(j, 0)).reshape(B, n_mod, ada_cols) for j in range(N_CHIPS)], axis=2)
    kvmod = jnp.concatenate([my_rows(from_chip(j, 1)) for j in range(N_CHIPS)], axis=1)
    ln_g_full = jnp.concatenate([from_chip(j, 2) for j in range(N_CHIPS)], axis=2)
    ssm_d_full = jnp.concatenate([from_chip(j, 3) for j in range(N_CHIPS)], axis=1)

    def mod3(l, s):
        mrow = mods[:, l * 2 + s]
        return [mrow[:, i * D:(i + 1) * D].reshape(B, 1, D) for i in range(3)]

    kv_shift, kv_scale = kvmod[:, :D].reshape(B, 1, D), kvmod[:, D:].reshape(B, 1, D)

    s5_tabs = []
    for l in range(n_a):
        prm = (ssm_lam_re[l], ssm_lam_im[l], ssm_log_dt[l], ssm_b_re[l], ssm_b_im[l], ssm_c_re[l], ssm_c_im[l])
        (bd, cd, _, _), disc_vjp = jax.vjp(_s5_discretize, *prm)
        coef_f, coef_b = _s5_scan_coefs(ssm_lam_re[l], ssm_lam_im[l], ssm_log_dt[l])
        s5_tabs.append((bd.astype(BF16), cd.astype(BF16), coef_f, coef_b, disc_vjp))

    h = x.reshape(N, D)
    saved = []
    k_all = v_all = None
    for l in range(depth):
        sv = {}
        shift, scale, gate = mod3(l, 0)
        sv["h0"], sv["scale0"], sv["gate0"] = h, scale, gate
        u = _normmod(h, ln_g_full[l, 0].reshape(1, D), scale, shift, B)
        sv["u0"] = u
        if l < n_a:
            bd, cd, coef_f, coef_b, _ = s5_tabs[l]
            z, carries = _s5_fwd(u, bd, cd, coef_f, ssm_d_full[l].reshape(1, D), B)
            zz = _mm("glu_proj", z, wg_glu, mode="nn", M=N, N=2 * D, K=D, b_lay="cs", b_l=l, b_ns=N_CHIPS)
            y, h_next = _glu_residual(zz, h, gate, B)
            sv["z"], sv["carries"], sv["zz"] = z, carries, zz
        else:
            j = l - n_a
            q = _mm("q_proj", u, wg_q, mode="nn", M=N, N=3 * D, K=D, b_lay="cs", b_l=j, b_ns=N_CHIPS)
            q3 = q.reshape(B, S, 3 * D)
            o, lse = _attn_fwd(q3, k_all, v_all, B)
            o2 = o.reshape(N, D)
            y, h_next = _mm("o_proj", o2, wg_o, mode="nn", M=N, N=D, K=D, b_lay="rs", b_l=j, b_ns=N_CHIPS,
                            out_dtype=(BF16, F32), epi=_gated_residual, extras=[("mn", h), ("ex", gate)], rows_per_ex=S)
            sv["q"], sv["o"], sv["lse"] = q3, o, lse
        sv["y0"] = y
        h = h_next
        shift, scale, gate = mod3(l, 1)
        sv["h1"], sv["scale1"], sv["gate1"] = h, scale, gate
        u = _normmod(h, ln_g_full[l, 1].reshape(1, D), scale, shift, B)
        r = _mm("mlp_up", u, wg_1, mode="nn", M=N, N=FF, K=D, b_lay="cs", b_l=l, b_ns=N_CHIPS, out_dtype=BF16, epi=_relu2)
        y, h = _mm("mlp_down", r, wg_2, mode="nn", M=N, N=D, K=FF, b_lay="rs", b_l=l, b_ns=N_CHIPS,
                   out_dtype=(BF16, F32), epi=_gated_residual, extras=[("mn", h), ("ex", gate)], rows_per_ex=S)
        sv["u1"], sv["r"], sv["y1"] = u, r, y
        saved.append(sv)
        if l == n_a - 1:
            h_kv = h
            u_kv = _normmod(h, kv_g.reshape(1, D), kv_scale, kv_shift, B)
            half = N_CHIPS // 2
            k_all = _mm("kv_proj", u_kv, wg_kv, mode="nn", M=N, N=3 * D, K=D, b_lay="cs", b_s0=0, b_ns=half).reshape(B, S, 3 * D)
            v_all = _mm("kv_proj", u_kv, wg_kv, mode="nn", M=N, N=3 * D, K=D, b_lay="cs", b_s0=half, b_ns=half).reshape(B, S, 3 * D)

    loss_buf, dh, d_final_g = _loss_head(h, final_g.reshape(1, D), loss_target.reshape(N, D))
    loss = lax.psum(loss_buf[0, 0], ("x", "y", "c"))

    def grads_like(wg):
        return jnp.zeros(wg.shape, BF16)

    dg_glu, dg_kv, dg_q, dg_o, dg_1, dg_2 = (grads_like(w) for w in (wg_glu, wg_kv, wg_q, wg_o, wg_1, wg_2))
    d_ln_g = [[None, None] for _ in range(depth)]
    d_mods = [[None, None] for _ in range(depth)]
    d_s5 = [None] * n_a
    dk_acc = dv_acc = None
    half = N_CHIPS // 2

    def tn_grad(name, a, d, into, l, Mr, Nc, lay, s0=0, ns=N_CHIPS):
        return _mm(name, a, _as4(d), mode="tn", M=Mr, N=Nc, K=N, b_lay="cs", out_dtype=BF16, out_lay=lay,
                   out4_shape=into.shape, out_into=into, out_l=l, out_s0=s0, out_ns=ns)

    for l in reversed(range(depth)):
        sv = saved[l]
        dy, d_gate1 = _residual_bwd(dh, sv["gate1"], sv["y1"], B)
        dg_2 = tn_grad("mlp_down_dw", sv["r"], dy, dg_2, l, FF, D, "rs")
        da = _mm("mlp_down_dx", dy, wg_2, mode="nt", M=N, N=FF, K=D, b_lay="rs", b_l=l, b_ns=N_CHIPS, out_dtype=BF16,
                 epi=_relu2_bwd, extras=[("mn", sv["r"])])
        dg_1 = tn_grad("mlp_up_dw", sv["u1"], da, dg_1, l, D, FF, "cs")
        du = _mm("mlp_up_dx", da, wg_1, mode="nt", M=N, N=D, K=FF, b_lay="cs", b_l=l, b_ns=N_CHIPS)
        dh, dgv, d_scale1, d_shift1 = _normmod_bwd(du, sv["h1"], ln_g_full[l, 1].reshape(1, D), sv["scale1"], dh, B)
        d_ln_g[l][1] = dgv
        d_mods[l][1] = jnp.concatenate([d_shift1, d_scale1, d_gate1], axis=2)
        dy, d_gate0 = _residual_bwd(dh, sv["gate0"], sv["y0"], B)
        if l < n_a:
            bd, cd, coef_f, coef_b, disc_vjp = s5_tabs[l]
            dzz = _glu_bwd(dy, sv["zz"])
            dg_glu = tn_grad("glu_proj_dw", sv["z"], dzz, dg_glu, l, D, 2 * D, "cs")
            dz = _mm("glu_proj_dx", dzz, wg_glu, mode="nt", M=N, N=D, K=2 * D, b_lay="cs", b_l=l, b_ns=N_CHIPS)
            du, d_bd, d_cd, d_a2, d_dskip = _s5_bwd(sv["u0"], dz, bd, cd, coef_f, coef_b, ssm_d_full[l].reshape(1, D),
                                                    sv["carries"], B)
            d_are = (d_a2[:, 0, :CHUNK_STATE] + d_a2[:, 0, CHUNK_STATE:]).reshape(-1, SSM_STATE)
            d_aim = (d_a2[:, 1, CHUNK_STATE:] - d_a2[:, 1, :CHUNK_STATE]).reshape(-1, SSM_STATE)
            d_s5[l] = disc_vjp((d_bd, d_cd, d_are, d_aim)) + (d_dskip,)
        else:
            j = l - n_a
            dg_o = tn_grad("o_proj_dw", sv["o"].reshape(N, D), dy, dg_o, j, D, D, "rs")
            do = _mm("o_proj_dx", dy, wg_o, mode="nt", M=N, N=D, K=D, b_lay="rs", b_l=j, b_ns=N_CHIPS)
            dq, dk_acc, dv_acc = _attn_bwd(sv["q"], k_all, v_all, sv["o"], sv["lse"], do.reshape(B, S, D), dk_acc, dv_acc, B)
            dq2 = dq.reshape(N, 3 * D)
            dg_q = tn_grad("q_proj_dw", sv["u0"], dq2, dg_q, j, D, 3 * D, "cs")
            du = _mm("q_proj_dx", dq2, wg_q, mode="nt", M=N, N=D, K=3 * D, b_lay="cs", b_l=j, b_ns=N_CHIPS)
        dh, dgv, d_scale0, d_shift0 = _normmod_bwd(du, sv["h0"], ln_g_full[l, 0].reshape(1, D), sv["scale0"], dh, B)
        d_ln_g[l][0] = dgv
        d_mods[l][0] = jnp.concatenate([d_shift0, d_scale0, d_gate0], axis=2)
        if l == n_a:
            dk2, dv2 = dk_acc.reshape(N, 3 * D), dv_acc.reshape(N, 3 * D)
            dg_kv = tn_grad("kv_proj_dw", u_kv, dk2, dg_kv, 0, D, 3 * D, "cs", s0=0, ns=half)
            dg_kv = tn_grad("kv_proj_dw", u_kv, dv2, dg_kv, 0, D, 3 * D, "cs", s0=half, ns=half)
            du_kv = _mm("kv_proj_dx", dk2, wg_kv, mode="nt", M=N, N=D, K=3 * D, b_lay="cs", b_s0=0, b_ns=half)
            du_kv = _mm("kv_proj_dx", dv2, wg_kv, mode="nt", M=N, N=D, K=3 * D, b_lay="cs", b_s0=half, b_ns=half,
                        epi=_add, extras=[("mn", du_kv)])
            dh, d_kv_g, d_kv_scale, d_kv_shift = _normmod_bwd(du_kv, h_kv, kv_g.reshape(1, D), kv_scale, dh, B)
    grad_x = dh.reshape(B, S, D)

    recv = _scatter_grads([dg_glu, dg_kv, dg_q, dg_o, dg_1, dg_2])
    own = [_sum_shards(r.reshape(N_CHIPS, -1, r.shape[-1])) for r in recv]
    other = _swap_with_sibling(own)

    d_kvmod = jnp.concatenate([d_kv_shift, d_kv_scale], axis=2).reshape(B, 2 * D)
    d_mod_all = jnp.concatenate([d_mods[l][s].reshape(B, 3 * D) for l in range(depth) for s in range(2)], axis=1)
    small = [
        d_mod_all, d_kvmod,
        jnp.stack([jnp.stack([d_ln_g[l][0].reshape(D), d_ln_g[l][1].reshape(D)]) for l in range(depth)]),
        jnp.stack([d_s5[l][0] for l in range(n_a)]), jnp.stack([d_s5[l][1] for l in range(n_a)]),
        jnp.stack([d_s5[l][2] for l in range(n_a)]),
        jnp.stack([d_s5[l][3] for l in range(n_a)]), jnp.stack([d_s5[l][4] for l in range(n_a)]),
        jnp.stack([d_s5[l][5] for l in range(n_a)]), jnp.stack([d_s5[l][6] for l in range(n_a)]),
        jnp.stack([d_s5[l][7].reshape(D) for l in range(n_a)]),
        d_kv_g.reshape(D), d_final_g.reshape(D),
    ]
    small_pack, small_layout, small_rows = _pack(small, D)
    small_buf = _all_gather_small(small_pack)
    small_sum = _sum_shards(small_buf.reshape(N_DEV, small_rows, D))
    red = lambda idx: _unpack(small_sum, small_layout, idx)
    per_dev = lambda idx: jnp.concatenate(
        [_unpack(small_buf[d * small_rows:(d + 1) * small_rows], small_layout, idx) for d in range(N_DEV)], axis=0)

    dm_all = per_dev(0).reshape(n_ex, n_mod, 3 * D)
    dm_cols = lax.dynamic_slice_in_dim(dm_all, chip * ada_cols, ada_cols, axis=2).reshape(n_ex, n_mod * ada_cols)
    g_ada_w = _mm("ada_dw", sc_all, _as4(dm_cols), mode="tn", M=D, N=n_mod * ada_cols, K=n_ex, b_lay="cs",
                  out_lay="cs", out4_shape=(n_mod, 1, D, ada_cols), out_ns=n_mod).reshape(ada_w.shape)
    dkvm_all = per_dev(1)
    dkvm_cols = lax.dynamic_slice_in_dim(dkvm_all, chip * kv_cols, kv_cols, axis=1)
    g_kv_ada_w = _mm("ada_dw", sc_all, _as4(dkvm_cols), mode="tn", M=D, N=kv_cols, K=n_ex, b_lay="cs")
    g_ada_b_full = (red(0)[0] + red(0)[1]).reshape(depth, 2, 3 * D) if B == 2 else jnp.sum(red(0), axis=0).reshape(depth, 2, 3 * D)
    g_ada_b = lax.dynamic_slice_in_dim(g_ada_b_full, chip * ada_cols, ada_cols, axis=2)
    g_kv_ada_b = red(1)[0] + red(1)[1] if B == 2 else jnp.sum(red(1), axis=0)
    g_ln_g = lax.dynamic_slice_in_dim(red(2), chip * (D // N_CHIPS), D // N_CHIPS, axis=2)
    g_ssm_d = lax.dynamic_slice_in_dim(red(10), chip * (D // N_CHIPS), D // N_CHIPS, axis=1)
    small_grads = {
        "ln_g": g_ln_g, "ada_b": g_ada_b, "ssm_lam_re": red(3), "ssm_lam_im": red(4), "ssm_log_dt": red(5),
        "ssm_b_re": red(6), "ssm_b_im": red(7), "ssm_c_re": red(8), "ssm_c_im": red(9), "ssm_d": g_ssm_d,
        "kv_g": red(11), "kv_ada_b": g_kv_ada_b, "final_g": red(12),
    }
    small_w = {"ln_g": (ln_g, m_ln_g, v_ln_g), "ada_b": (ada_b, m_ada_b, v_ada_b),
               "ssm_lam_re": (ssm_lam_re, m_ssm_lam_re, v_ssm_lam_re), "ssm_lam_im": (ssm_lam_im, m_ssm_lam_im, v_ssm_lam_im),
               "ssm_log_dt": (ssm_log_dt, m_ssm_log_dt, v_ssm_log_dt), "ssm_b_re": (ssm_b_re, m_ssm_b_re, v_ssm_b_re),
               "ssm_b_im": (ssm_b_im, m_ssm_b_im, v_ssm_b_im), "ssm_c_re": (ssm_c_re, m_ssm_c_re, v_ssm_c_re),
               "ssm_c_im": (ssm_c_im, m_ssm_c_im, v_ssm_c_im), "ssm_d": (ssm_d, m_ssm_d, v_ssm_d),
               "kv_g": (kv_g, m_kv_g, v_kv_g), "kv_ada_b": (kv_ada_b, m_kv_ada_b, v_kv_ada_b),
               "final_g": (final_g, m_final_g, v_final_g)}
    names = list(small_w)
    wp, lay_w, _ = _pack([small_w[n][0] for n in names], D)
    gp, _, _ = _pack([small_grads[n] for n in names], D)
    mp, _, _ = _pack([small_w[n][1] for n in names], D)
    vp, _, _ = _pack([small_w[n][2] for n in names], D)
    _, d_p, m_p, v_p = _adamw(wp, [gp], mp, vp)
    upd = {n: (small_grads[n].reshape(small_w[n][0].shape), _unpack(d_p, lay_w, i), _unpack(m_p, lay_w, i), _unpack(v_p, lay_w, i))
           for i, n in enumerate(names)}

    def big(w, m, v, g_own, g_other=None):
        C = w.shape[-1]
        gs = [g_own.reshape(-1, C)] + ([g_other.reshape(-1, C)] if g_other is not None else [])
        return tuple(t.reshape(w.shape) for t in _adamw(w.reshape(-1, C), gs, m.reshape(-1, C), v.reshape(-1, C)))

    upd["ssm_w_glu"] = big(ssm_w_glu, m_ssm_w_glu, v_ssm_w_glu, own[0], other[0])
    upd["w_kv"] = big(w_kv, m_w_kv, v_w_kv, own[1], other[1])
    upd["attn_w_q"] = big(attn_w_q, m_attn_w_q, v_attn_w_q, own[2], other[2])
    upd["attn_w_o"] = big(attn_w_o, m_attn_w_o, v_attn_w_o, own[3], other[3])
    upd["mlp_w1"] = big(mlp_w1, m_mlp_w1, v_mlp_w1, own[4], other[4])
    upd["mlp_w2"] = big(mlp_w2, m_mlp_w2, v_mlp_w2, own[5], other[5])
    upd["ada_w"] = big(ada_w, m_ada_w, v_ada_w, g_ada_w)
    upd["kv_ada_w"] = big(kv_ada_w, m_kv_ada_w, v_kv_ada_w, g_kv_ada_w)

    order = ["ln_g", "ada_w", "ada_b", "ssm_lam_re", "ssm_lam_im", "ssm_log_dt", "ssm_b_re", "ssm_b_im", "ssm_c_re",
             "ssm_c_im", "ssm_d", "ssm_w_glu", "kv_g", "kv_ada_w", "kv_ada_b", "w_kv", "attn_w_q", "attn_w_o", "mlp_w1",
             "mlp_w2", "final_g"]
    return (loss, grad_x, *[upd[n][0] for n in order], *[upd[n][1] for n in order], *[upd[n][2] for n in order],
            *[upd[n][3] for n in order])
```

```python
import functools
import math

import jax
import jax.numpy as jnp
from jax import lax
from jax.experimental import pallas as pl
from jax.experimental.pallas import tpu as pltpu

F32 = jnp.float32
BF16 = jnp.bfloat16
MESH = pl.DeviceIdType.MESH

EPS = 1e-6
NEG = -1e30
SSM_GROUP = 16
SSM_STATE = 64
HEAD_DIM = 64
ATTN_BLOCK = 128
DILATIONS = (1, 4, 16)
ADAM_LR, ADAM_B1, ADAM_B2, ADAM_EPS, ADAM_WD, ADAM_STEP = 0.001, 0.9, 0.999, 1e-08, 0.01, 10

LANES = 128
SUBLANES = 8
CHUNK_GROUPS = LANES // SSM_GROUP
CHUNK_STATE = CHUNK_GROUPS * SSM_STATE
VMEM_LIMIT = 56 * 1024 * 1024


def _div(dim, pref, mult):
    t = min(pref, dim) // mult * mult
    while t >= mult:
        if dim % t == 0:
            return t
        t -= mult
    return dim


def _params(*sem):
    return pltpu.CompilerParams(dimension_semantics=sem, vmem_limit_bytes=VMEM_LIMIT)


def _coords():
    return lax.axis_index("x"), lax.axis_index("y"), lax.axis_index("c")


def _other_chips(cx, cy):
    return [(1 - cx, cy), (cx, 1 - cy), (1 - cx, 1 - cy)]


class _Carry:
    def __init__(self, srcs, dsts, plan, n_remote, n_local, onward=None, n_onward=0):
        self.srcs, self.dsts, self.plan, self.n_remote, self.n_local = list(srcs), list(dsts), plan, n_remote, n_local
        self.onward, self.n_onward = onward, n_onward


def _carried_call(body, *, name, grid, in_specs, out_specs, out_shape, scratch_shapes, operands, sem, carry=None):
    if carry is None:
        outs = pl.pallas_call(body, name=name, grid=grid, in_specs=in_specs, out_specs=out_specs, out_shape=out_shape,
                              scratch_shapes=scratch_shapes, compiler_params=_params(*sem))(*operands)
        return list(outs), []
    n_in, n_out, n_scr = len(in_specs), len(out_specs), len(scratch_shapes)
    ns, nd = len(carry.srcs), len(carry.dsts)

    def wrapped(*refs):
        base_in, src_refs = refs[:n_in], refs[n_in:n_in + ns]
        o0 = n_in + ns + nd
        base_out, dst_refs = refs[o0:o0 + n_out], refs[o0 + n_out:o0 + n_out + nd]
        s0 = o0 + n_out + nd
        base_scr = refs[s0:s0 + n_scr]
        send_sems, recv_sems, local_sems = refs[s0 + n_scr:]
        pids = [pl.program_id(a) for a in range(len(grid))]
        first = functools.reduce(jnp.logical_and, [p == 0 for p in pids])
        last = functools.reduce(jnp.logical_and, [p == g - 1 for p, g in zip(pids, grid)])

        def remote_copies(moves, k0):
            return [pltpu.make_async_remote_copy(src_ref=s, dst_ref=d, send_sem=send_sems.at[k0 + i], recv_sem=recv_sems.at[k0 + i],
                                                 device_id=peer, device_id_type=MESH) for i, (s, d, peer) in enumerate(moves)]

        def copies():
            remote, local = carry.plan(src_refs, dst_refs, _coords())
            return remote_copies(remote, 0), [pltpu.make_async_copy(s, d, local_sems.at[i]) for i, (s, d) in enumerate(local)]

        @pl.when(first)
        def _():
            remote, local = copies()
            for cp in local + remote:
                cp.start()

        body(*base_in, *base_out, *base_scr)

        @pl.when(last)
        def _():
            remote, local = copies()
            for cp in remote:
                cp.wait_send()
                cp.wait_recv()
            for cp in local:
                cp.wait()
            if carry.onward is not None:
                second = remote_copies(carry.onward(src_refs, dst_refs, _coords()), carry.n_remote)
                for cp in second:
                    cp.start()
                for cp in second:
                    cp.wait_send()
                    cp.wait_recv()

    anyspec = pl.BlockSpec(memory_space=pl.ANY)
    outs = pl.pallas_call(
        wrapped, name=name, grid=grid, in_specs=list(in_specs) + [anyspec] * (ns + nd),
        out_specs=list(out_specs) + [anyspec] * nd,
        out_shape=list(out_shape) + [jax.ShapeDtypeStruct(d.shape, d.dtype) for d in carry.dsts],
        scratch_shapes=list(scratch_shapes) + [pltpu.SemaphoreType.DMA((carry.n_remote + carry.n_onward,)),
                                               pltpu.SemaphoreType.DMA((carry.n_remote + carry.n_onward,)),
                                               pltpu.SemaphoreType.DMA((max(carry.n_local, 1),))],
        input_output_aliases={n_in + ns + i: n_out + i for i in range(nd)},
        compiler_params=_params(*(["arbitrary"] * len(grid))),
    )(*operands, *carry.srcs, *carry.dsts)
    return list(outs[:n_out]), list(outs[n_out:])


def _mm(name, a, b4, *, mode, M, N, K, b_lay="cs", b_l=0, b_s0=0, b_ns=1, out_dtype=F32, out_lay=None, out4_shape=None,
        out_into=None, out_l=0, out_s0=0, out_ns=1, epi=None, extras=(), rows_per_ex=None, tm=1024, tn=1024, tk=1024):
    _, _, bR, bC = b4.shape
    tm = _div(M, tm, SUBLANES if M % 16 else 16)
    brows, bcols = (N, K) if mode == "nt" else (K, N)
    if b_lay == "cs":
        assert bR == brows and bC * b_ns == bcols, (name, b4.shape, brows, bcols)
    else:
        assert bC == bcols and bR * b_ns == brows, (name, b4.shape, brows, bcols)
    n_lim = N
    k_lim = K
    if mode == "nt":
        if b_lay == "cs":
            k_lim = bC
        else:
            n_lim = bR
    else:
        if b_lay == "cs":
            n_lim = bC
        else:
            k_lim = bR
    if out_lay == "cs":
        oR, oC = out4_shape[2], out4_shape[3]
        assert oR == M and oC * out_ns == N, (name, out4_shape, M, N)
        n_lim = math.gcd(n_lim, oC)
    elif out_lay == "rs":
        oR, oC = out4_shape[2], out4_shape[3]
        assert oC == N and oR * out_ns == M, (name, out4_shape, M, N)
        tm = _div(oR, tm, SUBLANES)
    tn = _div(n_lim, tn, LANES)
    tk = _div(k_lim, tk, LANES if mode != "tn" else SUBLANES)
    if mode == "tn":
        tk = _div(k_lim, tk, 16) if k_lim % 16 == 0 else tk
    nk = K // tk
    grid = (M // tm, N // tn, nk)

    if mode == "tn":
        a_spec = pl.BlockSpec((tk, tm), lambda i, j, k: (k, i))
    else:
        a_spec = pl.BlockSpec((tm, tk), lambda i, j, k: (i, k))

    def b_index(ri, ci, br, bc):
        if b_lay == "cs":
            per = bC // bc
            return (b_s0 + ci // per, b_l, ri, ci % per)
        per = bR // br
        return (b_s0 + ri // per, b_l, ri % per, ci)

    if mode == "nt":
        b_spec = pl.BlockSpec((None, None, tn, tk), lambda i, j, k: b_index(j, k, tn, tk))
    else:
        b_spec = pl.BlockSpec((None, None, tk, tn), lambda i, j, k: b_index(k, j, tk, tn))

    in_specs = [a_spec, b_spec]
    operands = [a, b4]
    for kind, arr in extras:
        if kind == "mn":
            in_specs.append(pl.BlockSpec((tm, tn), lambda i, j, k: (i, j)))
        elif kind == "ex":
            per_ex = rows_per_ex // tm
            in_specs.append(pl.BlockSpec((None, 1, tn), lambda i, j, k: (i // per_ex, 0, j)))
        else:
            in_specs.append(pl.BlockSpec((1, tn), lambda i, j, k: (0, j)))
        operands.append(arr)
    n_extra = len(extras)

    multi = isinstance(out_dtype, tuple)
    n_out = len(out_dtype) if multi else 1
    if out_lay is None:
        out_shape = [jax.ShapeDtypeStruct((M, N), dt) for dt in (out_dtype if multi else (out_dtype,))]
        out_spec = [pl.BlockSpec((tm, tn), lambda i, j, k: (i, j)) for _ in range(n_out)]
    else:
        out_shape = [jax.ShapeDtypeStruct(tuple(out4_shape), out_dtype)]
        if out_lay == "cs":
            per_o = oC // tn
            out_spec = [pl.BlockSpec((None, None, tm, tn), lambda i, j, k: (out_s0 + j // per_o, out_l, i, j % per_o))]
        else:
            per_o = oR // tm
            out_spec = [pl.BlockSpec((None, None, tm, tn), lambda i, j, k: (out_s0 + i // per_o, out_l, i % per_o, j))]
    aliases = {}
    if out_into is not None:
        in_specs.append(pl.BlockSpec(memory_space=pl.ANY))
        operands.append(out_into)
        aliases = {len(operands) - 1: 0}

    dims = {"nn": (((1,), (0,)), ((), ())), "nt": (((1,), (1,)), ((), ())), "tn": (((0,), (0,)), ((), ()))}[mode]

    def body(a_ref, b_ref, *rest):
        extra_refs = rest[:n_extra]
        o_refs = rest[len(rest) - n_out - (nk > 1):len(rest) - (nk > 1)]

        def finish(r):
            if epi is not None:
                r = epi(r, *[e[...] for e in extra_refs])
            for o_ref, val in zip(o_refs, r if multi else (r,)):
                o_ref[...] = val.astype(o_ref.dtype)

        part = lax.dot_general(a_ref[...].astype(BF16), b_ref[...].astype(BF16), dims, preferred_element_type=F32)
        if nk == 1:
            finish(part)
            return
        acc = rest[-1]
        k = pl.program_id(2)

        @pl.when(k == 0)
        def _():
            acc[...] = part

        @pl.when(k != 0)
        def _():
            acc[...] += part

        @pl.when(k == nk - 1)
        def _():
            finish(acc[...])

    outs = pl.pallas_call(
        body, name=name, grid=grid, in_specs=in_specs, out_specs=out_spec, out_shape=out_shape,
        scratch_shapes=[pltpu.VMEM((tm, tn), F32)] if nk > 1 else [], input_output_aliases=aliases,
        compiler_params=_params("parallel", "parallel", "arbitrary"),
    )(*operands)
    return tuple(outs) if multi else outs[0]


def _as4(w):
    return w.reshape((1, 1) + w.shape)


def _relu2(acc):
    r = jnp.maximum(acc, 0.0)
    return r * r


def _relu2_bwd(acc, r):
    return acc * (2.0 * jnp.sqrt(r.astype(F32)))


def _add(acc, e):
    return acc + e


def _gated_residual(acc, h, gate):
    return acc, h + gate * acc


def _row_tiles(N, B, pref=256):
    S = N // B
    tm = _div(S, pref, SUBLANES)
    return tm, S // tm


def _normmod(h, g, scale, shift, B):
    N, D = h.shape
    tm, per_ex = _row_tiles(N, B)

    def body(h_ref, g_ref, sc_ref, sh_ref, u_ref):
        x = h_ref[...]
        rstd = lax.rsqrt(jnp.mean(x * x, axis=-1, keepdims=True) + EPS)
        y = (x * rstd) * g_ref[...]
        u_ref[...] = (y * (1.0 + sc_ref[...]) + sh_ref[...]).astype(u_ref.dtype)

    tok = pl.BlockSpec((tm, D), lambda i: (i, 0))
    vec = pl.BlockSpec((1, D), lambda i: (0, 0))
    ex = pl.BlockSpec((None, 1, D), lambda i: (i // per_ex, 0, 0))
    return pl.pallas_call(
        body, name="normmod_fwd", grid=(N // tm,), in_specs=[tok, vec, ex, ex], out_specs=tok,
        out_shape=jax.ShapeDtypeStruct((N, D), BF16), compiler_params=_params("parallel"),
    )(h, g, scale, shift)


def _normmod_bwd(du, h, g, scale, dh_in, B):
    N, D = h.shape
    tm, per_ex = _row_tiles(N, B)

    def body(du_ref, h_ref, g_ref, sc_ref, dhin_ref, dh_ref, dg_ref, dsc_ref, dsh_ref):
        i = pl.program_id(0)
        x = h_ref[...]
        gv = g_ref[...]
        d_u = du_ref[...].astype(F32)
        rstd = lax.rsqrt(jnp.mean(x * x, axis=-1, keepdims=True) + EPS)
        xn = x * rstd
        dyg = d_u * (1.0 + sc_ref[...])
        dxn = dyg * gv
        dh_ref[...] = dhin_ref[...] + rstd * (dxn - xn * jnp.mean(dxn * xn, axis=-1, keepdims=True))
        dsh_t = jnp.sum(d_u, axis=0, keepdims=True)
        dsc_t = jnp.sum(d_u * (xn * gv), axis=0, keepdims=True)
        dg_t = jnp.sum(dyg * xn, axis=0, keepdims=True)

        @pl.when(i % per_ex == 0)
        def _():
            dsc_ref[...] = dsc_t
            dsh_ref[...] = dsh_t

        @pl.when(i % per_ex != 0)
        def _():
            dsc_ref[...] += dsc_t
            dsh_ref[...] += dsh_t

        @pl.when(i == 0)
        def _():
            dg_ref[...] = dg_t

        @pl.when(i != 0)
        def _():
            dg_ref[...] += dg_t

    tok = pl.BlockSpec((tm, D), lambda i: (i, 0))
    vec = pl.BlockSpec((1, D), lambda i: (0, 0))
    ex = pl.BlockSpec((None, 1, D), lambda i: (i // per_ex, 0, 0))
    return pl.pallas_call(
        body, name="normmod_bwd", grid=(N // tm,), in_specs=[tok, tok, vec, ex, tok], out_specs=[tok, vec, ex, ex],
        out_shape=[jax.ShapeDtypeStruct((N, D), F32), jax.ShapeDtypeStruct((1, D), F32),
                   jax.ShapeDtypeStruct((B, 1, D), F32), jax.ShapeDtypeStruct((B, 1, D), F32)],
        compiler_params=_params("arbitrary"),
    )(du, h, g, scale, dh_in)


def _residual_bwd(dh, gate, y, B):
    N, D = dh.shape
    tm, per_ex = _row_tiles(N, B)

    def body(dh_ref, gt_ref, y_ref, dy_ref, dgt_ref):
        i = pl.program_id(0)
        d = dh_ref[...]
        dy_ref[...] = (gt_ref[...] * d).astype(dy_ref.dtype)
        t = jnp.sum(d * y_ref[...], axis=0, keepdims=True)

        @pl.when(i % per_ex == 0)
        def _():
            dgt_ref[...] = t

        @pl.when(i % per_ex != 0)
        def _():
            dgt_ref[...] += t

    tok = pl.BlockSpec((tm, D), lambda i: (i, 0))
    ex = pl.BlockSpec((None, 1, D), lambda i: (i // per_ex, 0, 0))
    return pl.pallas_call(
        body, name="residual_bwd", grid=(N // tm,), in_specs=[tok, ex, tok], out_specs=[tok, ex],
        out_shape=[jax.ShapeDtypeStruct((N, D), BF16), jax.ShapeDtypeStruct((B, 1, D), F32)],
        compiler_params=_params("arbitrary"),
    )(dh, gate, y)


def _glu_residual(zz, h, gate, B):
    N, D2 = zz.shape
    D = D2 // 2
    tm, per_ex = _row_tiles(N, B)

    def body(v_ref, g_ref, h_ref, gt_ref, y_ref, o_ref):
        y = v_ref[...] * jax.nn.sigmoid(g_ref[...])
        y_ref[...] = y.astype(y_ref.dtype)
        o_ref[...] = h_ref[...] + gt_ref[...] * y

    tok = pl.BlockSpec((tm, D), lambda i: (i, 0))
    return pl.pallas_call(
        body, name="glu_fwd", grid=(N // tm,),
        in_specs=[tok, pl.BlockSpec((tm, D), lambda i: (i, 1)), tok,
                  pl.BlockSpec((None, 1, D), lambda i: (i // per_ex, 0, 0))],
        out_specs=[tok, tok], out_shape=[jax.ShapeDtypeStruct((N, D), BF16), jax.ShapeDtypeStruct((N, D), F32)],
        compiler_params=_params("parallel"),
    )(zz, zz, h, gate)


def _glu_bwd(dy, zz):
    N, D2 = zz.shape
    D = D2 // 2
    tm = _div(N, 256, SUBLANES)

    def body(dy_ref, v_ref, g_ref, o_ref):
        d = dy_ref[...].astype(F32)
        s = jax.nn.sigmoid(g_ref[...])
        o_ref[...] = jnp.concatenate([d * s, d * v_ref[...] * s * (1.0 - s)], axis=1).astype(o_ref.dtype)

    return pl.pallas_call(
        body, name="glu_bwd", grid=(N // tm,),
        in_specs=[pl.BlockSpec((tm, D), lambda i: (i, 0)), pl.BlockSpec((tm, D), lambda i: (i, 0)),
                  pl.BlockSpec((tm, D), lambda i: (i, 1))],
        out_specs=pl.BlockSpec((tm, D2), lambda i: (i, 0)), out_shape=jax.ShapeDtypeStruct((N, D2), BF16),
        compiler_params=_params("parallel"),
    )(dy, zz, zz)


def _loss_head(h, g, target):
    N, D = h.shape
    tm = _div(N, 256, SUBLANES)

    def body(h_ref, g_ref, t_ref, loss_ref, dh_ref, dg_ref):
        i = pl.program_id(0)
        x = h_ref[...]
        gv = g_ref[...]
        rstd = lax.rsqrt(jnp.mean(x * x, axis=-1, keepdims=True) + EPS)
        xn = x * rstd
        err = xn * gv - t_ref[...]
        part = 0.5 * jnp.sum(jnp.sum(err * err, axis=-1, keepdims=True) / D, axis=0, keepdims=True)
        dy = err / D
        dxn = dy * gv
        dh_ref[...] = rstd * (dxn - xn * jnp.mean(dxn * xn, axis=-1, keepdims=True))
        dg_t = jnp.sum(dy * xn, axis=0, keepdims=True)
        part = jnp.broadcast_to(part, loss_ref.shape)

        @pl.when(i == 0)
        def _():
            loss_ref[...] = part
            dg_ref[...] = dg_t

        @pl.when(i != 0)
        def _():
            loss_ref[...] += part
            dg_ref[...] += dg_t

    tok = pl.BlockSpec((tm, D), lambda i: (i, 0))
    vec = pl.BlockSpec((1, D), lambda i: (0, 0))
    return pl.pallas_call(
        body, name="loss_head", grid=(N // tm,), in_specs=[tok, vec, tok],
        out_specs=[pl.BlockSpec((SUBLANES, LANES), lambda i: (0, 0)), tok, vec],
        out_shape=[jax.ShapeDtypeStruct((SUBLANES, LANES), F32), jax.ShapeDtypeStruct((N, D), F32),
                   jax.ShapeDtypeStruct((1, D), F32)],
        compiler_params=_params("arbitrary"),
    )(h, g, target)


def _swap_halves(x):
    half = x.shape[-1] // 2
    return jnp.concatenate([x[:, half:], x[:, :half]], axis=1)


def _gelu(y):
    return jax.nn.gelu(y)


def _gelu_grad(y):
    c0 = math.sqrt(2.0 / math.pi)
    inner = c0 * (y + 0.044715 * y * y * y)
    t = jnp.tanh(inner)
    return 0.5 * (1.0 + t) + 0.5 * y * (1.0 - t * t) * c0 * (1.0 + 3.0 * 0.044715 * y * y)


def _s5_discretize(lam_re, lam_im, log_dt, b_re, b_im, c_re, c_im):
    G = lam_re.shape[0]
    nch = G // CHUNK_GROUPS
    dt = jnp.exp(log_dt)[:, None]
    er = jnp.exp(lam_re * dt)
    a_re = er * jnp.cos(lam_im * dt)
    a_im = er * jnp.sin(lam_im * dt)
    den = lam_re * lam_re + lam_im * lam_im
    n_re, n_im = a_re - 1.0, a_im
    f_re = (n_re * lam_re + n_im * lam_im) / den
    f_im = (n_im * lam_re - n_re * lam_im) / den
    bb_re = f_re[..., None] * b_re - f_im[..., None] * b_im
    bb_im = f_re[..., None] * b_im + f_im[..., None] * b_re
    eye = jnp.eye(CHUNK_GROUPS, dtype=F32)

    def pack_b(bb):
        bb = bb.reshape(nch, CHUNK_GROUPS, SSM_STATE, SSM_GROUP)
        return jnp.einsum("jgpc,gh->jgchp", bb, eye).reshape(nch, LANES, CHUNK_STATE)

    def pack_c(cc):
        cc = cc.reshape(nch, CHUNK_GROUPS, SSM_GROUP, SSM_STATE)
        return jnp.einsum("jgcp,gh->jgphc", cc, eye).reshape(nch, CHUNK_STATE, LANES)

    bd = jnp.concatenate([pack_b(bb_re), pack_b(bb_im)], axis=2)
    cd = jnp.concatenate([pack_c(c_re), pack_c(-c_im)], axis=1)
    return bd, cd, a_re, a_im


def _s5_scan_coefs(lam_re, lam_im, log_dt):
    G = lam_re.shape[0]
    nch = G // CHUNK_GROUPS
    dt = jnp.exp(log_dt)[:, None]

    def power(k):
        er = jnp.exp(k * lam_re * dt)
        re = (er * jnp.cos(k * lam_im * dt)).reshape(nch, 1, CHUNK_STATE)
        im = (er * jnp.sin(k * lam_im * dt)).reshape(nch, 1, CHUNK_STATE)
        return re, im

    pw = {k: power(k) for k in range(1, SUBLANES + 1)}
    row = jnp.arange(SUBLANES, dtype=jnp.int32)[None, :, None]

    def table(reverse):
        sign = -1.0 if reverse else 1.0
        tabs = []
        for s in (1, 2, 4):
            re, im = pw[s]
            mask = (row < SUBLANES - s) if reverse else (row >= s)
            ar = jnp.where(mask, jnp.concatenate([re, re], axis=2), 0.0)
            ai = jnp.where(mask, jnp.concatenate([-sign * im, sign * im], axis=2), 0.0)
            tabs += [ar, ai]
        pr_rows, pi_rows = [], []
        for j in range(SUBLANES):
            re, im = pw[SUBLANES - j] if reverse else pw[j + 1]
            pr_rows.append(jnp.concatenate([re, re], axis=2))
            pi_rows.append(jnp.concatenate([-sign * im, sign * im], axis=2))
        tabs += [jnp.concatenate(pr_rows, axis=1), jnp.concatenate(pi_rows, axis=1)]
        return jnp.stack([jnp.broadcast_to(t, (nch, SUBLANES, 2 * CHUNK_STATE)) for t in tabs], axis=1)

    return table(False), table(True)


def _scan_rows(x_ref, coef_ref, carry_ref, n_groups, reverse, prev_ref=None):
    row = lax.broadcasted_iota(jnp.int32, (SUBLANES, x_ref.shape[-1]), 0)

    def group(t, _):
        i = (n_groups - 1 - t) if reverse else t
        rows = pl.ds(pl.multiple_of(i * SUBLANES, SUBLANES), SUBLANES)
        x = x_ref[rows, :]
        for si, s in enumerate((1, 2, 4)):
            xs = pltpu.roll(x, (SUBLANES - s) if reverse else s, 0)
            x = x + coef_ref[2 * si] * xs + coef_ref[2 * si + 1] * _swap_halves(xs)
        c = carry_ref[...]
        x = x + coef_ref[6] * c + coef_ref[7] * _swap_halves(c)
        if prev_ref is not None:
            prev_ref[rows, :] = jnp.where(row == 0, c, pltpu.roll(x, 1, 0))
        edge = x[0:1, :] if reverse else x[SUBLANES - 1:SUBLANES, :]
        carry_ref[...] = jnp.broadcast_to(edge, carry_ref.shape)
        x_ref[rows, :] = x
        return 0

    lax.fori_loop(0, n_groups, group, 0)


def _s5_fwd(u, bd, cd, coef_f, d_skip, B, carry=None):
    N, D = u.shape
    S = N // B
    nch = D // LANES
    W = 2 * CHUNK_STATE
    tm = _div(S, 256, SUBLANES)
    nt = S // tm

    def body(u_ref, bd_ref, cd_ref, coef_ref, d_ref, z_ref, cin_ref, x_s, carry):
        t = pl.program_id(2)

        @pl.when(t == 0)
        def _():
            carry[...] = jnp.zeros_like(carry)

        cin_ref[...] = carry[...]
        uv = u_ref[...]
        x_s[...] = jnp.dot(uv, bd_ref[...], preferred_element_type=F32)
        _scan_rows(x_s, coef_ref, carry, tm // SUBLANES, False)
        y = jnp.dot(x_s[...].astype(BF16), cd_ref[...], preferred_element_type=F32) + d_ref[...] * uv.astype(F32)
        z_ref[...] = _gelu(y).astype(z_ref.dtype)

    (z, carries), moved = _carried_call(
        body, name="s5_fwd", grid=(nch, B, nt),
        in_specs=[pl.BlockSpec((tm, LANES), lambda j, b, t: (b * nt + t, j)),
                  pl.BlockSpec((None, LANES, W), lambda j, b, t: (j, 0, 0)),
                  pl.BlockSpec((None, W, LANES), lambda j, b, t: (j, 0, 0)),
                  pl.BlockSpec((None, 8, SUBLANES, W), lambda j, b, t: (j, 0, 0, 0)),
                  pl.BlockSpec((1, LANES), lambda j, b, t: (0, j))],
        out_specs=[pl.BlockSpec((tm, LANES), lambda j, b, t: (b * nt + t, j)),
                   pl.BlockSpec((None, None, SUBLANES, W), lambda j, b, t: (j, b * nt + t, 0, 0))],
        out_shape=[jax.ShapeDtypeStruct((N, D), BF16), jax.ShapeDtypeStruct((nch, B * nt, SUBLANES, W), F32)],
        scratch_shapes=[pltpu.VMEM((tm, W), F32), pltpu.VMEM((SUBLANES, W), F32)],
        operands=(u, bd, cd, coef_f, d_skip), sem=("parallel", "arbitrary", "arbitrary"), carry=carry)
    return z, carries, moved


def _s5_bwd(u, dz, bd, cd, coef_f, coef_b, d_skip, carries, B, carry=None):
    N, D = u.shape
    S = N // B
    nch = D // LANES
    W = 2 * CHUNK_STATE
    tm = carries.shape[1] // B
    tm = S // tm
    nt = S // tm
    ng = tm // SUBLANES

    def body(u_ref, dz_ref, bd_ref, cd_ref, cf_ref, cb_ref, d_ref, cin_ref,
             du_ref, dbd_ref, dcd_ref, da_ref, dd_ref, x_s, xp_s, l_s, carry, lcarry):
        b = pl.program_id(1)
        t = pl.program_id(2)

        @pl.when((b == 0) & (t == 0))
        def _():
            dbd_ref[...] = jnp.zeros_like(dbd_ref)
            dcd_ref[...] = jnp.zeros_like(dcd_ref)
            da_ref[...] = jnp.zeros_like(da_ref)
            dd_ref[...] = jnp.zeros_like(dd_ref)

        @pl.when(t == 0)
        def _():
            lcarry[...] = jnp.zeros_like(lcarry)

        uv = u_ref[...]
        uf = uv.astype(F32)
        carry[...] = cin_ref[...]
        x_s[...] = jnp.dot(uv, bd_ref[...], preferred_element_type=F32)
        _scan_rows(x_s, cf_ref, carry, ng, False, prev_ref=xp_s)
        xb = x_s[...].astype(BF16)
        y = jnp.dot(xb, cd_ref[...], preferred_element_type=F32) + d_ref[...] * uf
        dy = dz_ref[...] * _gelu_grad(y)
        dd_ref[...] += jnp.sum(dy * uf, axis=0, keepdims=True)
        dyb = dy.astype(BF16)
        dcd_ref[...] += lax.dot_general(xb, dyb, (((0,), (0,)), ((), ())), preferred_element_type=F32)
        l_s[...] = lax.dot_general(dyb, cd_ref[...], (((1,), (1,)), ((), ())), preferred_element_type=F32)
        _scan_rows(l_s, cb_ref, lcarry, ng, True)
        lam = l_s[...]
        lb = lam.astype(BF16)
        dbd_ref[...] += lax.dot_general(uv, lb, (((0,), (0,)), ((), ())), preferred_element_type=F32)
        du_ref[...] = lax.dot_general(lb, bd_ref[...], (((1,), (1,)), ((), ())), preferred_element_type=F32) + d_ref[...] * dy
        xp = xp_s[...]
        da_ref[0:1, :] += jnp.sum(lam * xp, axis=0, keepdims=True)
        da_ref[1:2, :] += jnp.sum(lam * _swap_halves(xp), axis=0, keepdims=True)

    tile = lambda j, b, t: (b * nt + (nt - 1 - t), j)
    outs, moved = _carried_call(
        body, name="s5_bwd", grid=(nch, B, nt),
        in_specs=[pl.BlockSpec((tm, LANES), tile), pl.BlockSpec((tm, LANES), tile),
                  pl.BlockSpec((None, LANES, W), lambda j, b, t: (j, 0, 0)),
                  pl.BlockSpec((None, W, LANES), lambda j, b, t: (j, 0, 0)),
                  pl.BlockSpec((None, 8, SUBLANES, W), lambda j, b, t: (j, 0, 0, 0)),
                  pl.BlockSpec((None, 8, SUBLANES, W), lambda j, b, t: (j, 0, 0, 0)),
                  pl.BlockSpec((1, LANES), lambda j, b, t: (0, j)),
                  pl.BlockSpec((None, None, SUBLANES, W), lambda j, b, t: (j, b * nt + (nt - 1 - t), 0, 0))],
        out_specs=[pl.BlockSpec((tm, LANES), tile),
                   pl.BlockSpec((None, LANES, W), lambda j, b, t: (j, 0, 0)),
                   pl.BlockSpec((None, W, LANES), lambda j, b, t: (j, 0, 0)),
                   pl.BlockSpec((None, 2, W), lambda j, b, t: (j, 0, 0)),
                   pl.BlockSpec((1, LANES), lambda j, b, t: (0, j))],
        out_shape=[jax.ShapeDtypeStruct((N, D), F32), jax.ShapeDtypeStruct((nch, LANES, W), F32),
                   jax.ShapeDtypeStruct((nch, W, LANES), F32), jax.ShapeDtypeStruct((nch, 2, W), F32),
                   jax.ShapeDtypeStruct((1, D), F32)],
        scratch_shapes=[pltpu.VMEM((tm, W), F32), pltpu.VMEM((tm, W), F32), pltpu.VMEM((tm, W), F32),
                        pltpu.VMEM((SUBLANES, W), F32), pltpu.VMEM((SUBLANES, W), F32)],
        operands=(u, dz, bd, cd, coef_f, coef_b, d_skip, carries), sem=("parallel", "arbitrary", "arbitrary"), carry=carry)
    return (*outs, moved)


ATTN_HEADS = LANES // HEAD_DIM
ATTN_UNROLL = 2


def _attn_mask(n):
    qi = lax.broadcasted_iota(jnp.int32, (ATTN_BLOCK, 2 * ATTN_BLOCK), 0)
    kj = lax.broadcasted_iota(jnp.int32, (ATTN_BLOCK, 2 * ATTN_BLOCK), 1)
    prev_ok = (kj < ATTN_BLOCK) & (kj >= qi) & (n > 0)
    return prev_ok | ((kj >= ATTN_BLOCK) & (kj - ATTN_BLOCK <= qi))


def _head_lanes(h):
    lane = lax.broadcasted_iota(jnp.int32, (ATTN_BLOCK, LANES), 1)
    return (lane >= h * HEAD_DIM) & (lane < (h + 1) * HEAD_DIM)


def _per_head(cols):
    out = jnp.broadcast_to(cols[-1], (ATTN_BLOCK, LANES))
    for h in range(len(cols) - 2, -1, -1):
        out = jnp.where(_head_lanes(h), jnp.broadcast_to(cols[h], (ATTN_BLOCK, LANES)), out)
    return out


def _only_head(x, h):
    return jnp.where(_head_lanes(h), x, 0.0).astype(BF16)


def _block_rows(tb, dil, nb):
    r = tb // nb
    n = tb % nb
    start = r + dil * ATTN_BLOCK * n
    startp = jnp.where(n > 0, start - dil * ATTN_BLOCK, start)
    return n, pl.ds(start, ATTN_BLOCK, stride=dil), pl.ds(startp, ATTN_BLOCK, stride=dil)


def _attn_fwd(q, k, v, B, carry=None):
    _, S, D3 = q.shape
    D = D3 // 3
    HP = D // LANES
    scale = HEAD_DIM ** -0.5
    n_blocks = S // ATTN_BLOCK
    nbr = len(DILATIONS)
    nt_dims = (((1,), (1,)), ((), ()))

    def branch(dil, q_ref, k_ref, v_ref, acc, m_s, l_s):
        nb = (S // dil) // ATTN_BLOCK

        def blk(tb, _):
            n, rows, rowsp = _block_rows(tb, dil, nb)
            qb = q_ref[rows, :] * scale
            kk = jnp.concatenate([k_ref[rowsp, :], k_ref[rows, :]], axis=0).astype(BF16)
            vv = jnp.concatenate([v_ref[rowsp, :], v_ref[rows, :]], axis=0).astype(BF16)
            ok = _attn_mask(n)
            ms, ls, accs = [], [], []
            for h in range(ATTN_HEADS):
                s = lax.dot_general(_only_head(qb, h), kk, nt_dims, preferred_element_type=F32)
                s = jnp.where(ok, s, NEG)
                mh = jnp.max(s, axis=-1, keepdims=True)
                p = jnp.exp(s - mh)
                ms.append(mh)
                ls.append(jnp.sum(p, axis=-1, keepdims=True))
                accs.append(jnp.dot(p.astype(BF16), vv, preferred_element_type=F32))
            m_s[rows, :] = _per_head(ms)
            l_s[rows, :] = _per_head(ls)
            acc[rows, :] = _per_head(accs)
            return 0

        lax.fori_loop(0, n_blocks, blk, 0, unroll=ATTN_UNROLL)

    def body(q_ref, k_ref, v_ref, o_ref, lse_ref, *scratch):
        accs, m_ss, l_ss = scratch[:nbr], scratch[nbr:2 * nbr], scratch[2 * nbr:]
        g = pl.program_id(2)
        for gi, dil in enumerate(DILATIONS):
            pl.when(g == gi)(functools.partial(branch, dil, q_ref, k_ref, v_ref, accs[gi], m_ss[gi], l_ss[gi]))

        @pl.when(g == nbr - 1)
        def _():
            def fin(i, _):
                rows = pl.ds(pl.multiple_of(i * ATTN_BLOCK, ATTN_BLOCK), ATTN_BLOCK)
                ms = [m[rows, :] for m in m_ss]
                m_all = functools.reduce(jnp.maximum, ms)
                ws = [jnp.exp(m - m_all) for m in ms]
                den = sum(w * l[rows, :] for w, l in zip(ws, l_ss))
                o_ref[rows, :] = sum(w * a[rows, :] for w, a in zip(ws, accs)) / den
                lse_ref[rows, :] = m_all + jnp.log(den)
                return 0

            lax.fori_loop(0, n_blocks, fin, 0)

    br = pl.BlockSpec((None, S, LANES), lambda b, hp, g: (b, 0, g * HP + hp))
    hd = pl.BlockSpec((None, S, LANES), lambda b, hp, g: (b, 0, hp))
    (o, lse), moved = _carried_call(
        body, name="attn_fwd", grid=(B, HP, nbr), in_specs=[br, br, br], out_specs=[hd, hd],
        out_shape=[jax.ShapeDtypeStruct((B, S, D), F32), jax.ShapeDtypeStruct((B, S, D), F32)],
        scratch_shapes=[pltpu.VMEM((S, LANES), F32)] * (3 * nbr),
        operands=(q, k, v), sem=("parallel", "parallel", "arbitrary"), carry=carry)
    return o, lse, moved


def _attn_bwd(q, k, v, o, lse, do, dk_prev, dv_prev, B, carry=None):
    _, S, D3 = q.shape
    D = D3 // 3
    HP = D // LANES
    scale = HEAD_DIM ** -0.5
    n_blocks = S // ATTN_BLOCK
    has_prev = dk_prev is not None
    nt_dims = (((1,), (1,)), ((), ()))
    tn_dims = (((0,), (0,)), ((), ()))

    def branch(dil, q_ref, k_ref, v_ref, lse_ref, do_ref, dq_ref, dk_ref, dv_ref, delta, dk_p, dv_p):
        nb = (S // dil) // ATTN_BLOCK

        def blk(tb, _):
            n, rows, rowsp = _block_rows(tb, dil, nb)
            qb = q_ref[rows, :] * scale
            dob, lb, db = do_ref[rows, :], lse_ref[rows, :], delta[rows, :]
            kk = jnp.concatenate([k_ref[rowsp, :], k_ref[rows, :]], axis=0).astype(BF16)
            vv = jnp.concatenate([v_ref[rowsp, :], v_ref[rows, :]], axis=0).astype(BF16)
            ok = _attn_mask(n)
            dqs = []
            dkk = dvv = None
            for h in range(ATTN_HEADS):
                qh, doh = _only_head(qb, h), _only_head(dob, h)
                lh = lb[:, h * HEAD_DIM:h * HEAD_DIM + 1]
                dlt = db[:, h * HEAD_DIM:h * HEAD_DIM + 1]
                s = lax.dot_general(qh, kk, nt_dims, preferred_element_type=F32)
                p = jnp.where(ok, jnp.exp(s - lh), 0.0)
                dp = lax.dot_general(doh, vv, nt_dims, preferred_element_type=F32)
                ds = (p * (dp - dlt)).astype(BF16)
                dqs.append(jnp.dot(ds, kk, preferred_element_type=F32))
                dk_h = lax.dot_general(ds, qh, tn_dims, preferred_element_type=F32)
                dv_h = lax.dot_general(p.astype(BF16), doh, tn_dims, preferred_element_type=F32)
                dkk = dk_h if dkk is None else dkk + dk_h
                dvv = dv_h if dvv is None else dvv + dv_h
            dq_ref[rows, :] = _per_head(dqs) * scale
            dk_p[rowsp, :] = dkk[:ATTN_BLOCK]
            dv_p[rowsp, :] = dvv[:ATTN_BLOCK]
            dk_ref[rows, :] = dkk[ATTN_BLOCK:]
            dv_ref[rows, :] = dvv[ATTN_BLOCK:]
            return 0

        lax.fori_loop(0, n_blocks, blk, 0, unroll=ATTN_UNROLL)

    def body(*refs):
        q_ref, k_ref, v_ref, o_ref, lse_ref, do_ref = refs[:6]
        n_in = 8 if has_prev else 6
        dq_ref, dk_ref, dv_ref, delta, dk_p, dv_p = refs[n_in:n_in + 6]
        g = pl.program_id(2)

        @pl.when(g == 0)
        def _():
            def dl(i, _):
                rows = pl.ds(pl.multiple_of(i * ATTN_BLOCK, ATTN_BLOCK), ATTN_BLOCK)
                prod = do_ref[rows, :] * o_ref[rows, :]
                delta[rows, :] = _per_head([jnp.sum(jnp.where(_head_lanes(h), prod, 0.0), axis=-1, keepdims=True)
                                            for h in range(ATTN_HEADS)])
                return 0

            lax.fori_loop(0, n_blocks, dl, 0)

        dk_p[...] = jnp.zeros_like(dk_p)
        dv_p[...] = jnp.zeros_like(dv_p)
        for gi, dil in enumerate(DILATIONS):
            pl.when(g == gi)(functools.partial(branch, dil, q_ref, k_ref, v_ref, lse_ref, do_ref, dq_ref, dk_ref, dv_ref,
                                               delta, dk_p, dv_p))

        def fin(i, _):
            rows = pl.ds(pl.multiple_of(i * ATTN_BLOCK, ATTN_BLOCK), ATTN_BLOCK)
            dk_t = dk_ref[rows, :] + dk_p[rows, :]
            dv_t = dv_ref[rows, :] + dv_p[rows, :]
            if has_prev:
                dk_t = dk_t + refs[6][rows, :]
                dv_t = dv_t + refs[7][rows, :]
            dk_ref[rows, :] = dk_t
            dv_ref[rows, :] = dv_t
            return 0

        lax.fori_loop(0, n_blocks, fin, 0)

    br = pl.BlockSpec((None, S, LANES), lambda b, hp, g: (b, 0, g * HP + hp))
    hd = pl.BlockSpec((None, S, LANES), lambda b, hp, g: (b, 0, hp))
    ins = [q, k, v, o, lse, do] + ([dk_prev, dv_prev] if has_prev else [])
    (dq, dk, dv), moved = _carried_call(
        body, name="attn_bwd", grid=(B, HP, len(DILATIONS)),
        in_specs=[br, br, br, hd, hd, hd] + ([br, br] if has_prev else []), out_specs=[br, br, br],
        out_shape=[jax.ShapeDtypeStruct(q.shape, F32)] * 3, scratch_shapes=[pltpu.VMEM((S, LANES), F32)] * 3,
        operands=ins, sem=("parallel", "parallel", "arbitrary"), carry=carry)
    return dq, dk, dv, moved


def _adamw(w, grads, m, v):
    R, C = w.shape
    tr = _div(R, 256, SUBLANES)
    ng = len(grads)
    c1 = 1.0 - ADAM_B1 ** ADAM_STEP
    c2 = 1.0 - ADAM_B2 ** ADAM_STEP

    def body(*refs):
        w_ref, m_ref, v_ref = refs[0], refs[1 + ng], refs[2 + ng]
        d_ref, mo_ref, vo_ref = refs[3 + ng:6 + ng]
        g = refs[1][...]
        if ng == 2:
            g = g + refs[2][...]
            refs[6 + ng][...] = g
        mn = ADAM_B1 * m_ref[...] + (1.0 - ADAM_B1) * g
        vn = ADAM_B2 * v_ref[...] + (1.0 - ADAM_B2) * (g * g)
        d_ref[...] = -ADAM_LR * ((mn / c1) / (jnp.sqrt(vn / c2) + ADAM_EPS) + ADAM_WD * w_ref[...])
        mo_ref[...] = mn
        vo_ref[...] = vn

    blk = pl.BlockSpec((tr, C), lambda i: (i, 0))
    n_out = 3 + (ng == 2)
    outs = pl.pallas_call(
        body, name="adamw", grid=(R // tr,), in_specs=[blk] * (3 + ng), out_specs=[blk] * n_out,
        out_shape=[jax.ShapeDtypeStruct((R, C), F32)] * n_out, compiler_params=_params("parallel"),
    )(w, *grads, m, v)
    return (outs[3] if ng == 2 else grads[0],) + tuple(outs[:3])


def _sum_shards(recv):
    n, R, C = recv.shape
    tr = _div(R, 256, SUBLANES if recv.dtype == F32 else 2 * SUBLANES)

    def body(r_ref, o_ref):
        s = r_ref[0].astype(F32)
        for i in range(1, n):
            s = s + r_ref[i].astype(F32)
        o_ref[...] = s

    return pl.pallas_call(
        body, name="sum_shards", grid=(R // tr,), in_specs=[pl.BlockSpec((n, tr, C), lambda i: (0, i, 0))],
        out_specs=pl.BlockSpec((tr, C), lambda i: (i, 0)), out_shape=jax.ShapeDtypeStruct((R, C), F32),
        compiler_params=_params("parallel"),
    )(recv)


N_DEV = 8
N_CHIPS = 4


def _all_gather_small(x):
    m_per, n = x.shape

    def body(x_ref, out_ref, send_sems, recv_sems, local_sem):
        cx, cy, cc = _coords()
        me, sibling = (cx, cy, cc), (cx, cy, 1 - cc)
        chips = [(1 - cx, cy), (cx, 1 - cy), (1 - cx, 1 - cy)]

        def rows(px, py, pc):
            return out_ref.at[pl.ds((4 * px + 2 * py + pc) * m_per, m_per), :]

        def copy(k, block, to, src=None):
            return pltpu.make_async_remote_copy(
                src_ref=rows(*block) if src is None else src, dst_ref=rows(*block), send_sem=send_sems.at[k],
                recv_sem=recv_sems.at[k], device_id=to, device_id_type=MESH)

        mine = pltpu.make_async_copy(x_ref, rows(*me), local_sem)
        mine.start()
        first = [copy(0, me, sibling, src=x_ref)]
        first += [copy(1 + j, me, (*chip, cc), src=x_ref) for j, chip in enumerate(chips)]
        for cp in first:
            cp.start()
        passed = [copy(4 + j, (*chip, cc), sibling) for j, chip in enumerate(chips)]
        for j, chip in enumerate(chips):
            copy(1 + j, (*chip, cc), me).wait_recv()
            passed[j].start()
        copy(0, sibling, me).wait_recv()
        for j, chip in enumerate(chips):
            copy(4 + j, (*chip, 1 - cc), me).wait_recv()
        for cp in first + passed:
            cp.wait_send()
        mine.wait()

    return pl.pallas_call(
        body, name="all_gather_small", out_shape=jax.ShapeDtypeStruct((N_DEV * m_per, n), x.dtype),
        in_specs=[pl.BlockSpec(memory_space=pltpu.VMEM)], out_specs=pl.BlockSpec(memory_space=pltpu.VMEM),
        scratch_shapes=[pltpu.SemaphoreType.DMA((7,)), pltpu.SemaphoreType.DMA((7,)), pltpu.SemaphoreType.DMA],
        compiler_params=pltpu.CompilerParams(vmem_limit_bytes=VMEM_LIMIT),
    )(x)


def _exchange(name, carry):
    return _carried_call(lambda: None, name=name, grid=(1,), in_specs=[], out_specs=[], out_shape=[], scratch_shapes=[],
                         operands=(), sem=("arbitrary",), carry=carry)[1]


def _layer_moves(kind, arrays_from, arrays_to, pieces):
    used = sorted({w for w, _ in pieces})
    pos = {w: i for i, w in enumerate(used)}
    gather = kind == "gather"

    def half(ref, c):
        rows = ref.shape[0] // 2
        return ref.at[pl.ds(c * rows, rows), :]

    def plan(src_refs, dst_refs, me):
        cx, cy, cc = me
        mine = 2 * cx + cy
        remote, local = [], []
        for w, l in pieces:
            s, d = src_refs[pos[w]], dst_refs[pos[w]]
            for px, py in _other_chips(cx, cy):
                if gather:
                    remote.append((half(s.at[l], cc), half(d.at[mine, l], cc), (px, py, cc)))
                else:
                    remote.append((s.at[2 * px + py, l], d.at[mine, l], (px, py, cc)))
            local.append((s.at[l] if gather else s.at[mine, l], d.at[mine, l]))
        return remote, local

    def onward(src_refs, dst_refs, me):
        cx, cy, cc = me
        moves = []
        for w, l in pieces:
            d = dst_refs[pos[w]]
            for px, py in _other_chips(cx, cy):
                landed = half(d.at[2 * px + py, l], cc)
                moves.append((landed, landed, (cx, cy, 1 - cc)))
        return moves

    n = 3 * len(pieces)
    carry = _Carry([arrays_from[w] for w in used], [arrays_to[w] for w in used], plan, n, len(pieces),
                   onward if gather else None, n if gather else 0)
    return carry, used


def _swap_with_sibling(sums):
    def plan(src_refs, dst_refs, me):
        cx, cy, cc = me
        return [(s, d, (cx, cy, 1 - cc)) for s, d in zip(src_refs, dst_refs)], []

    return _exchange("swap_with_sibling", _Carry(sums, [lax.empty(s.shape, s.dtype) for s in sums], plan, len(sums), 0))


def _pack(arrs, width):
    parts, layout, row = [], [], 0
    for a in arrs:
        flat = a.reshape(-1).astype(F32)
        rows = -(-flat.shape[0] // width)
        parts.append(jnp.pad(flat, (0, rows * width - flat.shape[0])).reshape(rows, width))
        layout.append((row, rows, a.shape))
        row += rows
    pad = -row % SUBLANES
    if pad:
        parts.append(jnp.zeros((pad, width), F32))
    return jnp.concatenate(parts, axis=0), layout, row + pad


def _unpack(buf, layout, idx):
    row, rows, shape = layout[idx]
    size = math.prod(shape)
    return buf[row:row + rows].reshape(-1)[:size].reshape(shape)


def kernel(x, c, ln_g, ada_w, ada_b, ssm_lam_re, ssm_lam_im, ssm_log_dt, ssm_b_re, ssm_b_im, ssm_c_re, ssm_c_im, ssm_d, ssm_w_glu, kv_g, kv_ada_w, kv_ada_b, w_kv, attn_w_q, attn_w_o, mlp_w1, mlp_w2, final_g, loss_target, m_ln_g, m_ada_w, m_ada_b, m_ssm_lam_re, m_ssm_lam_im, m_ssm_log_dt, m_ssm_b_re, m_ssm_b_im, m_ssm_c_re, m_ssm_c_im, m_ssm_d, m_ssm_w_glu, m_kv_g, m_kv_ada_w, m_kv_ada_b, m_w_kv, m_attn_w_q, m_attn_w_o, m_mlp_w1, m_mlp_w2, m_final_g, v_ln_g, v_ada_w, v_ada_b, v_ssm_lam_re, v_ssm_lam_im, v_ssm_log_dt, v_ssm_b_re, v_ssm_b_im, v_ssm_c_re, v_ssm_c_im, v_ssm_d, v_ssm_w_glu, v_kv_g, v_kv_ada_w, v_kv_ada_b, v_w_kv, v_attn_w_q, v_attn_w_o, v_mlp_w1, v_mlp_w2, v_final_g):
    B, S, D = x.shape
    N = B * S
    depth = ln_g.shape[0]
    n_a = ssm_w_glu.shape[0]
    n_b = attn_w_q.shape[0]
    FF = mlp_w1.shape[2] * N_CHIPS
    cx, cy, cc = _coords()
    chip = 2 * cx + cy
    dev = 4 * cx + 2 * cy + cc
    n_ex = N_DEV * B
    ada_cols = ada_w.shape[-1]
    kv_cols = kv_ada_w.shape[-1]

    GLU, KV, Q, O, W1, W2 = range(6)
    shards = [ssm_w_glu.astype(BF16), w_kv.astype(BF16)[None], attn_w_q.astype(BF16), attn_w_o.astype(BF16),
              mlp_w1.astype(BF16), mlp_w2.astype(BF16)]
    wg = [lax.empty((N_CHIPS,) + s.shape, BF16) for s in shards]

    def fetch(pieces):
        return _layer_moves("gather", shards, wg, pieces)

    def landed(arrays, used, moved):
        for w, a in zip(used, moved):
            arrays[w] = a

    fetch_with = {l: [(W1, l), (W2, l)] for l in range(depth)}
    fetch_with[0] += [(Q, 0), (O, 0)]
    fetch_with[n_a - 1] += [(KV, 0)]
    for j in range(1, n_b):
        fetch_with[n_a + j - 1] += [(Q, j), (O, j)]
    carry, used = fetch([(GLU, l) for l in range(n_a)])
    landed(wg, used, _exchange("gather_weights", carry))

    c_pack, c_layout, _ = _pack([c], D)
    c_all_buf = _all_gather_small(c_pack)
    c_rows = c_pack.shape[0]
    c_all = jnp.concatenate([_unpack(c_all_buf[d * c_rows:(d + 1) * c_rows], c_layout, 0) for d in range(N_DEV)], axis=0)
    sc_all = jax.nn.silu(c_all).astype(BF16)
    n_mod = depth * 2
    ada_w8 = ada_w.reshape(n_mod, 1, D, ada_cols)
    ada_b_row = ada_b.reshape(1, n_mod * ada_cols)
    mod_local = _mm("ada_fwd", sc_all, ada_w8, mode="nn", M=n_ex, N=n_mod * ada_cols, K=D, b_lay="cs", b_ns=n_mod,
                    epi=_add, extras=[("n", ada_b_row)])
    kv_ada_b_local = lax.dynamic_slice(kv_ada_b.reshape(N_CHIPS, kv_cols), (chip, 0), (1, kv_cols))
    kvmod_local = _mm("ada_fwd", sc_all, _as4(kv_ada_w), mode="nn", M=n_ex, N=kv_cols, K=D, epi=_add,
                      extras=[("n", kv_ada_b_local)])
    mod_pack, mod_layout, mod_rows = _pack([mod_local, kvmod_local, ln_g, ssm_d], D)
    mod_buf = _all_gather_small(mod_pack)

    def from_chip(j, idx):
        d = 2 * j
        return _unpack(mod_buf[d * mod_rows:(d + 1) * mod_rows], mod_layout, idx)

    my_rows = lambda a: lax.dynamic_slice_in_dim(a, dev * B, B, axis=0)
    mods = jnp.concatenate([my_rows(from_chip(j, 0)).reshape(B, n_mod, ada_cols) for j in range(N_CHIPS)], axis=2)
    kvmod = jnp.concatenate([my_rows(from_chip(j, 1)) for j in range(N_CHIPS)], axis=1)
    ln_g_full = jnp.concatenate([from_chip(j, 2) for j in range(N_CHIPS)], axis=2)
    ssm_d_full = jnp.concatenate([from_chip(j, 3) for j in range(N_CHIPS)], axis=1)

    def mod3(l, s):
        mrow = mods[:, l * 2 + s]
        return [mrow[:, i * D:(i + 1) * D].reshape(B, 1, D) for i in range(3)]

    kv_shift, kv_scale = kvmod[:, :D].reshape(B, 1, D), kvmod[:, D:].reshape(B, 1, D)

    s5_tabs = []
    for l in range(n_a):
        prm = (ssm_lam_re[l], ssm_lam_im[l], ssm_log_dt[l], ssm_b_re[l], ssm_b_im[l], ssm_c_re[l], ssm_c_im[l])
        (bd, cd, _, _), disc_vjp = jax.vjp(_s5_discretize, *prm)
        coef_f, coef_b = _s5_scan_coefs(ssm_lam_re[l], ssm_lam_im[l], ssm_log_dt[l])
        s5_tabs.append((bd.astype(BF16), cd.astype(BF16), coef_f, coef_b, disc_vjp))

    h = x.reshape(N, D)
    saved = []
    k_all = v_all = None
    for l in range(depth):
        sv = {}
        shift, scale, gate = mod3(l, 0)
        sv["h0"], sv["scale0"], sv["gate0"] = h, scale, gate
        u = _normmod(h, ln_g_full[l, 0].reshape(1, D), scale, shift, B)
        sv["u0"] = u
        carry, used = fetch(fetch_with[l])
        if l < n_a:
            bd, cd, coef_f, coef_b, _ = s5_tabs[l]
            z, carries, moved = _s5_fwd(u, bd, cd, coef_f, ssm_d_full[l].reshape(1, D), B, carry)
            landed(wg, used, moved)
            zz = _mm("glu_proj", z, wg[GLU], mode="nn", M=N, N=2 * D, K=D, b_lay="cs", b_l=l, b_ns=N_CHIPS)
            y, h_next = _glu_residual(zz, h, gate, B)
            sv["z"], sv["carries"], sv["zz"] = z, carries, zz
        else:
            j = l - n_a
            q = _mm("q_proj", u, wg[Q], mode="nn", M=N, N=3 * D, K=D, b_lay="cs", b_l=j, b_ns=N_CHIPS)
            q3 = q.reshape(B, S, 3 * D)
            o, lse, moved = _attn_fwd(q3, k_all, v_all, B, carry)
            landed(wg, used, moved)
            o2 = o.reshape(N, D)
            y, h_next = _mm("o_proj", o2, wg[O], mode="nn", M=N, N=D, K=D, b_lay="rs", b_l=j, b_ns=N_CHIPS,
                            out_dtype=(BF16, F32), epi=_gated_residual, extras=[("mn", h), ("ex", gate)], rows_per_ex=S)
            sv["q"], sv["o"], sv["lse"] = q3, o, lse
        sv["y0"] = y
        h = h_next
        shift, scale, gate = mod3(l, 1)
        sv["h1"], sv["scale1"], sv["gate1"] = h, scale, gate
        u = _normmod(h, ln_g_full[l, 1].reshape(1, D), scale, shift, B)
        r = _mm("mlp_up", u, wg[W1], mode="nn", M=N, N=FF, K=D, b_lay="cs", b_l=l, b_ns=N_CHIPS, out_dtype=BF16, epi=_relu2)
        y, h = _mm("mlp_down", r, wg[W2], mode="nn", M=N, N=D, K=FF, b_lay="rs", b_l=l, b_ns=N_CHIPS,
                   out_dtype=(BF16, F32), epi=_gated_residual, extras=[("mn", h), ("ex", gate)], rows_per_ex=S)
        sv["u1"], sv["r"], sv["y1"] = u, r, y
        saved.append(sv)
        if l == n_a - 1:
            h_kv = h
            u_kv = _normmod(h, kv_g.reshape(1, D), kv_scale, kv_shift, B)
            half = N_CHIPS // 2
            k_all = _mm("kv_proj", u_kv, wg[KV], mode="nn", M=N, N=3 * D, K=D, b_lay="cs", b_s0=0, b_ns=half).reshape(B, S, 3 * D)
            v_all = _mm("kv_proj", u_kv, wg[KV], mode="nn", M=N, N=3 * D, K=D, b_lay="cs", b_s0=half, b_ns=half).reshape(B, S, 3 * D)

    loss_buf, dh, d_final_g = _loss_head(h, final_g.reshape(1, D), loss_target.reshape(N, D))
    loss = lax.psum(loss_buf[0, 0], ("x", "y", "c"))

    dg = [lax.empty(w.shape, BF16) for w in wg]
    recv = [lax.empty(w.shape, BF16) for w in wg]

    def send(pieces):
        return _layer_moves("scatter", dg, recv, pieces)

    send_with = {l: [(W1, l), (W2, l)] for l in range(depth)}
    for l in range(n_a):
        send_with[l] += [(GLU, l)]
    for j in range(n_b):
        send_with[n_a + j] += [(O, j)]
        send_with[n_a + j - 1] += [(Q, j)]
    send_with[n_a - 1] += [(KV, 0)]
    d_ln_g = [[None, None] for _ in range(depth)]
    d_mods = [[None, None] for _ in range(depth)]
    d_s5 = [None] * n_a
    dk_acc = dv_acc = None
    half = N_CHIPS // 2

    def tn_grad(name, a, d, into, l, Mr, Nc, lay, s0=0, ns=N_CHIPS):
        return _mm(name, a, _as4(d), mode="tn", M=Mr, N=Nc, K=N, b_lay="cs", out_dtype=BF16, out_lay=lay,
                   out4_shape=into.shape, out_into=into, out_l=l, out_s0=s0, out_ns=ns)

    for l in reversed(range(depth)):
        sv = saved[l]
        dy, d_gate1 = _residual_bwd(dh, sv["gate1"], sv["y1"], B)
        dg[W2] = tn_grad("mlp_down_dw", sv["r"], dy, dg[W2], l, FF, D, "rs")
        da = _mm("mlp_down_dx", dy, wg[W2], mode="nt", M=N, N=FF, K=D, b_lay="rs", b_l=l, b_ns=N_CHIPS, out_dtype=BF16,
                 epi=_relu2_bwd, extras=[("mn", sv["r"])])
        dg[W1] = tn_grad("mlp_up_dw", sv["u1"], da, dg[W1], l, D, FF, "cs")
        du = _mm("mlp_up_dx", da, wg[W1], mode="nt", M=N, N=D, K=FF, b_lay="cs", b_l=l, b_ns=N_CHIPS)
        dh, dgv, d_scale1, d_shift1 = _normmod_bwd(du, sv["h1"], ln_g_full[l, 1].reshape(1, D), sv["scale1"], dh, B)
        d_ln_g[l][1] = dgv
        d_mods[l][1] = jnp.concatenate([d_shift1, d_scale1, d_gate1], axis=2)
        dy, d_gate0 = _residual_bwd(dh, sv["gate0"], sv["y0"], B)
        if l < n_a:
            bd, cd, coef_f, coef_b, disc_vjp = s5_tabs[l]
            dzz = _glu_bwd(dy, sv["zz"])
            dg[GLU] = tn_grad("glu_proj_dw", sv["z"], dzz, dg[GLU], l, D, 2 * D, "cs")
            dz = _mm("glu_proj_dx", dzz, wg[GLU], mode="nt", M=N, N=D, K=2 * D, b_lay="cs", b_l=l, b_ns=N_CHIPS)
            carry, used = send(send_with[l])
            du, d_bd, d_cd, d_a2, d_dskip, moved = _s5_bwd(sv["u0"], dz, bd, cd, coef_f, coef_b,
                                                           ssm_d_full[l].reshape(1, D), sv["carries"], B, carry)
            landed(recv, used, moved)
            d_are = (d_a2[:, 0, :CHUNK_STATE] + d_a2[:, 0, CHUNK_STATE:]).reshape(-1, SSM_STATE)
            d_aim = (d_a2[:, 1, CHUNK_STATE:] - d_a2[:, 1, :CHUNK_STATE]).reshape(-1, SSM_STATE)
            d_s5[l] = disc_vjp((d_bd, d_cd, d_are, d_aim)) + (d_dskip,)
        else:
            j = l - n_a
            dg[O] = tn_grad("o_proj_dw", sv["o"].reshape(N, D), dy, dg[O], j, D, D, "rs")
            do = _mm("o_proj_dx", dy, wg[O], mode="nt", M=N, N=D, K=D, b_lay="rs", b_l=j, b_ns=N_CHIPS)
            carry, used = send(send_with[l])
            dq, dk_acc, dv_acc, moved = _attn_bwd(sv["q"], k_all, v_all, sv["o"], sv["lse"], do.reshape(B, S, D),
                                                  dk_acc, dv_acc, B, carry)
            landed(recv, used, moved)
            dq2 = dq.reshape(N, 3 * D)
            dg[Q] = tn_grad("q_proj_dw", sv["u0"], dq2, dg[Q], j, D, 3 * D, "cs")
            du = _mm("q_proj_dx", dq2, wg[Q], mode="nt", M=N, N=D, K=3 * D, b_lay="cs", b_l=j, b_ns=N_CHIPS)
        dh, dgv, d_scale0, d_shift0 = _normmod_bwd(du, sv["h0"], ln_g_full[l, 0].reshape(1, D), sv["scale0"], dh, B)
        d_ln_g[l][0] = dgv
        d_mods[l][0] = jnp.concatenate([d_shift0, d_scale0, d_gate0], axis=2)
        if l == n_a:
            dk2, dv2 = dk_acc.reshape(N, 3 * D), dv_acc.reshape(N, 3 * D)
            dg[KV] = tn_grad("kv_proj_dw", u_kv, dk2, dg[KV], 0, D, 3 * D, "cs", s0=0, ns=half)
            dg[KV] = tn_grad("kv_proj_dw", u_kv, dv2, dg[KV], 0, D, 3 * D, "cs", s0=half, ns=half)
            du_kv = _mm("kv_proj_dx", dk2, wg[KV], mode="nt", M=N, N=D, K=3 * D, b_lay="cs", b_s0=0, b_ns=half)
            du_kv = _mm("kv_proj_dx", dv2, wg[KV], mode="nt", M=N, N=D, K=3 * D, b_lay="cs", b_s0=half, b_ns=half,
                        epi=_add, extras=[("mn", du_kv)])
            dh, d_kv_g, d_kv_scale, d_kv_shift = _normmod_bwd(du_kv, h_kv, kv_g.reshape(1, D), kv_scale, dh, B)
    grad_x = dh.reshape(B, S, D)

    own = [_sum_shards(r.reshape(N_CHIPS, -1, r.shape[-1])) for r in recv]
    other = _swap_with_sibling(own)

    d_kvmod = jnp.concatenate([d_kv_shift, d_kv_scale], axis=2).reshape(B, 2 * D)
    d_mod_all = jnp.concatenate([d_mods[l][s].reshape(B, 3 * D) for l in range(depth) for s in range(2)], axis=1)
    small = [
        d_mod_all, d_kvmod,
        jnp.stack([jnp.stack([d_ln_g[l][0].reshape(D), d_ln_g[l][1].reshape(D)]) for l in range(depth)]),
        jnp.stack([d_s5[l][0] for l in range(n_a)]), jnp.stack([d_s5[l][1] for l in range(n_a)]),
        jnp.stack([d_s5[l][2] for l in range(n_a)]),
        jnp.stack([d_s5[l][3] for l in range(n_a)]), jnp.stack([d_s5[l][4] for l in range(n_a)]),
        jnp.stack([d_s5[l][5] for l in range(n_a)]), jnp.stack([d_s5[l][6] for l in range(n_a)]),
        jnp.stack([d_s5[l][7].reshape(D) for l in range(n_a)]),
        d_kv_g.reshape(D), d_final_g.reshape(D),
    ]
    small_pack, small_layout, small_rows = _pack(small, D)
    small_buf = _all_gather_small(small_pack)
    small_sum = _sum_shards(small_buf.reshape(N_DEV, small_rows, D))
    red = lambda idx: _unpack(small_sum, small_layout, idx)
    per_dev = lambda idx: jnp.concatenate(
        [_unpack(small_buf[d * small_rows:(d + 1) * small_rows], small_layout, idx) for d in range(N_DEV)], axis=0)

    dm_all = per_dev(0).reshape(n_ex, n_mod, 3 * D)
    dm_cols = lax.dynamic_slice_in_dim(dm_all, chip * ada_cols, ada_cols, axis=2).reshape(n_ex, n_mod * ada_cols)
    g_ada_w = _mm("ada_dw", sc_all, _as4(dm_cols), mode="tn", M=D, N=n_mod * ada_cols, K=n_ex, b_lay="cs",
                  out_lay="cs", out4_shape=(n_mod, 1, D, ada_cols), out_ns=n_mod).reshape(ada_w.shape)
    dkvm_all = per_dev(1)
    dkvm_cols = lax.dynamic_slice_in_dim(dkvm_all, chip * kv_cols, kv_cols, axis=1)
    g_kv_ada_w = _mm("ada_dw", sc_all, _as4(dkvm_cols), mode="tn", M=D, N=kv_cols, K=n_ex, b_lay="cs")
    g_ada_b_full = (red(0)[0] + red(0)[1]).reshape(depth, 2, 3 * D) if B == 2 else jnp.sum(red(0), axis=0).reshape(depth, 2, 3 * D)
    g_ada_b = lax.dynamic_slice_in_dim(g_ada_b_full, chip * ada_cols, ada_cols, axis=2)
    g_kv_ada_b = red(1)[0] + red(1)[1] if B == 2 else jnp.sum(red(1), axis=0)
    g_ln_g = lax.dynamic_slice_in_dim(red(2), chip * (D // N_CHIPS), D // N_CHIPS, axis=2)
    g_ssm_d = lax.dynamic_slice_in_dim(red(10), chip * (D // N_CHIPS), D // N_CHIPS, axis=1)
    small_grads = {
        "ln_g": g_ln_g, "ada_b": g_ada_b, "ssm_lam_re": red(3), "ssm_lam_im": red(4), "ssm_log_dt": red(5),
        "ssm_b_re": red(6), "ssm_b_im": red(7), "ssm_c_re": red(8), "ssm_c_im": red(9), "ssm_d": g_ssm_d,
        "kv_g": red(11), "kv_ada_b": g_kv_ada_b, "final_g": red(12),
    }
    small_w = {"ln_g": (ln_g, m_ln_g, v_ln_g), "ada_b": (ada_b, m_ada_b, v_ada_b),
               "ssm_lam_re": (ssm_lam_re, m_ssm_lam_re, v_ssm_lam_re), "ssm_lam_im": (ssm_lam_im, m_ssm_lam_im, v_ssm_lam_im),
               "ssm_log_dt": (ssm_log_dt, m_ssm_log_dt, v_ssm_log_dt), "ssm_b_re": (ssm_b_re, m_ssm_b_re, v_ssm_b_re),
               "ssm_b_im": (ssm_b_im, m_ssm_b_im, v_ssm_b_im), "ssm_c_re": (ssm_c_re, m_ssm_c_re, v_ssm_c_re),
               "ssm_c_im": (ssm_c_im, m_ssm_c_im, v_ssm_c_im), "ssm_d": (ssm_d, m_ssm_d, v_ssm_d),
               "kv_g": (kv_g, m_kv_g, v_kv_g), "kv_ada_b": (kv_ada_b, m_kv_ada_b, v_kv_ada_b),
               "final_g": (final_g, m_final_g, v_final_g)}
    names = list(small_w)
    wp, lay_w, _ = _pack([small_w[n][0] for n in names], D)
    gp, _, _ = _pack([small_grads[n] for n in names], D)
    mp, _, _ = _pack([small_w[n][1] for n in names], D)
    vp, _, _ = _pack([small_w[n][2] for n in names], D)
    _, d_p, m_p, v_p = _adamw(wp, [gp], mp, vp)
    upd = {n: (small_grads[n].reshape(small_w[n][0].shape), _unpack(d_p, lay_w, i), _unpack(m_p, lay_w, i), _unpack(v_p, lay_w, i))
           for i, n in enumerate(names)}

    def big(w, m, v, g_own, g_other=None):
        C = w.shape[-1]
        gs = [g_own.reshape(-1, C)] + ([g_other.reshape(-1, C)] if g_other is not None else [])
        return tuple(t.reshape(w.shape) for t in _adamw(w.reshape(-1, C), gs, m.reshape(-1, C), v.reshape(-1, C)))

    upd["ssm_w_glu"] = big(ssm_w_glu, m_ssm_w_glu, v_ssm_w_glu, own[0], other[0])
    upd["w_kv"] = big(w_kv, m_w_kv, v_w_kv, own[1], other[1])
    upd["attn_w_q"] = big(attn_w_q, m_attn_w_q, v_attn_w_q, own[2], other[2])
    upd["attn_w_o"] = big(attn_w_o, m_attn_w_o, v_attn_w_o, own[3], other[3])
    upd["mlp_w1"] = big(mlp_w1, m_mlp_w1, v_mlp_w1, own[4], other[4])
    upd["mlp_w2"] = big(mlp_w2, m_mlp_w2, v_mlp_w2, own[5], other[5])
    upd["ada_w"] = big(ada_w, m_ada_w, v_ada_w, g_ada_w)
    upd["kv_ada_w"] = big(kv_ada_w, m_kv_ada_w, v_kv_ada_w, g_kv_ada_w)

    order = ["ln_g", "ada_w", "ada_b", "ssm_lam_re", "ssm_lam_im", "ssm_log_dt", "ssm_b_re", "ssm_b_im", "ssm_c_re",
             "ssm_c_im", "ssm_d", "ssm_w_glu", "kv_g", "kv_ada_w", "kv_ada_b", "w_kv", "attn_w_q", "attn_w_o", "mlp_w1",
             "mlp_w2", "final_g"]
    return (loss, grad_x, *[upd[n][0] for n in order], *[upd[n][1] for n in order], *[upd[n][2] for n in order],
            *[upd[n][3] for n in order])
```

```python
import functools
import math

import jax
import jax.numpy as jnp
from jax import lax
from jax.experimental import pallas as pl
from jax.experimental.pallas import tpu as pltpu

F32 = jnp.float32
BF16 = jnp.bfloat16
MESH = pl.DeviceIdType.MESH

EPS = 1e-6
NEG = -1e30
SSM_GROUP = 16
SSM_STATE = 64
HEAD_DIM = 64
ATTN_BLOCK = 128
DILATIONS = (1, 4, 16)
ADAM_LR, ADAM_B1, ADAM_B2, ADAM_EPS, ADAM_WD, ADAM_STEP = 0.001, 0.9, 0.999, 1e-08, 0.01, 10

LANES = 128
SUBLANES = 8
CHUNK_GROUPS = LANES // SSM_GROUP
CHUNK_STATE = CHUNK_GROUPS * SSM_STATE
VMEM_LIMIT = 56 * 1024 * 1024


def _div(dim, pref, mult):
    t = min(pref, dim) // mult * mult
    while t >= mult:
        if dim % t == 0:
            return t
        t -= mult
    return dim


def _params(*sem):
    return pltpu.CompilerParams(dimension_semantics=sem, vmem_limit_bytes=VMEM_LIMIT)


def _coords():
    return lax.axis_index("x"), lax.axis_index("y"), lax.axis_index("c")


def _other_chips(cx, cy):
    return [(1 - cx, cy), (cx, 1 - cy), (1 - cx, 1 - cy)]


class _Carry:
    def __init__(self, srcs, dsts, plan, n_remote, n_local, onward=None, n_onward=0):
        self.srcs, self.dsts, self.plan, self.n_remote, self.n_local = list(srcs), list(dsts), plan, n_remote, n_local
        self.onward, self.n_onward = onward, n_onward


def _carried_call(body, *, name, grid, in_specs, out_specs, out_shape, scratch_shapes, operands, sem, carry=None):
    if carry is None:
        outs = pl.pallas_call(body, name=name, grid=grid, in_specs=in_specs, out_specs=out_specs, out_shape=out_shape,
                              scratch_shapes=scratch_shapes, compiler_params=_params(*sem))(*operands)
        return list(outs), []
    n_in, n_out, n_scr = len(in_specs), len(out_specs), len(scratch_shapes)
    ns, nd = len(carry.srcs), len(carry.dsts)

    def wrapped(*refs):
        base_in, src_refs = refs[:n_in], refs[n_in:n_in + ns]
        o0 = n_in + ns + nd
        base_out, dst_refs = refs[o0:o0 + n_out], refs[o0 + n_out:o0 + n_out + nd]
        s0 = o0 + n_out + nd
        base_scr = refs[s0:s0 + n_scr]
        send_sems, recv_sems, local_sems = refs[s0 + n_scr:]
        pids = [pl.program_id(a) for a in range(len(grid))]
        first = functools.reduce(jnp.logical_and, [p == 0 for p in pids])
        last = functools.reduce(jnp.logical_and, [p == g - 1 for p, g in zip(pids, grid)])

        def remote_copies(moves, k0):
            return [pltpu.make_async_remote_copy(src_ref=s, dst_ref=d, send_sem=send_sems.at[k0 + i], recv_sem=recv_sems.at[k0 + i],
                                                 device_id=peer, device_id_type=MESH) for i, (s, d, peer) in enumerate(moves)]

        def copies():
            remote, local = carry.plan(src_refs, dst_refs, _coords())
            return remote_copies(remote, 0), [pltpu.make_async_copy(s, d, local_sems.at[i]) for i, (s, d) in enumerate(local)]

        @pl.when(first)
        def _():
            remote, local = copies()
            for cp in local + remote:
                cp.start()

        body(*base_in, *base_out, *base_scr)

        @pl.when(last)
        def _():
            remote, local = copies()
            for cp in remote:
                cp.wait_send()
                cp.wait_recv()
            for cp in local:
                cp.wait()
            if carry.onward is not None:
                second = remote_copies(carry.onward(src_refs, dst_refs, _coords()), carry.n_remote)
                for cp in second:
                    cp.start()
                for cp in second:
                    cp.wait_send()
                    cp.wait_recv()

    anyspec = pl.BlockSpec(memory_space=pl.ANY)
    outs = pl.pallas_call(
        wrapped, name=name, grid=grid, in_specs=list(in_specs) + [anyspec] * (ns + nd),
        out_specs=list(out_specs) + [anyspec] * nd,
        out_shape=list(out_shape) + [jax.ShapeDtypeStruct(d.shape, d.dtype) for d in carry.dsts],
        scratch_shapes=list(scratch_shapes) + [pltpu.SemaphoreType.DMA((carry.n_remote + carry.n_onward,)),
                                               pltpu.SemaphoreType.DMA((carry.n_remote + carry.n_onward,)),
                                               pltpu.SemaphoreType.DMA((max(carry.n_local, 1),))],
        input_output_aliases={n_in + ns + i: n_out + i for i in range(nd)},
        compiler_params=_params(*(["arbitrary"] * len(grid))),
    )(*operands, *carry.srcs, *carry.dsts)
    return list(outs[:n_out]), list(outs[n_out:])


def _mm(name, a, b4, *, mode, M, N, K, b_lay="cs", b_l=0, b_s0=0, b_ns=1, out_dtype=F32, out_lay=None, out4_shape=None,
        out_into=None, out_l=0, out_s0=0, out_ns=1, epi=None, extras=(), rows_per_ex=None, tm=1024, tn=1024, tk=1024):
    _, _, bR, bC = b4.shape
    tm = _div(M, tm, SUBLANES if M % 16 else 16)
    brows, bcols = (N, K) if mode == "nt" else (K, N)
    if b_lay == "cs":
        assert bR == brows and bC * b_ns == bcols, (name, b4.shape, brows, bcols)
    else:
        assert bC == bcols and bR * b_ns == brows, (name, b4.shape, brows, bcols)
    n_lim = N
    k_lim = K
    if mode == "nt":
        if b_lay == "cs":
            k_lim = bC
        else:
            n_lim = bR
    else:
        if b_lay == "cs":
            n_lim = bC
        else:
            k_lim = bR
    if out_lay == "cs":
        oR, oC = out4_shape[2], out4_shape[3]
        assert oR == M and oC * out_ns == N, (name, out4_shape, M, N)
        n_lim = math.gcd(n_lim, oC)
    elif out_lay == "rs":
        oR, oC = out4_shape[2], out4_shape[3]
        assert oC == N and oR * out_ns == M, (name, out4_shape, M, N)
        tm = _div(oR, tm, SUBLANES)
    tn = _div(n_lim, tn, LANES)
    tk = _div(k_lim, tk, LANES if mode != "tn" else SUBLANES)
    if mode == "tn":
        tk = _div(k_lim, tk, 16) if k_lim % 16 == 0 else tk
    nk = K // tk
    grid = (M // tm, N // tn, nk)

    if mode == "tn":
        a_spec = pl.BlockSpec((tk, tm), lambda i, j, k: (k, i))
    else:
        a_spec = pl.BlockSpec((tm, tk), lambda i, j, k: (i, k))

    def b_index(ri, ci, br, bc):
        if b_lay == "cs":
            per = bC // bc
            return (b_s0 + ci // per, b_l, ri, ci % per)
        per = bR // br
        return (b_s0 + ri // per, b_l, ri % per, ci)

    if mode == "nt":
        b_spec = pl.BlockSpec((None, None, tn, tk), lambda i, j, k: b_index(j, k, tn, tk))
    else:
        b_spec = pl.BlockSpec((None, None, tk, tn), lambda i, j, k: b_index(k, j, tk, tn))

    in_specs = [a_spec, b_spec]
    operands = [a, b4]
    for kind, arr in extras:
        if kind == "mn":
            in_specs.append(pl.BlockSpec((tm, tn), lambda i, j, k: (i, j)))
        elif kind == "ex":
            per_ex = rows_per_ex // tm
            in_specs.append(pl.BlockSpec((None, 1, tn), lambda i, j, k: (i // per_ex, 0, j)))
        else:
            in_specs.append(pl.BlockSpec((1, tn), lambda i, j, k: (0, j)))
        operands.append(arr)
    n_extra = len(extras)

    multi = isinstance(out_dtype, tuple)
    n_out = len(out_dtype) if multi else 1
    if out_lay is None:
        out_shape = [jax.ShapeDtypeStruct((M, N), dt) for dt in (out_dtype if multi else (out_dtype,))]
        out_spec = [pl.BlockSpec((tm, tn), lambda i, j, k: (i, j)) for _ in range(n_out)]
    else:
        out_shape = [jax.ShapeDtypeStruct(tuple(out4_shape), out_dtype)]
        if out_lay == "cs":
            per_o = oC // tn
            out_spec = [pl.BlockSpec((None, None, tm, tn), lambda i, j, k: (out_s0 + j // per_o, out_l, i, j % per_o))]
        else:
            per_o = oR // tm
            out_spec = [pl.BlockSpec((None, None, tm, tn), lambda i, j, k: (out_s0 + i // per_o, out_l, i % per_o, j))]
    aliases = {}
    if out_into is not None:
        in_specs.append(pl.BlockSpec(memory_space=pl.ANY))
        operands.append(out_into)
        aliases = {len(operands) - 1: 0}

    dims = {"nn": (((1,), (0,)), ((), ())), "nt": (((1,), (1,)), ((), ())), "tn": (((0,), (0,)), ((), ()))}[mode]

    def body(a_ref, b_ref, *rest):
        extra_refs = rest[:n_extra]
        o_refs = rest[len(rest) - n_out - (nk > 1):len(rest) - (nk > 1)]

        def finish(r):
            if epi is not None:
                r = epi(r, *[e[...] for e in extra_refs])
            for o_ref, val in zip(o_refs, r if multi else (r,)):
                o_ref[...] = val.astype(o_ref.dtype)

        part = lax.dot_general(a_ref[...].astype(BF16), b_ref[...].astype(BF16), dims, preferred_element_type=F32)
        if nk == 1:
            finish(part)
            return
        acc = rest[-1]
        k = pl.program_id(2)

        @pl.when(k == 0)
        def _():
            acc[...] = part

        @pl.when(k != 0)
        def _():
            acc[...] += part

        @pl.when(k == nk - 1)
        def _():
            finish(acc[...])

    outs = pl.pallas_call(
        body, name=name, grid=grid, in_specs=in_specs, out_specs=out_spec, out_shape=out_shape,
        scratch_shapes=[pltpu.VMEM((tm, tn), F32)] if nk > 1 else [], input_output_aliases=aliases,
        compiler_params=_params("parallel", "parallel", "arbitrary"),
    )(*operands)
    return tuple(outs) if multi else outs[0]


def _as4(w):
    return w.reshape((1, 1) + w.shape)


def _relu2(acc):
    r = jnp.maximum(acc, 0.0)
    return r * r


def _relu2_bwd(acc, r):
    return acc * (2.0 * jnp.sqrt(r.astype(F32)))


def _add(acc, e):
    return acc + e


def _gated_residual(acc, h, gate):
    return acc, h + gate * acc


def _row_tiles(N, B, pref=256):
    S = N // B
    tm = _div(S, pref, SUBLANES)
    return tm, S // tm


def _normmod(h, g, scale, shift, B):
    N, D = h.shape
    tm, per_ex = _row_tiles(N, B)

    def body(h_ref, g_ref, sc_ref, sh_ref, u_ref):
        x = h_ref[...]
        rstd = lax.rsqrt(jnp.mean(x * x, axis=-1, keepdims=True) + EPS)
        y = (x * rstd) * g_ref[...]
        u_ref[...] = (y * (1.0 + sc_ref[...]) + sh_ref[...]).astype(u_ref.dtype)

    tok = pl.BlockSpec((tm, D), lambda i: (i, 0))
    vec = pl.BlockSpec((1, D), lambda i: (0, 0))
    ex = pl.BlockSpec((None, 1, D), lambda i: (i // per_ex, 0, 0))
    return pl.pallas_call(
        body, name="normmod_fwd", grid=(N // tm,), in_specs=[tok, vec, ex, ex], out_specs=tok,
        out_shape=jax.ShapeDtypeStruct((N, D), BF16), compiler_params=_params("parallel"),
    )(h, g, scale, shift)


def _normmod_bwd(du, h, g, scale, dh_in, B):
    N, D = h.shape
    tm, per_ex = _row_tiles(N, B)

    def body(du_ref, h_ref, g_ref, sc_ref, dhin_ref, dh_ref, dg_ref, dsc_ref, dsh_ref):
        i = pl.program_id(0)
        x = h_ref[...]
        gv = g_ref[...]
        d_u = du_ref[...].astype(F32)
        rstd = lax.rsqrt(jnp.mean(x * x, axis=-1, keepdims=True) + EPS)
        xn = x * rstd
        dyg = d_u * (1.0 + sc_ref[...])
        dxn = dyg * gv
        dh_ref[...] = dhin_ref[...] + rstd * (dxn - xn * jnp.mean(dxn * xn, axis=-1, keepdims=True))
        dsh_t = jnp.sum(d_u, axis=0, keepdims=True)
        dsc_t = jnp.sum(d_u * (xn * gv), axis=0, keepdims=True)
        dg_t = jnp.sum(dyg * xn, axis=0, keepdims=True)

        @pl.when(i % per_ex == 0)
        def _():
            dsc_ref[...] = dsc_t
            dsh_ref[...] = dsh_t

        @pl.when(i % per_ex != 0)
        def _():
            dsc_ref[...] += dsc_t
            dsh_ref[...] += dsh_t

        @pl.when(i == 0)
        def _():
            dg_ref[...] = dg_t

        @pl.when(i != 0)
        def _():
            dg_ref[...] += dg_t

    tok = pl.BlockSpec((tm, D), lambda i: (i, 0))
    vec = pl.BlockSpec((1, D), lambda i: (0, 0))
    ex = pl.BlockSpec((None, 1, D), lambda i: (i // per_ex, 0, 0))
    return pl.pallas_call(
        body, name="normmod_bwd", grid=(N // tm,), in_specs=[tok, tok, vec, ex, tok], out_specs=[tok, vec, ex, ex],
        out_shape=[jax.ShapeDtypeStruct((N, D), F32), jax.ShapeDtypeStruct((1, D), F32),
                   jax.ShapeDtypeStruct((B, 1, D), F32), jax.ShapeDtypeStruct((B, 1, D), F32)],
        compiler_params=_params("arbitrary"),
    )(du, h, g, scale, dh_in)


def _residual_bwd(dh, gate, y, B):
    N, D = dh.shape
    tm, per_ex = _row_tiles(N, B)

    def body(dh_ref, gt_ref, y_ref, dy_ref, dgt_ref):
        i = pl.program_id(0)
        d = dh_ref[...]
        dy_ref[...] = (gt_ref[...] * d).astype(dy_ref.dtype)
        t = jnp.sum(d * y_ref[...], axis=0, keepdims=True)

        @pl.when(i % per_ex == 0)
        def _():
            dgt_ref[...] = t

        @pl.when(i % per_ex != 0)
        def _():
            dgt_ref[...] += t

    tok = pl.BlockSpec((tm, D), lambda i: (i, 0))
    ex = pl.BlockSpec((None, 1, D), lambda i: (i // per_ex, 0, 0))
    return pl.pallas_call(
        body, name="residual_bwd", grid=(N // tm,), in_specs=[tok, ex, tok], out_specs=[tok, ex],
        out_shape=[jax.ShapeDtypeStruct((N, D), BF16), jax.ShapeDtypeStruct((B, 1, D), F32)],
        compiler_params=_params("arbitrary"),
    )(dh, gate, y)


def _glu_residual(zz, h, gate, B):
    N, D2 = zz.shape
    D = D2 // 2
    tm, per_ex = _row_tiles(N, B)

    def body(v_ref, g_ref, h_ref, gt_ref, y_ref, o_ref):
        y = v_ref[...] * jax.nn.sigmoid(g_ref[...])
        y_ref[...] = y.astype(y_ref.dtype)
        o_ref[...] = h_ref[...] + gt_ref[...] * y

    tok = pl.BlockSpec((tm, D), lambda i: (i, 0))
    return pl.pallas_call(
        body, name="glu_fwd", grid=(N // tm,),
        in_specs=[tok, pl.BlockSpec((tm, D), lambda i: (i, 1)), tok,
                  pl.BlockSpec((None, 1, D), lambda i: (i // per_ex, 0, 0))],
        out_specs=[tok, tok], out_shape=[jax.ShapeDtypeStruct((N, D), BF16), jax.ShapeDtypeStruct((N, D), F32)],
        compiler_params=_params("parallel"),
    )(zz, zz, h, gate)


def _glu_bwd(dy, zz):
    N, D2 = zz.shape
    D = D2 // 2
    tm = _div(N, 256, SUBLANES)

    def body(dy_ref, v_ref, g_ref, o_ref):
        d = dy_ref[...].astype(F32)
        s = jax.nn.sigmoid(g_ref[...])
        o_ref[...] = jnp.concatenate([d * s, d * v_ref[...] * s * (1.0 - s)], axis=1).astype(o_ref.dtype)

    return pl.pallas_call(
        body, name="glu_bwd", grid=(N // tm,),
        in_specs=[pl.BlockSpec((tm, D), lambda i: (i, 0)), pl.BlockSpec((tm, D), lambda i: (i, 0)),
                  pl.BlockSpec((tm, D), lambda i: (i, 1))],
        out_specs=pl.BlockSpec((tm, D2), lambda i: (i, 0)), out_shape=jax.ShapeDtypeStruct((N, D2), BF16),
        compiler_params=_params("parallel"),
    )(dy, zz, zz)


def _loss_head(h, g, target):
    N, D = h.shape
    tm = _div(N, 256, SUBLANES)

    def body(h_ref, g_ref, t_ref, loss_ref, dh_ref, dg_ref):
        i = pl.program_id(0)
        x = h_ref[...]
        gv = g_ref[...]
        rstd = lax.rsqrt(jnp.mean(x * x, axis=-1, keepdims=True) + EPS)
        xn = x * rstd
        err = xn * gv - t_ref[...]
        part = 0.5 * jnp.sum(jnp.sum(err * err, axis=-1, keepdims=True) / D, axis=0, keepdims=True)
        dy = err / D
        dxn = dy * gv
        dh_ref[...] = rstd * (dxn - xn * jnp.mean(dxn * xn, axis=-1, keepdims=True))
        dg_t = jnp.sum(dy * xn, axis=0, keepdims=True)
        part = jnp.broadcast_to(part, loss_ref.shape)

        @pl.when(i == 0)
        def _():
            loss_ref[...] = part
            dg_ref[...] = dg_t

        @pl.when(i != 0)
        def _():
            loss_ref[...] += part
            dg_ref[...] += dg_t

    tok = pl.BlockSpec((tm, D), lambda i: (i, 0))
    vec = pl.BlockSpec((1, D), lambda i: (0, 0))
    return pl.pallas_call(
        body, name="loss_head", grid=(N // tm,), in_specs=[tok, vec, tok],
        out_specs=[pl.BlockSpec((SUBLANES, LANES), lambda i: (0, 0)), tok, vec],
        out_shape=[jax.ShapeDtypeStruct((SUBLANES, LANES), F32), jax.ShapeDtypeStruct((N, D), F32),
                   jax.ShapeDtypeStruct((1, D), F32)],
        compiler_params=_params("arbitrary"),
    )(h, g, target)


def _swap_halves(x):
    half = x.shape[-1] // 2
    return jnp.concatenate([x[:, half:], x[:, :half]], axis=1)


def _gelu(y):
    return jax.nn.gelu(y)


def _gelu_grad(y):
    c0 = math.sqrt(2.0 / math.pi)
    inner = c0 * (y + 0.044715 * y * y * y)
    t = jnp.tanh(inner)
    return 0.5 * (1.0 + t) + 0.5 * y * (1.0 - t * t) * c0 * (1.0 + 3.0 * 0.044715 * y * y)


def _s5_discretize(lam_re, lam_im, log_dt, b_re, b_im, c_re, c_im):
    G = lam_re.shape[0]
    nch = G // CHUNK_GROUPS
    dt = jnp.exp(log_dt)[:, None]
    er = jnp.exp(lam_re * dt)
    a_re = er * jnp.cos(lam_im * dt)
    a_im = er * jnp.sin(lam_im * dt)
    den = lam_re * lam_re + lam_im * lam_im
    n_re, n_im = a_re - 1.0, a_im
    f_re = (n_re * lam_re + n_im * lam_im) / den
    f_im = (n_im * lam_re - n_re * lam_im) / den
    bb_re = f_re[..., None] * b_re - f_im[..., None] * b_im
    bb_im = f_re[..., None] * b_im + f_im[..., None] * b_re
    eye = jnp.eye(CHUNK_GROUPS, dtype=F32)

    def pack_b(bb):
        bb = bb.reshape(nch, CHUNK_GROUPS, SSM_STATE, SSM_GROUP)
        return jnp.einsum("jgpc,gh->jgchp", bb, eye).reshape(nch, LANES, CHUNK_STATE)

    def pack_c(cc):
        cc = cc.reshape(nch, CHUNK_GROUPS, SSM_GROUP, SSM_STATE)
        return jnp.einsum("jgcp,gh->jgphc", cc, eye).reshape(nch, CHUNK_STATE, LANES)

    bd = jnp.concatenate([pack_b(bb_re), pack_b(bb_im)], axis=2)
    cd = jnp.concatenate([pack_c(c_re), pack_c(-c_im)], axis=1)
    return bd, cd, a_re, a_im


S5_TILE = 256
S5_SEG = S5_TILE // SUBLANES
S5_UNROLL = 4


def _s5_scan_coefs(lam_re, lam_im, log_dt, seg):
    G = lam_re.shape[0]
    nch = G // CHUNK_GROUPS
    dt = jnp.exp(log_dt)[:, None]

    def power(k):
        er = jnp.exp(k * lam_re * dt)
        re = (er * jnp.cos(k * lam_im * dt)).reshape(nch, 1, CHUNK_STATE)
        im = (er * jnp.sin(k * lam_im * dt)).reshape(nch, 1, CHUNK_STATE)
        return jnp.concatenate([re, re], axis=2), jnp.concatenate([-im, im], axis=2)

    rows = [power(i + 1) for i in range(seg)]
    pw = jnp.stack([jnp.concatenate([r for r, _ in rows], axis=1), jnp.concatenate([i for _, i in rows], axis=1)], axis=1)
    row = jnp.arange(SUBLANES, dtype=jnp.int32)[None, :, None]

    def table(reverse):
        tabs = []
        for s in (1, 2, 4):
            re, im = power(s * seg)
            mask = (row < SUBLANES - s) if reverse else (row >= s)
            tabs += [jnp.where(mask, re, 0.0), jnp.where(mask, -im if reverse else im, 0.0)]
        return jnp.stack([jnp.broadcast_to(t, (nch, SUBLANES, 2 * CHUNK_STATE)) for t in tabs], axis=1)

    return pw, table(False), table(True)


def _to_segments(dst_s, src_ref, seg):
    for j in range(SUBLANES):
        dst_s[pl.ds(j, seg, stride=SUBLANES), :] = src_ref[pl.ds(j * seg, seg), :].astype(F32)


def _from_segments(dst_ref, src_s, seg):
    for j in range(SUBLANES):
        dst_ref[pl.ds(j * seg, seg), :] = src_s[pl.ds(j, seg, stride=SUBLANES), :].astype(dst_ref.dtype)


def _seg_scan(x_ref, pw_ref, seg_ref, carry_ref, c_ref, seg, reverse):
    W = x_ref.shape[-1]
    tm = x_ref.shape[0]
    sgn = -1.0 if reverse else 1.0
    ar = jnp.broadcast_to(pw_ref[0, 0:1, :], (SUBLANES, W))
    ai = sgn * jnp.broadcast_to(pw_ref[1, 0:1, :], (SUBLANES, W))

    def rows(i):
        return pl.ds(pl.multiple_of(i * SUBLANES, SUBLANES), SUBLANES)

    def step(t, prev):
        i = (seg - 2 - t) if reverse else (t + 1)
        x = x_ref[rows(i), :] + ar * prev + ai * _swap_halves(prev)
        x_ref[rows(i), :] = x
        return x

    start = (seg - 1) * SUBLANES if reverse else 0
    edge = lax.fori_loop(0, seg - 1, step, x_ref[start:start + SUBLANES, :], unroll=S5_UNROLL)
    row = lax.broadcasted_iota(jnp.int32, (SUBLANES, W), 0)
    if reverse:
        f = jnp.where(row == SUBLANES - 1, carry_ref[...], pltpu.roll(edge, SUBLANES - 1, 0))
    else:
        f = jnp.where(row == 0, carry_ref[...], pltpu.roll(edge, 1, 0))
    for si, s in enumerate((1, 2, 4)):
        fs = pltpu.roll(f, (SUBLANES - s) if reverse else s, 0)
        f = f + seg_ref[2 * si] * fs + seg_ref[2 * si + 1] * _swap_halves(fs)
    c_ref[...] = f
    fsw = _swap_halves(f)

    def fix(i, _):
        k = (seg - 1 - i) if reverse else i
        x_ref[rows(i), :] = x_ref[rows(i), :] + pw_ref[0, pl.ds(k, 1), :] * f + (sgn * pw_ref[1, pl.ds(k, 1), :]) * fsw
        return 0

    lax.fori_loop(0, seg, fix, 0, unroll=S5_UNROLL)
    leaving = x_ref[0:1, :] if reverse else x_ref[tm - 1:tm, :]
    carry_ref[...] = jnp.broadcast_to(leaving, carry_ref.shape)


def _s5_fwd(u, bd, cd, pw, seg_f, d_skip, B, carry=None):
    N, D = u.shape
    S = N // B
    nch = D // LANES
    W = 2 * CHUNK_STATE
    tm, seg = S5_TILE, S5_SEG
    nt = S // tm

    def body(u_ref, bd_ref, cd_ref, pw_ref, seg_ref, d_ref, z_ref, cin_ref, x_s, carry, c_s, u_s, z_s):
        t = pl.program_id(2)

        @pl.when(t == 0)
        def _():
            carry[...] = jnp.zeros_like(carry)

        cin_ref[...] = carry[...]
        _to_segments(u_s, u_ref, seg)
        uf = u_s[...]
        x_s[...] = jnp.dot(uf.astype(BF16), bd_ref[...], preferred_element_type=F32)
        _seg_scan(x_s, pw_ref, seg_ref, carry, c_s, seg, False)
        y = jnp.dot(x_s[...].astype(BF16), cd_ref[...], preferred_element_type=F32) + d_ref[...] * uf
        z_s[...] = _gelu(y)
        _from_segments(z_ref, z_s, seg)

    (z, carries), moved = _carried_call(
        body, name="s5_fwd", grid=(nch, B, nt),
        in_specs=[pl.BlockSpec((tm, LANES), lambda j, b, t: (b * nt + t, j)),
                  pl.BlockSpec((None, LANES, W), lambda j, b, t: (j, 0, 0)),
                  pl.BlockSpec((None, W, LANES), lambda j, b, t: (j, 0, 0)),
                  pl.BlockSpec((None, 2, seg, W), lambda j, b, t: (j, 0, 0, 0)),
                  pl.BlockSpec((None, 6, SUBLANES, W), lambda j, b, t: (j, 0, 0, 0)),
                  pl.BlockSpec((1, LANES), lambda j, b, t: (0, j))],
        out_specs=[pl.BlockSpec((tm, LANES), lambda j, b, t: (b * nt + t, j)),
                   pl.BlockSpec((None, None, SUBLANES, W), lambda j, b, t: (j, b * nt + t, 0, 0))],
        out_shape=[jax.ShapeDtypeStruct((N, D), BF16), jax.ShapeDtypeStruct((nch, B * nt, SUBLANES, W), F32)],
        scratch_shapes=[pltpu.VMEM((tm, W), F32), pltpu.VMEM((SUBLANES, W), F32), pltpu.VMEM((SUBLANES, W), F32),
                        pltpu.VMEM((tm, LANES), F32), pltpu.VMEM((tm, LANES), F32)],
        operands=(u, bd, cd, pw, seg_f, d_skip), sem=("parallel", "arbitrary", "arbitrary"), carry=carry)
    return z, carries, moved


def _s5_bwd(u, dz, bd, cd, pw, seg_f, seg_b, d_skip, carries, B, carry=None):
    N, D = u.shape
    S = N // B
    nch = D // LANES
    W = 2 * CHUNK_STATE
    tm, seg = S5_TILE, S5_SEG
    nt = S // tm
    tn_dims = (((0,), (0,)), ((), ()))
    nt_dims = (((1,), (1,)), ((), ()))

    def body(u_ref, dz_ref, bd_ref, cd_ref, pw_ref, sf_ref, sb_ref, d_ref, cin_ref,
             du_ref, dbd_ref, dcd_ref, da_ref, dd_ref, x_s, l_s, carry, lcarry, c_s, lc_s, u_s, t_s):
        b = pl.program_id(1)
        t = pl.program_id(2)

        @pl.when((b == 0) & (t == 0))
        def _():
            dbd_ref[...] = jnp.zeros_like(dbd_ref)
            dcd_ref[...] = jnp.zeros_like(dcd_ref)
            da_ref[...] = jnp.zeros_like(da_ref)
            dd_ref[...] = jnp.zeros_like(dd_ref)

        @pl.when(t == 0)
        def _():
            lcarry[...] = jnp.zeros_like(lcarry)

        _to_segments(u_s, u_ref, seg)
        _to_segments(t_s, dz_ref, seg)
        uf = u_s[...]
        uv = uf.astype(BF16)
        carry[...] = cin_ref[...]
        x_s[...] = jnp.dot(uv, bd_ref[...], preferred_element_type=F32)
        _seg_scan(x_s, pw_ref, sf_ref, carry, c_s, seg, False)
        xb = x_s[...].astype(BF16)
        y = jnp.dot(xb, cd_ref[...], preferred_element_type=F32) + d_ref[...] * uf
        dy = t_s[...] * _gelu_grad(y)
        dd_ref[...] += jnp.sum(dy * uf, axis=0, keepdims=True)
        dyb = dy.astype(BF16)
        dcd_ref[...] += lax.dot_general(xb, dyb, tn_dims, preferred_element_type=F32)
        l_s[...] = lax.dot_general(dyb, cd_ref[...], nt_dims, preferred_element_type=F32)
        _seg_scan(l_s, pw_ref, sb_ref, lcarry, lc_s, seg, True)
        lb = l_s[...].astype(BF16)
        dbd_ref[...] += lax.dot_general(uv, lb, tn_dims, preferred_element_type=F32)
        t_s[...] = lax.dot_general(lb, bd_ref[...], nt_dims, preferred_element_type=F32) + d_ref[...] * dy
        _from_segments(du_ref, t_s, seg)
        lam_rest, x_prev = l_s[SUBLANES:, :], x_s[:tm - SUBLANES, :]
        lam_0, c_in = l_s[:SUBLANES, :], c_s[...]
        da_ref[0:1, :] += (jnp.sum(lam_rest * x_prev, axis=0, keepdims=True) + jnp.sum(lam_0 * c_in, axis=0, keepdims=True))
        da_ref[1:2, :] += (jnp.sum(lam_rest * _swap_halves(x_prev), axis=0, keepdims=True)
                           + jnp.sum(lam_0 * _swap_halves(c_in), axis=0, keepdims=True))

    tile = lambda j, b, t: (b * nt + (nt - 1 - t), j)
    chunk3 = lambda j, b, t: (j, 0, 0)
    chunk4 = lambda j, b, t: (j, 0, 0, 0)
    outs, moved = _carried_call(
        body, name="s5_bwd", grid=(nch, B, nt),
        in_specs=[pl.BlockSpec((tm, LANES), tile), pl.BlockSpec((tm, LANES), tile),
                  pl.BlockSpec((None, LANES, W), chunk3), pl.BlockSpec((None, W, LANES), chunk3),
                  pl.BlockSpec((None, 2, seg, W), chunk4), pl.BlockSpec((None, 6, SUBLANES, W), chunk4),
                  pl.BlockSpec((None, 6, SUBLANES, W), chunk4), pl.BlockSpec((1, LANES), lambda j, b, t: (0, j)),
                  pl.BlockSpec((None, None, SUBLANES, W), lambda j, b, t: (j, b * nt + (nt - 1 - t), 0, 0))],
        out_specs=[pl.BlockSpec((tm, LANES), tile), pl.BlockSpec((None, LANES, W), chunk3),
                   pl.BlockSpec((None, W, LANES), chunk3), pl.BlockSpec((None, 2, W), chunk3),
                   pl.BlockSpec((1, LANES), lambda j, b, t: (0, j))],
        out_shape=[jax.ShapeDtypeStruct((N, D), F32), jax.ShapeDtypeStruct((nch, LANES, W), F32),
                   jax.ShapeDtypeStruct((nch, W, LANES), F32), jax.ShapeDtypeStruct((nch, 2, W), F32),
                   jax.ShapeDtypeStruct((1, D), F32)],
        scratch_shapes=[pltpu.VMEM((tm, W), F32), pltpu.VMEM((tm, W), F32)] + [pltpu.VMEM((SUBLANES, W), F32)] * 4
        + [pltpu.VMEM((tm, LANES), F32)] * 2,
        operands=(u, dz, bd, cd, pw, seg_f, seg_b, d_skip, carries), sem=("parallel", "arbitrary", "arbitrary"), carry=carry)
    return (*outs, moved)


ATTN_HEADS = LANES // HEAD_DIM
ATTN_FWD_UNROLL = 4
ATTN_BWD_UNROLL = 2


def _attn_mask(n):
    qi = lax.broadcasted_iota(jnp.int32, (ATTN_BLOCK, 2 * ATTN_BLOCK), 0)
    kj = lax.broadcasted_iota(jnp.int32, (ATTN_BLOCK, 2 * ATTN_BLOCK), 1)
    prev_ok = (kj < ATTN_BLOCK) & (kj >= qi) & (n > 0)
    return prev_ok | ((kj >= ATTN_BLOCK) & (kj - ATTN_BLOCK <= qi))


def _head_lanes(h):
    lane = lax.broadcasted_iota(jnp.int32, (ATTN_BLOCK, LANES), 1)
    return (lane >= h * HEAD_DIM) & (lane < (h + 1) * HEAD_DIM)


def _per_head(cols):
    out = jnp.broadcast_to(cols[-1], (ATTN_BLOCK, LANES))
    for h in range(len(cols) - 2, -1, -1):
        out = jnp.where(_head_lanes(h), jnp.broadcast_to(cols[h], (ATTN_BLOCK, LANES)), out)
    return out


def _only_head(x, h):
    return jnp.where(_head_lanes(h), x, 0.0).astype(BF16)


def _block_rows(tb, dil, nb):
    r = tb // nb
    n = tb % nb
    start = r + dil * ATTN_BLOCK * n
    startp = jnp.where(n > 0, start - dil * ATTN_BLOCK, start)
    return n, pl.ds(start, ATTN_BLOCK, stride=dil), pl.ds(startp, ATTN_BLOCK, stride=dil)


def _attn_fwd(q, k, v, B, carry=None):
    _, S, D3 = q.shape
    D = D3 // 3
    HP = D // LANES
    scale = HEAD_DIM ** -0.5
    n_blocks = S // ATTN_BLOCK
    nbr = len(DILATIONS)
    nt_dims = (((1,), (1,)), ((), ()))

    def branch(dil, q_ref, k_ref, v_ref, acc, m_s, l_s):
        nb = (S // dil) // ATTN_BLOCK

        def blk(tb, _):
            n, rows, rowsp = _block_rows(tb, dil, nb)
            qb = q_ref[rows, :] * scale
            kk = jnp.concatenate([k_ref[rowsp, :], k_ref[rows, :]], axis=0).astype(BF16)
            vv = jnp.concatenate([v_ref[rowsp, :], v_ref[rows, :]], axis=0).astype(BF16)
            ok = _attn_mask(n)
            ms, ls, accs = [], [], []
            for h in range(ATTN_HEADS):
                s = lax.dot_general(_only_head(qb, h), kk, nt_dims, preferred_element_type=F32)
                s = jnp.where(ok, s, NEG)
                mh = jnp.max(s, axis=-1, keepdims=True)
                p = jnp.exp(s - mh)
                ms.append(mh)
                ls.append(jnp.sum(p, axis=-1, keepdims=True))
                accs.append(jnp.dot(p.astype(BF16), vv, preferred_element_type=F32))
            m_s[rows, :] = _per_head(ms)
            l_s[rows, :] = _per_head(ls)
            acc[rows, :] = _per_head(accs)
            return 0

        lax.fori_loop(0, n_blocks, blk, 0, unroll=ATTN_FWD_UNROLL)

    def body(q_ref, k_ref, v_ref, o_ref, lse_ref, *scratch):
        accs, m_ss, l_ss = scratch[:nbr], scratch[nbr:2 * nbr], scratch[2 * nbr:]
        g = pl.program_id(2)
        for gi, dil in enumerate(DILATIONS):
            pl.when(g == gi)(functools.partial(branch, dil, q_ref, k_ref, v_ref, accs[gi], m_ss[gi], l_ss[gi]))

        @pl.when(g == nbr - 1)
        def _():
            def fin(i, _):
                rows = pl.ds(pl.multiple_of(i * ATTN_BLOCK, ATTN_BLOCK), ATTN_BLOCK)
                ms = [m[rows, :] for m in m_ss]
                m_all = functools.reduce(jnp.maximum, ms)
                ws = [jnp.exp(m - m_all) for m in ms]
                den = sum(w * l[rows, :] for w, l in zip(ws, l_ss))
                o_ref[rows, :] = sum(w * a[rows, :] for w, a in zip(ws, accs)) / den
                lse_ref[rows, :] = m_all + jnp.log(den)
                return 0

            lax.fori_loop(0, n_blocks, fin, 0)

    br = pl.BlockSpec((None, S, LANES), lambda b, hp, g: (b, 0, g * HP + hp))
    hd = pl.BlockSpec((None, S, LANES), lambda b, hp, g: (b, 0, hp))
    (o, lse), moved = _carried_call(
        body, name="attn_fwd", grid=(B, HP, nbr), in_specs=[br, br, br], out_specs=[hd, hd],
        out_shape=[jax.ShapeDtypeStruct((B, S, D), F32), jax.ShapeDtypeStruct((B, S, D), F32)],
        scratch_shapes=[pltpu.VMEM((S, LANES), F32)] * (3 * nbr),
        operands=(q, k, v), sem=("parallel", "parallel", "arbitrary"), carry=carry)
    return o, lse, moved


def _attn_bwd(q, k, v, o, lse, do, dk_prev, dv_prev, B, carry=None):
    _, S, D3 = q.shape
    D = D3 // 3
    HP = D // LANES
    scale = HEAD_DIM ** -0.5
    n_blocks = S // ATTN_BLOCK
    has_prev = dk_prev is not None
    nt_dims = (((1,), (1,)), ((), ()))
    tn_dims = (((0,), (0,)), ((), ()))

    def branch(dil, q_ref, k_ref, v_ref, lse_ref, do_ref, dq_ref, dk_ref, dv_ref, delta, dk_p, dv_p):
        nb = (S // dil) // ATTN_BLOCK

        def blk(tb, _):
            n, rows, rowsp = _block_rows(tb, dil, nb)
            qb = q_ref[rows, :] * scale
            dob, lb, db = do_ref[rows, :], lse_ref[rows, :], delta[rows, :]
            kk = jnp.concatenate([k_ref[rowsp, :], k_ref[rows, :]], axis=0).astype(BF16)
            vv = jnp.concatenate([v_ref[rowsp, :], v_ref[rows, :]], axis=0).astype(BF16)
            ok = _attn_mask(n)
            dqs = []
            dkk = dvv = None
            for h in range(ATTN_HEADS):
                qh, doh = _only_head(qb, h), _only_head(dob, h)
                lh = lb[:, h * HEAD_DIM:h * HEAD_DIM + 1]
                dlt = db[:, h * HEAD_DIM:h * HEAD_DIM + 1]
                s = lax.dot_general(qh, kk, nt_dims, preferred_element_type=F32)
                p = jnp.where(ok, jnp.exp(s - lh), 0.0)
                dp = lax.dot_general(doh, vv, nt_dims, preferred_element_type=F32)
                ds = (p * (dp - dlt)).astype(BF16)
                dqs.append(jnp.dot(ds, kk, preferred_element_type=F32))
                dk_h = lax.dot_general(ds, qh, tn_dims, preferred_element_type=F32)
                dv_h = lax.dot_general(p.astype(BF16), doh, tn_dims, preferred_element_type=F32)
                dkk = dk_h if dkk is None else dkk + dk_h
                dvv = dv_h if dvv is None else dvv + dv_h
            dq_ref[rows, :] = _per_head(dqs) * scale
            dk_p[rowsp, :] = dkk[:ATTN_BLOCK]
            dv_p[rowsp, :] = dvv[:ATTN_BLOCK]
            dk_ref[rows, :] = dkk[ATTN_BLOCK:]
            dv_ref[rows, :] = dvv[ATTN_BLOCK:]
            return 0

        lax.fori_loop(0, n_blocks, blk, 0, unroll=ATTN_BWD_UNROLL)

    def body(*refs):
        q_ref, k_ref, v_ref, o_ref, lse_ref, do_ref = refs[:6]
        n_in = 8 if has_prev else 6
        dq_ref, dk_ref, dv_ref, delta, dk_p, dv_p = refs[n_in:n_in + 6]
        g = pl.program_id(2)

        @pl.when(g == 0)
        def _():
            def dl(i, _):
                rows = pl.ds(pl.multiple_of(i * ATTN_BLOCK, ATTN_BLOCK), ATTN_BLOCK)
                prod = do_ref[rows, :] * o_ref[rows, :]
                delta[rows, :] = _per_head([jnp.sum(jnp.where(_head_lanes(h), prod, 0.0), axis=-1, keepdims=True)
                                            for h in range(ATTN_HEADS)])
                return 0

            lax.fori_loop(0, n_blocks, dl, 0)

        dk_p[...] = jnp.zeros_like(dk_p)
        dv_p[...] = jnp.zeros_like(dv_p)
        for gi, dil in enumerate(DILATIONS):
            pl.when(g == gi)(functools.partial(branch, dil, q_ref, k_ref, v_ref, lse_ref, do_ref, dq_ref, dk_ref, dv_ref,
                                               delta, dk_p, dv_p))

        def fin(i, _):
            rows = pl.ds(pl.multiple_of(i * ATTN_BLOCK, ATTN_BLOCK), ATTN_BLOCK)
            dk_t = dk_ref[rows, :] + dk_p[rows, :]
            dv_t = dv_ref[rows, :] + dv_p[rows, :]
            if has_prev:
                dk_t = dk_t + refs[6][rows, :]
                dv_t = dv_t + refs[7][rows, :]
            dk_ref[rows, :] = dk_t
            dv_ref[rows, :] = dv_t
            return 0

        lax.fori_loop(0, n_blocks, fin, 0)

    br = pl.BlockSpec((None, S, LANES), lambda b, hp, g: (b, 0, g * HP + hp))
    hd = pl.BlockSpec((None, S, LANES), lambda b, hp, g: (b, 0, hp))
    ins = [q, k, v, o, lse, do] + ([dk_prev, dv_prev] if has_prev else [])
    (dq, dk, dv), moved = _carried_call(
        body, name="attn_bwd", grid=(B, HP, len(DILATIONS)),
        in_specs=[br, br, br, hd, hd, hd] + ([br, br] if has_prev else []), out_specs=[br, br, br],
        out_shape=[jax.ShapeDtypeStruct(q.shape, F32)] * 3, scratch_shapes=[pltpu.VMEM((S, LANES), F32)] * 3,
        operands=ins, sem=("parallel", "parallel", "arbitrary"), carry=carry)
    return dq, dk, dv, moved


def _adamw(w, grads, m, v):
    R, C = w.shape
    tr = _div(R, 256, SUBLANES)
    ng = len(grads)
    c1 = 1.0 - ADAM_B1 ** ADAM_STEP
    c2 = 1.0 - ADAM_B2 ** ADAM_STEP

    def body(*refs):
        w_ref, m_ref, v_ref = refs[0], refs[1 + ng], refs[2 + ng]
        d_ref, mo_ref, vo_ref = refs[3 + ng:6 + ng]
        g = refs[1][...]
        if ng == 2:
            g = g + refs[2][...]
            refs[6 + ng][...] = g
        mn = ADAM_B1 * m_ref[...] + (1.0 - ADAM_B1) * g
        vn = ADAM_B2 * v_ref[...] + (1.0 - ADAM_B2) * (g * g)
        d_ref[...] = -ADAM_LR * ((mn / c1) / (jnp.sqrt(vn / c2) + ADAM_EPS) + ADAM_WD * w_ref[...])
        mo_ref[...] = mn
        vo_ref[...] = vn

    blk = pl.BlockSpec((tr, C), lambda i: (i, 0))
    n_out = 3 + (ng == 2)
    outs = pl.pallas_call(
        body, name="adamw", grid=(R // tr,), in_specs=[blk] * (3 + ng), out_specs=[blk] * n_out,
        out_shape=[jax.ShapeDtypeStruct((R, C), F32)] * n_out, compiler_params=_params("parallel"),
    )(w, *grads, m, v)
    return (outs[3] if ng == 2 else grads[0],) + tuple(outs[:3])


def _sum_shards(recv):
    n, R, C = recv.shape
    tr = _div(R, 256, SUBLANES if recv.dtype == F32 else 2 * SUBLANES)

    def body(r_ref, o_ref):
        s = r_ref[0].astype(F32)
        for i in range(1, n):
            s = s + r_ref[i].astype(F32)
        o_ref[...] = s

    return pl.pallas_call(
        body, name="sum_shards", grid=(R // tr,), in_specs=[pl.BlockSpec((n, tr, C), lambda i: (0, i, 0))],
        out_specs=pl.BlockSpec((tr, C), lambda i: (i, 0)), out_shape=jax.ShapeDtypeStruct((R, C), F32),
        compiler_params=_params("parallel"),
    )(recv)


N_DEV = 8
N_CHIPS = 4


def _all_gather_small(x):
    m_per, n = x.shape

    def body(x_ref, out_ref, send_sems, recv_sems, local_sem):
        cx, cy, cc = _coords()
        me, sibling = (cx, cy, cc), (cx, cy, 1 - cc)
        chips = [(1 - cx, cy), (cx, 1 - cy), (1 - cx, 1 - cy)]

        def rows(px, py, pc):
            return out_ref.at[pl.ds((4 * px + 2 * py + pc) * m_per, m_per), :]

        def copy(k, block, to, src=None):
            return pltpu.make_async_remote_copy(
                src_ref=rows(*block) if src is None else src, dst_ref=rows(*block), send_sem=send_sems.at[k],
                recv_sem=recv_sems.at[k], device_id=to, device_id_type=MESH)

        mine = pltpu.make_async_copy(x_ref, rows(*me), local_sem)
        mine.start()
        first = [copy(0, me, sibling, src=x_ref)]
        first += [copy(1 + j, me, (*chip, cc), src=x_ref) for j, chip in enumerate(chips)]
        for cp in first:
            cp.start()
        passed = [copy(4 + j, (*chip, cc), sibling) for j, chip in enumerate(chips)]
        for j, chip in enumerate(chips):
            copy(1 + j, (*chip, cc), me).wait_recv()
            passed[j].start()
        copy(0, sibling, me).wait_recv()
        for j, chip in enumerate(chips):
            copy(4 + j, (*chip, 1 - cc), me).wait_recv()
        for cp in first + passed:
            cp.wait_send()
        mine.wait()

    return pl.pallas_call(
        body, name="all_gather_small", out_shape=jax.ShapeDtypeStruct((N_DEV * m_per, n), x.dtype),
        in_specs=[pl.BlockSpec(memory_space=pltpu.VMEM)], out_specs=pl.BlockSpec(memory_space=pltpu.VMEM),
        scratch_shapes=[pltpu.SemaphoreType.DMA((7,)), pltpu.SemaphoreType.DMA((7,)), pltpu.SemaphoreType.DMA],
        compiler_params=pltpu.CompilerParams(vmem_limit_bytes=VMEM_LIMIT),
    )(x)


def _exchange(name, carry):
    return _carried_call(lambda: None, name=name, grid=(1,), in_specs=[], out_specs=[], out_shape=[], scratch_shapes=[],
                         operands=(), sem=("arbitrary",), carry=carry)[1]


def _layer_moves(kind, arrays_from, arrays_to, pieces):
    used = sorted({w for w, _ in pieces})
    pos = {w: i for i, w in enumerate(used)}
    gather = kind == "gather"

    def half(ref, c):
        rows = ref.shape[0] // 2
        return ref.at[pl.ds(c * rows, rows), :]

    def plan(src_refs, dst_refs, me):
        cx, cy, cc = me
        mine = 2 * cx + cy
        remote, local = [], []
        for w, l in pieces:
            s, d = src_refs[pos[w]], dst_refs[pos[w]]
            for px, py in _other_chips(cx, cy):
                if gather:
                    remote.append((half(s.at[l], cc), half(d.at[mine, l], cc), (px, py, cc)))
                else:
                    remote.append((s.at[2 * px + py, l], d.at[mine, l], (px, py, cc)))
            local.append((s.at[l] if gather else s.at[mine, l], d.at[mine, l]))
        return remote, local

    def onward(src_refs, dst_refs, me):
        cx, cy, cc = me
        moves = []
        for w, l in pieces:
            d = dst_refs[pos[w]]
            for px, py in _other_chips(cx, cy):
                landed = half(d.at[2 * px + py, l], cc)
                moves.append((landed, landed, (cx, cy, 1 - cc)))
        return moves

    n = 3 * len(pieces)
    carry = _Carry([arrays_from[w] for w in used], [arrays_to[w] for w in used], plan, n, len(pieces),
                   onward if gather else None, n if gather else 0)
    return carry, used


def _swap_with_sibling(sums):
    def plan(src_refs, dst_refs, me):
        cx, cy, cc = me
        return [(s, d, (cx, cy, 1 - cc)) for s, d in zip(src_refs, dst_refs)], []

    return _exchange("swap_with_sibling", _Carry(sums, [lax.empty(s.shape, s.dtype) for s in sums], plan, len(sums), 0))


def _pack(arrs, width):
    parts, layout, row = [], [], 0
    for a in arrs:
        flat = a.reshape(-1).astype(F32)
        rows = -(-flat.shape[0] // width)
        parts.append(jnp.pad(flat, (0, rows * width - flat.shape[0])).reshape(rows, width))
        layout.append((row, rows, a.shape))
        row += rows
    pad = -row % SUBLANES
    if pad:
        parts.append(jnp.zeros((pad, width), F32))
    return jnp.concatenate(parts, axis=0), layout, row + pad


def _unpack(buf, layout, idx):
    row, rows, shape = layout[idx]
    size = math.prod(shape)
    return buf[row:row + rows].reshape(-1)[:size].reshape(shape)


def kernel(x, c, ln_g, ada_w, ada_b, ssm_lam_re, ssm_lam_im, ssm_log_dt, ssm_b_re, ssm_b_im, ssm_c_re, ssm_c_im, ssm_d, ssm_w_glu, kv_g, kv_ada_w, kv_ada_b, w_kv, attn_w_q, attn_w_o, mlp_w1, mlp_w2, final_g, loss_target, m_ln_g, m_ada_w, m_ada_b, m_ssm_lam_re, m_ssm_lam_im, m_ssm_log_dt, m_ssm_b_re, m_ssm_b_im, m_ssm_c_re, m_ssm_c_im, m_ssm_d, m_ssm_w_glu, m_kv_g, m_kv_ada_w, m_kv_ada_b, m_w_kv, m_attn_w_q, m_attn_w_o, m_mlp_w1, m_mlp_w2, m_final_g, v_ln_g, v_ada_w, v_ada_b, v_ssm_lam_re, v_ssm_lam_im, v_ssm_log_dt, v_ssm_b_re, v_ssm_b_im, v_ssm_c_re, v_ssm_c_im, v_ssm_d, v_ssm_w_glu, v_kv_g, v_kv_ada_w, v_kv_ada_b, v_w_kv, v_attn_w_q, v_attn_w_o, v_mlp_w1, v_mlp_w2, v_final_g):
    B, S, D = x.shape
    N = B * S
    depth = ln_g.shape[0]
    n_a = ssm_w_glu.shape[0]
    n_b = attn_w_q.shape[0]
    FF = mlp_w1.shape[2] * N_CHIPS
    cx, cy, cc = _coords()
    chip = 2 * cx + cy
    dev = 4 * cx + 2 * cy + cc
    n_ex = N_DEV * B
    ada_cols = ada_w.shape[-1]
    kv_cols = kv_ada_w.shape[-1]

    GLU, KV, Q, O, W1, W2 = range(6)
    shards = [ssm_w_glu.astype(BF16), w_kv.astype(BF16)[None], attn_w_q.astype(BF16), attn_w_o.astype(BF16),
              mlp_w1.astype(BF16), mlp_w2.astype(BF16)]
    wg = [lax.empty((N_CHIPS,) + s.shape, BF16) for s in shards]

    def fetch(pieces):
        return _layer_moves("gather", shards, wg, pieces)

    def landed(arrays, used, moved):
        for w, a in zip(used, moved):
            arrays[w] = a

    fetch_with = {l: [(W1, l), (W2, l)] for l in range(depth)}
    fetch_with[0] += [(Q, 0), (O, 0)]
    fetch_with[n_a - 1] += [(KV, 0)]
    for j in range(1, n_b):
        fetch_with[n_a + j - 1] += [(Q, j), (O, j)]
    carry, used = fetch([(GLU, l) for l in range(n_a)])
    landed(wg, used, _exchange("gather_weights", carry))

    c_pack, c_layout, _ = _pack([c], D)
    c_all_buf = _all_gather_small(c_pack)
    c_rows = c_pack.shape[0]
    c_all = jnp.concatenate([_unpack(c_all_buf[d * c_rows:(d + 1) * c_rows], c_layout, 0) for d in range(N_DEV)], axis=0)
    sc_all = jax.nn.silu(c_all).astype(BF16)
    n_mod = depth * 2
    ada_w8 = ada_w.reshape(n_mod, 1, D, ada_cols)
    ada_b_row = ada_b.reshape(1, n_mod * ada_cols)
    mod_local = _mm("ada_fwd", sc_all, ada_w8, mode="nn", M=n_ex, N=n_mod * ada_cols, K=D, b_lay="cs", b_ns=n_mod,
                    epi=_add, extras=[("n", ada_b_row)])
    kv_ada_b_local = lax.dynamic_slice(kv_ada_b.reshape(N_CHIPS, kv_cols), (chip, 0), (1, kv_cols))
    kvmod_local = _mm("ada_fwd", sc_all, _as4(kv_ada_w), mode="nn", M=n_ex, N=kv_cols, K=D, epi=_add,
                      extras=[("n", kv_ada_b_local)])
    mod_pack, mod_layout, mod_rows = _pack([mod_local, kvmod_local, ln_g, ssm_d], D)
    mod_buf = _all_gather_small(mod_pack)

    def from_chip(j, idx):
        d = 2 * j
        return _unpack(mod_buf[d * mod_rows:(d + 1) * mod_rows], mod_layout, idx)

    my_rows = lambda a: lax.dynamic_slice_in_dim(a, dev * B, B, axis=0)
    mods = jnp.concatenate([my_rows(from_chip(j, 0)).reshape(B, n_mod, ada_cols) for j in range(N_CHIPS)], axis=2)
    kvmod = jnp.concatenate([my_rows(from_chip(j, 1)) for j in range(N_CHIPS)], axis=1)
    ln_g_full = jnp.concatenate([from_chip(j, 2) for j in range(N_CHIPS)], axis=2)
    ssm_d_full = jnp.concatenate([from_chip(j, 3) for j in range(N_CHIPS)], axis=1)

    def mod3(l, s):
        mrow = mods[:, l * 2 + s]
        return [mrow[:, i * D:(i + 1) * D].reshape(B, 1, D) for i in range(3)]

    kv_shift, kv_scale = kvmod[:, :D].reshape(B, 1, D), kvmod[:, D:].reshape(B, 1, D)

    s5_tabs = []
    for l in range(n_a):
        prm = (ssm_lam_re[l], ssm_lam_im[l], ssm_log_dt[l], ssm_b_re[l], ssm_b_im[l], ssm_c_re[l], ssm_c_im[l])
        (bd, cd, _, _), disc_vjp = jax.vjp(_s5_discretize, *prm)
        pw, seg_f, seg_b = _s5_scan_coefs(ssm_lam_re[l], ssm_lam_im[l], ssm_log_dt[l], S5_SEG)
        s5_tabs.append((bd.astype(BF16), cd.astype(BF16), pw, seg_f, seg_b, disc_vjp))

    h = x.reshape(N, D)
    saved = []
    k_all = v_all = None
    for l in range(depth):
        sv = {}
        shift, scale, gate = mod3(l, 0)
        sv["h0"], sv["scale0"], sv["gate0"] = h, scale, gate
        u = _normmod(h, ln_g_full[l, 0].reshape(1, D), scale, shift, B)
        sv["u0"] = u
        carry, used = fetch(fetch_with[l])
        if l < n_a:
            bd, cd, pw, seg_f, _, _ = s5_tabs[l]
            z, carries, moved = _s5_fwd(u, bd, cd, pw, seg_f, ssm_d_full[l].reshape(1, D), B, carry)
            landed(wg, used, moved)
            zz = _mm("glu_proj", z, wg[GLU], mode="nn", M=N, N=2 * D, K=D, b_lay="cs", b_l=l, b_ns=N_CHIPS)
            y, h_next = _glu_residual(zz, h, gate, B)
            sv["z"], sv["carries"], sv["zz"] = z, carries, zz
        else:
            j = l - n_a
            q = _mm("q_proj", u, wg[Q], mode="nn", M=N, N=3 * D, K=D, b_lay="cs", b_l=j, b_ns=N_CHIPS)
            q3 = q.reshape(B, S, 3 * D)
            o, lse, moved = _attn_fwd(q3, k_all, v_all, B, carry)
            landed(wg, used, moved)
            o2 = o.reshape(N, D)
            y, h_next = _mm("o_proj", o2, wg[O], mode="nn", M=N, N=D, K=D, b_lay="rs", b_l=j, b_ns=N_CHIPS,
                            out_dtype=(BF16, F32), epi=_gated_residual, extras=[("mn", h), ("ex", gate)], rows_per_ex=S)
            sv["q"], sv["o"], sv["lse"] = q3, o, lse
        sv["y0"] = y
        h = h_next
        shift, scale, gate = mod3(l, 1)
        sv["h1"], sv["scale1"], sv["gate1"] = h, scale, gate
        u = _normmod(h, ln_g_full[l, 1].reshape(1, D), scale, shift, B)
        r = _mm("mlp_up", u, wg[W1], mode="nn", M=N, N=FF, K=D, b_lay="cs", b_l=l, b_ns=N_CHIPS, out_dtype=BF16, epi=_relu2)
        y, h = _mm("mlp_down", r, wg[W2], mode="nn", M=N, N=D, K=FF, b_lay="rs", b_l=l, b_ns=N_CHIPS,
                   out_dtype=(BF16, F32), epi=_gated_residual, extras=[("mn", h), ("ex", gate)], rows_per_ex=S)
        sv["u1"], sv["r"], sv["y1"] = u, r, y
        saved.append(sv)
        if l == n_a - 1:
            h_kv = h
            u_kv = _normmod(h, kv_g.reshape(1, D), kv_scale, kv_shift, B)
            half = N_CHIPS // 2
            k_all = _mm("kv_proj", u_kv, wg[KV], mode="nn", M=N, N=3 * D, K=D, b_lay="cs", b_s0=0, b_ns=half).reshape(B, S, 3 * D)
            v_all = _mm("kv_proj", u_kv, wg[KV], mode="nn", M=N, N=3 * D, K=D, b_lay="cs", b_s0=half, b_ns=half).reshape(B, S, 3 * D)

    loss_buf, dh, d_final_g = _loss_head(h, final_g.reshape(1, D), loss_target.reshape(N, D))
    loss = lax.psum(loss_buf[0, 0], ("x", "y", "c"))

    dg = [lax.empty(w.shape, BF16) for w in wg]
    recv = [lax.empty(w.shape, BF16) for w in wg]

    def send(pieces):
        return _layer_moves("scatter", dg, recv, pieces)

    send_with = {l: [(W1, l), (W2, l)] for l in range(depth)}
    for l in range(n_a):
        send_with[l] += [(GLU, l)]
    for j in range(n_b):
        send_with[n_a + j] += [(O, j)]
        send_with[n_a + j - 1] += [(Q, j)]
    send_with[n_a - 1] += [(KV, 0)]
    d_ln_g = [[None, None] for _ in range(depth)]
    d_mods = [[None, None] for _ in range(depth)]
    d_s5 = [None] * n_a
    dk_acc = dv_acc = None
    half = N_CHIPS // 2

    def tn_grad(name, a, d, into, l, Mr, Nc, lay, s0=0, ns=N_CHIPS):
        return _mm(name, a, _as4(d), mode="tn", M=Mr, N=Nc, K=N, b_lay="cs", out_dtype=BF16, out_lay=lay,
                   out4_shape=into.shape, out_into=into, out_l=l, out_s0=s0, out_ns=ns)

    for l in reversed(range(depth)):
        sv = saved[l]
        dy, d_gate1 = _residual_bwd(dh, sv["gate1"], sv["y1"], B)
        dg[W2] = tn_grad("mlp_down_dw", sv["r"], dy, dg[W2], l, FF, D, "rs")
        da = _mm("mlp_down_dx", dy, wg[W2], mode="nt", M=N, N=FF, K=D, b_lay="rs", b_l=l, b_ns=N_CHIPS, out_dtype=BF16,
                 epi=_relu2_bwd, extras=[("mn", sv["r"])])
        dg[W1] = tn_grad("mlp_up_dw", sv["u1"], da, dg[W1], l, D, FF, "cs")
        du = _mm("mlp_up_dx", da, wg[W1], mode="nt", M=N, N=D, K=FF, b_lay="cs", b_l=l, b_ns=N_CHIPS)
        dh, dgv, d_scale1, d_shift1 = _normmod_bwd(du, sv["h1"], ln_g_full[l, 1].reshape(1, D), sv["scale1"], dh, B)
        d_ln_g[l][1] = dgv
        d_mods[l][1] = jnp.concatenate([d_shift1, d_scale1, d_gate1], axis=2)
        dy, d_gate0 = _residual_bwd(dh, sv["gate0"], sv["y0"], B)
        if l < n_a:
            bd, cd, pw, seg_f, seg_b, disc_vjp = s5_tabs[l]
            dzz = _glu_bwd(dy, sv["zz"])
            dg[GLU] = tn_grad("glu_proj_dw", sv["z"], dzz, dg[GLU], l, D, 2 * D, "cs")
            dz = _mm("glu_proj_dx", dzz, wg[GLU], mode="nt", M=N, N=D, K=2 * D, b_lay="cs", b_l=l, b_ns=N_CHIPS)
            carry, used = send(send_with[l])
            du, d_bd, d_cd, d_a2, d_dskip, moved = _s5_bwd(sv["u0"], dz, bd, cd, pw, seg_f, seg_b,
                                                           ssm_d_full[l].reshape(1, D), sv["carries"], B, carry)
            landed(recv, used, moved)
            d_are = (d_a2[:, 0, :CHUNK_STATE] + d_a2[:, 0, CHUNK_STATE:]).reshape(-1, SSM_STATE)
            d_aim = (d_a2[:, 1, CHUNK_STATE:] - d_a2[:, 1, :CHUNK_STATE]).reshape(-1, SSM_STATE)
            d_s5[l] = disc_vjp((d_bd, d_cd, d_are, d_aim)) + (d_dskip,)
        else:
            j = l - n_a
            dg[O] = tn_grad("o_proj_dw", sv["o"].reshape(N, D), dy, dg[O], j, D, D, "rs")
            do = _mm("o_proj_dx", dy, wg[O], mode="nt", M=N, N=D, K=D, b_lay="rs", b_l=j, b_ns=N_CHIPS)
            carry, used = send(send_with[l])
            dq, dk_acc, dv_acc, moved = _attn_bwd(sv["q"], k_all, v_all, sv["o"], sv["lse"], do.reshape(B, S, D),
                                                  dk_acc, dv_acc, B, carry)
            landed(recv, used, moved)
            dq2 = dq.reshape(N, 3 * D)
            dg[Q] = tn_grad("q_proj_dw", sv["u0"], dq2, dg[Q], j, D, 3 * D, "cs")
            du = _mm("q_proj_dx", dq2, wg[Q], mode="nt", M=N, N=D, K=3 * D, b_lay="cs", b_l=j, b_ns=N_CHIPS)
        dh, dgv, d_scale0, d_shift0 = _normmod_bwd(du, sv["h0"], ln_g_full[l, 0].reshape(1, D), sv["scale0"], dh, B)
        d_ln_g[l][0] = dgv
        d_mods[l][0] = jnp.concatenate([d_shift0, d_scale0, d_gate0], axis=2)
        if l == n_a:
            dk2, dv2 = dk_acc.reshape(N, 3 * D), dv_acc.reshape(N, 3 * D)
            dg[KV] = tn_grad("kv_proj_dw", u_kv, dk2, dg[KV], 0, D, 3 * D, "cs", s0=0, ns=half)
            dg[KV] = tn_grad("kv_proj_dw", u_kv, dv2, dg[KV], 0, D, 3 * D, "cs", s0=half, ns=half)
            du_kv = _mm("kv_proj_dx", dk2, wg[KV], mode="nt", M=N, N=D, K=3 * D, b_lay="cs", b_s0=0, b_ns=half)
            du_kv = _mm("kv_proj_dx", dv2, wg[KV], mode="nt", M=N, N=D, K=3 * D, b_lay="cs", b_s0=half, b_ns=half,
                        epi=_add, extras=[("mn", du_kv)])
            dh, d_kv_g, d_kv_scale, d_kv_shift = _normmod_bwd(du_kv, h_kv, kv_g.reshape(1, D), kv_scale, dh, B)
    grad_x = dh.reshape(B, S, D)

    own = [_sum_shards(r.reshape(N_CHIPS, -1, r.shape[-1])) for r in recv]
    other = _swap_with_sibling(own)

    d_kvmod = jnp.concatenate([d_kv_shift, d_kv_scale], axis=2).reshape(B, 2 * D)
    d_mod_all = jnp.concatenate([d_mods[l][s].reshape(B, 3 * D) for l in range(depth) for s in range(2)], axis=1)
    small = [
        d_mod_all, d_kvmod,
        jnp.stack([jnp.stack([d_ln_g[l][0].reshape(D), d_ln_g[l][1].reshape(D)]) for l in range(depth)]),
        jnp.stack([d_s5[l][0] for l in range(n_a)]), jnp.stack([d_s5[l][1] for l in range(n_a)]),
        jnp.stack([d_s5[l][2] for l in range(n_a)]),
        jnp.stack([d_s5[l][3] for l in range(n_a)]), jnp.stack([d_s5[l][4] for l in range(n_a)]),
        jnp.stack([d_s5[l][5] for l in range(n_a)]), jnp.stack([d_s5[l][6] for l in range(n_a)]),
        jnp.stack([d_s5[l][7].reshape(D) for l in range(n_a)]),
        d_kv_g.reshape(D), d_final_g.reshape(D),
    ]
    small_pack, small_layout, small_rows = _pack(small, D)
    small_buf = _all_gather_small(small_pack)
    small_sum = _sum_shards(small_buf.reshape(N_DEV, small_rows, D))
    red = lambda idx: _unpack(small_sum, small_layout, idx)
    per_dev = lambda idx: jnp.concatenate(
        [_unpack(small_buf[d * small_rows:(d + 1) * small_rows], small_layout, idx) for d in range(N_DEV)], axis=0)

    dm_all = per_dev(0).reshape(n_ex, n_mod, 3 * D)
    dm_cols = lax.dynamic_slice_in_dim(dm_all, chip * ada_cols, ada_cols, axis=2).reshape(n_ex, n_mod * ada_cols)
    g_ada_w = _mm("ada_dw", sc_all, _as4(dm_cols), mode="tn", M=D, N=n_mod * ada_cols, K=n_ex, b_lay="cs",
                  out_lay="cs", out4_shape=(n_mod, 1, D, ada_cols), out_ns=n_mod).reshape(ada_w.shape)
    dkvm_all = per_dev(1)
    dkvm_cols = lax.dynamic_slice_in_dim(dkvm_all, chip * kv_cols, kv_cols, axis=1)
    g_kv_ada_w = _mm("ada_dw", sc_all, _as4(dkvm_cols), mode="tn", M=D, N=kv_cols, K=n_ex, b_lay="cs")
    g_ada_b_full = (red(0)[0] + red(0)[1]).reshape(depth, 2, 3 * D) if B == 2 else jnp.sum(red(0), axis=0).reshape(depth, 2, 3 * D)
    g_ada_b = lax.dynamic_slice_in_dim(g_ada_b_full, chip * ada_cols, ada_cols, axis=2)
    g_kv_ada_b = red(1)[0] + red(1)[1] if B == 2 else jnp.sum(red(1), axis=0)
    g_ln_g = lax.dynamic_slice_in_dim(red(2), chip * (D // N_CHIPS), D // N_CHIPS, axis=2)
    g_ssm_d = lax.dynamic_slice_in_dim(red(10), chip * (D // N_CHIPS), D // N_CHIPS, axis=1)
    small_grads = {
        "ln_g": g_ln_g, "ada_b": g_ada_b, "ssm_lam_re": red(3), "ssm_lam_im": red(4), "ssm_log_dt": red(5),
        "ssm_b_re": red(6), "ssm_b_im": red(7), "ssm_c_re": red(8), "ssm_c_im": red(9), "ssm_d": g_ssm_d,
        "kv_g": red(11), "kv_ada_b": g_kv_ada_b, "final_g": red(12),
    }
    small_w = {"ln_g": (ln_g, m_ln_g, v_ln_g), "ada_b": (ada_b, m_ada_b, v_ada_b),
               "ssm_lam_re": (ssm_lam_re, m_ssm_lam_re, v_ssm_lam_re), "ssm_lam_im": (ssm_lam_im, m_ssm_lam_im, v_ssm_lam_im),
               "ssm_log_dt": (ssm_log_dt, m_ssm_log_dt, v_ssm_log_dt), "ssm_b_re": (ssm_b_re, m_ssm_b_re, v_ssm_b_re),
               "ssm_b_im": (ssm_b_im, m_ssm_b_im, v_ssm_b_im), "ssm_c_re": (ssm_c_re, m_ssm_c_re, v_ssm_c_re),
               "ssm_c_im": (ssm_c_im, m_ssm_c_im, v_ssm_c_im), "ssm_d": (ssm_d, m_ssm_d, v_ssm_d),
               "kv_g": (kv_g, m_kv_g, v_kv_g), "kv_ada_b": (kv_ada_b, m_kv_ada_b, v_kv_ada_b),
               "final_g": (final_g, m_final_g, v_final_g)}
    names = list(small_w)
    wp, lay_w, _ = _pack([small_w[n][0] for n in names], D)
    gp, _, _ = _pack([small_grads[n] for n in names], D)
    mp, _, _ = _pack([small_w[n][1] for n in names], D)
    vp, _, _ = _pack([small_w[n][2] for n in names], D)
    _, d_p, m_p, v_p = _adamw(wp, [gp], mp, vp)
    upd = {n: (small_grads[n].reshape(small_w[n][0].shape), _unpack(d_p, lay_w, i), _unpack(m_p, lay_w, i), _unpack(v_p, lay_w, i))
           for i, n in enumerate(names)}

    def big(w, m, v, g_own, g_other=None):
        C = w.shape[-1]
        gs = [g_own.reshape(-1, C)] + ([g_other.reshape(-1, C)] if g_other is not None else [])
        return tuple(t.reshape(w.shape) for t in _adamw(w.reshape(-1, C), gs, m.reshape(-1, C), v.reshape(-1, C)))

    upd["ssm_w_glu"] = big(ssm_w_glu, m_ssm_w_glu, v_ssm_w_glu, own[0], other[0])
    upd["w_kv"] = big(w_kv, m_w_kv, v_w_kv, own[1], other[1])
    upd["attn_w_q"] = big(attn_w_q, m_attn_w_q, v_attn_w_q, own[2], other[2])
    upd["attn_w_o"] = big(attn_w_o, m_attn_w_o, v_attn_w_o, own[3], other[3])
    upd["mlp_w1"] = big(mlp_w1, m_mlp_w1, v_mlp_w1, own[4], other[4])
    upd["mlp_w2"] = big(mlp_w2, m_mlp_w2, v_mlp_w2, own[5], other[5])
    upd["ada_w"] = big(ada_w, m_ada_w, v_ada_w, g_ada_w)
    upd["kv_ada_w"] = big(kv_ada_w, m_kv_ada_w, v_kv_ada_w, g_kv_ada_w)

    order = ["ln_g", "ada_w", "ada_b", "ssm_lam_re", "ssm_lam_im", "ssm_log_dt", "ssm_b_re", "ssm_b_im", "ssm_c_re",
             "ssm_c_im", "ssm_d", "ssm_w_glu", "kv_g", "kv_ada_w", "kv_ada_b", "w_kv", "attn_w_q", "attn_w_o", "mlp_w1",
             "mlp_w2", "final_g"]
    return (loss, grad_x, *[upd[n][0] for n in order], *[upd[n][1] for n in order], *[upd[n][2] for n in order],
            *[upd[n][3] for n in order])
```

```python
import functools
import math

import jax
import jax.numpy as jnp
from jax import lax
from jax.experimental import pallas as pl
from jax.experimental.pallas import tpu as pltpu

F32 = jnp.float32
BF16 = jnp.bfloat16
MESH = pl.DeviceIdType.MESH

EPS = 1e-6
NEG = -1e30
SSM_GROUP = 16
SSM_STATE = 64
HEAD_DIM = 64
ATTN_BLOCK = 128
DILATIONS = (1, 4, 16)
ADAM_LR, ADAM_B1, ADAM_B2, ADAM_EPS, ADAM_WD, ADAM_STEP = 0.001, 0.9, 0.999, 1e-08, 0.01, 10

LANES = 128
SUBLANES = 8
CHUNK_GROUPS = LANES // SSM_GROUP
CHUNK_STATE = CHUNK_GROUPS * SSM_STATE
VMEM_LIMIT = 56 * 1024 * 1024


def _div(dim, pref, mult):
    t = min(pref, dim) // mult * mult
    while t >= mult:
        if dim % t == 0:
            return t
        t -= mult
    return dim


def _params(*sem):
    return pltpu.CompilerParams(dimension_semantics=sem, vmem_limit_bytes=VMEM_LIMIT)


def _coords():
    return lax.axis_index("x"), lax.axis_index("y"), lax.axis_index("c")


def _other_chips(cx, cy):
    return [(1 - cx, cy), (cx, 1 - cy), (1 - cx, 1 - cy)]


class _Carry:
    def __init__(self, srcs, dsts, plan, n_remote, n_local, onward=None, n_onward=0):
        self.srcs, self.dsts, self.plan, self.n_remote, self.n_local = list(srcs), list(dsts), plan, n_remote, n_local
        self.onward, self.n_onward = onward, n_onward


def _carried_call(body, *, name, grid, in_specs, out_specs, out_shape, scratch_shapes, operands, sem, carry=None):
    if carry is None:
        outs = pl.pallas_call(body, name=name, grid=grid, in_specs=in_specs, out_specs=out_specs, out_shape=out_shape,
                              scratch_shapes=scratch_shapes, compiler_params=_params(*sem))(*operands)
        return list(outs), []
    n_in, n_out, n_scr = len(in_specs), len(out_specs), len(scratch_shapes)
    ns, nd = len(carry.srcs), len(carry.dsts)

    def wrapped(*refs):
        base_in, src_refs = refs[:n_in], refs[n_in:n_in + ns]
        o0 = n_in + ns + nd
        base_out, dst_refs = refs[o0:o0 + n_out], refs[o0 + n_out:o0 + n_out + nd]
        s0 = o0 + n_out + nd
        base_scr = refs[s0:s0 + n_scr]
        send_sems, recv_sems, local_sems = refs[s0 + n_scr:]
        pids = [pl.program_id(a) for a in range(len(grid))]
        first = functools.reduce(jnp.logical_and, [p == 0 for p in pids])
        last = functools.reduce(jnp.logical_and, [p == g - 1 for p, g in zip(pids, grid)])

        def remote_copies(moves, k0):
            return [pltpu.make_async_remote_copy(src_ref=s, dst_ref=d, send_sem=send_sems.at[k0 + i], recv_sem=recv_sems.at[k0 + i],
                                                 device_id=peer, device_id_type=MESH) for i, (s, d, peer) in enumerate(moves)]

        def copies():
            remote, local = carry.plan(src_refs, dst_refs, _coords())
            return remote_copies(remote, 0), [pltpu.make_async_copy(s, d, local_sems.at[i]) for i, (s, d) in enumerate(local)]

        @pl.when(first)
        def _():
            remote, local = copies()
            for cp in local + remote:
                cp.start()

        body(*base_in, *base_out, *base_scr)

        @pl.when(last)
        def _():
            remote, local = copies()
            for cp in remote:
                cp.wait_send()
                cp.wait_recv()
            for cp in local:
                cp.wait()
            if carry.onward is not None:
                second = remote_copies(carry.onward(src_refs, dst_refs, _coords()), carry.n_remote)
                for cp in second:
                    cp.start()
                for cp in second:
                    cp.wait_send()
                    cp.wait_recv()

    anyspec = pl.BlockSpec(memory_space=pl.ANY)
    outs = pl.pallas_call(
        wrapped, name=name, grid=grid, in_specs=list(in_specs) + [anyspec] * (ns + nd),
        out_specs=list(out_specs) + [anyspec] * nd,
        out_shape=list(out_shape) + [jax.ShapeDtypeStruct(d.shape, d.dtype) for d in carry.dsts],
        scratch_shapes=list(scratch_shapes) + [pltpu.SemaphoreType.DMA((carry.n_remote + carry.n_onward,)),
                                               pltpu.SemaphoreType.DMA((carry.n_remote + carry.n_onward,)),
                                               pltpu.SemaphoreType.DMA((max(carry.n_local, 1),))],
        input_output_aliases={n_in + ns + i: n_out + i for i in range(nd)},
        compiler_params=_params(*(["arbitrary"] * len(grid))),
    )(*operands, *carry.srcs, *carry.dsts)
    return list(outs[:n_out]), list(outs[n_out:])


def _mm(name, a, b4, *, mode, M, N, K, b_lay="cs", b_l=0, b_s0=0, b_ns=1, out_dtype=F32, out_lay=None, out4_shape=None,
        out_into=None, out_l=0, out_s0=0, out_ns=1, epi=None, extras=(), rows_per_ex=None, tm=1024, tn=1024, tk=1024):
    _, _, bR, bC = b4.shape
    tm = _div(M, tm, SUBLANES if M % 16 else 16)
    brows, bcols = (N, K) if mode == "nt" else (K, N)
    if b_lay == "cs":
        assert bR == brows and bC * b_ns == bcols, (name, b4.shape, brows, bcols)
    else:
        assert bC == bcols and bR * b_ns == brows, (name, b4.shape, brows, bcols)
    n_lim = N
    k_lim = K
    if mode == "nt":
        if b_lay == "cs":
            k_lim = bC
        else:
            n_lim = bR
    else:
        if b_lay == "cs":
            n_lim = bC
        else:
            k_lim = bR
    if out_lay == "cs":
        oR, oC = out4_shape[2], out4_shape[3]
        assert oR == M and oC * out_ns == N, (name, out4_shape, M, N)
        n_lim = math.gcd(n_lim, oC)
    elif out_lay == "rs":
        oR, oC = out4_shape[2], out4_shape[3]
        assert oC == N and oR * out_ns == M, (name, out4_shape, M, N)
        tm = _div(oR, tm, SUBLANES)
    tn = _div(n_lim, tn, LANES)
    tk = _div(k_lim, tk, LANES if mode != "tn" else SUBLANES)
    if mode == "tn":
        tk = _div(k_lim, tk, 16) if k_lim % 16 == 0 else tk
    nk = K // tk
    grid = (M // tm, N // tn, nk)

    if mode == "tn":
        a_spec = pl.BlockSpec((tk, tm), lambda i, j, k: (k, i))
    else:
        a_spec = pl.BlockSpec((tm, tk), lambda i, j, k: (i, k))

    def b_index(ri, ci, br, bc):
        if b_lay == "cs":
            per = bC // bc
            return (b_s0 + ci // per, b_l, ri, ci % per)
        per = bR // br
        return (b_s0 + ri // per, b_l, ri % per, ci)

    if mode == "nt":
        b_spec = pl.BlockSpec((None, None, tn, tk), lambda i, j, k: b_index(j, k, tn, tk))
    else:
        b_spec = pl.BlockSpec((None, None, tk, tn), lambda i, j, k: b_index(k, j, tk, tn))

    in_specs = [a_spec, b_spec]
    operands = [a, b4]
    for kind, arr in extras:
        if kind == "mn":
            in_specs.append(pl.BlockSpec((tm, tn), lambda i, j, k: (i, j)))
        elif kind == "ex":
            per_ex = rows_per_ex // tm
            in_specs.append(pl.BlockSpec((None, 1, tn), lambda i, j, k: (i // per_ex, 0, j)))
        else:
            in_specs.append(pl.BlockSpec((1, tn), lambda i, j, k: (0, j)))
        operands.append(arr)
    n_extra = len(extras)

    multi = isinstance(out_dtype, tuple)
    n_out = len(out_dtype) if multi else 1
    if out_lay is None:
        out_shape = [jax.ShapeDtypeStruct((M, N), dt) for dt in (out_dtype if multi else (out_dtype,))]
        out_spec = [pl.BlockSpec((tm, tn), lambda i, j, k: (i, j)) for _ in range(n_out)]
    else:
        out_shape = [jax.ShapeDtypeStruct(tuple(out4_shape), out_dtype)]
        if out_lay == "cs":
            per_o = oC // tn
            out_spec = [pl.BlockSpec((None, None, tm, tn), lambda i, j, k: (out_s0 + j // per_o, out_l, i, j % per_o))]
        else:
            per_o = oR // tm
            out_spec = [pl.BlockSpec((None, None, tm, tn), lambda i, j, k: (out_s0 + i // per_o, out_l, i % per_o, j))]
    aliases = {}
    if out_into is not None:
        in_specs.append(pl.BlockSpec(memory_space=pl.ANY))
        operands.append(out_into)
        aliases = {len(operands) - 1: 0}

    dims = {"nn": (((1,), (0,)), ((), ())), "nt": (((1,), (1,)), ((), ())), "tn": (((0,), (0,)), ((), ()))}[mode]

    def body(a_ref, b_ref, *rest):
        extra_refs = rest[:n_extra]
        o_refs = rest[len(rest) - n_out - (nk > 1):len(rest) - (nk > 1)]

        def finish(r):
            if epi is not None:
                r = epi(r, *[e[...] for e in extra_refs])
            for o_ref, val in zip(o_refs, r if multi else (r,)):
                o_ref[...] = val.astype(o_ref.dtype)

        part = lax.dot_general(a_ref[...].astype(BF16), b_ref[...].astype(BF16), dims, preferred_element_type=F32)
        if nk == 1:
            finish(part)
            return
        acc = rest[-1]
        k = pl.program_id(2)

        @pl.when(k == 0)
        def _():
            acc[...] = part

        @pl.when(k != 0)
        def _():
            acc[...] += part

        @pl.when(k == nk - 1)
        def _():
            finish(acc[...])

    outs = pl.pallas_call(
        body, name=name, grid=grid, in_specs=in_specs, out_specs=out_spec, out_shape=out_shape,
        scratch_shapes=[pltpu.VMEM((tm, tn), F32)] if nk > 1 else [], input_output_aliases=aliases,
        compiler_params=_params("parallel", "parallel", "arbitrary"),
    )(*operands)
    return tuple(outs) if multi else outs[0]


def _as4(w):
    return w.reshape((1, 1) + w.shape)


def _relu2(acc):
    r = jnp.maximum(acc, 0.0)
    return r * r


def _relu2_bwd(acc, r):
    return acc * (2.0 * jnp.sqrt(r.astype(F32)))


def _add(acc, e):
    return acc + e


def _gated_residual(acc, h, gate):
    return acc, h + gate * acc


def _row_tiles(N, B, pref=256):
    S = N // B
    tm = _div(S, pref, SUBLANES)
    return tm, S // tm


def _normmod(h, g, scale, shift, B):
    N, D = h.shape
    tm, per_ex = _row_tiles(N, B)

    def body(h_ref, g_ref, sc_ref, sh_ref, u_ref):
        x = h_ref[...]
        rstd = lax.rsqrt(jnp.mean(x * x, axis=-1, keepdims=True) + EPS)
        y = (x * rstd) * g_ref[...]
        u_ref[...] = (y * (1.0 + sc_ref[...]) + sh_ref[...]).astype(u_ref.dtype)

    tok = pl.BlockSpec((tm, D), lambda i: (i, 0))
    vec = pl.BlockSpec((1, D), lambda i: (0, 0))
    ex = pl.BlockSpec((None, 1, D), lambda i: (i // per_ex, 0, 0))
    return pl.pallas_call(
        body, name="normmod_fwd", grid=(N // tm,), in_specs=[tok, vec, ex, ex], out_specs=tok,
        out_shape=jax.ShapeDtypeStruct((N, D), BF16), compiler_params=_params("parallel"),
    )(h, g, scale, shift)


def _normmod_bwd(du, h, g, scale, dh_in, B):
    N, D = h.shape
    tm, per_ex = _row_tiles(N, B)

    def body(du_ref, h_ref, g_ref, sc_ref, dhin_ref, dh_ref, dg_ref, dsc_ref, dsh_ref):
        i = pl.program_id(0)
        x = h_ref[...]
        gv = g_ref[...]
        d_u = du_ref[...].astype(F32)
        rstd = lax.rsqrt(jnp.mean(x * x, axis=-1, keepdims=True) + EPS)
        xn = x * rstd
        dyg = d_u * (1.0 + sc_ref[...])
        dxn = dyg * gv
        dh_ref[...] = dhin_ref[...] + rstd * (dxn - xn * jnp.mean(dxn * xn, axis=-1, keepdims=True))
        dsh_t = jnp.sum(d_u, axis=0, keepdims=True)
        dsc_t = jnp.sum(d_u * (xn * gv), axis=0, keepdims=True)
        dg_t = jnp.sum(dyg * xn, axis=0, keepdims=True)

        @pl.when(i % per_ex == 0)
        def _():
            dsc_ref[...] = dsc_t
            dsh_ref[...] = dsh_t

        @pl.when(i % per_ex != 0)
        def _():
            dsc_ref[...] += dsc_t
            dsh_ref[...] += dsh_t

        @pl.when(i == 0)
        def _():
            dg_ref[...] = dg_t

        @pl.when(i != 0)
        def _():
            dg_ref[...] += dg_t

    tok = pl.BlockSpec((tm, D), lambda i: (i, 0))
    vec = pl.BlockSpec((1, D), lambda i: (0, 0))
    ex = pl.BlockSpec((None, 1, D), lambda i: (i // per_ex, 0, 0))
    return pl.pallas_call(
        body, name="normmod_bwd", grid=(N // tm,), in_specs=[tok, tok, vec, ex, tok], out_specs=[tok, vec, ex, ex],
        out_shape=[jax.ShapeDtypeStruct((N, D), F32), jax.ShapeDtypeStruct((1, D), F32),
                   jax.ShapeDtypeStruct((B, 1, D), F32), jax.ShapeDtypeStruct((B, 1, D), F32)],
        compiler_params=_params("arbitrary"),
    )(du, h, g, scale, dh_in)


def _residual_bwd(dh, gate, y, B):
    N, D = dh.shape
    tm, per_ex = _row_tiles(N, B)

    def body(dh_ref, gt_ref, y_ref, dy_ref, dgt_ref):
        i = pl.program_id(0)
        d = dh_ref[...]
        dy_ref[...] = (gt_ref[...] * d).astype(dy_ref.dtype)
        t = jnp.sum(d * y_ref[...], axis=0, keepdims=True)

        @pl.when(i % per_ex == 0)
        def _():
            dgt_ref[...] = t

        @pl.when(i % per_ex != 0)
        def _():
            dgt_ref[...] += t

    tok = pl.BlockSpec((tm, D), lambda i: (i, 0))
    ex = pl.BlockSpec((None, 1, D), lambda i: (i // per_ex, 0, 0))
    return pl.pallas_call(
        body, name="residual_bwd", grid=(N // tm,), in_specs=[tok, ex, tok], out_specs=[tok, ex],
        out_shape=[jax.ShapeDtypeStruct((N, D), BF16), jax.ShapeDtypeStruct((B, 1, D), F32)],
        compiler_params=_params("arbitrary"),
    )(dh, gate, y)


def _glu_residual(zz, h, gate, B):
    N, D2 = zz.shape
    D = D2 // 2
    tm, per_ex = _row_tiles(N, B)

    def body(v_ref, g_ref, h_ref, gt_ref, y_ref, o_ref):
        y = v_ref[...] * jax.nn.sigmoid(g_ref[...])
        y_ref[...] = y.astype(y_ref.dtype)
        o_ref[...] = h_ref[...] + gt_ref[...] * y

    tok = pl.BlockSpec((tm, D), lambda i: (i, 0))
    return pl.pallas_call(
        body, name="glu_fwd", grid=(N // tm,),
        in_specs=[tok, pl.BlockSpec((tm, D), lambda i: (i, 1)), tok,
                  pl.BlockSpec((None, 1, D), lambda i: (i // per_ex, 0, 0))],
        out_specs=[tok, tok], out_shape=[jax.ShapeDtypeStruct((N, D), BF16), jax.ShapeDtypeStruct((N, D), F32)],
        compiler_params=_params("parallel"),
    )(zz, zz, h, gate)


def _glu_bwd(dy, zz):
    N, D2 = zz.shape
    D = D2 // 2
    tm = _div(N, 256, SUBLANES)

    def body(dy_ref, v_ref, g_ref, o_ref):
        d = dy_ref[...].astype(F32)
        s = jax.nn.sigmoid(g_ref[...])
        o_ref[...] = jnp.concatenate([d * s, d * v_ref[...] * s * (1.0 - s)], axis=1).astype(o_ref.dtype)

    return pl.pallas_call(
        body, name="glu_bwd", grid=(N // tm,),
        in_specs=[pl.BlockSpec((tm, D), lambda i: (i, 0)), pl.BlockSpec((tm, D), lambda i: (i, 0)),
                  pl.BlockSpec((tm, D), lambda i: (i, 1))],
        out_specs=pl.BlockSpec((tm, D2), lambda i: (i, 0)), out_shape=jax.ShapeDtypeStruct((N, D2), BF16),
        compiler_params=_params("parallel"),
    )(dy, zz, zz)


def _loss_head(h, g, target):
    N, D = h.shape
    tm = _div(N, 256, SUBLANES)

    def body(h_ref, g_ref, t_ref, loss_ref, dh_ref, dg_ref):
        i = pl.program_id(0)
        x = h_ref[...]
        gv = g_ref[...]
        rstd = lax.rsqrt(jnp.mean(x * x, axis=-1, keepdims=True) + EPS)
        xn = x * rstd
        err = xn * gv - t_ref[...]
        part = 0.5 * jnp.sum(jnp.sum(err * err, axis=-1, keepdims=True) / D, axis=0, keepdims=True)
        dy = err / D
        dxn = dy * gv
        dh_ref[...] = rstd * (dxn - xn * jnp.mean(dxn * xn, axis=-1, keepdims=True))
        dg_t = jnp.sum(dy * xn, axis=0, keepdims=True)
        part = jnp.broadcast_to(part, loss_ref.shape)

        @pl.when(i == 0)
        def _():
            loss_ref[...] = part
            dg_ref[...] = dg_t

        @pl.when(i != 0)
        def _():
            loss_ref[...] += part
            dg_ref[...] += dg_t

    tok = pl.BlockSpec((tm, D), lambda i: (i, 0))
    vec = pl.BlockSpec((1, D), lambda i: (0, 0))
    return pl.pallas_call(
        body, name="loss_head", grid=(N // tm,), in_specs=[tok, vec, tok],
        out_specs=[pl.BlockSpec((SUBLANES, LANES), lambda i: (0, 0)), tok, vec],
        out_shape=[jax.ShapeDtypeStruct((SUBLANES, LANES), F32), jax.ShapeDtypeStruct((N, D), F32),
                   jax.ShapeDtypeStruct((1, D), F32)],
        compiler_params=_params("arbitrary"),
    )(h, g, target)


def _swap_halves(x):
    half = x.shape[-1] // 2
    return jnp.concatenate([x[:, half:], x[:, :half]], axis=1)


def _gelu(y):
    return jax.nn.gelu(y)


def _gelu_grad(y):
    c0 = math.sqrt(2.0 / math.pi)
    inner = c0 * (y + 0.044715 * y * y * y)
    t = jnp.tanh(inner)
    return 0.5 * (1.0 + t) + 0.5 * y * (1.0 - t * t) * c0 * (1.0 + 3.0 * 0.044715 * y * y)


def _s5_discretize(lam_re, lam_im, log_dt, b_re, b_im, c_re, c_im):
    G = lam_re.shape[0]
    nch = G // CHUNK_GROUPS
    dt = jnp.exp(log_dt)[:, None]
    er = jnp.exp(lam_re * dt)
    a_re = er * jnp.cos(lam_im * dt)
    a_im = er * jnp.sin(lam_im * dt)
    den = lam_re * lam_re + lam_im * lam_im
    n_re, n_im = a_re - 1.0, a_im
    f_re = (n_re * lam_re + n_im * lam_im) / den
    f_im = (n_im * lam_re - n_re * lam_im) / den
    bb_re = f_re[..., None] * b_re - f_im[..., None] * b_im
    bb_im = f_re[..., None] * b_im + f_im[..., None] * b_re
    eye = jnp.eye(CHUNK_GROUPS, dtype=F32)

    def pack_b(bb):
        bb = bb.reshape(nch, CHUNK_GROUPS, SSM_STATE, SSM_GROUP)
        return jnp.einsum("jgpc,gh->jgchp", bb, eye).reshape(nch, LANES, CHUNK_STATE)

    def pack_c(cc):
        cc = cc.reshape(nch, CHUNK_GROUPS, SSM_GROUP, SSM_STATE)
        return jnp.einsum("jgcp,gh->jgphc", cc, eye).reshape(nch, CHUNK_STATE, LANES)

    bd = jnp.concatenate([pack_b(bb_re), pack_b(bb_im)], axis=2)
    cd = jnp.concatenate([pack_c(c_re), pack_c(-c_im)], axis=1)
    return bd, cd, a_re, a_im


S5_TILE = 256
S5_SEG = S5_TILE // SUBLANES
S5_UNROLL = 4


def _s5_scan_coefs(lam_re, lam_im, log_dt, seg):
    G = lam_re.shape[0]
    nch = G // CHUNK_GROUPS
    dt = jnp.exp(log_dt)[:, None]

    def power(k):
        er = jnp.exp(k * lam_re * dt)
        re = (er * jnp.cos(k * lam_im * dt)).reshape(nch, 1, CHUNK_STATE)
        im = (er * jnp.sin(k * lam_im * dt)).reshape(nch, 1, CHUNK_STATE)
        return jnp.concatenate([re, re], axis=2), jnp.concatenate([-im, im], axis=2)

    rows = [power(i + 1) for i in range(seg)]
    pw = jnp.stack([jnp.concatenate([r for r, _ in rows], axis=1), jnp.concatenate([i for _, i in rows], axis=1)], axis=1)
    row = jnp.arange(SUBLANES, dtype=jnp.int32)[None, :, None]

    def table(reverse):
        tabs = []
        for s in (1, 2, 4):
            re, im = power(s * seg)
            mask = (row < SUBLANES - s) if reverse else (row >= s)
            tabs += [jnp.where(mask, re, 0.0), jnp.where(mask, -im if reverse else im, 0.0)]
        return jnp.stack([jnp.broadcast_to(t, (nch, SUBLANES, 2 * CHUNK_STATE)) for t in tabs], axis=1)

    return pw, table(False), table(True)


def _to_segments(dst_s, src_ref, seg):
    for j in range(SUBLANES):
        dst_s[pl.ds(j, seg, stride=SUBLANES), :] = src_ref[pl.ds(j * seg, seg), :].astype(F32)


def _from_segments(dst_ref, src_s, seg):
    for j in range(SUBLANES):
        dst_ref[pl.ds(j * seg, seg), :] = src_s[pl.ds(j, seg, stride=SUBLANES), :].astype(dst_ref.dtype)


def _seg_scan(x_ref, pw_ref, seg_ref, carry_ref, c_ref, seg, reverse):
    W = x_ref.shape[-1]
    tm = x_ref.shape[0]
    sgn = -1.0 if reverse else 1.0
    ar = jnp.broadcast_to(pw_ref[0, 0:1, :], (SUBLANES, W))
    ai = sgn * jnp.broadcast_to(pw_ref[1, 0:1, :], (SUBLANES, W))

    def rows(i):
        return pl.ds(pl.multiple_of(i * SUBLANES, SUBLANES), SUBLANES)

    def step(t, prev):
        i = (seg - 2 - t) if reverse else (t + 1)
        x = x_ref[rows(i), :] + ar * prev + ai * _swap_halves(prev)
        x_ref[rows(i), :] = x
        return x

    start = (seg - 1) * SUBLANES if reverse else 0
    edge = lax.fori_loop(0, seg - 1, step, x_ref[start:start + SUBLANES, :], unroll=S5_UNROLL)
    row = lax.broadcasted_iota(jnp.int32, (SUBLANES, W), 0)
    if reverse:
        f = jnp.where(row == SUBLANES - 1, carry_ref[...], pltpu.roll(edge, SUBLANES - 1, 0))
    else:
        f = jnp.where(row == 0, carry_ref[...], pltpu.roll(edge, 1, 0))
    for si, s in enumerate((1, 2, 4)):
        fs = pltpu.roll(f, (SUBLANES - s) if reverse else s, 0)
        f = f + seg_ref[2 * si] * fs + seg_ref[2 * si + 1] * _swap_halves(fs)
    c_ref[...] = f
    fsw = _swap_halves(f)

    def fix(i, _):
        k = (seg - 1 - i) if reverse else i
        x_ref[rows(i), :] = x_ref[rows(i), :] + pw_ref[0, pl.ds(k, 1), :] * f + (sgn * pw_ref[1, pl.ds(k, 1), :]) * fsw
        return 0

    lax.fori_loop(0, seg, fix, 0, unroll=S5_UNROLL)
    leaving = x_ref[0:1, :] if reverse else x_ref[tm - 1:tm, :]
    carry_ref[...] = jnp.broadcast_to(leaving, carry_ref.shape)


def _s5_fwd(u, bd, cd, pw, seg_f, d_skip, B, carry=None):
    N, D = u.shape
    S = N // B
    nch = D // LANES
    W = 2 * CHUNK_STATE
    tm, seg = S5_TILE, S5_SEG
    nt = S // tm

    def body(u_ref, bd_ref, cd_ref, pw_ref, seg_ref, d_ref, z_ref, cin_ref, x_s, carry, c_s, u_s, z_s):
        t = pl.program_id(2)

        @pl.when(t == 0)
        def _():
            carry[...] = jnp.zeros_like(carry)

        cin_ref[...] = carry[...]
        _to_segments(u_s, u_ref, seg)
        uf = u_s[...]
        x_s[...] = jnp.dot(uf.astype(BF16), bd_ref[...], preferred_element_type=F32)
        _seg_scan(x_s, pw_ref, seg_ref, carry, c_s, seg, False)
        y = jnp.dot(x_s[...].astype(BF16), cd_ref[...], preferred_element_type=F32) + d_ref[...] * uf
        z_s[...] = _gelu(y)
        _from_segments(z_ref, z_s, seg)

    (z, carries), moved = _carried_call(
        body, name="s5_fwd", grid=(nch, B, nt),
        in_specs=[pl.BlockSpec((tm, LANES), lambda j, b, t: (b * nt + t, j)),
                  pl.BlockSpec((None, LANES, W), lambda j, b, t: (j, 0, 0)),
                  pl.BlockSpec((None, W, LANES), lambda j, b, t: (j, 0, 0)),
                  pl.BlockSpec((None, 2, seg, W), lambda j, b, t: (j, 0, 0, 0)),
                  pl.BlockSpec((None, 6, SUBLANES, W), lambda j, b, t: (j, 0, 0, 0)),
                  pl.BlockSpec((1, LANES), lambda j, b, t: (0, j))],
        out_specs=[pl.BlockSpec((tm, LANES), lambda j, b, t: (b * nt + t, j)),
                   pl.BlockSpec((None, None, SUBLANES, W), lambda j, b, t: (j, b * nt + t, 0, 0))],
        out_shape=[jax.ShapeDtypeStruct((N, D), BF16), jax.ShapeDtypeStruct((nch, B * nt, SUBLANES, W), F32)],
        scratch_shapes=[pltpu.VMEM((tm, W), F32), pltpu.VMEM((SUBLANES, W), F32), pltpu.VMEM((SUBLANES, W), F32),
                        pltpu.VMEM((tm, LANES), F32), pltpu.VMEM((tm, LANES), F32)],
        operands=(u, bd, cd, pw, seg_f, d_skip), sem=("parallel", "arbitrary", "arbitrary"), carry=carry)
    return z, carries, moved


def _s5_bwd(u, dz, bd, cd, pw, seg_f, seg_b, d_skip, carries, B, carry=None):
    N, D = u.shape
    S = N // B
    nch = D // LANES
    W = 2 * CHUNK_STATE
    tm, seg = S5_TILE, S5_SEG
    nt = S // tm
    tn_dims = (((0,), (0,)), ((), ()))
    nt_dims = (((1,), (1,)), ((), ()))

    def body(u_ref, dz_ref, bd_ref, cd_ref, pw_ref, sf_ref, sb_ref, d_ref, cin_ref,
             du_ref, dbd_ref, dcd_ref, da_ref, dd_ref, x_s, l_s, carry, lcarry, c_s, lc_s, u_s, t_s):
        b = pl.program_id(1)
        t = pl.program_id(2)

        @pl.when((b == 0) & (t == 0))
        def _():
            dbd_ref[...] = jnp.zeros_like(dbd_ref)
            dcd_ref[...] = jnp.zeros_like(dcd_ref)
            da_ref[...] = jnp.zeros_like(da_ref)
            dd_ref[...] = jnp.zeros_like(dd_ref)

        @pl.when(t == 0)
        def _():
            lcarry[...] = jnp.zeros_like(lcarry)

        _to_segments(u_s, u_ref, seg)
        _to_segments(t_s, dz_ref, seg)
        uf = u_s[...]
        uv = uf.astype(BF16)
        carry[...] = cin_ref[...]
        x_s[...] = jnp.dot(uv, bd_ref[...], preferred_element_type=F32)
        _seg_scan(x_s, pw_ref, sf_ref, carry, c_s, seg, False)
        xb = x_s[...].astype(BF16)
        y = jnp.dot(xb, cd_ref[...], preferred_element_type=F32) + d_ref[...] * uf
        dy = t_s[...] * _gelu_grad(y)
        dd_ref[...] += jnp.sum(dy * uf, axis=0, keepdims=True)
        dyb = dy.astype(BF16)
        dcd_ref[...] += lax.dot_general(xb, dyb, tn_dims, preferred_element_type=F32)
        l_s[...] = lax.dot_general(dyb, cd_ref[...], nt_dims, preferred_element_type=F32)
        _seg_scan(l_s, pw_ref, sb_ref, lcarry, lc_s, seg, True)
        lb = l_s[...].astype(BF16)
        dbd_ref[...] += lax.dot_general(uv, lb, tn_dims, preferred_element_type=F32)
        t_s[...] = lax.dot_general(lb, bd_ref[...], nt_dims, preferred_element_type=F32) + d_ref[...] * dy
        _from_segments(du_ref, t_s, seg)
        lam_rest, x_prev = l_s[SUBLANES:, :], x_s[:tm - SUBLANES, :]
        lam_0, c_in = l_s[:SUBLANES, :], c_s[...]
        da_ref[0:1, :] += (jnp.sum(lam_rest * x_prev, axis=0, keepdims=True) + jnp.sum(lam_0 * c_in, axis=0, keepdims=True))
        da_ref[1:2, :] += (jnp.sum(lam_rest * _swap_halves(x_prev), axis=0, keepdims=True)
                           + jnp.sum(lam_0 * _swap_halves(c_in), axis=0, keepdims=True))

    tile = lambda j, b, t: (b * nt + (nt - 1 - t), j)
    chunk3 = lambda j, b, t: (j, 0, 0)
    chunk4 = lambda j, b, t: (j, 0, 0, 0)
    outs, moved = _carried_call(
        body, name="s5_bwd", grid=(nch, B, nt),
        in_specs=[pl.BlockSpec((tm, LANES), tile), pl.BlockSpec((tm, LANES), tile),
                  pl.BlockSpec((None, LANES, W), chunk3), pl.BlockSpec((None, W, LANES), chunk3),
                  pl.BlockSpec((None, 2, seg, W), chunk4), pl.BlockSpec((None, 6, SUBLANES, W), chunk4),
                  pl.BlockSpec((None, 6, SUBLANES, W), chunk4), pl.BlockSpec((1, LANES), lambda j, b, t: (0, j)),
                  pl.BlockSpec((None, None, SUBLANES, W), lambda j, b, t: (j, b * nt + (nt - 1 - t), 0, 0))],
        out_specs=[pl.BlockSpec((tm, LANES), tile), pl.BlockSpec((None, LANES, W), chunk3),
                   pl.BlockSpec((None, W, LANES), chunk3), pl.BlockSpec((None, 2, W), chunk3),
                   pl.BlockSpec((1, LANES), lambda j, b, t: (0, j))],
        out_shape=[jax.ShapeDtypeStruct((N, D), F32), jax.ShapeDtypeStruct((nch, LANES, W), F32),
                   jax.ShapeDtypeStruct((nch, W, LANES), F32), jax.ShapeDtypeStruct((nch, 2, W), F32),
                   jax.ShapeDtypeStruct((1, D), F32)],
        scratch_shapes=[pltpu.VMEM((tm, W), F32), pltpu.VMEM((tm, W), F32)] + [pltpu.VMEM((SUBLANES, W), F32)] * 4
        + [pltpu.VMEM((tm, LANES), F32)] * 2,
        operands=(u, dz, bd, cd, pw, seg_f, seg_b, d_skip, carries), sem=("parallel", "arbitrary", "arbitrary"), carry=carry)
    return (*outs, moved)


ATTN_HEADS = LANES // HEAD_DIM
ATTN_FWD_UNROLL = 4
ATTN_BWD_UNROLL = 2


def _attn_mask(n):
    qi = lax.broadcasted_iota(jnp.int32, (ATTN_BLOCK, 2 * ATTN_BLOCK), 0)
    kj = lax.broadcasted_iota(jnp.int32, (ATTN_BLOCK, 2 * ATTN_BLOCK), 1)
    prev_ok = (kj < ATTN_BLOCK) & (kj >= qi) & (n > 0)
    return prev_ok | ((kj >= ATTN_BLOCK) & (kj - ATTN_BLOCK <= qi))


def _head_lanes(h):
    lane = lax.broadcasted_iota(jnp.int32, (ATTN_BLOCK, LANES), 1)
    return (lane >= h * HEAD_DIM) & (lane < (h + 1) * HEAD_DIM)


def _per_head(cols):
    out = jnp.broadcast_to(cols[-1], (ATTN_BLOCK, LANES))
    for h in range(len(cols) - 2, -1, -1):
        out = jnp.where(_head_lanes(h), jnp.broadcast_to(cols[h], (ATTN_BLOCK, LANES)), out)
    return out


def _only_head(x, h):
    return jnp.where(_head_lanes(h), x, 0.0).astype(BF16)


def _block_rows(tb, dil, nb):
    r = tb // nb
    n = tb % nb
    start = r + dil * ATTN_BLOCK * n
    startp = jnp.where(n > 0, start - dil * ATTN_BLOCK, start)
    return n, pl.ds(start, ATTN_BLOCK, stride=dil), pl.ds(startp, ATTN_BLOCK, stride=dil)


def _attn_fwd(q, k, v, B, carry=None):
    _, S, D3 = q.shape
    D = D3 // 3
    HP = D // LANES
    scale = HEAD_DIM ** -0.5
    n_blocks = S // ATTN_BLOCK
    nbr = len(DILATIONS)
    nt_dims = (((1,), (1,)), ((), ()))

    def branch(dil, q_ref, k_ref, v_ref, acc, m_s, l_s):
        nb = (S // dil) // ATTN_BLOCK

        def blk(tb, _):
            n, rows, rowsp = _block_rows(tb, dil, nb)
            qb = q_ref[rows, :] * scale
            kk = jnp.concatenate([k_ref[rowsp, :], k_ref[rows, :]], axis=0).astype(BF16)
            vv = jnp.concatenate([v_ref[rowsp, :], v_ref[rows, :]], axis=0).astype(BF16)
            ok = _attn_mask(n)
            ms, ls, accs = [], [], []
            for h in range(ATTN_HEADS):
                s = lax.dot_general(_only_head(qb, h), kk, nt_dims, preferred_element_type=F32)
                s = jnp.where(ok, s, NEG)
                mh = jnp.max(s, axis=-1, keepdims=True)
                p = jnp.exp(s - mh)
                ms.append(mh)
                ls.append(jnp.sum(p, axis=-1, keepdims=True))
                accs.append(jnp.dot(p.astype(BF16), vv, preferred_element_type=F32))
            m_s[rows, :] = _per_head(ms)
            l_s[rows, :] = _per_head(ls)
            acc[rows, :] = _per_head(accs)
            return 0

        lax.fori_loop(0, n_blocks, blk, 0, unroll=ATTN_FWD_UNROLL)

    def body(q_ref, k_ref, v_ref, o_ref, lse_ref, *scratch):
        accs, m_ss, l_ss = scratch[:nbr], scratch[nbr:2 * nbr], scratch[2 * nbr:]
        g = pl.program_id(2)
        for gi, dil in enumerate(DILATIONS):
            pl.when(g == gi)(functools.partial(branch, dil, q_ref, k_ref, v_ref, accs[gi], m_ss[gi], l_ss[gi]))

        @pl.when(g == nbr - 1)
        def _():
            def fin(i, _):
                rows = pl.ds(pl.multiple_of(i * ATTN_BLOCK, ATTN_BLOCK), ATTN_BLOCK)
                ms = [m[rows, :] for m in m_ss]
                m_all = functools.reduce(jnp.maximum, ms)
                ws = [jnp.exp(m - m_all) for m in ms]
                den = sum(w * l[rows, :] for w, l in zip(ws, l_ss))
                o_ref[rows, :] = sum(w * a[rows, :] for w, a in zip(ws, accs)) / den
                lse_ref[rows, :] = m_all + jnp.log(den)
                return 0

            lax.fori_loop(0, n_blocks, fin, 0)

    br = pl.BlockSpec((None, S, LANES), lambda b, hp, g: (b, 0, g * HP + hp))
    hd = pl.BlockSpec((None, S, LANES), lambda b, hp, g: (b, 0, hp))
    (o, lse), moved = _carried_call(
        body, name="attn_fwd", grid=(B, HP, nbr), in_specs=[br, br, br], out_specs=[hd, hd],
        out_shape=[jax.ShapeDtypeStruct((B, S, D), F32), jax.ShapeDtypeStruct((B, S, D), F32)],
        scratch_shapes=[pltpu.VMEM((S, LANES), F32)] * (3 * nbr),
        operands=(q, k, v), sem=("parallel", "parallel", "arbitrary"), carry=carry)
    return o, lse, moved


def _attn_bwd(q, k, v, o, lse, do, dk_prev, dv_prev, B, last, carry=None):
    _, S, D3 = q.shape
    D = D3 // 3
    HP = D // LANES
    scale = HEAD_DIM ** -0.5
    n_blocks = S // ATTN_BLOCK
    has_prev = dk_prev is not None
    nt_dims = (((1,), (1,)), ((), ()))
    tn_dims = (((0,), (0,)), ((), ()))

    def branch(dil, q_ref, k_ref, v_ref, lse_ref, do_ref, dq_s, dk_c, dv_c, delta, dk_p, dv_p):
        nb = (S // dil) // ATTN_BLOCK

        def blk(tb, _):
            n, rows, rowsp = _block_rows(tb, dil, nb)
            qb = q_ref[rows, :] * scale
            dob, lb, db = do_ref[rows, :], lse_ref[rows, :], delta[rows, :]
            kk = jnp.concatenate([k_ref[rowsp, :], k_ref[rows, :]], axis=0).astype(BF16)
            vv = jnp.concatenate([v_ref[rowsp, :], v_ref[rows, :]], axis=0).astype(BF16)
            ok = _attn_mask(n)
            dqs = []
            dkk = dvv = None
            for h in range(ATTN_HEADS):
                qh, doh = _only_head(qb, h), _only_head(dob, h)
                lh = lb[:, h * HEAD_DIM:h * HEAD_DIM + 1]
                dlt = db[:, h * HEAD_DIM:h * HEAD_DIM + 1]
                s = lax.dot_general(qh, kk, nt_dims, preferred_element_type=F32)
                p = jnp.where(ok, jnp.exp(s - lh), 0.0)
                dp = lax.dot_general(doh, vv, nt_dims, preferred_element_type=F32)
                ds = (p * (dp - dlt)).astype(BF16)
                dqs.append(jnp.dot(ds, kk, preferred_element_type=F32))
                dk_h = lax.dot_general(ds, qh, tn_dims, preferred_element_type=F32)
                dv_h = lax.dot_general(p.astype(BF16), doh, tn_dims, preferred_element_type=F32)
                dkk = dk_h if dkk is None else dkk + dk_h
                dvv = dv_h if dvv is None else dvv + dv_h
            dq_s[rows, :] = _per_head(dqs) * scale
            dk_p[rowsp, :] = dkk[:ATTN_BLOCK]
            dv_p[rowsp, :] = dvv[:ATTN_BLOCK]
            dk_c[rows, :] = dkk[ATTN_BLOCK:]
            dv_c[rows, :] = dvv[ATTN_BLOCK:]
            return 0

        lax.fori_loop(0, n_blocks, blk, 0, unroll=ATTN_BWD_UNROLL)

    def body(*refs):
        q_ref, k_ref, v_ref, o_ref, lse_ref, do_ref = refs[:6]
        n_in = 8 if has_prev else 6
        dq_ref, dk_ref, dv_ref, delta, dk_p, dv_p, dq_s, dk_c, dv_c = refs[n_in:n_in + 9]
        g = pl.program_id(2)

        @pl.when(g == 0)
        def _():
            def dl(i, _):
                rows = pl.ds(pl.multiple_of(i * ATTN_BLOCK, ATTN_BLOCK), ATTN_BLOCK)
                prod = do_ref[rows, :] * o_ref[rows, :]
                delta[rows, :] = _per_head([jnp.sum(jnp.where(_head_lanes(h), prod, 0.0), axis=-1, keepdims=True)
                                            for h in range(ATTN_HEADS)])
                return 0

            lax.fori_loop(0, n_blocks, dl, 0)

        dk_p[...] = jnp.zeros_like(dk_p)
        dv_p[...] = jnp.zeros_like(dv_p)
        for gi, dil in enumerate(DILATIONS):
            pl.when(g == gi)(functools.partial(branch, dil, q_ref, k_ref, v_ref, lse_ref, do_ref, dq_s, dk_c, dv_c,
                                               delta, dk_p, dv_p))

        def fin(i, _):
            rows = pl.ds(pl.multiple_of(i * ATTN_BLOCK, ATTN_BLOCK), ATTN_BLOCK)
            dk_t = dk_c[rows, :] + dk_p[rows, :]
            dv_t = dv_c[rows, :] + dv_p[rows, :]
            if has_prev:
                dk_t = dk_t + refs[6][rows, :].astype(F32)
                dv_t = dv_t + refs[7][rows, :].astype(F32)
            dq_ref[rows, :] = dq_s[rows, :].astype(dq_ref.dtype)
            dk_ref[rows, :] = dk_t.astype(dk_ref.dtype)
            dv_ref[rows, :] = dv_t.astype(dv_ref.dtype)
            return 0

        lax.fori_loop(0, n_blocks, fin, 0)

    br = pl.BlockSpec((None, S, LANES), lambda b, hp, g: (b, 0, g * HP + hp))
    hd = pl.BlockSpec((None, S, LANES), lambda b, hp, g: (b, 0, hp))
    ins = [q, k, v, o, lse, do] + ([dk_prev, dv_prev] if has_prev else [])
    kv_dtype = BF16 if last else F32
    (dq, dk, dv), moved = _carried_call(
        body, name="attn_bwd", grid=(B, HP, len(DILATIONS)),
        in_specs=[br, br, br, hd, hd, hd] + ([br, br] if has_prev else []), out_specs=[br, br, br],
        out_shape=[jax.ShapeDtypeStruct(q.shape, BF16), jax.ShapeDtypeStruct(q.shape, kv_dtype),
                   jax.ShapeDtypeStruct(q.shape, kv_dtype)],
        scratch_shapes=[pltpu.VMEM((S, LANES), F32)] * 6,
        operands=ins, sem=("parallel", "parallel", "arbitrary"), carry=carry)
    return dq, dk, dv, moved


def _adamw(w, grads, m, v):
    R, C = w.shape
    tr = _div(R, 256, SUBLANES)
    ng = len(grads)
    c1 = 1.0 - ADAM_B1 ** ADAM_STEP
    c2 = 1.0 - ADAM_B2 ** ADAM_STEP

    def body(*refs):
        w_ref, m_ref, v_ref = refs[0], refs[1 + ng], refs[2 + ng]
        d_ref, mo_ref, vo_ref = refs[3 + ng:6 + ng]
        g = refs[1][...]
        if ng == 2:
            g = g + refs[2][...]
            refs[6 + ng][...] = g
        mn = ADAM_B1 * m_ref[...] + (1.0 - ADAM_B1) * g
        vn = ADAM_B2 * v_ref[...] + (1.0 - ADAM_B2) * (g * g)
        d_ref[...] = -ADAM_LR * ((mn / c1) / (jnp.sqrt(vn / c2) + ADAM_EPS) + ADAM_WD * w_ref[...])
        mo_ref[...] = mn
        vo_ref[...] = vn

    blk = pl.BlockSpec((tr, C), lambda i: (i, 0))
    n_out = 3 + (ng == 2)
    outs = pl.pallas_call(
        body, name="adamw", grid=(R // tr,), in_specs=[blk] * (3 + ng), out_specs=[blk] * n_out,
        out_shape=[jax.ShapeDtypeStruct((R, C), F32)] * n_out, compiler_params=_params("parallel"),
    )(w, *grads, m, v)
    return (outs[3] if ng == 2 else grads[0],) + tuple(outs[:3])


def _sum_shards(recv):
    n, R, C = recv.shape
    tr = _div(R, 256, SUBLANES if recv.dtype == F32 else 2 * SUBLANES)

    def body(r_ref, o_ref):
        s = r_ref[0].astype(F32)
        for i in range(1, n):
            s = s + r_ref[i].astype(F32)
        o_ref[...] = s

    return pl.pallas_call(
        body, name="sum_shards", grid=(R // tr,), in_specs=[pl.BlockSpec((n, tr, C), lambda i: (0, i, 0))],
        out_specs=pl.BlockSpec((tr, C), lambda i: (i, 0)), out_shape=jax.ShapeDtypeStruct((R, C), F32),
        compiler_params=_params("parallel"),
    )(recv)


N_DEV = 8
N_CHIPS = 4


def _all_gather_small(x):
    m_per, n = x.shape

    def body(x_ref, out_ref, send_sems, recv_sems, local_sem):
        cx, cy, cc = _coords()
        me, sibling = (cx, cy, cc), (cx, cy, 1 - cc)
        chips = [(1 - cx, cy), (cx, 1 - cy), (1 - cx, 1 - cy)]

        def rows(px, py, pc):
            return out_ref.at[pl.ds((4 * px + 2 * py + pc) * m_per, m_per), :]

        def copy(k, block, to, src=None):
            return pltpu.make_async_remote_copy(
                src_ref=rows(*block) if src is None else src, dst_ref=rows(*block), send_sem=send_sems.at[k],
                recv_sem=recv_sems.at[k], device_id=to, device_id_type=MESH)

        mine = pltpu.make_async_copy(x_ref, rows(*me), local_sem)
        mine.start()
        first = [copy(0, me, sibling, src=x_ref)]
        first += [copy(1 + j, me, (*chip, cc), src=x_ref) for j, chip in enumerate(chips)]
        for cp in first:
            cp.start()
        passed = [copy(4 + j, (*chip, cc), sibling) for j, chip in enumerate(chips)]
        for j, chip in enumerate(chips):
            copy(1 + j, (*chip, cc), me).wait_recv()
            passed[j].start()
        copy(0, sibling, me).wait_recv()
        for j, chip in enumerate(chips):
            copy(4 + j, (*chip, 1 - cc), me).wait_recv()
        for cp in first + passed:
            cp.wait_send()
        mine.wait()

    return pl.pallas_call(
        body, name="all_gather_small", out_shape=jax.ShapeDtypeStruct((N_DEV * m_per, n), x.dtype),
        in_specs=[pl.BlockSpec(memory_space=pltpu.VMEM)], out_specs=pl.BlockSpec(memory_space=pltpu.VMEM),
        scratch_shapes=[pltpu.SemaphoreType.DMA((7,)), pltpu.SemaphoreType.DMA((7,)), pltpu.SemaphoreType.DMA],
        compiler_params=pltpu.CompilerParams(vmem_limit_bytes=VMEM_LIMIT),
    )(x)


def _exchange(name, carry):
    return _carried_call(lambda: None, name=name, grid=(1,), in_specs=[], out_specs=[], out_shape=[], scratch_shapes=[],
                         operands=(), sem=("arbitrary",), carry=carry)[1]


def _layer_moves(kind, arrays_from, arrays_to, pieces, layer_major=()):
    used = sorted({w for w, _ in pieces})
    pos = {w: i for i, w in enumerate(used)}
    gather = kind == "gather"

    def half(ref, c):
        rows = ref.shape[0] // 2
        return ref.at[pl.ds(c * rows, rows), :]

    def slot(d, w, chip, l):
        return d.at[l, chip] if w in layer_major else d.at[chip, l]

    def plan(src_refs, dst_refs, me):
        cx, cy, cc = me
        mine = 2 * cx + cy
        remote, local = [], []
        for w, l in pieces:
            s, d = src_refs[pos[w]], dst_refs[pos[w]]
            for px, py in _other_chips(cx, cy):
                if gather:
                    remote.append((half(s.at[l], cc), half(slot(d, w, mine, l), cc), (px, py, cc)))
                else:
                    remote.append((s.at[2 * px + py, l], d.at[mine, l], (px, py, cc)))
            local.append((s.at[l], slot(d, w, mine, l)) if gather else (s.at[mine, l], d.at[mine, l]))
        return remote, local

    def onward(src_refs, dst_refs, me):
        cx, cy, cc = me
        moves = []
        for w, l in pieces:
            d = dst_refs[pos[w]]
            for px, py in _other_chips(cx, cy):
                landed = half(slot(d, w, 2 * px + py, l), cc)
                moves.append((landed, landed, (cx, cy, 1 - cc)))
        return moves

    n = 3 * len(pieces)
    carry = _Carry([arrays_from[w] for w in used], [arrays_to[w] for w in used], plan, n, len(pieces),
                   onward if gather else None, n if gather else 0)
    return carry, used


def _swap_with_sibling(sums):
    def plan(src_refs, dst_refs, me):
        cx, cy, cc = me
        return [(s, d, (cx, cy, 1 - cc)) for s, d in zip(src_refs, dst_refs)], []

    return _exchange("swap_with_sibling", _Carry(sums, [lax.empty(s.shape, s.dtype) for s in sums], plan, len(sums), 0))


def _pack(arrs, width):
    parts, layout, row = [], [], 0
    for a in arrs:
        flat = a.reshape(-1).astype(F32)
        rows = -(-flat.shape[0] // (width * SUBLANES)) * SUBLANES
        parts.append(jnp.pad(flat, (0, rows * width - flat.shape[0])).reshape(rows, width))
        layout.append((row, rows, a.shape))
        row += rows
    pad = -row % (8 * SUBLANES) if row > 8 * SUBLANES else 0
    if pad:
        parts.append(jnp.zeros((pad, width), F32))
    return jnp.concatenate(parts, axis=0), layout, row + pad


def _unpack(buf, layout, idx):
    row, rows, shape = layout[idx]
    size = math.prod(shape)
    return buf[row:row + rows].reshape(-1)[:size].reshape(shape)


def kernel(x, c, ln_g, ada_w, ada_b, ssm_lam_re, ssm_lam_im, ssm_log_dt, ssm_b_re, ssm_b_im, ssm_c_re, ssm_c_im, ssm_d, ssm_w_glu, kv_g, kv_ada_w, kv_ada_b, w_kv, attn_w_q, attn_w_o, mlp_w1, mlp_w2, final_g, loss_target, m_ln_g, m_ada_w, m_ada_b, m_ssm_lam_re, m_ssm_lam_im, m_ssm_log_dt, m_ssm_b_re, m_ssm_b_im, m_ssm_c_re, m_ssm_c_im, m_ssm_d, m_ssm_w_glu, m_kv_g, m_kv_ada_w, m_kv_ada_b, m_w_kv, m_attn_w_q, m_attn_w_o, m_mlp_w1, m_mlp_w2, m_final_g, v_ln_g, v_ada_w, v_ada_b, v_ssm_lam_re, v_ssm_lam_im, v_ssm_log_dt, v_ssm_b_re, v_ssm_b_im, v_ssm_c_re, v_ssm_c_im, v_ssm_d, v_ssm_w_glu, v_kv_g, v_kv_ada_w, v_kv_ada_b, v_w_kv, v_attn_w_q, v_attn_w_o, v_mlp_w1, v_mlp_w2, v_final_g):
    B, S, D = x.shape
    N = B * S
    depth = ln_g.shape[0]
    n_a = ssm_w_glu.shape[0]
    n_b = attn_w_q.shape[0]
    FF = mlp_w1.shape[2] * N_CHIPS
    cx, cy, cc = _coords()
    chip = 2 * cx + cy
    dev = 4 * cx + 2 * cy + cc
    n_ex = N_DEV * B
    ada_cols = ada_w.shape[-1]
    kv_cols = kv_ada_w.shape[-1]

    GLU, KV, Q, O, W1, W2 = range(6)
    shards = [ssm_w_glu.astype(BF16), w_kv.astype(BF16)[None], attn_w_q.astype(BF16), attn_w_o.astype(BF16),
              mlp_w1.astype(BF16), mlp_w2.astype(BF16)]
    row_sharded = (O, W2)
    wg = [lax.empty((s.shape[0], N_CHIPS) + s.shape[1:] if w in row_sharded else (N_CHIPS,) + s.shape, BF16)
          for w, s in enumerate(shards)]

    def whole_rows(w):
        L, _, R, C = wg[w].shape
        return wg[w].reshape(1, L, N_CHIPS * R, C)

    def fetch(pieces):
        return _layer_moves("gather", shards, wg, pieces, layer_major=row_sharded)

    def landed(arrays, used, moved):
        for w, a in zip(used, moved):
            arrays[w] = a

    fetch_with = {l: [(W1, l), (W2, l)] for l in range(depth)}
    fetch_with[0] += [(Q, 0), (O, 0)]
    fetch_with[n_a - 1] += [(KV, 0)]
    for j in range(1, n_b):
        fetch_with[n_a + j - 1] += [(Q, j), (O, j)]
    carry, used = fetch([(GLU, l) for l in range(n_a)])
    landed(wg, used, _exchange("gather_weights", carry))

    c_pack, c_layout, _ = _pack([c], D)
    c_all_buf = _all_gather_small(c_pack)
    c_rows = c_pack.shape[0]
    c_all = jnp.concatenate([_unpack(c_all_buf[d * c_rows:(d + 1) * c_rows], c_layout, 0) for d in range(N_DEV)], axis=0)
    sc_all = jax.nn.silu(c_all).astype(BF16)
    n_mod = depth * 2
    ada_w8 = ada_w.reshape(n_mod, 1, D, ada_cols)
    ada_b_row = ada_b.reshape(1, n_mod * ada_cols)
    mod_local = _mm("ada_fwd", sc_all, ada_w8, mode="nn", M=n_ex, N=n_mod * ada_cols, K=D, b_lay="cs", b_ns=n_mod,
                    epi=_add, extras=[("n", ada_b_row)])
    kv_ada_b_local = lax.dynamic_slice(kv_ada_b.reshape(N_CHIPS, kv_cols), (chip, 0), (1, kv_cols))
    kvmod_local = _mm("ada_fwd", sc_all, _as4(kv_ada_w), mode="nn", M=n_ex, N=kv_cols, K=D, epi=_add,
                      extras=[("n", kv_ada_b_local)])
    mod_pack, mod_layout, mod_rows = _pack([mod_local, kvmod_local, ln_g, ssm_d], D)
    mod_buf = _all_gather_small(mod_pack)

    def from_chip(j, idx):
        d = 2 * j
        return _unpack(mod_buf[d * mod_rows:(d + 1) * mod_rows], mod_layout, idx)

    my_rows = lambda a: lax.dynamic_slice_in_dim(a, dev * B, B, axis=0)
    mods = jnp.concatenate([my_rows(from_chip(j, 0)).reshape(B, n_mod, ada_cols) for j in range(N_CHIPS)], axis=2)
    kvmod = jnp.concatenate([my_rows(from_chip(j, 1)) for j in range(N_CHIPS)], axis=1)
    ln_g_full = jnp.concatenate([from_chip(j, 2) for j in range(N_CHIPS)], axis=2)
    ssm_d_full = jnp.concatenate([from_chip(j, 3) for j in range(N_CHIPS)], axis=1)

    def mod3(l, s):
        mrow = mods[:, l * 2 + s]
        return [mrow[:, i * D:(i + 1) * D].reshape(B, 1, D) for i in range(3)]

    kv_shift, kv_scale = kvmod[:, :D].reshape(B, 1, D), kvmod[:, D:].reshape(B, 1, D)

    s5_tabs = []
    for l in range(n_a):
        prm = (ssm_lam_re[l], ssm_lam_im[l], ssm_log_dt[l], ssm_b_re[l], ssm_b_im[l], ssm_c_re[l], ssm_c_im[l])
        (bd, cd, _, _), disc_vjp = jax.vjp(_s5_discretize, *prm)
        pw, seg_f, seg_b = _s5_scan_coefs(ssm_lam_re[l], ssm_lam_im[l], ssm_log_dt[l], S5_SEG)
        s5_tabs.append((bd.astype(BF16), cd.astype(BF16), pw, seg_f, seg_b, disc_vjp))

    h = x.reshape(N, D)
    saved = []
    k_all = v_all = None
    for l in range(depth):
        sv = {}
        shift, scale, gate = mod3(l, 0)
        sv["h0"], sv["scale0"], sv["gate0"] = h, scale, gate
        u = _normmod(h, ln_g_full[l, 0].reshape(1, D), scale, shift, B)
        sv["u0"] = u
        carry, used = fetch(fetch_with[l])
        if l < n_a:
            bd, cd, pw, seg_f, _, _ = s5_tabs[l]
            z, carries, moved = _s5_fwd(u, bd, cd, pw, seg_f, ssm_d_full[l].reshape(1, D), B, carry)
            landed(wg, used, moved)
            zz = _mm("glu_proj", z, wg[GLU], mode="nn", M=N, N=2 * D, K=D, b_lay="cs", b_l=l, b_ns=N_CHIPS)
            y, h_next = _glu_residual(zz, h, gate, B)
            sv["z"], sv["carries"], sv["zz"] = z, carries, zz
        else:
            j = l - n_a
            q = _mm("q_proj", u, wg[Q], mode="nn", M=N, N=3 * D, K=D, b_lay="cs", b_l=j, b_ns=N_CHIPS)
            q3 = q.reshape(B, S, 3 * D)
            o, lse, moved = _attn_fwd(q3, k_all, v_all, B, carry)
            landed(wg, used, moved)
            o2 = o.reshape(N, D)
            y, h_next = _mm("o_proj", o2, whole_rows(O), mode="nn", M=N, N=D, K=D, b_l=j,
                            out_dtype=(BF16, F32), epi=_gated_residual, extras=[("mn", h), ("ex", gate)], rows_per_ex=S)
            sv["q"], sv["o"], sv["lse"] = q3, o, lse
        sv["y0"] = y
        h = h_next
        shift, scale, gate = mod3(l, 1)
        sv["h1"], sv["scale1"], sv["gate1"] = h, scale, gate
        u = _normmod(h, ln_g_full[l, 1].reshape(1, D), scale, shift, B)
        r = _mm("mlp_up", u, wg[W1], mode="nn", M=N, N=FF, K=D, b_lay="cs", b_l=l, b_ns=N_CHIPS, out_dtype=BF16, epi=_relu2)
        y, h = _mm("mlp_down", r, whole_rows(W2), mode="nn", M=N, N=D, K=FF, b_l=l, tk=2048,
                   out_dtype=(BF16, F32), epi=_gated_residual, extras=[("mn", h), ("ex", gate)], rows_per_ex=S)
        sv["u1"], sv["r"], sv["y1"] = u, r, y
        saved.append(sv)
        if l == n_a - 1:
            h_kv = h
            u_kv = _normmod(h, kv_g.reshape(1, D), kv_scale, kv_shift, B)
            half = N_CHIPS // 2
            k_all = _mm("kv_proj", u_kv, wg[KV], mode="nn", M=N, N=3 * D, K=D, b_lay="cs", b_s0=0, b_ns=half).reshape(B, S, 3 * D)
            v_all = _mm("kv_proj", u_kv, wg[KV], mode="nn", M=N, N=3 * D, K=D, b_lay="cs", b_s0=half, b_ns=half).reshape(B, S, 3 * D)

    loss_buf, dh, d_final_g = _loss_head(h, final_g.reshape(1, D), loss_target.reshape(N, D))
    loss = lax.psum(loss_buf[0, 0], ("x", "y", "c"))

    dg = [lax.empty((N_CHIPS,) + s.shape, BF16) for s in shards]
    recv = [lax.empty((N_CHIPS,) + s.shape, BF16) for s in shards]

    def send(pieces):
        return _layer_moves("scatter", dg, recv, pieces)

    send_with = {l: [(W1, l), (W2, l)] for l in range(depth)}
    for l in range(n_a):
        send_with[l] += [(GLU, l)]
    for j in range(n_b):
        send_with[n_a + j] += [(O, j)]
        send_with[n_a + j - 1] += [(Q, j)]
    send_with[n_a - 1] += [(KV, 0)]
    d_ln_g = [[None, None] for _ in range(depth)]
    d_mods = [[None, None] for _ in range(depth)]
    d_s5 = [None] * n_a
    dk_acc = dv_acc = None
    half = N_CHIPS // 2

    def tn_grad(name, a, d, into, l, Mr, Nc, lay, s0=0, ns=N_CHIPS):
        return _mm(name, a, _as4(d), mode="tn", M=Mr, N=Nc, K=N, b_lay="cs", out_dtype=BF16, out_lay=lay,
                   out4_shape=into.shape, out_into=into, out_l=l, out_s0=s0, out_ns=ns, tk=2048)

    for l in reversed(range(depth)):
        sv = saved[l]
        dy, d_gate1 = _residual_bwd(dh, sv["gate1"], sv["y1"], B)
        dg[W2] = tn_grad("mlp_down_dw", sv["r"], dy, dg[W2], l, FF, D, "rs")
        da = _mm("mlp_down_dx", dy, whole_rows(W2), mode="nt", M=N, N=FF, K=D, b_l=l, out_dtype=BF16,
                 epi=_relu2_bwd, extras=[("mn", sv["r"])])
        dg[W1] = tn_grad("mlp_up_dw", sv["u1"], da, dg[W1], l, D, FF, "cs")
        du = _mm("mlp_up_dx", da, wg[W1], mode="nt", M=N, N=D, K=FF, b_lay="cs", b_l=l, b_ns=N_CHIPS)
        dh, dgv, d_scale1, d_shift1 = _normmod_bwd(du, sv["h1"], ln_g_full[l, 1].reshape(1, D), sv["scale1"], dh, B)
        d_ln_g[l][1] = dgv
        d_mods[l][1] = jnp.concatenate([d_shift1, d_scale1, d_gate1], axis=2)
        dy, d_gate0 = _residual_bwd(dh, sv["gate0"], sv["y0"], B)
        if l < n_a:
            bd, cd, pw, seg_f, seg_b, disc_vjp = s5_tabs[l]
            dzz = _glu_bwd(dy, sv["zz"])
            dg[GLU] = tn_grad("glu_proj_dw", sv["z"], dzz, dg[GLU], l, D, 2 * D, "cs")
            dz = _mm("glu_proj_dx", dzz, wg[GLU], mode="nt", M=N, N=D, K=2 * D, b_lay="cs", b_l=l, b_ns=N_CHIPS)
            carry, used = send(send_with[l])
            du, d_bd, d_cd, d_a2, d_dskip, moved = _s5_bwd(sv["u0"], dz, bd, cd, pw, seg_f, seg_b,
                                                           ssm_d_full[l].reshape(1, D), sv["carries"], B, carry)
            landed(recv, used, moved)
            d_are = (d_a2[:, 0, :CHUNK_STATE] + d_a2[:, 0, CHUNK_STATE:]).reshape(-1, SSM_STATE)
            d_aim = (d_a2[:, 1, CHUNK_STATE:] - d_a2[:, 1, :CHUNK_STATE]).reshape(-1, SSM_STATE)
            d_s5[l] = disc_vjp((d_bd, d_cd, d_are, d_aim)) + (d_dskip,)
        else:
            j = l - n_a
            dg[O] = tn_grad("o_proj_dw", sv["o"].reshape(N, D), dy, dg[O], j, D, D, "rs")
            do = _mm("o_proj_dx", dy, whole_rows(O), mode="nt", M=N, N=D, K=D, b_l=j)
            carry, used = send(send_with[l])
            dq, dk_acc, dv_acc, moved = _attn_bwd(sv["q"], k_all, v_all, sv["o"], sv["lse"], do.reshape(B, S, D),
                                                  dk_acc, dv_acc, B, l == n_a, carry)
            landed(recv, used, moved)
            dq2 = dq.reshape(N, 3 * D)
            dg[Q] = tn_grad("q_proj_dw", sv["u0"], dq2, dg[Q], j, D, 3 * D, "cs")
            du = _mm("q_proj_dx", dq2, wg[Q], mode="nt", M=N, N=D, K=3 * D, b_lay="cs", b_l=j, b_ns=N_CHIPS)
        dh, dgv, d_scale0, d_shift0 = _normmod_bwd(du, sv["h0"], ln_g_full[l, 0].reshape(1, D), sv["scale0"], dh, B)
        d_ln_g[l][0] = dgv
        d_mods[l][0] = jnp.concatenate([d_shift0, d_scale0, d_gate0], axis=2)
        if l == n_a:
            dk2, dv2 = dk_acc.reshape(N, 3 * D), dv_acc.reshape(N, 3 * D)
            dg[KV] = tn_grad("kv_proj_dw", u_kv, dk2, dg[KV], 0, D, 3 * D, "cs", s0=0, ns=half)
            dg[KV] = tn_grad("kv_proj_dw", u_kv, dv2, dg[KV], 0, D, 3 * D, "cs", s0=half, ns=half)
            du_kv = _mm("kv_proj_dx", dk2, wg[KV], mode="nt", M=N, N=D, K=3 * D, b_lay="cs", b_s0=0, b_ns=half)
            du_kv = _mm("kv_proj_dx", dv2, wg[KV], mode="nt", M=N, N=D, K=3 * D, b_lay="cs", b_s0=half, b_ns=half,
                        epi=_add, extras=[("mn", du_kv)])
            dh, d_kv_g, d_kv_scale, d_kv_shift = _normmod_bwd(du_kv, h_kv, kv_g.reshape(1, D), kv_scale, dh, B)
    grad_x = dh.reshape(B, S, D)

    own = [_sum_shards(r.reshape(N_CHIPS, -1, r.shape[-1])) for r in recv]
    other = _swap_with_sibling(own)

    d_kvmod = jnp.concatenate([d_kv_shift, d_kv_scale], axis=2).reshape(B, 2 * D)
    d_mod_all = jnp.concatenate([d_mods[l][s].reshape(B, 3 * D) for l in range(depth) for s in range(2)], axis=1)
    small = [
        d_mod_all, d_kvmod,
        jnp.stack([jnp.stack([d_ln_g[l][0].reshape(D), d_ln_g[l][1].reshape(D)]) for l in range(depth)]),
        jnp.stack([d_s5[l][0] for l in range(n_a)]), jnp.stack([d_s5[l][1] for l in range(n_a)]),
        jnp.stack([d_s5[l][2] for l in range(n_a)]),
        jnp.stack([d_s5[l][3] for l in range(n_a)]), jnp.stack([d_s5[l][4] for l in range(n_a)]),
        jnp.stack([d_s5[l][5] for l in range(n_a)]), jnp.stack([d_s5[l][6] for l in range(n_a)]),
        jnp.stack([d_s5[l][7].reshape(D) for l in range(n_a)]),
        d_kv_g.reshape(D), d_final_g.reshape(D),
    ]
    small_pack, small_layout, small_rows = _pack(small, D)
    small_buf = _all_gather_small(small_pack)
    small_sum = _sum_shards(small_buf.reshape(N_DEV, small_rows, D))
    red = lambda idx: _unpack(small_sum, small_layout, idx)
    per_dev = lambda idx: jnp.concatenate(
        [_unpack(small_buf[d * small_rows:(d + 1) * small_rows], small_layout, idx) for d in range(N_DEV)], axis=0)

    dm_all = per_dev(0).reshape(n_ex, n_mod, 3 * D)
    dm_cols = lax.dynamic_slice_in_dim(dm_all, chip * ada_cols, ada_cols, axis=2).reshape(n_ex, n_mod * ada_cols)
    g_ada_w = _mm("ada_dw", sc_all, _as4(dm_cols), mode="tn", M=D, N=n_mod * ada_cols, K=n_ex, b_lay="cs",
                  out_lay="cs", out4_shape=(n_mod, 1, D, ada_cols), out_ns=n_mod).reshape(ada_w.shape)
    dkvm_all = per_dev(1)
    dkvm_cols = lax.dynamic_slice_in_dim(dkvm_all, chip * kv_cols, kv_cols, axis=1)
    g_kv_ada_w = _mm("ada_dw", sc_all, _as4(dkvm_cols), mode="tn", M=D, N=kv_cols, K=n_ex, b_lay="cs")
    g_ada_b_full = (red(0)[0] + red(0)[1]).reshape(depth, 2, 3 * D) if B == 2 else jnp.sum(red(0), axis=0).reshape(depth, 2, 3 * D)
    g_ada_b = lax.dynamic_slice_in_dim(g_ada_b_full, chip * ada_cols, ada_cols, axis=2)
    g_kv_ada_b = red(1)[0] + red(1)[1] if B == 2 else jnp.sum(red(1), axis=0)
    g_ln_g = lax.dynamic_slice_in_dim(red(2), chip * (D // N_CHIPS), D // N_CHIPS, axis=2)
    g_ssm_d = lax.dynamic_slice_in_dim(red(10), chip * (D // N_CHIPS), D // N_CHIPS, axis=1)
    small_grads = {
        "ln_g": g_ln_g, "ada_b": g_ada_b, "ssm_lam_re": red(3), "ssm_lam_im": red(4), "ssm_log_dt": red(5),
        "ssm_b_re": red(6), "ssm_b_im": red(7), "ssm_c_re": red(8), "ssm_c_im": red(9), "ssm_d": g_ssm_d,
        "kv_g": red(11), "kv_ada_b": g_kv_ada_b, "final_g": red(12),
    }
    small_w = {"ln_g": (ln_g, m_ln_g, v_ln_g), "ada_b": (ada_b, m_ada_b, v_ada_b),
               "ssm_lam_re": (ssm_lam_re, m_ssm_lam_re, v_ssm_lam_re), "ssm_lam_im": (ssm_lam_im, m_ssm_lam_im, v_ssm_lam_im),
               "ssm_log_dt": (ssm_log_dt, m_ssm_log_dt, v_ssm_log_dt), "ssm_b_re": (ssm_b_re, m_ssm_b_re, v_ssm_b_re),
               "ssm_b_im": (ssm_b_im, m_ssm_b_im, v_ssm_b_im), "ssm_c_re": (ssm_c_re, m_ssm_c_re, v_ssm_c_re),
               "ssm_c_im": (ssm_c_im, m_ssm_c_im, v_ssm_c_im), "ssm_d": (ssm_d, m_ssm_d, v_ssm_d),
               "kv_g": (kv_g, m_kv_g, v_kv_g), "kv_ada_b": (kv_ada_b, m_kv_ada_b, v_kv_ada_b),
               "final_g": (final_g, m_final_g, v_final_g)}
    names = list(small_w)
    wp, lay_w, _ = _pack([small_w[n][0] for n in names], D)
    gp, _, _ = _pack([small_grads[n] for n in names], D)
    mp, _, _ = _pack([small_w[n][1] for n in names], D)
    vp, _, _ = _pack([small_w[n][2] for n in names], D)
    _, d_p, m_p, v_p = _adamw(wp, [gp], mp, vp)
    upd = {n: (small_grads[n].reshape(small_w[n][0].shape), _unpack(d_p, lay_w, i), _unpack(m_p, lay_w, i), _unpack(v_p, lay_w, i))
           for i, n in enumerate(names)}

    def big(w, m, v, g_own, g_other=None):
        C = w.shape[-1]
        gs = [g_own.reshape(-1, C)] + ([g_other.reshape(-1, C)] if g_other is not None else [])
        return tuple(t.reshape(w.shape) for t in _adamw(w.reshape(-1, C), gs, m.reshape(-1, C), v.reshape(-1, C)))

    upd["ssm_w_glu"] = big(ssm_w_glu, m_ssm_w_glu, v_ssm_w_glu, own[0], other[0])
    upd["w_kv"] = big(w_kv, m_w_kv, v_w_kv, own[1], other[1])
    upd["attn_w_q"] = big(attn_w_q, m_attn_w_q, v_attn_w_q, own[2], other[2])
    upd["attn_w_o"] = big(attn_w_o, m_attn_w_o, v_attn_w_o, own[3], other[3])
    upd["mlp_w1"] = big(mlp_w1, m_mlp_w1, v_mlp_w1, own[4], other[4])
    upd["mlp_w2"] = big(mlp_w2, m_mlp_w2, v_mlp_w2, own[5], other[5])
    upd["ada_w"] = big(ada_w, m_ada_w, v_ada_w, g_ada_w)
    upd["kv_ada_w"] = big(kv_ada_w, m_kv_ada_w, v_kv_ada_w, g_kv_ada_w)

    order = ["ln_g", "ada_w", "ada_b", "ssm_lam_re", "ssm_lam_im", "ssm_log_dt", "ssm_b_re", "ssm_b_im", "ssm_c_re",
             "ssm_c_im", "ssm_d", "ssm_w_glu", "kv_g", "kv_ada_w", "kv_ada_b", "w_kv", "attn_w_q", "attn_w_o", "mlp_w1",
             "mlp_w2", "final_g"]
    return (loss, grad_x, *[upd[n][0] for n in order], *[upd[n][1] for n in order], *[upd[n][2] for n in order],
            *[upd[n][3] for n in order])
```

```python
import functools
import math

import jax
import jax.numpy as jnp
from jax import lax
from jax.experimental import pallas as pl
from jax.experimental.pallas import tpu as pltpu

F32 = jnp.float32
BF16 = jnp.bfloat16
MESH = pl.DeviceIdType.MESH

EPS = 1e-6
NEG = -1e30
SSM_GROUP = 16
SSM_STATE = 64
HEAD_DIM = 64
ATTN_BLOCK = 128
DILATIONS = (1, 4, 16)
ADAM_LR, ADAM_B1, ADAM_B2, ADAM_EPS, ADAM_WD, ADAM_STEP = 0.001, 0.9, 0.999, 1e-08, 0.01, 10

LANES = 128
SUBLANES = 8
CHUNK_GROUPS = LANES // SSM_GROUP
CHUNK_STATE = CHUNK_GROUPS * SSM_STATE
VMEM_LIMIT = 56 * 1024 * 1024


def _div(dim, pref, mult):
    t = min(pref, dim) // mult * mult
    while t >= mult:
        if dim % t == 0:
            return t
        t -= mult
    return dim


def _params(*sem):
    return pltpu.CompilerParams(dimension_semantics=sem, vmem_limit_bytes=VMEM_LIMIT)


def _coords():
    return lax.axis_index("x"), lax.axis_index("y"), lax.axis_index("c")


def _other_chips(cx, cy):
    return [(1 - cx, cy), (cx, 1 - cy), (1 - cx, 1 - cy)]


class _Carry:
    def __init__(self, srcs, dsts, plan, n_remote, n_local, onward=None, n_onward=0):
        self.srcs, self.dsts, self.plan, self.n_remote, self.n_local = list(srcs), list(dsts), plan, n_remote, n_local
        self.onward, self.n_onward = onward, n_onward


def _carried_call(body, *, name, grid, in_specs, out_specs, out_shape, scratch_shapes, operands, sem, carry=None):
    if carry is None:
        outs = pl.pallas_call(body, name=name, grid=grid, in_specs=in_specs, out_specs=out_specs, out_shape=out_shape,
                              scratch_shapes=scratch_shapes, compiler_params=_params(*sem))(*operands)
        return list(outs), []
    n_in, n_out, n_scr = len(in_specs), len(out_specs), len(scratch_shapes)
    ns, nd = len(carry.srcs), len(carry.dsts)

    def wrapped(*refs):
        base_in, src_refs = refs[:n_in], refs[n_in:n_in + ns]
        o0 = n_in + ns + nd
        base_out, dst_refs = refs[o0:o0 + n_out], refs[o0 + n_out:o0 + n_out + nd]
        s0 = o0 + n_out + nd
        base_scr = refs[s0:s0 + n_scr]
        send_sems, recv_sems, local_sems = refs[s0 + n_scr:]
        pids = [pl.program_id(a) for a in range(len(grid))]
        first = functools.reduce(jnp.logical_and, [p == 0 for p in pids])
        last = functools.reduce(jnp.logical_and, [p == g - 1 for p, g in zip(pids, grid)])

        def remote_copies(moves, k0):
            return [pltpu.make_async_remote_copy(src_ref=s, dst_ref=d, send_sem=send_sems.at[k0 + i], recv_sem=recv_sems.at[k0 + i],
                                                 device_id=peer, device_id_type=MESH) for i, (s, d, peer) in enumerate(moves)]

        def copies():
            remote, local = carry.plan(src_refs, dst_refs, _coords())
            return remote_copies(remote, 0), [pltpu.make_async_copy(s, d, local_sems.at[i]) for i, (s, d) in enumerate(local)]

        @pl.when(first)
        def _():
            remote, local = copies()
            for cp in local + remote:
                cp.start()

        body(*base_in, *base_out, *base_scr)

        @pl.when(last)
        def _():
            remote, local = copies()
            for cp in remote:
                cp.wait_send()
                cp.wait_recv()
            for cp in local:
                cp.wait()
            if carry.onward is not None:
                second = remote_copies(carry.onward(src_refs, dst_refs, _coords()), carry.n_remote)
                for cp in second:
                    cp.start()
                for cp in second:
                    cp.wait_send()
                    cp.wait_recv()

    anyspec = pl.BlockSpec(memory_space=pl.ANY)
    outs = pl.pallas_call(
        wrapped, name=name, grid=grid, in_specs=list(in_specs) + [anyspec] * (ns + nd),
        out_specs=list(out_specs) + [anyspec] * nd,
        out_shape=list(out_shape) + [jax.ShapeDtypeStruct(d.shape, d.dtype) for d in carry.dsts],
        scratch_shapes=list(scratch_shapes) + [pltpu.SemaphoreType.DMA((carry.n_remote + carry.n_onward,)),
                                               pltpu.SemaphoreType.DMA((carry.n_remote + carry.n_onward,)),
                                               pltpu.SemaphoreType.DMA((max(carry.n_local, 1),))],
        input_output_aliases={n_in + ns + i: n_out + i for i in range(nd)},
        compiler_params=_params(*(["arbitrary"] * len(grid))),
    )(*operands, *carry.srcs, *carry.dsts)
    return list(outs[:n_out]), list(outs[n_out:])


def _mm(name, a, b4, *, mode, M, N, K, b_lay="cs", b_l=0, b_s0=0, b_ns=1, out_dtype=F32, out_lay=None, out4_shape=None,
        out_into=None, out_l=0, out_s0=0, out_ns=1, epi=None, extras=(), rows_per_ex=None, tm=1024, tn=1024, tk=1024):
    _, _, bR, bC = b4.shape
    tm = _div(M, tm, SUBLANES if M % 16 else 16)
    brows, bcols = (N, K) if mode == "nt" else (K, N)
    if b_lay == "cs":
        assert bR == brows and bC * b_ns == bcols, (name, b4.shape, brows, bcols)
    else:
        assert bC == bcols and bR * b_ns == brows, (name, b4.shape, brows, bcols)
    n_lim = N
    k_lim = K
    if mode == "nt":
        if b_lay == "cs":
            k_lim = bC
        else:
            n_lim = bR
    else:
        if b_lay == "cs":
            n_lim = bC
        else:
            k_lim = bR
    if out_lay == "cs":
        oR, oC = out4_shape[2], out4_shape[3]
        assert oR == M and oC * out_ns == N, (name, out4_shape, M, N)
        n_lim = math.gcd(n_lim, oC)
    elif out_lay == "rs":
        oR, oC = out4_shape[2], out4_shape[3]
        assert oC == N and oR * out_ns == M, (name, out4_shape, M, N)
        tm = _div(oR, tm, SUBLANES)
    tn = _div(n_lim, tn, LANES)
    tk = _div(k_lim, tk, LANES if mode != "tn" else SUBLANES)
    if mode == "tn":
        tk = _div(k_lim, tk, 16) if k_lim % 16 == 0 else tk
    nk = K // tk
    grid = (M // tm, N // tn, nk)

    if mode == "tn":
        a_spec = pl.BlockSpec((tk, tm), lambda i, j, k: (k, i))
    else:
        a_spec = pl.BlockSpec((tm, tk), lambda i, j, k: (i, k))

    def b_index(ri, ci, br, bc):
        if b_lay == "cs":
            per = bC // bc
            return (b_s0 + ci // per, b_l, ri, ci % per)
        per = bR // br
        return (b_s0 + ri // per, b_l, ri % per, ci)

    if mode == "nt":
        b_spec = pl.BlockSpec((None, None, tn, tk), lambda i, j, k: b_index(j, k, tn, tk))
    else:
        b_spec = pl.BlockSpec((None, None, tk, tn), lambda i, j, k: b_index(k, j, tk, tn))

    in_specs = [a_spec, b_spec]
    operands = [a, b4]
    for kind, arr in extras:
        if kind == "mn":
            in_specs.append(pl.BlockSpec((tm, tn), lambda i, j, k: (i, j)))
        elif kind == "ex":
            per_ex = rows_per_ex // tm
            in_specs.append(pl.BlockSpec((None, 1, tn), lambda i, j, k: (i // per_ex, 0, j)))
        else:
            in_specs.append(pl.BlockSpec((1, tn), lambda i, j, k: (0, j)))
        operands.append(arr)
    n_extra = len(extras)

    multi = isinstance(out_dtype, tuple)
    n_out = len(out_dtype) if multi else 1
    if out_lay is None:
        out_shape = [jax.ShapeDtypeStruct((M, N), dt) for dt in (out_dtype if multi else (out_dtype,))]
        out_spec = [pl.BlockSpec((tm, tn), lambda i, j, k: (i, j)) for _ in range(n_out)]
    else:
        out_shape = [jax.ShapeDtypeStruct(tuple(out4_shape), out_dtype)]
        if out_lay == "cs":
            per_o = oC // tn
            out_spec = [pl.BlockSpec((None, None, tm, tn), lambda i, j, k: (out_s0 + j // per_o, out_l, i, j % per_o))]
        else:
            per_o = oR // tm
            out_spec = [pl.BlockSpec((None, None, tm, tn), lambda i, j, k: (out_s0 + i // per_o, out_l, i % per_o, j))]
    aliases = {}
    if out_into is not None:
        in_specs.append(pl.BlockSpec(memory_space=pl.ANY))
        operands.append(out_into)
        aliases = {len(operands) - 1: 0}

    dims = {"nn": (((1,), (0,)), ((), ())), "nt": (((1,), (1,)), ((), ())), "tn": (((0,), (0,)), ((), ()))}[mode]

    def body(a_ref, b_ref, *rest):
        extra_refs = rest[:n_extra]
        o_refs = rest[len(rest) - n_out - (nk > 1):len(rest) - (nk > 1)]

        def finish(r):
            if epi is not None:
                r = epi(r, *[e[...] for e in extra_refs])
            for o_ref, val in zip(o_refs, r if multi else (r,)):
                o_ref[...] = val.astype(o_ref.dtype)

        part = lax.dot_general(a_ref[...].astype(BF16), b_ref[...].astype(BF16), dims, preferred_element_type=F32)
        if nk == 1:
            finish(part)
            return
        acc = rest[-1]
        k = pl.program_id(2)

        @pl.when(k == 0)
        def _():
            acc[...] = part

        @pl.when(k != 0)
        def _():
            acc[...] += part

        @pl.when(k == nk - 1)
        def _():
            finish(acc[...])

    outs = pl.pallas_call(
        body, name=name, grid=grid, in_specs=in_specs, out_specs=out_spec, out_shape=out_shape,
        scratch_shapes=[pltpu.VMEM((tm, tn), F32)] if nk > 1 else [], input_output_aliases=aliases,
        compiler_params=_params("parallel", "parallel", "arbitrary"),
    )(*operands)
    return tuple(outs) if multi else outs[0]


def _as4(w):
    return w.reshape((1, 1) + w.shape)


def _relu2(acc):
    r = jnp.maximum(acc, 0.0)
    return r * r


def _relu2_bwd(acc, r):
    return acc * (2.0 * jnp.sqrt(r.astype(F32)))


def _add(acc, e):
    return acc + e


def _gated_residual(acc, h, gate):
    return acc, h + gate * acc


def _row_tiles(N, B, pref=256):
    S = N // B
    tm = _div(S, pref, SUBLANES)
    return tm, S // tm


def _normmod(h, g, scale, shift, B):
    N, D = h.shape
    tm, per_ex = _row_tiles(N, B)

    def body(h_ref, g_ref, sc_ref, sh_ref, u_ref):
        x = h_ref[...]
        rstd = lax.rsqrt(jnp.mean(x * x, axis=-1, keepdims=True) + EPS)
        y = (x * rstd) * g_ref[...]
        u_ref[...] = (y * (1.0 + sc_ref[...]) + sh_ref[...]).astype(u_ref.dtype)

    tok = pl.BlockSpec((tm, D), lambda i: (i, 0))
    vec = pl.BlockSpec((1, D), lambda i: (0, 0))
    ex = pl.BlockSpec((None, 1, D), lambda i: (i // per_ex, 0, 0))
    return pl.pallas_call(
        body, name="normmod_fwd", grid=(N // tm,), in_specs=[tok, vec, ex, ex], out_specs=tok,
        out_shape=jax.ShapeDtypeStruct((N, D), BF16), compiler_params=_params("parallel"),
    )(h, g, scale, shift)


def _normmod_bwd(du, h, g, scale, dh_in, B):
    N, D = h.shape
    tm, per_ex = _row_tiles(N, B)

    def body(du_ref, h_ref, g_ref, sc_ref, dhin_ref, dh_ref, dg_ref, dsc_ref, dsh_ref):
        i = pl.program_id(0)
        x = h_ref[...]
        gv = g_ref[...]
        d_u = du_ref[...].astype(F32)
        rstd = lax.rsqrt(jnp.mean(x * x, axis=-1, keepdims=True) + EPS)
        xn = x * rstd
        dyg = d_u * (1.0 + sc_ref[...])
        dxn = dyg * gv
        dh_ref[...] = dhin_ref[...] + rstd * (dxn - xn * jnp.mean(dxn * xn, axis=-1, keepdims=True))
        dsh_t = jnp.sum(d_u, axis=0, keepdims=True)
        dsc_t = jnp.sum(d_u * (xn * gv), axis=0, keepdims=True)
        dg_t = jnp.sum(dyg * xn, axis=0, keepdims=True)

        @pl.when(i % per_ex == 0)
        def _():
            dsc_ref[...] = dsc_t
            dsh_ref[...] = dsh_t

        @pl.when(i % per_ex != 0)
        def _():
            dsc_ref[...] += dsc_t
            dsh_ref[...] += dsh_t

        @pl.when(i == 0)
        def _():
            dg_ref[...] = dg_t

        @pl.when(i != 0)
        def _():
            dg_ref[...] += dg_t

    tok = pl.BlockSpec((tm, D), lambda i: (i, 0))
    vec = pl.BlockSpec((1, D), lambda i: (0, 0))
    ex = pl.BlockSpec((None, 1, D), lambda i: (i // per_ex, 0, 0))
    return pl.pallas_call(
        body, name="normmod_bwd", grid=(N // tm,), in_specs=[tok, tok, vec, ex, tok], out_specs=[tok, vec, ex, ex],
        out_shape=[jax.ShapeDtypeStruct((N, D), F32), jax.ShapeDtypeStruct((1, D), F32),
                   jax.ShapeDtypeStruct((B, 1, D), F32), jax.ShapeDtypeStruct((B, 1, D), F32)],
        compiler_params=_params("arbitrary"),
    )(du, h, g, scale, dh_in)


def _residual_bwd(dh, gate, y, B):
    N, D = dh.shape
    tm, per_ex = _row_tiles(N, B)

    def body(dh_ref, gt_ref, y_ref, dy_ref, dgt_ref):
        i = pl.program_id(0)
        d = dh_ref[...]
        dy_ref[...] = (gt_ref[...] * d).astype(dy_ref.dtype)
        t = jnp.sum(d * y_ref[...], axis=0, keepdims=True)

        @pl.when(i % per_ex == 0)
        def _():
            dgt_ref[...] = t

        @pl.when(i % per_ex != 0)
        def _():
            dgt_ref[...] += t

    tok = pl.BlockSpec((tm, D), lambda i: (i, 0))
    ex = pl.BlockSpec((None, 1, D), lambda i: (i // per_ex, 0, 0))
    return pl.pallas_call(
        body, name="residual_bwd", grid=(N // tm,), in_specs=[tok, ex, tok], out_specs=[tok, ex],
        out_shape=[jax.ShapeDtypeStruct((N, D), BF16), jax.ShapeDtypeStruct((B, 1, D), F32)],
        compiler_params=_params("arbitrary"),
    )(dh, gate, y)


def _glu_residual(zz, h, gate, B):
    N, D2 = zz.shape
    D = D2 // 2
    tm, per_ex = _row_tiles(N, B)

    def body(v_ref, g_ref, h_ref, gt_ref, y_ref, o_ref):
        y = v_ref[...] * jax.nn.sigmoid(g_ref[...])
        y_ref[...] = y.astype(y_ref.dtype)
        o_ref[...] = h_ref[...] + gt_ref[...] * y

    tok = pl.BlockSpec((tm, D), lambda i: (i, 0))
    return pl.pallas_call(
        body, name="glu_fwd", grid=(N // tm,),
        in_specs=[tok, pl.BlockSpec((tm, D), lambda i: (i, 1)), tok,
                  pl.BlockSpec((None, 1, D), lambda i: (i // per_ex, 0, 0))],
        out_specs=[tok, tok], out_shape=[jax.ShapeDtypeStruct((N, D), BF16), jax.ShapeDtypeStruct((N, D), F32)],
        compiler_params=_params("parallel"),
    )(zz, zz, h, gate)


def _glu_bwd(dy, zz):
    N, D2 = zz.shape
    D = D2 // 2
    tm = _div(N, 256, SUBLANES)

    def body(dy_ref, v_ref, g_ref, o_ref):
        d = dy_ref[...].astype(F32)
        s = jax.nn.sigmoid(g_ref[...])
        o_ref[...] = jnp.concatenate([d * s, d * v_ref[...] * s * (1.0 - s)], axis=1).astype(o_ref.dtype)

    return pl.pallas_call(
        body, name="glu_bwd", grid=(N // tm,),
        in_specs=[pl.BlockSpec((tm, D), lambda i: (i, 0)), pl.BlockSpec((tm, D), lambda i: (i, 0)),
                  pl.BlockSpec((tm, D), lambda i: (i, 1))],
        out_specs=pl.BlockSpec((tm, D2), lambda i: (i, 0)), out_shape=jax.ShapeDtypeStruct((N, D2), BF16),
        compiler_params=_params("parallel"),
    )(dy, zz, zz)


def _loss_head(h, g, target):
    N, D = h.shape
    tm = _div(N, 256, SUBLANES)

    def body(h_ref, g_ref, t_ref, loss_ref, dh_ref, dg_ref):
        i = pl.program_id(0)
        x = h_ref[...]
        gv = g_ref[...]
        rstd = lax.rsqrt(jnp.mean(x * x, axis=-1, keepdims=True) + EPS)
        xn = x * rstd
        err = xn * gv - t_ref[...]
        part = 0.5 * jnp.sum(jnp.sum(err * err, axis=-1, keepdims=True) / D, axis=0, keepdims=True)
        dy = err / D
        dxn = dy * gv
        dh_ref[...] = rstd * (dxn - xn * jnp.mean(dxn * xn, axis=-1, keepdims=True))
        dg_t = jnp.sum(dy * xn, axis=0, keepdims=True)
        part = jnp.broadcast_to(part, loss_ref.shape)

        @pl.when(i == 0)
        def _():
            loss_ref[...] = part
            dg_ref[...] = dg_t

        @pl.when(i != 0)
        def _():
            loss_ref[...] += part
            dg_ref[...] += dg_t

    tok = pl.BlockSpec((tm, D), lambda i: (i, 0))
    vec = pl.BlockSpec((1, D), lambda i: (0, 0))
    return pl.pallas_call(
        body, name="loss_head", grid=(N // tm,), in_specs=[tok, vec, tok],
        out_specs=[pl.BlockSpec((SUBLANES, LANES), lambda i: (0, 0)), tok, vec],
        out_shape=[jax.ShapeDtypeStruct((SUBLANES, LANES), F32), jax.ShapeDtypeStruct((N, D), F32),
                   jax.ShapeDtypeStruct((1, D), F32)],
        compiler_params=_params("arbitrary"),
    )(h, g, target)


def _swap_halves(x):
    half = x.shape[-1] // 2
    return jnp.concatenate([x[:, half:], x[:, :half]], axis=1)


def _gelu(y):
    return jax.nn.gelu(y)


def _gelu_grad(y):
    c0 = math.sqrt(2.0 / math.pi)
    inner = c0 * (y + 0.044715 * y * y * y)
    t = jnp.tanh(inner)
    return 0.5 * (1.0 + t) + 0.5 * y * (1.0 - t * t) * c0 * (1.0 + 3.0 * 0.044715 * y * y)


def _s5_discretize(lam_re, lam_im, log_dt, b_re, b_im, c_re, c_im):
    G = lam_re.shape[0]
    nch = G // CHUNK_GROUPS
    dt = jnp.exp(log_dt)[:, None]
    er = jnp.exp(lam_re * dt)
    a_re = er * jnp.cos(lam_im * dt)
    a_im = er * jnp.sin(lam_im * dt)
    den = lam_re * lam_re + lam_im * lam_im
    n_re, n_im = a_re - 1.0, a_im
    f_re = (n_re * lam_re + n_im * lam_im) / den
    f_im = (n_im * lam_re - n_re * lam_im) / den
    bb_re = f_re[..., None] * b_re - f_im[..., None] * b_im
    bb_im = f_re[..., None] * b_im + f_im[..., None] * b_re
    eye = jnp.eye(CHUNK_GROUPS, dtype=F32)

    def pack_b(bb):
        bb = bb.reshape(nch, CHUNK_GROUPS, SSM_STATE, SSM_GROUP)
        return jnp.einsum("jgpc,gh->jgchp", bb, eye).reshape(nch, LANES, CHUNK_STATE)

    def pack_c(cc):
        cc = cc.reshape(nch, CHUNK_GROUPS, SSM_GROUP, SSM_STATE)
        return jnp.einsum("jgcp,gh->jgphc", cc, eye).reshape(nch, CHUNK_STATE, LANES)

    bd = jnp.concatenate([pack_b(bb_re), pack_b(bb_im)], axis=2)
    cd = jnp.concatenate([pack_c(c_re), pack_c(-c_im)], axis=1)
    return bd, cd, a_re, a_im


S5_TILE = 256
S5_SEG = S5_TILE // SUBLANES
S5_UNROLL = 4


def _s5_scan_coefs(lam_re, lam_im, log_dt, seg):
    G = lam_re.shape[0]
    nch = G // CHUNK_GROUPS
    dt = jnp.exp(log_dt)[:, None]

    def power(k):
        er = jnp.exp(k * lam_re * dt)
        re = (er * jnp.cos(k * lam_im * dt)).reshape(nch, 1, CHUNK_STATE)
        im = (er * jnp.sin(k * lam_im * dt)).reshape(nch, 1, CHUNK_STATE)
        return jnp.concatenate([re, re], axis=2), jnp.concatenate([-im, im], axis=2)

    rows = [power(i + 1) for i in range(seg)]
    pw = jnp.stack([jnp.concatenate([r for r, _ in rows], axis=1), jnp.concatenate([i for _, i in rows], axis=1)], axis=1)
    row = jnp.arange(SUBLANES, dtype=jnp.int32)[None, :, None]

    def table(reverse):
        tabs = []
        for s in (1, 2, 4):
            re, im = power(s * seg)
            mask = (row < SUBLANES - s) if reverse else (row >= s)
            tabs += [jnp.where(mask, re, 0.0), jnp.where(mask, -im if reverse else im, 0.0)]
        return jnp.stack([jnp.broadcast_to(t, (nch, SUBLANES, 2 * CHUNK_STATE)) for t in tabs], axis=1)

    return pw, table(False), table(True)


def _segment_permutation():
    r = jnp.arange(S5_TILE)
    src = (r % SUBLANES) * S5_SEG + r // SUBLANES
    p = (src[:, None] == jnp.arange(S5_TILE)[None, :]).astype(BF16)
    return p, p.T


def _permute_rows(p, x):
    if x.dtype == BF16:
        return jnp.dot(p, x, preferred_element_type=F32)
    hi = x.astype(BF16)
    lo = (x - hi.astype(F32)).astype(BF16)
    return jnp.dot(p, hi, preferred_element_type=F32) + jnp.dot(p, lo, preferred_element_type=F32)


def _seg_scan(x_ref, pw_ref, seg_ref, carry_ref, c_ref, seg, reverse):
    W = x_ref.shape[-1]
    tm = x_ref.shape[0]
    sgn = -1.0 if reverse else 1.0
    ar = jnp.broadcast_to(pw_ref[0, 0:1, :], (SUBLANES, W))
    ai = sgn * jnp.broadcast_to(pw_ref[1, 0:1, :], (SUBLANES, W))

    def rows(i):
        return pl.ds(pl.multiple_of(i * SUBLANES, SUBLANES), SUBLANES)

    def step(t, prev):
        i = (seg - 2 - t) if reverse else (t + 1)
        x = x_ref[rows(i), :] + ar * prev + ai * _swap_halves(prev)
        x_ref[rows(i), :] = x
        return x

    start = (seg - 1) * SUBLANES if reverse else 0
    edge = lax.fori_loop(0, seg - 1, step, x_ref[start:start + SUBLANES, :], unroll=S5_UNROLL)
    row = lax.broadcasted_iota(jnp.int32, (SUBLANES, W), 0)
    if reverse:
        f = jnp.where(row == SUBLANES - 1, carry_ref[...], pltpu.roll(edge, SUBLANES - 1, 0))
    else:
        f = jnp.where(row == 0, carry_ref[...], pltpu.roll(edge, 1, 0))
    for si, s in enumerate((1, 2, 4)):
        fs = pltpu.roll(f, (SUBLANES - s) if reverse else s, 0)
        f = f + seg_ref[2 * si] * fs + seg_ref[2 * si + 1] * _swap_halves(fs)
    c_ref[...] = f
    fsw = _swap_halves(f)

    def fix(i, _):
        k = (seg - 1 - i) if reverse else i
        x_ref[rows(i), :] = x_ref[rows(i), :] + pw_ref[0, pl.ds(k, 1), :] * f + (sgn * pw_ref[1, pl.ds(k, 1), :]) * fsw
        return 0

    lax.fori_loop(0, seg, fix, 0, unroll=S5_UNROLL)
    leaving = x_ref[0:1, :] if reverse else x_ref[tm - 1:tm, :]
    carry_ref[...] = jnp.broadcast_to(leaving, carry_ref.shape)


def _s5_fwd(u, bd, cd, pw, seg_f, d_skip, B, carry=None):
    N, D = u.shape
    S = N // B
    nch = D // LANES
    W = 2 * CHUNK_STATE
    tm, seg = S5_TILE, S5_SEG
    nt = S // tm

    def body(u_ref, p_ref, pt_ref, bd_ref, cd_ref, pw_ref, seg_ref, d_ref, z_ref, cin_ref, x_s, carry, c_s):
        t = pl.program_id(2)

        @pl.when(t == 0)
        def _():
            carry[...] = jnp.zeros_like(carry)

        cin_ref[...] = carry[...]
        uf = _permute_rows(p_ref[...], u_ref[...])
        x_s[...] = jnp.dot(uf.astype(BF16), bd_ref[...], preferred_element_type=F32)
        _seg_scan(x_s, pw_ref, seg_ref, carry, c_s, seg, False)
        y = jnp.dot(x_s[...].astype(BF16), cd_ref[...], preferred_element_type=F32) + d_ref[...] * uf
        z_ref[...] = _permute_rows(pt_ref[...], _gelu(y).astype(BF16)).astype(z_ref.dtype)

    perm = pl.BlockSpec((tm, tm), lambda j, b, t: (0, 0))
    (z, carries), moved = _carried_call(
        body, name="s5_fwd", grid=(nch, B, nt),
        in_specs=[pl.BlockSpec((tm, LANES), lambda j, b, t: (b * nt + t, j)), perm, perm,
                  pl.BlockSpec((None, LANES, W), lambda j, b, t: (j, 0, 0)),
                  pl.BlockSpec((None, W, LANES), lambda j, b, t: (j, 0, 0)),
                  pl.BlockSpec((None, 2, seg, W), lambda j, b, t: (j, 0, 0, 0)),
                  pl.BlockSpec((None, 6, SUBLANES, W), lambda j, b, t: (j, 0, 0, 0)),
                  pl.BlockSpec((1, LANES), lambda j, b, t: (0, j))],
        out_specs=[pl.BlockSpec((tm, LANES), lambda j, b, t: (b * nt + t, j)),
                   pl.BlockSpec((None, None, SUBLANES, W), lambda j, b, t: (j, b * nt + t, 0, 0))],
        out_shape=[jax.ShapeDtypeStruct((N, D), BF16), jax.ShapeDtypeStruct((nch, B * nt, SUBLANES, W), F32)],
        scratch_shapes=[pltpu.VMEM((tm, W), F32), pltpu.VMEM((SUBLANES, W), F32), pltpu.VMEM((SUBLANES, W), F32)],
        operands=(u, *_segment_permutation(), bd, cd, pw, seg_f, d_skip), sem=("parallel", "arbitrary", "arbitrary"),
        carry=carry)
    return z, carries, moved


def _s5_bwd(u, dz, bd, cd, pw, seg_f, seg_b, d_skip, carries, B, carry=None):
    N, D = u.shape
    S = N // B
    nch = D // LANES
    W = 2 * CHUNK_STATE
    tm, seg = S5_TILE, S5_SEG
    nt = S // tm
    tn_dims = (((0,), (0,)), ((), ()))
    nt_dims = (((1,), (1,)), ((), ()))

    def body(u_ref, dz_ref, p_ref, pt_ref, bd_ref, cd_ref, pw_ref, sf_ref, sb_ref, d_ref, cin_ref,
             du_ref, dbd_ref, dcd_ref, da_ref, dd_ref, x_s, l_s, carry, lcarry, c_s, lc_s):
        b = pl.program_id(1)
        t = pl.program_id(2)

        @pl.when((b == 0) & (t == 0))
        def _():
            dbd_ref[...] = jnp.zeros_like(dbd_ref)
            dcd_ref[...] = jnp.zeros_like(dcd_ref)
            da_ref[...] = jnp.zeros_like(da_ref)
            dd_ref[...] = jnp.zeros_like(dd_ref)

        @pl.when(t == 0)
        def _():
            lcarry[...] = jnp.zeros_like(lcarry)

        uf = _permute_rows(p_ref[...], u_ref[...])
        dz = _permute_rows(p_ref[...], dz_ref[...])
        uv = uf.astype(BF16)
        carry[...] = cin_ref[...]
        x_s[...] = jnp.dot(uv, bd_ref[...], preferred_element_type=F32)
        _seg_scan(x_s, pw_ref, sf_ref, carry, c_s, seg, False)
        xb = x_s[...].astype(BF16)
        y = jnp.dot(xb, cd_ref[...], preferred_element_type=F32) + d_ref[...] * uf
        dy = dz * _gelu_grad(y)
        dd_ref[...] += jnp.sum(dy * uf, axis=0, keepdims=True)
        dyb = dy.astype(BF16)
        dcd_ref[...] += lax.dot_general(xb, dyb, tn_dims, preferred_element_type=F32)
        l_s[...] = lax.dot_general(dyb, cd_ref[...], nt_dims, preferred_element_type=F32)
        _seg_scan(l_s, pw_ref, sb_ref, lcarry, lc_s, seg, True)
        lb = l_s[...].astype(BF16)
        dbd_ref[...] += lax.dot_general(uv, lb, tn_dims, preferred_element_type=F32)
        du = lax.dot_general(lb, bd_ref[...], nt_dims, preferred_element_type=F32) + d_ref[...] * dy
        du_ref[...] = _permute_rows(pt_ref[...], du)
        lam_rest, x_prev = l_s[SUBLANES:, :], x_s[:tm - SUBLANES, :]
        lam_0, c_in = l_s[:SUBLANES, :], c_s[...]
        da_ref[0:1, :] += (jnp.sum(lam_rest * x_prev, axis=0, keepdims=True) + jnp.sum(lam_0 * c_in, axis=0, keepdims=True))
        da_ref[1:2, :] += (jnp.sum(lam_rest * _swap_halves(x_prev), axis=0, keepdims=True)
                           + jnp.sum(lam_0 * _swap_halves(c_in), axis=0, keepdims=True))

    tile = lambda j, b, t: (b * nt + (nt - 1 - t), j)
    chunk3 = lambda j, b, t: (j, 0, 0)
    chunk4 = lambda j, b, t: (j, 0, 0, 0)
    perm = pl.BlockSpec((tm, tm), lambda j, b, t: (0, 0))
    outs, moved = _carried_call(
        body, name="s5_bwd", grid=(nch, B, nt),
        in_specs=[pl.BlockSpec((tm, LANES), tile), pl.BlockSpec((tm, LANES), tile), perm, perm,
                  pl.BlockSpec((None, LANES, W), chunk3), pl.BlockSpec((None, W, LANES), chunk3),
                  pl.BlockSpec((None, 2, seg, W), chunk4), pl.BlockSpec((None, 6, SUBLANES, W), chunk4),
                  pl.BlockSpec((None, 6, SUBLANES, W), chunk4), pl.BlockSpec((1, LANES), lambda j, b, t: (0, j)),
                  pl.BlockSpec((None, None, SUBLANES, W), lambda j, b, t: (j, b * nt + (nt - 1 - t), 0, 0))],
        out_specs=[pl.BlockSpec((tm, LANES), tile), pl.BlockSpec((None, LANES, W), chunk3),
                   pl.BlockSpec((None, W, LANES), chunk3), pl.BlockSpec((None, 2, W), chunk3),
                   pl.BlockSpec((1, LANES), lambda j, b, t: (0, j))],
        out_shape=[jax.ShapeDtypeStruct((N, D), F32), jax.ShapeDtypeStruct((nch, LANES, W), F32),
                   jax.ShapeDtypeStruct((nch, W, LANES), F32), jax.ShapeDtypeStruct((nch, 2, W), F32),
                   jax.ShapeDtypeStruct((1, D), F32)],
        scratch_shapes=[pltpu.VMEM((tm, W), F32), pltpu.VMEM((tm, W), F32)] + [pltpu.VMEM((SUBLANES, W), F32)] * 4,
        operands=(u, dz, *_segment_permutation(), bd, cd, pw, seg_f, seg_b, d_skip, carries),
        sem=("parallel", "arbitrary", "arbitrary"), carry=carry)
    return (*outs, moved)


ATTN_HEADS = LANES // HEAD_DIM
ATTN_FWD_UNROLL = 4
ATTN_BWD_UNROLL = 2


def _attn_mask(n):
    qi = lax.broadcasted_iota(jnp.int32, (ATTN_BLOCK, 2 * ATTN_BLOCK), 0)
    kj = lax.broadcasted_iota(jnp.int32, (ATTN_BLOCK, 2 * ATTN_BLOCK), 1)
    prev_ok = (kj < ATTN_BLOCK) & (kj >= qi) & (n > 0)
    return prev_ok | ((kj >= ATTN_BLOCK) & (kj - ATTN_BLOCK <= qi))


def _head_lanes(h):
    lane = lax.broadcasted_iota(jnp.int32, (ATTN_BLOCK, LANES), 1)
    return (lane >= h * HEAD_DIM) & (lane < (h + 1) * HEAD_DIM)


def _per_head(cols):
    out = jnp.broadcast_to(cols[-1], (ATTN_BLOCK, LANES))
    for h in range(len(cols) - 2, -1, -1):
        out = jnp.where(_head_lanes(h), jnp.broadcast_to(cols[h], (ATTN_BLOCK, LANES)), out)
    return out


def _only_head(x, h):
    return jnp.where(_head_lanes(h), x, 0.0).astype(BF16)


def _block_rows(tb, dil, nb):
    r = tb // nb
    n = tb % nb
    start = r + dil * ATTN_BLOCK * n
    startp = jnp.where(n > 0, start - dil * ATTN_BLOCK, start)
    return n, pl.ds(start, ATTN_BLOCK, stride=dil), pl.ds(startp, ATTN_BLOCK, stride=dil)


def _attn_fwd(q, k, v, B, carry=None):
    _, S, D3 = q.shape
    D = D3 // 3
    HP = D // LANES
    scale = HEAD_DIM ** -0.5
    n_blocks = S // ATTN_BLOCK
    nbr = len(DILATIONS)
    nt_dims = (((1,), (1,)), ((), ()))

    def branch(dil, q_ref, k_ref, v_ref, acc, m_s, l_s):
        nb = (S // dil) // ATTN_BLOCK

        def blk(tb, _):
            n, rows, rowsp = _block_rows(tb, dil, nb)
            qb = q_ref[rows, :] * scale
            kk = jnp.concatenate([k_ref[rowsp, :], k_ref[rows, :]], axis=0).astype(BF16)
            vv = jnp.concatenate([v_ref[rowsp, :], v_ref[rows, :]], axis=0).astype(BF16)
            ok = _attn_mask(n)
            ms, ls, accs = [], [], []
            for h in range(ATTN_HEADS):
                s = lax.dot_general(_only_head(qb, h), kk, nt_dims, preferred_element_type=F32)
                s = jnp.where(ok, s, NEG)
                mh = jnp.max(s, axis=-1, keepdims=True)
                p = jnp.exp(s - mh)
                ms.append(mh)
                ls.append(jnp.sum(p, axis=-1, keepdims=True))
                accs.append(jnp.dot(p.astype(BF16), vv, preferred_element_type=F32))
            m_s[rows, :] = _per_head(ms)
            l_s[rows, :] = _per_head(ls)
            acc[rows, :] = _per_head(accs)
            return 0

        lax.fori_loop(0, n_blocks, blk, 0, unroll=ATTN_FWD_UNROLL)

    def body(q_ref, k_ref, v_ref, o_ref, lse_ref, *scratch):
        accs, m_ss, l_ss = scratch[:nbr], scratch[nbr:2 * nbr], scratch[2 * nbr:]
        g = pl.program_id(2)
        for gi, dil in enumerate(DILATIONS):
            pl.when(g == gi)(functools.partial(branch, dil, q_ref, k_ref, v_ref, accs[gi], m_ss[gi], l_ss[gi]))

        @pl.when(g == nbr - 1)
        def _():
            def fin(i, _):
                rows = pl.ds(pl.multiple_of(i * ATTN_BLOCK, ATTN_BLOCK), ATTN_BLOCK)
                ms = [m[rows, :] for m in m_ss]
                m_all = functools.reduce(jnp.maximum, ms)
                ws = [jnp.exp(m - m_all) for m in ms]
                den = sum(w * l[rows, :] for w, l in zip(ws, l_ss))
                o_ref[rows, :] = sum(w * a[rows, :] for w, a in zip(ws, accs)) / den
                lse_ref[rows, :] = m_all + jnp.log(den)
                return 0

            lax.fori_loop(0, n_blocks, fin, 0)

    br = pl.BlockSpec((None, S, LANES), lambda b, hp, g: (b, 0, g * HP + hp))
    hd = pl.BlockSpec((None, S, LANES), lambda b, hp, g: (b, 0, hp))
    (o, lse), moved = _carried_call(
        body, name="attn_fwd", grid=(B, HP, nbr), in_specs=[br, br, br], out_specs=[hd, hd],
        out_shape=[jax.ShapeDtypeStruct((B, S, D), F32), jax.ShapeDtypeStruct((B, S, D), F32)],
        scratch_shapes=[pltpu.VMEM((S, LANES), F32)] * (3 * nbr),
        operands=(q, k, v), sem=("parallel", "parallel", "arbitrary"), carry=carry)
    return o, lse, moved


def _attn_bwd(q, k, v, o, lse, do, dk_prev, dv_prev, B, last, carry=None):
    _, S, D3 = q.shape
    D = D3 // 3
    HP = D // LANES
    scale = HEAD_DIM ** -0.5
    n_blocks = S // ATTN_BLOCK
    has_prev = dk_prev is not None
    nt_dims = (((1,), (1,)), ((), ()))
    tn_dims = (((0,), (0,)), ((), ()))

    def branch(dil, q_ref, k_ref, v_ref, lse_ref, do_ref, dq_s, dk_c, dv_c, delta, dk_p, dv_p):
        nb = (S // dil) // ATTN_BLOCK

        def blk(tb, _):
            n, rows, rowsp = _block_rows(tb, dil, nb)
            qb = q_ref[rows, :] * scale
            dob, lb, db = do_ref[rows, :], lse_ref[rows, :], delta[rows, :]
            kk = jnp.concatenate([k_ref[rowsp, :], k_ref[rows, :]], axis=0).astype(BF16)
            vv = jnp.concatenate([v_ref[rowsp, :], v_ref[rows, :]], axis=0).astype(BF16)
            ok = _attn_mask(n)
            dqs = []
            dkk = dvv = None
            for h in range(ATTN_HEADS):
                qh, doh = _only_head(qb, h), _only_head(dob, h)
                lh = lb[:, h * HEAD_DIM:h * HEAD_DIM + 1]
                dlt = db[:, h * HEAD_DIM:h * HEAD_DIM + 1]
                s = lax.dot_general(qh, kk, nt_dims, preferred_element_type=F32)
                p = jnp.where(ok, jnp.exp(s - lh), 0.0)
                dp = lax.dot_general(doh, vv, nt_dims, preferred_element_type=F32)
                ds = (p * (dp - dlt)).astype(BF16)
                dqs.append(jnp.dot(ds, kk, preferred_element_type=F32))
                dk_h = lax.dot_general(ds, qh, tn_dims, preferred_element_type=F32)
                dv_h = lax.dot_general(p.astype(BF16), doh, tn_dims, preferred_element_type=F32)
                dkk = dk_h if dkk is None else dkk + dk_h
                dvv = dv_h if dvv is None else dvv + dv_h
            dq_s[rows, :] = _per_head(dqs) * scale
            dk_p[rowsp, :] = dkk[:ATTN_BLOCK]
            dv_p[rowsp, :] = dvv[:ATTN_BLOCK]
            dk_c[rows, :] = dkk[ATTN_BLOCK:]
            dv_c[rows, :] = dvv[ATTN_BLOCK:]
            return 0

        lax.fori_loop(0, n_blocks, blk, 0, unroll=ATTN_BWD_UNROLL)

    def body(*refs):
        q_ref, k_ref, v_ref, o_ref, lse_ref, do_ref = refs[:6]
        n_in = 8 if has_prev else 6
        dq_ref, dk_ref, dv_ref, delta, dk_p, dv_p, dq_s, dk_c, dv_c = refs[n_in:n_in + 9]
        g = pl.program_id(2)

        @pl.when(g == 0)
        def _():
            def dl(i, _):
                rows = pl.ds(pl.multiple_of(i * ATTN_BLOCK, ATTN_BLOCK), ATTN_BLOCK)
                prod = do_ref[rows, :] * o_ref[rows, :]
                delta[rows, :] = _per_head([jnp.sum(jnp.where(_head_lanes(h), prod, 0.0), axis=-1, keepdims=True)
                                            for h in range(ATTN_HEADS)])
                return 0

            lax.fori_loop(0, n_blocks, dl, 0)

        dk_p[...] = jnp.zeros_like(dk_p)
        dv_p[...] = jnp.zeros_like(dv_p)
        for gi, dil in enumerate(DILATIONS):
            pl.when(g == gi)(functools.partial(branch, dil, q_ref, k_ref, v_ref, lse_ref, do_ref, dq_s, dk_c, dv_c,
                                               delta, dk_p, dv_p))

        def fin(i, _):
            rows = pl.ds(pl.multiple_of(i * ATTN_BLOCK, ATTN_BLOCK), ATTN_BLOCK)
            dk_t = dk_c[rows, :] + dk_p[rows, :]
            dv_t = dv_c[rows, :] + dv_p[rows, :]
            if has_prev:
                dk_t = dk_t + refs[6][rows, :].astype(F32)
                dv_t = dv_t + refs[7][rows, :].astype(F32)
            dq_ref[rows, :] = dq_s[rows, :].astype(dq_ref.dtype)
            dk_ref[rows, :] = dk_t.astype(dk_ref.dtype)
            dv_ref[rows, :] = dv_t.astype(dv_ref.dtype)
            return 0

        lax.fori_loop(0, n_blocks, fin, 0)

    br = pl.BlockSpec((None, S, LANES), lambda b, hp, g: (b, 0, g * HP + hp))
    hd = pl.BlockSpec((None, S, LANES), lambda b, hp, g: (b, 0, hp))
    ins = [q, k, v, o, lse, do] + ([dk_prev, dv_prev] if has_prev else [])
    kv_dtype = BF16 if last else F32
    (dq, dk, dv), moved = _carried_call(
        body, name="attn_bwd", grid=(B, HP, len(DILATIONS)),
        in_specs=[br, br, br, hd, hd, hd] + ([br, br] if has_prev else []), out_specs=[br, br, br],
        out_shape=[jax.ShapeDtypeStruct(q.shape, BF16), jax.ShapeDtypeStruct(q.shape, kv_dtype),
                   jax.ShapeDtypeStruct(q.shape, kv_dtype)],
        scratch_shapes=[pltpu.VMEM((S, LANES), F32)] * 6,
        operands=ins, sem=("parallel", "parallel", "arbitrary"), carry=carry)
    return dq, dk, dv, moved


def _adamw(w, grads, m, v):
    R, C = w.shape
    tr = _div(R, 256, SUBLANES)
    ng = len(grads)
    c1 = 1.0 - ADAM_B1 ** ADAM_STEP
    c2 = 1.0 - ADAM_B2 ** ADAM_STEP

    def body(*refs):
        w_ref, m_ref, v_ref = refs[0], refs[1 + ng], refs[2 + ng]
        d_ref, mo_ref, vo_ref = refs[3 + ng:6 + ng]
        g = refs[1][...]
        if ng == 2:
            g = g + refs[2][...]
            refs[6 + ng][...] = g
        mn = ADAM_B1 * m_ref[...] + (1.0 - ADAM_B1) * g
        vn = ADAM_B2 * v_ref[...] + (1.0 - ADAM_B2) * (g * g)
        d_ref[...] = -ADAM_LR * ((mn / c1) / (jnp.sqrt(vn / c2) + ADAM_EPS) + ADAM_WD * w_ref[...])
        mo_ref[...] = mn
        vo_ref[...] = vn

    blk = pl.BlockSpec((tr, C), lambda i: (i, 0))
    n_out = 3 + (ng == 2)
    outs = pl.pallas_call(
        body, name="adamw", grid=(R // tr,), in_specs=[blk] * (3 + ng), out_specs=[blk] * n_out,
        out_shape=[jax.ShapeDtypeStruct((R, C), F32)] * n_out, compiler_params=_params("parallel"),
    )(w, *grads, m, v)
    return (outs[3] if ng == 2 else grads[0],) + tuple(outs[:3])


def _sum_shards(recv):
    n, R, C = recv.shape
    tr = _div(R, 256, SUBLANES if recv.dtype == F32 else 2 * SUBLANES)

    def body(r_ref, o_ref):
        s = r_ref[0].astype(F32)
        for i in range(1, n):
            s = s + r_ref[i].astype(F32)
        o_ref[...] = s

    return pl.pallas_call(
        body, name="sum_shards", grid=(R // tr,), in_specs=[pl.BlockSpec((n, tr, C), lambda i: (0, i, 0))],
        out_specs=pl.BlockSpec((tr, C), lambda i: (i, 0)), out_shape=jax.ShapeDtypeStruct((R, C), F32),
        compiler_params=_params("parallel"),
    )(recv)


N_DEV = 8
N_CHIPS = 4


def _all_gather_small(x, carry=None):
    m_per, n = x.shape

    def body(x_ref, out_ref, send_sems, recv_sems, local_sem):
        cx, cy, cc = _coords()
        me, sibling = (cx, cy, cc), (cx, cy, 1 - cc)
        chips = [(1 - cx, cy), (cx, 1 - cy), (1 - cx, 1 - cy)]

        def rows(px, py, pc):
            return out_ref.at[pl.ds((4 * px + 2 * py + pc) * m_per, m_per), :]

        def copy(k, block, to, src=None):
            return pltpu.make_async_remote_copy(
                src_ref=rows(*block) if src is None else src, dst_ref=rows(*block), send_sem=send_sems.at[k],
                recv_sem=recv_sems.at[k], device_id=to, device_id_type=MESH)

        mine = pltpu.make_async_copy(x_ref, rows(*me), local_sem)
        mine.start()
        first = [copy(0, me, sibling, src=x_ref)]
        first += [copy(1 + j, me, (*chip, cc), src=x_ref) for j, chip in enumerate(chips)]
        for cp in first:
            cp.start()
        passed = [copy(4 + j, (*chip, cc), sibling) for j, chip in enumerate(chips)]
        for j, chip in enumerate(chips):
            copy(1 + j, (*chip, cc), me).wait_recv()
            passed[j].start()
        copy(0, sibling, me).wait_recv()
        for j, chip in enumerate(chips):
            copy(4 + j, (*chip, 1 - cc), me).wait_recv()
        for cp in first + passed:
            cp.wait_send()
        mine.wait()

    (out,), moved = _carried_call(
        body, name="all_gather_small", grid=(1,), out_shape=[jax.ShapeDtypeStruct((N_DEV * m_per, n), x.dtype)],
        in_specs=[pl.BlockSpec(memory_space=pltpu.VMEM)], out_specs=[pl.BlockSpec(memory_space=pltpu.VMEM)],
        scratch_shapes=[pltpu.SemaphoreType.DMA((7,)), pltpu.SemaphoreType.DMA((7,)), pltpu.SemaphoreType.DMA],
        operands=(x,), sem=("arbitrary",), carry=carry)
    return out, moved


def _layer_moves(kind, arrays_from, arrays_to, pieces, layer_major=()):
    used = sorted({w for w, _ in pieces})
    pos = {w: i for i, w in enumerate(used)}
    gather = kind == "gather"

    def half(ref, c):
        rows = ref.shape[0] // 2
        return ref.at[pl.ds(c * rows, rows), :]

    def slot(d, w, chip, l):
        return d.at[l, chip] if w in layer_major else d.at[chip, l]

    def plan(src_refs, dst_refs, me):
        cx, cy, cc = me
        mine = 2 * cx + cy
        remote, local = [], []
        for w, l in pieces:
            s, d = src_refs[pos[w]], dst_refs[pos[w]]
            for px, py in _other_chips(cx, cy):
                if gather:
                    remote.append((half(s.at[l], cc), half(slot(d, w, mine, l), cc), (px, py, cc)))
                else:
                    remote.append((s.at[2 * px + py, l], d.at[mine, l], (px, py, cc)))
            local.append((s.at[l], slot(d, w, mine, l)) if gather else (s.at[mine, l], d.at[mine, l]))
        return remote, local

    def onward(src_refs, dst_refs, me):
        cx, cy, cc = me
        moves = []
        for w, l in pieces:
            d = dst_refs[pos[w]]
            for px, py in _other_chips(cx, cy):
                landed = half(slot(d, w, 2 * px + py, l), cc)
                moves.append((landed, landed, (cx, cy, 1 - cc)))
        return moves

    n = 3 * len(pieces)
    carry = _Carry([arrays_from[w] for w in used], [arrays_to[w] for w in used], plan, n, len(pieces),
                   onward if gather else None, n if gather else 0)
    return carry, used


def _swap_with_sibling(sums):
    def plan(src_refs, dst_refs, me):
        cx, cy, cc = me
        return [(s, d, (cx, cy, 1 - cc)) for s, d in zip(src_refs, dst_refs)], []

    return _Carry(sums, [lax.empty(s.shape, s.dtype) for s in sums], plan, len(sums), 0)


def _pack(arrs, width):
    parts, layout, row = [], [], 0
    for a in arrs:
        flat = a.reshape(-1).astype(F32)
        rows = -(-flat.shape[0] // (width * SUBLANES)) * SUBLANES
        parts.append(jnp.pad(flat, (0, rows * width - flat.shape[0])).reshape(rows, width))
        layout.append((row, rows, a.shape))
        row += rows
    pad = -row % (8 * SUBLANES) if row > 8 * SUBLANES else 0
    if pad:
        parts.append(jnp.zeros((pad, width), F32))
    return jnp.concatenate(parts, axis=0), layout, row + pad


def _unpack(buf, layout, idx):
    row, rows, shape = layout[idx]
    size = math.prod(shape)
    return buf[row:row + rows].reshape(-1)[:size].reshape(shape)


def kernel(x, c, ln_g, ada_w, ada_b, ssm_lam_re, ssm_lam_im, ssm_log_dt, ssm_b_re, ssm_b_im, ssm_c_re, ssm_c_im, ssm_d, ssm_w_glu, kv_g, kv_ada_w, kv_ada_b, w_kv, attn_w_q, attn_w_o, mlp_w1, mlp_w2, final_g, loss_target, m_ln_g, m_ada_w, m_ada_b, m_ssm_lam_re, m_ssm_lam_im, m_ssm_log_dt, m_ssm_b_re, m_ssm_b_im, m_ssm_c_re, m_ssm_c_im, m_ssm_d, m_ssm_w_glu, m_kv_g, m_kv_ada_w, m_kv_ada_b, m_w_kv, m_attn_w_q, m_attn_w_o, m_mlp_w1, m_mlp_w2, m_final_g, v_ln_g, v_ada_w, v_ada_b, v_ssm_lam_re, v_ssm_lam_im, v_ssm_log_dt, v_ssm_b_re, v_ssm_b_im, v_ssm_c_re, v_ssm_c_im, v_ssm_d, v_ssm_w_glu, v_kv_g, v_kv_ada_w, v_kv_ada_b, v_w_kv, v_attn_w_q, v_attn_w_o, v_mlp_w1, v_mlp_w2, v_final_g):
    B, S, D = x.shape
    N = B * S
    depth = ln_g.shape[0]
    n_a = ssm_w_glu.shape[0]
    n_b = attn_w_q.shape[0]
    FF = mlp_w1.shape[2] * N_CHIPS
    cx, cy, cc = _coords()
    chip = 2 * cx + cy
    dev = 4 * cx + 2 * cy + cc
    n_ex = N_DEV * B
    ada_cols = ada_w.shape[-1]
    kv_cols = kv_ada_w.shape[-1]

    GLU, KV, Q, O, W1, W2 = range(6)
    shards = [ssm_w_glu.astype(BF16), w_kv.astype(BF16)[None], attn_w_q.astype(BF16), attn_w_o.astype(BF16),
              mlp_w1.astype(BF16), mlp_w2.astype(BF16)]
    row_sharded = (O, W2)
    wg = [lax.empty((s.shape[0], N_CHIPS) + s.shape[1:] if w in row_sharded else (N_CHIPS,) + s.shape, BF16)
          for w, s in enumerate(shards)]

    def whole_rows(w):
        L, _, R, C = wg[w].shape
        return wg[w].reshape(1, L, N_CHIPS * R, C)

    def fetch(pieces):
        return _layer_moves("gather", shards, wg, pieces, layer_major=row_sharded)

    def landed(arrays, used, moved):
        for w, a in zip(used, moved):
            arrays[w] = a

    fetch_with = {l: [(W1, l), (W2, l)] for l in range(depth)}
    fetch_with[0] += [(GLU, l) for l in range(n_a)] + [(Q, 0), (O, 0)]
    fetch_with[n_a - 1] += [(KV, 0)]
    for j in range(1, n_b):
        fetch_with[n_a + j - 1] += [(Q, j), (O, j)]

    c_pack, c_layout, _ = _pack([c], D)
    c_all_buf, _ = _all_gather_small(c_pack)
    c_rows = c_pack.shape[0]
    c_all = jnp.concatenate([_unpack(c_all_buf[d * c_rows:(d + 1) * c_rows], c_layout, 0) for d in range(N_DEV)], axis=0)
    sc_all = jax.nn.silu(c_all).astype(BF16)
    n_mod = depth * 2
    ada_w8 = ada_w.reshape(n_mod, 1, D, ada_cols)
    ada_b_row = ada_b.reshape(1, n_mod * ada_cols)
    mod_local = _mm("ada_fwd", sc_all, ada_w8, mode="nn", M=n_ex, N=n_mod * ada_cols, K=D, b_lay="cs", b_ns=n_mod,
                    epi=_add, extras=[("n", ada_b_row)])
    kv_ada_b_local = lax.dynamic_slice(kv_ada_b.reshape(N_CHIPS, kv_cols), (chip, 0), (1, kv_cols))
    kvmod_local = _mm("ada_fwd", sc_all, _as4(kv_ada_w), mode="nn", M=n_ex, N=kv_cols, K=D, epi=_add,
                      extras=[("n", kv_ada_b_local)])
    mod_pack, mod_layout, mod_rows = _pack([mod_local, kvmod_local, ln_g, ssm_d], D)
    mod_buf, _ = _all_gather_small(mod_pack)

    def from_chip(j, idx):
        d = 2 * j
        return _unpack(mod_buf[d * mod_rows:(d + 1) * mod_rows], mod_layout, idx)

    my_rows = lambda a: lax.dynamic_slice_in_dim(a, dev * B, B, axis=0)
    mods = jnp.concatenate([my_rows(from_chip(j, 0)).reshape(B, n_mod, ada_cols) for j in range(N_CHIPS)], axis=2)
    kvmod = jnp.concatenate([my_rows(from_chip(j, 1)) for j in range(N_CHIPS)], axis=1)
    ln_g_full = jnp.concatenate([from_chip(j, 2) for j in range(N_CHIPS)], axis=2)
    ssm_d_full = jnp.concatenate([from_chip(j, 3) for j in range(N_CHIPS)], axis=1)

    def mod3(l, s):
        mrow = mods[:, l * 2 + s]
        return [mrow[:, i * D:(i + 1) * D].reshape(B, 1, D) for i in range(3)]

    kv_shift, kv_scale = kvmod[:, :D].reshape(B, 1, D), kvmod[:, D:].reshape(B, 1, D)

    s5_tabs = []
    for l in range(n_a):
        prm = (ssm_lam_re[l], ssm_lam_im[l], ssm_log_dt[l], ssm_b_re[l], ssm_b_im[l], ssm_c_re[l], ssm_c_im[l])
        (bd, cd, _, _), disc_vjp = jax.vjp(_s5_discretize, *prm)
        pw, seg_f, seg_b = _s5_scan_coefs(ssm_lam_re[l], ssm_lam_im[l], ssm_log_dt[l], S5_SEG)
        s5_tabs.append((bd.astype(BF16), cd.astype(BF16), pw, seg_f, seg_b, disc_vjp))

    h = x.reshape(N, D)
    saved = []
    k_all = v_all = None
    for l in range(depth):
        sv = {}
        shift, scale, gate = mod3(l, 0)
        sv["h0"], sv["scale0"], sv["gate0"] = h, scale, gate
        u = _normmod(h, ln_g_full[l, 0].reshape(1, D), scale, shift, B)
        sv["u0"] = u
        carry, used = fetch(fetch_with[l])
        if l < n_a:
            bd, cd, pw, seg_f, _, _ = s5_tabs[l]
            z, carries, moved = _s5_fwd(u, bd, cd, pw, seg_f, ssm_d_full[l].reshape(1, D), B, carry)
            landed(wg, used, moved)
            zz = _mm("glu_proj", z, wg[GLU], mode="nn", M=N, N=2 * D, K=D, b_lay="cs", b_l=l, b_ns=N_CHIPS)
            y, h_next = _glu_residual(zz, h, gate, B)
            sv["z"], sv["carries"], sv["zz"] = z, carries, zz
        else:
            j = l - n_a
            q = _mm("q_proj", u, wg[Q], mode="nn", M=N, N=3 * D, K=D, b_lay="cs", b_l=j, b_ns=N_CHIPS)
            q3 = q.reshape(B, S, 3 * D)
            o, lse, moved = _attn_fwd(q3, k_all, v_all, B, carry)
            landed(wg, used, moved)
            o2 = o.reshape(N, D)
            y, h_next = _mm("o_proj", o2, whole_rows(O), mode="nn", M=N, N=D, K=D, b_l=j,
                            out_dtype=(BF16, F32), epi=_gated_residual, extras=[("mn", h), ("ex", gate)], rows_per_ex=S)
            sv["q"], sv["o"], sv["lse"] = q3, o, lse
        sv["y0"] = y
        h = h_next
        shift, scale, gate = mod3(l, 1)
        sv["h1"], sv["scale1"], sv["gate1"] = h, scale, gate
        u = _normmod(h, ln_g_full[l, 1].reshape(1, D), scale, shift, B)
        r = _mm("mlp_up", u, wg[W1], mode="nn", M=N, N=FF, K=D, b_lay="cs", b_l=l, b_ns=N_CHIPS, out_dtype=BF16, epi=_relu2)
        y, h = _mm("mlp_down", r, whole_rows(W2), mode="nn", M=N, N=D, K=FF, b_l=l, tk=2048,
                   out_dtype=(BF16, F32), epi=_gated_residual, extras=[("mn", h), ("ex", gate)], rows_per_ex=S)
        sv["u1"], sv["r"], sv["y1"] = u, r, y
        saved.append(sv)
        if l == n_a - 1:
            h_kv = h
            u_kv = _normmod(h, kv_g.reshape(1, D), kv_scale, kv_shift, B)
            half = N_CHIPS // 2
            k_all = _mm("kv_proj", u_kv, wg[KV], mode="nn", M=N, N=3 * D, K=D, b_lay="cs", b_s0=0, b_ns=half).reshape(B, S, 3 * D)
            v_all = _mm("kv_proj", u_kv, wg[KV], mode="nn", M=N, N=3 * D, K=D, b_lay="cs", b_s0=half, b_ns=half).reshape(B, S, 3 * D)

    loss_buf, dh, d_final_g = _loss_head(h, final_g.reshape(1, D), loss_target.reshape(N, D))
    loss = lax.psum(loss_buf[0, 0], ("x", "y", "c"))

    dg = [lax.empty((N_CHIPS,) + s.shape, BF16) for s in shards]
    recv = [lax.empty((N_CHIPS,) + s.shape, BF16) for s in shards]

    def send(pieces):
        return _layer_moves("scatter", dg, recv, pieces)

    send_with = {l: [(W1, l), (W2, l)] for l in range(depth)}
    for l in range(n_a):
        send_with[l] += [(GLU, l)]
    for j in range(n_b):
        send_with[n_a + j] += [(O, j)]
        send_with[n_a + j - 1] += [(Q, j)]
    send_with[n_a - 1] += [(KV, 0)]
    d_ln_g = [[None, None] for _ in range(depth)]
    d_mods = [[None, None] for _ in range(depth)]
    d_s5 = [None] * n_a
    dk_acc = dv_acc = None
    half = N_CHIPS // 2

    def tn_grad(name, a, d, into, l, Mr, Nc, lay, s0=0, ns=N_CHIPS):
        return _mm(name, a, _as4(d), mode="tn", M=Mr, N=Nc, K=N, b_lay="cs", out_dtype=BF16, out_lay=lay,
                   out4_shape=into.shape, out_into=into, out_l=l, out_s0=s0, out_ns=ns, tk=2048)

    for l in reversed(range(depth)):
        sv = saved[l]
        dy, d_gate1 = _residual_bwd(dh, sv["gate1"], sv["y1"], B)
        dg[W2] = tn_grad("mlp_down_dw", sv["r"], dy, dg[W2], l, FF, D, "rs")
        da = _mm("mlp_down_dx", dy, whole_rows(W2), mode="nt", M=N, N=FF, K=D, b_l=l, out_dtype=BF16,
                 epi=_relu2_bwd, extras=[("mn", sv["r"])])
        dg[W1] = tn_grad("mlp_up_dw", sv["u1"], da, dg[W1], l, D, FF, "cs")
        du = _mm("mlp_up_dx", da, wg[W1], mode="nt", M=N, N=D, K=FF, b_lay="cs", b_l=l, b_ns=N_CHIPS)
        dh, dgv, d_scale1, d_shift1 = _normmod_bwd(du, sv["h1"], ln_g_full[l, 1].reshape(1, D), sv["scale1"], dh, B)
        d_ln_g[l][1] = dgv
        d_mods[l][1] = jnp.concatenate([d_shift1, d_scale1, d_gate1], axis=2)
        dy, d_gate0 = _residual_bwd(dh, sv["gate0"], sv["y0"], B)
        if l < n_a:
            bd, cd, pw, seg_f, seg_b, disc_vjp = s5_tabs[l]
            dzz = _glu_bwd(dy, sv["zz"])
            dg[GLU] = tn_grad("glu_proj_dw", sv["z"], dzz, dg[GLU], l, D, 2 * D, "cs")
            dz = _mm("glu_proj_dx", dzz, wg[GLU], mode="nt", M=N, N=D, K=2 * D, b_lay="cs", b_l=l, b_ns=N_CHIPS)
            carry, used = send(send_with[l])
            du, d_bd, d_cd, d_a2, d_dskip, moved = _s5_bwd(sv["u0"], dz, bd, cd, pw, seg_f, seg_b,
                                                           ssm_d_full[l].reshape(1, D), sv["carries"], B, carry)
            landed(recv, used, moved)
            d_are = (d_a2[:, 0, :CHUNK_STATE] + d_a2[:, 0, CHUNK_STATE:]).reshape(-1, SSM_STATE)
            d_aim = (d_a2[:, 1, CHUNK_STATE:] - d_a2[:, 1, :CHUNK_STATE]).reshape(-1, SSM_STATE)
            d_s5[l] = disc_vjp((d_bd, d_cd, d_are, d_aim)) + (d_dskip,)
        else:
            j = l - n_a
            dg[O] = tn_grad("o_proj_dw", sv["o"].reshape(N, D), dy, dg[O], j, D, D, "rs")
            do = _mm("o_proj_dx", dy, whole_rows(O), mode="nt", M=N, N=D, K=D, b_l=j)
            carry, used = send(send_with[l])
            dq, dk_acc, dv_acc, moved = _attn_bwd(sv["q"], k_all, v_all, sv["o"], sv["lse"], do.reshape(B, S, D),
                                                  dk_acc, dv_acc, B, l == n_a, carry)
            landed(recv, used, moved)
            dq2 = dq.reshape(N, 3 * D)
            dg[Q] = tn_grad("q_proj_dw", sv["u0"], dq2, dg[Q], j, D, 3 * D, "cs")
            du = _mm("q_proj_dx", dq2, wg[Q], mode="nt", M=N, N=D, K=3 * D, b_lay="cs", b_l=j, b_ns=N_CHIPS)
        dh, dgv, d_scale0, d_shift0 = _normmod_bwd(du, sv["h0"], ln_g_full[l, 0].reshape(1, D), sv["scale0"], dh, B)
        d_ln_g[l][0] = dgv
        d_mods[l][0] = jnp.concatenate([d_shift0, d_scale0, d_gate0], axis=2)
        if l == n_a:
            dk2, dv2 = dk_acc.reshape(N, 3 * D), dv_acc.reshape(N, 3 * D)
            dg[KV] = tn_grad("kv_proj_dw", u_kv, dk2, dg[KV], 0, D, 3 * D, "cs", s0=0, ns=half)
            dg[KV] = tn_grad("kv_proj_dw", u_kv, dv2, dg[KV], 0, D, 3 * D, "cs", s0=half, ns=half)
            du_kv = _mm("kv_proj_dx", dk2, wg[KV], mode="nt", M=N, N=D, K=3 * D, b_lay="cs", b_s0=0, b_ns=half)
            du_kv = _mm("kv_proj_dx", dv2, wg[KV], mode="nt", M=N, N=D, K=3 * D, b_lay="cs", b_s0=half, b_ns=half,
                        epi=_add, extras=[("mn", du_kv)])
            dh, d_kv_g, d_kv_scale, d_kv_shift = _normmod_bwd(du_kv, h_kv, kv_g.reshape(1, D), kv_scale, dh, B)
    grad_x = dh.reshape(B, S, D)

    own = [_sum_shards(r.reshape(N_CHIPS, -1, r.shape[-1])) for r in recv]

    d_kvmod = jnp.concatenate([d_kv_shift, d_kv_scale], axis=2).reshape(B, 2 * D)
    d_mod_all = jnp.concatenate([d_mods[l][s].reshape(B, 3 * D) for l in range(depth) for s in range(2)], axis=1)
    small = [
        d_mod_all, d_kvmod,
        jnp.stack([jnp.stack([d_ln_g[l][0].reshape(D), d_ln_g[l][1].reshape(D)]) for l in range(depth)]),
        jnp.stack([d_s5[l][0] for l in range(n_a)]), jnp.stack([d_s5[l][1] for l in range(n_a)]),
        jnp.stack([d_s5[l][2] for l in range(n_a)]),
        jnp.stack([d_s5[l][3] for l in range(n_a)]), jnp.stack([d_s5[l][4] for l in range(n_a)]),
        jnp.stack([d_s5[l][5] for l in range(n_a)]), jnp.stack([d_s5[l][6] for l in range(n_a)]),
        jnp.stack([d_s5[l][7].reshape(D) for l in range(n_a)]),
        d_kv_g.reshape(D), d_final_g.reshape(D),
    ]
    small_pack, small_layout, small_rows = _pack(small, D)
    small_buf, other = _all_gather_small(small_pack, _swap_with_sibling(own))
    small_sum = _sum_shards(small_buf.reshape(N_DEV, small_rows, D))
    red = lambda idx: _unpack(small_sum, small_layout, idx)
    per_dev = lambda idx: jnp.concatenate(
        [_unpack(small_buf[d * small_rows:(d + 1) * small_rows], small_layout, idx) for d in range(N_DEV)], axis=0)

    dm_all = per_dev(0).reshape(n_ex, n_mod, 3 * D)
    dm_cols = lax.dynamic_slice_in_dim(dm_all, chip * ada_cols, ada_cols, axis=2).reshape(n_ex, n_mod * ada_cols)
    g_ada_w = _mm("ada_dw", sc_all, _as4(dm_cols), mode="tn", M=D, N=n_mod * ada_cols, K=n_ex, b_lay="cs",
                  out_lay="cs", out4_shape=(n_mod, 1, D, ada_cols), out_ns=n_mod).reshape(ada_w.shape)
    dkvm_all = per_dev(1)
    dkvm_cols = lax.dynamic_slice_in_dim(dkvm_all, chip * kv_cols, kv_cols, axis=1)
    g_kv_ada_w = _mm("ada_dw", sc_all, _as4(dkvm_cols), mode="tn", M=D, N=kv_cols, K=n_ex, b_lay="cs")
    g_ada_b_full = (red(0)[0] + red(0)[1]).reshape(depth, 2, 3 * D) if B == 2 else jnp.sum(red(0), axis=0).reshape(depth, 2, 3 * D)
    g_ada_b = lax.dynamic_slice_in_dim(g_ada_b_full, chip * ada_cols, ada_cols, axis=2)
    g_kv_ada_b = red(1)[0] + red(1)[1] if B == 2 else jnp.sum(red(1), axis=0)
    g_ln_g = lax.dynamic_slice_in_dim(red(2), chip * (D // N_CHIPS), D // N_CHIPS, axis=2)
    g_ssm_d = lax.dynamic_slice_in_dim(red(10), chip * (D // N_CHIPS), D // N_CHIPS, axis=1)
    small_grads = {
        "ln_g": g_ln_g, "ada_b": g_ada_b, "ssm_lam_re": red(3), "ssm_lam_im": red(4), "ssm_log_dt": red(5),
        "ssm_b_re": red(6), "ssm_b_im": red(7), "ssm_c_re": red(8), "ssm_c_im": red(9), "ssm_d": g_ssm_d,
        "kv_g": red(11), "kv_ada_b": g_kv_ada_b, "final_g": red(12),
    }
    small_w = {"ln_g": (ln_g, m_ln_g, v_ln_g), "ada_b": (ada_b, m_ada_b, v_ada_b),
               "ssm_lam_re": (ssm_lam_re, m_ssm_lam_re, v_ssm_lam_re), "ssm_lam_im": (ssm_lam_im, m_ssm_lam_im, v_ssm_lam_im),
               "ssm_log_dt": (ssm_log_dt, m_ssm_log_dt, v_ssm_log_dt), "ssm_b_re": (ssm_b_re, m_ssm_b_re, v_ssm_b_re),
               "ssm_b_im": (ssm_b_im, m_ssm_b_im, v_ssm_b_im), "ssm_c_re": (ssm_c_re, m_ssm_c_re, v_ssm_c_re),
               "ssm_c_im": (ssm_c_im, m_ssm_c_im, v_ssm_c_im), "ssm_d": (ssm_d, m_ssm_d, v_ssm_d),
               "kv_g": (kv_g, m_kv_g, v_kv_g), "kv_ada_b": (kv_ada_b, m_kv_ada_b, v_kv_ada_b),
               "final_g": (final_g, m_final_g, v_final_g)}
    names = list(small_w)
    wp, lay_w, _ = _pack([small_w[n][0] for n in names], D)
    gp, _, _ = _pack([small_grads[n] for n in names], D)
    mp, _, _ = _pack([small_w[n][1] for n in names], D)
    vp, _, _ = _pack([small_w[n][2] for n in names], D)
    _, d_p, m_p, v_p = _adamw(wp, [gp], mp, vp)
    upd = {n: (small_grads[n].reshape(small_w[n][0].shape), _unpack(d_p, lay_w, i), _unpack(m_p, lay_w, i), _unpack(v_p, lay_w, i))
           for i, n in enumerate(names)}

    def big(w, m, v, g_own, g_other=None):
        C = w.shape[-1]
        gs = [g_own.reshape(-1, C)] + ([g_other.reshape(-1, C)] if g_other is not None else [])
        return tuple(t.reshape(w.shape) for t in _adamw(w.reshape(-1, C), gs, m.reshape(-1, C), v.reshape(-1, C)))

    upd["ssm_w_glu"] = big(ssm_w_glu, m_ssm_w_glu, v_ssm_w_glu, own[0], other[0])
    upd["w_kv"] = big(w_kv, m_w_kv, v_w_kv, own[1], other[1])
    upd["attn_w_q"] = big(attn_w_q, m_attn_w_q, v_attn_w_q, own[2], other[2])
    upd["attn_w_o"] = big(attn_w_o, m_attn_w_o, v_attn_w_o, own[3], other[3])
    upd["mlp_w1"] = big(mlp_w1, m_mlp_w1, v_mlp_w1, own[4], other[4])
    upd["mlp_w2"] = big(mlp_w2, m_mlp_w2, v_mlp_w2, own[5], other[5])
    upd["ada_w"] = big(ada_w, m_ada_w, v_ada_w, g_ada_w)
    upd["kv_ada_w"] = big(kv_ada_w, m_kv_ada_w, v_kv_ada_w, g_kv_ada_w)

    order = ["ln_g", "ada_w", "ada_b", "ssm_lam_re", "ssm_lam_im", "ssm_log_dt", "ssm_b_re", "ssm_b_im", "ssm_c_re",
             "ssm_c_im", "ssm_d", "ssm_w_glu", "kv_g", "kv_ada_w", "kv_ada_b", "w_kv", "attn_w_q", "attn_w_o", "mlp_w1",
             "mlp_w2", "final_g"]
    return (loss, grad_x, *[upd[n][0] for n in order], *[upd[n][1] for n in order], *[upd[n][2] for n in order],
            *[upd[n][3] for n in order])
```

```python
import functools
import math

import jax
import jax.numpy as jnp
from jax import lax
from jax.experimental import pallas as pl
from jax.experimental.pallas import tpu as pltpu

F32 = jnp.float32
BF16 = jnp.bfloat16
MESH = pl.DeviceIdType.MESH

EPS = 1e-6
NEG = -1e30
SSM_GROUP = 16
SSM_STATE = 64
HEAD_DIM = 64
ATTN_BLOCK = 128
DILATIONS = (1, 4, 16)
ADAM_LR, ADAM_B1, ADAM_B2, ADAM_EPS, ADAM_WD, ADAM_STEP = 0.001, 0.9, 0.999, 1e-08, 0.01, 10

LANES = 128
SUBLANES = 8
CHUNK_GROUPS = LANES // SSM_GROUP
CHUNK_STATE = CHUNK_GROUPS * SSM_STATE
VMEM_LIMIT = 56 * 1024 * 1024


def _div(dim, pref, mult):
    t = min(pref, dim) // mult * mult
    while t >= mult:
        if dim % t == 0:
            return t
        t -= mult
    return dim


def _params(*sem):
    return pltpu.CompilerParams(dimension_semantics=sem, vmem_limit_bytes=VMEM_LIMIT)


def _coords():
    return lax.axis_index("x"), lax.axis_index("y"), lax.axis_index("c")


def _other_chips(cx, cy):
    return [(1 - cx, cy), (cx, 1 - cy), (1 - cx, 1 - cy)]


class _Carry:
    def __init__(self, srcs, dsts, plan, n_remote, n_local, onward=None, n_onward=0):
        self.srcs, self.dsts, self.plan, self.n_remote, self.n_local = list(srcs), list(dsts), plan, n_remote, n_local
        self.onward, self.n_onward = onward, n_onward


def _carried_call(body, *, name, grid, in_specs, out_specs, out_shape, scratch_shapes, operands, sem, carry=None):
    if carry is None:
        outs = pl.pallas_call(body, name=name, grid=grid, in_specs=in_specs, out_specs=out_specs, out_shape=out_shape,
                              scratch_shapes=scratch_shapes, compiler_params=_params(*sem))(*operands)
        return list(outs), []
    n_in, n_out, n_scr = len(in_specs), len(out_specs), len(scratch_shapes)
    ns, nd = len(carry.srcs), len(carry.dsts)

    def wrapped(*refs):
        base_in, src_refs = refs[:n_in], refs[n_in:n_in + ns]
        o0 = n_in + ns + nd
        base_out, dst_refs = refs[o0:o0 + n_out], refs[o0 + n_out:o0 + n_out + nd]
        s0 = o0 + n_out + nd
        base_scr = refs[s0:s0 + n_scr]
        send_sems, recv_sems, local_sems = refs[s0 + n_scr:]
        pids = [pl.program_id(a) for a in range(len(grid))]
        first = functools.reduce(jnp.logical_and, [p == 0 for p in pids])
        last = functools.reduce(jnp.logical_and, [p == g - 1 for p, g in zip(pids, grid)])

        def remote_copies(moves, k0):
            return [pltpu.make_async_remote_copy(src_ref=s, dst_ref=d, send_sem=send_sems.at[k0 + i], recv_sem=recv_sems.at[k0 + i],
                                                 device_id=peer, device_id_type=MESH) for i, (s, d, peer) in enumerate(moves)]

        def copies():
            remote, local = carry.plan(src_refs, dst_refs, _coords())
            return remote_copies(remote, 0), [pltpu.make_async_copy(s, d, local_sems.at[i]) for i, (s, d) in enumerate(local)]

        @pl.when(first)
        def _():
            remote, local = copies()
            for cp in local + remote:
                cp.start()

        body(*base_in, *base_out, *base_scr)

        @pl.when(last)
        def _():
            remote, local = copies()
            for cp in remote:
                cp.wait_send()
                cp.wait_recv()
            for cp in local:
                cp.wait()
            if carry.onward is not None:
                second = remote_copies(carry.onward(src_refs, dst_refs, _coords()), carry.n_remote)
                for cp in second:
                    cp.start()
                for cp in second:
                    cp.wait_send()
                    cp.wait_recv()

    anyspec = pl.BlockSpec(memory_space=pl.ANY)
    outs = pl.pallas_call(
        wrapped, name=name, grid=grid, in_specs=list(in_specs) + [anyspec] * (ns + nd),
        out_specs=list(out_specs) + [anyspec] * nd,
        out_shape=list(out_shape) + [jax.ShapeDtypeStruct(d.shape, d.dtype) for d in carry.dsts],
        scratch_shapes=list(scratch_shapes) + [pltpu.SemaphoreType.DMA((carry.n_remote + carry.n_onward,)),
                                               pltpu.SemaphoreType.DMA((carry.n_remote + carry.n_onward,)),
                                               pltpu.SemaphoreType.DMA((max(carry.n_local, 1),))],
        input_output_aliases={n_in + ns + i: n_out + i for i in range(nd)},
        compiler_params=_params(*(["arbitrary"] * len(grid))),
    )(*operands, *carry.srcs, *carry.dsts)
    return list(outs[:n_out]), list(outs[n_out:])


def _mm(name, a, b4, *, mode, M, N, K, b_lay="cs", b_l=0, b_s0=0, b_ns=1, out_dtype=F32, out_lay=None, out4_shape=None,
        out_into=None, out_l=0, out_s0=0, out_ns=1, epi=None, extras=(), rows_per_ex=None, tm=1024, tn=1024, tk=1024):
    _, _, bR, bC = b4.shape
    tm = _div(M, tm, SUBLANES if M % 16 else 16)
    brows, bcols = (N, K) if mode == "nt" else (K, N)
    if b_lay == "cs":
        assert bR == brows and bC * b_ns == bcols, (name, b4.shape, brows, bcols)
    else:
        assert bC == bcols and bR * b_ns == brows, (name, b4.shape, brows, bcols)
    n_lim = N
    k_lim = K
    if mode == "nt":
        if b_lay == "cs":
            k_lim = bC
        else:
            n_lim = bR
    else:
        if b_lay == "cs":
            n_lim = bC
        else:
            k_lim = bR
    if out_lay == "cs":
        oR, oC = out4_shape[2], out4_shape[3]
        assert oR == M and oC * out_ns == N, (name, out4_shape, M, N)
        n_lim = math.gcd(n_lim, oC)
    elif out_lay == "rs":
        oR, oC = out4_shape[2], out4_shape[3]
        assert oC == N and oR * out_ns == M, (name, out4_shape, M, N)
        tm = _div(oR, tm, SUBLANES)
    tn = _div(n_lim, tn, LANES)
    tk = _div(k_lim, tk, LANES if mode != "tn" else SUBLANES)
    if mode == "tn":
        tk = _div(k_lim, tk, 16) if k_lim % 16 == 0 else tk
    nk = K // tk
    grid = (M // tm, N // tn, nk)

    if mode == "tn":
        a_spec = pl.BlockSpec((tk, tm), lambda i, j, k: (k, i))
    else:
        a_spec = pl.BlockSpec((tm, tk), lambda i, j, k: (i, k))

    def b_index(ri, ci, br, bc):
        if b_lay == "cs":
            per = bC // bc
            return (b_s0 + ci // per, b_l, ri, ci % per)
        per = bR // br
        return (b_s0 + ri // per, b_l, ri % per, ci)

    if mode == "nt":
        b_spec = pl.BlockSpec((None, None, tn, tk), lambda i, j, k: b_index(j, k, tn, tk))
    else:
        b_spec = pl.BlockSpec((None, None, tk, tn), lambda i, j, k: b_index(k, j, tk, tn))

    in_specs = [a_spec, b_spec]
    operands = [a, b4]
    for kind, arr in extras:
        if kind == "mn":
            in_specs.append(pl.BlockSpec((tm, tn), lambda i, j, k: (i, j)))
        elif kind == "ex":
            per_ex = rows_per_ex // tm
            in_specs.append(pl.BlockSpec((None, 1, tn), lambda i, j, k: (i // per_ex, 0, j)))
        else:
            in_specs.append(pl.BlockSpec((1, tn), lambda i, j, k: (0, j)))
        operands.append(arr)
    n_extra = len(extras)

    multi = isinstance(out_dtype, tuple)
    n_out = len(out_dtype) if multi else 1
    if out_lay is None:
        out_shape = [jax.ShapeDtypeStruct((M, N), dt) for dt in (out_dtype if multi else (out_dtype,))]
        out_spec = [pl.BlockSpec((tm, tn), lambda i, j, k: (i, j)) for _ in range(n_out)]
    else:
        out_shape = [jax.ShapeDtypeStruct(tuple(out4_shape), out_dtype)]
        if out_lay == "cs":
            per_o = oC // tn
            out_spec = [pl.BlockSpec((None, None, tm, tn), lambda i, j, k: (out_s0 + j // per_o, out_l, i, j % per_o))]
        else:
            per_o = oR // tm
            out_spec = [pl.BlockSpec((None, None, tm, tn), lambda i, j, k: (out_s0 + i // per_o, out_l, i % per_o, j))]
    aliases = {}
    if out_into is not None:
        in_specs.append(pl.BlockSpec(memory_space=pl.ANY))
        operands.append(out_into)
        aliases = {len(operands) - 1: 0}

    dims = {"nn": (((1,), (0,)), ((), ())), "nt": (((1,), (1,)), ((), ())), "tn": (((0,), (0,)), ((), ()))}[mode]

    def body(a_ref, b_ref, *rest):
        extra_refs = rest[:n_extra]
        o_refs = rest[len(rest) - n_out - (nk > 1):len(rest) - (nk > 1)]

        def finish(r):
            if epi is not None:
                r = epi(r, *[e[...] for e in extra_refs])
            for o_ref, val in zip(o_refs, r if multi else (r,)):
                o_ref[...] = val.astype(o_ref.dtype)

        part = lax.dot_general(a_ref[...].astype(BF16), b_ref[...].astype(BF16), dims, preferred_element_type=F32)
        if nk == 1:
            finish(part)
            return
        acc = rest[-1]
        k = pl.program_id(2)

        @pl.when(k == 0)
        def _():
            acc[...] = part

        @pl.when(k != 0)
        def _():
            acc[...] += part

        @pl.when(k == nk - 1)
        def _():
            finish(acc[...])

    outs = pl.pallas_call(
        body, name=name, grid=grid, in_specs=in_specs, out_specs=out_spec, out_shape=out_shape,
        scratch_shapes=[pltpu.VMEM((tm, tn), F32)] if nk > 1 else [], input_output_aliases=aliases,
        compiler_params=_params("parallel", "parallel", "arbitrary"),
    )(*operands)
    return tuple(outs) if multi else outs[0]


def _as4(w):
    return w.reshape((1, 1) + w.shape)


def _relu2(acc):
    r = jnp.maximum(acc, 0.0)
    return r * r


def _relu2_bwd(acc, r):
    return acc * (2.0 * jnp.sqrt(r.astype(F32)))


def _add(acc, e):
    return acc + e


def _gated_residual(acc, h, gate):
    return acc, h + gate * acc


def _modulated_norm(x, g, scale, shift):
    rstd = lax.rsqrt(jnp.mean(x * x, axis=-1, keepdims=True) + EPS)
    return ((x * rstd) * g) * (1.0 + scale) + shift


def _gated_residual_norm(acc, h, gate, g, scale, shift):
    h_new = h + gate * acc
    return acc, h_new, _modulated_norm(h_new, g, scale, shift)


def _row_tiles(N, B, pref=256):
    S = N // B
    tm = _div(S, pref, SUBLANES)
    return tm, S // tm


def _normmod(h, g, scale, shift, B):
    N, D = h.shape
    tm, per_ex = _row_tiles(N, B)

    def body(h_ref, g_ref, sc_ref, sh_ref, u_ref):
        u_ref[...] = _modulated_norm(h_ref[...], g_ref[...], sc_ref[...], sh_ref[...]).astype(u_ref.dtype)

    tok = pl.BlockSpec((tm, D), lambda i: (i, 0))
    vec = pl.BlockSpec((1, D), lambda i: (0, 0))
    ex = pl.BlockSpec((None, 1, D), lambda i: (i // per_ex, 0, 0))
    return pl.pallas_call(
        body, name="normmod_fwd", grid=(N // tm,), in_specs=[tok, vec, ex, ex], out_specs=tok,
        out_shape=jax.ShapeDtypeStruct((N, D), BF16), compiler_params=_params("parallel"),
    )(h, g, scale, shift)


def _normmod_bwd(du, h, g, scale, dh_in, B, below=None):
    N, D = h.shape
    tm, per_ex = _row_tiles(N, B)
    fused = below is not None

    def body(*refs):
        du_ref, h_ref, g_ref, sc_ref, dhin_ref = refs[:5]
        dh_ref, dg_ref, dsc_ref, dsh_ref = refs[5 + 2 * fused:9 + 2 * fused]
        i = pl.program_id(0)
        x = h_ref[...]
        gv = g_ref[...]
        d_u = du_ref[...].astype(F32)
        rstd = lax.rsqrt(jnp.mean(x * x, axis=-1, keepdims=True) + EPS)
        xn = x * rstd
        dyg = d_u * (1.0 + sc_ref[...])
        dxn = dyg * gv
        dh = dhin_ref[...] + rstd * (dxn - xn * jnp.mean(dxn * xn, axis=-1, keepdims=True))
        dh_ref[...] = dh
        sums = [(dsc_ref, jnp.sum(d_u * (xn * gv), axis=0, keepdims=True)), (dsh_ref, jnp.sum(d_u, axis=0, keepdims=True))]
        if fused:
            y_ref, gt_ref = refs[5:7]
            dy_ref, dgt_ref = refs[9 + 2 * fused:]
            dy_ref[...] = (gt_ref[...] * dh).astype(dy_ref.dtype)
            sums.append((dgt_ref, jnp.sum(dh * y_ref[...], axis=0, keepdims=True)))
        dg_t = jnp.sum(dyg * xn, axis=0, keepdims=True)

        @pl.when(i % per_ex == 0)
        def _():
            for ref, val in sums:
                ref[...] = val

        @pl.when(i % per_ex != 0)
        def _():
            for ref, val in sums:
                ref[...] += val

        @pl.when(i == 0)
        def _():
            dg_ref[...] = dg_t

        @pl.when(i != 0)
        def _():
            dg_ref[...] += dg_t

    tok = pl.BlockSpec((tm, D), lambda i: (i, 0))
    vec = pl.BlockSpec((1, D), lambda i: (0, 0))
    ex = pl.BlockSpec((None, 1, D), lambda i: (i // per_ex, 0, 0))
    per_ex_shape = jax.ShapeDtypeStruct((B, 1, D), F32)
    outs = pl.pallas_call(
        body, name="normmod_bwd", grid=(N // tm,), in_specs=[tok, tok, vec, ex, tok] + ([tok, ex] if fused else []),
        out_specs=[tok, vec, ex, ex] + ([tok, ex] if fused else []),
        out_shape=[jax.ShapeDtypeStruct((N, D), F32), jax.ShapeDtypeStruct((1, D), F32), per_ex_shape, per_ex_shape]
        + ([jax.ShapeDtypeStruct((N, D), BF16), per_ex_shape] if fused else []),
        compiler_params=_params("arbitrary"),
    )(du, h, g, scale, dh_in, *(below if fused else ()))
    return tuple(outs) if fused else tuple(outs) + (None, None)


def _residual_bwd(dh, gate, y, B):
    N, D = dh.shape
    tm, per_ex = _row_tiles(N, B)

    def body(dh_ref, gt_ref, y_ref, dy_ref, dgt_ref):
        i = pl.program_id(0)
        d = dh_ref[...]
        dy_ref[...] = (gt_ref[...] * d).astype(dy_ref.dtype)
        t = jnp.sum(d * y_ref[...], axis=0, keepdims=True)

        @pl.when(i % per_ex == 0)
        def _():
            dgt_ref[...] = t

        @pl.when(i % per_ex != 0)
        def _():
            dgt_ref[...] += t

    tok = pl.BlockSpec((tm, D), lambda i: (i, 0))
    ex = pl.BlockSpec((None, 1, D), lambda i: (i // per_ex, 0, 0))
    return pl.pallas_call(
        body, name="residual_bwd", grid=(N // tm,), in_specs=[tok, ex, tok], out_specs=[tok, ex],
        out_shape=[jax.ShapeDtypeStruct((N, D), BF16), jax.ShapeDtypeStruct((B, 1, D), F32)],
        compiler_params=_params("arbitrary"),
    )(dh, gate, y)


def _glu_residual_norm(zz, h, gate, g, scale, shift, B):
    N, D2 = zz.shape
    D = D2 // 2
    tm, per_ex = _row_tiles(N, B)

    def body(v_ref, g_ref, h_ref, gt_ref, ng_ref, sc_ref, sh_ref, y_ref, o_ref, u_ref):
        y = v_ref[...] * jax.nn.sigmoid(g_ref[...])
        y_ref[...] = y.astype(y_ref.dtype)
        h_new = h_ref[...] + gt_ref[...] * y
        o_ref[...] = h_new
        u_ref[...] = _modulated_norm(h_new, ng_ref[...], sc_ref[...], sh_ref[...]).astype(u_ref.dtype)

    tok = pl.BlockSpec((tm, D), lambda i: (i, 0))
    vec = pl.BlockSpec((1, D), lambda i: (0, 0))
    ex = pl.BlockSpec((None, 1, D), lambda i: (i // per_ex, 0, 0))
    return pl.pallas_call(
        body, name="glu_fwd", grid=(N // tm,),
        in_specs=[tok, pl.BlockSpec((tm, D), lambda i: (i, 1)), tok, ex, vec, ex, ex], out_specs=[tok, tok, tok],
        out_shape=[jax.ShapeDtypeStruct((N, D), BF16), jax.ShapeDtypeStruct((N, D), F32), jax.ShapeDtypeStruct((N, D), BF16)],
        compiler_params=_params("parallel"),
    )(zz, zz, h, gate, g, scale, shift)


def _glu_bwd(dy, zz):
    N, D2 = zz.shape
    D = D2 // 2
    tm = _div(N, 256, SUBLANES)

    def body(dy_ref, v_ref, g_ref, o_ref):
        d = dy_ref[...].astype(F32)
        s = jax.nn.sigmoid(g_ref[...])
        o_ref[...] = jnp.concatenate([d * s, d * v_ref[...] * s * (1.0 - s)], axis=1).astype(o_ref.dtype)

    return pl.pallas_call(
        body, name="glu_bwd", grid=(N // tm,),
        in_specs=[pl.BlockSpec((tm, D), lambda i: (i, 0)), pl.BlockSpec((tm, D), lambda i: (i, 0)),
                  pl.BlockSpec((tm, D), lambda i: (i, 1))],
        out_specs=pl.BlockSpec((tm, D2), lambda i: (i, 0)), out_shape=jax.ShapeDtypeStruct((N, D2), BF16),
        compiler_params=_params("parallel"),
    )(dy, zz, zz)


def _loss_head(h, g, target):
    N, D = h.shape
    tm = _div(N, 256, SUBLANES)

    def body(h_ref, g_ref, t_ref, loss_ref, dh_ref, dg_ref):
        i = pl.program_id(0)
        x = h_ref[...]
        gv = g_ref[...]
        rstd = lax.rsqrt(jnp.mean(x * x, axis=-1, keepdims=True) + EPS)
        xn = x * rstd
        err = xn * gv - t_ref[...]
        part = 0.5 * jnp.sum(jnp.sum(err * err, axis=-1, keepdims=True) / D, axis=0, keepdims=True)
        dy = err / D
        dxn = dy * gv
        dh_ref[...] = rstd * (dxn - xn * jnp.mean(dxn * xn, axis=-1, keepdims=True))
        dg_t = jnp.sum(dy * xn, axis=0, keepdims=True)
        part = jnp.broadcast_to(part, loss_ref.shape)

        @pl.when(i == 0)
        def _():
            loss_ref[...] = part
            dg_ref[...] = dg_t

        @pl.when(i != 0)
        def _():
            loss_ref[...] += part
            dg_ref[...] += dg_t

    tok = pl.BlockSpec((tm, D), lambda i: (i, 0))
    vec = pl.BlockSpec((1, D), lambda i: (0, 0))
    return pl.pallas_call(
        body, name="loss_head", grid=(N // tm,), in_specs=[tok, vec, tok],
        out_specs=[pl.BlockSpec((SUBLANES, LANES), lambda i: (0, 0)), tok, vec],
        out_shape=[jax.ShapeDtypeStruct((SUBLANES, LANES), F32), jax.ShapeDtypeStruct((N, D), F32),
                   jax.ShapeDtypeStruct((1, D), F32)],
        compiler_params=_params("arbitrary"),
    )(h, g, target)


def _swap_halves(x):
    half = x.shape[-1] // 2
    return jnp.concatenate([x[:, half:], x[:, :half]], axis=1)


def _gelu(y):
    return jax.nn.gelu(y)


def _gelu_grad(y):
    c0 = math.sqrt(2.0 / math.pi)
    inner = c0 * (y + 0.044715 * y * y * y)
    t = jnp.tanh(inner)
    return 0.5 * (1.0 + t) + 0.5 * y * (1.0 - t * t) * c0 * (1.0 + 3.0 * 0.044715 * y * y)


def _s5_discretize(lam_re, lam_im, log_dt, b_re, b_im, c_re, c_im):
    G = lam_re.shape[0]
    nch = G // CHUNK_GROUPS
    dt = jnp.exp(log_dt)[:, None]
    er = jnp.exp(lam_re * dt)
    a_re = er * jnp.cos(lam_im * dt)
    a_im = er * jnp.sin(lam_im * dt)
    den = lam_re * lam_re + lam_im * lam_im
    n_re, n_im = a_re - 1.0, a_im
    f_re = (n_re * lam_re + n_im * lam_im) / den
    f_im = (n_im * lam_re - n_re * lam_im) / den
    bb_re = f_re[..., None] * b_re - f_im[..., None] * b_im
    bb_im = f_re[..., None] * b_im + f_im[..., None] * b_re
    eye = jnp.eye(CHUNK_GROUPS, dtype=F32)

    def pack_b(bb):
        bb = bb.reshape(nch, CHUNK_GROUPS, SSM_STATE, SSM_GROUP)
        return jnp.einsum("jgpc,gh->jgchp", bb, eye).reshape(nch, LANES, CHUNK_STATE)

    def pack_c(cc):
        cc = cc.reshape(nch, CHUNK_GROUPS, SSM_GROUP, SSM_STATE)
        return jnp.einsum("jgcp,gh->jgphc", cc, eye).reshape(nch, CHUNK_STATE, LANES)

    bd = jnp.concatenate([pack_b(bb_re), pack_b(bb_im)], axis=2)
    cd = jnp.concatenate([pack_c(c_re), pack_c(-c_im)], axis=1)
    return bd, cd, a_re, a_im


S5_TILE = 256
S5_SEG = S5_TILE // SUBLANES
S5_UNROLL = 4


def _s5_scan_coefs(lam_re, lam_im, log_dt, seg):
    G = lam_re.shape[0]
    nch = G // CHUNK_GROUPS
    dt = jnp.exp(log_dt)[:, None]
    rate = (lam_re * dt).reshape(nch, 1, CHUNK_STATE)
    freq = (lam_im * dt).reshape(nch, 1, CHUNK_STATE)

    def powers(ks):
        k = jnp.asarray(ks, F32)[None, :, None]
        er = jnp.exp(k * rate)
        re, im = er * jnp.cos(k * freq), er * jnp.sin(k * freq)
        return jnp.concatenate([re, re], axis=2), jnp.concatenate([-im, im], axis=2)

    pw = jnp.stack(powers(range(1, seg + 1)), axis=1)
    steps = (1, 2, 4)
    re, im = powers([s * seg for s in steps])
    row = jnp.arange(SUBLANES, dtype=jnp.int32)[None, None, :, None]
    shift = jnp.asarray(steps, jnp.int32)[None, :, None, None]

    def table(reverse):
        mask = (row < SUBLANES - shift) if reverse else (row >= shift)
        pair = jnp.stack([jnp.where(mask, re[:, :, None, :], 0.0),
                          jnp.where(mask, (-im if reverse else im)[:, :, None, :], 0.0)], axis=2)
        return pair.reshape(nch, 2 * len(steps), SUBLANES, 2 * CHUNK_STATE)

    return pw, table(False), table(True)


def _to_segments(dst_s, src_ref, seg):
    for j in range(SUBLANES):
        dst_s[pl.ds(j, seg, stride=SUBLANES), :] = src_ref[pl.ds(j * seg, seg), :].astype(F32)


def _from_segments(dst_ref, src_s, seg):
    for j in range(SUBLANES):
        dst_ref[pl.ds(j * seg, seg), :] = src_s[pl.ds(j, seg, stride=SUBLANES), :].astype(dst_ref.dtype)


def _seg_scan(x_ref, pw_ref, seg_ref, carry_ref, c_ref, seg, reverse):
    W = x_ref.shape[-1]
    tm = x_ref.shape[0]
    sgn = -1.0 if reverse else 1.0
    ar = jnp.broadcast_to(pw_ref[0, 0:1, :], (SUBLANES, W))
    ai = sgn * jnp.broadcast_to(pw_ref[1, 0:1, :], (SUBLANES, W))

    def rows(i):
        return pl.ds(pl.multiple_of(i * SUBLANES, SUBLANES), SUBLANES)

    def step(t, prev):
        i = (seg - 2 - t) if reverse else (t + 1)
        x = x_ref[rows(i), :] + ar * prev + ai * _swap_halves(prev)
        x_ref[rows(i), :] = x
        return x

    start = (seg - 1) * SUBLANES if reverse else 0
    edge = lax.fori_loop(0, seg - 1, step, x_ref[start:start + SUBLANES, :], unroll=S5_UNROLL)
    row = lax.broadcasted_iota(jnp.int32, (SUBLANES, W), 0)
    if reverse:
        f = jnp.where(row == SUBLANES - 1, carry_ref[...], pltpu.roll(edge, SUBLANES - 1, 0))
    else:
        f = jnp.where(row == 0, carry_ref[...], pltpu.roll(edge, 1, 0))
    for si, s in enumerate((1, 2, 4)):
        fs = pltpu.roll(f, (SUBLANES - s) if reverse else s, 0)
        f = f + seg_ref[2 * si] * fs + seg_ref[2 * si + 1] * _swap_halves(fs)
    c_ref[...] = f
    fsw = _swap_halves(f)

    def fix(i, _):
        k = (seg - 1 - i) if reverse else i
        x_ref[rows(i), :] = x_ref[rows(i), :] + pw_ref[0, pl.ds(k, 1), :] * f + (sgn * pw_ref[1, pl.ds(k, 1), :]) * fsw
        return 0

    lax.fori_loop(0, seg, fix, 0, unroll=S5_UNROLL)
    leaving = x_ref[0:1, :] if reverse else x_ref[tm - 1:tm, :]
    carry_ref[...] = jnp.broadcast_to(leaving, carry_ref.shape)


def _s5_fwd(u, bd, cd, pw, seg_f, d_skip, B, carry=None):
    N, D = u.shape
    S = N // B
    nch = D // LANES
    W = 2 * CHUNK_STATE
    tm, seg = S5_TILE, S5_SEG
    nt = S // tm

    def body(u_ref, bd_ref, cd_ref, pw_ref, seg_ref, d_ref, z_ref, cin_ref, x_s, carry, c_s, u_s, z_s):
        t = pl.program_id(2)

        @pl.when(t == 0)
        def _():
            carry[...] = jnp.zeros_like(carry)

        cin_ref[...] = carry[...]
        _to_segments(u_s, u_ref, seg)
        uf = u_s[...]
        x_s[...] = jnp.dot(uf.astype(BF16), bd_ref[...], preferred_element_type=F32)
        _seg_scan(x_s, pw_ref, seg_ref, carry, c_s, seg, False)
        y = jnp.dot(x_s[...].astype(BF16), cd_ref[...], preferred_element_type=F32) + d_ref[...] * uf
        z_s[...] = _gelu(y)
        _from_segments(z_ref, z_s, seg)

    (z, carries), moved = _carried_call(
        body, name="s5_fwd", grid=(nch, B, nt),
        in_specs=[pl.BlockSpec((tm, LANES), lambda j, b, t: (b * nt + t, j)),
                  pl.BlockSpec((None, LANES, W), lambda j, b, t: (j, 0, 0)),
                  pl.BlockSpec((None, W, LANES), lambda j, b, t: (j, 0, 0)),
                  pl.BlockSpec((None, 2, seg, W), lambda j, b, t: (j, 0, 0, 0)),
                  pl.BlockSpec((None, 6, SUBLANES, W), lambda j, b, t: (j, 0, 0, 0)),
                  pl.BlockSpec((1, LANES), lambda j, b, t: (0, j))],
        out_specs=[pl.BlockSpec((tm, LANES), lambda j, b, t: (b * nt + t, j)),
                   pl.BlockSpec((None, None, SUBLANES, W), lambda j, b, t: (j, b * nt + t, 0, 0))],
        out_shape=[jax.ShapeDtypeStruct((N, D), BF16), jax.ShapeDtypeStruct((nch, B * nt, SUBLANES, W), F32)],
        scratch_shapes=[pltpu.VMEM((tm, W), F32), pltpu.VMEM((SUBLANES, W), F32), pltpu.VMEM((SUBLANES, W), F32),
                        pltpu.VMEM((tm, LANES), F32), pltpu.VMEM((tm, LANES), F32)],
        operands=(u, bd, cd, pw, seg_f, d_skip), sem=("parallel", "arbitrary", "arbitrary"), carry=carry)
    return z, carries, moved


def _s5_bwd(u, dz, bd, cd, pw, seg_f, seg_b, d_skip, carries, B, carry=None):
    N, D = u.shape
    S = N // B
    nch = D // LANES
    W = 2 * CHUNK_STATE
    tm, seg = S5_TILE, S5_SEG
    nt = S // tm
    tn_dims = (((0,), (0,)), ((), ()))
    nt_dims = (((1,), (1,)), ((), ()))

    def body(u_ref, dz_ref, bd_ref, cd_ref, pw_ref, sf_ref, sb_ref, d_ref, cin_ref,
             du_ref, dbd_ref, dcd_ref, da_ref, dd_ref, x_s, l_s, carry, lcarry, c_s, lc_s, u_s, t_s):
        b = pl.program_id(1)
        t = pl.program_id(2)

        @pl.when((b == 0) & (t == 0))
        def _():
            dbd_ref[...] = jnp.zeros_like(dbd_ref)
            dcd_ref[...] = jnp.zeros_like(dcd_ref)
            da_ref[...] = jnp.zeros_like(da_ref)
            dd_ref[...] = jnp.zeros_like(dd_ref)

        @pl.when(t == 0)
        def _():
            lcarry[...] = jnp.zeros_like(lcarry)

        _to_segments(u_s, u_ref, seg)
        _to_segments(t_s, dz_ref, seg)
        uf = u_s[...]
        uv = uf.astype(BF16)
        carry[...] = cin_ref[...]
        x_s[...] = jnp.dot(uv, bd_ref[...], preferred_element_type=F32)
        _seg_scan(x_s, pw_ref, sf_ref, carry, c_s, seg, False)
        xb = x_s[...].astype(BF16)
        y = jnp.dot(xb, cd_ref[...], preferred_element_type=F32) + d_ref[...] * uf
        dy = t_s[...] * _gelu_grad(y)
        dd_ref[...] += jnp.sum(dy * uf, axis=0, keepdims=True)
        dyb = dy.astype(BF16)
        dcd_ref[...] += lax.dot_general(xb, dyb, tn_dims, preferred_element_type=F32)
        l_s[...] = lax.dot_general(dyb, cd_ref[...], nt_dims, preferred_element_type=F32)
        _seg_scan(l_s, pw_ref, sb_ref, lcarry, lc_s, seg, True)
        lb = l_s[...].astype(BF16)
        dbd_ref[...] += lax.dot_general(uv, lb, tn_dims, preferred_element_type=F32)
        t_s[...] = lax.dot_general(lb, bd_ref[...], nt_dims, preferred_element_type=F32) + d_ref[...] * dy
        _from_segments(du_ref, t_s, seg)
        lam_rest, x_prev = l_s[SUBLANES:, :], x_s[:tm - SUBLANES, :]
        lam_0, c_in = l_s[:SUBLANES, :], c_s[...]
        da_ref[0:1, :] += (jnp.sum(lam_rest * x_prev, axis=0, keepdims=True) + jnp.sum(lam_0 * c_in, axis=0, keepdims=True))
        da_ref[1:2, :] += (jnp.sum(lam_rest * _swap_halves(x_prev), axis=0, keepdims=True)
                           + jnp.sum(lam_0 * _swap_halves(c_in), axis=0, keepdims=True))

    tile = lambda j, b, t: (b * nt + (nt - 1 - t), j)
    chunk3 = lambda j, b, t: (j, 0, 0)
    chunk4 = lambda j, b, t: (j, 0, 0, 0)
    outs, moved = _carried_call(
        body, name="s5_bwd", grid=(nch, B, nt),
        in_specs=[pl.BlockSpec((tm, LANES), tile), pl.BlockSpec((tm, LANES), tile),
                  pl.BlockSpec((None, LANES, W), chunk3), pl.BlockSpec((None, W, LANES), chunk3),
                  pl.BlockSpec((None, 2, seg, W), chunk4), pl.BlockSpec((None, 6, SUBLANES, W), chunk4),
                  pl.BlockSpec((None, 6, SUBLANES, W), chunk4), pl.BlockSpec((1, LANES), lambda j, b, t: (0, j)),
                  pl.BlockSpec((None, None, SUBLANES, W), lambda j, b, t: (j, b * nt + (nt - 1 - t), 0, 0))],
        out_specs=[pl.BlockSpec((tm, LANES), tile), pl.BlockSpec((None, LANES, W), chunk3),
                   pl.BlockSpec((None, W, LANES), chunk3), pl.BlockSpec((None, 2, W), chunk3),
                   pl.BlockSpec((1, LANES), lambda j, b, t: (0, j))],
        out_shape=[jax.ShapeDtypeStruct((N, D), F32), jax.ShapeDtypeStruct((nch, LANES, W), F32),
                   jax.ShapeDtypeStruct((nch, W, LANES), F32), jax.ShapeDtypeStruct((nch, 2, W), F32),
                   jax.ShapeDtypeStruct((1, D), F32)],
        scratch_shapes=[pltpu.VMEM((tm, W), F32), pltpu.VMEM((tm, W), F32)] + [pltpu.VMEM((SUBLANES, W), F32)] * 4
        + [pltpu.VMEM((tm, LANES), F32)] * 2,
        operands=(u, dz, bd, cd, pw, seg_f, seg_b, d_skip, carries), sem=("parallel", "arbitrary", "arbitrary"), carry=carry)
    return (*outs, moved)


ATTN_HEADS = LANES // HEAD_DIM
ATTN_FWD_UNROLL = 4
ATTN_BWD_UNROLL = 2


def _attn_mask(n):
    qi = lax.broadcasted_iota(jnp.int32, (ATTN_BLOCK, 2 * ATTN_BLOCK), 0)
    kj = lax.broadcasted_iota(jnp.int32, (ATTN_BLOCK, 2 * ATTN_BLOCK), 1)
    prev_ok = (kj < ATTN_BLOCK) & (kj >= qi) & (n > 0)
    return prev_ok | ((kj >= ATTN_BLOCK) & (kj - ATTN_BLOCK <= qi))


def _head_lanes(h):
    lane = lax.broadcasted_iota(jnp.int32, (ATTN_BLOCK, LANES), 1)
    return (lane >= h * HEAD_DIM) & (lane < (h + 1) * HEAD_DIM)


def _per_head(cols):
    out = jnp.broadcast_to(cols[-1], (ATTN_BLOCK, LANES))
    for h in range(len(cols) - 2, -1, -1):
        out = jnp.where(_head_lanes(h), jnp.broadcast_to(cols[h], (ATTN_BLOCK, LANES)), out)
    return out


def _only_head(x, h):
    return jnp.where(_head_lanes(h), x, 0.0).astype(BF16)


def _block_rows(tb, dil, nb):
    r = tb // nb
    n = tb % nb
    start = r + dil * ATTN_BLOCK * n
    startp = jnp.where(n > 0, start - dil * ATTN_BLOCK, start)
    return n, pl.ds(start, ATTN_BLOCK, stride=dil), pl.ds(startp, ATTN_BLOCK, stride=dil)


def _attn_fwd(q, k, v, B, carry=None):
    _, S, D3 = q.shape
    D = D3 // 3
    HP = D // LANES
    scale = HEAD_DIM ** -0.5
    n_blocks = S // ATTN_BLOCK
    nbr = len(DILATIONS)
    nt_dims = (((1,), (1,)), ((), ()))

    def branch(dil, q_ref, k_ref, v_ref, acc, m_s, l_s):
        nb = (S // dil) // ATTN_BLOCK

        def blk(tb, _):
            n, rows, rowsp = _block_rows(tb, dil, nb)
            qb = q_ref[rows, :] * scale
            kk = jnp.concatenate([k_ref[rowsp, :], k_ref[rows, :]], axis=0).astype(BF16)
            vv = jnp.concatenate([v_ref[rowsp, :], v_ref[rows, :]], axis=0).astype(BF16)
            ok = _attn_mask(n)
            ms, ls, accs = [], [], []
            for h in range(ATTN_HEADS):
                s = lax.dot_general(_only_head(qb, h), kk, nt_dims, preferred_element_type=F32)
                s = jnp.where(ok, s, NEG)
                mh = jnp.max(s, axis=-1, keepdims=True)
                p = jnp.exp(s - mh)
                ms.append(mh)
                ls.append(jnp.sum(p, axis=-1, keepdims=True))
                accs.append(jnp.dot(p.astype(BF16), vv, preferred_element_type=F32))
            m_s[rows, :] = _per_head(ms)
            l_s[rows, :] = _per_head(ls)
            acc[rows, :] = _per_head(accs)
            return 0

        lax.fori_loop(0, n_blocks, blk, 0, unroll=ATTN_FWD_UNROLL)

    def body(q_ref, k_ref, v_ref, o_ref, lse_ref, *scratch):
        accs, m_ss, l_ss = scratch[:nbr], scratch[nbr:2 * nbr], scratch[2 * nbr:]
        g = pl.program_id(2)
        for gi, dil in enumerate(DILATIONS):
            pl.when(g == gi)(functools.partial(branch, dil, q_ref, k_ref, v_ref, accs[gi], m_ss[gi], l_ss[gi]))

        @pl.when(g == nbr - 1)
        def _():
            def fin(i, _):
                rows = pl.ds(pl.multiple_of(i * ATTN_BLOCK, ATTN_BLOCK), ATTN_BLOCK)
                ms = [m[rows, :] for m in m_ss]
                m_all = functools.reduce(jnp.maximum, ms)
                ws = [jnp.exp(m - m_all) for m in ms]
                den = sum(w * l[rows, :] for w, l in zip(ws, l_ss))
                o_ref[rows, :] = sum(w * a[rows, :] for w, a in zip(ws, accs)) / den
                lse_ref[rows, :] = m_all + jnp.log(den)
                return 0

            lax.fori_loop(0, n_blocks, fin, 0)

    br = pl.BlockSpec((None, S, LANES), lambda b, hp, g: (b, 0, g * HP + hp))
    hd = pl.BlockSpec((None, S, LANES), lambda b, hp, g: (b, 0, hp))
    (o, lse), moved = _carried_call(
        body, name="attn_fwd", grid=(B, HP, nbr), in_specs=[br, br, br], out_specs=[hd, hd],
        out_shape=[jax.ShapeDtypeStruct((B, S, D), F32), jax.ShapeDtypeStruct((B, S, D), F32)],
        scratch_shapes=[pltpu.VMEM((S, LANES), F32)] * (3 * nbr),
        operands=(q, k, v), sem=("parallel", "parallel", "arbitrary"), carry=carry)
    return o, lse, moved


def _attn_bwd(q, k, v, o, lse, do, dk_prev, dv_prev, B, last, carry=None):
    _, S, D3 = q.shape
    D = D3 // 3
    HP = D // LANES
    scale = HEAD_DIM ** -0.5
    n_blocks = S // ATTN_BLOCK
    has_prev = dk_prev is not None
    nt_dims = (((1,), (1,)), ((), ()))
    tn_dims = (((0,), (0,)), ((), ()))

    def branch(dil, q_ref, k_ref, v_ref, lse_ref, do_ref, dq_s, dk_c, dv_c, delta, dk_p, dv_p):
        nb = (S // dil) // ATTN_BLOCK

        def blk(tb, _):
            n, rows, rowsp = _block_rows(tb, dil, nb)
            qb = q_ref[rows, :] * scale
            dob, lb, db = do_ref[rows, :], lse_ref[rows, :], delta[rows, :]
            kk = jnp.concatenate([k_ref[rowsp, :], k_ref[rows, :]], axis=0).astype(BF16)
            vv = jnp.concatenate([v_ref[rowsp, :], v_ref[rows, :]], axis=0).astype(BF16)
            ok = _attn_mask(n)
            dqs = []
            dkk = dvv = None
            for h in range(ATTN_HEADS):
                qh, doh = _only_head(qb, h), _only_head(dob, h)
                lh = lb[:, h * HEAD_DIM:h * HEAD_DIM + 1]
                dlt = db[:, h * HEAD_DIM:h * HEAD_DIM + 1]
                s = lax.dot_general(qh, kk, nt_dims, preferred_element_type=F32)
                p = jnp.where(ok, jnp.exp(s - lh), 0.0)
                dp = lax.dot_general(doh, vv, nt_dims, preferred_element_type=F32)
                ds = (p * (dp - dlt)).astype(BF16)
                dqs.append(jnp.dot(ds, kk, preferred_element_type=F32))
                dk_h = lax.dot_general(ds, qh, tn_dims, preferred_element_type=F32)
                dv_h = lax.dot_general(p.astype(BF16), doh, tn_dims, preferred_element_type=F32)
                dkk = dk_h if dkk is None else dkk + dk_h
                dvv = dv_h if dvv is None else dvv + dv_h
            dq_s[rows, :] = _per_head(dqs) * scale
            dk_p[rowsp, :] = dkk[:ATTN_BLOCK]
            dv_p[rowsp, :] = dvv[:ATTN_BLOCK]
            dk_c[rows, :] = dkk[ATTN_BLOCK:]
            dv_c[rows, :] = dvv[ATTN_BLOCK:]
            return 0

        lax.fori_loop(0, n_blocks, blk, 0, unroll=ATTN_BWD_UNROLL)

    def body(*refs):
        q_ref, k_ref, v_ref, o_ref, lse_ref, do_ref = refs[:6]
        n_in = 8 if has_prev else 6
        dq_ref, dk_ref, dv_ref, delta, dk_p, dv_p, dq_s, dk_c, dv_c = refs[n_in:n_in + 9]
        g = pl.program_id(2)

        @pl.when(g == 0)
        def _():
            def dl(i, _):
                rows = pl.ds(pl.multiple_of(i * ATTN_BLOCK, ATTN_BLOCK), ATTN_BLOCK)
                prod = do_ref[rows, :] * o_ref[rows, :]
                delta[rows, :] = _per_head([jnp.sum(jnp.where(_head_lanes(h), prod, 0.0), axis=-1, keepdims=True)
                                            for h in range(ATTN_HEADS)])
                return 0

            lax.fori_loop(0, n_blocks, dl, 0)

        dk_p[...] = jnp.zeros_like(dk_p)
        dv_p[...] = jnp.zeros_like(dv_p)
        for gi, dil in enumerate(DILATIONS):
            pl.when(g == gi)(functools.partial(branch, dil, q_ref, k_ref, v_ref, lse_ref, do_ref, dq_s, dk_c, dv_c,
                                               delta, dk_p, dv_p))

        def fin(i, _):
            rows = pl.ds(pl.multiple_of(i * ATTN_BLOCK, ATTN_BLOCK), ATTN_BLOCK)
            dk_t = dk_c[rows, :] + dk_p[rows, :]
            dv_t = dv_c[rows, :] + dv_p[rows, :]
            if has_prev:
                dk_t = dk_t + refs[6][rows, :].astype(F32)
                dv_t = dv_t + refs[7][rows, :].astype(F32)
            dq_ref[rows, :] = dq_s[rows, :].astype(dq_ref.dtype)
            dk_ref[rows, :] = dk_t.astype(dk_ref.dtype)
            dv_ref[rows, :] = dv_t.astype(dv_ref.dtype)
            return 0

        lax.fori_loop(0, n_blocks, fin, 0)

    br = pl.BlockSpec((None, S, LANES), lambda b, hp, g: (b, 0, g * HP + hp))
    hd = pl.BlockSpec((None, S, LANES), lambda b, hp, g: (b, 0, hp))
    ins = [q, k, v, o, lse, do] + ([dk_prev, dv_prev] if has_prev else [])
    kv_dtype = BF16 if last else F32
    (dq, dk, dv), moved = _carried_call(
        body, name="attn_bwd", grid=(B, HP, len(DILATIONS)),
        in_specs=[br, br, br, hd, hd, hd] + ([br, br] if has_prev else []), out_specs=[br, br, br],
        out_shape=[jax.ShapeDtypeStruct(q.shape, BF16), jax.ShapeDtypeStruct(q.shape, kv_dtype),
                   jax.ShapeDtypeStruct(q.shape, kv_dtype)],
        scratch_shapes=[pltpu.VMEM((S, LANES), F32)] * 6,
        operands=ins, sem=("parallel", "parallel", "arbitrary"), carry=carry)
    return dq, dk, dv, moved


def _adamw(w, grads, m, v):
    R, C = w.shape
    tr = _div(R, 256, SUBLANES)
    ng = len(grads)
    c1 = 1.0 - ADAM_B1 ** ADAM_STEP
    c2 = 1.0 - ADAM_B2 ** ADAM_STEP

    def body(*refs):
        w_ref, m_ref, v_ref = refs[0], refs[1 + ng], refs[2 + ng]
        d_ref, mo_ref, vo_ref = refs[3 + ng:6 + ng]
        g = refs[1][...]
        if ng == 2:
            g = g + refs[2][...]
            refs[6 + ng][...] = g
        mn = ADAM_B1 * m_ref[...] + (1.0 - ADAM_B1) * g
        vn = ADAM_B2 * v_ref[...] + (1.0 - ADAM_B2) * (g * g)
        d_ref[...] = -ADAM_LR * ((mn / c1) / (jnp.sqrt(vn / c2) + ADAM_EPS) + ADAM_WD * w_ref[...])
        mo_ref[...] = mn
        vo_ref[...] = vn

    blk = pl.BlockSpec((tr, C), lambda i: (i, 0))
    n_out = 3 + (ng == 2)
    outs = pl.pallas_call(
        body, name="adamw", grid=(R // tr,), in_specs=[blk] * (3 + ng), out_specs=[blk] * n_out,
        out_shape=[jax.ShapeDtypeStruct((R, C), F32)] * n_out, compiler_params=_params("parallel"),
    )(w, *grads, m, v)
    return (outs[3] if ng == 2 else grads[0],) + tuple(outs[:3])


def _sum_shards(recv):
    n, R, C = recv.shape
    tr = _div(R, 256, SUBLANES if recv.dtype == F32 else 2 * SUBLANES)

    def body(r_ref, o_ref):
        s = r_ref[0].astype(F32)
        for i in range(1, n):
            s = s + r_ref[i].astype(F32)
        o_ref[...] = s

    return pl.pallas_call(
        body, name="sum_shards", grid=(R // tr,), in_specs=[pl.BlockSpec((n, tr, C), lambda i: (0, i, 0))],
        out_specs=pl.BlockSpec((tr, C), lambda i: (i, 0)), out_shape=jax.ShapeDtypeStruct((R, C), F32),
        compiler_params=_params("parallel"),
    )(recv)


N_DEV = 8
N_CHIPS = 4


def _all_gather_small(x, carry=None):
    m_per, n = x.shape

    def body(x_ref, out_ref, send_sems, recv_sems, local_sem):
        cx, cy, cc = _coords()
        me, sibling = (cx, cy, cc), (cx, cy, 1 - cc)
        chips = [(1 - cx, cy), (cx, 1 - cy), (1 - cx, 1 - cy)]

        def rows(px, py, pc):
            return out_ref.at[pl.ds((4 * px + 2 * py + pc) * m_per, m_per), :]

        def copy(k, block, to, src=None):
            return pltpu.make_async_remote_copy(
                src_ref=rows(*block) if src is None else src, dst_ref=rows(*block), send_sem=send_sems.at[k],
                recv_sem=recv_sems.at[k], device_id=to, device_id_type=MESH)

        mine = pltpu.make_async_copy(x_ref, rows(*me), local_sem)
        mine.start()
        first = [copy(0, me, sibling, src=x_ref)]
        first += [copy(1 + j, me, (*chip, cc), src=x_ref) for j, chip in enumerate(chips)]
        for cp in first:
            cp.start()
        passed = [copy(4 + j, (*chip, cc), sibling) for j, chip in enumerate(chips)]
        for j, chip in enumerate(chips):
            copy(1 + j, (*chip, cc), me).wait_recv()
            passed[j].start()
        copy(0, sibling, me).wait_recv()
        for j, chip in enumerate(chips):
            copy(4 + j, (*chip, 1 - cc), me).wait_recv()
        for cp in first + passed:
            cp.wait_send()
        mine.wait()

    (out,), moved = _carried_call(
        body, name="all_gather_small", grid=(1,), out_shape=[jax.ShapeDtypeStruct((N_DEV * m_per, n), x.dtype)],
        in_specs=[pl.BlockSpec(memory_space=pltpu.VMEM)], out_specs=[pl.BlockSpec(memory_space=pltpu.VMEM)],
        scratch_shapes=[pltpu.SemaphoreType.DMA((7,)), pltpu.SemaphoreType.DMA((7,)), pltpu.SemaphoreType.DMA],
        operands=(x,), sem=("arbitrary",), carry=carry)
    return out, moved


def _layer_moves(kind, arrays_from, arrays_to, pieces, layer_major=()):
    used = sorted({w for w, _ in pieces})
    pos = {w: i for i, w in enumerate(used)}
    gather = kind == "gather"

    def half(ref, c):
        rows = ref.shape[0] // 2
        return ref.at[pl.ds(c * rows, rows), :]

    def slot(d, w, chip, l):
        return d.at[l, chip] if w in layer_major else d.at[chip, l]

    def plan(src_refs, dst_refs, me):
        cx, cy, cc = me
        mine = 2 * cx + cy
        remote, local = [], []
        for w, l in pieces:
            s, d = src_refs[pos[w]], dst_refs[pos[w]]
            for px, py in _other_chips(cx, cy):
                if gather:
                    remote.append((half(s.at[l], cc), half(slot(d, w, mine, l), cc), (px, py, cc)))
                else:
                    remote.append((s.at[2 * px + py, l], d.at[mine, l], (px, py, cc)))
            local.append((s.at[l], slot(d, w, mine, l)) if gather else (s.at[mine, l], d.at[mine, l]))
        return remote, local

    def onward(src_refs, dst_refs, me):
        cx, cy, cc = me
        moves = []
        for w, l in pieces:
            d = dst_refs[pos[w]]
            for px, py in _other_chips(cx, cy):
                landed = half(slot(d, w, 2 * px + py, l), cc)
                moves.append((landed, landed, (cx, cy, 1 - cc)))
        return moves

    n = 3 * len(pieces)
    carry = _Carry([arrays_from[w] for w in used], [arrays_to[w] for w in used], plan, n, len(pieces),
                   onward if gather else None, n if gather else 0)
    return carry, used


def _swap_with_sibling(sums):
    def plan(src_refs, dst_refs, me):
        cx, cy, cc = me
        return [(s, d, (cx, cy, 1 - cc)) for s, d in zip(src_refs, dst_refs)], []

    return _Carry(sums, [lax.empty(s.shape, s.dtype) for s in sums], plan, len(sums), 0)


def _pack(arrs, width):
    parts, layout, row = [], [], 0
    for a in arrs:
        flat = a.reshape(-1).astype(F32)
        rows = -(-flat.shape[0] // (width * SUBLANES)) * SUBLANES
        parts.append(jnp.pad(flat, (0, rows * width - flat.shape[0])).reshape(rows, width))
        layout.append((row, rows, a.shape))
        row += rows
    pad = -row % (8 * SUBLANES) if row > 8 * SUBLANES else 0
    if pad:
        parts.append(jnp.zeros((pad, width), F32))
    return jnp.concatenate(parts, axis=0), layout, row + pad


def _unpack(buf, layout, idx):
    row, rows, shape = layout[idx]
    size = math.prod(shape)
    return buf[row:row + rows].reshape(-1)[:size].reshape(shape)


def kernel(x, c, ln_g, ada_w, ada_b, ssm_lam_re, ssm_lam_im, ssm_log_dt, ssm_b_re, ssm_b_im, ssm_c_re, ssm_c_im, ssm_d, ssm_w_glu, kv_g, kv_ada_w, kv_ada_b, w_kv, attn_w_q, attn_w_o, mlp_w1, mlp_w2, final_g, loss_target, m_ln_g, m_ada_w, m_ada_b, m_ssm_lam_re, m_ssm_lam_im, m_ssm_log_dt, m_ssm_b_re, m_ssm_b_im, m_ssm_c_re, m_ssm_c_im, m_ssm_d, m_ssm_w_glu, m_kv_g, m_kv_ada_w, m_kv_ada_b, m_w_kv, m_attn_w_q, m_attn_w_o, m_mlp_w1, m_mlp_w2, m_final_g, v_ln_g, v_ada_w, v_ada_b, v_ssm_lam_re, v_ssm_lam_im, v_ssm_log_dt, v_ssm_b_re, v_ssm_b_im, v_ssm_c_re, v_ssm_c_im, v_ssm_d, v_ssm_w_glu, v_kv_g, v_kv_ada_w, v_kv_ada_b, v_w_kv, v_attn_w_q, v_attn_w_o, v_mlp_w1, v_mlp_w2, v_final_g):
    B, S, D = x.shape
    N = B * S
    depth = ln_g.shape[0]
    n_a = ssm_w_glu.shape[0]
    n_b = attn_w_q.shape[0]
    FF = mlp_w1.shape[2] * N_CHIPS
    cx, cy, cc = _coords()
    chip = 2 * cx + cy
    dev = 4 * cx + 2 * cy + cc
    n_ex = N_DEV * B
    ada_cols = ada_w.shape[-1]
    kv_cols = kv_ada_w.shape[-1]

    GLU, KV, Q, O, W1, W2 = range(6)
    shards = [ssm_w_glu.astype(BF16), w_kv.astype(BF16)[None], attn_w_q.astype(BF16), attn_w_o.astype(BF16),
              mlp_w1.astype(BF16), mlp_w2.astype(BF16)]
    row_sharded = (O, W2)
    wg = [lax.empty((s.shape[0], N_CHIPS) + s.shape[1:] if w in row_sharded else (N_CHIPS,) + s.shape, BF16)
          for w, s in enumerate(shards)]

    def whole_rows(w):
        L, _, R, C = wg[w].shape
        return wg[w].reshape(1, L, N_CHIPS * R, C)

    def fetch(pieces):
        return _layer_moves("gather", shards, wg, pieces, layer_major=row_sharded)

    def landed(arrays, used, moved):
        for w, a in zip(used, moved):
            arrays[w] = a

    fetch_with = {l: [(W1, l), (W2, l)] for l in range(depth)}
    fetch_with[0] += [(GLU, l) for l in range(n_a)] + [(Q, 0), (O, 0)]
    fetch_with[n_a - 1] += [(KV, 0)]
    for j in range(1, n_b):
        fetch_with[n_a + j - 1] += [(Q, j), (O, j)]

    c_pack, c_layout, _ = _pack([c], D)
    c_all_buf, _ = _all_gather_small(c_pack)
    c_rows = c_pack.shape[0]
    c_all = jnp.concatenate([_unpack(c_all_buf[d * c_rows:(d + 1) * c_rows], c_layout, 0) for d in range(N_DEV)], axis=0)
    sc_all = jax.nn.silu(c_all).astype(BF16)
    n_mod = depth * 2
    ada_w8 = ada_w.reshape(n_mod, 1, D, ada_cols)
    ada_b_row = ada_b.reshape(1, n_mod * ada_cols)
    mod_local = _mm("ada_fwd", sc_all, ada_w8, mode="nn", M=n_ex, N=n_mod * ada_cols, K=D, b_lay="cs", b_ns=n_mod,
                    epi=_add, extras=[("n", ada_b_row)])
    kv_ada_b_local = lax.dynamic_slice(kv_ada_b.reshape(N_CHIPS, kv_cols), (chip, 0), (1, kv_cols))
    kvmod_local = _mm("ada_fwd", sc_all, _as4(kv_ada_w), mode="nn", M=n_ex, N=kv_cols, K=D, epi=_add,
                      extras=[("n", kv_ada_b_local)])
    mod_pack, mod_layout, mod_rows = _pack([mod_local, kvmod_local, ln_g, ssm_d], D)
    mod_buf, _ = _all_gather_small(mod_pack)

    def from_chip(j, idx):
        d = 2 * j
        return _unpack(mod_buf[d * mod_rows:(d + 1) * mod_rows], mod_layout, idx)

    my_rows = lambda a: lax.dynamic_slice_in_dim(a, dev * B, B, axis=0)
    mods = jnp.concatenate([my_rows(from_chip(j, 0)).reshape(B, n_mod, ada_cols) for j in range(N_CHIPS)], axis=2)
    kvmod = jnp.concatenate([my_rows(from_chip(j, 1)) for j in range(N_CHIPS)], axis=1)
    ln_g_full = jnp.concatenate([from_chip(j, 2) for j in range(N_CHIPS)], axis=2)
    ssm_d_full = jnp.concatenate([from_chip(j, 3) for j in range(N_CHIPS)], axis=1)

    def mod3(l, s):
        mrow = mods[:, l * 2 + s]
        return [mrow[:, i * D:(i + 1) * D].reshape(B, 1, D) for i in range(3)]

    kv_shift, kv_scale = kvmod[:, :D].reshape(B, 1, D), kvmod[:, D:].reshape(B, 1, D)

    s5_tabs = []
    for l in range(n_a):
        prm = (ssm_lam_re[l], ssm_lam_im[l], ssm_log_dt[l], ssm_b_re[l], ssm_b_im[l], ssm_c_re[l], ssm_c_im[l])
        (bd, cd, _, _), disc_vjp = jax.vjp(_s5_discretize, *prm)
        pw, seg_f, seg_b = _s5_scan_coefs(ssm_lam_re[l], ssm_lam_im[l], ssm_log_dt[l], S5_SEG)
        s5_tabs.append((bd.astype(BF16), cd.astype(BF16), pw, seg_f, seg_b, disc_vjp))

    h = x.reshape(N, D)
    saved = []
    k_all = v_all = None
    shift, scale, gate = mod3(0, 0)
    u = _normmod(h, ln_g_full[0, 0].reshape(1, D), scale, shift, B)
    for l in range(depth):
        sv = {}
        sv["h0"], sv["scale0"], sv["gate0"], sv["u0"] = h, scale, gate, u
        shift1, scale1, gate1 = mod3(l, 1)
        norm1 = [("n", ln_g_full[l, 1].reshape(1, D)), ("ex", scale1), ("ex", shift1)]
        carry, used = fetch(fetch_with[l])
        if l < n_a:
            bd, cd, pw, seg_f, _, _ = s5_tabs[l]
            z, carries, moved = _s5_fwd(u, bd, cd, pw, seg_f, ssm_d_full[l].reshape(1, D), B, carry)
            landed(wg, used, moved)
            zz = _mm("glu_proj", z, wg[GLU], mode="nn", M=N, N=2 * D, K=D, b_lay="cs", b_l=l, b_ns=N_CHIPS)
            y, h, u = _glu_residual_norm(zz, h, gate, ln_g_full[l, 1].reshape(1, D), scale1, shift1, B)
            sv["z"], sv["carries"], sv["zz"] = z, carries, zz
        else:
            j = l - n_a
            q = _mm("q_proj", u, wg[Q], mode="nn", M=N, N=3 * D, K=D, b_lay="cs", b_l=j, b_ns=N_CHIPS)
            q3 = q.reshape(B, S, 3 * D)
            o, lse, moved = _attn_fwd(q3, k_all, v_all, B, carry)
            landed(wg, used, moved)
            o2 = o.reshape(N, D)
            y, h, u = _mm("o_proj", o2, whole_rows(O), mode="nn", M=N, N=D, K=D, b_l=j, out_dtype=(BF16, F32, BF16),
                          epi=_gated_residual_norm, extras=[("mn", h), ("ex", gate)] + norm1, rows_per_ex=S)
            sv["q"], sv["o"], sv["lse"] = q3, o, lse
        sv["y0"] = y
        sv["h1"], sv["scale1"], sv["gate1"], sv["u1"] = h, scale1, gate1, u
        r = _mm("mlp_up", u, wg[W1], mode="nn", M=N, N=FF, K=D, b_lay="cs", b_l=l, b_ns=N_CHIPS, out_dtype=BF16, epi=_relu2)
        if l + 1 < depth:
            shift, scale, gate = mod3(l + 1, 0)
            y, h, u = _mm("mlp_down", r, whole_rows(W2), mode="nn", M=N, N=D, K=FF, b_l=l, tk=2048, tm=512,
                          out_dtype=(BF16, F32, BF16), epi=_gated_residual_norm, rows_per_ex=S,
                          extras=[("mn", h), ("ex", gate1), ("n", ln_g_full[l + 1, 0].reshape(1, D)), ("ex", scale), ("ex", shift)])
        else:
            y, h = _mm("mlp_down", r, whole_rows(W2), mode="nn", M=N, N=D, K=FF, b_l=l, tk=2048,
                       out_dtype=(BF16, F32), epi=_gated_residual, extras=[("mn", h), ("ex", gate1)], rows_per_ex=S)
        sv["r"], sv["y1"] = r, y
        saved.append(sv)
        if l == n_a - 1:
            h_kv = h
            u_kv = _normmod(h, kv_g.reshape(1, D), kv_scale, kv_shift, B)
            half = N_CHIPS // 2
            k_all = _mm("kv_proj", u_kv, wg[KV], mode="nn", M=N, N=3 * D, K=D, b_lay="cs", b_s0=0, b_ns=half).reshape(B, S, 3 * D)
            v_all = _mm("kv_proj", u_kv, wg[KV], mode="nn", M=N, N=3 * D, K=D, b_lay="cs", b_s0=half, b_ns=half).reshape(B, S, 3 * D)

    loss_buf, dh, d_final_g = _loss_head(h, final_g.reshape(1, D), loss_target.reshape(N, D))
    loss = lax.psum(loss_buf[0, 0], ("x", "y", "c"))

    dg = [lax.empty((N_CHIPS,) + s.shape, BF16) for s in shards]
    recv = [lax.empty((N_CHIPS,) + s.shape, BF16) for s in shards]

    def send(pieces):
        return _layer_moves("scatter", dg, recv, pieces)

    send_with = {l: [(W1, l), (W2, l)] for l in range(depth)}
    for l in range(n_a):
        send_with[l] += [(GLU, l)]
    for j in range(n_b):
        send_with[n_a + j] += [(O, j)]
        send_with[n_a + j - 1] += [(Q, j)]
    send_with[n_a - 1] += [(KV, 0)]
    d_ln_g = [[None, None] for _ in range(depth)]
    d_mods = [[None, None] for _ in range(depth)]
    d_s5 = [None] * n_a
    dk_acc = dv_acc = None
    half = N_CHIPS // 2

    def tn_grad(name, a, d, into, l, Mr, Nc, lay, s0=0, ns=N_CHIPS):
        return _mm(name, a, _as4(d), mode="tn", M=Mr, N=Nc, K=N, b_lay="cs", out_dtype=BF16, out_lay=lay,
                   out4_shape=into.shape, out_into=into, out_l=l, out_s0=s0, out_ns=ns, tk=2048)

    dy, d_gate1 = _residual_bwd(dh, saved[-1]["gate1"], saved[-1]["y1"], B)
    for l in reversed(range(depth)):
        sv = saved[l]
        dg[W2] = tn_grad("mlp_down_dw", sv["r"], dy, dg[W2], l, FF, D, "rs")
        da = _mm("mlp_down_dx", dy, whole_rows(W2), mode="nt", M=N, N=FF, K=D, b_l=l, out_dtype=BF16,
                 epi=_relu2_bwd, extras=[("mn", sv["r"])])
        dg[W1] = tn_grad("mlp_up_dw", sv["u1"], da, dg[W1], l, D, FF, "cs")
        du = _mm("mlp_up_dx", da, wg[W1], mode="nt", M=N, N=D, K=FF, b_lay="cs", b_l=l, b_ns=N_CHIPS)
        dh, dgv, d_scale1, d_shift1, dy, d_gate0 = _normmod_bwd(du, sv["h1"], ln_g_full[l, 1].reshape(1, D), sv["scale1"],
                                                                dh, B, below=(sv["y0"], sv["gate0"]))
        d_ln_g[l][1] = dgv
        d_mods[l][1] = jnp.concatenate([d_shift1, d_scale1, d_gate1], axis=2)
        if l < n_a:
            bd, cd, pw, seg_f, seg_b, disc_vjp = s5_tabs[l]
            dzz = _glu_bwd(dy, sv["zz"])
            dg[GLU] = tn_grad("glu_proj_dw", sv["z"], dzz, dg[GLU], l, D, 2 * D, "cs")
            dz = _mm("glu_proj_dx", dzz, wg[GLU], mode="nt", M=N, N=D, K=2 * D, b_lay="cs", b_l=l, b_ns=N_CHIPS)
            carry, used = send(send_with[l])
            du, d_bd, d_cd, d_a2, d_dskip, moved = _s5_bwd(sv["u0"], dz, bd, cd, pw, seg_f, seg_b,
                                                           ssm_d_full[l].reshape(1, D), sv["carries"], B, carry)
            landed(recv, used, moved)
            d_are = (d_a2[:, 0, :CHUNK_STATE] + d_a2[:, 0, CHUNK_STATE:]).reshape(-1, SSM_STATE)
            d_aim = (d_a2[:, 1, CHUNK_STATE:] - d_a2[:, 1, :CHUNK_STATE]).reshape(-1, SSM_STATE)
            d_s5[l] = disc_vjp((d_bd, d_cd, d_are, d_aim)) + (d_dskip,)
        else:
            j = l - n_a
            dg[O] = tn_grad("o_proj_dw", sv["o"].reshape(N, D), dy, dg[O], j, D, D, "rs")
            do = _mm("o_proj_dx", dy, whole_rows(O), mode="nt", M=N, N=D, K=D, b_l=j)
            carry, used = send(send_with[l])
            dq, dk_acc, dv_acc, moved = _attn_bwd(sv["q"], k_all, v_all, sv["o"], sv["lse"], do.reshape(B, S, D),
                                                  dk_acc, dv_acc, B, l == n_a, carry)
            landed(recv, used, moved)
            dq2 = dq.reshape(N, 3 * D)
            dg[Q] = tn_grad("q_proj_dw", sv["u0"], dq2, dg[Q], j, D, 3 * D, "cs")
            du = _mm("q_proj_dx", dq2, wg[Q], mode="nt", M=N, N=D, K=3 * D, b_lay="cs", b_l=j, b_ns=N_CHIPS)
        below = (saved[l - 1]["y1"], saved[l - 1]["gate1"]) if l > 0 else None
        dh, dgv, d_scale0, d_shift0, dy, d_gate1 = _normmod_bwd(du, sv["h0"], ln_g_full[l, 0].reshape(1, D), sv["scale0"],
                                                                dh, B, below=None if l == n_a else below)
        d_ln_g[l][0] = dgv
        d_mods[l][0] = jnp.concatenate([d_shift0, d_scale0, d_gate0], axis=2)
        if l == n_a:
            dk2, dv2 = dk_acc.reshape(N, 3 * D), dv_acc.reshape(N, 3 * D)
            dg[KV] = tn_grad("kv_proj_dw", u_kv, dk2, dg[KV], 0, D, 3 * D, "cs", s0=0, ns=half)
            dg[KV] = tn_grad("kv_proj_dw", u_kv, dv2, dg[KV], 0, D, 3 * D, "cs", s0=half, ns=half)
            du_kv = _mm("kv_proj_dx", dk2, wg[KV], mode="nt", M=N, N=D, K=3 * D, b_lay="cs", b_s0=0, b_ns=half)
            du_kv = _mm("kv_proj_dx", dv2, wg[KV], mode="nt", M=N, N=D, K=3 * D, b_lay="cs", b_s0=half, b_ns=half,
                        epi=_add, extras=[("mn", du_kv)])
            dh, d_kv_g, d_kv_scale, d_kv_shift, dy, d_gate1 = _normmod_bwd(du_kv, h_kv, kv_g.reshape(1, D), kv_scale, dh, B,
                                                                           below=below)
    grad_x = dh.reshape(B, S, D)

    own = [_sum_shards(r.reshape(N_CHIPS, -1, r.shape[-1])) for r in recv]

    d_kvmod = jnp.concatenate([d_kv_shift, d_kv_scale], axis=2).reshape(B, 2 * D)
    d_mod_all = jnp.concatenate([d_mods[l][s].reshape(B, 3 * D) for l in range(depth) for s in range(2)], axis=1)
    small = [
        d_mod_all, d_kvmod,
        jnp.stack([jnp.stack([d_ln_g[l][0].reshape(D), d_ln_g[l][1].reshape(D)]) for l in range(depth)]),
        jnp.stack([d_s5[l][0] for l in range(n_a)]), jnp.stack([d_s5[l][1] for l in range(n_a)]),
        jnp.stack([d_s5[l][2] for l in range(n_a)]),
        jnp.stack([d_s5[l][3] for l in range(n_a)]), jnp.stack([d_s5[l][4] for l in range(n_a)]),
        jnp.stack([d_s5[l][5] for l in range(n_a)]), jnp.stack([d_s5[l][6] for l in range(n_a)]),
        jnp.stack([d_s5[l][7].reshape(D) for l in range(n_a)]),
        d_kv_g.reshape(D), d_final_g.reshape(D),
    ]
    small_pack, small_layout, small_rows = _pack(small, D)
    small_buf, other = _all_gather_small(small_pack, _swap_with_sibling(own))
    small_sum = _sum_shards(small_buf.reshape(N_DEV, small_rows, D))
    red = lambda idx: _unpack(small_sum, small_layout, idx)
    per_dev = lambda idx: jnp.concatenate(
        [_unpack(small_buf[d * small_rows:(d + 1) * small_rows], small_layout, idx) for d in range(N_DEV)], axis=0)

    dm_all = per_dev(0).reshape(n_ex, n_mod, 3 * D)
    dm_cols = lax.dynamic_slice_in_dim(dm_all, chip * ada_cols, ada_cols, axis=2).reshape(n_ex, n_mod * ada_cols)
    g_ada_w = _mm("ada_dw", sc_all, _as4(dm_cols), mode="tn", M=D, N=n_mod * ada_cols, K=n_ex, b_lay="cs",
                  out_lay="cs", out4_shape=(n_mod, 1, D, ada_cols), out_ns=n_mod).reshape(ada_w.shape)
    dkvm_all = per_dev(1)
    dkvm_cols = lax.dynamic_slice_in_dim(dkvm_all, chip * kv_cols, kv_cols, axis=1)
    g_kv_ada_w = _mm("ada_dw", sc_all, _as4(dkvm_cols), mode="tn", M=D, N=kv_cols, K=n_ex, b_lay="cs")
    g_ada_b_full = (red(0)[0] + red(0)[1]).reshape(depth, 2, 3 * D) if B == 2 else jnp.sum(red(0), axis=0).reshape(depth, 2, 3 * D)
    g_ada_b = lax.dynamic_slice_in_dim(g_ada_b_full, chip * ada_cols, ada_cols, axis=2)
    g_kv_ada_b = red(1)[0] + red(1)[1] if B == 2 else jnp.sum(red(1), axis=0)
    g_ln_g = lax.dynamic_slice_in_dim(red(2), chip * (D // N_CHIPS), D // N_CHIPS, axis=2)
    g_ssm_d = lax.dynamic_slice_in_dim(red(10), chip * (D // N_CHIPS), D // N_CHIPS, axis=1)
    small_grads = {
        "ln_g": g_ln_g, "ada_b": g_ada_b, "ssm_lam_re": red(3), "ssm_lam_im": red(4), "ssm_log_dt": red(5),
        "ssm_b_re": red(6), "ssm_b_im": red(7), "ssm_c_re": red(8), "ssm_c_im": red(9), "ssm_d": g_ssm_d,
        "kv_g": red(11), "kv_ada_b": g_kv_ada_b, "final_g": red(12),
    }
    small_w = {"ln_g": (ln_g, m_ln_g, v_ln_g), "ada_b": (ada_b, m_ada_b, v_ada_b),
               "ssm_lam_re": (ssm_lam_re, m_ssm_lam_re, v_ssm_lam_re), "ssm_lam_im": (ssm_lam_im, m_ssm_lam_im, v_ssm_lam_im),
               "ssm_log_dt": (ssm_log_dt, m_ssm_log_dt, v_ssm_log_dt), "ssm_b_re": (ssm_b_re, m_ssm_b_re, v_ssm_b_re),
               "ssm_b_im": (ssm_b_im, m_ssm_b_im, v_ssm_b_im), "ssm_c_re": (ssm_c_re, m_ssm_c_re, v_ssm_c_re),
               "ssm_c_im": (ssm_c_im, m_ssm_c_im, v_ssm_c_im), "ssm_d": (ssm_d, m_ssm_d, v_ssm_d),
               "kv_g": (kv_g, m_kv_g, v_kv_g), "kv_ada_b": (kv_ada_b, m_kv_ada_b, v_kv_ada_b),
               "final_g": (final_g, m_final_g, v_final_g)}
    names = list(small_w)
    wp, lay_w, _ = _pack([small_w[n][0] for n in names], D)
    gp, _, _ = _pack([small_grads[n] for n in names], D)
    mp, _, _ = _pack([small_w[n][1] for n in names], D)
    vp, _, _ = _pack([small_w[n][2] for n in names], D)
    _, d_p, m_p, v_p = _adamw(wp, [gp], mp, vp)
    upd = {n: (small_grads[n].reshape(small_w[n][0].shape), _unpack(d_p, lay_w, i), _unpack(m_p, lay_w, i), _unpack(v_p, lay_w, i))
           for i, n in enumerate(names)}

    def big(w, m, v, g_own, g_other=None):
        C = w.shape[-1]
        gs = [g_own.reshape(-1, C)] + ([g_other.reshape(-1, C)] if g_other is not None else [])
        return tuple(t.reshape(w.shape) for t in _adamw(w.reshape(-1, C), gs, m.reshape(-1, C), v.reshape(-1, C)))

    upd["ssm_w_glu"] = big(ssm_w_glu, m_ssm_w_glu, v_ssm_w_glu, own[0], other[0])
    upd["w_kv"] = big(w_kv, m_w_kv, v_w_kv, own[1], other[1])
    upd["attn_w_q"] = big(attn_w_q, m_attn_w_q, v_attn_w_q, own[2], other[2])
    upd["attn_w_o"] = big(attn_w_o, m_attn_w_o, v_attn_w_o, own[3], other[3])
    upd["mlp_w1"] = big(mlp_w1, m_mlp_w1, v_mlp_w1, own[4], other[4])
    upd["mlp_w2"] = big(mlp_w2, m_mlp_w2, v_mlp_w2, own[5], other[5])
    upd["ada_w"] = big(ada_w, m_ada_w, v_ada_w, g_ada_w)
    upd["kv_ada_w"] = big(kv_ada_w, m_kv_ada_w, v_kv_ada_w, g_kv_ada_w)

    order = ["ln_g", "ada_w", "ada_b", "ssm_lam_re", "ssm_lam_im", "ssm_log_dt", "ssm_b_re", "ssm_b_im", "ssm_c_re",
             "ssm_c_im", "ssm_d", "ssm_w_glu", "kv_g", "kv_ada_w", "kv_ada_b", "w_kv", "attn_w_q", "attn_w_o", "mlp_w1",
             "mlp_w2", "final_g"]
    return (loss, grad_x, *[upd[n][0] for n in order], *[upd[n][1] for n in order], *[upd[n][2] for n in order],
            *[upd[n][3] for n in order])
```

```python
import functools
import math

import jax
import jax.numpy as jnp
from jax import lax
from jax.experimental import pallas as pl
from jax.experimental.pallas import tpu as pltpu

F32 = jnp.float32
BF16 = jnp.bfloat16
MESH = pl.DeviceIdType.MESH

EPS = 1e-6
NEG = -1e30
SSM_GROUP = 16
SSM_STATE = 64
HEAD_DIM = 64
ATTN_BLOCK = 128
DILATIONS = (1, 4, 16)
ADAM_LR, ADAM_B1, ADAM_B2, ADAM_EPS, ADAM_WD, ADAM_STEP = 0.001, 0.9, 0.999, 1e-08, 0.01, 10

LANES = 128
SUBLANES = 8
CHUNK_GROUPS = LANES // SSM_GROUP
CHUNK_STATE = CHUNK_GROUPS * SSM_STATE
VMEM_LIMIT = 56 * 1024 * 1024


def _div(dim, pref, mult):
    t = min(pref, dim) // mult * mult
    while t >= mult:
        if dim % t == 0:
            return t
        t -= mult
    return dim


def _params(*sem):
    return pltpu.CompilerParams(dimension_semantics=sem, vmem_limit_bytes=VMEM_LIMIT)


def _coords():
    return lax.axis_index("x"), lax.axis_index("y"), lax.axis_index("c")


def _other_chips(cx, cy):
    return [(1 - cx, cy), (cx, 1 - cy), (1 - cx, 1 - cy)]


class _Carry:
    def __init__(self, srcs, dsts, plan, n_remote, n_local, onward=None, n_onward=0):
        self.srcs, self.dsts, self.plan, self.n_remote, self.n_local = list(srcs), list(dsts), plan, n_remote, n_local
        self.onward, self.n_onward = onward, n_onward


def _carried_call(body, *, name, grid, in_specs, out_specs, out_shape, scratch_shapes, operands, sem, carry=None):
    if carry is None:
        outs = pl.pallas_call(body, name=name, grid=grid, in_specs=in_specs, out_specs=out_specs, out_shape=out_shape,
                              scratch_shapes=scratch_shapes, compiler_params=_params(*sem))(*operands)
        return list(outs), []
    n_in, n_out, n_scr = len(in_specs), len(out_specs), len(scratch_shapes)
    ns, nd = len(carry.srcs), len(carry.dsts)

    def wrapped(*refs):
        base_in, src_refs = refs[:n_in], refs[n_in:n_in + ns]
        o0 = n_in + ns + nd
        base_out, dst_refs = refs[o0:o0 + n_out], refs[o0 + n_out:o0 + n_out + nd]
        s0 = o0 + n_out + nd
        base_scr = refs[s0:s0 + n_scr]
        send_sems, recv_sems, local_sems = refs[s0 + n_scr:]
        pids = [pl.program_id(a) for a in range(len(grid))]
        first = functools.reduce(jnp.logical_and, [p == 0 for p in pids])
        last = functools.reduce(jnp.logical_and, [p == g - 1 for p, g in zip(pids, grid)])

        def remote_copies(moves, k0):
            return [pltpu.make_async_remote_copy(src_ref=s, dst_ref=d, send_sem=send_sems.at[k0 + i], recv_sem=recv_sems.at[k0 + i],
                                                 device_id=peer, device_id_type=MESH) for i, (s, d, peer) in enumerate(moves)]

        def copies():
            remote, local = carry.plan(src_refs, dst_refs, _coords())
            return remote_copies(remote, 0), [pltpu.make_async_copy(s, d, local_sems.at[i]) for i, (s, d) in enumerate(local)]

        @pl.when(first)
        def _():
            remote, local = copies()
            for cp in local + remote:
                cp.start()

        body(*base_in, *base_out, *base_scr)

        @pl.when(last)
        def _():
            remote, local = copies()
            for cp in remote:
                cp.wait_send()
                cp.wait_recv()
            for cp in local:
                cp.wait()
            if carry.onward is not None:
                second = remote_copies(carry.onward(src_refs, dst_refs, _coords()), carry.n_remote)
                for cp in second:
                    cp.start()
                for cp in second:
                    cp.wait_send()
                    cp.wait_recv()

    anyspec = pl.BlockSpec(memory_space=pl.ANY)
    outs = pl.pallas_call(
        wrapped, name=name, grid=grid, in_specs=list(in_specs) + [anyspec] * (ns + nd),
        out_specs=list(out_specs) + [anyspec] * nd,
        out_shape=list(out_shape) + [jax.ShapeDtypeStruct(d.shape, d.dtype) for d in carry.dsts],
        scratch_shapes=list(scratch_shapes) + [pltpu.SemaphoreType.DMA((carry.n_remote + carry.n_onward,)),
                                               pltpu.SemaphoreType.DMA((carry.n_remote + carry.n_onward,)),
                                               pltpu.SemaphoreType.DMA((max(carry.n_local, 1),))],
        input_output_aliases={n_in + ns + i: n_out + i for i in range(nd)},
        compiler_params=_params(*(["arbitrary"] * len(grid))),
    )(*operands, *carry.srcs, *carry.dsts)
    return list(outs[:n_out]), list(outs[n_out:])


def _mm(name, a, b4, *, mode, M, N, K, b_lay="cs", b_l=0, b_s0=0, b_ns=1, out_dtype=F32, out_lay=None, out4_shape=None,
        out_into=None, out_l=0, out_s0=0, out_ns=1, epi=None, extras=(), rows_per_ex=None, tm=1024, tn=1024, tk=1024):
    _, _, bR, bC = b4.shape
    tm = _div(M, tm, SUBLANES if M % 16 else 16)
    brows, bcols = (N, K) if mode == "nt" else (K, N)
    if b_lay == "cs":
        assert bR == brows and bC * b_ns == bcols, (name, b4.shape, brows, bcols)
    else:
        assert bC == bcols and bR * b_ns == brows, (name, b4.shape, brows, bcols)
    n_lim = N
    k_lim = K
    if mode == "nt":
        if b_lay == "cs":
            k_lim = bC
        else:
            n_lim = bR
    else:
        if b_lay == "cs":
            n_lim = bC
        else:
            k_lim = bR
    if out_lay == "cs":
        oR, oC = out4_shape[2], out4_shape[3]
        assert oR == M and oC * out_ns == N, (name, out4_shape, M, N)
        n_lim = math.gcd(n_lim, oC)
    elif out_lay == "rs":
        oR, oC = out4_shape[2], out4_shape[3]
        assert oC == N and oR * out_ns == M, (name, out4_shape, M, N)
        tm = _div(oR, tm, SUBLANES)
    tn = _div(n_lim, tn, LANES)
    tk = _div(k_lim, tk, LANES if mode != "tn" else SUBLANES)
    if mode == "tn":
        tk = _div(k_lim, tk, 16) if k_lim % 16 == 0 else tk
    nk = K // tk
    grid = (M // tm, N // tn, nk)

    if mode == "tn":
        a_spec = pl.BlockSpec((tk, tm), lambda i, j, k: (k, i))
    else:
        a_spec = pl.BlockSpec((tm, tk), lambda i, j, k: (i, k))

    def b_index(ri, ci, br, bc):
        if b_lay == "cs":
            per = bC // bc
            return (b_s0 + ci // per, b_l, ri, ci % per)
        per = bR // br
        return (b_s0 + ri // per, b_l, ri % per, ci)

    if mode == "nt":
        b_spec = pl.BlockSpec((None, None, tn, tk), lambda i, j, k: b_index(j, k, tn, tk))
    else:
        b_spec = pl.BlockSpec((None, None, tk, tn), lambda i, j, k: b_index(k, j, tk, tn))

    in_specs = [a_spec, b_spec]
    operands = [a, b4]
    for kind, arr in extras:
        if kind == "mn":
            in_specs.append(pl.BlockSpec((tm, tn), lambda i, j, k: (i, j)))
        elif kind == "ex":
            per_ex = rows_per_ex // tm
            in_specs.append(pl.BlockSpec((None, 1, tn), lambda i, j, k: (i // per_ex, 0, j)))
        else:
            in_specs.append(pl.BlockSpec((1, tn), lambda i, j, k: (0, j)))
        operands.append(arr)
    n_extra = len(extras)

    multi = isinstance(out_dtype, tuple)
    n_out = len(out_dtype) if multi else 1
    if out_lay is None:
        out_shape = [jax.ShapeDtypeStruct((M, N), dt) for dt in (out_dtype if multi else (out_dtype,))]
        out_spec = [pl.BlockSpec((tm, tn), lambda i, j, k: (i, j)) for _ in range(n_out)]
    else:
        out_shape = [jax.ShapeDtypeStruct(tuple(out4_shape), out_dtype)]
        if out_lay == "cs":
            per_o = oC // tn
            out_spec = [pl.BlockSpec((None, None, tm, tn), lambda i, j, k: (out_s0 + j // per_o, out_l, i, j % per_o))]
        else:
            per_o = oR // tm
            out_spec = [pl.BlockSpec((None, None, tm, tn), lambda i, j, k: (out_s0 + i // per_o, out_l, i % per_o, j))]
    aliases = {}
    if out_into is not None:
        in_specs.append(pl.BlockSpec(memory_space=pl.ANY))
        operands.append(out_into)
        aliases = {len(operands) - 1: 0}

    dims = {"nn": (((1,), (0,)), ((), ())), "nt": (((1,), (1,)), ((), ())), "tn": (((0,), (0,)), ((), ()))}[mode]

    def body(a_ref, b_ref, *rest):
        extra_refs = rest[:n_extra]
        o_refs = rest[len(rest) - n_out - (nk > 1):len(rest) - (nk > 1)]

        def finish(r):
            if epi is not None:
                r = epi(r, *[e[...] for e in extra_refs])
            for o_ref, val in zip(o_refs, r if multi else (r,)):
                o_ref[...] = val.astype(o_ref.dtype)

        part = lax.dot_general(a_ref[...].astype(BF16), b_ref[...].astype(BF16), dims, preferred_element_type=F32)
        if nk == 1:
            finish(part)
            return
        acc = rest[-1]
        k = pl.program_id(2)

        @pl.when(k == 0)
        def _():
            acc[...] = part

        @pl.when(k != 0)
        def _():
            acc[...] += part

        @pl.when(k == nk - 1)
        def _():
            finish(acc[...])

    outs = pl.pallas_call(
        body, name=name, grid=grid, in_specs=in_specs, out_specs=out_spec, out_shape=out_shape,
        scratch_shapes=[pltpu.VMEM((tm, tn), F32)] if nk > 1 else [], input_output_aliases=aliases,
        compiler_params=_params("parallel", "parallel", "arbitrary"),
    )(*operands)
    return tuple(outs) if multi else outs[0]


def _as4(w):
    return w.reshape((1, 1) + w.shape)


def _relu2(acc):
    r = jnp.maximum(acc, 0.0)
    return r * r


def _relu2_bwd(acc, r):
    return acc * (2.0 * jnp.sqrt(r.astype(F32)))


def _add(acc, e):
    return acc + e


def _gated_residual(acc, h, gate):
    return acc, h + gate * acc


def _modulated_norm(x, g, scale, shift):
    rstd = lax.rsqrt(jnp.mean(x * x, axis=-1, keepdims=True) + EPS)
    return ((x * rstd) * g) * (1.0 + scale) + shift


def _gated_residual_norm(acc, h, gate, g, scale, shift):
    h_new = h + gate * acc
    return acc, h_new, _modulated_norm(h_new, g, scale, shift)


def _row_tiles(N, B, pref=256):
    S = N // B
    tm = _div(S, pref, SUBLANES)
    return tm, S // tm


def _normmod(h, g, scale, shift, B):
    N, D = h.shape
    tm, per_ex = _row_tiles(N, B)

    def body(h_ref, g_ref, sc_ref, sh_ref, u_ref):
        u_ref[...] = _modulated_norm(h_ref[...], g_ref[...], sc_ref[...], sh_ref[...]).astype(u_ref.dtype)

    tok = pl.BlockSpec((tm, D), lambda i: (i, 0))
    vec = pl.BlockSpec((1, D), lambda i: (0, 0))
    ex = pl.BlockSpec((None, 1, D), lambda i: (i // per_ex, 0, 0))
    return pl.pallas_call(
        body, name="normmod_fwd", grid=(N // tm,), in_specs=[tok, vec, ex, ex], out_specs=tok,
        out_shape=jax.ShapeDtypeStruct((N, D), BF16), compiler_params=_params("parallel"),
    )(h, g, scale, shift)


def _normmod_bwd(du, h, g, scale, dh_in, B, below=None):
    N, D = h.shape
    tm, per_ex = _row_tiles(N, B)
    fused = below is not None

    def body(*refs):
        du_ref, h_ref, g_ref, sc_ref, dhin_ref = refs[:5]
        dh_ref, dg_ref, dsc_ref, dsh_ref = refs[5 + 2 * fused:9 + 2 * fused]
        i = pl.program_id(0)
        x = h_ref[...]
        gv = g_ref[...]
        d_u = du_ref[...].astype(F32)
        rstd = lax.rsqrt(jnp.mean(x * x, axis=-1, keepdims=True) + EPS)
        xn = x * rstd
        dyg = d_u * (1.0 + sc_ref[...])
        dxn = dyg * gv
        dh = dhin_ref[...] + rstd * (dxn - xn * jnp.mean(dxn * xn, axis=-1, keepdims=True))
        dh_ref[...] = dh
        sums = [(dsc_ref, jnp.sum(d_u * (xn * gv), axis=0, keepdims=True)), (dsh_ref, jnp.sum(d_u, axis=0, keepdims=True))]
        if fused:
            y_ref, gt_ref = refs[5:7]
            dy_ref, dgt_ref = refs[9 + 2 * fused:]
            dy_ref[...] = (gt_ref[...] * dh).astype(dy_ref.dtype)
            sums.append((dgt_ref, jnp.sum(dh * y_ref[...], axis=0, keepdims=True)))
        dg_t = jnp.sum(dyg * xn, axis=0, keepdims=True)

        @pl.when(i % per_ex == 0)
        def _():
            for ref, val in sums:
                ref[...] = val

        @pl.when(i % per_ex != 0)
        def _():
            for ref, val in sums:
                ref[...] += val

        @pl.when(i == 0)
        def _():
            dg_ref[...] = dg_t

        @pl.when(i != 0)
        def _():
            dg_ref[...] += dg_t

    tok = pl.BlockSpec((tm, D), lambda i: (i, 0))
    vec = pl.BlockSpec((1, D), lambda i: (0, 0))
    ex = pl.BlockSpec((None, 1, D), lambda i: (i // per_ex, 0, 0))
    per_ex_shape = jax.ShapeDtypeStruct((B, 1, D), F32)
    outs = pl.pallas_call(
        body, name="normmod_bwd", grid=(N // tm,), in_specs=[tok, tok, vec, ex, tok] + ([tok, ex] if fused else []),
        out_specs=[tok, vec, ex, ex] + ([tok, ex] if fused else []),
        out_shape=[jax.ShapeDtypeStruct((N, D), F32), jax.ShapeDtypeStruct((1, D), F32), per_ex_shape, per_ex_shape]
        + ([jax.ShapeDtypeStruct((N, D), BF16), per_ex_shape] if fused else []),
        compiler_params=_params("arbitrary"),
    )(du, h, g, scale, dh_in, *(below if fused else ()))
    return tuple(outs) if fused else tuple(outs) + (None, None)


def _residual_bwd(dh, gate, y, B):
    N, D = dh.shape
    tm, per_ex = _row_tiles(N, B)

    def body(dh_ref, gt_ref, y_ref, dy_ref, dgt_ref):
        i = pl.program_id(0)
        d = dh_ref[...]
        dy_ref[...] = (gt_ref[...] * d).astype(dy_ref.dtype)
        t = jnp.sum(d * y_ref[...], axis=0, keepdims=True)

        @pl.when(i % per_ex == 0)
        def _():
            dgt_ref[...] = t

        @pl.when(i % per_ex != 0)
        def _():
            dgt_ref[...] += t

    tok = pl.BlockSpec((tm, D), lambda i: (i, 0))
    ex = pl.BlockSpec((None, 1, D), lambda i: (i // per_ex, 0, 0))
    return pl.pallas_call(
        body, name="residual_bwd", grid=(N // tm,), in_specs=[tok, ex, tok], out_specs=[tok, ex],
        out_shape=[jax.ShapeDtypeStruct((N, D), BF16), jax.ShapeDtypeStruct((B, 1, D), F32)],
        compiler_params=_params("arbitrary"),
    )(dh, gate, y)


def _glu_residual_norm(zz, h, gate, g, scale, shift, B):
    N, D2 = zz.shape
    D = D2 // 2
    tm, per_ex = _row_tiles(N, B)

    def body(v_ref, g_ref, h_ref, gt_ref, ng_ref, sc_ref, sh_ref, y_ref, o_ref, u_ref):
        y = v_ref[...] * jax.nn.sigmoid(g_ref[...])
        y_ref[...] = y.astype(y_ref.dtype)
        h_new = h_ref[...] + gt_ref[...] * y
        o_ref[...] = h_new
        u_ref[...] = _modulated_norm(h_new, ng_ref[...], sc_ref[...], sh_ref[...]).astype(u_ref.dtype)

    tok = pl.BlockSpec((tm, D), lambda i: (i, 0))
    vec = pl.BlockSpec((1, D), lambda i: (0, 0))
    ex = pl.BlockSpec((None, 1, D), lambda i: (i // per_ex, 0, 0))
    return pl.pallas_call(
        body, name="glu_fwd", grid=(N // tm,),
        in_specs=[tok, pl.BlockSpec((tm, D), lambda i: (i, 1)), tok, ex, vec, ex, ex], out_specs=[tok, tok, tok],
        out_shape=[jax.ShapeDtypeStruct((N, D), BF16), jax.ShapeDtypeStruct((N, D), F32), jax.ShapeDtypeStruct((N, D), BF16)],
        compiler_params=_params("parallel"),
    )(zz, zz, h, gate, g, scale, shift)


def _glu_bwd(dy, zz):
    N, D2 = zz.shape
    D = D2 // 2
    tm = _div(N, 256, SUBLANES)

    def body(dy_ref, v_ref, g_ref, o_ref):
        d = dy_ref[...].astype(F32)
        s = jax.nn.sigmoid(g_ref[...])
        o_ref[...] = jnp.concatenate([d * s, d * v_ref[...] * s * (1.0 - s)], axis=1).astype(o_ref.dtype)

    return pl.pallas_call(
        body, name="glu_bwd", grid=(N // tm,),
        in_specs=[pl.BlockSpec((tm, D), lambda i: (i, 0)), pl.BlockSpec((tm, D), lambda i: (i, 0)),
                  pl.BlockSpec((tm, D), lambda i: (i, 1))],
        out_specs=pl.BlockSpec((tm, D2), lambda i: (i, 0)), out_shape=jax.ShapeDtypeStruct((N, D2), BF16),
        compiler_params=_params("parallel"),
    )(dy, zz, zz)


def _loss_head(h, g, target):
    N, D = h.shape
    tm = _div(N, 256, SUBLANES)

    def body(h_ref, g_ref, t_ref, loss_ref, dh_ref, dg_ref):
        i = pl.program_id(0)
        x = h_ref[...]
        gv = g_ref[...]
        rstd = lax.rsqrt(jnp.mean(x * x, axis=-1, keepdims=True) + EPS)
        xn = x * rstd
        err = xn * gv - t_ref[...]
        part = 0.5 * jnp.sum(jnp.sum(err * err, axis=-1, keepdims=True) / D, axis=0, keepdims=True)
        dy = err / D
        dxn = dy * gv
        dh_ref[...] = rstd * (dxn - xn * jnp.mean(dxn * xn, axis=-1, keepdims=True))
        dg_t = jnp.sum(dy * xn, axis=0, keepdims=True)
        part = jnp.broadcast_to(part, loss_ref.shape)

        @pl.when(i == 0)
        def _():
            loss_ref[...] = part
            dg_ref[...] = dg_t

        @pl.when(i != 0)
        def _():
            loss_ref[...] += part
            dg_ref[...] += dg_t

    tok = pl.BlockSpec((tm, D), lambda i: (i, 0))
    vec = pl.BlockSpec((1, D), lambda i: (0, 0))
    return pl.pallas_call(
        body, name="loss_head", grid=(N // tm,), in_specs=[tok, vec, tok],
        out_specs=[pl.BlockSpec((SUBLANES, LANES), lambda i: (0, 0)), tok, vec],
        out_shape=[jax.ShapeDtypeStruct((SUBLANES, LANES), F32), jax.ShapeDtypeStruct((N, D), F32),
                   jax.ShapeDtypeStruct((1, D), F32)],
        compiler_params=_params("arbitrary"),
    )(h, g, target)


def _swap_halves(x):
    half = x.shape[-1] // 2
    return jnp.concatenate([x[:, half:], x[:, :half]], axis=1)


def _gelu(y):
    return jax.nn.gelu(y)


def _gelu_grad(y):
    c0 = math.sqrt(2.0 / math.pi)
    inner = c0 * (y + 0.044715 * y * y * y)
    t = jnp.tanh(inner)
    return 0.5 * (1.0 + t) + 0.5 * y * (1.0 - t * t) * c0 * (1.0 + 3.0 * 0.044715 * y * y)


def _s5_discretize(lam_re, lam_im, log_dt, b_re, b_im, c_re, c_im):
    G = lam_re.shape[0]
    nch = G // CHUNK_GROUPS
    dt = jnp.exp(log_dt)[:, None]
    er = jnp.exp(lam_re * dt)
    a_re = er * jnp.cos(lam_im * dt)
    a_im = er * jnp.sin(lam_im * dt)
    den = lam_re * lam_re + lam_im * lam_im
    n_re, n_im = a_re - 1.0, a_im
    f_re = (n_re * lam_re + n_im * lam_im) / den
    f_im = (n_im * lam_re - n_re * lam_im) / den
    bb_re = f_re[..., None] * b_re - f_im[..., None] * b_im
    bb_im = f_re[..., None] * b_im + f_im[..., None] * b_re
    eye = jnp.eye(CHUNK_GROUPS, dtype=F32)

    def pack_b(bb):
        bb = bb.reshape(nch, CHUNK_GROUPS, SSM_STATE, SSM_GROUP)
        return jnp.einsum("jgpc,gh->jgchp", bb, eye).reshape(nch, LANES, CHUNK_STATE)

    def pack_c(cc):
        cc = cc.reshape(nch, CHUNK_GROUPS, SSM_GROUP, SSM_STATE)
        return jnp.einsum("jgcp,gh->jgphc", cc, eye).reshape(nch, CHUNK_STATE, LANES)

    bd = jnp.concatenate([pack_b(bb_re), pack_b(bb_im)], axis=2)
    cd = jnp.concatenate([pack_c(c_re), pack_c(-c_im)], axis=1)
    return bd, cd, a_re, a_im


S5_TILE = 1024
S5_SEG = S5_TILE // SUBLANES
S5_UNROLL = 4


def _s5_scan_coefs(lam_re, lam_im, log_dt, seg):
    G = lam_re.shape[0]
    nch = G // CHUNK_GROUPS
    dt = jnp.exp(log_dt)[:, None]
    rate = (lam_re * dt).reshape(nch, 1, CHUNK_STATE)
    freq = (lam_im * dt).reshape(nch, 1, CHUNK_STATE)

    def powers(ks):
        k = jnp.asarray(ks, F32)[None, :, None]
        er = jnp.exp(k * rate)
        re, im = er * jnp.cos(k * freq), er * jnp.sin(k * freq)
        return jnp.concatenate([re, re], axis=2), jnp.concatenate([-im, im], axis=2)

    pw = jnp.stack(powers(range(1, seg + 1)), axis=1)
    steps = (1, 2, 4)
    re, im = powers([s * seg for s in steps])
    row = jnp.arange(SUBLANES, dtype=jnp.int32)[None, None, :, None]
    shift = jnp.asarray(steps, jnp.int32)[None, :, None, None]

    def table(reverse):
        mask = (row < SUBLANES - shift) if reverse else (row >= shift)
        pair = jnp.stack([jnp.where(mask, re[:, :, None, :], 0.0),
                          jnp.where(mask, (-im if reverse else im)[:, :, None, :], 0.0)], axis=2)
        return pair.reshape(nch, 2 * len(steps), SUBLANES, 2 * CHUNK_STATE)

    return pw, table(False), table(True)


def _to_segments(dst_s, src_ref, seg):
    for j in range(SUBLANES):
        dst_s[pl.ds(j, seg, stride=SUBLANES), :] = src_ref[pl.ds(j * seg, seg), :].astype(F32)


def _from_segments(dst_ref, src_s, seg):
    for j in range(SUBLANES):
        dst_ref[pl.ds(j * seg, seg), :] = src_s[pl.ds(j, seg, stride=SUBLANES), :].astype(dst_ref.dtype)


def _seg_scan(x_ref, pw_ref, seg_ref, carry_ref, c_ref, seg, reverse):
    W = x_ref.shape[-1]
    tm = x_ref.shape[0]
    sgn = -1.0 if reverse else 1.0
    ar = jnp.broadcast_to(pw_ref[0, 0:1, :], (SUBLANES, W))
    ai = sgn * jnp.broadcast_to(pw_ref[1, 0:1, :], (SUBLANES, W))

    def rows(i):
        return pl.ds(pl.multiple_of(i * SUBLANES, SUBLANES), SUBLANES)

    def step(t, prev):
        i = (seg - 2 - t) if reverse else (t + 1)
        x = x_ref[rows(i), :] + ar * prev + ai * _swap_halves(prev)
        x_ref[rows(i), :] = x
        return x

    start = (seg - 1) * SUBLANES if reverse else 0
    edge = lax.fori_loop(0, seg - 1, step, x_ref[start:start + SUBLANES, :], unroll=S5_UNROLL)
    row = lax.broadcasted_iota(jnp.int32, (SUBLANES, W), 0)
    if reverse:
        f = jnp.where(row == SUBLANES - 1, carry_ref[...], pltpu.roll(edge, SUBLANES - 1, 0))
    else:
        f = jnp.where(row == 0, carry_ref[...], pltpu.roll(edge, 1, 0))
    for si, s in enumerate((1, 2, 4)):
        fs = pltpu.roll(f, (SUBLANES - s) if reverse else s, 0)
        f = f + seg_ref[2 * si] * fs + seg_ref[2 * si + 1] * _swap_halves(fs)
    c_ref[...] = f
    fsw = _swap_halves(f)

    def fix(i, _):
        k = (seg - 1 - i) if reverse else i
        x_ref[rows(i), :] = x_ref[rows(i), :] + pw_ref[0, pl.ds(k, 1), :] * f + (sgn * pw_ref[1, pl.ds(k, 1), :]) * fsw
        return 0

    lax.fori_loop(0, seg, fix, 0, unroll=S5_UNROLL)
    leaving = x_ref[0:1, :] if reverse else x_ref[tm - 1:tm, :]
    carry_ref[...] = jnp.broadcast_to(leaving, carry_ref.shape)


def _s5_fwd(u, bd, cd, pw, seg_f, d_skip, B, carry=None):
    N, D = u.shape
    S = N // B
    nch = D // LANES
    W = 2 * CHUNK_STATE
    tm, seg = S5_TILE, S5_SEG
    nt = S // tm

    def body(u_ref, bd_ref, cd_ref, pw_ref, seg_ref, d_ref, z_ref, cin_ref, x_s, carry, c_s, u_s, z_s):
        t = pl.program_id(2)

        @pl.when(t == 0)
        def _():
            carry[...] = jnp.zeros_like(carry)

        cin_ref[...] = carry[...]
        _to_segments(u_s, u_ref, seg)
        uf = u_s[...]
        x_s[...] = jnp.dot(uf.astype(BF16), bd_ref[...], preferred_element_type=F32)
        _seg_scan(x_s, pw_ref, seg_ref, carry, c_s, seg, False)
        y = jnp.dot(x_s[...].astype(BF16), cd_ref[...], preferred_element_type=F32) + d_ref[...] * uf
        z_s[...] = _gelu(y)
        _from_segments(z_ref, z_s, seg)

    (z, carries), moved = _carried_call(
        body, name="s5_fwd", grid=(nch, B, nt),
        in_specs=[pl.BlockSpec((tm, LANES), lambda j, b, t: (b * nt + t, j)),
                  pl.BlockSpec((None, LANES, W), lambda j, b, t: (j, 0, 0)),
                  pl.BlockSpec((None, W, LANES), lambda j, b, t: (j, 0, 0)),
                  pl.BlockSpec((None, 2, seg, W), lambda j, b, t: (j, 0, 0, 0)),
                  pl.BlockSpec((None, 6, SUBLANES, W), lambda j, b, t: (j, 0, 0, 0)),
                  pl.BlockSpec((1, LANES), lambda j, b, t: (0, j))],
        out_specs=[pl.BlockSpec((tm, LANES), lambda j, b, t: (b * nt + t, j)),
                   pl.BlockSpec((None, None, SUBLANES, W), lambda j, b, t: (j, b * nt + t, 0, 0))],
        out_shape=[jax.ShapeDtypeStruct((N, D), BF16), jax.ShapeDtypeStruct((nch, B * nt, SUBLANES, W), F32)],
        scratch_shapes=[pltpu.VMEM((tm, W), F32), pltpu.VMEM((SUBLANES, W), F32), pltpu.VMEM((SUBLANES, W), F32),
                        pltpu.VMEM((tm, LANES), F32), pltpu.VMEM((tm, LANES), F32)],
        operands=(u, bd, cd, pw, seg_f, d_skip), sem=("parallel", "arbitrary", "arbitrary"), carry=carry)
    return z, carries, moved


def _s5_bwd(u, dz, bd, cd, pw, seg_f, seg_b, d_skip, carries, B, carry=None):
    N, D = u.shape
    S = N // B
    nch = D // LANES
    W = 2 * CHUNK_STATE
    tm, seg = S5_TILE, S5_SEG
    nt = S // tm
    tn_dims = (((0,), (0,)), ((), ()))
    nt_dims = (((1,), (1,)), ((), ()))

    def body(u_ref, dz_ref, bd_ref, cd_ref, pw_ref, sf_ref, sb_ref, d_ref, cin_ref,
             du_ref, dbd_ref, dcd_ref, da_ref, dd_ref, x_s, l_s, carry, lcarry, c_s, lc_s, u_s, t_s):
        b = pl.program_id(1)
        t = pl.program_id(2)

        @pl.when((b == 0) & (t == 0))
        def _():
            dbd_ref[...] = jnp.zeros_like(dbd_ref)
            dcd_ref[...] = jnp.zeros_like(dcd_ref)
            da_ref[...] = jnp.zeros_like(da_ref)
            dd_ref[...] = jnp.zeros_like(dd_ref)

        @pl.when(t == 0)
        def _():
            lcarry[...] = jnp.zeros_like(lcarry)

        _to_segments(u_s, u_ref, seg)
        _to_segments(t_s, dz_ref, seg)
        uf = u_s[...]
        uv = uf.astype(BF16)
        carry[...] = cin_ref[...]
        x_s[...] = jnp.dot(uv, bd_ref[...], preferred_element_type=F32)
        _seg_scan(x_s, pw_ref, sf_ref, carry, c_s, seg, False)
        xb = x_s[...].astype(BF16)
        y = jnp.dot(xb, cd_ref[...], preferred_element_type=F32) + d_ref[...] * uf
        dy = t_s[...] * _gelu_grad(y)
        dd_ref[...] += jnp.sum(dy * uf, axis=0, keepdims=True)
        dyb = dy.astype(BF16)
        dcd_ref[...] += lax.dot_general(xb, dyb, tn_dims, preferred_element_type=F32)
        l_s[...] = lax.dot_general(dyb, cd_ref[...], nt_dims, preferred_element_type=F32)
        _seg_scan(l_s, pw_ref, sb_ref, lcarry, lc_s, seg, True)
        lb = l_s[...].astype(BF16)
        dbd_ref[...] += lax.dot_general(uv, lb, tn_dims, preferred_element_type=F32)
        t_s[...] = lax.dot_general(lb, bd_ref[...], nt_dims, preferred_element_type=F32) + d_ref[...] * dy
        _from_segments(du_ref, t_s, seg)
        lam_rest, x_prev = l_s[SUBLANES:, :], x_s[:tm - SUBLANES, :]
        lam_0, c_in = l_s[:SUBLANES, :], c_s[...]
        da_ref[0:1, :] += (jnp.sum(lam_rest * x_prev, axis=0, keepdims=True) + jnp.sum(lam_0 * c_in, axis=0, keepdims=True))
        da_ref[1:2, :] += (jnp.sum(lam_rest * _swap_halves(x_prev), axis=0, keepdims=True)
                           + jnp.sum(lam_0 * _swap_halves(c_in), axis=0, keepdims=True))

    tile = lambda j, b, t: (b * nt + (nt - 1 - t), j)
    chunk3 = lambda j, b, t: (j, 0, 0)
    chunk4 = lambda j, b, t: (j, 0, 0, 0)
    outs, moved = _carried_call(
        body, name="s5_bwd", grid=(nch, B, nt),
        in_specs=[pl.BlockSpec((tm, LANES), tile), pl.BlockSpec((tm, LANES), tile),
                  pl.BlockSpec((None, LANES, W), chunk3), pl.BlockSpec((None, W, LANES), chunk3),
                  pl.BlockSpec((None, 2, seg, W), chunk4), pl.BlockSpec((None, 6, SUBLANES, W), chunk4),
                  pl.BlockSpec((None, 6, SUBLANES, W), chunk4), pl.BlockSpec((1, LANES), lambda j, b, t: (0, j)),
                  pl.BlockSpec((None, None, SUBLANES, W), lambda j, b, t: (j, b * nt + (nt - 1 - t), 0, 0))],
        out_specs=[pl.BlockSpec((tm, LANES), tile), pl.BlockSpec((None, LANES, W), chunk3),
                   pl.BlockSpec((None, W, LANES), chunk3), pl.BlockSpec((None, 2, W), chunk3),
                   pl.BlockSpec((1, LANES), lambda j, b, t: (0, j))],
        out_shape=[jax.ShapeDtypeStruct((N, D), F32), jax.ShapeDtypeStruct((nch, LANES, W), F32),
                   jax.ShapeDtypeStruct((nch, W, LANES), F32), jax.ShapeDtypeStruct((nch, 2, W), F32),
                   jax.ShapeDtypeStruct((1, D), F32)],
        scratch_shapes=[pltpu.VMEM((tm, W), F32), pltpu.VMEM((tm, W), F32)] + [pltpu.VMEM((SUBLANES, W), F32)] * 4
        + [pltpu.VMEM((tm, LANES), F32)] * 2,
        operands=(u, dz, bd, cd, pw, seg_f, seg_b, d_skip, carries), sem=("parallel", "arbitrary", "arbitrary"), carry=carry)
    return (*outs, moved)


ATTN_HEADS = LANES // HEAD_DIM
ATTN_FWD_UNROLL = 4
ATTN_BWD_UNROLL = 4


def _attn_mask(n):
    qi = lax.broadcasted_iota(jnp.int32, (ATTN_BLOCK, 2 * ATTN_BLOCK), 0)
    kj = lax.broadcasted_iota(jnp.int32, (ATTN_BLOCK, 2 * ATTN_BLOCK), 1)
    prev_ok = (kj < ATTN_BLOCK) & (kj >= qi) & (n > 0)
    return prev_ok | ((kj >= ATTN_BLOCK) & (kj - ATTN_BLOCK <= qi))


def _head_lanes(h):
    lane = lax.broadcasted_iota(jnp.int32, (ATTN_BLOCK, LANES), 1)
    return (lane >= h * HEAD_DIM) & (lane < (h + 1) * HEAD_DIM)


def _per_head(cols):
    out = jnp.broadcast_to(cols[-1], (ATTN_BLOCK, LANES))
    for h in range(len(cols) - 2, -1, -1):
        out = jnp.where(_head_lanes(h), jnp.broadcast_to(cols[h], (ATTN_BLOCK, LANES)), out)
    return out


def _only_head(x, h):
    return jnp.where(_head_lanes(h), x, 0.0).astype(BF16)


def _block_rows(tb, dil, nb):
    r = tb // nb
    n = tb % nb
    start = r + dil * ATTN_BLOCK * n
    startp = jnp.where(n > 0, start - dil * ATTN_BLOCK, start)
    return n, pl.ds(start, ATTN_BLOCK, stride=dil), pl.ds(startp, ATTN_BLOCK, stride=dil)


def _attn_fwd(q, k, v, B, carry=None):
    _, S, D3 = q.shape
    D = D3 // 3
    HP = D // LANES
    scale = HEAD_DIM ** -0.5
    n_blocks = S // ATTN_BLOCK
    nbr = len(DILATIONS)
    nt_dims = (((1,), (1,)), ((), ()))

    def branch(dil, q_ref, k_ref, v_ref, acc, m_s, l_s):
        nb = (S // dil) // ATTN_BLOCK

        def blk(tb, _):
            n, rows, rowsp = _block_rows(tb, dil, nb)
            qb = q_ref[rows, :] * scale
            kk = jnp.concatenate([k_ref[rowsp, :], k_ref[rows, :]], axis=0).astype(BF16)
            vv = jnp.concatenate([v_ref[rowsp, :], v_ref[rows, :]], axis=0).astype(BF16)
            ok = _attn_mask(n)
            ms, ls, accs = [], [], []
            for h in range(ATTN_HEADS):
                s = lax.dot_general(_only_head(qb, h), kk, nt_dims, preferred_element_type=F32)
                s = jnp.where(ok, s, NEG)
                mh = jnp.max(s, axis=-1, keepdims=True)
                p = jnp.exp(s - mh)
                ms.append(mh)
                ls.append(jnp.sum(p, axis=-1, keepdims=True))
                accs.append(jnp.dot(p.astype(BF16), vv, preferred_element_type=F32))
            m_s[rows, :] = _per_head(ms)
            l_s[rows, :] = _per_head(ls)
            acc[rows, :] = _per_head(accs)
            return 0

        lax.fori_loop(0, n_blocks, blk, 0, unroll=ATTN_FWD_UNROLL)

    def body(q_ref, k_ref, v_ref, o_ref, lse_ref, *scratch):
        accs, m_ss, l_ss = scratch[:nbr], scratch[nbr:2 * nbr], scratch[2 * nbr:]
        g = pl.program_id(2)
        for gi, dil in enumerate(DILATIONS):
            pl.when(g == gi)(functools.partial(branch, dil, q_ref, k_ref, v_ref, accs[gi], m_ss[gi], l_ss[gi]))

        @pl.when(g == nbr - 1)
        def _():
            def fin(i, _):
                rows = pl.ds(pl.multiple_of(i * ATTN_BLOCK, ATTN_BLOCK), ATTN_BLOCK)
                ms = [m[rows, :] for m in m_ss]
                m_all = functools.reduce(jnp.maximum, ms)
                ws = [jnp.exp(m - m_all) for m in ms]
                den = sum(w * l[rows, :] for w, l in zip(ws, l_ss))
                o_ref[rows, :] = sum(w * a[rows, :] for w, a in zip(ws, accs)) / den
                lse_ref[rows, :] = m_all + jnp.log(den)
                return 0

            lax.fori_loop(0, n_blocks, fin, 0)

    br = pl.BlockSpec((None, S, LANES), lambda b, hp, g: (b, 0, g * HP + hp))
    hd = pl.BlockSpec((None, S, LANES), lambda b, hp, g: (b, 0, hp))
    (o, lse), moved = _carried_call(
        body, name="attn_fwd", grid=(B, HP, nbr), in_specs=[br, br, br], out_specs=[hd, hd],
        out_shape=[jax.ShapeDtypeStruct((B, S, D), F32), jax.ShapeDtypeStruct((B, S, D), F32)],
        scratch_shapes=[pltpu.VMEM((S, LANES), F32)] * (3 * nbr),
        operands=(q, k, v), sem=("parallel", "parallel", "arbitrary"), carry=carry)
    return o, lse, moved


def _attn_bwd(q, k, v, o, lse, do, dk_prev, dv_prev, B, last, carry=None):
    _, S, D3 = q.shape
    D = D3 // 3
    HP = D // LANES
    scale = HEAD_DIM ** -0.5
    n_blocks = S // ATTN_BLOCK
    has_prev = dk_prev is not None
    nt_dims = (((1,), (1,)), ((), ()))
    tn_dims = (((0,), (0,)), ((), ()))

    def branch(dil, q_ref, k_ref, v_ref, lse_ref, do_ref, dq_s, dk_c, dv_c, delta, dk_p, dv_p):
        nb = (S // dil) // ATTN_BLOCK

        def blk(tb, _):
            n, rows, rowsp = _block_rows(tb, dil, nb)
            qb = q_ref[rows, :] * scale
            dob, lb, db = do_ref[rows, :], lse_ref[rows, :], delta[rows, :]
            kk = jnp.concatenate([k_ref[rowsp, :], k_ref[rows, :]], axis=0).astype(BF16)
            vv = jnp.concatenate([v_ref[rowsp, :], v_ref[rows, :]], axis=0).astype(BF16)
            ok = _attn_mask(n)
            dqs = []
            dkk = dvv = None
            for h in range(ATTN_HEADS):
                qh, doh = _only_head(qb, h), _only_head(dob, h)
                lh = lb[:, h * HEAD_DIM:h * HEAD_DIM + 1]
                dlt = db[:, h * HEAD_DIM:h * HEAD_DIM + 1]
                s = lax.dot_general(qh, kk, nt_dims, preferred_element_type=F32)
                p = jnp.where(ok, jnp.exp(s - lh), 0.0)
                dp = lax.dot_general(doh, vv, nt_dims, preferred_element_type=F32)
                ds = (p * (dp - dlt)).astype(BF16)
                dqs.append(jnp.dot(ds, kk, preferred_element_type=F32))
                dk_h = lax.dot_general(ds, qh, tn_dims, preferred_element_type=F32)
                dv_h = lax.dot_general(p.astype(BF16), doh, tn_dims, preferred_element_type=F32)
                dkk = dk_h if dkk is None else dkk + dk_h
                dvv = dv_h if dvv is None else dvv + dv_h
            dq_s[rows, :] = _per_head(dqs) * scale
            dk_p[rowsp, :] = dkk[:ATTN_BLOCK]
            dv_p[rowsp, :] = dvv[:ATTN_BLOCK]
            dk_c[rows, :] = dkk[ATTN_BLOCK:]
            dv_c[rows, :] = dvv[ATTN_BLOCK:]
            return 0

        lax.fori_loop(0, n_blocks, blk, 0, unroll=ATTN_BWD_UNROLL)

    def body(*refs):
        q_ref, k_ref, v_ref, o_ref, lse_ref, do_ref = refs[:6]
        n_in = 8 if has_prev else 6
        dq_ref, dk_ref, dv_ref, delta, dk_p, dv_p, dq_s, dk_c, dv_c = refs[n_in:n_in + 9]
        g = pl.program_id(2)

        @pl.when(g == 0)
        def _():
            def dl(i, _):
                rows = pl.ds(pl.multiple_of(i * ATTN_BLOCK, ATTN_BLOCK), ATTN_BLOCK)
                prod = do_ref[rows, :] * o_ref[rows, :]
                delta[rows, :] = _per_head([jnp.sum(jnp.where(_head_lanes(h), prod, 0.0), axis=-1, keepdims=True)
                                            for h in range(ATTN_HEADS)])
                return 0

            lax.fori_loop(0, n_blocks, dl, 0)

        dk_p[...] = jnp.zeros_like(dk_p)
        dv_p[...] = jnp.zeros_like(dv_p)
        for gi, dil in enumerate(DILATIONS):
            pl.when(g == gi)(functools.partial(branch, dil, q_ref, k_ref, v_ref, lse_ref, do_ref, dq_s, dk_c, dv_c,
                                               delta, dk_p, dv_p))

        def fin(i, _):
            rows = pl.ds(pl.multiple_of(i * ATTN_BLOCK, ATTN_BLOCK), ATTN_BLOCK)
            dk_t = dk_c[rows, :] + dk_p[rows, :]
            dv_t = dv_c[rows, :] + dv_p[rows, :]
            if has_prev:
                dk_t = dk_t + refs[6][rows, :].astype(F32)
                dv_t = dv_t + refs[7][rows, :].astype(F32)
            dq_ref[rows, :] = dq_s[rows, :].astype(dq_ref.dtype)
            dk_ref[rows, :] = dk_t.astype(dk_ref.dtype)
            dv_ref[rows, :] = dv_t.astype(dv_ref.dtype)
            return 0

        lax.fori_loop(0, n_blocks, fin, 0)

    br = pl.BlockSpec((None, S, LANES), lambda b, hp, g: (b, 0, g * HP + hp))
    hd = pl.BlockSpec((None, S, LANES), lambda b, hp, g: (b, 0, hp))
    ins = [q, k, v, o, lse, do] + ([dk_prev, dv_prev] if has_prev else [])
    kv_dtype = BF16 if last else F32
    (dq, dk, dv), moved = _carried_call(
        body, name="attn_bwd", grid=(B, HP, len(DILATIONS)),
        in_specs=[br, br, br, hd, hd, hd] + ([br, br] if has_prev else []), out_specs=[br, br, br],
        out_shape=[jax.ShapeDtypeStruct(q.shape, BF16), jax.ShapeDtypeStruct(q.shape, kv_dtype),
                   jax.ShapeDtypeStruct(q.shape, kv_dtype)],
        scratch_shapes=[pltpu.VMEM((S, LANES), F32)] * 6,
        operands=ins, sem=("parallel", "parallel", "arbitrary"), carry=carry)
    return dq, dk, dv, moved


def _adamw(w, grads, m, v):
    R, C = w.shape
    tr = _div(R, 256, SUBLANES)
    ng = len(grads)
    c1 = 1.0 - ADAM_B1 ** ADAM_STEP
    c2 = 1.0 - ADAM_B2 ** ADAM_STEP

    def body(*refs):
        w_ref, m_ref, v_ref = refs[0], refs[1 + ng], refs[2 + ng]
        d_ref, mo_ref, vo_ref = refs[3 + ng:6 + ng]
        g = refs[1][...]
        if ng == 2:
            g = g + refs[2][...]
            refs[6 + ng][...] = g
        mn = ADAM_B1 * m_ref[...] + (1.0 - ADAM_B1) * g
        vn = ADAM_B2 * v_ref[...] + (1.0 - ADAM_B2) * (g * g)
        d_ref[...] = -ADAM_LR * ((mn / c1) / (jnp.sqrt(vn / c2) + ADAM_EPS) + ADAM_WD * w_ref[...])
        mo_ref[...] = mn
        vo_ref[...] = vn

    blk = pl.BlockSpec((tr, C), lambda i: (i, 0))
    n_out = 3 + (ng == 2)
    outs = pl.pallas_call(
        body, name="adamw", grid=(R // tr,), in_specs=[blk] * (3 + ng), out_specs=[blk] * n_out,
        out_shape=[jax.ShapeDtypeStruct((R, C), F32)] * n_out, compiler_params=_params("parallel"),
    )(w, *grads, m, v)
    return (outs[3] if ng == 2 else grads[0],) + tuple(outs[:3])


def _sum_shards(recv):
    n, R, C = recv.shape
    tr = _div(R, 256, SUBLANES if recv.dtype == F32 else 2 * SUBLANES)

    def body(r_ref, o_ref):
        s = r_ref[0].astype(F32)
        for i in range(1, n):
            s = s + r_ref[i].astype(F32)
        o_ref[...] = s

    return pl.pallas_call(
        body, name="sum_shards", grid=(R // tr,), in_specs=[pl.BlockSpec((n, tr, C), lambda i: (0, i, 0))],
        out_specs=pl.BlockSpec((tr, C), lambda i: (i, 0)), out_shape=jax.ShapeDtypeStruct((R, C), F32),
        compiler_params=_params("parallel"),
    )(recv)


N_DEV = 8
N_CHIPS = 4


def _all_gather_small(x, carry=None):
    m_per, n = x.shape

    def body(x_ref, out_ref, send_sems, recv_sems, local_sem):
        cx, cy, cc = _coords()
        me, sibling = (cx, cy, cc), (cx, cy, 1 - cc)
        chips = [(1 - cx, cy), (cx, 1 - cy), (1 - cx, 1 - cy)]

        def rows(px, py, pc):
            return out_ref.at[pl.ds((4 * px + 2 * py + pc) * m_per, m_per), :]

        def copy(k, block, to, src=None):
            return pltpu.make_async_remote_copy(
                src_ref=rows(*block) if src is None else src, dst_ref=rows(*block), send_sem=send_sems.at[k],
                recv_sem=recv_sems.at[k], device_id=to, device_id_type=MESH)

        mine = pltpu.make_async_copy(x_ref, rows(*me), local_sem)
        mine.start()
        first = [copy(0, me, sibling, src=x_ref)]
        first += [copy(1 + j, me, (*chip, cc), src=x_ref) for j, chip in enumerate(chips)]
        for cp in first:
            cp.start()
        passed = [copy(4 + j, (*chip, cc), sibling) for j, chip in enumerate(chips)]
        for j, chip in enumerate(chips):
            copy(1 + j, (*chip, cc), me).wait_recv()
            passed[j].start()
        copy(0, sibling, me).wait_recv()
        for j, chip in enumerate(chips):
            copy(4 + j, (*chip, 1 - cc), me).wait_recv()
        for cp in first + passed:
            cp.wait_send()
        mine.wait()

    (out,), moved = _carried_call(
        body, name="all_gather_small", grid=(1,), out_shape=[jax.ShapeDtypeStruct((N_DEV * m_per, n), x.dtype)],
        in_specs=[pl.BlockSpec(memory_space=pltpu.VMEM)], out_specs=[pl.BlockSpec(memory_space=pltpu.VMEM)],
        scratch_shapes=[pltpu.SemaphoreType.DMA((7,)), pltpu.SemaphoreType.DMA((7,)), pltpu.SemaphoreType.DMA],
        operands=(x,), sem=("arbitrary",), carry=carry)
    return out, moved


def _layer_moves(kind, arrays_from, arrays_to, pieces, layer_major=()):
    used = sorted({w for w, _ in pieces})
    pos = {w: i for i, w in enumerate(used)}
    gather = kind == "gather"

    def half(ref, c):
        rows = ref.shape[0] // 2
        return ref.at[pl.ds(c * rows, rows), :]

    def slot(d, w, chip, l):
        return d.at[l, chip] if w in layer_major else d.at[chip, l]

    def plan(src_refs, dst_refs, me):
        cx, cy, cc = me
        mine = 2 * cx + cy
        remote, local = [], []
        for w, l in pieces:
            s, d = src_refs[pos[w]], dst_refs[pos[w]]
            for px, py in _other_chips(cx, cy):
                if gather:
                    remote.append((half(s.at[l], cc), half(slot(d, w, mine, l), cc), (px, py, cc)))
                else:
                    remote.append((s.at[2 * px + py, l], d.at[mine, l], (px, py, cc)))
            local.append((s.at[l], slot(d, w, mine, l)) if gather else (s.at[mine, l], d.at[mine, l]))
        return remote, local

    def onward(src_refs, dst_refs, me):
        cx, cy, cc = me
        moves = []
        for w, l in pieces:
            d = dst_refs[pos[w]]
            for px, py in _other_chips(cx, cy):
                landed = half(slot(d, w, 2 * px + py, l), cc)
                moves.append((landed, landed, (cx, cy, 1 - cc)))
        return moves

    n = 3 * len(pieces)
    carry = _Carry([arrays_from[w] for w in used], [arrays_to[w] for w in used], plan, n, len(pieces),
                   onward if gather else None, n if gather else 0)
    return carry, used


def _swap_with_sibling(sums):
    def plan(src_refs, dst_refs, me):
        cx, cy, cc = me
        return [(s, d, (cx, cy, 1 - cc)) for s, d in zip(src_refs, dst_refs)], []

    return _Carry(sums, [lax.empty(s.shape, s.dtype) for s in sums], plan, len(sums), 0)


def _pack(arrs, width):
    parts, layout, row = [], [], 0
    for a in arrs:
        flat = a.reshape(-1).astype(F32)
        rows = -(-flat.shape[0] // (width * SUBLANES)) * SUBLANES
        parts.append(jnp.pad(flat, (0, rows * width - flat.shape[0])).reshape(rows, width))
        layout.append((row, rows, a.shape))
        row += rows
    pad = -row % (8 * SUBLANES) if row > 8 * SUBLANES else 0
    if pad:
        parts.append(jnp.zeros((pad, width), F32))
    return jnp.concatenate(parts, axis=0), layout, row + pad


def _unpack(buf, layout, idx):
    row, rows, shape = layout[idx]
    size = math.prod(shape)
    return buf[row:row + rows].reshape(-1)[:size].reshape(shape)


def kernel(x, c, ln_g, ada_w, ada_b, ssm_lam_re, ssm_lam_im, ssm_log_dt, ssm_b_re, ssm_b_im, ssm_c_re, ssm_c_im, ssm_d, ssm_w_glu, kv_g, kv_ada_w, kv_ada_b, w_kv, attn_w_q, attn_w_o, mlp_w1, mlp_w2, final_g, loss_target, m_ln_g, m_ada_w, m_ada_b, m_ssm_lam_re, m_ssm_lam_im, m_ssm_log_dt, m_ssm_b_re, m_ssm_b_im, m_ssm_c_re, m_ssm_c_im, m_ssm_d, m_ssm_w_glu, m_kv_g, m_kv_ada_w, m_kv_ada_b, m_w_kv, m_attn_w_q, m_attn_w_o, m_mlp_w1, m_mlp_w2, m_final_g, v_ln_g, v_ada_w, v_ada_b, v_ssm_lam_re, v_ssm_lam_im, v_ssm_log_dt, v_ssm_b_re, v_ssm_b_im, v_ssm_c_re, v_ssm_c_im, v_ssm_d, v_ssm_w_glu, v_kv_g, v_kv_ada_w, v_kv_ada_b, v_w_kv, v_attn_w_q, v_attn_w_o, v_mlp_w1, v_mlp_w2, v_final_g):
    B, S, D = x.shape
    N = B * S
    depth = ln_g.shape[0]
    n_a = ssm_w_glu.shape[0]
    n_b = attn_w_q.shape[0]
    FF = mlp_w1.shape[2] * N_CHIPS
    cx, cy, cc = _coords()
    chip = 2 * cx + cy
    dev = 4 * cx + 2 * cy + cc
    n_ex = N_DEV * B
    ada_cols = ada_w.shape[-1]
    kv_cols = kv_ada_w.shape[-1]

    GLU, KV, Q, O, W1, W2 = range(6)
    shards = [ssm_w_glu.astype(BF16), w_kv.astype(BF16)[None], attn_w_q.astype(BF16), attn_w_o.astype(BF16),
              mlp_w1.astype(BF16), mlp_w2.astype(BF16)]
    row_sharded = (O, W2)
    wg = [lax.empty((s.shape[0], N_CHIPS) + s.shape[1:] if w in row_sharded else (N_CHIPS,) + s.shape, BF16)
          for w, s in enumerate(shards)]

    def whole_rows(w):
        L, _, R, C = wg[w].shape
        return wg[w].reshape(1, L, N_CHIPS * R, C)

    def fetch(pieces):
        return _layer_moves("gather", shards, wg, pieces, layer_major=row_sharded)

    def landed(arrays, used, moved):
        for w, a in zip(used, moved):
            arrays[w] = a

    fetch_with = {l: [(W1, l), (W2, l)] for l in range(depth)}
    fetch_with[0] += [(GLU, l) for l in range(n_a)] + [(Q, 0), (O, 0)]
    fetch_with[n_a - 1] += [(KV, 0)]
    for j in range(1, n_b):
        fetch_with[n_a + j - 1] += [(Q, j), (O, j)]

    c_pack, c_layout, _ = _pack([c], D)
    c_all_buf, _ = _all_gather_small(c_pack)
    c_rows = c_pack.shape[0]
    c_all = jnp.concatenate([_unpack(c_all_buf[d * c_rows:(d + 1) * c_rows], c_layout, 0) for d in range(N_DEV)], axis=0)
    sc_all = jax.nn.silu(c_all).astype(BF16)
    n_mod = depth * 2
    ada_w8 = ada_w.reshape(n_mod, 1, D, ada_cols)
    ada_b_row = ada_b.reshape(1, n_mod * ada_cols)
    mod_local = _mm("ada_fwd", sc_all, ada_w8, mode="nn", M=n_ex, N=n_mod * ada_cols, K=D, b_lay="cs", b_ns=n_mod,
                    epi=_add, extras=[("n", ada_b_row)])
    kv_ada_b_local = lax.dynamic_slice(kv_ada_b.reshape(N_CHIPS, kv_cols), (chip, 0), (1, kv_cols))
    kvmod_local = _mm("ada_fwd", sc_all, _as4(kv_ada_w), mode="nn", M=n_ex, N=kv_cols, K=D, epi=_add,
                      extras=[("n", kv_ada_b_local)])
    mod_pack, mod_layout, mod_rows = _pack([mod_local, kvmod_local, ln_g, ssm_d], D)
    mod_buf, _ = _all_gather_small(mod_pack)

    def from_chip(j, idx):
        d = 2 * j
        return _unpack(mod_buf[d * mod_rows:(d + 1) * mod_rows], mod_layout, idx)

    my_rows = lambda a: lax.dynamic_slice_in_dim(a, dev * B, B, axis=0)
    mods = jnp.concatenate([my_rows(from_chip(j, 0)).reshape(B, n_mod, ada_cols) for j in range(N_CHIPS)], axis=2)
    kvmod = jnp.concatenate([my_rows(from_chip(j, 1)) for j in range(N_CHIPS)], axis=1)
    ln_g_full = jnp.concatenate([from_chip(j, 2) for j in range(N_CHIPS)], axis=2)
    ssm_d_full = jnp.concatenate([from_chip(j, 3) for j in range(N_CHIPS)], axis=1)

    def mod3(l, s):
        mrow = mods[:, l * 2 + s]
        return [mrow[:, i * D:(i + 1) * D].reshape(B, 1, D) for i in range(3)]

    kv_shift, kv_scale = kvmod[:, :D].reshape(B, 1, D), kvmod[:, D:].reshape(B, 1, D)

    s5_tabs = []
    for l in range(n_a):
        prm = (ssm_lam_re[l], ssm_lam_im[l], ssm_log_dt[l], ssm_b_re[l], ssm_b_im[l], ssm_c_re[l], ssm_c_im[l])
        (bd, cd, _, _), disc_vjp = jax.vjp(_s5_discretize, *prm)
        pw, seg_f, seg_b = _s5_scan_coefs(ssm_lam_re[l], ssm_lam_im[l], ssm_log_dt[l], S5_SEG)
        s5_tabs.append((bd.astype(BF16), cd.astype(BF16), pw, seg_f, seg_b, disc_vjp))

    h = x.reshape(N, D)
    saved = []
    k_all = v_all = None
    shift, scale, gate = mod3(0, 0)
    u = _normmod(h, ln_g_full[0, 0].reshape(1, D), scale, shift, B)
    for l in range(depth):
        sv = {}
        sv["h0"], sv["scale0"], sv["gate0"], sv["u0"] = h, scale, gate, u
        shift1, scale1, gate1 = mod3(l, 1)
        norm1 = [("n", ln_g_full[l, 1].reshape(1, D)), ("ex", scale1), ("ex", shift1)]
        carry, used = fetch(fetch_with[l])
        if l < n_a:
            bd, cd, pw, seg_f, _, _ = s5_tabs[l]
            z, carries, moved = _s5_fwd(u, bd, cd, pw, seg_f, ssm_d_full[l].reshape(1, D), B, carry)
            landed(wg, used, moved)
            zz = _mm("glu_proj", z, wg[GLU], mode="nn", M=N, N=2 * D, K=D, b_lay="cs", b_l=l, b_ns=N_CHIPS)
            y, h, u = _glu_residual_norm(zz, h, gate, ln_g_full[l, 1].reshape(1, D), scale1, shift1, B)
            sv["z"], sv["carries"], sv["zz"] = z, carries, zz
        else:
            j = l - n_a
            q = _mm("q_proj", u, wg[Q], mode="nn", M=N, N=3 * D, K=D, b_lay="cs", b_l=j, b_ns=N_CHIPS)
            q3 = q.reshape(B, S, 3 * D)
            o, lse, moved = _attn_fwd(q3, k_all, v_all, B, carry)
            landed(wg, used, moved)
            o2 = o.reshape(N, D)
            y, h, u = _mm("o_proj", o2, whole_rows(O), mode="nn", M=N, N=D, K=D, b_l=j, out_dtype=(BF16, F32, BF16),
                          epi=_gated_residual_norm, extras=[("mn", h), ("ex", gate)] + norm1, rows_per_ex=S)
            sv["q"], sv["o"], sv["lse"] = q3, o, lse
        sv["y0"] = y
        sv["h1"], sv["scale1"], sv["gate1"], sv["u1"] = h, scale1, gate1, u
        r = _mm("mlp_up", u, wg[W1], mode="nn", M=N, N=FF, K=D, b_lay="cs", b_l=l, b_ns=N_CHIPS, out_dtype=BF16, epi=_relu2)
        if l + 1 < depth:
            shift, scale, gate = mod3(l + 1, 0)
            y, h, u = _mm("mlp_down", r, whole_rows(W2), mode="nn", M=N, N=D, K=FF, b_l=l, tk=2048, tm=512,
                          out_dtype=(BF16, F32, BF16), epi=_gated_residual_norm, rows_per_ex=S,
                          extras=[("mn", h), ("ex", gate1), ("n", ln_g_full[l + 1, 0].reshape(1, D)), ("ex", scale), ("ex", shift)])
        else:
            y, h = _mm("mlp_down", r, whole_rows(W2), mode="nn", M=N, N=D, K=FF, b_l=l, tk=2048,
                       out_dtype=(BF16, F32), epi=_gated_residual, extras=[("mn", h), ("ex", gate1)], rows_per_ex=S)
        sv["r"], sv["y1"] = r, y
        saved.append(sv)
        if l == n_a - 1:
            h_kv = h
            u_kv = _normmod(h, kv_g.reshape(1, D), kv_scale, kv_shift, B)
            half = N_CHIPS // 2
            k_all = _mm("kv_proj", u_kv, wg[KV], mode="nn", M=N, N=3 * D, K=D, b_lay="cs", b_s0=0, b_ns=half).reshape(B, S, 3 * D)
            v_all = _mm("kv_proj", u_kv, wg[KV], mode="nn", M=N, N=3 * D, K=D, b_lay="cs", b_s0=half, b_ns=half).reshape(B, S, 3 * D)

    loss_buf, dh, d_final_g = _loss_head(h, final_g.reshape(1, D), loss_target.reshape(N, D))
    loss = lax.psum(loss_buf[0, 0], ("x", "y", "c"))

    dg = [lax.empty((N_CHIPS,) + s.shape, BF16) for s in shards]
    recv = [lax.empty((N_CHIPS,) + s.shape, BF16) for s in shards]

    def send(pieces):
        return _layer_moves("scatter", dg, recv, pieces)

    send_with = {l: [(W1, l), (W2, l)] for l in range(depth)}
    for l in range(n_a):
        send_with[l] += [(GLU, l)]
    for j in range(n_b):
        send_with[n_a + j] += [(O, j)]
        send_with[n_a + j - 1] += [(Q, j)]
    send_with[n_a - 1] += [(KV, 0)]
    d_ln_g = [[None, None] for _ in range(depth)]
    d_mods = [[None, None] for _ in range(depth)]
    d_s5 = [None] * n_a
    dk_acc = dv_acc = None
    half = N_CHIPS // 2

    def tn_grad(name, a, d, into, l, Mr, Nc, lay, s0=0, ns=N_CHIPS):
        return _mm(name, a, _as4(d), mode="tn", M=Mr, N=Nc, K=N, b_lay="cs", out_dtype=BF16, out_lay=lay,
                   out4_shape=into.shape, out_into=into, out_l=l, out_s0=s0, out_ns=ns, tk=2048)

    dy, d_gate1 = _residual_bwd(dh, saved[-1]["gate1"], saved[-1]["y1"], B)
    for l in reversed(range(depth)):
        sv = saved[l]
        dg[W2] = tn_grad("mlp_down_dw", sv["r"], dy, dg[W2], l, FF, D, "rs")
        da = _mm("mlp_down_dx", dy, whole_rows(W2), mode="nt", M=N, N=FF, K=D, b_l=l, out_dtype=BF16,
                 epi=_relu2_bwd, extras=[("mn", sv["r"])])
        dg[W1] = tn_grad("mlp_up_dw", sv["u1"], da, dg[W1], l, D, FF, "cs")
        du = _mm("mlp_up_dx", da, wg[W1], mode="nt", M=N, N=D, K=FF, b_lay="cs", b_l=l, b_ns=N_CHIPS)
        dh, dgv, d_scale1, d_shift1, dy, d_gate0 = _normmod_bwd(du, sv["h1"], ln_g_full[l, 1].reshape(1, D), sv["scale1"],
                                                                dh, B, below=(sv["y0"], sv["gate0"]))
        d_ln_g[l][1] = dgv
        d_mods[l][1] = jnp.concatenate([d_shift1, d_scale1, d_gate1], axis=2)
        if l < n_a:
            bd, cd, pw, seg_f, seg_b, disc_vjp = s5_tabs[l]
            dzz = _glu_bwd(dy, sv["zz"])
            dg[GLU] = tn_grad("glu_proj_dw", sv["z"], dzz, dg[GLU], l, D, 2 * D, "cs")
            dz = _mm("glu_proj_dx", dzz, wg[GLU], mode="nt", M=N, N=D, K=2 * D, b_lay="cs", b_l=l, b_ns=N_CHIPS)
            carry, used = send(send_with[l])
            du, d_bd, d_cd, d_a2, d_dskip, moved = _s5_bwd(sv["u0"], dz, bd, cd, pw, seg_f, seg_b,
                                                           ssm_d_full[l].reshape(1, D), sv["carries"], B, carry)
            landed(recv, used, moved)
            d_are = (d_a2[:, 0, :CHUNK_STATE] + d_a2[:, 0, CHUNK_STATE:]).reshape(-1, SSM_STATE)
            d_aim = (d_a2[:, 1, CHUNK_STATE:] - d_a2[:, 1, :CHUNK_STATE]).reshape(-1, SSM_STATE)
            d_s5[l] = disc_vjp((d_bd, d_cd, d_are, d_aim)) + (d_dskip,)
        else:
            j = l - n_a
            dg[O] = tn_grad("o_proj_dw", sv["o"].reshape(N, D), dy, dg[O], j, D, D, "rs")
            do = _mm("o_proj_dx", dy, whole_rows(O), mode="nt", M=N, N=D, K=D, b_l=j)
            carry, used = send(send_with[l])
            dq, dk_acc, dv_acc, moved = _attn_bwd(sv["q"], k_all, v_all, sv["o"], sv["lse"], do.reshape(B, S, D),
                                                  dk_acc, dv_acc, B, l == n_a, carry)
            landed(recv, used, moved)
            dq2 = dq.reshape(N, 3 * D)
            dg[Q] = tn_grad("q_proj_dw", sv["u0"], dq2, dg[Q], j, D, 3 * D, "cs")
            du = _mm("q_proj_dx", dq2, wg[Q], mode="nt", M=N, N=D, K=3 * D, b_lay="cs", b_l=j, b_ns=N_CHIPS)
        below = (saved[l - 1]["y1"], saved[l - 1]["gate1"]) if l > 0 else None
        dh, dgv, d_scale0, d_shift0, dy, d_gate1 = _normmod_bwd(du, sv["h0"], ln_g_full[l, 0].reshape(1, D), sv["scale0"],
                                                                dh, B, below=None if l == n_a else below)
        d_ln_g[l][0] = dgv
        d_mods[l][0] = jnp.concatenate([d_shift0, d_scale0, d_gate0], axis=2)
        if l == n_a:
            dk2, dv2 = dk_acc.reshape(N, 3 * D), dv_acc.reshape(N, 3 * D)
            dg[KV] = tn_grad("kv_proj_dw", u_kv, dk2, dg[KV], 0, D, 3 * D, "cs", s0=0, ns=half)
            dg[KV] = tn_grad("kv_proj_dw", u_kv, dv2, dg[KV], 0, D, 3 * D, "cs", s0=half, ns=half)
            du_kv = _mm("kv_proj_dx", dk2, wg[KV], mode="nt", M=N, N=D, K=3 * D, b_lay="cs", b_s0=0, b_ns=half)
            du_kv = _mm("kv_proj_dx", dv2, wg[KV], mode="nt", M=N, N=D, K=3 * D, b_lay="cs", b_s0=half, b_ns=half,
                        epi=_add, extras=[("mn", du_kv)])
            dh, d_kv_g, d_kv_scale, d_kv_shift, dy, d_gate1 = _normmod_bwd(du_kv, h_kv, kv_g.reshape(1, D), kv_scale, dh, B,
                                                                           below=below)
    grad_x = dh.reshape(B, S, D)

    own = [_sum_shards(r.reshape(N_CHIPS, -1, r.shape[-1])) for r in recv]

    d_kvmod = jnp.concatenate([d_kv_shift, d_kv_scale], axis=2).reshape(B, 2 * D)
    d_mod_all = jnp.concatenate([d_mods[l][s].reshape(B, 3 * D) for l in range(depth) for s in range(2)], axis=1)
    small = [
        d_mod_all, d_kvmod,
        jnp.stack([jnp.stack([d_ln_g[l][0].reshape(D), d_ln_g[l][1].reshape(D)]) for l in range(depth)]),
        jnp.stack([d_s5[l][0] for l in range(n_a)]), jnp.stack([d_s5[l][1] for l in range(n_a)]),
        jnp.stack([d_s5[l][2] for l in range(n_a)]),
        jnp.stack([d_s5[l][3] for l in range(n_a)]), jnp.stack([d_s5[l][4] for l in range(n_a)]),
        jnp.stack([d_s5[l][5] for l in range(n_a)]), jnp.stack([d_s5[l][6] for l in range(n_a)]),
        jnp.stack([d_s5[l][7].reshape(D) for l in range(n_a)]),
        d_kv_g.reshape(D), d_final_g.reshape(D),
    ]
    small_pack, small_layout, small_rows = _pack(small, D)
    small_buf, other = _all_gather_small(small_pack, _swap_with_sibling(own))
    small_sum = _sum_shards(small_buf.reshape(N_DEV, small_rows, D))
    red = lambda idx: _unpack(small_sum, small_layout, idx)
    per_dev = lambda idx: jnp.concatenate(
        [_unpack(small_buf[d * small_rows:(d + 1) * small_rows], small_layout, idx) for d in range(N_DEV)], axis=0)

    dm_all = per_dev(0).reshape(n_ex, n_mod, 3 * D)
    dm_cols = lax.dynamic_slice_in_dim(dm_all, chip * ada_cols, ada_cols, axis=2).reshape(n_ex, n_mod * ada_cols)
    g_ada_w = _mm("ada_dw", sc_all, _as4(dm_cols), mode="tn", M=D, N=n_mod * ada_cols, K=n_ex, b_lay="cs",
                  out_lay="cs", out4_shape=(n_mod, 1, D, ada_cols), out_ns=n_mod).reshape(ada_w.shape)
    dkvm_all = per_dev(1)
    dkvm_cols = lax.dynamic_slice_in_dim(dkvm_all, chip * kv_cols, kv_cols, axis=1)
    g_kv_ada_w = _mm("ada_dw", sc_all, _as4(dkvm_cols), mode="tn", M=D, N=kv_cols, K=n_ex, b_lay="cs")
    g_ada_b_full = (red(0)[0] + red(0)[1]).reshape(depth, 2, 3 * D) if B == 2 else jnp.sum(red(0), axis=0).reshape(depth, 2, 3 * D)
    g_ada_b = lax.dynamic_slice_in_dim(g_ada_b_full, chip * ada_cols, ada_cols, axis=2)
    g_kv_ada_b = red(1)[0] + red(1)[1] if B == 2 else jnp.sum(red(1), axis=0)
    g_ln_g = lax.dynamic_slice_in_dim(red(2), chip * (D // N_CHIPS), D // N_CHIPS, axis=2)
    g_ssm_d = lax.dynamic_slice_in_dim(red(10), chip * (D // N_CHIPS), D // N_CHIPS, axis=1)
    small_grads = {
        "ln_g": g_ln_g, "ada_b": g_ada_b, "ssm_lam_re": red(3), "ssm_lam_im": red(4), "ssm_log_dt": red(5),
        "ssm_b_re": red(6), "ssm_b_im": red(7), "ssm_c_re": red(8), "ssm_c_im": red(9), "ssm_d": g_ssm_d,
        "kv_g": red(11), "kv_ada_b": g_kv_ada_b, "final_g": red(12),
    }
    small_w = {"ln_g": (ln_g, m_ln_g, v_ln_g), "ada_b": (ada_b, m_ada_b, v_ada_b),
               "ssm_lam_re": (ssm_lam_re, m_ssm_lam_re, v_ssm_lam_re), "ssm_lam_im": (ssm_lam_im, m_ssm_lam_im, v_ssm_lam_im),
               "ssm_log_dt": (ssm_log_dt, m_ssm_log_dt, v_ssm_log_dt), "ssm_b_re": (ssm_b_re, m_ssm_b_re, v_ssm_b_re),
               "ssm_b_im": (ssm_b_im, m_ssm_b_im, v_ssm_b_im), "ssm_c_re": (ssm_c_re, m_ssm_c_re, v_ssm_c_re),
               "ssm_c_im": (ssm_c_im, m_ssm_c_im, v_ssm_c_im), "ssm_d": (ssm_d, m_ssm_d, v_ssm_d),
               "kv_g": (kv_g, m_kv_g, v_kv_g), "kv_ada_b": (kv_ada_b, m_kv_ada_b, v_kv_ada_b),
               "final_g": (final_g, m_final_g, v_final_g)}
    names = list(small_w)
    wp, lay_w, _ = _pack([small_w[n][0] for n in names], D)
    gp, _, _ = _pack([small_grads[n] for n in names], D)
    mp, _, _ = _pack([small_w[n][1] for n in names], D)
    vp, _, _ = _pack([small_w[n][2] for n in names], D)
    _, d_p, m_p, v_p = _adamw(wp, [gp], mp, vp)
    upd = {n: (small_grads[n].reshape(small_w[n][0].shape), _unpack(d_p, lay_w, i), _unpack(m_p, lay_w, i), _unpack(v_p, lay_w, i))
           for i, n in enumerate(names)}

    def big(w, m, v, g_own, g_other=None):
        C = w.shape[-1]
        gs = [g_own.reshape(-1, C)] + ([g_other.reshape(-1, C)] if g_other is not None else [])
        return tuple(t.reshape(w.shape) for t in _adamw(w.reshape(-1, C), gs, m.reshape(-1, C), v.reshape(-1, C)))

    upd["ssm_w_glu"] = big(ssm_w_glu, m_ssm_w_glu, v_ssm_w_glu, own[0], other[0])
    upd["w_kv"] = big(w_kv, m_w_kv, v_w_kv, own[1], other[1])
    upd["attn_w_q"] = big(attn_w_q, m_attn_w_q, v_attn_w_q, own[2], other[2])
    upd["attn_w_o"] = big(attn_w_o, m_attn_w_o, v_attn_w_o, own[3], other[3])
    upd["mlp_w1"] = big(mlp_w1, m_mlp_w1, v_mlp_w1, own[4], other[4])
    upd["mlp_w2"] = big(mlp_w2, m_mlp_w2, v_mlp_w2, own[5], other[5])
    upd["ada_w"] = big(ada_w, m_ada_w, v_ada_w, g_ada_w)
    upd["kv_ada_w"] = big(kv_ada_w, m_kv_ada_w, v_kv_ada_w, g_kv_ada_w)

    order = ["ln_g", "ada_w", "ada_b", "ssm_lam_re", "ssm_lam_im", "ssm_log_dt", "ssm_b_re", "ssm_b_im", "ssm_c_re",
             "ssm_c_im", "ssm_d", "ssm_w_glu", "kv_g", "kv_ada_w", "kv_ada_b", "w_kv", "attn_w_q", "attn_w_o", "mlp_w1",
             "mlp_w2", "final_g"]
    return (loss, grad_x, *[upd[n][0] for n in order], *[upd[n][1] for n in order], *[upd[n][2] for n in order],
            *[upd[n][3] for n in order])
```

```python
import functools
import math

import jax
import jax.numpy as jnp
from jax import lax
from jax.experimental import pallas as pl
from jax.experimental.pallas import tpu as pltpu

F32 = jnp.float32
BF16 = jnp.bfloat16
MESH = pl.DeviceIdType.MESH

EPS = 1e-6
NEG = -1e30
SSM_GROUP = 16
SSM_STATE = 64
HEAD_DIM = 64
ATTN_BLOCK = 128
DILATIONS = (1, 4, 16)
ADAM_LR, ADAM_B1, ADAM_B2, ADAM_EPS, ADAM_WD, ADAM_STEP = 0.001, 0.9, 0.999, 1e-08, 0.01, 10

LANES = 128
SUBLANES = 8
CHUNK_GROUPS = LANES // SSM_GROUP
CHUNK_STATE = CHUNK_GROUPS * SSM_STATE
VMEM_LIMIT = 56 * 1024 * 1024


def _div(dim, pref, mult):
    t = min(pref, dim) // mult * mult
    while t >= mult:
        if dim % t == 0:
            return t
        t -= mult
    return dim


def _params(*sem):
    return pltpu.CompilerParams(dimension_semantics=sem, vmem_limit_bytes=VMEM_LIMIT)


def _coords():
    return lax.axis_index("x"), lax.axis_index("y"), lax.axis_index("c")


def _other_chips(cx, cy):
    return [(1 - cx, cy), (cx, 1 - cy), (1 - cx, 1 - cy)]


class _Carry:
    def __init__(self, srcs, dsts, plan, n_remote, n_local, onward=None, n_onward=0):
        self.srcs, self.dsts, self.plan, self.n_remote, self.n_local = list(srcs), list(dsts), plan, n_remote, n_local
        self.onward, self.n_onward = onward, n_onward


def _carried_call(body, *, name, grid, in_specs, out_specs, out_shape, scratch_shapes, operands, sem, carry=None):
    if carry is None:
        outs = pl.pallas_call(body, name=name, grid=grid, in_specs=in_specs, out_specs=out_specs, out_shape=out_shape,
                              scratch_shapes=scratch_shapes, compiler_params=_params(*sem))(*operands)
        return list(outs), []
    n_in, n_out, n_scr = len(in_specs), len(out_specs), len(scratch_shapes)
    ns, nd = len(carry.srcs), len(carry.dsts)

    def wrapped(*refs):
        base_in, src_refs = refs[:n_in], refs[n_in:n_in + ns]
        o0 = n_in + ns + nd
        base_out, dst_refs = refs[o0:o0 + n_out], refs[o0 + n_out:o0 + n_out + nd]
        s0 = o0 + n_out + nd
        base_scr = refs[s0:s0 + n_scr]
        send_sems, recv_sems, local_sems = refs[s0 + n_scr:]
        pids = [pl.program_id(a) for a in range(len(grid))]
        first = functools.reduce(jnp.logical_and, [p == 0 for p in pids])
        last = functools.reduce(jnp.logical_and, [p == g - 1 for p, g in zip(pids, grid)])

        def remote_copies(moves, k0):
            return [pltpu.make_async_remote_copy(src_ref=s, dst_ref=d, send_sem=send_sems.at[k0 + i], recv_sem=recv_sems.at[k0 + i],
                                                 device_id=peer, device_id_type=MESH) for i, (s, d, peer) in enumerate(moves)]

        def copies():
            remote, local = carry.plan(src_refs, dst_refs, _coords())
            return remote_copies(remote, 0), [pltpu.make_async_copy(s, d, local_sems.at[i]) for i, (s, d) in enumerate(local)]

        @pl.when(first)
        def _():
            remote, local = copies()
            for cp in local + remote:
                cp.start()

        body(*base_in, *base_out, *base_scr)

        @pl.when(last)
        def _():
            remote, local = copies()
            for cp in remote:
                cp.wait_send()
                cp.wait_recv()
            for cp in local:
                cp.wait()
            if carry.onward is not None:
                second = remote_copies(carry.onward(src_refs, dst_refs, _coords()), carry.n_remote)
                for cp in second:
                    cp.start()
                for cp in second:
                    cp.wait_send()
                    cp.wait_recv()

    anyspec = pl.BlockSpec(memory_space=pl.ANY)
    outs = pl.pallas_call(
        wrapped, name=name, grid=grid, in_specs=list(in_specs) + [anyspec] * (ns + nd),
        out_specs=list(out_specs) + [anyspec] * nd,
        out_shape=list(out_shape) + [jax.ShapeDtypeStruct(d.shape, d.dtype) for d in carry.dsts],
        scratch_shapes=list(scratch_shapes) + [pltpu.SemaphoreType.DMA((carry.n_remote + carry.n_onward,)),
                                               pltpu.SemaphoreType.DMA((carry.n_remote + carry.n_onward,)),
                                               pltpu.SemaphoreType.DMA((max(carry.n_local, 1),))],
        input_output_aliases={n_in + ns + i: n_out + i for i in range(nd)},
        compiler_params=_params(*(["arbitrary"] * len(grid))),
    )(*operands, *carry.srcs, *carry.dsts)
    return list(outs[:n_out]), list(outs[n_out:])


def _mm(name, a, b4, *, mode, M, N, K, b_lay="cs", b_l=0, b_s0=0, b_ns=1, out_dtype=F32, out_lay=None, out4_shape=None,
        out_into=None, out_l=0, out_s0=0, out_ns=1, epi=None, extras=(), rows_per_ex=None, tm=1024, tn=1024, tk=1024,
        carry=None):
    _, _, bR, bC = b4.shape
    tm = _div(M, tm, SUBLANES if M % 16 else 16)
    brows, bcols = (N, K) if mode == "nt" else (K, N)
    if b_lay == "cs":
        assert bR == brows and bC * b_ns == bcols, (name, b4.shape, brows, bcols)
    else:
        assert bC == bcols and bR * b_ns == brows, (name, b4.shape, brows, bcols)
    n_lim = N
    k_lim = K
    if mode == "nt":
        if b_lay == "cs":
            k_lim = bC
        else:
            n_lim = bR
    else:
        if b_lay == "cs":
            n_lim = bC
        else:
            k_lim = bR
    if out_lay == "cs":
        oR, oC = out4_shape[2], out4_shape[3]
        assert oR == M and oC * out_ns == N, (name, out4_shape, M, N)
        n_lim = math.gcd(n_lim, oC)
    elif out_lay == "rs":
        oR, oC = out4_shape[2], out4_shape[3]
        assert oC == N and oR * out_ns == M, (name, out4_shape, M, N)
        tm = _div(oR, tm, SUBLANES)
    tn = _div(n_lim, tn, LANES)
    tk = _div(k_lim, tk, LANES if mode != "tn" else SUBLANES)
    if mode == "tn":
        tk = _div(k_lim, tk, 16) if k_lim % 16 == 0 else tk
    nk = K // tk
    grid = (M // tm, N // tn, nk)

    if mode == "tn":
        a_spec = pl.BlockSpec((tk, tm), lambda i, j, k: (k, i))
    else:
        a_spec = pl.BlockSpec((tm, tk), lambda i, j, k: (i, k))

    def b_index(ri, ci, br, bc):
        if b_lay == "cs":
            per = bC // bc
            return (b_s0 + ci // per, b_l, ri, ci % per)
        per = bR // br
        return (b_s0 + ri // per, b_l, ri % per, ci)

    if mode == "nt":
        b_spec = pl.BlockSpec((None, None, tn, tk), lambda i, j, k: b_index(j, k, tn, tk))
    else:
        b_spec = pl.BlockSpec((None, None, tk, tn), lambda i, j, k: b_index(k, j, tk, tn))

    in_specs = [a_spec, b_spec]
    operands = [a, b4]
    for kind, arr in extras:
        if kind == "mn":
            in_specs.append(pl.BlockSpec((tm, tn), lambda i, j, k: (i, j)))
        elif kind == "ex":
            per_ex = rows_per_ex // tm
            in_specs.append(pl.BlockSpec((None, 1, tn), lambda i, j, k: (i // per_ex, 0, j)))
        else:
            in_specs.append(pl.BlockSpec((1, tn), lambda i, j, k: (0, j)))
        operands.append(arr)
    n_extra = len(extras)

    multi = isinstance(out_dtype, tuple)
    n_out = len(out_dtype) if multi else 1
    if out_lay is None:
        out_shape = [jax.ShapeDtypeStruct((M, N), dt) for dt in (out_dtype if multi else (out_dtype,))]
        out_spec = [pl.BlockSpec((tm, tn), lambda i, j, k: (i, j)) for _ in range(n_out)]
    else:
        out_shape = [jax.ShapeDtypeStruct(tuple(out4_shape), out_dtype)]
        if out_lay == "cs":
            per_o = oC // tn
            out_spec = [pl.BlockSpec((None, None, tm, tn), lambda i, j, k: (out_s0 + j // per_o, out_l, i, j % per_o))]
        else:
            per_o = oR // tm
            out_spec = [pl.BlockSpec((None, None, tm, tn), lambda i, j, k: (out_s0 + i // per_o, out_l, i % per_o, j))]
    aliases = {}
    if out_into is not None:
        in_specs.append(pl.BlockSpec(memory_space=pl.ANY))
        operands.append(out_into)
        aliases = {len(operands) - 1: 0}

    dims = {"nn": (((1,), (0,)), ((), ())), "nt": (((1,), (1,)), ((), ())), "tn": (((0,), (0,)), ((), ()))}[mode]

    def body(a_ref, b_ref, *rest):
        extra_refs = rest[:n_extra]
        o_refs = rest[len(rest) - n_out - (nk > 1):len(rest) - (nk > 1)]

        def finish(r):
            if epi is not None:
                r = epi(r, *[e[...] for e in extra_refs])
            for o_ref, val in zip(o_refs, r if multi else (r,)):
                o_ref[...] = val.astype(o_ref.dtype)

        part = lax.dot_general(a_ref[...].astype(BF16), b_ref[...].astype(BF16), dims, preferred_element_type=F32)
        if nk == 1:
            finish(part)
            return
        acc = rest[-1]
        k = pl.program_id(2)

        @pl.when(k == 0)
        def _():
            acc[...] = part

        @pl.when(k != 0)
        def _():
            acc[...] += part

        @pl.when(k == nk - 1)
        def _():
            finish(acc[...])

    scratch = [pltpu.VMEM((tm, tn), F32)] if nk > 1 else []
    if carry is not None:
        assert out_into is None, name
        outs, moved = _carried_call(body, name=name, grid=grid, in_specs=in_specs, out_specs=out_spec, out_shape=out_shape,
                                    scratch_shapes=scratch, operands=operands, sem=("arbitrary",) * 3, carry=carry)
        return (tuple(outs) if multi else outs[0]), moved
    outs = pl.pallas_call(
        body, name=name, grid=grid, in_specs=in_specs, out_specs=out_spec, out_shape=out_shape,
        scratch_shapes=scratch, input_output_aliases=aliases,
        compiler_params=_params("parallel", "parallel", "arbitrary"),
    )(*operands)
    return tuple(outs) if multi else outs[0]


def _as4(w):
    return w.reshape((1, 1) + w.shape)


def _relu2(acc):
    r = jnp.maximum(acc, 0.0)
    return r * r


def _relu2_bwd(acc, r):
    return acc * (2.0 * jnp.sqrt(r.astype(F32)))


def _add(acc, e):
    return acc + e


def _gated_residual(acc, h, gate):
    return acc, h + gate * acc


def _modulated_norm(x, g, scale, shift):
    rstd = lax.rsqrt(jnp.mean(x * x, axis=-1, keepdims=True) + EPS)
    return ((x * rstd) * g) * (1.0 + scale) + shift


def _gated_residual_norm(acc, h, gate, g, scale, shift):
    h_new = h + gate * acc
    return acc, h_new, _modulated_norm(h_new, g, scale, shift)


def _row_tiles(N, B, pref=256):
    S = N // B
    tm = _div(S, pref, SUBLANES)
    return tm, S // tm


def _normmod(h, g, scale, shift, B):
    N, D = h.shape
    tm, per_ex = _row_tiles(N, B)

    def body(h_ref, g_ref, sc_ref, sh_ref, u_ref):
        u_ref[...] = _modulated_norm(h_ref[...], g_ref[...], sc_ref[...], sh_ref[...]).astype(u_ref.dtype)

    tok = pl.BlockSpec((tm, D), lambda i: (i, 0))
    vec = pl.BlockSpec((1, D), lambda i: (0, 0))
    ex = pl.BlockSpec((None, 1, D), lambda i: (i // per_ex, 0, 0))
    return pl.pallas_call(
        body, name="normmod_fwd", grid=(N // tm,), in_specs=[tok, vec, ex, ex], out_specs=tok,
        out_shape=jax.ShapeDtypeStruct((N, D), BF16), compiler_params=_params("parallel"),
    )(h, g, scale, shift)


def _normmod_bwd(du, h, g, scale, dh_in, B, below=None):
    N, D = h.shape
    tm, per_ex = _row_tiles(N, B)
    fused = below is not None

    def body(*refs):
        du_ref, h_ref, g_ref, sc_ref, dhin_ref = refs[:5]
        dh_ref, dg_ref, dsc_ref, dsh_ref = refs[5 + 2 * fused:9 + 2 * fused]
        i = pl.program_id(0)
        x = h_ref[...]
        gv = g_ref[...]
        d_u = du_ref[...].astype(F32)
        rstd = lax.rsqrt(jnp.mean(x * x, axis=-1, keepdims=True) + EPS)
        xn = x * rstd
        dyg = d_u * (1.0 + sc_ref[...])
        dxn = dyg * gv
        dh = dhin_ref[...] + rstd * (dxn - xn * jnp.mean(dxn * xn, axis=-1, keepdims=True))
        dh_ref[...] = dh
        sums = [(dsc_ref, jnp.sum(d_u * (xn * gv), axis=0, keepdims=True)), (dsh_ref, jnp.sum(d_u, axis=0, keepdims=True))]
        if fused:
            y_ref, gt_ref = refs[5:7]
            dy_ref, dgt_ref = refs[9 + 2 * fused:]
            dy_ref[...] = (gt_ref[...] * dh).astype(dy_ref.dtype)
            sums.append((dgt_ref, jnp.sum(dh * y_ref[...], axis=0, keepdims=True)))
        dg_t = jnp.sum(dyg * xn, axis=0, keepdims=True)

        @pl.when(i % per_ex == 0)
        def _():
            for ref, val in sums:
                ref[...] = val

        @pl.when(i % per_ex != 0)
        def _():
            for ref, val in sums:
                ref[...] += val

        @pl.when(i == 0)
        def _():
            dg_ref[...] = dg_t

        @pl.when(i != 0)
        def _():
            dg_ref[...] += dg_t

    tok = pl.BlockSpec((tm, D), lambda i: (i, 0))
    vec = pl.BlockSpec((1, D), lambda i: (0, 0))
    ex = pl.BlockSpec((None, 1, D), lambda i: (i // per_ex, 0, 0))
    per_ex_shape = jax.ShapeDtypeStruct((B, 1, D), F32)
    outs = pl.pallas_call(
        body, name="normmod_bwd", grid=(N // tm,), in_specs=[tok, tok, vec, ex, tok] + ([tok, ex] if fused else []),
        out_specs=[tok, vec, ex, ex] + ([tok, ex] if fused else []),
        out_shape=[jax.ShapeDtypeStruct((N, D), F32), jax.ShapeDtypeStruct((1, D), F32), per_ex_shape, per_ex_shape]
        + ([jax.ShapeDtypeStruct((N, D), BF16), per_ex_shape] if fused else []),
        compiler_params=_params("arbitrary"),
    )(du, h, g, scale, dh_in, *(below if fused else ()))
    return tuple(outs) if fused else tuple(outs) + (None, None)


def _residual_bwd(dh, gate, y, B):
    N, D = dh.shape
    tm, per_ex = _row_tiles(N, B)

    def body(dh_ref, gt_ref, y_ref, dy_ref, dgt_ref):
        i = pl.program_id(0)
        d = dh_ref[...]
        dy_ref[...] = (gt_ref[...] * d).astype(dy_ref.dtype)
        t = jnp.sum(d * y_ref[...], axis=0, keepdims=True)

        @pl.when(i % per_ex == 0)
        def _():
            dgt_ref[...] = t

        @pl.when(i % per_ex != 0)
        def _():
            dgt_ref[...] += t

    tok = pl.BlockSpec((tm, D), lambda i: (i, 0))
    ex = pl.BlockSpec((None, 1, D), lambda i: (i // per_ex, 0, 0))
    return pl.pallas_call(
        body, name="residual_bwd", grid=(N // tm,), in_specs=[tok, ex, tok], out_specs=[tok, ex],
        out_shape=[jax.ShapeDtypeStruct((N, D), BF16), jax.ShapeDtypeStruct((B, 1, D), F32)],
        compiler_params=_params("arbitrary"),
    )(dh, gate, y)


def _glu_residual_norm(zz, h, gate, g, scale, shift, B):
    N, D2 = zz.shape
    D = D2 // 2
    tm, per_ex = _row_tiles(N, B)

    def body(v_ref, g_ref, h_ref, gt_ref, ng_ref, sc_ref, sh_ref, y_ref, o_ref, u_ref):
        y = v_ref[...] * jax.nn.sigmoid(g_ref[...])
        y_ref[...] = y.astype(y_ref.dtype)
        h_new = h_ref[...] + gt_ref[...] * y
        o_ref[...] = h_new
        u_ref[...] = _modulated_norm(h_new, ng_ref[...], sc_ref[...], sh_ref[...]).astype(u_ref.dtype)

    tok = pl.BlockSpec((tm, D), lambda i: (i, 0))
    vec = pl.BlockSpec((1, D), lambda i: (0, 0))
    ex = pl.BlockSpec((None, 1, D), lambda i: (i // per_ex, 0, 0))
    return pl.pallas_call(
        body, name="glu_fwd", grid=(N // tm,),
        in_specs=[tok, pl.BlockSpec((tm, D), lambda i: (i, 1)), tok, ex, vec, ex, ex], out_specs=[tok, tok, tok],
        out_shape=[jax.ShapeDtypeStruct((N, D), BF16), jax.ShapeDtypeStruct((N, D), F32), jax.ShapeDtypeStruct((N, D), BF16)],
        compiler_params=_params("parallel"),
    )(zz, zz, h, gate, g, scale, shift)


def _glu_bwd(dy, zz):
    N, D2 = zz.shape
    D = D2 // 2
    tm = _div(N, 256, SUBLANES)

    def body(dy_ref, v_ref, g_ref, o_ref):
        d = dy_ref[...].astype(F32)
        s = jax.nn.sigmoid(g_ref[...])
        o_ref[...] = jnp.concatenate([d * s, d * v_ref[...] * s * (1.0 - s)], axis=1).astype(o_ref.dtype)

    return pl.pallas_call(
        body, name="glu_bwd", grid=(N // tm,),
        in_specs=[pl.BlockSpec((tm, D), lambda i: (i, 0)), pl.BlockSpec((tm, D), lambda i: (i, 0)),
                  pl.BlockSpec((tm, D), lambda i: (i, 1))],
        out_specs=pl.BlockSpec((tm, D2), lambda i: (i, 0)), out_shape=jax.ShapeDtypeStruct((N, D2), BF16),
        compiler_params=_params("parallel"),
    )(dy, zz, zz)


def _loss_head(h, g, target):
    N, D = h.shape
    tm = _div(N, 256, SUBLANES)

    def body(h_ref, g_ref, t_ref, loss_ref, dh_ref, dg_ref):
        i = pl.program_id(0)
        x = h_ref[...]
        gv = g_ref[...]
        rstd = lax.rsqrt(jnp.mean(x * x, axis=-1, keepdims=True) + EPS)
        xn = x * rstd
        err = xn * gv - t_ref[...]
        part = 0.5 * jnp.sum(jnp.sum(err * err, axis=-1, keepdims=True) / D, axis=0, keepdims=True)
        dy = err / D
        dxn = dy * gv
        dh_ref[...] = rstd * (dxn - xn * jnp.mean(dxn * xn, axis=-1, keepdims=True))
        dg_t = jnp.sum(dy * xn, axis=0, keepdims=True)
        part = jnp.broadcast_to(part, loss_ref.shape)

        @pl.when(i == 0)
        def _():
            loss_ref[...] = part
            dg_ref[...] = dg_t

        @pl.when(i != 0)
        def _():
            loss_ref[...] += part
            dg_ref[...] += dg_t

    tok = pl.BlockSpec((tm, D), lambda i: (i, 0))
    vec = pl.BlockSpec((1, D), lambda i: (0, 0))
    return pl.pallas_call(
        body, name="loss_head", grid=(N // tm,), in_specs=[tok, vec, tok],
        out_specs=[pl.BlockSpec((SUBLANES, LANES), lambda i: (0, 0)), tok, vec],
        out_shape=[jax.ShapeDtypeStruct((SUBLANES, LANES), F32), jax.ShapeDtypeStruct((N, D), F32),
                   jax.ShapeDtypeStruct((1, D), F32)],
        compiler_params=_params("arbitrary"),
    )(h, g, target)


def _swap_halves(x):
    half = x.shape[-1] // 2
    return jnp.concatenate([x[:, half:], x[:, :half]], axis=1)


def _gelu(y):
    return jax.nn.gelu(y)


def _gelu_grad(y):
    c0 = math.sqrt(2.0 / math.pi)
    inner = c0 * (y + 0.044715 * y * y * y)
    t = jnp.tanh(inner)
    return 0.5 * (1.0 + t) + 0.5 * y * (1.0 - t * t) * c0 * (1.0 + 3.0 * 0.044715 * y * y)


def _s5_discretize(lam_re, lam_im, log_dt, b_re, b_im, c_re, c_im):
    G = lam_re.shape[0]
    nch = G // CHUNK_GROUPS
    dt = jnp.exp(log_dt)[:, None]
    er = jnp.exp(lam_re * dt)
    a_re = er * jnp.cos(lam_im * dt)
    a_im = er * jnp.sin(lam_im * dt)
    den = lam_re * lam_re + lam_im * lam_im
    n_re, n_im = a_re - 1.0, a_im
    f_re = (n_re * lam_re + n_im * lam_im) / den
    f_im = (n_im * lam_re - n_re * lam_im) / den
    bb_re = f_re[..., None] * b_re - f_im[..., None] * b_im
    bb_im = f_re[..., None] * b_im + f_im[..., None] * b_re
    eye = jnp.eye(CHUNK_GROUPS, dtype=F32)

    def pack_b(bb):
        bb = bb.reshape(nch, CHUNK_GROUPS, SSM_STATE, SSM_GROUP)
        return jnp.einsum("jgpc,gh->jgchp", bb, eye).reshape(nch, LANES, CHUNK_STATE)

    def pack_c(cc):
        cc = cc.reshape(nch, CHUNK_GROUPS, SSM_GROUP, SSM_STATE)
        return jnp.einsum("jgcp,gh->jgphc", cc, eye).reshape(nch, CHUNK_STATE, LANES)

    bd = jnp.concatenate([pack_b(bb_re), pack_b(bb_im)], axis=2)
    cd = jnp.concatenate([pack_c(c_re), pack_c(-c_im)], axis=1)
    return bd, cd, a_re, a_im


S5_TILE = 1024
S5_SEG = S5_TILE // SUBLANES
S5_UNROLL = 4


def _s5_scan_coefs(lam_re, lam_im, log_dt, seg):
    G = lam_re.shape[0]
    nch = G // CHUNK_GROUPS
    dt = jnp.exp(log_dt)[:, None]
    rate = (lam_re * dt).reshape(nch, 1, CHUNK_STATE)
    freq = (lam_im * dt).reshape(nch, 1, CHUNK_STATE)

    def powers(ks):
        k = jnp.asarray(ks, F32)[None, :, None]
        er = jnp.exp(k * rate)
        re, im = er * jnp.cos(k * freq), er * jnp.sin(k * freq)
        return jnp.concatenate([re, re], axis=2), jnp.concatenate([-im, im], axis=2)

    pw = jnp.stack(powers(range(1, seg + 1)), axis=1)
    steps = (1, 2, 4)
    re, im = powers([s * seg for s in steps])
    row = jnp.arange(SUBLANES, dtype=jnp.int32)[None, None, :, None]
    shift = jnp.asarray(steps, jnp.int32)[None, :, None, None]

    def table(reverse):
        mask = (row < SUBLANES - shift) if reverse else (row >= shift)
        pair = jnp.stack([jnp.where(mask, re[:, :, None, :], 0.0),
                          jnp.where(mask, (-im if reverse else im)[:, :, None, :], 0.0)], axis=2)
        return pair.reshape(nch, 2 * len(steps), SUBLANES, 2 * CHUNK_STATE)

    return pw, table(False), table(True)


def _to_segments(dst_s, src_ref, seg):
    for j in range(SUBLANES):
        dst_s[pl.ds(j, seg, stride=SUBLANES), :] = src_ref[pl.ds(j * seg, seg), :].astype(F32)


def _from_segments(dst_ref, src_s, seg):
    for j in range(SUBLANES):
        dst_ref[pl.ds(j * seg, seg), :] = src_s[pl.ds(j, seg, stride=SUBLANES), :].astype(dst_ref.dtype)


def _seg_scan(x_ref, pw_ref, seg_ref, carry_ref, c_ref, seg, reverse):
    W = x_ref.shape[-1]
    tm = x_ref.shape[0]
    sgn = -1.0 if reverse else 1.0
    ar = jnp.broadcast_to(pw_ref[0, 0:1, :], (SUBLANES, W))
    ai = sgn * jnp.broadcast_to(pw_ref[1, 0:1, :], (SUBLANES, W))

    def rows(i):
        return pl.ds(pl.multiple_of(i * SUBLANES, SUBLANES), SUBLANES)

    def step(t, prev):
        i = (seg - 2 - t) if reverse else (t + 1)
        x = x_ref[rows(i), :] + ar * prev + ai * _swap_halves(prev)
        x_ref[rows(i), :] = x
        return x

    start = (seg - 1) * SUBLANES if reverse else 0
    edge = lax.fori_loop(0, seg - 1, step, x_ref[start:start + SUBLANES, :], unroll=S5_UNROLL)
    row = lax.broadcasted_iota(jnp.int32, (SUBLANES, W), 0)
    if reverse:
        f = jnp.where(row == SUBLANES - 1, carry_ref[...], pltpu.roll(edge, SUBLANES - 1, 0))
    else:
        f = jnp.where(row == 0, carry_ref[...], pltpu.roll(edge, 1, 0))
    for si, s in enumerate((1, 2, 4)):
        fs = pltpu.roll(f, (SUBLANES - s) if reverse else s, 0)
        f = f + seg_ref[2 * si] * fs + seg_ref[2 * si + 1] * _swap_halves(fs)
    c_ref[...] = f
    fsw = _swap_halves(f)

    def fix(i, _):
        k = (seg - 1 - i) if reverse else i
        x_ref[rows(i), :] = x_ref[rows(i), :] + pw_ref[0, pl.ds(k, 1), :] * f + (sgn * pw_ref[1, pl.ds(k, 1), :]) * fsw
        return 0

    lax.fori_loop(0, seg, fix, 0, unroll=S5_UNROLL)
    leaving = x_ref[0:1, :] if reverse else x_ref[tm - 1:tm, :]
    carry_ref[...] = jnp.broadcast_to(leaving, carry_ref.shape)


def _s5_fwd(u, bd, cd, pw, seg_f, d_skip, B, carry=None):
    N, D = u.shape
    S = N // B
    nch = D // LANES
    W = 2 * CHUNK_STATE
    tm, seg = S5_TILE, S5_SEG
    nt = S // tm

    def body(u_ref, bd_ref, cd_ref, pw_ref, seg_ref, d_ref, z_ref, cin_ref, x_s, carry, c_s, u_s, z_s):
        t = pl.program_id(2)

        @pl.when(t == 0)
        def _():
            carry[...] = jnp.zeros_like(carry)

        cin_ref[...] = carry[...]
        _to_segments(u_s, u_ref, seg)
        uf = u_s[...]
        x_s[...] = jnp.dot(uf.astype(BF16), bd_ref[...], preferred_element_type=F32)
        _seg_scan(x_s, pw_ref, seg_ref, carry, c_s, seg, False)
        y = jnp.dot(x_s[...].astype(BF16), cd_ref[...], preferred_element_type=F32) + d_ref[...] * uf
        z_s[...] = _gelu(y)
        _from_segments(z_ref, z_s, seg)

    (z, carries), moved = _carried_call(
        body, name="s5_fwd", grid=(nch, B, nt),
        in_specs=[pl.BlockSpec((tm, LANES), lambda j, b, t: (b * nt + t, j)),
                  pl.BlockSpec((None, LANES, W), lambda j, b, t: (j, 0, 0)),
                  pl.BlockSpec((None, W, LANES), lambda j, b, t: (j, 0, 0)),
                  pl.BlockSpec((None, 2, seg, W), lambda j, b, t: (j, 0, 0, 0)),
                  pl.BlockSpec((None, 6, SUBLANES, W), lambda j, b, t: (j, 0, 0, 0)),
                  pl.BlockSpec((1, LANES), lambda j, b, t: (0, j))],
        out_specs=[pl.BlockSpec((tm, LANES), lambda j, b, t: (b * nt + t, j)),
                   pl.BlockSpec((None, None, SUBLANES, W), lambda j, b, t: (j, b * nt + t, 0, 0))],
        out_shape=[jax.ShapeDtypeStruct((N, D), BF16), jax.ShapeDtypeStruct((nch, B * nt, SUBLANES, W), F32)],
        scratch_shapes=[pltpu.VMEM((tm, W), F32), pltpu.VMEM((SUBLANES, W), F32), pltpu.VMEM((SUBLANES, W), F32),
                        pltpu.VMEM((tm, LANES), F32), pltpu.VMEM((tm, LANES), F32)],
        operands=(u, bd, cd, pw, seg_f, d_skip), sem=("parallel", "arbitrary", "arbitrary"), carry=carry)
    return z, carries, moved


def _s5_bwd(u, dz, bd, cd, pw, seg_f, seg_b, d_skip, carries, B, carry=None):
    N, D = u.shape
    S = N // B
    nch = D // LANES
    W = 2 * CHUNK_STATE
    tm, seg = S5_TILE, S5_SEG
    nt = S // tm
    tn_dims = (((0,), (0,)), ((), ()))
    nt_dims = (((1,), (1,)), ((), ()))

    def body(u_ref, dz_ref, bd_ref, cd_ref, pw_ref, sf_ref, sb_ref, d_ref, cin_ref,
             du_ref, dbd_ref, dcd_ref, da_ref, dd_ref, x_s, l_s, carry, lcarry, c_s, lc_s, u_s, t_s):
        b = pl.program_id(1)
        t = pl.program_id(2)

        @pl.when((b == 0) & (t == 0))
        def _():
            dbd_ref[...] = jnp.zeros_like(dbd_ref)
            dcd_ref[...] = jnp.zeros_like(dcd_ref)
            da_ref[...] = jnp.zeros_like(da_ref)
            dd_ref[...] = jnp.zeros_like(dd_ref)

        @pl.when(t == 0)
        def _():
            lcarry[...] = jnp.zeros_like(lcarry)

        _to_segments(u_s, u_ref, seg)
        _to_segments(t_s, dz_ref, seg)
        uf = u_s[...]
        uv = uf.astype(BF16)
        carry[...] = cin_ref[...]
        x_s[...] = jnp.dot(uv, bd_ref[...], preferred_element_type=F32)
        _seg_scan(x_s, pw_ref, sf_ref, carry, c_s, seg, False)
        xb = x_s[...].astype(BF16)
        y = jnp.dot(xb, cd_ref[...], preferred_element_type=F32) + d_ref[...] * uf
        dy = t_s[...] * _gelu_grad(y)
        dd_ref[...] += jnp.sum(dy * uf, axis=0, keepdims=True)
        dyb = dy.astype(BF16)
        dcd_ref[...] += lax.dot_general(xb, dyb, tn_dims, preferred_element_type=F32)
        l_s[...] = lax.dot_general(dyb, cd_ref[...], nt_dims, preferred_element_type=F32)
        _seg_scan(l_s, pw_ref, sb_ref, lcarry, lc_s, seg, True)
        lb = l_s[...].astype(BF16)
        dbd_ref[...] += lax.dot_general(uv, lb, tn_dims, preferred_element_type=F32)
        t_s[...] = lax.dot_general(lb, bd_ref[...], nt_dims, preferred_element_type=F32) + d_ref[...] * dy
        _from_segments(du_ref, t_s, seg)
        lam_rest, x_prev = l_s[SUBLANES:, :], x_s[:tm - SUBLANES, :]
        lam_0, c_in = l_s[:SUBLANES, :], c_s[...]
        da_ref[0:1, :] += (jnp.sum(lam_rest * x_prev, axis=0, keepdims=True) + jnp.sum(lam_0 * c_in, axis=0, keepdims=True))
        da_ref[1:2, :] += (jnp.sum(lam_rest * _swap_halves(x_prev), axis=0, keepdims=True)
                           + jnp.sum(lam_0 * _swap_halves(c_in), axis=0, keepdims=True))

    tile = lambda j, b, t: (b * nt + (nt - 1 - t), j)
    chunk3 = lambda j, b, t: (j, 0, 0)
    chunk4 = lambda j, b, t: (j, 0, 0, 0)
    outs, moved = _carried_call(
        body, name="s5_bwd", grid=(nch, B, nt),
        in_specs=[pl.BlockSpec((tm, LANES), tile), pl.BlockSpec((tm, LANES), tile),
                  pl.BlockSpec((None, LANES, W), chunk3), pl.BlockSpec((None, W, LANES), chunk3),
                  pl.BlockSpec((None, 2, seg, W), chunk4), pl.BlockSpec((None, 6, SUBLANES, W), chunk4),
                  pl.BlockSpec((None, 6, SUBLANES, W), chunk4), pl.BlockSpec((1, LANES), lambda j, b, t: (0, j)),
                  pl.BlockSpec((None, None, SUBLANES, W), lambda j, b, t: (j, b * nt + (nt - 1 - t), 0, 0))],
        out_specs=[pl.BlockSpec((tm, LANES), tile), pl.BlockSpec((None, LANES, W), chunk3),
                   pl.BlockSpec((None, W, LANES), chunk3), pl.BlockSpec((None, 2, W), chunk3),
                   pl.BlockSpec((1, LANES), lambda j, b, t: (0, j))],
        out_shape=[jax.ShapeDtypeStruct((N, D), F32), jax.ShapeDtypeStruct((nch, LANES, W), F32),
                   jax.ShapeDtypeStruct((nch, W, LANES), F32), jax.ShapeDtypeStruct((nch, 2, W), F32),
                   jax.ShapeDtypeStruct((1, D), F32)],
        scratch_shapes=[pltpu.VMEM((tm, W), F32), pltpu.VMEM((tm, W), F32)] + [pltpu.VMEM((SUBLANES, W), F32)] * 4
        + [pltpu.VMEM((tm, LANES), F32)] * 2,
        operands=(u, dz, bd, cd, pw, seg_f, seg_b, d_skip, carries), sem=("parallel", "arbitrary", "arbitrary"), carry=carry)
    return (*outs, moved)


ATTN_HEADS = LANES // HEAD_DIM
ATTN_FWD_UNROLL = 4
ATTN_BWD_UNROLL = 4


def _attn_mask(n):
    qi = lax.broadcasted_iota(jnp.int32, (ATTN_BLOCK, 2 * ATTN_BLOCK), 0)
    kj = lax.broadcasted_iota(jnp.int32, (ATTN_BLOCK, 2 * ATTN_BLOCK), 1)
    prev_ok = (kj < ATTN_BLOCK) & (kj >= qi) & (n > 0)
    return prev_ok | ((kj >= ATTN_BLOCK) & (kj - ATTN_BLOCK <= qi))


def _head_lanes(h):
    lane = lax.broadcasted_iota(jnp.int32, (ATTN_BLOCK, LANES), 1)
    return (lane >= h * HEAD_DIM) & (lane < (h + 1) * HEAD_DIM)


def _per_head(cols):
    out = jnp.broadcast_to(cols[-1], (ATTN_BLOCK, LANES))
    for h in range(len(cols) - 2, -1, -1):
        out = jnp.where(_head_lanes(h), jnp.broadcast_to(cols[h], (ATTN_BLOCK, LANES)), out)
    return out


def _only_head(x, h):
    return jnp.where(_head_lanes(h), x, 0.0).astype(BF16)


def _block_rows(tb, dil, nb):
    r = tb // nb
    n = tb % nb
    start = r + dil * ATTN_BLOCK * n
    startp = jnp.where(n > 0, start - dil * ATTN_BLOCK, start)
    return n, pl.ds(start, ATTN_BLOCK, stride=dil), pl.ds(startp, ATTN_BLOCK, stride=dil)


def _attn_fwd(q, k, v, B, carry=None):
    _, S, D3 = q.shape
    D = D3 // 3
    HP = D // LANES
    scale = HEAD_DIM ** -0.5
    n_blocks = S // ATTN_BLOCK
    nbr = len(DILATIONS)
    nt_dims = (((1,), (1,)), ((), ()))

    def branch(dil, q_ref, k_ref, v_ref, acc, m_s, l_s):
        nb = (S // dil) // ATTN_BLOCK

        def blk(tb, _):
            n, rows, rowsp = _block_rows(tb, dil, nb)
            qb = q_ref[rows, :] * scale
            kk = jnp.concatenate([k_ref[rowsp, :], k_ref[rows, :]], axis=0).astype(BF16)
            vv = jnp.concatenate([v_ref[rowsp, :], v_ref[rows, :]], axis=0).astype(BF16)
            ok = _attn_mask(n)
            ms, ls, accs = [], [], []
            for h in range(ATTN_HEADS):
                s = lax.dot_general(_only_head(qb, h), kk, nt_dims, preferred_element_type=F32)
                s = jnp.where(ok, s, NEG)
                mh = jnp.max(s, axis=-1, keepdims=True)
                p = jnp.exp(s - mh)
                ms.append(mh)
                ls.append(jnp.sum(p, axis=-1, keepdims=True))
                accs.append(jnp.dot(p.astype(BF16), vv, preferred_element_type=F32))
            m_s[rows, :] = _per_head(ms)
            l_s[rows, :] = _per_head(ls)
            acc[rows, :] = _per_head(accs)
            return 0

        lax.fori_loop(0, n_blocks, blk, 0, unroll=ATTN_FWD_UNROLL)

    def body(q_ref, k_ref, v_ref, o_ref, lse_ref, *scratch):
        accs, m_ss, l_ss = scratch[:nbr], scratch[nbr:2 * nbr], scratch[2 * nbr:]
        g = pl.program_id(2)
        for gi, dil in enumerate(DILATIONS):
            pl.when(g == gi)(functools.partial(branch, dil, q_ref, k_ref, v_ref, accs[gi], m_ss[gi], l_ss[gi]))

        @pl.when(g == nbr - 1)
        def _():
            def fin(i, _):
                rows = pl.ds(pl.multiple_of(i * ATTN_BLOCK, ATTN_BLOCK), ATTN_BLOCK)
                ms = [m[rows, :] for m in m_ss]
                m_all = functools.reduce(jnp.maximum, ms)
                ws = [jnp.exp(m - m_all) for m in ms]
                den = sum(w * l[rows, :] for w, l in zip(ws, l_ss))
                o_ref[rows, :] = sum(w * a[rows, :] for w, a in zip(ws, accs)) / den
                lse_ref[rows, :] = m_all + jnp.log(den)
                return 0

            lax.fori_loop(0, n_blocks, fin, 0)

    br = pl.BlockSpec((None, S, LANES), lambda b, hp, g: (b, 0, g * HP + hp))
    hd = pl.BlockSpec((None, S, LANES), lambda b, hp, g: (b, 0, hp))
    (o, lse), moved = _carried_call(
        body, name="attn_fwd", grid=(B, HP, nbr), in_specs=[br, br, br], out_specs=[hd, hd],
        out_shape=[jax.ShapeDtypeStruct((B, S, D), F32), jax.ShapeDtypeStruct((B, S, D), F32)],
        scratch_shapes=[pltpu.VMEM((S, LANES), F32)] * (3 * nbr),
        operands=(q, k, v), sem=("parallel", "parallel", "arbitrary"), carry=carry)
    return o, lse, moved


def _attn_bwd(q, k, v, o, lse, do, dk_prev, dv_prev, B, last, carry=None):
    _, S, D3 = q.shape
    D = D3 // 3
    HP = D // LANES
    scale = HEAD_DIM ** -0.5
    n_blocks = S // ATTN_BLOCK
    has_prev = dk_prev is not None
    nt_dims = (((1,), (1,)), ((), ()))
    tn_dims = (((0,), (0,)), ((), ()))

    def branch(dil, q_ref, k_ref, v_ref, lse_ref, do_ref, dq_s, dk_c, dv_c, delta, dk_p, dv_p):
        nb = (S // dil) // ATTN_BLOCK

        def blk(tb, _):
            n, rows, rowsp = _block_rows(tb, dil, nb)
            qb = q_ref[rows, :] * scale
            dob, lb, db = do_ref[rows, :], lse_ref[rows, :], delta[rows, :]
            kk = jnp.concatenate([k_ref[rowsp, :], k_ref[rows, :]], axis=0).astype(BF16)
            vv = jnp.concatenate([v_ref[rowsp, :], v_ref[rows, :]], axis=0).astype(BF16)
            ok = _attn_mask(n)
            dqs = []
            dkk = dvv = None
            for h in range(ATTN_HEADS):
                qh, doh = _only_head(qb, h), _only_head(dob, h)
                lh = lb[:, h * HEAD_DIM:h * HEAD_DIM + 1]
                dlt = db[:, h * HEAD_DIM:h * HEAD_DIM + 1]
                s = lax.dot_general(qh, kk, nt_dims, preferred_element_type=F32)
                p = jnp.where(ok, jnp.exp(s - lh), 0.0)
                dp = lax.dot_general(doh, vv, nt_dims, preferred_element_type=F32)
                ds = (p * (dp - dlt)).astype(BF16)
                dqs.append(jnp.dot(ds, kk, preferred_element_type=F32))
                dk_h = lax.dot_general(ds, qh, tn_dims, preferred_element_type=F32)
                dv_h = lax.dot_general(p.astype(BF16), doh, tn_dims, preferred_element_type=F32)
                dkk = dk_h if dkk is None else dkk + dk_h
                dvv = dv_h if dvv is None else dvv + dv_h
            dq_s[rows, :] = _per_head(dqs) * scale
            dk_p[rowsp, :] = dkk[:ATTN_BLOCK]
            dv_p[rowsp, :] = dvv[:ATTN_BLOCK]
            dk_c[rows, :] = dkk[ATTN_BLOCK:]
            dv_c[rows, :] = dvv[ATTN_BLOCK:]
            return 0

        lax.fori_loop(0, n_blocks, blk, 0, unroll=ATTN_BWD_UNROLL)

    def body(*refs):
        q_ref, k_ref, v_ref, o_ref, lse_ref, do_ref = refs[:6]
        n_in = 8 if has_prev else 6
        dq_ref, dk_ref, dv_ref, delta, dk_p, dv_p, dq_s, dk_c, dv_c = refs[n_in:n_in + 9]
        g = pl.program_id(2)

        @pl.when(g == 0)
        def _():
            def dl(i, _):
                rows = pl.ds(pl.multiple_of(i * ATTN_BLOCK, ATTN_BLOCK), ATTN_BLOCK)
                prod = do_ref[rows, :] * o_ref[rows, :]
                delta[rows, :] = _per_head([jnp.sum(jnp.where(_head_lanes(h), prod, 0.0), axis=-1, keepdims=True)
                                            for h in range(ATTN_HEADS)])
                return 0

            lax.fori_loop(0, n_blocks, dl, 0)

        dk_p[...] = jnp.zeros_like(dk_p)
        dv_p[...] = jnp.zeros_like(dv_p)
        for gi, dil in enumerate(DILATIONS):
            pl.when(g == gi)(functools.partial(branch, dil, q_ref, k_ref, v_ref, lse_ref, do_ref, dq_s, dk_c, dv_c,
                                               delta, dk_p, dv_p))

        def fin(i, _):
            rows = pl.ds(pl.multiple_of(i * ATTN_BLOCK, ATTN_BLOCK), ATTN_BLOCK)
            dk_t = dk_c[rows, :] + dk_p[rows, :]
            dv_t = dv_c[rows, :] + dv_p[rows, :]
            if has_prev:
                dk_t = dk_t + refs[6][rows, :].astype(F32)
                dv_t = dv_t + refs[7][rows, :].astype(F32)
            dq_ref[rows, :] = dq_s[rows, :].astype(dq_ref.dtype)
            dk_ref[rows, :] = dk_t.astype(dk_ref.dtype)
            dv_ref[rows, :] = dv_t.astype(dv_ref.dtype)
            return 0

        lax.fori_loop(0, n_blocks, fin, 0)

    br = pl.BlockSpec((None, S, LANES), lambda b, hp, g: (b, 0, g * HP + hp))
    hd = pl.BlockSpec((None, S, LANES), lambda b, hp, g: (b, 0, hp))
    ins = [q, k, v, o, lse, do] + ([dk_prev, dv_prev] if has_prev else [])
    kv_dtype = BF16 if last else F32
    (dq, dk, dv), moved = _carried_call(
        body, name="attn_bwd", grid=(B, HP, len(DILATIONS)),
        in_specs=[br, br, br, hd, hd, hd] + ([br, br] if has_prev else []), out_specs=[br, br, br],
        out_shape=[jax.ShapeDtypeStruct(q.shape, BF16), jax.ShapeDtypeStruct(q.shape, kv_dtype),
                   jax.ShapeDtypeStruct(q.shape, kv_dtype)],
        scratch_shapes=[pltpu.VMEM((S, LANES), F32)] * 6,
        operands=ins, sem=("parallel", "parallel", "arbitrary"), carry=carry)
    return dq, dk, dv, moved


def _adamw(w, grads, m, v):
    R, C = w.shape
    tr = _div(R, 256, SUBLANES)
    ng = len(grads)
    c1 = 1.0 - ADAM_B1 ** ADAM_STEP
    c2 = 1.0 - ADAM_B2 ** ADAM_STEP

    def body(*refs):
        w_ref, m_ref, v_ref = refs[0], refs[1 + ng], refs[2 + ng]
        d_ref, mo_ref, vo_ref = refs[3 + ng:6 + ng]
        g = refs[1][...]
        if ng == 2:
            g = g + refs[2][...]
            refs[6 + ng][...] = g
        mn = ADAM_B1 * m_ref[...] + (1.0 - ADAM_B1) * g
        vn = ADAM_B2 * v_ref[...] + (1.0 - ADAM_B2) * (g * g)
        d_ref[...] = -ADAM_LR * ((mn / c1) / (jnp.sqrt(vn / c2) + ADAM_EPS) + ADAM_WD * w_ref[...])
        mo_ref[...] = mn
        vo_ref[...] = vn

    blk = pl.BlockSpec((tr, C), lambda i: (i, 0))
    n_out = 3 + (ng == 2)
    outs = pl.pallas_call(
        body, name="adamw", grid=(R // tr,), in_specs=[blk] * (3 + ng), out_specs=[blk] * n_out,
        out_shape=[jax.ShapeDtypeStruct((R, C), F32)] * n_out, compiler_params=_params("parallel"),
    )(w, *grads, m, v)
    return (outs[3] if ng == 2 else grads[0],) + tuple(outs[:3])


def _sum_shards(recv):
    n, R, C = recv.shape
    tr = _div(R, 256, SUBLANES if recv.dtype == F32 else 2 * SUBLANES)

    def body(r_ref, o_ref):
        s = r_ref[0].astype(F32)
        for i in range(1, n):
            s = s + r_ref[i].astype(F32)
        o_ref[...] = s

    return pl.pallas_call(
        body, name="sum_shards", grid=(R // tr,), in_specs=[pl.BlockSpec((n, tr, C), lambda i: (0, i, 0))],
        out_specs=pl.BlockSpec((tr, C), lambda i: (i, 0)), out_shape=jax.ShapeDtypeStruct((R, C), F32),
        compiler_params=_params("parallel"),
    )(recv)


N_DEV = 8
N_CHIPS = 4


def _all_gather_small(x, carry=None):
    m_per, n = x.shape

    def body(x_ref, out_ref, send_sems, recv_sems, local_sem):
        cx, cy, cc = _coords()
        me, sibling = (cx, cy, cc), (cx, cy, 1 - cc)
        chips = [(1 - cx, cy), (cx, 1 - cy), (1 - cx, 1 - cy)]

        def rows(px, py, pc):
            return out_ref.at[pl.ds((4 * px + 2 * py + pc) * m_per, m_per), :]

        def copy(k, block, to, src=None):
            return pltpu.make_async_remote_copy(
                src_ref=rows(*block) if src is None else src, dst_ref=rows(*block), send_sem=send_sems.at[k],
                recv_sem=recv_sems.at[k], device_id=to, device_id_type=MESH)

        mine = pltpu.make_async_copy(x_ref, rows(*me), local_sem)
        mine.start()
        first = [copy(0, me, sibling, src=x_ref)]
        first += [copy(1 + j, me, (*chip, cc), src=x_ref) for j, chip in enumerate(chips)]
        for cp in first:
            cp.start()
        passed = [copy(4 + j, (*chip, cc), sibling) for j, chip in enumerate(chips)]
        for j, chip in enumerate(chips):
            copy(1 + j, (*chip, cc), me).wait_recv()
            passed[j].start()
        copy(0, sibling, me).wait_recv()
        for j, chip in enumerate(chips):
            copy(4 + j, (*chip, 1 - cc), me).wait_recv()
        for cp in first + passed:
            cp.wait_send()
        mine.wait()

    (out,), moved = _carried_call(
        body, name="all_gather_small", grid=(1,), out_shape=[jax.ShapeDtypeStruct((N_DEV * m_per, n), x.dtype)],
        in_specs=[pl.BlockSpec(memory_space=pltpu.VMEM)], out_specs=[pl.BlockSpec(memory_space=pltpu.VMEM)],
        scratch_shapes=[pltpu.SemaphoreType.DMA((7,)), pltpu.SemaphoreType.DMA((7,)), pltpu.SemaphoreType.DMA],
        operands=(x,), sem=("arbitrary",), carry=carry)
    return out, moved


def _layer_moves(kind, arrays_from, arrays_to, pieces, layer_major=()):
    used = sorted({w for w, _ in pieces})
    pos = {w: i for i, w in enumerate(used)}
    gather = kind == "gather"

    def half(ref, c):
        rows = ref.shape[0] // 2
        return ref.at[pl.ds(c * rows, rows), :]

    def slot(d, w, chip, l):
        return d.at[l, chip] if w in layer_major else d.at[chip, l]

    def plan(src_refs, dst_refs, me):
        cx, cy, cc = me
        mine = 2 * cx + cy
        remote, local = [], []
        for w, l in pieces:
            s, d = src_refs[pos[w]], dst_refs[pos[w]]
            for px, py in _other_chips(cx, cy):
                if gather:
                    remote.append((half(s.at[l], cc), half(slot(d, w, mine, l), cc), (px, py, cc)))
                else:
                    remote.append((s.at[2 * px + py, l], d.at[mine, l], (px, py, cc)))
            local.append((s.at[l], slot(d, w, mine, l)) if gather else (s.at[mine, l], d.at[mine, l]))
        return remote, local

    def onward(src_refs, dst_refs, me):
        cx, cy, cc = me
        moves = []
        for w, l in pieces:
            d = dst_refs[pos[w]]
            for px, py in _other_chips(cx, cy):
                landed = half(slot(d, w, 2 * px + py, l), cc)
                moves.append((landed, landed, (cx, cy, 1 - cc)))
        return moves

    n = 3 * len(pieces)
    carry = _Carry([arrays_from[w] for w in used], [arrays_to[w] for w in used], plan, n, len(pieces),
                   onward if gather else None, n if gather else 0)
    return carry, used


def _swap_with_sibling(sums):
    def plan(src_refs, dst_refs, me):
        cx, cy, cc = me
        return [(s, d, (cx, cy, 1 - cc)) for s, d in zip(src_refs, dst_refs)], []

    return _Carry(sums, [lax.empty(s.shape, s.dtype) for s in sums], plan, len(sums), 0)


def _pack(arrs, width):
    parts, layout, row = [], [], 0
    for a in arrs:
        flat = a.reshape(-1).astype(F32)
        rows = -(-flat.shape[0] // (width * SUBLANES)) * SUBLANES
        parts.append(jnp.pad(flat, (0, rows * width - flat.shape[0])).reshape(rows, width))
        layout.append((row, rows, a.shape))
        row += rows
    pad = -row % (8 * SUBLANES) if row > 8 * SUBLANES else 0
    if pad:
        parts.append(jnp.zeros((pad, width), F32))
    return jnp.concatenate(parts, axis=0), layout, row + pad


def _unpack(buf, layout, idx):
    row, rows, shape = layout[idx]
    size = math.prod(shape)
    return buf[row:row + rows].reshape(-1)[:size].reshape(shape)


def kernel(x, c, ln_g, ada_w, ada_b, ssm_lam_re, ssm_lam_im, ssm_log_dt, ssm_b_re, ssm_b_im, ssm_c_re, ssm_c_im, ssm_d, ssm_w_glu, kv_g, kv_ada_w, kv_ada_b, w_kv, attn_w_q, attn_w_o, mlp_w1, mlp_w2, final_g, loss_target, m_ln_g, m_ada_w, m_ada_b, m_ssm_lam_re, m_ssm_lam_im, m_ssm_log_dt, m_ssm_b_re, m_ssm_b_im, m_ssm_c_re, m_ssm_c_im, m_ssm_d, m_ssm_w_glu, m_kv_g, m_kv_ada_w, m_kv_ada_b, m_w_kv, m_attn_w_q, m_attn_w_o, m_mlp_w1, m_mlp_w2, m_final_g, v_ln_g, v_ada_w, v_ada_b, v_ssm_lam_re, v_ssm_lam_im, v_ssm_log_dt, v_ssm_b_re, v_ssm_b_im, v_ssm_c_re, v_ssm_c_im, v_ssm_d, v_ssm_w_glu, v_kv_g, v_kv_ada_w, v_kv_ada_b, v_w_kv, v_attn_w_q, v_attn_w_o, v_mlp_w1, v_mlp_w2, v_final_g):
    B, S, D = x.shape
    N = B * S
    depth = ln_g.shape[0]
    n_a = ssm_w_glu.shape[0]
    n_b = attn_w_q.shape[0]
    FF = mlp_w1.shape[2] * N_CHIPS
    cx, cy, cc = _coords()
    chip = 2 * cx + cy
    dev = 4 * cx + 2 * cy + cc
    n_ex = N_DEV * B
    ada_cols = ada_w.shape[-1]
    kv_cols = kv_ada_w.shape[-1]

    GLU, KV, Q, O, W1, W2 = range(6)
    shards = [ssm_w_glu.astype(BF16), w_kv.astype(BF16)[None], attn_w_q.astype(BF16), attn_w_o.astype(BF16),
              mlp_w1.astype(BF16), mlp_w2.astype(BF16)]
    row_sharded = (O, W2)
    wg = [lax.empty((s.shape[0], N_CHIPS) + s.shape[1:] if w in row_sharded else (N_CHIPS,) + s.shape, BF16)
          for w, s in enumerate(shards)]

    def whole_rows(w):
        L, _, R, C = wg[w].shape
        return wg[w].reshape(1, L, N_CHIPS * R, C)

    def landed(arrays, used, moved):
        for w, a in zip(used, moved):
            arrays[w] = a

    fetch_with = {}

    def carried_by(kind, l, *pieces):
        fetch_with.setdefault((kind, l), []).extend(pieces)

    assert n_a >= 1 and n_b >= 1, (n_a, n_b)
    carried_by("mixer", 0, *[(GLU, l) for l in range(n_a)], (W1, 0))
    carried_by("glu_proj", 0, (W2, 0))
    carried_by("mlp_up", 0, (Q, 0))
    carried_by("mlp_up", n_a - 1, (O, 0))
    carried_by("mixer", n_a - 1, (KV, 0))
    for l in range(1, depth):
        carried_by("mixer" if l - 1 >= n_a else "mlp_down", l - 1, (W1, l))
        carried_by("mixer", l, (W2, l))
    for j in range(1, n_b):
        carried_by("mixer", n_a + j - 1, (Q, j), (O, j))

    def fetch(kind, l):
        pieces = fetch_with.get((kind, l))
        if not pieces:
            return None, []
        return _layer_moves("gather", shards, wg, pieces, layer_major=row_sharded)

    def mm_carrying(kind, l, *args, **kw):
        carry, used = fetch(kind, l)
        if carry is None:
            return _mm(kind, *args, **kw)
        out, moved = _mm(kind, *args, carry=carry, **kw)
        landed(wg, used, moved)
        return out

    c_pack, c_layout, _ = _pack([c], D)
    c_all_buf, _ = _all_gather_small(c_pack)
    c_rows = c_pack.shape[0]
    c_all = jnp.concatenate([_unpack(c_all_buf[d * c_rows:(d + 1) * c_rows], c_layout, 0) for d in range(N_DEV)], axis=0)
    sc_all = jax.nn.silu(c_all).astype(BF16)
    n_mod = depth * 2
    ada_w8 = ada_w.reshape(n_mod, 1, D, ada_cols)
    ada_b_row = ada_b.reshape(1, n_mod * ada_cols)
    mod_local = _mm("ada_fwd", sc_all, ada_w8, mode="nn", M=n_ex, N=n_mod * ada_cols, K=D, b_lay="cs", b_ns=n_mod,
                    epi=_add, extras=[("n", ada_b_row)])
    kv_ada_b_local = lax.dynamic_slice(kv_ada_b.reshape(N_CHIPS, kv_cols), (chip, 0), (1, kv_cols))
    kvmod_local = _mm("ada_fwd", sc_all, _as4(kv_ada_w), mode="nn", M=n_ex, N=kv_cols, K=D, epi=_add,
                      extras=[("n", kv_ada_b_local)])
    mod_pack, mod_layout, mod_rows = _pack([mod_local, kvmod_local, ln_g, ssm_d], D)
    mod_buf, _ = _all_gather_small(mod_pack)

    def from_chip(j, idx):
        d = 2 * j
        return _unpack(mod_buf[d * mod_rows:(d + 1) * mod_rows], mod_layout, idx)

    my_rows = lambda a: lax.dynamic_slice_in_dim(a, dev * B, B, axis=0)
    mods = jnp.concatenate([my_rows(from_chip(j, 0)).reshape(B, n_mod, ada_cols) for j in range(N_CHIPS)], axis=2)
    kvmod = jnp.concatenate([my_rows(from_chip(j, 1)) for j in range(N_CHIPS)], axis=1)
    ln_g_full = jnp.concatenate([from_chip(j, 2) for j in range(N_CHIPS)], axis=2)
    ssm_d_full = jnp.concatenate([from_chip(j, 3) for j in range(N_CHIPS)], axis=1)

    def mod3(l, s):
        mrow = mods[:, l * 2 + s]
        return [mrow[:, i * D:(i + 1) * D].reshape(B, 1, D) for i in range(3)]

    kv_shift, kv_scale = kvmod[:, :D].reshape(B, 1, D), kvmod[:, D:].reshape(B, 1, D)

    s5_tabs = []
    for l in range(n_a):
        prm = (ssm_lam_re[l], ssm_lam_im[l], ssm_log_dt[l], ssm_b_re[l], ssm_b_im[l], ssm_c_re[l], ssm_c_im[l])
        (bd, cd, _, _), disc_vjp = jax.vjp(_s5_discretize, *prm)
        pw, seg_f, seg_b = _s5_scan_coefs(ssm_lam_re[l], ssm_lam_im[l], ssm_log_dt[l], S5_SEG)
        s5_tabs.append((bd.astype(BF16), cd.astype(BF16), pw, seg_f, seg_b, disc_vjp))

    h = x.reshape(N, D)
    saved = []
    k_all = v_all = None
    shift, scale, gate = mod3(0, 0)
    u = _normmod(h, ln_g_full[0, 0].reshape(1, D), scale, shift, B)
    for l in range(depth):
        sv = {}
        sv["h0"], sv["scale0"], sv["gate0"], sv["u0"] = h, scale, gate, u
        shift1, scale1, gate1 = mod3(l, 1)
        norm1 = [("n", ln_g_full[l, 1].reshape(1, D)), ("ex", scale1), ("ex", shift1)]
        carry, used = fetch("mixer", l)
        if l < n_a:
            bd, cd, pw, seg_f, _, _ = s5_tabs[l]
            z, carries, moved = _s5_fwd(u, bd, cd, pw, seg_f, ssm_d_full[l].reshape(1, D), B, carry)
            landed(wg, used, moved)
            zz = mm_carrying("glu_proj", l, z, wg[GLU], mode="nn", M=N, N=2 * D, K=D, b_lay="cs", b_l=l, b_ns=N_CHIPS)
            y, h, u = _glu_residual_norm(zz, h, gate, ln_g_full[l, 1].reshape(1, D), scale1, shift1, B)
            sv["z"], sv["carries"], sv["zz"] = z, carries, zz
        else:
            j = l - n_a
            q = _mm("q_proj", u, wg[Q], mode="nn", M=N, N=3 * D, K=D, b_lay="cs", b_l=j, b_ns=N_CHIPS)
            q3 = q.reshape(B, S, 3 * D)
            o, lse, moved = _attn_fwd(q3, k_all, v_all, B, carry)
            landed(wg, used, moved)
            o2 = o.reshape(N, D)
            y, h, u = _mm("o_proj", o2, whole_rows(O), mode="nn", M=N, N=D, K=D, b_l=j, out_dtype=(BF16, F32, BF16),
                          epi=_gated_residual_norm, extras=[("mn", h), ("ex", gate)] + norm1, rows_per_ex=S)
            sv["q"], sv["o"], sv["lse"] = q3, o, lse
        sv["y0"] = y
        sv["h1"], sv["scale1"], sv["gate1"], sv["u1"] = h, scale1, gate1, u
        r = mm_carrying("mlp_up", l, u, wg[W1], mode="nn", M=N, N=FF, K=D, b_lay="cs", b_l=l, b_ns=N_CHIPS,
                        out_dtype=BF16, epi=_relu2)
        if l + 1 < depth:
            shift, scale, gate = mod3(l + 1, 0)
            y, h, u = mm_carrying(
                "mlp_down", l, r, whole_rows(W2), mode="nn", M=N, N=D, K=FF, b_l=l, tk=2048, tm=512,
                out_dtype=(BF16, F32, BF16), epi=_gated_residual_norm, rows_per_ex=S,
                extras=[("mn", h), ("ex", gate1), ("n", ln_g_full[l + 1, 0].reshape(1, D)), ("ex", scale), ("ex", shift)])
        else:
            y, h = mm_carrying("mlp_down", l, r, whole_rows(W2), mode="nn", M=N, N=D, K=FF, b_l=l, tk=2048,
                               out_dtype=(BF16, F32), epi=_gated_residual, extras=[("mn", h), ("ex", gate1)], rows_per_ex=S)
        sv["r"], sv["y1"] = r, y
        saved.append(sv)
        if l == n_a - 1:
            h_kv = h
            u_kv = _normmod(h, kv_g.reshape(1, D), kv_scale, kv_shift, B)
            half = N_CHIPS // 2
            k_all = _mm("kv_proj", u_kv, wg[KV], mode="nn", M=N, N=3 * D, K=D, b_lay="cs", b_s0=0, b_ns=half).reshape(B, S, 3 * D)
            v_all = _mm("kv_proj", u_kv, wg[KV], mode="nn", M=N, N=3 * D, K=D, b_lay="cs", b_s0=half, b_ns=half).reshape(B, S, 3 * D)

    loss_buf, dh, d_final_g = _loss_head(h, final_g.reshape(1, D), loss_target.reshape(N, D))
    loss = lax.psum(loss_buf[0, 0], ("x", "y", "c"))

    dg = [lax.empty((N_CHIPS,) + s.shape, BF16) for s in shards]
    recv = [lax.empty((N_CHIPS,) + s.shape, BF16) for s in shards]

    def send(pieces):
        return _layer_moves("scatter", dg, recv, pieces)

    send_with = {l: [(W1, l), (W2, l)] for l in range(depth)}
    for l in range(n_a):
        send_with[l] += [(GLU, l)]
    for j in range(n_b):
        send_with[n_a + j] += [(O, j)]
        send_with[n_a + j - 1] += [(Q, j)]
    send_with[n_a - 1] += [(KV, 0)]
    d_ln_g = [[None, None] for _ in range(depth)]
    d_mods = [[None, None] for _ in range(depth)]
    d_s5 = [None] * n_a
    dk_acc = dv_acc = None
    half = N_CHIPS // 2

    def tn_grad(name, a, d, into, l, Mr, Nc, lay, s0=0, ns=N_CHIPS):
        return _mm(name, a, _as4(d), mode="tn", M=Mr, N=Nc, K=N, b_lay="cs", out_dtype=BF16, out_lay=lay,
                   out4_shape=into.shape, out_into=into, out_l=l, out_s0=s0, out_ns=ns, tk=2048)

    dy, d_gate1 = _residual_bwd(dh, saved[-1]["gate1"], saved[-1]["y1"], B)
    for l in reversed(range(depth)):
        sv = saved[l]
        dg[W2] = tn_grad("mlp_down_dw", sv["r"], dy, dg[W2], l, FF, D, "rs")
        da = _mm("mlp_down_dx", dy, whole_rows(W2), mode="nt", M=N, N=FF, K=D, b_l=l, out_dtype=BF16,
                 epi=_relu2_bwd, extras=[("mn", sv["r"])])
        dg[W1] = tn_grad("mlp_up_dw", sv["u1"], da, dg[W1], l, D, FF, "cs")
        du = _mm("mlp_up_dx", da, wg[W1], mode="nt", M=N, N=D, K=FF, b_lay="cs", b_l=l, b_ns=N_CHIPS)
        dh, dgv, d_scale1, d_shift1, dy, d_gate0 = _normmod_bwd(du, sv["h1"], ln_g_full[l, 1].reshape(1, D), sv["scale1"],
                                                                dh, B, below=(sv["y0"], sv["gate0"]))
        d_ln_g[l][1] = dgv
        d_mods[l][1] = jnp.concatenate([d_shift1, d_scale1, d_gate1], axis=2)
        if l < n_a:
            bd, cd, pw, seg_f, seg_b, disc_vjp = s5_tabs[l]
            dzz = _glu_bwd(dy, sv["zz"])
            dg[GLU] = tn_grad("glu_proj_dw", sv["z"], dzz, dg[GLU], l, D, 2 * D, "cs")
            dz = _mm("glu_proj_dx", dzz, wg[GLU], mode="nt", M=N, N=D, K=2 * D, b_lay="cs", b_l=l, b_ns=N_CHIPS)
            carry, used = send(send_with[l])
            du, d_bd, d_cd, d_a2, d_dskip, moved = _s5_bwd(sv["u0"], dz, bd, cd, pw, seg_f, seg_b,
                                                           ssm_d_full[l].reshape(1, D), sv["carries"], B, carry)
            landed(recv, used, moved)
            d_are = (d_a2[:, 0, :CHUNK_STATE] + d_a2[:, 0, CHUNK_STATE:]).reshape(-1, SSM_STATE)
            d_aim = (d_a2[:, 1, CHUNK_STATE:] - d_a2[:, 1, :CHUNK_STATE]).reshape(-1, SSM_STATE)
            d_s5[l] = disc_vjp((d_bd, d_cd, d_are, d_aim)) + (d_dskip,)
        else:
            j = l - n_a
            dg[O] = tn_grad("o_proj_dw", sv["o"].reshape(N, D), dy, dg[O], j, D, D, "rs")
            do = _mm("o_proj_dx", dy, whole_rows(O), mode="nt", M=N, N=D, K=D, b_l=j)
            carry, used = send(send_with[l])
            dq, dk_acc, dv_acc, moved = _attn_bwd(sv["q"], k_all, v_all, sv["o"], sv["lse"], do.reshape(B, S, D),
                                                  dk_acc, dv_acc, B, l == n_a, carry)
            landed(recv, used, moved)
            dq2 = dq.reshape(N, 3 * D)
            dg[Q] = tn_grad("q_proj_dw", sv["u0"], dq2, dg[Q], j, D, 3 * D, "cs")
            du = _mm("q_proj_dx", dq2, wg[Q], mode="nt", M=N, N=D, K=3 * D, b_lay="cs", b_l=j, b_ns=N_CHIPS)
        below = (saved[l - 1]["y1"], saved[l - 1]["gate1"]) if l > 0 else None
        dh, dgv, d_scale0, d_shift0, dy, d_gate1 = _normmod_bwd(du, sv["h0"], ln_g_full[l, 0].reshape(1, D), sv["scale0"],
                                                                dh, B, below=None if l == n_a else below)
        d_ln_g[l][0] = dgv
        d_mods[l][0] = jnp.concatenate([d_shift0, d_scale0, d_gate0], axis=2)
        if l == n_a:
            dk2, dv2 = dk_acc.reshape(N, 3 * D), dv_acc.reshape(N, 3 * D)
            dg[KV] = tn_grad("kv_proj_dw", u_kv, dk2, dg[KV], 0, D, 3 * D, "cs", s0=0, ns=half)
            dg[KV] = tn_grad("kv_proj_dw", u_kv, dv2, dg[KV], 0, D, 3 * D, "cs", s0=half, ns=half)
            du_kv = _mm("kv_proj_dx", dk2, wg[KV], mode="nt", M=N, N=D, K=3 * D, b_lay="cs", b_s0=0, b_ns=half)
            du_kv = _mm("kv_proj_dx", dv2, wg[KV], mode="nt", M=N, N=D, K=3 * D, b_lay="cs", b_s0=half, b_ns=half,
                        epi=_add, extras=[("mn", du_kv)])
            dh, d_kv_g, d_kv_scale, d_kv_shift, dy, d_gate1 = _normmod_bwd(du_kv, h_kv, kv_g.reshape(1, D), kv_scale, dh, B,
                                                                           below=below)
    grad_x = dh.reshape(B, S, D)

    own = [_sum_shards(r.reshape(N_CHIPS, -1, r.shape[-1])) for r in recv]

    d_kvmod = jnp.concatenate([d_kv_shift, d_kv_scale], axis=2).reshape(B, 2 * D)
    d_mod_all = jnp.concatenate([d_mods[l][s].reshape(B, 3 * D) for l in range(depth) for s in range(2)], axis=1)
    small = [
        d_mod_all, d_kvmod,
        jnp.stack([jnp.stack([d_ln_g[l][0].reshape(D), d_ln_g[l][1].reshape(D)]) for l in range(depth)]),
        jnp.stack([d_s5[l][0] for l in range(n_a)]), jnp.stack([d_s5[l][1] for l in range(n_a)]),
        jnp.stack([d_s5[l][2] for l in range(n_a)]),
        jnp.stack([d_s5[l][3] for l in range(n_a)]), jnp.stack([d_s5[l][4] for l in range(n_a)]),
        jnp.stack([d_s5[l][5] for l in range(n_a)]), jnp.stack([d_s5[l][6] for l in range(n_a)]),
        jnp.stack([d_s5[l][7].reshape(D) for l in range(n_a)]),
        d_kv_g.reshape(D), d_final_g.reshape(D),
    ]
    small_pack, small_layout, small_rows = _pack(small, D)
    small_buf, other = _all_gather_small(small_pack, _swap_with_sibling(own))
    small_sum = _sum_shards(small_buf.reshape(N_DEV, small_rows, D))
    red = lambda idx: _unpack(small_sum, small_layout, idx)
    per_dev = lambda idx: jnp.concatenate(
        [_unpack(small_buf[d * small_rows:(d + 1) * small_rows], small_layout, idx) for d in range(N_DEV)], axis=0)

    dm_all = per_dev(0).reshape(n_ex, n_mod, 3 * D)
    dm_cols = lax.dynamic_slice_in_dim(dm_all, chip * ada_cols, ada_cols, axis=2).reshape(n_ex, n_mod * ada_cols)
    g_ada_w = _mm("ada_dw", sc_all, _as4(dm_cols), mode="tn", M=D, N=n_mod * ada_cols, K=n_ex, b_lay="cs",
                  out_lay="cs", out4_shape=(n_mod, 1, D, ada_cols), out_ns=n_mod).reshape(ada_w.shape)
    dkvm_all = per_dev(1)
    dkvm_cols = lax.dynamic_slice_in_dim(dkvm_all, chip * kv_cols, kv_cols, axis=1)
    g_kv_ada_w = _mm("ada_dw", sc_all, _as4(dkvm_cols), mode="tn", M=D, N=kv_cols, K=n_ex, b_lay="cs")
    g_ada_b_full = (red(0)[0] + red(0)[1]).reshape(depth, 2, 3 * D) if B == 2 else jnp.sum(red(0), axis=0).reshape(depth, 2, 3 * D)
    g_ada_b = lax.dynamic_slice_in_dim(g_ada_b_full, chip * ada_cols, ada_cols, axis=2)
    g_kv_ada_b = red(1)[0] + red(1)[1] if B == 2 else jnp.sum(red(1), axis=0)
    g_ln_g = lax.dynamic_slice_in_dim(red(2), chip * (D // N_CHIPS), D // N_CHIPS, axis=2)
    g_ssm_d = lax.dynamic_slice_in_dim(red(10), chip * (D // N_CHIPS), D // N_CHIPS, axis=1)
    small_grads = {
        "ln_g": g_ln_g, "ada_b": g_ada_b, "ssm_lam_re": red(3), "ssm_lam_im": red(4), "ssm_log_dt": red(5),
        "ssm_b_re": red(6), "ssm_b_im": red(7), "ssm_c_re": red(8), "ssm_c_im": red(9), "ssm_d": g_ssm_d,
        "kv_g": red(11), "kv_ada_b": g_kv_ada_b, "final_g": red(12),
    }
    small_w = {"ln_g": (ln_g, m_ln_g, v_ln_g), "ada_b": (ada_b, m_ada_b, v_ada_b),
               "ssm_lam_re": (ssm_lam_re, m_ssm_lam_re, v_ssm_lam_re), "ssm_lam_im": (ssm_lam_im, m_ssm_lam_im, v_ssm_lam_im),
               "ssm_log_dt": (ssm_log_dt, m_ssm_log_dt, v_ssm_log_dt), "ssm_b_re": (ssm_b_re, m_ssm_b_re, v_ssm_b_re),
               "ssm_b_im": (ssm_b_im, m_ssm_b_im, v_ssm_b_im), "ssm_c_re": (ssm_c_re, m_ssm_c_re, v_ssm_c_re),
               "ssm_c_im": (ssm_c_im, m_ssm_c_im, v_ssm_c_im), "ssm_d": (ssm_d, m_ssm_d, v_ssm_d),
               "kv_g": (kv_g, m_kv_g, v_kv_g), "kv_ada_b": (kv_ada_b, m_kv_ada_b, v_kv_ada_b),
               "final_g": (final_g, m_final_g, v_final_g)}
    names = list(small_w)
    wp, lay_w, _ = _pack([small_w[n][0] for n in names], D)
    gp, _, _ = _pack([small_grads[n] for n in names], D)
    mp, _, _ = _pack([small_w[n][1] for n in names], D)
    vp, _, _ = _pack([small_w[n][2] for n in names], D)
    _, d_p, m_p, v_p = _adamw(wp, [gp], mp, vp)
    upd = {n: (small_grads[n].reshape(small_w[n][0].shape), _unpack(d_p, lay_w, i), _unpack(m_p, lay_w, i), _unpack(v_p, lay_w, i))
           for i, n in enumerate(names)}

    def big(w, m, v, g_own, g_other=None):
        C = w.shape[-1]
        gs = [g_own.reshape(-1, C)] + ([g_other.reshape(-1, C)] if g_other is not None else [])
        return tuple(t.reshape(w.shape) for t in _adamw(w.reshape(-1, C), gs, m.reshape(-1, C), v.reshape(-1, C)))

    upd["ssm_w_glu"] = big(ssm_w_glu, m_ssm_w_glu, v_ssm_w_glu, own[0], other[0])
    upd["w_kv"] = big(w_kv, m_w_kv, v_w_kv, own[1], other[1])
    upd["attn_w_q"] = big(attn_w_q, m_attn_w_q, v_attn_w_q, own[2], other[2])
    upd["attn_w_o"] = big(attn_w_o, m_attn_w_o, v_attn_w_o, own[3], other[3])
    upd["mlp_w1"] = big(mlp_w1, m_mlp_w1, v_mlp_w1, own[4], other[4])
    upd["mlp_w2"] = big(mlp_w2, m_mlp_w2, v_mlp_w2, own[5], other[5])
    upd["ada_w"] = big(ada_w, m_ada_w, v_ada_w, g_ada_w)
    upd["kv_ada_w"] = big(kv_ada_w, m_kv_ada_w, v_kv_ada_w, g_kv_ada_w)

    order = ["ln_g", "ada_w", "ada_b", "ssm_lam_re", "ssm_lam_im", "ssm_log_dt", "ssm_b_re", "ssm_b_im", "ssm_c_re",
             "ssm_c_im", "ssm_d", "ssm_w_glu", "kv_g", "kv_ada_w", "kv_ada_b", "w_kv", "attn_w_q", "attn_w_o", "mlp_w1",
             "mlp_w2", "final_g"]
    return (loss, grad_x, *[upd[n][0] for n in order], *[upd[n][1] for n in order], *[upd[n][2] for n in order],
            *[upd[n][3] for n in order])
```

```python
import functools
import math

import jax
import jax.numpy as jnp
from jax import lax
from jax.experimental import pallas as pl
from jax.experimental.pallas import tpu as pltpu

F32 = jnp.float32
BF16 = jnp.bfloat16
MESH = pl.DeviceIdType.MESH

EPS = 1e-6
NEG = -1e30
SSM_GROUP = 16
SSM_STATE = 64
HEAD_DIM = 64
ATTN_BLOCK = 128
DILATIONS = (1, 4, 16)
ADAM_LR, ADAM_B1, ADAM_B2, ADAM_EPS, ADAM_WD, ADAM_STEP = 0.001, 0.9, 0.999, 1e-08, 0.01, 10

LANES = 128
SUBLANES = 8
CHUNK_GROUPS = LANES // SSM_GROUP
CHUNK_STATE = CHUNK_GROUPS * SSM_STATE
VMEM_LIMIT = 56 * 1024 * 1024


def _div(dim, pref, mult):
    t = min(pref, dim) // mult * mult
    while t >= mult:
        if dim % t == 0:
            return t
        t -= mult
    return dim


def _params(*sem):
    return pltpu.CompilerParams(dimension_semantics=sem, vmem_limit_bytes=VMEM_LIMIT)


def _coords():
    return lax.axis_index("x"), lax.axis_index("y"), lax.axis_index("c")


def _other_chips(cx, cy):
    return [(1 - cx, cy), (cx, 1 - cy), (1 - cx, 1 - cy)]


class _Carry:
    def __init__(self, srcs, dsts, plan, n_remote, n_local, onward=None, n_onward=0):
        self.srcs, self.dsts, self.plan, self.n_remote, self.n_local = list(srcs), list(dsts), plan, n_remote, n_local
        self.onward, self.n_onward = onward, n_onward


def _carried_call(body, *, name, grid, in_specs, out_specs, out_shape, scratch_shapes, operands, sem, carry=None):
    if carry is None:
        outs = pl.pallas_call(body, name=name, grid=grid, in_specs=in_specs, out_specs=out_specs, out_shape=out_shape,
                              scratch_shapes=scratch_shapes, compiler_params=_params(*sem))(*operands)
        return list(outs), []
    n_in, n_out, n_scr = len(in_specs), len(out_specs), len(scratch_shapes)
    ns, nd = len(carry.srcs), len(carry.dsts)

    def wrapped(*refs):
        base_in, src_refs = refs[:n_in], refs[n_in:n_in + ns]
        o0 = n_in + ns + nd
        base_out, dst_refs = refs[o0:o0 + n_out], refs[o0 + n_out:o0 + n_out + nd]
        s0 = o0 + n_out + nd
        base_scr = refs[s0:s0 + n_scr]
        send_sems, recv_sems, local_sems = refs[s0 + n_scr:]
        pids = [pl.program_id(a) for a in range(len(grid))]
        first = functools.reduce(jnp.logical_and, [p == 0 for p in pids])
        last = functools.reduce(jnp.logical_and, [p == g - 1 for p, g in zip(pids, grid)])

        def remote_copies(moves, k0):
            return [pltpu.make_async_remote_copy(src_ref=s, dst_ref=d, send_sem=send_sems.at[k0 + i], recv_sem=recv_sems.at[k0 + i],
                                                 device_id=peer, device_id_type=MESH) for i, (s, d, peer) in enumerate(moves)]

        def copies():
            remote, local = carry.plan(src_refs, dst_refs, _coords())
            return remote_copies(remote, 0), [pltpu.make_async_copy(s, d, local_sems.at[i]) for i, (s, d) in enumerate(local)]

        @pl.when(first)
        def _():
            remote, local = copies()
            for cp in local + remote:
                cp.start()

        body(*base_in, *base_out, *base_scr)

        @pl.when(last)
        def _():
            remote, local = copies()
            for cp in remote:
                cp.wait_send()
                cp.wait_recv()
            for cp in local:
                cp.wait()
            if carry.onward is not None:
                second = remote_copies(carry.onward(src_refs, dst_refs, _coords()), carry.n_remote)
                for cp in second:
                    cp.start()
                for cp in second:
                    cp.wait_send()
                    cp.wait_recv()

    anyspec = pl.BlockSpec(memory_space=pl.ANY)
    outs = pl.pallas_call(
        wrapped, name=name, grid=grid, in_specs=list(in_specs) + [anyspec] * (ns + nd),
        out_specs=list(out_specs) + [anyspec] * nd,
        out_shape=list(out_shape) + [jax.ShapeDtypeStruct(d.shape, d.dtype) for d in carry.dsts],
        scratch_shapes=list(scratch_shapes) + [pltpu.SemaphoreType.DMA((carry.n_remote + carry.n_onward,)),
                                               pltpu.SemaphoreType.DMA((carry.n_remote + carry.n_onward,)),
                                               pltpu.SemaphoreType.DMA((max(carry.n_local, 1),))],
        input_output_aliases={n_in + ns + i: n_out + i for i in range(nd)},
        compiler_params=_params(*(["arbitrary"] * len(grid))),
    )(*operands, *carry.srcs, *carry.dsts)
    return list(outs[:n_out]), list(outs[n_out:])


def _mm(name, a, b4, *, mode, M, N, K, b_lay="cs", b_l=0, b_s0=0, b_ns=1, out_dtype=F32, out_lay=None, out4_shape=None,
        out_into=None, out_l=0, out_s0=0, out_ns=1, epi=None, extras=(), rows_per_ex=None, tm=1024, tn=1024, tk=1024,
        carry=None):
    _, _, bR, bC = b4.shape
    tm = _div(M, tm, SUBLANES if M % 16 else 16)
    brows, bcols = (N, K) if mode == "nt" else (K, N)
    if b_lay == "cs":
        assert bR == brows and bC * b_ns == bcols, (name, b4.shape, brows, bcols)
    else:
        assert bC == bcols and bR * b_ns == brows, (name, b4.shape, brows, bcols)
    n_lim = N
    k_lim = K
    if mode == "nt":
        if b_lay == "cs":
            k_lim = bC
        else:
            n_lim = bR
    else:
        if b_lay == "cs":
            n_lim = bC
        else:
            k_lim = bR
    if out_lay == "cs":
        oR, oC = out4_shape[2], out4_shape[3]
        assert oR == M and oC * out_ns == N, (name, out4_shape, M, N)
        n_lim = math.gcd(n_lim, oC)
    elif out_lay == "rs":
        oR, oC = out4_shape[2], out4_shape[3]
        assert oC == N and oR * out_ns == M, (name, out4_shape, M, N)
        tm = _div(oR, tm, SUBLANES)
    tn = _div(n_lim, tn, LANES)
    tk = _div(k_lim, tk, LANES if mode != "tn" else SUBLANES)
    if mode == "tn":
        tk = _div(k_lim, tk, 16) if k_lim % 16 == 0 else tk
    nk = K // tk
    grid = (M // tm, N // tn, nk)

    if mode == "tn":
        a_spec = pl.BlockSpec((tk, tm), lambda i, j, k: (k, i))
    else:
        a_spec = pl.BlockSpec((tm, tk), lambda i, j, k: (i, k))

    def b_index(ri, ci, br, bc):
        if b_lay == "cs":
            per = bC // bc
            return (b_s0 + ci // per, b_l, ri, ci % per)
        per = bR // br
        return (b_s0 + ri // per, b_l, ri % per, ci)

    if mode == "nt":
        b_spec = pl.BlockSpec((None, None, tn, tk), lambda i, j, k: b_index(j, k, tn, tk))
    else:
        b_spec = pl.BlockSpec((None, None, tk, tn), lambda i, j, k: b_index(k, j, tk, tn))

    in_specs = [a_spec, b_spec]
    operands = [a, b4]
    for kind, arr in extras:
        if kind == "mn":
            in_specs.append(pl.BlockSpec((tm, tn), lambda i, j, k: (i, j)))
        elif kind == "ex":
            per_ex = rows_per_ex // tm
            in_specs.append(pl.BlockSpec((None, 1, tn), lambda i, j, k: (i // per_ex, 0, j)))
        else:
            in_specs.append(pl.BlockSpec((1, tn), lambda i, j, k: (0, j)))
        operands.append(arr)
    n_extra = len(extras)

    multi = isinstance(out_dtype, tuple)
    n_out = len(out_dtype) if multi else 1
    if out_lay is None:
        out_shape = [jax.ShapeDtypeStruct((M, N), dt) for dt in (out_dtype if multi else (out_dtype,))]
        out_spec = [pl.BlockSpec((tm, tn), lambda i, j, k: (i, j)) for _ in range(n_out)]
    else:
        out_shape = [jax.ShapeDtypeStruct(tuple(out4_shape), out_dtype)]
        if out_lay == "cs":
            per_o = oC // tn
            out_spec = [pl.BlockSpec((None, None, tm, tn), lambda i, j, k: (out_s0 + j // per_o, out_l, i, j % per_o))]
        else:
            per_o = oR // tm
            out_spec = [pl.BlockSpec((None, None, tm, tn), lambda i, j, k: (out_s0 + i // per_o, out_l, i % per_o, j))]
    aliases = {}
    if out_into is not None:
        in_specs.append(pl.BlockSpec(memory_space=pl.ANY))
        operands.append(out_into)
        aliases = {len(operands) - 1: 0}

    dims = {"nn": (((1,), (0,)), ((), ())), "nt": (((1,), (1,)), ((), ())), "tn": (((0,), (0,)), ((), ()))}[mode]

    def body(a_ref, b_ref, *rest):
        extra_refs = rest[:n_extra]
        o_refs = rest[len(rest) - n_out - (nk > 1):len(rest) - (nk > 1)]

        def finish(r):
            if epi is not None:
                r = epi(r, *[e[...] for e in extra_refs])
            for o_ref, val in zip(o_refs, r if multi else (r,)):
                o_ref[...] = val.astype(o_ref.dtype)

        part = lax.dot_general(a_ref[...].astype(BF16), b_ref[...].astype(BF16), dims, preferred_element_type=F32)
        if nk == 1:
            finish(part)
            return
        acc = rest[-1]
        k = pl.program_id(2)

        @pl.when(k == 0)
        def _():
            acc[...] = part

        @pl.when(k != 0)
        def _():
            acc[...] += part

        @pl.when(k == nk - 1)
        def _():
            finish(acc[...])

    scratch = [pltpu.VMEM((tm, tn), F32)] if nk > 1 else []
    if carry is not None:
        assert out_into is None, name
        outs, moved = _carried_call(body, name=name, grid=grid, in_specs=in_specs, out_specs=out_spec, out_shape=out_shape,
                                    scratch_shapes=scratch, operands=operands, sem=("arbitrary",) * 3, carry=carry)
        return (tuple(outs) if multi else outs[0]), moved
    outs = pl.pallas_call(
        body, name=name, grid=grid, in_specs=in_specs, out_specs=out_spec, out_shape=out_shape,
        scratch_shapes=scratch, input_output_aliases=aliases,
        compiler_params=_params("parallel", "parallel", "arbitrary"),
    )(*operands)
    return tuple(outs) if multi else outs[0]


def _as4(w):
    return w.reshape((1, 1) + w.shape)


def _relu2(acc):
    r = jnp.maximum(acc, 0.0)
    return r * r


def _relu2_bwd(acc, r):
    return acc * (2.0 * jnp.sqrt(r.astype(F32)))


def _add(acc, e):
    return acc + e


def _gated_residual(acc, h, gate):
    return acc, h + gate * acc


def _modulated_norm(x, g, scale, shift):
    rstd = lax.rsqrt(jnp.mean(x * x, axis=-1, keepdims=True) + EPS)
    return ((x * rstd) * g) * (1.0 + scale) + shift


def _gated_residual_norm(acc, h, gate, g, scale, shift):
    h_new = h + gate * acc
    return acc, h_new, _modulated_norm(h_new, g, scale, shift)


def _row_tiles(N, B, pref=256):
    S = N // B
    tm = _div(S, pref, SUBLANES)
    return tm, S // tm


def _normmod(h, g, scale, shift, B):
    N, D = h.shape
    tm, per_ex = _row_tiles(N, B)

    def body(h_ref, g_ref, sc_ref, sh_ref, u_ref):
        u_ref[...] = _modulated_norm(h_ref[...], g_ref[...], sc_ref[...], sh_ref[...]).astype(u_ref.dtype)

    tok = pl.BlockSpec((tm, D), lambda i: (i, 0))
    vec = pl.BlockSpec((1, D), lambda i: (0, 0))
    ex = pl.BlockSpec((None, 1, D), lambda i: (i // per_ex, 0, 0))
    return pl.pallas_call(
        body, name="normmod_fwd", grid=(N // tm,), in_specs=[tok, vec, ex, ex], out_specs=tok,
        out_shape=jax.ShapeDtypeStruct((N, D), BF16), compiler_params=_params("parallel"),
    )(h, g, scale, shift)


def _normmod_bwd(du, h, g, scale, dh_in, B, below=None):
    N, D = h.shape
    tm, per_ex = _row_tiles(N, B)
    fused = below is not None

    def body(*refs):
        du_ref, h_ref, g_ref, sc_ref, dhin_ref = refs[:5]
        dh_ref, dg_ref, dsc_ref, dsh_ref = refs[5 + 2 * fused:9 + 2 * fused]
        i = pl.program_id(0)
        x = h_ref[...]
        gv = g_ref[...]
        d_u = du_ref[...].astype(F32)
        rstd = lax.rsqrt(jnp.mean(x * x, axis=-1, keepdims=True) + EPS)
        xn = x * rstd
        dyg = d_u * (1.0 + sc_ref[...])
        dxn = dyg * gv
        dh = dhin_ref[...] + rstd * (dxn - xn * jnp.mean(dxn * xn, axis=-1, keepdims=True))
        dh_ref[...] = dh
        sums = [(dsc_ref, jnp.sum(d_u * (xn * gv), axis=0, keepdims=True)), (dsh_ref, jnp.sum(d_u, axis=0, keepdims=True))]
        if fused:
            y_ref, gt_ref = refs[5:7]
            dy_ref, dgt_ref = refs[9 + 2 * fused:]
            dy_ref[...] = (gt_ref[...] * dh).astype(dy_ref.dtype)
            sums.append((dgt_ref, jnp.sum(dh * y_ref[...], axis=0, keepdims=True)))
        dg_t = jnp.sum(dyg * xn, axis=0, keepdims=True)

        @pl.when(i % per_ex == 0)
        def _():
            for ref, val in sums:
                ref[...] = val

        @pl.when(i % per_ex != 0)
        def _():
            for ref, val in sums:
                ref[...] += val

        @pl.when(i == 0)
        def _():
            dg_ref[...] = dg_t

        @pl.when(i != 0)
        def _():
            dg_ref[...] += dg_t

    tok = pl.BlockSpec((tm, D), lambda i: (i, 0))
    vec = pl.BlockSpec((1, D), lambda i: (0, 0))
    ex = pl.BlockSpec((None, 1, D), lambda i: (i // per_ex, 0, 0))
    per_ex_shape = jax.ShapeDtypeStruct((B, 1, D), F32)
    outs = pl.pallas_call(
        body, name="normmod_bwd", grid=(N // tm,), in_specs=[tok, tok, vec, ex, tok] + ([tok, ex] if fused else []),
        out_specs=[tok, vec, ex, ex] + ([tok, ex] if fused else []),
        out_shape=[jax.ShapeDtypeStruct((N, D), F32), jax.ShapeDtypeStruct((1, D), F32), per_ex_shape, per_ex_shape]
        + ([jax.ShapeDtypeStruct((N, D), BF16), per_ex_shape] if fused else []),
        compiler_params=_params("arbitrary"),
    )(du, h, g, scale, dh_in, *(below if fused else ()))
    return tuple(outs) if fused else tuple(outs) + (None, None)


def _residual_bwd(dh, gate, y, B):
    N, D = dh.shape
    tm, per_ex = _row_tiles(N, B)

    def body(dh_ref, gt_ref, y_ref, dy_ref, dgt_ref):
        i = pl.program_id(0)
        d = dh_ref[...]
        dy_ref[...] = (gt_ref[...] * d).astype(dy_ref.dtype)
        t = jnp.sum(d * y_ref[...], axis=0, keepdims=True)

        @pl.when(i % per_ex == 0)
        def _():
            dgt_ref[...] = t

        @pl.when(i % per_ex != 0)
        def _():
            dgt_ref[...] += t

    tok = pl.BlockSpec((tm, D), lambda i: (i, 0))
    ex = pl.BlockSpec((None, 1, D), lambda i: (i // per_ex, 0, 0))
    return pl.pallas_call(
        body, name="residual_bwd", grid=(N // tm,), in_specs=[tok, ex, tok], out_specs=[tok, ex],
        out_shape=[jax.ShapeDtypeStruct((N, D), BF16), jax.ShapeDtypeStruct((B, 1, D), F32)],
        compiler_params=_params("arbitrary"),
    )(dh, gate, y)


def _glu_residual_norm(zz, h, gate, g, scale, shift, B):
    N, D2 = zz.shape
    D = D2 // 2
    tm, per_ex = _row_tiles(N, B)

    def body(v_ref, g_ref, h_ref, gt_ref, ng_ref, sc_ref, sh_ref, y_ref, o_ref, u_ref):
        y = v_ref[...] * jax.nn.sigmoid(g_ref[...])
        y_ref[...] = y.astype(y_ref.dtype)
        h_new = h_ref[...] + gt_ref[...] * y
        o_ref[...] = h_new
        u_ref[...] = _modulated_norm(h_new, ng_ref[...], sc_ref[...], sh_ref[...]).astype(u_ref.dtype)

    tok = pl.BlockSpec((tm, D), lambda i: (i, 0))
    vec = pl.BlockSpec((1, D), lambda i: (0, 0))
    ex = pl.BlockSpec((None, 1, D), lambda i: (i // per_ex, 0, 0))
    return pl.pallas_call(
        body, name="glu_fwd", grid=(N // tm,),
        in_specs=[tok, pl.BlockSpec((tm, D), lambda i: (i, 1)), tok, ex, vec, ex, ex], out_specs=[tok, tok, tok],
        out_shape=[jax.ShapeDtypeStruct((N, D), BF16), jax.ShapeDtypeStruct((N, D), F32), jax.ShapeDtypeStruct((N, D), BF16)],
        compiler_params=_params("parallel"),
    )(zz, zz, h, gate, g, scale, shift)


def _glu_bwd(dy, zz):
    N, D2 = zz.shape
    D = D2 // 2
    tm = _div(N, 256, SUBLANES)

    def body(dy_ref, v_ref, g_ref, o_ref):
        d = dy_ref[...].astype(F32)
        s = jax.nn.sigmoid(g_ref[...])
        o_ref[...] = jnp.concatenate([d * s, d * v_ref[...] * s * (1.0 - s)], axis=1).astype(o_ref.dtype)

    return pl.pallas_call(
        body, name="glu_bwd", grid=(N // tm,),
        in_specs=[pl.BlockSpec((tm, D), lambda i: (i, 0)), pl.BlockSpec((tm, D), lambda i: (i, 0)),
                  pl.BlockSpec((tm, D), lambda i: (i, 1))],
        out_specs=pl.BlockSpec((tm, D2), lambda i: (i, 0)), out_shape=jax.ShapeDtypeStruct((N, D2), BF16),
        compiler_params=_params("parallel"),
    )(dy, zz, zz)


def _loss_head(h, g, target):
    N, D = h.shape
    tm = _div(N, 256, SUBLANES)

    def body(h_ref, g_ref, t_ref, loss_ref, dh_ref, dg_ref):
        i = pl.program_id(0)
        x = h_ref[...]
        gv = g_ref[...]
        rstd = lax.rsqrt(jnp.mean(x * x, axis=-1, keepdims=True) + EPS)
        xn = x * rstd
        err = xn * gv - t_ref[...]
        part = 0.5 * jnp.sum(jnp.sum(err * err, axis=-1, keepdims=True) / D, axis=0, keepdims=True)
        dy = err / D
        dxn = dy * gv
        dh_ref[...] = rstd * (dxn - xn * jnp.mean(dxn * xn, axis=-1, keepdims=True))
        dg_t = jnp.sum(dy * xn, axis=0, keepdims=True)
        part = jnp.broadcast_to(part, loss_ref.shape)

        @pl.when(i == 0)
        def _():
            loss_ref[...] = part
            dg_ref[...] = dg_t

        @pl.when(i != 0)
        def _():
            loss_ref[...] += part
            dg_ref[...] += dg_t

    tok = pl.BlockSpec((tm, D), lambda i: (i, 0))
    vec = pl.BlockSpec((1, D), lambda i: (0, 0))
    return pl.pallas_call(
        body, name="loss_head", grid=(N // tm,), in_specs=[tok, vec, tok],
        out_specs=[pl.BlockSpec((SUBLANES, LANES), lambda i: (0, 0)), tok, vec],
        out_shape=[jax.ShapeDtypeStruct((SUBLANES, LANES), F32), jax.ShapeDtypeStruct((N, D), F32),
                   jax.ShapeDtypeStruct((1, D), F32)],
        compiler_params=_params("arbitrary"),
    )(h, g, target)


def _swap_halves(x):
    half = x.shape[-1] // 2
    return jnp.concatenate([x[:, half:], x[:, :half]], axis=1)


def _gelu(y):
    return jax.nn.gelu(y)


def _gelu_grad(y):
    c0 = math.sqrt(2.0 / math.pi)
    inner = c0 * (y + 0.044715 * y * y * y)
    t = jnp.tanh(inner)
    return 0.5 * (1.0 + t) + 0.5 * y * (1.0 - t * t) * c0 * (1.0 + 3.0 * 0.044715 * y * y)


def _s5_discretize(lam_re, lam_im, log_dt, b_re, b_im, c_re, c_im):
    G = lam_re.shape[0]
    nch = G // CHUNK_GROUPS
    dt = jnp.exp(log_dt)[:, None]
    er = jnp.exp(lam_re * dt)
    a_re = er * jnp.cos(lam_im * dt)
    a_im = er * jnp.sin(lam_im * dt)
    den = lam_re * lam_re + lam_im * lam_im
    n_re, n_im = a_re - 1.0, a_im
    f_re = (n_re * lam_re + n_im * lam_im) / den
    f_im = (n_im * lam_re - n_re * lam_im) / den
    bb_re = f_re[..., None] * b_re - f_im[..., None] * b_im
    bb_im = f_re[..., None] * b_im + f_im[..., None] * b_re
    eye = jnp.eye(CHUNK_GROUPS, dtype=F32)

    def pack_b(bb):
        bb = bb.reshape(nch, CHUNK_GROUPS, SSM_STATE, SSM_GROUP)
        return jnp.einsum("jgpc,gh->jgchp", bb, eye).reshape(nch, LANES, CHUNK_STATE)

    def pack_c(cc):
        cc = cc.reshape(nch, CHUNK_GROUPS, SSM_GROUP, SSM_STATE)
        return jnp.einsum("jgcp,gh->jgphc", cc, eye).reshape(nch, CHUNK_STATE, LANES)

    bd = jnp.concatenate([pack_b(bb_re), pack_b(bb_im)], axis=2)
    cd = jnp.concatenate([pack_c(c_re), pack_c(-c_im)], axis=1)
    return bd, cd, a_re, a_im


S5_TILE = 1024
S5_SEG = S5_TILE // SUBLANES
S5_UNROLL = 4


def _s5_scan_coefs(lam_re, lam_im, log_dt, seg):
    G = lam_re.shape[0]
    nch = G // CHUNK_GROUPS
    dt = jnp.exp(log_dt)[:, None]
    rate = (lam_re * dt).reshape(nch, 1, CHUNK_STATE)
    freq = (lam_im * dt).reshape(nch, 1, CHUNK_STATE)

    def powers(ks):
        k = jnp.asarray(ks, F32)[None, :, None]
        er = jnp.exp(k * rate)
        re, im = er * jnp.cos(k * freq), er * jnp.sin(k * freq)
        return jnp.concatenate([re, re], axis=2), jnp.concatenate([-im, im], axis=2)

    pw = jnp.stack(powers(range(1, seg + 1)), axis=1)
    steps = (1, 2, 4)
    re, im = powers([s * seg for s in steps])
    row = jnp.arange(SUBLANES, dtype=jnp.int32)[None, None, :, None]
    shift = jnp.asarray(steps, jnp.int32)[None, :, None, None]

    def table(reverse):
        mask = (row < SUBLANES - shift) if reverse else (row >= shift)
        pair = jnp.stack([jnp.where(mask, re[:, :, None, :], 0.0),
                          jnp.where(mask, (-im if reverse else im)[:, :, None, :], 0.0)], axis=2)
        return pair.reshape(nch, 2 * len(steps), SUBLANES, 2 * CHUNK_STATE)

    return pw, table(False), table(True)


def _to_segments(dst_s, src_ref, seg):
    for j in range(SUBLANES):
        dst_s[pl.ds(j, seg, stride=SUBLANES), :] = src_ref[pl.ds(j * seg, seg), :].astype(F32)


def _from_segments(dst_ref, src_s, seg):
    for j in range(SUBLANES):
        dst_ref[pl.ds(j * seg, seg), :] = src_s[pl.ds(j, seg, stride=SUBLANES), :].astype(dst_ref.dtype)


def _seg_scan(x_ref, pw_ref, seg_ref, carry_ref, c_ref, seg, reverse):
    W = x_ref.shape[-1]
    tm = x_ref.shape[0]
    sgn = -1.0 if reverse else 1.0
    ar = jnp.broadcast_to(pw_ref[0, 0:1, :], (SUBLANES, W))
    ai = sgn * jnp.broadcast_to(pw_ref[1, 0:1, :], (SUBLANES, W))

    def rows(i):
        return pl.ds(pl.multiple_of(i * SUBLANES, SUBLANES), SUBLANES)

    def step(t, prev):
        i = (seg - 2 - t) if reverse else (t + 1)
        x = x_ref[rows(i), :] + ar * prev + ai * _swap_halves(prev)
        x_ref[rows(i), :] = x
        return x

    start = (seg - 1) * SUBLANES if reverse else 0
    edge = lax.fori_loop(0, seg - 1, step, x_ref[start:start + SUBLANES, :], unroll=S5_UNROLL)
    row = lax.broadcasted_iota(jnp.int32, (SUBLANES, W), 0)
    if reverse:
        f = jnp.where(row == SUBLANES - 1, carry_ref[...], pltpu.roll(edge, SUBLANES - 1, 0))
    else:
        f = jnp.where(row == 0, carry_ref[...], pltpu.roll(edge, 1, 0))
    for si, s in enumerate((1, 2, 4)):
        fs = pltpu.roll(f, (SUBLANES - s) if reverse else s, 0)
        f = f + seg_ref[2 * si] * fs + seg_ref[2 * si + 1] * _swap_halves(fs)
    c_ref[...] = f
    fsw = _swap_halves(f)

    def fix(i, _):
        k = (seg - 1 - i) if reverse else i
        x_ref[rows(i), :] = x_ref[rows(i), :] + pw_ref[0, pl.ds(k, 1), :] * f + (sgn * pw_ref[1, pl.ds(k, 1), :]) * fsw
        return 0

    lax.fori_loop(0, seg, fix, 0, unroll=S5_UNROLL)
    leaving = x_ref[0:1, :] if reverse else x_ref[tm - 1:tm, :]
    carry_ref[...] = jnp.broadcast_to(leaving, carry_ref.shape)


def _s5_fwd(u, bd, cd, pw, seg_f, d_skip, B, carry=None):
    N, D = u.shape
    S = N // B
    nch = D // LANES
    W = 2 * CHUNK_STATE
    tm, seg = S5_TILE, S5_SEG
    nt = S // tm

    def body(u_ref, bd_ref, cd_ref, pw_ref, seg_ref, d_ref, z_ref, cin_ref, x_s, carry, c_s, u_s, z_s):
        t = pl.program_id(2)

        @pl.when(t == 0)
        def _():
            carry[...] = jnp.zeros_like(carry)

        cin_ref[...] = carry[...]
        _to_segments(u_s, u_ref, seg)
        uf = u_s[...]
        x_s[...] = jnp.dot(uf.astype(BF16), bd_ref[...], preferred_element_type=F32)
        _seg_scan(x_s, pw_ref, seg_ref, carry, c_s, seg, False)
        y = jnp.dot(x_s[...].astype(BF16), cd_ref[...], preferred_element_type=F32) + d_ref[...] * uf
        z_s[...] = _gelu(y)
        _from_segments(z_ref, z_s, seg)

    (z, carries), moved = _carried_call(
        body, name="s5_fwd", grid=(nch, B, nt),
        in_specs=[pl.BlockSpec((tm, LANES), lambda j, b, t: (b * nt + t, j)),
                  pl.BlockSpec((None, LANES, W), lambda j, b, t: (j, 0, 0)),
                  pl.BlockSpec((None, W, LANES), lambda j, b, t: (j, 0, 0)),
                  pl.BlockSpec((None, 2, seg, W), lambda j, b, t: (j, 0, 0, 0)),
                  pl.BlockSpec((None, 6, SUBLANES, W), lambda j, b, t: (j, 0, 0, 0)),
                  pl.BlockSpec((1, LANES), lambda j, b, t: (0, j))],
        out_specs=[pl.BlockSpec((tm, LANES), lambda j, b, t: (b * nt + t, j)),
                   pl.BlockSpec((None, None, SUBLANES, W), lambda j, b, t: (j, b * nt + t, 0, 0))],
        out_shape=[jax.ShapeDtypeStruct((N, D), BF16), jax.ShapeDtypeStruct((nch, B * nt, SUBLANES, W), F32)],
        scratch_shapes=[pltpu.VMEM((tm, W), F32), pltpu.VMEM((SUBLANES, W), F32), pltpu.VMEM((SUBLANES, W), F32),
                        pltpu.VMEM((tm, LANES), F32), pltpu.VMEM((tm, LANES), F32)],
        operands=(u, bd, cd, pw, seg_f, d_skip), sem=("parallel", "arbitrary", "arbitrary"), carry=carry)
    return z, carries, moved


def _s5_bwd(u, dz, bd, cd, pw, seg_f, seg_b, d_skip, carries, B, carry=None):
    N, D = u.shape
    S = N // B
    nch = D // LANES
    W = 2 * CHUNK_STATE
    tm, seg = S5_TILE, S5_SEG
    nt = S // tm
    tn_dims = (((0,), (0,)), ((), ()))
    nt_dims = (((1,), (1,)), ((), ()))

    def body(u_ref, dz_ref, bd_ref, cd_ref, pw_ref, sf_ref, sb_ref, d_ref, cin_ref,
             du_ref, dbd_ref, dcd_ref, da_ref, dd_ref, x_s, l_s, carry, lcarry, c_s, lc_s, u_s, t_s):
        b = pl.program_id(1)
        t = pl.program_id(2)

        @pl.when((b == 0) & (t == 0))
        def _():
            dbd_ref[...] = jnp.zeros_like(dbd_ref)
            dcd_ref[...] = jnp.zeros_like(dcd_ref)
            da_ref[...] = jnp.zeros_like(da_ref)
            dd_ref[...] = jnp.zeros_like(dd_ref)

        @pl.when(t == 0)
        def _():
            lcarry[...] = jnp.zeros_like(lcarry)

        _to_segments(u_s, u_ref, seg)
        _to_segments(t_s, dz_ref, seg)
        uf = u_s[...]
        uv = uf.astype(BF16)
        carry[...] = cin_ref[...]
        x_s[...] = jnp.dot(uv, bd_ref[...], preferred_element_type=F32)
        _seg_scan(x_s, pw_ref, sf_ref, carry, c_s, seg, False)
        xb = x_s[...].astype(BF16)
        y = jnp.dot(xb, cd_ref[...], preferred_element_type=F32) + d_ref[...] * uf
        dy = t_s[...] * _gelu_grad(y)
        dd_ref[...] += jnp.sum(dy * uf, axis=0, keepdims=True)
        dyb = dy.astype(BF16)
        dcd_ref[...] += lax.dot_general(xb, dyb, tn_dims, preferred_element_type=F32)
        l_s[...] = lax.dot_general(dyb, cd_ref[...], nt_dims, preferred_element_type=F32)
        _seg_scan(l_s, pw_ref, sb_ref, lcarry, lc_s, seg, True)
        lb = l_s[...].astype(BF16)
        dbd_ref[...] += lax.dot_general(uv, lb, tn_dims, preferred_element_type=F32)
        t_s[...] = lax.dot_general(lb, bd_ref[...], nt_dims, preferred_element_type=F32) + d_ref[...] * dy
        _from_segments(du_ref, t_s, seg)
        lam_rest, x_prev = l_s[SUBLANES:, :], x_s[:tm - SUBLANES, :]
        lam_0, c_in = l_s[:SUBLANES, :], c_s[...]
        da_ref[0:1, :] += (jnp.sum(lam_rest * x_prev, axis=0, keepdims=True) + jnp.sum(lam_0 * c_in, axis=0, keepdims=True))
        da_ref[1:2, :] += (jnp.sum(lam_rest * _swap_halves(x_prev), axis=0, keepdims=True)
                           + jnp.sum(lam_0 * _swap_halves(c_in), axis=0, keepdims=True))

    tile = lambda j, b, t: (b * nt + (nt - 1 - t), j)
    chunk3 = lambda j, b, t: (j, 0, 0)
    chunk4 = lambda j, b, t: (j, 0, 0, 0)
    outs, moved = _carried_call(
        body, name="s5_bwd", grid=(nch, B, nt),
        in_specs=[pl.BlockSpec((tm, LANES), tile), pl.BlockSpec((tm, LANES), tile),
                  pl.BlockSpec((None, LANES, W), chunk3), pl.BlockSpec((None, W, LANES), chunk3),
                  pl.BlockSpec((None, 2, seg, W), chunk4), pl.BlockSpec((None, 6, SUBLANES, W), chunk4),
                  pl.BlockSpec((None, 6, SUBLANES, W), chunk4), pl.BlockSpec((1, LANES), lambda j, b, t: (0, j)),
                  pl.BlockSpec((None, None, SUBLANES, W), lambda j, b, t: (j, b * nt + (nt - 1 - t), 0, 0))],
        out_specs=[pl.BlockSpec((tm, LANES), tile), pl.BlockSpec((None, LANES, W), chunk3),
                   pl.BlockSpec((None, W, LANES), chunk3), pl.BlockSpec((None, 2, W), chunk3),
                   pl.BlockSpec((1, LANES), lambda j, b, t: (0, j))],
        out_shape=[jax.ShapeDtypeStruct((N, D), F32), jax.ShapeDtypeStruct((nch, LANES, W), F32),
                   jax.ShapeDtypeStruct((nch, W, LANES), F32), jax.ShapeDtypeStruct((nch, 2, W), F32),
                   jax.ShapeDtypeStruct((1, D), F32)],
        scratch_shapes=[pltpu.VMEM((tm, W), F32), pltpu.VMEM((tm, W), F32)] + [pltpu.VMEM((SUBLANES, W), F32)] * 4
        + [pltpu.VMEM((tm, LANES), F32)] * 2,
        operands=(u, dz, bd, cd, pw, seg_f, seg_b, d_skip, carries), sem=("parallel", "arbitrary", "arbitrary"), carry=carry)
    return (*outs, moved)


ATTN_HEADS = LANES // HEAD_DIM
ATTN_FWD_UNROLL = 4
ATTN_BWD_UNROLL = 4


def _attn_mask(n):
    qi = lax.broadcasted_iota(jnp.int32, (ATTN_HEADS * ATTN_BLOCK, 2 * ATTN_BLOCK), 0) % ATTN_BLOCK
    kj = lax.broadcasted_iota(jnp.int32, (ATTN_HEADS * ATTN_BLOCK, 2 * ATTN_BLOCK), 1)
    prev_ok = (kj < ATTN_BLOCK) & (kj >= qi) & (n > 0)
    return prev_ok | ((kj >= ATTN_BLOCK) & (kj - ATTN_BLOCK <= qi))


def _stack_heads(x):
    return jnp.concatenate([_only_head(x, h) for h in range(ATTN_HEADS)], axis=0)


def _stack_head_columns(x):
    return jnp.concatenate([x[:, h * HEAD_DIM:h * HEAD_DIM + 1] for h in range(ATTN_HEADS)], axis=0)


def _unstack_heads(x):
    return _per_head([x[h * ATTN_BLOCK:(h + 1) * ATTN_BLOCK] for h in range(ATTN_HEADS)])


def _head_lanes(h):
    lane = lax.broadcasted_iota(jnp.int32, (ATTN_BLOCK, LANES), 1)
    return (lane >= h * HEAD_DIM) & (lane < (h + 1) * HEAD_DIM)


def _per_head(cols):
    out = jnp.broadcast_to(cols[-1], (ATTN_BLOCK, LANES))
    for h in range(len(cols) - 2, -1, -1):
        out = jnp.where(_head_lanes(h), jnp.broadcast_to(cols[h], (ATTN_BLOCK, LANES)), out)
    return out


def _only_head(x, h):
    return jnp.where(_head_lanes(h), x, 0.0).astype(BF16)


def _block_rows(tb, dil, nb):
    r = tb // nb
    n = tb % nb
    start = r + dil * ATTN_BLOCK * n
    startp = jnp.where(n > 0, start - dil * ATTN_BLOCK, start)
    return n, pl.ds(start, ATTN_BLOCK, stride=dil), pl.ds(startp, ATTN_BLOCK, stride=dil)


def _attn_fwd(q, k, v, B, carry=None):
    _, S, D3 = q.shape
    D = D3 // 3
    HP = D // LANES
    scale = HEAD_DIM ** -0.5
    n_blocks = S // ATTN_BLOCK
    nbr = len(DILATIONS)
    nt_dims = (((1,), (1,)), ((), ()))

    def branch(dil, q_ref, k_ref, v_ref, acc, m_s, l_s):
        nb = (S // dil) // ATTN_BLOCK

        def blk(tb, _):
            n, rows, rowsp = _block_rows(tb, dil, nb)
            qb = q_ref[rows, :] * scale
            kk = jnp.concatenate([k_ref[rowsp, :], k_ref[rows, :]], axis=0).astype(BF16)
            vv = jnp.concatenate([v_ref[rowsp, :], v_ref[rows, :]], axis=0).astype(BF16)
            s = lax.dot_general(_stack_heads(qb), kk, nt_dims, preferred_element_type=F32)
            s = jnp.where(_attn_mask(n), s, NEG)
            m = jnp.max(s, axis=-1, keepdims=True)
            p = jnp.exp(s - m)
            m_s[rows, :] = _unstack_heads(m)
            l_s[rows, :] = _unstack_heads(jnp.sum(p, axis=-1, keepdims=True))
            acc[rows, :] = _unstack_heads(jnp.dot(p.astype(BF16), vv, preferred_element_type=F32))
            return 0

        lax.fori_loop(0, n_blocks, blk, 0, unroll=ATTN_FWD_UNROLL)

    def body(q_ref, k_ref, v_ref, o_ref, lse_ref, *scratch):
        accs, m_ss, l_ss = scratch[:nbr], scratch[nbr:2 * nbr], scratch[2 * nbr:]
        g = pl.program_id(2)
        for gi, dil in enumerate(DILATIONS):
            pl.when(g == gi)(functools.partial(branch, dil, q_ref, k_ref, v_ref, accs[gi], m_ss[gi], l_ss[gi]))

        @pl.when(g == nbr - 1)
        def _():
            def fin(i, _):
                rows = pl.ds(pl.multiple_of(i * ATTN_BLOCK, ATTN_BLOCK), ATTN_BLOCK)
                ms = [m[rows, :] for m in m_ss]
                m_all = functools.reduce(jnp.maximum, ms)
                ws = [jnp.exp(m - m_all) for m in ms]
                den = sum(w * l[rows, :] for w, l in zip(ws, l_ss))
                o_ref[rows, :] = sum(w * a[rows, :] for w, a in zip(ws, accs)) / den
                lse_ref[rows, :] = m_all + jnp.log(den)
                return 0

            lax.fori_loop(0, n_blocks, fin, 0)

    br = pl.BlockSpec((None, S, LANES), lambda b, hp, g: (b, 0, g * HP + hp))
    hd = pl.BlockSpec((None, S, LANES), lambda b, hp, g: (b, 0, hp))
    (o, lse), moved = _carried_call(
        body, name="attn_fwd", grid=(B, HP, nbr), in_specs=[br, br, br], out_specs=[hd, hd],
        out_shape=[jax.ShapeDtypeStruct((B, S, D), F32), jax.ShapeDtypeStruct((B, S, D), F32)],
        scratch_shapes=[pltpu.VMEM((S, LANES), F32)] * (3 * nbr),
        operands=(q, k, v), sem=("parallel", "parallel", "arbitrary"), carry=carry)
    return o, lse, moved


def _attn_bwd(q, k, v, o, lse, do, dk_prev, dv_prev, B, last, carry=None):
    _, S, D3 = q.shape
    D = D3 // 3
    HP = D // LANES
    scale = HEAD_DIM ** -0.5
    n_blocks = S // ATTN_BLOCK
    has_prev = dk_prev is not None
    nt_dims = (((1,), (1,)), ((), ()))
    tn_dims = (((0,), (0,)), ((), ()))

    def branch(dil, q_ref, k_ref, v_ref, lse_ref, do_ref, dq_s, dk_c, dv_c, delta, dk_p, dv_p):
        nb = (S // dil) // ATTN_BLOCK

        def blk(tb, _):
            n, rows, rowsp = _block_rows(tb, dil, nb)
            qb = q_ref[rows, :] * scale
            dob, lb, db = do_ref[rows, :], lse_ref[rows, :], delta[rows, :]
            kk = jnp.concatenate([k_ref[rowsp, :], k_ref[rows, :]], axis=0).astype(BF16)
            vv = jnp.concatenate([v_ref[rowsp, :], v_ref[rows, :]], axis=0).astype(BF16)
            qs, dos = _stack_heads(qb), _stack_heads(dob)
            s = lax.dot_general(qs, kk, nt_dims, preferred_element_type=F32)
            p = jnp.where(_attn_mask(n), jnp.exp(s - _stack_head_columns(lb)), 0.0)
            dp = lax.dot_general(dos, vv, nt_dims, preferred_element_type=F32)
            ds = (p * (dp - _stack_head_columns(db))).astype(BF16)
            dkk = lax.dot_general(ds, qs, tn_dims, preferred_element_type=F32)
            dvv = lax.dot_general(p.astype(BF16), dos, tn_dims, preferred_element_type=F32)
            dq_s[rows, :] = _unstack_heads(jnp.dot(ds, kk, preferred_element_type=F32)) * scale
            dk_p[rowsp, :] = dkk[:ATTN_BLOCK]
            dv_p[rowsp, :] = dvv[:ATTN_BLOCK]
            dk_c[rows, :] = dkk[ATTN_BLOCK:]
            dv_c[rows, :] = dvv[ATTN_BLOCK:]
            return 0

        lax.fori_loop(0, n_blocks, blk, 0, unroll=ATTN_BWD_UNROLL)

    def body(*refs):
        q_ref, k_ref, v_ref, o_ref, lse_ref, do_ref = refs[:6]
        n_in = 8 if has_prev else 6
        dq_ref, dk_ref, dv_ref, delta, dk_p, dv_p, dq_s, dk_c, dv_c = refs[n_in:n_in + 9]
        g = pl.program_id(2)

        @pl.when(g == 0)
        def _():
            def dl(i, _):
                rows = pl.ds(pl.multiple_of(i * ATTN_BLOCK, ATTN_BLOCK), ATTN_BLOCK)
                prod = do_ref[rows, :] * o_ref[rows, :]
                delta[rows, :] = _per_head([jnp.sum(jnp.where(_head_lanes(h), prod, 0.0), axis=-1, keepdims=True)
                                            for h in range(ATTN_HEADS)])
                return 0

            lax.fori_loop(0, n_blocks, dl, 0)

        dk_p[...] = jnp.zeros_like(dk_p)
        dv_p[...] = jnp.zeros_like(dv_p)
        for gi, dil in enumerate(DILATIONS):
            pl.when(g == gi)(functools.partial(branch, dil, q_ref, k_ref, v_ref, lse_ref, do_ref, dq_s, dk_c, dv_c,
                                               delta, dk_p, dv_p))

        def fin(i, _):
            rows = pl.ds(pl.multiple_of(i * ATTN_BLOCK, ATTN_BLOCK), ATTN_BLOCK)
            dk_t = dk_c[rows, :] + dk_p[rows, :]
            dv_t = dv_c[rows, :] + dv_p[rows, :]
            if has_prev:
                dk_t = dk_t + refs[6][rows, :].astype(F32)
                dv_t = dv_t + refs[7][rows, :].astype(F32)
            dq_ref[rows, :] = dq_s[rows, :].astype(dq_ref.dtype)
            dk_ref[rows, :] = dk_t.astype(dk_ref.dtype)
            dv_ref[rows, :] = dv_t.astype(dv_ref.dtype)
            return 0

        lax.fori_loop(0, n_blocks, fin, 0)

    br = pl.BlockSpec((None, S, LANES), lambda b, hp, g: (b, 0, g * HP + hp))
    hd = pl.BlockSpec((None, S, LANES), lambda b, hp, g: (b, 0, hp))
    ins = [q, k, v, o, lse, do] + ([dk_prev, dv_prev] if has_prev else [])
    kv_dtype = BF16 if last else F32
    (dq, dk, dv), moved = _carried_call(
        body, name="attn_bwd", grid=(B, HP, len(DILATIONS)),
        in_specs=[br, br, br, hd, hd, hd] + ([br, br] if has_prev else []), out_specs=[br, br, br],
        out_shape=[jax.ShapeDtypeStruct(q.shape, BF16), jax.ShapeDtypeStruct(q.shape, kv_dtype),
                   jax.ShapeDtypeStruct(q.shape, kv_dtype)],
        scratch_shapes=[pltpu.VMEM((S, LANES), F32)] * 6,
        operands=ins, sem=("parallel", "parallel", "arbitrary"), carry=carry)
    return dq, dk, dv, moved


def _adamw(w, grads, m, v):
    R, C = w.shape
    tr = _div(R, 256, SUBLANES)
    ng = len(grads)
    c1 = 1.0 - ADAM_B1 ** ADAM_STEP
    c2 = 1.0 - ADAM_B2 ** ADAM_STEP

    def body(*refs):
        w_ref, m_ref, v_ref = refs[0], refs[1 + ng], refs[2 + ng]
        d_ref, mo_ref, vo_ref = refs[3 + ng:6 + ng]
        g = refs[1][...]
        if ng == 2:
            g = g + refs[2][...]
            refs[6 + ng][...] = g
        mn = ADAM_B1 * m_ref[...] + (1.0 - ADAM_B1) * g
        vn = ADAM_B2 * v_ref[...] + (1.0 - ADAM_B2) * (g * g)
        d_ref[...] = -ADAM_LR * ((mn / c1) / (jnp.sqrt(vn / c2) + ADAM_EPS) + ADAM_WD * w_ref[...])
        mo_ref[...] = mn
        vo_ref[...] = vn

    blk = pl.BlockSpec((tr, C), lambda i: (i, 0))
    n_out = 3 + (ng == 2)
    outs = pl.pallas_call(
        body, name="adamw", grid=(R // tr,), in_specs=[blk] * (3 + ng), out_specs=[blk] * n_out,
        out_shape=[jax.ShapeDtypeStruct((R, C), F32)] * n_out, compiler_params=_params("parallel"),
    )(w, *grads, m, v)
    return (outs[3] if ng == 2 else grads[0],) + tuple(outs[:3])


def _sum_shards(recv):
    n, R, C = recv.shape
    tr = _div(R, 256, SUBLANES if recv.dtype == F32 else 2 * SUBLANES)

    def body(r_ref, o_ref):
        s = r_ref[0].astype(F32)
        for i in range(1, n):
            s = s + r_ref[i].astype(F32)
        o_ref[...] = s

    return pl.pallas_call(
        body, name="sum_shards", grid=(R // tr,), in_specs=[pl.BlockSpec((n, tr, C), lambda i: (0, i, 0))],
        out_specs=pl.BlockSpec((tr, C), lambda i: (i, 0)), out_shape=jax.ShapeDtypeStruct((R, C), F32),
        compiler_params=_params("parallel"),
    )(recv)


N_DEV = 8
N_CHIPS = 4


def _all_gather_small(x, carry=None):
    m_per, n = x.shape

    def body(x_ref, out_ref, send_sems, recv_sems, local_sem):
        cx, cy, cc = _coords()
        me, sibling = (cx, cy, cc), (cx, cy, 1 - cc)
        chips = [(1 - cx, cy), (cx, 1 - cy), (1 - cx, 1 - cy)]

        def rows(px, py, pc):
            return out_ref.at[pl.ds((4 * px + 2 * py + pc) * m_per, m_per), :]

        def copy(k, block, to, src=None):
            return pltpu.make_async_remote_copy(
                src_ref=rows(*block) if src is None else src, dst_ref=rows(*block), send_sem=send_sems.at[k],
                recv_sem=recv_sems.at[k], device_id=to, device_id_type=MESH)

        mine = pltpu.make_async_copy(x_ref, rows(*me), local_sem)
        mine.start()
        first = [copy(0, me, sibling, src=x_ref)]
        first += [copy(1 + j, me, (*chip, cc), src=x_ref) for j, chip in enumerate(chips)]
        for cp in first:
            cp.start()
        passed = [copy(4 + j, (*chip, cc), sibling) for j, chip in enumerate(chips)]
        for j, chip in enumerate(chips):
            copy(1 + j, (*chip, cc), me).wait_recv()
            passed[j].start()
        copy(0, sibling, me).wait_recv()
        for j, chip in enumerate(chips):
            copy(4 + j, (*chip, 1 - cc), me).wait_recv()
        for cp in first + passed:
            cp.wait_send()
        mine.wait()

    (out,), moved = _carried_call(
        body, name="all_gather_small", grid=(1,), out_shape=[jax.ShapeDtypeStruct((N_DEV * m_per, n), x.dtype)],
        in_specs=[pl.BlockSpec(memory_space=pltpu.VMEM)], out_specs=[pl.BlockSpec(memory_space=pltpu.VMEM)],
        scratch_shapes=[pltpu.SemaphoreType.DMA((7,)), pltpu.SemaphoreType.DMA((7,)), pltpu.SemaphoreType.DMA],
        operands=(x,), sem=("arbitrary",), carry=carry)
    return out, moved


def _layer_moves(kind, arrays_from, arrays_to, pieces, layer_major=()):
    used = sorted({w for w, _ in pieces})
    pos = {w: i for i, w in enumerate(used)}
    gather = kind == "gather"

    def half(ref, c):
        rows = ref.shape[0] // 2
        return ref.at[pl.ds(c * rows, rows), :]

    def slot(d, w, chip, l):
        return d.at[l, chip] if w in layer_major else d.at[chip, l]

    def plan(src_refs, dst_refs, me):
        cx, cy, cc = me
        mine = 2 * cx + cy
        remote, local = [], []
        for w, l in pieces:
            s, d = src_refs[pos[w]], dst_refs[pos[w]]
            for px, py in _other_chips(cx, cy):
                if gather:
                    remote.append((half(s.at[l], cc), half(slot(d, w, mine, l), cc), (px, py, cc)))
                else:
                    remote.append((s.at[2 * px + py, l], d.at[mine, l], (px, py, cc)))
            local.append((s.at[l], slot(d, w, mine, l)) if gather else (s.at[mine, l], d.at[mine, l]))
        return remote, local

    def onward(src_refs, dst_refs, me):
        cx, cy, cc = me
        moves = []
        for w, l in pieces:
            d = dst_refs[pos[w]]
            for px, py in _other_chips(cx, cy):
                landed = half(slot(d, w, 2 * px + py, l), cc)
                moves.append((landed, landed, (cx, cy, 1 - cc)))
        return moves

    n = 3 * len(pieces)
    carry = _Carry([arrays_from[w] for w in used], [arrays_to[w] for w in used], plan, n, len(pieces),
                   onward if gather else None, n if gather else 0)
    return carry, used


def _swap_with_sibling(sums):
    def plan(src_refs, dst_refs, me):
        cx, cy, cc = me
        return [(s, d, (cx, cy, 1 - cc)) for s, d in zip(src_refs, dst_refs)], []

    return _Carry(sums, [lax.empty(s.shape, s.dtype) for s in sums], plan, len(sums), 0)


def _pack(arrs, width):
    parts, layout, row = [], [], 0
    for a in arrs:
        flat = a.reshape(-1).astype(F32)
        rows = -(-flat.shape[0] // (width * SUBLANES)) * SUBLANES
        parts.append(jnp.pad(flat, (0, rows * width - flat.shape[0])).reshape(rows, width))
        layout.append((row, rows, a.shape))
        row += rows
    pad = -row % (8 * SUBLANES) if row > 8 * SUBLANES else 0
    if pad:
        parts.append(jnp.zeros((pad, width), F32))
    return jnp.concatenate(parts, axis=0), layout, row + pad


def _unpack(buf, layout, idx):
    row, rows, shape = layout[idx]
    size = math.prod(shape)
    return buf[row:row + rows].reshape(-1)[:size].reshape(shape)


def kernel(x, c, ln_g, ada_w, ada_b, ssm_lam_re, ssm_lam_im, ssm_log_dt, ssm_b_re, ssm_b_im, ssm_c_re, ssm_c_im, ssm_d, ssm_w_glu, kv_g, kv_ada_w, kv_ada_b, w_kv, attn_w_q, attn_w_o, mlp_w1, mlp_w2, final_g, loss_target, m_ln_g, m_ada_w, m_ada_b, m_ssm_lam_re, m_ssm_lam_im, m_ssm_log_dt, m_ssm_b_re, m_ssm_b_im, m_ssm_c_re, m_ssm_c_im, m_ssm_d, m_ssm_w_glu, m_kv_g, m_kv_ada_w, m_kv_ada_b, m_w_kv, m_attn_w_q, m_attn_w_o, m_mlp_w1, m_mlp_w2, m_final_g, v_ln_g, v_ada_w, v_ada_b, v_ssm_lam_re, v_ssm_lam_im, v_ssm_log_dt, v_ssm_b_re, v_ssm_b_im, v_ssm_c_re, v_ssm_c_im, v_ssm_d, v_ssm_w_glu, v_kv_g, v_kv_ada_w, v_kv_ada_b, v_w_kv, v_attn_w_q, v_attn_w_o, v_mlp_w1, v_mlp_w2, v_final_g):
    B, S, D = x.shape
    N = B * S
    depth = ln_g.shape[0]
    n_a = ssm_w_glu.shape[0]
    n_b = attn_w_q.shape[0]
    FF = mlp_w1.shape[2] * N_CHIPS
    cx, cy, cc = _coords()
    chip = 2 * cx + cy
    dev = 4 * cx + 2 * cy + cc
    n_ex = N_DEV * B
    ada_cols = ada_w.shape[-1]
    kv_cols = kv_ada_w.shape[-1]

    GLU, KV, Q, O, W1, W2 = range(6)
    shards = [ssm_w_glu.astype(BF16), w_kv.astype(BF16)[None], attn_w_q.astype(BF16), attn_w_o.astype(BF16),
              mlp_w1.astype(BF16), mlp_w2.astype(BF16)]
    row_sharded = (O, W2)
    wg = [lax.empty((s.shape[0], N_CHIPS) + s.shape[1:] if w in row_sharded else (N_CHIPS,) + s.shape, BF16)
          for w, s in enumerate(shards)]

    def whole_rows(w):
        L, _, R, C = wg[w].shape
        return wg[w].reshape(1, L, N_CHIPS * R, C)

    def landed(arrays, used, moved):
        for w, a in zip(used, moved):
            arrays[w] = a

    fetch_with = {}

    def carried_by(kind, l, *pieces):
        fetch_with.setdefault((kind, l), []).extend(pieces)

    assert n_a >= 1 and n_b >= 1, (n_a, n_b)
    carried_by("mixer", 0, *[(GLU, l) for l in range(n_a)], (W1, 0))
    carried_by("glu_proj", 0, (W2, 0))
    carried_by("mlp_up", 0, (Q, 0))
    carried_by("mlp_up", n_a - 1, (O, 0))
    carried_by("mixer", n_a - 1, (KV, 0))
    for l in range(1, depth):
        carried_by("mixer" if l - 1 >= n_a else "mlp_down", l - 1, (W1, l))
        carried_by("mixer", l, (W2, l))
    for j in range(1, n_b):
        carried_by("mixer", n_a + j - 1, (Q, j), (O, j))

    def fetch(kind, l):
        pieces = fetch_with.get((kind, l))
        if not pieces:
            return None, []
        return _layer_moves("gather", shards, wg, pieces, layer_major=row_sharded)

    def mm_carrying(kind, l, *args, **kw):
        carry, used = fetch(kind, l)
        if carry is None:
            return _mm(kind, *args, **kw)
        out, moved = _mm(kind, *args, carry=carry, **kw)
        landed(wg, used, moved)
        return out

    c_pack, c_layout, _ = _pack([c], D)
    c_all_buf, _ = _all_gather_small(c_pack)
    c_rows = c_pack.shape[0]
    c_all = jnp.concatenate([_unpack(c_all_buf[d * c_rows:(d + 1) * c_rows], c_layout, 0) for d in range(N_DEV)], axis=0)
    sc_all = jax.nn.silu(c_all).astype(BF16)
    n_mod = depth * 2
    ada_w8 = ada_w.reshape(n_mod, 1, D, ada_cols)
    ada_b_row = ada_b.reshape(1, n_mod * ada_cols)
    mod_local = _mm("ada_fwd", sc_all, ada_w8, mode="nn", M=n_ex, N=n_mod * ada_cols, K=D, b_lay="cs", b_ns=n_mod,
                    epi=_add, extras=[("n", ada_b_row)])
    kv_ada_b_local = lax.dynamic_slice(kv_ada_b.reshape(N_CHIPS, kv_cols), (chip, 0), (1, kv_cols))
    kvmod_local = _mm("ada_fwd", sc_all, _as4(kv_ada_w), mode="nn", M=n_ex, N=kv_cols, K=D, epi=_add,
                      extras=[("n", kv_ada_b_local)])
    mod_pack, mod_layout, mod_rows = _pack([mod_local, kvmod_local, ln_g, ssm_d], D)
    mod_buf, _ = _all_gather_small(mod_pack)

    def from_chip(j, idx):
        d = 2 * j
        return _unpack(mod_buf[d * mod_rows:(d + 1) * mod_rows], mod_layout, idx)

    my_rows = lambda a: lax.dynamic_slice_in_dim(a, dev * B, B, axis=0)
    mods = jnp.concatenate([my_rows(from_chip(j, 0)).reshape(B, n_mod, ada_cols) for j in range(N_CHIPS)], axis=2)
    kvmod = jnp.concatenate([my_rows(from_chip(j, 1)) for j in range(N_CHIPS)], axis=1)
    ln_g_full = jnp.concatenate([from_chip(j, 2) for j in range(N_CHIPS)], axis=2)
    ssm_d_full = jnp.concatenate([from_chip(j, 3) for j in range(N_CHIPS)], axis=1)

    def mod3(l, s):
        mrow = mods[:, l * 2 + s]
        return [mrow[:, i * D:(i + 1) * D].reshape(B, 1, D) for i in range(3)]

    kv_shift, kv_scale = kvmod[:, :D].reshape(B, 1, D), kvmod[:, D:].reshape(B, 1, D)

    s5_tabs = []
    for l in range(n_a):
        prm = (ssm_lam_re[l], ssm_lam_im[l], ssm_log_dt[l], ssm_b_re[l], ssm_b_im[l], ssm_c_re[l], ssm_c_im[l])
        (bd, cd, _, _), disc_vjp = jax.vjp(_s5_discretize, *prm)
        pw, seg_f, seg_b = _s5_scan_coefs(ssm_lam_re[l], ssm_lam_im[l], ssm_log_dt[l], S5_SEG)
        s5_tabs.append((bd.astype(BF16), cd.astype(BF16), pw, seg_f, seg_b, disc_vjp))

    h = x.reshape(N, D)
    saved = []
    k_all = v_all = None
    shift, scale, gate = mod3(0, 0)
    u = _normmod(h, ln_g_full[0, 0].reshape(1, D), scale, shift, B)
    for l in range(depth):
        sv = {}
        sv["h0"], sv["scale0"], sv["gate0"], sv["u0"] = h, scale, gate, u
        shift1, scale1, gate1 = mod3(l, 1)
        norm1 = [("n", ln_g_full[l, 1].reshape(1, D)), ("ex", scale1), ("ex", shift1)]
        carry, used = fetch("mixer", l)
        if l < n_a:
            bd, cd, pw, seg_f, _, _ = s5_tabs[l]
            z, carries, moved = _s5_fwd(u, bd, cd, pw, seg_f, ssm_d_full[l].reshape(1, D), B, carry)
            landed(wg, used, moved)
            zz = mm_carrying("glu_proj", l, z, wg[GLU], mode="nn", M=N, N=2 * D, K=D, b_lay="cs", b_l=l, b_ns=N_CHIPS)
            y, h, u = _glu_residual_norm(zz, h, gate, ln_g_full[l, 1].reshape(1, D), scale1, shift1, B)
            sv["z"], sv["carries"], sv["zz"] = z, carries, zz
        else:
            j = l - n_a
            q = _mm("q_proj", u, wg[Q], mode="nn", M=N, N=3 * D, K=D, b_lay="cs", b_l=j, b_ns=N_CHIPS)
            q3 = q.reshape(B, S, 3 * D)
            o, lse, moved = _attn_fwd(q3, k_all, v_all, B, carry)
            landed(wg, used, moved)
            o2 = o.reshape(N, D)
            y, h, u = _mm("o_proj", o2, whole_rows(O), mode="nn", M=N, N=D, K=D, b_l=j, out_dtype=(BF16, F32, BF16),
                          epi=_gated_residual_norm, extras=[("mn", h), ("ex", gate)] + norm1, rows_per_ex=S)
            sv["q"], sv["o"], sv["lse"] = q3, o, lse
        sv["y0"] = y
        sv["h1"], sv["scale1"], sv["gate1"], sv["u1"] = h, scale1, gate1, u
        r = mm_carrying("mlp_up", l, u, wg[W1], mode="nn", M=N, N=FF, K=D, b_lay="cs", b_l=l, b_ns=N_CHIPS,
                        out_dtype=BF16, epi=_relu2)
        if l + 1 < depth:
            shift, scale, gate = mod3(l + 1, 0)
            y, h, u = mm_carrying(
                "mlp_down", l, r, whole_rows(W2), mode="nn", M=N, N=D, K=FF, b_l=l, tk=2048, tm=512,
                out_dtype=(BF16, F32, BF16), epi=_gated_residual_norm, rows_per_ex=S,
                extras=[("mn", h), ("ex", gate1), ("n", ln_g_full[l + 1, 0].reshape(1, D)), ("ex", scale), ("ex", shift)])
        else:
            y, h = mm_carrying("mlp_down", l, r, whole_rows(W2), mode="nn", M=N, N=D, K=FF, b_l=l, tk=2048,
                               out_dtype=(BF16, F32), epi=_gated_residual, extras=[("mn", h), ("ex", gate1)], rows_per_ex=S)
        sv["r"], sv["y1"] = r, y
        saved.append(sv)
        if l == n_a - 1:
            h_kv = h
            u_kv = _normmod(h, kv_g.reshape(1, D), kv_scale, kv_shift, B)
            half = N_CHIPS // 2
            k_all = _mm("kv_proj", u_kv, wg[KV], mode="nn", M=N, N=3 * D, K=D, b_lay="cs", b_s0=0, b_ns=half).reshape(B, S, 3 * D)
            v_all = _mm("kv_proj", u_kv, wg[KV], mode="nn", M=N, N=3 * D, K=D, b_lay="cs", b_s0=half, b_ns=half).reshape(B, S, 3 * D)

    loss_buf, dh, d_final_g = _loss_head(h, final_g.reshape(1, D), loss_target.reshape(N, D))
    loss = lax.psum(loss_buf[0, 0], ("x", "y", "c"))

    dg = [lax.empty((N_CHIPS,) + s.shape, BF16) for s in shards]
    recv = [lax.empty((N_CHIPS,) + s.shape, BF16) for s in shards]

    def send(pieces):
        return _layer_moves("scatter", dg, recv, pieces)

    send_with = {l: [(W1, l), (W2, l)] for l in range(depth)}
    for l in range(n_a):
        send_with[l] += [(GLU, l)]
    for j in range(n_b):
        send_with[n_a + j] += [(O, j)]
        send_with[n_a + j - 1] += [(Q, j)]
    send_with[n_a - 1] += [(KV, 0)]
    d_ln_g = [[None, None] for _ in range(depth)]
    d_mods = [[None, None] for _ in range(depth)]
    d_s5 = [None] * n_a
    dk_acc = dv_acc = None
    half = N_CHIPS // 2

    def tn_grad(name, a, d, into, l, Mr, Nc, lay, s0=0, ns=N_CHIPS):
        return _mm(name, a, _as4(d), mode="tn", M=Mr, N=Nc, K=N, b_lay="cs", out_dtype=BF16, out_lay=lay,
                   out4_shape=into.shape, out_into=into, out_l=l, out_s0=s0, out_ns=ns, tk=2048)

    dy, d_gate1 = _residual_bwd(dh, saved[-1]["gate1"], saved[-1]["y1"], B)
    for l in reversed(range(depth)):
        sv = saved[l]
        dg[W2] = tn_grad("mlp_down_dw", sv["r"], dy, dg[W2], l, FF, D, "rs")
        da = _mm("mlp_down_dx", dy, whole_rows(W2), mode="nt", M=N, N=FF, K=D, b_l=l, out_dtype=BF16,
                 epi=_relu2_bwd, extras=[("mn", sv["r"])])
        dg[W1] = tn_grad("mlp_up_dw", sv["u1"], da, dg[W1], l, D, FF, "cs")
        du = _mm("mlp_up_dx", da, wg[W1], mode="nt", M=N, N=D, K=FF, b_lay="cs", b_l=l, b_ns=N_CHIPS)
        dh, dgv, d_scale1, d_shift1, dy, d_gate0 = _normmod_bwd(du, sv["h1"], ln_g_full[l, 1].reshape(1, D), sv["scale1"],
                                                                dh, B, below=(sv["y0"], sv["gate0"]))
        d_ln_g[l][1] = dgv
        d_mods[l][1] = jnp.concatenate([d_shift1, d_scale1, d_gate1], axis=2)
        if l < n_a:
            bd, cd, pw, seg_f, seg_b, disc_vjp = s5_tabs[l]
            dzz = _glu_bwd(dy, sv["zz"])
            dg[GLU] = tn_grad("glu_proj_dw", sv["z"], dzz, dg[GLU], l, D, 2 * D, "cs")
            dz = _mm("glu_proj_dx", dzz, wg[GLU], mode="nt", M=N, N=D, K=2 * D, b_lay="cs", b_l=l, b_ns=N_CHIPS)
            carry, used = send(send_with[l])
            du, d_bd, d_cd, d_a2, d_dskip, moved = _s5_bwd(sv["u0"], dz, bd, cd, pw, seg_f, seg_b,
                                                           ssm_d_full[l].reshape(1, D), sv["carries"], B, carry)
            landed(recv, used, moved)
            d_are = (d_a2[:, 0, :CHUNK_STATE] + d_a2[:, 0, CHUNK_STATE:]).reshape(-1, SSM_STATE)
            d_aim = (d_a2[:, 1, CHUNK_STATE:] - d_a2[:, 1, :CHUNK_STATE]).reshape(-1, SSM_STATE)
            d_s5[l] = disc_vjp((d_bd, d_cd, d_are, d_aim)) + (d_dskip,)
        else:
            j = l - n_a
            dg[O] = tn_grad("o_proj_dw", sv["o"].reshape(N, D), dy, dg[O], j, D, D, "rs")
            do = _mm("o_proj_dx", dy, whole_rows(O), mode="nt", M=N, N=D, K=D, b_l=j)
            carry, used = send(send_with[l])
            dq, dk_acc, dv_acc, moved = _attn_bwd(sv["q"], k_all, v_all, sv["o"], sv["lse"], do.reshape(B, S, D),
                                                  dk_acc, dv_acc, B, l == n_a, carry)
            landed(recv, used, moved)
            dq2 = dq.reshape(N, 3 * D)
            dg[Q] = tn_grad("q_proj_dw", sv["u0"], dq2, dg[Q], j, D, 3 * D, "cs")
            du = _mm("q_proj_dx", dq2, wg[Q], mode="nt", M=N, N=D, K=3 * D, b_lay="cs", b_l=j, b_ns=N_CHIPS)
        below = (saved[l - 1]["y1"], saved[l - 1]["gate1"]) if l > 0 else None
        dh, dgv, d_scale0, d_shift0, dy, d_gate1 = _normmod_bwd(du, sv["h0"], ln_g_full[l, 0].reshape(1, D), sv["scale0"],
                                                                dh, B, below=None if l == n_a else below)
        d_ln_g[l][0] = dgv
        d_mods[l][0] = jnp.concatenate([d_shift0, d_scale0, d_gate0], axis=2)
        if l == n_a:
            dk2, dv2 = dk_acc.reshape(N, 3 * D), dv_acc.reshape(N, 3 * D)
            dg[KV] = tn_grad("kv_proj_dw", u_kv, dk2, dg[KV], 0, D, 3 * D, "cs", s0=0, ns=half)
            dg[KV] = tn_grad("kv_proj_dw", u_kv, dv2, dg[KV], 0, D, 3 * D, "cs", s0=half, ns=half)
            du_kv = _mm("kv_proj_dx", dk2, wg[KV], mode="nt", M=N, N=D, K=3 * D, b_lay="cs", b_s0=0, b_ns=half)
            du_kv = _mm("kv_proj_dx", dv2, wg[KV], mode="nt", M=N, N=D, K=3 * D, b_lay="cs", b_s0=half, b_ns=half,
                        epi=_add, extras=[("mn", du_kv)])
            dh, d_kv_g, d_kv_scale, d_kv_shift, dy, d_gate1 = _normmod_bwd(du_kv, h_kv, kv_g.reshape(1, D), kv_scale, dh, B,
                                                                           below=below)
    grad_x = dh.reshape(B, S, D)

    own = [_sum_shards(r.reshape(N_CHIPS, -1, r.shape[-1])) for r in recv]

    d_kvmod = jnp.concatenate([d_kv_shift, d_kv_scale], axis=2).reshape(B, 2 * D)
    d_mod_all = jnp.concatenate([d_mods[l][s].reshape(B, 3 * D) for l in range(depth) for s in range(2)], axis=1)
    small = [
        d_mod_all, d_kvmod,
        jnp.stack([jnp.stack([d_ln_g[l][0].reshape(D), d_ln_g[l][1].reshape(D)]) for l in range(depth)]),
        jnp.stack([d_s5[l][0] for l in range(n_a)]), jnp.stack([d_s5[l][1] for l in range(n_a)]),
        jnp.stack([d_s5[l][2] for l in range(n_a)]),
        jnp.stack([d_s5[l][3] for l in range(n_a)]), jnp.stack([d_s5[l][4] for l in range(n_a)]),
        jnp.stack([d_s5[l][5] for l in range(n_a)]), jnp.stack([d_s5[l][6] for l in range(n_a)]),
        jnp.stack([d_s5[l][7].reshape(D) for l in range(n_a)]),
        d_kv_g.reshape(D), d_final_g.reshape(D),
    ]
    small_pack, small_layout, small_rows = _pack(small, D)
    small_buf, other = _all_gather_small(small_pack, _swap_with_sibling(own))
    small_sum = _sum_shards(small_buf.reshape(N_DEV, small_rows, D))
    red = lambda idx: _unpack(small_sum, small_layout, idx)
    per_dev = lambda idx: jnp.concatenate(
        [_unpack(small_buf[d * small_rows:(d + 1) * small_rows], small_layout, idx) for d in range(N_DEV)], axis=0)

    dm_all = per_dev(0).reshape(n_ex, n_mod, 3 * D)
    dm_cols = lax.dynamic_slice_in_dim(dm_all, chip * ada_cols, ada_cols, axis=2).reshape(n_ex, n_mod * ada_cols)
    g_ada_w = _mm("ada_dw", sc_all, _as4(dm_cols), mode="tn", M=D, N=n_mod * ada_cols, K=n_ex, b_lay="cs",
                  out_lay="cs", out4_shape=(n_mod, 1, D, ada_cols), out_ns=n_mod).reshape(ada_w.shape)
    dkvm_all = per_dev(1)
    dkvm_cols = lax.dynamic_slice_in_dim(dkvm_all, chip * kv_cols, kv_cols, axis=1)
    g_kv_ada_w = _mm("ada_dw", sc_all, _as4(dkvm_cols), mode="tn", M=D, N=kv_cols, K=n_ex, b_lay="cs")
    g_ada_b_full = (red(0)[0] + red(0)[1]).reshape(depth, 2, 3 * D) if B == 2 else jnp.sum(red(0), axis=0).reshape(depth, 2, 3 * D)
    g_ada_b = lax.dynamic_slice_in_dim(g_ada_b_full, chip * ada_cols, ada_cols, axis=2)
    g_kv_ada_b = red(1)[0] + red(1)[1] if B == 2 else jnp.sum(red(1), axis=0)
    g_ln_g = lax.dynamic_slice_in_dim(red(2), chip * (D // N_CHIPS), D // N_CHIPS, axis=2)
    g_ssm_d = lax.dynamic_slice_in_dim(red(10), chip * (D // N_CHIPS), D // N_CHIPS, axis=1)
    small_grads = {
        "ln_g": g_ln_g, "ada_b": g_ada_b, "ssm_lam_re": red(3), "ssm_lam_im": red(4), "ssm_log_dt": red(5),
        "ssm_b_re": red(6), "ssm_b_im": red(7), "ssm_c_re": red(8), "ssm_c_im": red(9), "ssm_d": g_ssm_d,
        "kv_g": red(11), "kv_ada_b": g_kv_ada_b, "final_g": red(12),
    }
    small_w = {"ln_g": (ln_g, m_ln_g, v_ln_g), "ada_b": (ada_b, m_ada_b, v_ada_b),
               "ssm_lam_re": (ssm_lam_re, m_ssm_lam_re, v_ssm_lam_re), "ssm_lam_im": (ssm_lam_im, m_ssm_lam_im, v_ssm_lam_im),
               "ssm_log_dt": (ssm_log_dt, m_ssm_log_dt, v_ssm_log_dt), "ssm_b_re": (ssm_b_re, m_ssm_b_re, v_ssm_b_re),
               "ssm_b_im": (ssm_b_im, m_ssm_b_im, v_ssm_b_im), "ssm_c_re": (ssm_c_re, m_ssm_c_re, v_ssm_c_re),
               "ssm_c_im": (ssm_c_im, m_ssm_c_im, v_ssm_c_im), "ssm_d": (ssm_d, m_ssm_d, v_ssm_d),
               "kv_g": (kv_g, m_kv_g, v_kv_g), "kv_ada_b": (kv_ada_b, m_kv_ada_b, v_kv_ada_b),
               "final_g": (final_g, m_final_g, v_final_g)}
    names = list(small_w)
    wp, lay_w, _ = _pack([small_w[n][0] for n in names], D)
    gp, _, _ = _pack([small_grads[n] for n in names], D)
    mp, _, _ = _pack([small_w[n][1] for n in names], D)
    vp, _, _ = _pack([small_w[n][2] for n in names], D)
    _, d_p, m_p, v_p = _adamw(wp, [gp], mp, vp)
    upd = {n: (small_grads[n].reshape(small_w[n][0].shape), _unpack(d_p, lay_w, i), _unpack(m_p, lay_w, i), _unpack(v_p, lay_w, i))
           for i, n in enumerate(names)}

    def big(w, m, v, g_own, g_other=None):
        C = w.shape[-1]
        gs = [g_own.reshape(-1, C)] + ([g_other.reshape(-1, C)] if g_other is not None else [])
        return tuple(t.reshape(w.shape) for t in _adamw(w.reshape(-1, C), gs, m.reshape(-1, C), v.reshape(-1, C)))

    upd["ssm_w_glu"] = big(ssm_w_glu, m_ssm_w_glu, v_ssm_w_glu, own[0], other[0])
    upd["w_kv"] = big(w_kv, m_w_kv, v_w_kv, own[1], other[1])
    upd["attn_w_q"] = big(attn_w_q, m_attn_w_q, v_attn_w_q, own[2], other[2])
    upd["attn_w_o"] = big(attn_w_o, m_attn_w_o, v_attn_w_o, own[3], other[3])
    upd["mlp_w1"] = big(mlp_w1, m_mlp_w1, v_mlp_w1, own[4], other[4])
    upd["mlp_w2"] = big(mlp_w2, m_mlp_w2, v_mlp_w2, own[5], other[5])
    upd["ada_w"] = big(ada_w, m_ada_w, v_ada_w, g_ada_w)
    upd["kv_ada_w"] = big(kv_ada_w, m_kv_ada_w, v_kv_ada_w, g_kv_ada_w)

    order = ["ln_g", "ada_w", "ada_b", "ssm_lam_re", "ssm_lam_im", "ssm_log_dt", "ssm_b_re", "ssm_b_im", "ssm_c_re",
             "ssm_c_im", "ssm_d", "ssm_w_glu", "kv_g", "kv_ada_w", "kv_ada_b", "w_kv", "attn_w_q", "attn_w_o", "mlp_w1",
             "mlp_w2", "final_g"]
    return (loss, grad_x, *[upd[n][0] for n in order], *[upd[n][1] for n in order], *[upd[n][2] for n in order],
            *[upd[n][3] for n in order])
```

```python
import functools
import math

import jax
import jax.numpy as jnp
from jax import lax
from jax.experimental import pallas as pl
from jax.experimental.pallas import tpu as pltpu

F32 = jnp.float32
BF16 = jnp.bfloat16
MESH = pl.DeviceIdType.MESH

EPS = 1e-6
NEG = -1e30
SSM_GROUP = 16
SSM_STATE = 64
HEAD_DIM = 64
ATTN_BLOCK = 128
DILATIONS = (1, 4, 16)
ADAM_LR, ADAM_B1, ADAM_B2, ADAM_EPS, ADAM_WD, ADAM_STEP = 0.001, 0.9, 0.999, 1e-08, 0.01, 10

LANES = 128
SUBLANES = 8
CHUNK_GROUPS = LANES // SSM_GROUP
CHUNK_STATE = CHUNK_GROUPS * SSM_STATE
VMEM_LIMIT = 56 * 1024 * 1024


def _div(dim, pref, mult):
    t = min(pref, dim) // mult * mult
    while t >= mult:
        if dim % t == 0:
            return t
        t -= mult
    return dim


def _params(*sem):
    return pltpu.CompilerParams(dimension_semantics=sem, vmem_limit_bytes=VMEM_LIMIT)


def _coords():
    return lax.axis_index("x"), lax.axis_index("y"), lax.axis_index("c")


def _other_chips(cx, cy):
    return [(1 - cx, cy), (cx, 1 - cy), (1 - cx, 1 - cy)]


class _Carry:
    def __init__(self, srcs, dsts, plan, n_remote, n_local, onward=None, n_onward=0):
        self.srcs, self.dsts, self.plan, self.n_remote, self.n_local = list(srcs), list(dsts), plan, n_remote, n_local
        self.onward, self.n_onward = onward, n_onward


def _carried_call(body, *, name, grid, in_specs, out_specs, out_shape, scratch_shapes, operands, sem, carry=None):
    if carry is None:
        outs = pl.pallas_call(body, name=name, grid=grid, in_specs=in_specs, out_specs=out_specs, out_shape=out_shape,
                              scratch_shapes=scratch_shapes, compiler_params=_params(*sem))(*operands)
        return list(outs), []
    n_in, n_out, n_scr = len(in_specs), len(out_specs), len(scratch_shapes)
    ns, nd = len(carry.srcs), len(carry.dsts)

    def wrapped(*refs):
        base_in, src_refs = refs[:n_in], refs[n_in:n_in + ns]
        o0 = n_in + ns + nd
        base_out, dst_refs = refs[o0:o0 + n_out], refs[o0 + n_out:o0 + n_out + nd]
        s0 = o0 + n_out + nd
        base_scr = refs[s0:s0 + n_scr]
        send_sems, recv_sems, local_sems = refs[s0 + n_scr:]
        pids = [pl.program_id(a) for a in range(len(grid))]
        first = functools.reduce(jnp.logical_and, [p == 0 for p in pids])
        last = functools.reduce(jnp.logical_and, [p == g - 1 for p, g in zip(pids, grid)])

        def remote_copies(moves, k0):
            return [pltpu.make_async_remote_copy(src_ref=s, dst_ref=d, send_sem=send_sems.at[k0 + i], recv_sem=recv_sems.at[k0 + i],
                                                 device_id=peer, device_id_type=MESH) for i, (s, d, peer) in enumerate(moves)]

        def copies():
            remote, local = carry.plan(src_refs, dst_refs, _coords())
            return remote_copies(remote, 0), [pltpu.make_async_copy(s, d, local_sems.at[i]) for i, (s, d) in enumerate(local)]

        @pl.when(first)
        def _():
            remote, local = copies()
            for cp in local + remote:
                cp.start()

        body(*base_in, *base_out, *base_scr)

        @pl.when(last)
        def _():
            remote, local = copies()
            for cp in remote:
                cp.wait_send()
                cp.wait_recv()
            for cp in local:
                cp.wait()
            if carry.onward is not None:
                second = remote_copies(carry.onward(src_refs, dst_refs, _coords()), carry.n_remote)
                for cp in second:
                    cp.start()
                for cp in second:
                    cp.wait_send()
                    cp.wait_recv()

    anyspec = pl.BlockSpec(memory_space=pl.ANY)
    outs = pl.pallas_call(
        wrapped, name=name, grid=grid, in_specs=list(in_specs) + [anyspec] * (ns + nd),
        out_specs=list(out_specs) + [anyspec] * nd,
        out_shape=list(out_shape) + [jax.ShapeDtypeStruct(d.shape, d.dtype) for d in carry.dsts],
        scratch_shapes=list(scratch_shapes) + [pltpu.SemaphoreType.DMA((carry.n_remote + carry.n_onward,)),
                                               pltpu.SemaphoreType.DMA((carry.n_remote + carry.n_onward,)),
                                               pltpu.SemaphoreType.DMA((max(carry.n_local, 1),))],
        input_output_aliases={n_in + ns + i: n_out + i for i in range(nd)},
        compiler_params=_params(*(["arbitrary"] * len(grid))),
    )(*operands, *carry.srcs, *carry.dsts)
    return list(outs[:n_out]), list(outs[n_out:])


def _mm(name, a, b4, *, mode, M, N, K, b_lay="cs", b_l=0, b_s0=0, b_ns=1, out_dtype=F32, out_lay=None, out4_shape=None,
        out_into=None, out_l=0, out_s0=0, out_ns=1, epi=None, extras=(), rows_per_ex=None, tm=1024, tn=1024, tk=1024,
        carry=None):
    _, _, bR, bC = b4.shape
    tm = _div(M, tm, SUBLANES if M % 16 else 16)
    brows, bcols = (N, K) if mode == "nt" else (K, N)
    if b_lay == "cs":
        assert bR == brows and bC * b_ns == bcols, (name, b4.shape, brows, bcols)
    else:
        assert bC == bcols and bR * b_ns == brows, (name, b4.shape, brows, bcols)
    n_lim = N
    k_lim = K
    if mode == "nt":
        if b_lay == "cs":
            k_lim = bC
        else:
            n_lim = bR
    else:
        if b_lay == "cs":
            n_lim = bC
        else:
            k_lim = bR
    if out_lay == "cs":
        oR, oC = out4_shape[2], out4_shape[3]
        assert oR == M and oC * out_ns == N, (name, out4_shape, M, N)
        n_lim = math.gcd(n_lim, oC)
    elif out_lay == "rs":
        oR, oC = out4_shape[2], out4_shape[3]
        assert oC == N and oR * out_ns == M, (name, out4_shape, M, N)
        tm = _div(oR, tm, SUBLANES)
    tn = _div(n_lim, tn, LANES)
    tk = _div(k_lim, tk, LANES if mode != "tn" else SUBLANES)
    if mode == "tn":
        tk = _div(k_lim, tk, 16) if k_lim % 16 == 0 else tk
    nk = K // tk
    grid = (M // tm, N // tn, nk)

    if mode == "tn":
        a_spec = pl.BlockSpec((tk, tm), lambda i, j, k: (k, i))
    else:
        a_spec = pl.BlockSpec((tm, tk), lambda i, j, k: (i, k))

    def b_index(ri, ci, br, bc):
        if b_lay == "cs":
            per = bC // bc
            return (b_s0 + ci // per, b_l, ri, ci % per)
        per = bR // br
        return (b_s0 + ri // per, b_l, ri % per, ci)

    if mode == "nt":
        b_spec = pl.BlockSpec((None, None, tn, tk), lambda i, j, k: b_index(j, k, tn, tk))
    else:
        b_spec = pl.BlockSpec((None, None, tk, tn), lambda i, j, k: b_index(k, j, tk, tn))

    in_specs = [a_spec, b_spec]
    operands = [a, b4]
    for kind, arr in extras:
        if kind == "mn":
            in_specs.append(pl.BlockSpec((tm, tn), lambda i, j, k: (i, j)))
        elif kind == "ex":
            per_ex = rows_per_ex // tm
            in_specs.append(pl.BlockSpec((None, 1, tn), lambda i, j, k: (i // per_ex, 0, j)))
        else:
            in_specs.append(pl.BlockSpec((1, tn), lambda i, j, k: (0, j)))
        operands.append(arr)
    n_extra = len(extras)

    multi = isinstance(out_dtype, tuple)
    n_out = len(out_dtype) if multi else 1
    if out_lay is None:
        out_shape = [jax.ShapeDtypeStruct((M, N), dt) for dt in (out_dtype if multi else (out_dtype,))]
        out_spec = [pl.BlockSpec((tm, tn), lambda i, j, k: (i, j)) for _ in range(n_out)]
    else:
        out_shape = [jax.ShapeDtypeStruct(tuple(out4_shape), out_dtype)]
        if out_lay == "cs":
            per_o = oC // tn
            out_spec = [pl.BlockSpec((None, None, tm, tn), lambda i, j, k: (out_s0 + j // per_o, out_l, i, j % per_o))]
        else:
            per_o = oR // tm
            out_spec = [pl.BlockSpec((None, None, tm, tn), lambda i, j, k: (out_s0 + i // per_o, out_l, i % per_o, j))]
    aliases = {}
    if out_into is not None:
        in_specs.append(pl.BlockSpec(memory_space=pl.ANY))
        operands.append(out_into)
        aliases = {len(operands) - 1: 0}

    dims = {"nn": (((1,), (0,)), ((), ())), "nt": (((1,), (1,)), ((), ())), "tn": (((0,), (0,)), ((), ()))}[mode]

    def body(a_ref, b_ref, *rest):
        extra_refs = rest[:n_extra]
        o_refs = rest[len(rest) - n_out - (nk > 1):len(rest) - (nk > 1)]

        def finish(r):
            if epi is not None:
                r = epi(r, *[e[...] for e in extra_refs])
            for o_ref, val in zip(o_refs, r if multi else (r,)):
                o_ref[...] = val.astype(o_ref.dtype)

        part = lax.dot_general(a_ref[...].astype(BF16), b_ref[...].astype(BF16), dims, preferred_element_type=F32)
        if nk == 1:
            finish(part)
            return
        acc = rest[-1]
        k = pl.program_id(2)

        @pl.when(k == 0)
        def _():
            acc[...] = part

        @pl.when(k != 0)
        def _():
            acc[...] += part

        @pl.when(k == nk - 1)
        def _():
            finish(acc[...])

    scratch = [pltpu.VMEM((tm, tn), F32)] if nk > 1 else []
    if carry is not None:
        assert out_into is None, name
        outs, moved = _carried_call(body, name=name, grid=grid, in_specs=in_specs, out_specs=out_spec, out_shape=out_shape,
                                    scratch_shapes=scratch, operands=operands, sem=("arbitrary",) * 3, carry=carry)
        return (tuple(outs) if multi else outs[0]), moved
    outs = pl.pallas_call(
        body, name=name, grid=grid, in_specs=in_specs, out_specs=out_spec, out_shape=out_shape,
        scratch_shapes=scratch, input_output_aliases=aliases,
        compiler_params=_params("parallel", "parallel", "arbitrary"),
    )(*operands)
    return tuple(outs) if multi else outs[0]


def _as4(w):
    return w.reshape((1, 1) + w.shape)


def _relu2(acc):
    r = jnp.maximum(acc, 0.0)
    return r * r


def _relu2_bwd(acc, r):
    return acc * (2.0 * jnp.sqrt(r.astype(F32)))


def _add(acc, e):
    return acc + e


def _gated_residual(acc, h, gate):
    return acc, h + gate * acc


def _modulated_norm(x, g, scale, shift):
    rstd = lax.rsqrt(jnp.mean(x * x, axis=-1, keepdims=True) + EPS)
    return ((x * rstd) * g) * (1.0 + scale) + shift


def _gated_residual_norm(acc, h, gate, g, scale, shift):
    h_new = h + gate * acc
    return acc, h_new, _modulated_norm(h_new, g, scale, shift)


def _row_tiles(N, B, pref=256):
    S = N // B
    tm = _div(S, pref, SUBLANES)
    return tm, S // tm


def _normmod(h, g, scale, shift, B):
    N, D = h.shape
    tm, per_ex = _row_tiles(N, B)

    def body(h_ref, g_ref, sc_ref, sh_ref, u_ref):
        u_ref[...] = _modulated_norm(h_ref[...], g_ref[...], sc_ref[...], sh_ref[...]).astype(u_ref.dtype)

    tok = pl.BlockSpec((tm, D), lambda i: (i, 0))
    vec = pl.BlockSpec((1, D), lambda i: (0, 0))
    ex = pl.BlockSpec((None, 1, D), lambda i: (i // per_ex, 0, 0))
    return pl.pallas_call(
        body, name="normmod_fwd", grid=(N // tm,), in_specs=[tok, vec, ex, ex], out_specs=tok,
        out_shape=jax.ShapeDtypeStruct((N, D), BF16), compiler_params=_params("parallel"),
    )(h, g, scale, shift)


def _normmod_bwd(du, h, g, scale, dh_in, B, below=None):
    N, D = h.shape
    tm, per_ex = _row_tiles(N, B)
    fused = below is not None

    def body(*refs):
        du_ref, h_ref, g_ref, sc_ref, dhin_ref = refs[:5]
        dh_ref, dg_ref, dsc_ref, dsh_ref = refs[5 + 2 * fused:9 + 2 * fused]
        i = pl.program_id(0)
        x = h_ref[...]
        gv = g_ref[...]
        d_u = du_ref[...].astype(F32)
        rstd = lax.rsqrt(jnp.mean(x * x, axis=-1, keepdims=True) + EPS)
        xn = x * rstd
        dyg = d_u * (1.0 + sc_ref[...])
        dxn = dyg * gv
        dh = dhin_ref[...] + rstd * (dxn - xn * jnp.mean(dxn * xn, axis=-1, keepdims=True))
        dh_ref[...] = dh
        sums = [(dsc_ref, jnp.sum(d_u * (xn * gv), axis=0, keepdims=True)), (dsh_ref, jnp.sum(d_u, axis=0, keepdims=True))]
        if fused:
            y_ref, gt_ref = refs[5:7]
            dy_ref, dgt_ref = refs[9 + 2 * fused:]
            dy_ref[...] = (gt_ref[...] * dh).astype(dy_ref.dtype)
            sums.append((dgt_ref, jnp.sum(dh * y_ref[...], axis=0, keepdims=True)))
        dg_t = jnp.sum(dyg * xn, axis=0, keepdims=True)

        @pl.when(i % per_ex == 0)
        def _():
            for ref, val in sums:
                ref[...] = val

        @pl.when(i % per_ex != 0)
        def _():
            for ref, val in sums:
                ref[...] += val

        @pl.when(i == 0)
        def _():
            dg_ref[...] = dg_t

        @pl.when(i != 0)
        def _():
            dg_ref[...] += dg_t

    tok = pl.BlockSpec((tm, D), lambda i: (i, 0))
    vec = pl.BlockSpec((1, D), lambda i: (0, 0))
    ex = pl.BlockSpec((None, 1, D), lambda i: (i // per_ex, 0, 0))
    per_ex_shape = jax.ShapeDtypeStruct((B, 1, D), F32)
    outs = pl.pallas_call(
        body, name="normmod_bwd", grid=(N // tm,), in_specs=[tok, tok, vec, ex, tok] + ([tok, ex] if fused else []),
        out_specs=[tok, vec, ex, ex] + ([tok, ex] if fused else []),
        out_shape=[jax.ShapeDtypeStruct((N, D), F32), jax.ShapeDtypeStruct((1, D), F32), per_ex_shape, per_ex_shape]
        + ([jax.ShapeDtypeStruct((N, D), BF16), per_ex_shape] if fused else []),
        compiler_params=_params("arbitrary"),
    )(du, h, g, scale, dh_in, *(below if fused else ()))
    return tuple(outs) if fused else tuple(outs) + (None, None)


def _residual_bwd(dh, gate, y, B):
    N, D = dh.shape
    tm, per_ex = _row_tiles(N, B)

    def body(dh_ref, gt_ref, y_ref, dy_ref, dgt_ref):
        i = pl.program_id(0)
        d = dh_ref[...]
        dy_ref[...] = (gt_ref[...] * d).astype(dy_ref.dtype)
        t = jnp.sum(d * y_ref[...], axis=0, keepdims=True)

        @pl.when(i % per_ex == 0)
        def _():
            dgt_ref[...] = t

        @pl.when(i % per_ex != 0)
        def _():
            dgt_ref[...] += t

    tok = pl.BlockSpec((tm, D), lambda i: (i, 0))
    ex = pl.BlockSpec((None, 1, D), lambda i: (i // per_ex, 0, 0))
    return pl.pallas_call(
        body, name="residual_bwd", grid=(N // tm,), in_specs=[tok, ex, tok], out_specs=[tok, ex],
        out_shape=[jax.ShapeDtypeStruct((N, D), BF16), jax.ShapeDtypeStruct((B, 1, D), F32)],
        compiler_params=_params("arbitrary"),
    )(dh, gate, y)


def _glu_residual_norm(zz, h, gate, g, scale, shift, B):
    N, D2 = zz.shape
    D = D2 // 2
    tm, per_ex = _row_tiles(N, B)

    def body(v_ref, g_ref, h_ref, gt_ref, ng_ref, sc_ref, sh_ref, y_ref, o_ref, u_ref):
        y = v_ref[...] * jax.nn.sigmoid(g_ref[...])
        y_ref[...] = y.astype(y_ref.dtype)
        h_new = h_ref[...] + gt_ref[...] * y
        o_ref[...] = h_new
        u_ref[...] = _modulated_norm(h_new, ng_ref[...], sc_ref[...], sh_ref[...]).astype(u_ref.dtype)

    tok = pl.BlockSpec((tm, D), lambda i: (i, 0))
    vec = pl.BlockSpec((1, D), lambda i: (0, 0))
    ex = pl.BlockSpec((None, 1, D), lambda i: (i // per_ex, 0, 0))
    return pl.pallas_call(
        body, name="glu_fwd", grid=(N // tm,),
        in_specs=[tok, pl.BlockSpec((tm, D), lambda i: (i, 1)), tok, ex, vec, ex, ex], out_specs=[tok, tok, tok],
        out_shape=[jax.ShapeDtypeStruct((N, D), BF16), jax.ShapeDtypeStruct((N, D), F32), jax.ShapeDtypeStruct((N, D), BF16)],
        compiler_params=_params("parallel"),
    )(zz, zz, h, gate, g, scale, shift)


def _glu_bwd(dy, zz):
    N, D2 = zz.shape
    D = D2 // 2
    tm = _div(N, 256, SUBLANES)

    def body(dy_ref, v_ref, g_ref, o_ref):
        d = dy_ref[...].astype(F32)
        s = jax.nn.sigmoid(g_ref[...])
        o_ref[...] = jnp.concatenate([d * s, d * v_ref[...] * s * (1.0 - s)], axis=1).astype(o_ref.dtype)

    return pl.pallas_call(
        body, name="glu_bwd", grid=(N // tm,),
        in_specs=[pl.BlockSpec((tm, D), lambda i: (i, 0)), pl.BlockSpec((tm, D), lambda i: (i, 0)),
                  pl.BlockSpec((tm, D), lambda i: (i, 1))],
        out_specs=pl.BlockSpec((tm, D2), lambda i: (i, 0)), out_shape=jax.ShapeDtypeStruct((N, D2), BF16),
        compiler_params=_params("parallel"),
    )(dy, zz, zz)


def _loss_head(h, g, target):
    N, D = h.shape
    tm = _div(N, 256, SUBLANES)

    def body(h_ref, g_ref, t_ref, loss_ref, dh_ref, dg_ref):
        i = pl.program_id(0)
        x = h_ref[...]
        gv = g_ref[...]
        rstd = lax.rsqrt(jnp.mean(x * x, axis=-1, keepdims=True) + EPS)
        xn = x * rstd
        err = xn * gv - t_ref[...]
        part = 0.5 * jnp.sum(jnp.sum(err * err, axis=-1, keepdims=True) / D, axis=0, keepdims=True)
        dy = err / D
        dxn = dy * gv
        dh_ref[...] = rstd * (dxn - xn * jnp.mean(dxn * xn, axis=-1, keepdims=True))
        dg_t = jnp.sum(dy * xn, axis=0, keepdims=True)
        part = jnp.broadcast_to(part, loss_ref.shape)

        @pl.when(i == 0)
        def _():
            loss_ref[...] = part
            dg_ref[...] = dg_t

        @pl.when(i != 0)
        def _():
            loss_ref[...] += part
            dg_ref[...] += dg_t

    tok = pl.BlockSpec((tm, D), lambda i: (i, 0))
    vec = pl.BlockSpec((1, D), lambda i: (0, 0))
    return pl.pallas_call(
        body, name="loss_head", grid=(N // tm,), in_specs=[tok, vec, tok],
        out_specs=[pl.BlockSpec((SUBLANES, LANES), lambda i: (0, 0)), tok, vec],
        out_shape=[jax.ShapeDtypeStruct((SUBLANES, LANES), F32), jax.ShapeDtypeStruct((N, D), F32),
                   jax.ShapeDtypeStruct((1, D), F32)],
        compiler_params=_params("arbitrary"),
    )(h, g, target)


def _swap_halves(x):
    half = x.shape[-1] // 2
    return jnp.concatenate([x[:, half:], x[:, :half]], axis=1)


def _gelu(y):
    return jax.nn.gelu(y)


def _gelu_grad(y):
    c0 = math.sqrt(2.0 / math.pi)
    inner = c0 * (y + 0.044715 * y * y * y)
    t = jnp.tanh(inner)
    return 0.5 * (1.0 + t) + 0.5 * y * (1.0 - t * t) * c0 * (1.0 + 3.0 * 0.044715 * y * y)


def _s5_discretize(lam_re, lam_im, log_dt, b_re, b_im, c_re, c_im):
    G = lam_re.shape[0]
    nch = G // CHUNK_GROUPS
    dt = jnp.exp(log_dt)[:, None]
    er = jnp.exp(lam_re * dt)
    a_re = er * jnp.cos(lam_im * dt)
    a_im = er * jnp.sin(lam_im * dt)
    den = lam_re * lam_re + lam_im * lam_im
    n_re, n_im = a_re - 1.0, a_im
    f_re = (n_re * lam_re + n_im * lam_im) / den
    f_im = (n_im * lam_re - n_re * lam_im) / den
    bb_re = f_re[..., None] * b_re - f_im[..., None] * b_im
    bb_im = f_re[..., None] * b_im + f_im[..., None] * b_re
    eye = jnp.eye(CHUNK_GROUPS, dtype=F32)

    def pack_b(bb):
        bb = bb.reshape(nch, CHUNK_GROUPS, SSM_STATE, SSM_GROUP)
        return jnp.einsum("jgpc,gh->jgchp", bb, eye).reshape(nch, LANES, CHUNK_STATE)

    def pack_c(cc):
        cc = cc.reshape(nch, CHUNK_GROUPS, SSM_GROUP, SSM_STATE)
        return jnp.einsum("jgcp,gh->jgphc", cc, eye).reshape(nch, CHUNK_STATE, LANES)

    bd = jnp.concatenate([pack_b(bb_re), pack_b(bb_im)], axis=2)
    cd = jnp.concatenate([pack_c(c_re), pack_c(-c_im)], axis=1)
    return bd, cd, a_re, a_im


S5_TILE = 1024
S5_SEG = S5_TILE // SUBLANES
S5_UNROLL = 8


def _s5_scan_coefs(lam_re, lam_im, log_dt, seg):
    G = lam_re.shape[0]
    nch = G // CHUNK_GROUPS
    dt = jnp.exp(log_dt)[:, None]
    rate = (lam_re * dt).reshape(nch, 1, CHUNK_STATE)
    freq = (lam_im * dt).reshape(nch, 1, CHUNK_STATE)

    def powers(ks):
        k = jnp.asarray(ks, F32)[None, :, None]
        er = jnp.exp(k * rate)
        re, im = er * jnp.cos(k * freq), er * jnp.sin(k * freq)
        return jnp.concatenate([re, re], axis=2), jnp.concatenate([-im, im], axis=2)

    pw = jnp.stack(powers(range(1, seg + 1)), axis=1)
    steps = (1, 2, 4)
    re, im = powers([s * seg for s in steps])
    row = jnp.arange(SUBLANES, dtype=jnp.int32)[None, None, :, None]
    shift = jnp.asarray(steps, jnp.int32)[None, :, None, None]

    def table(reverse):
        mask = (row < SUBLANES - shift) if reverse else (row >= shift)
        pair = jnp.stack([jnp.where(mask, re[:, :, None, :], 0.0),
                          jnp.where(mask, (-im if reverse else im)[:, :, None, :], 0.0)], axis=2)
        return pair.reshape(nch, 2 * len(steps), SUBLANES, 2 * CHUNK_STATE)

    return pw, table(False), table(True)


def _to_segments(dst_s, src_ref, seg):
    for j in range(SUBLANES):
        dst_s[pl.ds(j, seg, stride=SUBLANES), :] = src_ref[pl.ds(j * seg, seg), :].astype(F32)


def _from_segments(dst_ref, src_s, seg):
    for j in range(SUBLANES):
        dst_ref[pl.ds(j * seg, seg), :] = src_s[pl.ds(j, seg, stride=SUBLANES), :].astype(dst_ref.dtype)


def _seg_scan(x_ref, pw_ref, seg_ref, carry_ref, c_ref, seg, reverse):
    W = x_ref.shape[-1]
    tm = x_ref.shape[0]
    sgn = -1.0 if reverse else 1.0
    ar = jnp.broadcast_to(pw_ref[0, 0:1, :], (SUBLANES, W))
    ai = sgn * jnp.broadcast_to(pw_ref[1, 0:1, :], (SUBLANES, W))

    def rows(i):
        return pl.ds(pl.multiple_of(i * SUBLANES, SUBLANES), SUBLANES)

    def step(t, prev):
        i = (seg - 2 - t) if reverse else (t + 1)
        x = x_ref[rows(i), :] + ar * prev + ai * _swap_halves(prev)
        x_ref[rows(i), :] = x
        return x

    start = (seg - 1) * SUBLANES if reverse else 0
    edge = lax.fori_loop(0, seg - 1, step, x_ref[start:start + SUBLANES, :], unroll=S5_UNROLL)
    row = lax.broadcasted_iota(jnp.int32, (SUBLANES, W), 0)
    if reverse:
        f = jnp.where(row == SUBLANES - 1, carry_ref[...], pltpu.roll(edge, SUBLANES - 1, 0))
    else:
        f = jnp.where(row == 0, carry_ref[...], pltpu.roll(edge, 1, 0))
    for si, s in enumerate((1, 2, 4)):
        fs = pltpu.roll(f, (SUBLANES - s) if reverse else s, 0)
        f = f + seg_ref[2 * si] * fs + seg_ref[2 * si + 1] * _swap_halves(fs)
    c_ref[...] = f
    fsw = _swap_halves(f)

    def fix(i, _):
        k = (seg - 1 - i) if reverse else i
        x_ref[rows(i), :] = x_ref[rows(i), :] + pw_ref[0, pl.ds(k, 1), :] * f + (sgn * pw_ref[1, pl.ds(k, 1), :]) * fsw
        return 0

    lax.fori_loop(0, seg, fix, 0, unroll=S5_UNROLL)
    leaving = x_ref[0:1, :] if reverse else x_ref[tm - 1:tm, :]
    carry_ref[...] = jnp.broadcast_to(leaving, carry_ref.shape)


def _s5_fwd(u, bd, cd, pw, seg_f, d_skip, B, carry=None):
    N, D = u.shape
    S = N // B
    nch = D // LANES
    W = 2 * CHUNK_STATE
    tm, seg = S5_TILE, S5_SEG
    nt = S // tm

    def body(u_ref, bd_ref, cd_ref, pw_ref, seg_ref, d_ref, z_ref, cin_ref, x_s, carry, c_s, u_s, z_s):
        t = pl.program_id(2)

        @pl.when(t == 0)
        def _():
            carry[...] = jnp.zeros_like(carry)

        cin_ref[...] = carry[...]
        _to_segments(u_s, u_ref, seg)
        uf = u_s[...]
        x_s[...] = jnp.dot(uf.astype(BF16), bd_ref[...], preferred_element_type=F32)
        _seg_scan(x_s, pw_ref, seg_ref, carry, c_s, seg, False)
        y = jnp.dot(x_s[...].astype(BF16), cd_ref[...], preferred_element_type=F32) + d_ref[...] * uf
        z_s[...] = _gelu(y)
        _from_segments(z_ref, z_s, seg)

    (z, carries), moved = _carried_call(
        body, name="s5_fwd", grid=(nch, B, nt),
        in_specs=[pl.BlockSpec((tm, LANES), lambda j, b, t: (b * nt + t, j)),
                  pl.BlockSpec((None, LANES, W), lambda j, b, t: (j, 0, 0)),
                  pl.BlockSpec((None, W, LANES), lambda j, b, t: (j, 0, 0)),
                  pl.BlockSpec((None, 2, seg, W), lambda j, b, t: (j, 0, 0, 0)),
                  pl.BlockSpec((None, 6, SUBLANES, W), lambda j, b, t: (j, 0, 0, 0)),
                  pl.BlockSpec((1, LANES), lambda j, b, t: (0, j))],
        out_specs=[pl.BlockSpec((tm, LANES), lambda j, b, t: (b * nt + t, j)),
                   pl.BlockSpec((None, None, SUBLANES, W), lambda j, b, t: (j, b * nt + t, 0, 0))],
        out_shape=[jax.ShapeDtypeStruct((N, D), BF16), jax.ShapeDtypeStruct((nch, B * nt, SUBLANES, W), F32)],
        scratch_shapes=[pltpu.VMEM((tm, W), F32), pltpu.VMEM((SUBLANES, W), F32), pltpu.VMEM((SUBLANES, W), F32),
                        pltpu.VMEM((tm, LANES), F32), pltpu.VMEM((tm, LANES), F32)],
        operands=(u, bd, cd, pw, seg_f, d_skip), sem=("parallel", "arbitrary", "arbitrary"), carry=carry)
    return z, carries, moved


def _s5_bwd(u, dz, bd, cd, pw, seg_f, seg_b, d_skip, carries, B, carry=None):
    N, D = u.shape
    S = N // B
    nch = D // LANES
    W = 2 * CHUNK_STATE
    tm, seg = S5_TILE, S5_SEG
    nt = S // tm
    tn_dims = (((0,), (0,)), ((), ()))
    nt_dims = (((1,), (1,)), ((), ()))

    def body(u_ref, dz_ref, bd_ref, cd_ref, pw_ref, sf_ref, sb_ref, d_ref, cin_ref,
             du_ref, dbd_ref, dcd_ref, da_ref, dd_ref, x_s, l_s, carry, lcarry, c_s, lc_s, u_s, t_s):
        b = pl.program_id(1)
        t = pl.program_id(2)

        @pl.when((b == 0) & (t == 0))
        def _():
            dbd_ref[...] = jnp.zeros_like(dbd_ref)
            dcd_ref[...] = jnp.zeros_like(dcd_ref)
            da_ref[...] = jnp.zeros_like(da_ref)
            dd_ref[...] = jnp.zeros_like(dd_ref)

        @pl.when(t == 0)
        def _():
            lcarry[...] = jnp.zeros_like(lcarry)

        _to_segments(u_s, u_ref, seg)
        _to_segments(t_s, dz_ref, seg)
        uf = u_s[...]
        uv = uf.astype(BF16)
        carry[...] = cin_ref[...]
        x_s[...] = jnp.dot(uv, bd_ref[...], preferred_element_type=F32)
        _seg_scan(x_s, pw_ref, sf_ref, carry, c_s, seg, False)
        xb = x_s[...].astype(BF16)
        y = jnp.dot(xb, cd_ref[...], preferred_element_type=F32) + d_ref[...] * uf
        dy = t_s[...] * _gelu_grad(y)
        dd_ref[...] += jnp.sum(dy * uf, axis=0, keepdims=True)
        dyb = dy.astype(BF16)
        dcd_ref[...] += lax.dot_general(dyb, xb, tn_dims, preferred_element_type=F32)
        l_s[...] = lax.dot_general(dyb, cd_ref[...], nt_dims, preferred_element_type=F32)
        _seg_scan(l_s, pw_ref, sb_ref, lcarry, lc_s, seg, True)
        lb = l_s[...].astype(BF16)
        dbd_ref[...] += lax.dot_general(uv, lb, tn_dims, preferred_element_type=F32)
        t_s[...] = lax.dot_general(lb, bd_ref[...], nt_dims, preferred_element_type=F32) + d_ref[...] * dy
        _from_segments(du_ref, t_s, seg)
        lam_rest, x_prev = l_s[SUBLANES:, :], x_s[:tm - SUBLANES, :]
        lam_0, c_in = l_s[:SUBLANES, :], c_s[...]
        da_ref[0:1, :] += (jnp.sum(lam_rest * x_prev, axis=0, keepdims=True) + jnp.sum(lam_0 * c_in, axis=0, keepdims=True))
        da_ref[1:2, :] += (jnp.sum(lam_rest * _swap_halves(x_prev), axis=0, keepdims=True)
                           + jnp.sum(lam_0 * _swap_halves(c_in), axis=0, keepdims=True))

    tile = lambda j, b, t: (b * nt + (nt - 1 - t), j)
    chunk3 = lambda j, b, t: (j, 0, 0)
    chunk4 = lambda j, b, t: (j, 0, 0, 0)
    outs, moved = _carried_call(
        body, name="s5_bwd", grid=(nch, B, nt),
        in_specs=[pl.BlockSpec((tm, LANES), tile), pl.BlockSpec((tm, LANES), tile),
                  pl.BlockSpec((None, LANES, W), chunk3), pl.BlockSpec((None, W, LANES), chunk3),
                  pl.BlockSpec((None, 2, seg, W), chunk4), pl.BlockSpec((None, 6, SUBLANES, W), chunk4),
                  pl.BlockSpec((None, 6, SUBLANES, W), chunk4), pl.BlockSpec((1, LANES), lambda j, b, t: (0, j)),
                  pl.BlockSpec((None, None, SUBLANES, W), lambda j, b, t: (j, b * nt + (nt - 1 - t), 0, 0))],
        out_specs=[pl.BlockSpec((tm, LANES), tile), pl.BlockSpec((None, LANES, W), chunk3),
                   pl.BlockSpec((None, LANES, W), chunk3), pl.BlockSpec((None, 2, W), chunk3),
                   pl.BlockSpec((1, LANES), lambda j, b, t: (0, j))],
        out_shape=[jax.ShapeDtypeStruct((N, D), F32), jax.ShapeDtypeStruct((nch, LANES, W), F32),
                   jax.ShapeDtypeStruct((nch, LANES, W), F32), jax.ShapeDtypeStruct((nch, 2, W), F32),
                   jax.ShapeDtypeStruct((1, D), F32)],
        scratch_shapes=[pltpu.VMEM((tm, W), F32), pltpu.VMEM((tm, W), F32)] + [pltpu.VMEM((SUBLANES, W), F32)] * 4
        + [pltpu.VMEM((tm, LANES), F32)] * 2,
        operands=(u, dz, bd, cd, pw, seg_f, seg_b, d_skip, carries), sem=("parallel", "arbitrary", "arbitrary"), carry=carry)
    return (*outs, moved)


ATTN_HEADS = LANES // HEAD_DIM
ATTN_FWD_UNROLL = 4
ATTN_BWD_UNROLL = 4


def _attn_mask(n):
    qi = lax.broadcasted_iota(jnp.int32, (ATTN_HEADS * ATTN_BLOCK, 2 * ATTN_BLOCK), 0) % ATTN_BLOCK
    kj = lax.broadcasted_iota(jnp.int32, (ATTN_HEADS * ATTN_BLOCK, 2 * ATTN_BLOCK), 1)
    prev_ok = (kj < ATTN_BLOCK) & (kj >= qi) & (n > 0)
    return prev_ok | ((kj >= ATTN_BLOCK) & (kj - ATTN_BLOCK <= qi))


def _stack_heads(x):
    return jnp.concatenate([_only_head(x, h) for h in range(ATTN_HEADS)], axis=0)


def _stack_head_columns(x):
    return jnp.concatenate([x[:, h * HEAD_DIM:h * HEAD_DIM + 1] for h in range(ATTN_HEADS)], axis=0)


def _unstack_heads(x):
    return _per_head([x[h * ATTN_BLOCK:(h + 1) * ATTN_BLOCK] for h in range(ATTN_HEADS)])


def _head_lanes(h):
    lane = lax.broadcasted_iota(jnp.int32, (ATTN_BLOCK, LANES), 1)
    return (lane >= h * HEAD_DIM) & (lane < (h + 1) * HEAD_DIM)


def _per_head(cols):
    out = jnp.broadcast_to(cols[-1], (ATTN_BLOCK, LANES))
    for h in range(len(cols) - 2, -1, -1):
        out = jnp.where(_head_lanes(h), jnp.broadcast_to(cols[h], (ATTN_BLOCK, LANES)), out)
    return out


def _only_head(x, h):
    return jnp.where(_head_lanes(h), x, 0.0).astype(BF16)


def _block_rows(tb, dil, nb):
    r = tb // nb
    n = tb % nb
    start = r + dil * ATTN_BLOCK * n
    startp = jnp.where(n > 0, start - dil * ATTN_BLOCK, start)
    return n, pl.ds(start, ATTN_BLOCK, stride=dil), pl.ds(startp, ATTN_BLOCK, stride=dil)


def _attn_fwd(q, k, v, B, carry=None):
    _, S, D3 = q.shape
    D = D3 // 3
    HP = D // LANES
    scale = HEAD_DIM ** -0.5
    n_blocks = S // ATTN_BLOCK
    nbr = len(DILATIONS)
    nt_dims = (((1,), (1,)), ((), ()))

    def branch(dil, q_ref, k_ref, v_ref, acc, m_s, l_s):
        nb = (S // dil) // ATTN_BLOCK

        def blk(tb, _):
            n, rows, rowsp = _block_rows(tb, dil, nb)
            qb = q_ref[rows, :] * scale
            kk = jnp.concatenate([k_ref[rowsp, :], k_ref[rows, :]], axis=0).astype(BF16)
            vv = jnp.concatenate([v_ref[rowsp, :], v_ref[rows, :]], axis=0).astype(BF16)
            s = lax.dot_general(_stack_heads(qb), kk, nt_dims, preferred_element_type=F32)
            s = jnp.where(_attn_mask(n), s, NEG)
            m = jnp.max(s, axis=-1, keepdims=True)
            p = jnp.exp(s - m)
            m_s[rows, :] = _unstack_heads(m)
            l_s[rows, :] = _unstack_heads(jnp.sum(p, axis=-1, keepdims=True))
            acc[rows, :] = _unstack_heads(jnp.dot(p.astype(BF16), vv, preferred_element_type=F32))
            return 0

        lax.fori_loop(0, n_blocks, blk, 0, unroll=ATTN_FWD_UNROLL)

    def body(q_ref, k_ref, v_ref, o_ref, lse_ref, *scratch):
        accs, m_ss, l_ss = scratch[:nbr], scratch[nbr:2 * nbr], scratch[2 * nbr:]
        g = pl.program_id(2)
        for gi, dil in enumerate(DILATIONS):
            pl.when(g == gi)(functools.partial(branch, dil, q_ref, k_ref, v_ref, accs[gi], m_ss[gi], l_ss[gi]))

        @pl.when(g == nbr - 1)
        def _():
            def fin(i, _):
                rows = pl.ds(pl.multiple_of(i * ATTN_BLOCK, ATTN_BLOCK), ATTN_BLOCK)
                ms = [m[rows, :] for m in m_ss]
                m_all = functools.reduce(jnp.maximum, ms)
                ws = [jnp.exp(m - m_all) for m in ms]
                den = sum(w * l[rows, :] for w, l in zip(ws, l_ss))
                o_ref[rows, :] = sum(w * a[rows, :] for w, a in zip(ws, accs)) / den
                lse_ref[rows, :] = m_all + jnp.log(den)
                return 0

            lax.fori_loop(0, n_blocks, fin, 0)

    br = pl.BlockSpec((None, S, LANES), lambda b, hp, g: (b, 0, g * HP + hp))
    hd = pl.BlockSpec((None, S, LANES), lambda b, hp, g: (b, 0, hp))
    (o, lse), moved = _carried_call(
        body, name="attn_fwd", grid=(B, HP, nbr), in_specs=[br, br, br], out_specs=[hd, hd],
        out_shape=[jax.ShapeDtypeStruct((B, S, D), F32), jax.ShapeDtypeStruct((B, S, D), F32)],
        scratch_shapes=[pltpu.VMEM((S, LANES), F32)] * (3 * nbr),
        operands=(q, k, v), sem=("parallel", "parallel", "arbitrary"), carry=carry)
    return o, lse, moved


def _attn_bwd(q, k, v, o, lse, do, dk_prev, dv_prev, B, last, carry=None):
    _, S, D3 = q.shape
    D = D3 // 3
    HP = D // LANES
    scale = HEAD_DIM ** -0.5
    n_blocks = S // ATTN_BLOCK
    has_prev = dk_prev is not None
    nt_dims = (((1,), (1,)), ((), ()))
    tn_dims = (((0,), (0,)), ((), ()))

    def branch(dil, q_ref, k_ref, v_ref, lse_ref, do_ref, dq_s, dk_c, dv_c, delta, dk_p, dv_p):
        nb = (S // dil) // ATTN_BLOCK

        def blk(tb, _):
            n, rows, rowsp = _block_rows(tb, dil, nb)
            qb = q_ref[rows, :] * scale
            dob, lb, db = do_ref[rows, :], lse_ref[rows, :], delta[rows, :]
            kk = jnp.concatenate([k_ref[rowsp, :], k_ref[rows, :]], axis=0).astype(BF16)
            vv = jnp.concatenate([v_ref[rowsp, :], v_ref[rows, :]], axis=0).astype(BF16)
            qs, dos = _stack_heads(qb), _stack_heads(dob)
            s = lax.dot_general(qs, kk, nt_dims, preferred_element_type=F32)
            p = jnp.where(_attn_mask(n), jnp.exp(s - _stack_head_columns(lb)), 0.0)
            dp = lax.dot_general(dos, vv, nt_dims, preferred_element_type=F32)
            ds = (p * (dp - _stack_head_columns(db))).astype(BF16)
            dkk = lax.dot_general(ds, qs, tn_dims, preferred_element_type=F32)
            dvv = lax.dot_general(p.astype(BF16), dos, tn_dims, preferred_element_type=F32)
            dq_s[rows, :] = _unstack_heads(jnp.dot(ds, kk, preferred_element_type=F32)) * scale
            dk_p[rowsp, :] = dkk[:ATTN_BLOCK]
            dv_p[rowsp, :] = dvv[:ATTN_BLOCK]
            dk_c[rows, :] = dkk[ATTN_BLOCK:]
            dv_c[rows, :] = dvv[ATTN_BLOCK:]
            return 0

        lax.fori_loop(0, n_blocks, blk, 0, unroll=ATTN_BWD_UNROLL)

    def body(*refs):
        q_ref, k_ref, v_ref, o_ref, lse_ref, do_ref = refs[:6]
        n_in = 8 if has_prev else 6
        dq_ref, dk_ref, dv_ref, delta, dk_p, dv_p, dq_s, dk_c, dv_c = refs[n_in:n_in + 9]
        g = pl.program_id(2)

        @pl.when(g == 0)
        def _():
            def dl(i, _):
                rows = pl.ds(pl.multiple_of(i * ATTN_BLOCK, ATTN_BLOCK), ATTN_BLOCK)
                prod = do_ref[rows, :] * o_ref[rows, :]
                delta[rows, :] = _per_head([jnp.sum(jnp.where(_head_lanes(h), prod, 0.0), axis=-1, keepdims=True)
                                            for h in range(ATTN_HEADS)])
                return 0

            lax.fori_loop(0, n_blocks, dl, 0)

        dk_p[...] = jnp.zeros_like(dk_p)
        dv_p[...] = jnp.zeros_like(dv_p)
        for gi, dil in enumerate(DILATIONS):
            pl.when(g == gi)(functools.partial(branch, dil, q_ref, k_ref, v_ref, lse_ref, do_ref, dq_s, dk_c, dv_c,
                                               delta, dk_p, dv_p))

        def fin(i, _):
            rows = pl.ds(pl.multiple_of(i * ATTN_BLOCK, ATTN_BLOCK), ATTN_BLOCK)
            dk_t = dk_c[rows, :] + dk_p[rows, :]
            dv_t = dv_c[rows, :] + dv_p[rows, :]
            if has_prev:
                dk_t = dk_t + refs[6][rows, :].astype(F32)
                dv_t = dv_t + refs[7][rows, :].astype(F32)
            dq_ref[rows, :] = dq_s[rows, :].astype(dq_ref.dtype)
            dk_ref[rows, :] = dk_t.astype(dk_ref.dtype)
            dv_ref[rows, :] = dv_t.astype(dv_ref.dtype)
            return 0

        lax.fori_loop(0, n_blocks, fin, 0)

    br = pl.BlockSpec((None, S, LANES), lambda b, hp, g: (b, 0, g * HP + hp))
    hd = pl.BlockSpec((None, S, LANES), lambda b, hp, g: (b, 0, hp))
    ins = [q, k, v, o, lse, do] + ([dk_prev, dv_prev] if has_prev else [])
    kv_dtype = BF16 if last else F32
    (dq, dk, dv), moved = _carried_call(
        body, name="attn_bwd", grid=(B, HP, len(DILATIONS)),
        in_specs=[br, br, br, hd, hd, hd] + ([br, br] if has_prev else []), out_specs=[br, br, br],
        out_shape=[jax.ShapeDtypeStruct(q.shape, BF16), jax.ShapeDtypeStruct(q.shape, kv_dtype),
                   jax.ShapeDtypeStruct(q.shape, kv_dtype)],
        scratch_shapes=[pltpu.VMEM((S, LANES), F32)] * 6,
        operands=ins, sem=("parallel", "parallel", "arbitrary"), carry=carry)
    return dq, dk, dv, moved


def _adamw(w, grads, m, v):
    R, C = w.shape
    tr = _div(R, 256, SUBLANES)
    ng = len(grads)
    c1 = 1.0 - ADAM_B1 ** ADAM_STEP
    c2 = 1.0 - ADAM_B2 ** ADAM_STEP

    def body(*refs):
        w_ref, m_ref, v_ref = refs[0], refs[1 + ng], refs[2 + ng]
        d_ref, mo_ref, vo_ref = refs[3 + ng:6 + ng]
        g = refs[1][...]
        if ng == 2:
            g = g + refs[2][...]
            refs[6 + ng][...] = g
        mn = ADAM_B1 * m_ref[...] + (1.0 - ADAM_B1) * g
        vn = ADAM_B2 * v_ref[...] + (1.0 - ADAM_B2) * (g * g)
        d_ref[...] = -ADAM_LR * ((mn / c1) / (jnp.sqrt(vn / c2) + ADAM_EPS) + ADAM_WD * w_ref[...])
        mo_ref[...] = mn
        vo_ref[...] = vn

    blk = pl.BlockSpec((tr, C), lambda i: (i, 0))
    n_out = 3 + (ng == 2)
    outs = pl.pallas_call(
        body, name="adamw", grid=(R // tr,), in_specs=[blk] * (3 + ng), out_specs=[blk] * n_out,
        out_shape=[jax.ShapeDtypeStruct((R, C), F32)] * n_out, compiler_params=_params("parallel"),
    )(w, *grads, m, v)
    return (outs[3] if ng == 2 else grads[0],) + tuple(outs[:3])


def _sum_shards(recv):
    n, R, C = recv.shape
    tr = _div(R, 256, SUBLANES if recv.dtype == F32 else 2 * SUBLANES)

    def body(r_ref, o_ref):
        s = r_ref[0].astype(F32)
        for i in range(1, n):
            s = s + r_ref[i].astype(F32)
        o_ref[...] = s

    return pl.pallas_call(
        body, name="sum_shards", grid=(R // tr,), in_specs=[pl.BlockSpec((n, tr, C), lambda i: (0, i, 0))],
        out_specs=pl.BlockSpec((tr, C), lambda i: (i, 0)), out_shape=jax.ShapeDtypeStruct((R, C), F32),
        compiler_params=_params("parallel"),
    )(recv)


N_DEV = 8
N_CHIPS = 4


def _all_gather_small(x, carry=None):
    m_per, n = x.shape

    def body(x_ref, out_ref, send_sems, recv_sems, local_sem):
        cx, cy, cc = _coords()
        me, sibling = (cx, cy, cc), (cx, cy, 1 - cc)
        chips = [(1 - cx, cy), (cx, 1 - cy), (1 - cx, 1 - cy)]

        def rows(px, py, pc):
            return out_ref.at[pl.ds((4 * px + 2 * py + pc) * m_per, m_per), :]

        def copy(k, block, to, src=None):
            return pltpu.make_async_remote_copy(
                src_ref=rows(*block) if src is None else src, dst_ref=rows(*block), send_sem=send_sems.at[k],
                recv_sem=recv_sems.at[k], device_id=to, device_id_type=MESH)

        mine = pltpu.make_async_copy(x_ref, rows(*me), local_sem)
        mine.start()
        first = [copy(0, me, sibling, src=x_ref)]
        first += [copy(1 + j, me, (*chip, cc), src=x_ref) for j, chip in enumerate(chips)]
        for cp in first:
            cp.start()
        passed = [copy(4 + j, (*chip, cc), sibling) for j, chip in enumerate(chips)]
        for j, chip in enumerate(chips):
            copy(1 + j, (*chip, cc), me).wait_recv()
            passed[j].start()
        copy(0, sibling, me).wait_recv()
        for j, chip in enumerate(chips):
            copy(4 + j, (*chip, 1 - cc), me).wait_recv()
        for cp in first + passed:
            cp.wait_send()
        mine.wait()

    (out,), moved = _carried_call(
        body, name="all_gather_small", grid=(1,), out_shape=[jax.ShapeDtypeStruct((N_DEV * m_per, n), x.dtype)],
        in_specs=[pl.BlockSpec(memory_space=pltpu.VMEM)], out_specs=[pl.BlockSpec(memory_space=pltpu.VMEM)],
        scratch_shapes=[pltpu.SemaphoreType.DMA((7,)), pltpu.SemaphoreType.DMA((7,)), pltpu.SemaphoreType.DMA],
        operands=(x,), sem=("arbitrary",), carry=carry)
    return out, moved


def _layer_moves(kind, arrays_from, arrays_to, pieces, layer_major=()):
    used = sorted({w for w, _ in pieces})
    pos = {w: i for i, w in enumerate(used)}
    gather = kind == "gather"

    def half(ref, c):
        rows = ref.shape[0] // 2
        return ref.at[pl.ds(c * rows, rows), :]

    def slot(d, w, chip, l):
        return d.at[l, chip] if w in layer_major else d.at[chip, l]

    def plan(src_refs, dst_refs, me):
        cx, cy, cc = me
        mine = 2 * cx + cy
        remote, local = [], []
        for w, l in pieces:
            s, d = src_refs[pos[w]], dst_refs[pos[w]]
            for px, py in _other_chips(cx, cy):
                if gather:
                    remote.append((half(s.at[l], cc), half(slot(d, w, mine, l), cc), (px, py, cc)))
                else:
                    remote.append((s.at[2 * px + py, l], d.at[mine, l], (px, py, cc)))
            local.append((s.at[l], slot(d, w, mine, l)) if gather else (s.at[mine, l], d.at[mine, l]))
        return remote, local

    def onward(src_refs, dst_refs, me):
        cx, cy, cc = me
        moves = []
        for w, l in pieces:
            d = dst_refs[pos[w]]
            for px, py in _other_chips(cx, cy):
                landed = half(slot(d, w, 2 * px + py, l), cc)
                moves.append((landed, landed, (cx, cy, 1 - cc)))
        return moves

    n = 3 * len(pieces)
    carry = _Carry([arrays_from[w] for w in used], [arrays_to[w] for w in used], plan, n, len(pieces),
                   onward if gather else None, n if gather else 0)
    return carry, used


def _swap_with_sibling(sums):
    def plan(src_refs, dst_refs, me):
        cx, cy, cc = me
        return [(s, d, (cx, cy, 1 - cc)) for s, d in zip(src_refs, dst_refs)], []

    return _Carry(sums, [lax.empty(s.shape, s.dtype) for s in sums], plan, len(sums), 0)


def _pack(arrs, width):
    parts, layout, row = [], [], 0
    for a in arrs:
        flat = a.reshape(-1).astype(F32)
        rows = -(-flat.shape[0] // (width * SUBLANES)) * SUBLANES
        parts.append(jnp.pad(flat, (0, rows * width - flat.shape[0])).reshape(rows, width))
        layout.append((row, rows, a.shape))
        row += rows
    pad = -row % (8 * SUBLANES) if row > 8 * SUBLANES else 0
    if pad:
        parts.append(jnp.zeros((pad, width), F32))
    return jnp.concatenate(parts, axis=0), layout, row + pad


def _unpack(buf, layout, idx):
    row, rows, shape = layout[idx]
    size = math.prod(shape)
    return buf[row:row + rows].reshape(-1)[:size].reshape(shape)


def kernel(x, c, ln_g, ada_w, ada_b, ssm_lam_re, ssm_lam_im, ssm_log_dt, ssm_b_re, ssm_b_im, ssm_c_re, ssm_c_im, ssm_d, ssm_w_glu, kv_g, kv_ada_w, kv_ada_b, w_kv, attn_w_q, attn_w_o, mlp_w1, mlp_w2, final_g, loss_target, m_ln_g, m_ada_w, m_ada_b, m_ssm_lam_re, m_ssm_lam_im, m_ssm_log_dt, m_ssm_b_re, m_ssm_b_im, m_ssm_c_re, m_ssm_c_im, m_ssm_d, m_ssm_w_glu, m_kv_g, m_kv_ada_w, m_kv_ada_b, m_w_kv, m_attn_w_q, m_attn_w_o, m_mlp_w1, m_mlp_w2, m_final_g, v_ln_g, v_ada_w, v_ada_b, v_ssm_lam_re, v_ssm_lam_im, v_ssm_log_dt, v_ssm_b_re, v_ssm_b_im, v_ssm_c_re, v_ssm_c_im, v_ssm_d, v_ssm_w_glu, v_kv_g, v_kv_ada_w, v_kv_ada_b, v_w_kv, v_attn_w_q, v_attn_w_o, v_mlp_w1, v_mlp_w2, v_final_g):
    B, S, D = x.shape
    N = B * S
    depth = ln_g.shape[0]
    n_a = ssm_w_glu.shape[0]
    n_b = attn_w_q.shape[0]
    FF = mlp_w1.shape[2] * N_CHIPS
    cx, cy, cc = _coords()
    chip = 2 * cx + cy
    dev = 4 * cx + 2 * cy + cc
    n_ex = N_DEV * B
    ada_cols = ada_w.shape[-1]
    kv_cols = kv_ada_w.shape[-1]

    GLU, KV, Q, O, W1, W2 = range(6)
    shards = [ssm_w_glu.astype(BF16), w_kv.astype(BF16)[None], attn_w_q.astype(BF16), attn_w_o.astype(BF16),
              mlp_w1.astype(BF16), mlp_w2.astype(BF16)]
    row_sharded = (O, W2)
    wg = [lax.empty((s.shape[0], N_CHIPS) + s.shape[1:] if w in row_sharded else (N_CHIPS,) + s.shape, BF16)
          for w, s in enumerate(shards)]

    def whole_rows(w):
        L, _, R, C = wg[w].shape
        return wg[w].reshape(1, L, N_CHIPS * R, C)

    def landed(arrays, used, moved):
        for w, a in zip(used, moved):
            arrays[w] = a

    fetch_with = {}

    def carried_by(kind, l, *pieces):
        fetch_with.setdefault((kind, l), []).extend(pieces)

    assert n_a >= 1 and n_b >= 1, (n_a, n_b)
    carried_by("mixer", 0, *[(GLU, l) for l in range(n_a)], (W1, 0))
    carried_by("glu_proj", 0, (W2, 0))
    carried_by("mlp_up", 0, (Q, 0))
    carried_by("mlp_up", n_a - 1, (O, 0))
    carried_by("mixer", n_a - 1, (KV, 0))
    for l in range(1, depth):
        carried_by("mixer" if l - 1 >= n_a else "mlp_down", l - 1, (W1, l))
        carried_by("mixer", l, (W2, l))
    for j in range(1, n_b):
        carried_by("mixer", n_a + j - 1, (Q, j), (O, j))

    def fetch(kind, l):
        pieces = fetch_with.get((kind, l))
        if not pieces:
            return None, []
        return _layer_moves("gather", shards, wg, pieces, layer_major=row_sharded)

    def mm_carrying(kind, l, *args, **kw):
        carry, used = fetch(kind, l)
        if carry is None:
            return _mm(kind, *args, **kw)
        out, moved = _mm(kind, *args, carry=carry, **kw)
        landed(wg, used, moved)
        return out

    c_pack, c_layout, _ = _pack([c], D)
    c_all_buf, _ = _all_gather_small(c_pack)
    c_rows = c_pack.shape[0]
    c_all = jnp.concatenate([_unpack(c_all_buf[d * c_rows:(d + 1) * c_rows], c_layout, 0) for d in range(N_DEV)], axis=0)
    sc_all = jax.nn.silu(c_all).astype(BF16)
    n_mod = depth * 2
    ada_w8 = ada_w.reshape(n_mod, 1, D, ada_cols)
    ada_b_row = ada_b.reshape(1, n_mod * ada_cols)
    mod_local = _mm("ada_fwd", sc_all, ada_w8, mode="nn", M=n_ex, N=n_mod * ada_cols, K=D, b_lay="cs", b_ns=n_mod,
                    epi=_add, extras=[("n", ada_b_row)])
    kv_ada_b_local = lax.dynamic_slice(kv_ada_b.reshape(N_CHIPS, kv_cols), (chip, 0), (1, kv_cols))
    kvmod_local = _mm("ada_fwd", sc_all, _as4(kv_ada_w), mode="nn", M=n_ex, N=kv_cols, K=D, epi=_add,
                      extras=[("n", kv_ada_b_local)])
    mod_pack, mod_layout, mod_rows = _pack([mod_local, kvmod_local, ln_g, ssm_d], D)
    mod_buf, _ = _all_gather_small(mod_pack)

    def from_chip(j, idx):
        d = 2 * j
        return _unpack(mod_buf[d * mod_rows:(d + 1) * mod_rows], mod_layout, idx)

    my_rows = lambda a: lax.dynamic_slice_in_dim(a, dev * B, B, axis=0)
    mods = jnp.concatenate([my_rows(from_chip(j, 0)).reshape(B, n_mod, ada_cols) for j in range(N_CHIPS)], axis=2)
    kvmod = jnp.concatenate([my_rows(from_chip(j, 1)) for j in range(N_CHIPS)], axis=1)
    ln_g_full = jnp.concatenate([from_chip(j, 2) for j in range(N_CHIPS)], axis=2)
    ssm_d_full = jnp.concatenate([from_chip(j, 3) for j in range(N_CHIPS)], axis=1)

    def mod3(l, s):
        mrow = mods[:, l * 2 + s]
        return [mrow[:, i * D:(i + 1) * D].reshape(B, 1, D) for i in range(3)]

    kv_shift, kv_scale = kvmod[:, :D].reshape(B, 1, D), kvmod[:, D:].reshape(B, 1, D)

    s5_tabs = []
    for l in range(n_a):
        prm = (ssm_lam_re[l], ssm_lam_im[l], ssm_log_dt[l], ssm_b_re[l], ssm_b_im[l], ssm_c_re[l], ssm_c_im[l])
        (bd, cd, _, _), disc_vjp = jax.vjp(_s5_discretize, *prm)
        pw, seg_f, seg_b = _s5_scan_coefs(ssm_lam_re[l], ssm_lam_im[l], ssm_log_dt[l], S5_SEG)
        s5_tabs.append((bd.astype(BF16), cd.astype(BF16), pw, seg_f, seg_b, disc_vjp))

    h = x.reshape(N, D)
    saved = []
    k_all = v_all = None
    shift, scale, gate = mod3(0, 0)
    u = _normmod(h, ln_g_full[0, 0].reshape(1, D), scale, shift, B)
    for l in range(depth):
        sv = {}
        sv["h0"], sv["scale0"], sv["gate0"], sv["u0"] = h, scale, gate, u
        shift1, scale1, gate1 = mod3(l, 1)
        norm1 = [("n", ln_g_full[l, 1].reshape(1, D)), ("ex", scale1), ("ex", shift1)]
        carry, used = fetch("mixer", l)
        if l < n_a:
            bd, cd, pw, seg_f, _, _ = s5_tabs[l]
            z, carries, moved = _s5_fwd(u, bd, cd, pw, seg_f, ssm_d_full[l].reshape(1, D), B, carry)
            landed(wg, used, moved)
            zz = mm_carrying("glu_proj", l, z, wg[GLU], mode="nn", M=N, N=2 * D, K=D, b_lay="cs", b_l=l, b_ns=N_CHIPS)
            y, h, u = _glu_residual_norm(zz, h, gate, ln_g_full[l, 1].reshape(1, D), scale1, shift1, B)
            sv["z"], sv["carries"], sv["zz"] = z, carries, zz
        else:
            j = l - n_a
            q = _mm("q_proj", u, wg[Q], mode="nn", M=N, N=3 * D, K=D, b_lay="cs", b_l=j, b_ns=N_CHIPS)
            q3 = q.reshape(B, S, 3 * D)
            o, lse, moved = _attn_fwd(q3, k_all, v_all, B, carry)
            landed(wg, used, moved)
            o2 = o.reshape(N, D)
            y, h, u = _mm("o_proj", o2, whole_rows(O), mode="nn", M=N, N=D, K=D, b_l=j, out_dtype=(BF16, F32, BF16),
                          epi=_gated_residual_norm, extras=[("mn", h), ("ex", gate)] + norm1, rows_per_ex=S)
            sv["q"], sv["o"], sv["lse"] = q3, o, lse
        sv["y0"] = y
        sv["h1"], sv["scale1"], sv["gate1"], sv["u1"] = h, scale1, gate1, u
        r = mm_carrying("mlp_up", l, u, wg[W1], mode="nn", M=N, N=FF, K=D, b_lay="cs", b_l=l, b_ns=N_CHIPS,
                        out_dtype=BF16, epi=_relu2)
        if l + 1 < depth:
            shift, scale, gate = mod3(l + 1, 0)
            y, h, u = mm_carrying(
                "mlp_down", l, r, whole_rows(W2), mode="nn", M=N, N=D, K=FF, b_l=l, tk=2048, tm=512,
                out_dtype=(BF16, F32, BF16), epi=_gated_residual_norm, rows_per_ex=S,
                extras=[("mn", h), ("ex", gate1), ("n", ln_g_full[l + 1, 0].reshape(1, D)), ("ex", scale), ("ex", shift)])
        else:
            y, h = mm_carrying("mlp_down", l, r, whole_rows(W2), mode="nn", M=N, N=D, K=FF, b_l=l, tk=2048,
                               out_dtype=(BF16, F32), epi=_gated_residual, extras=[("mn", h), ("ex", gate1)], rows_per_ex=S)
        sv["r"], sv["y1"] = r, y
        saved.append(sv)
        if l == n_a - 1:
            h_kv = h
            u_kv = _normmod(h, kv_g.reshape(1, D), kv_scale, kv_shift, B)
            half = N_CHIPS // 2
            k_all = _mm("kv_proj", u_kv, wg[KV], mode="nn", M=N, N=3 * D, K=D, b_lay="cs", b_s0=0, b_ns=half).reshape(B, S, 3 * D)
            v_all = _mm("kv_proj", u_kv, wg[KV], mode="nn", M=N, N=3 * D, K=D, b_lay="cs", b_s0=half, b_ns=half).reshape(B, S, 3 * D)

    loss_buf, dh, d_final_g = _loss_head(h, final_g.reshape(1, D), loss_target.reshape(N, D))
    loss = lax.psum(loss_buf[0, 0], ("x", "y", "c"))

    dg = [lax.empty((N_CHIPS,) + s.shape, BF16) for s in shards]
    recv = [lax.empty((N_CHIPS,) + s.shape, BF16) for s in shards]

    def send(pieces):
        return _layer_moves("scatter", dg, recv, pieces)

    send_with = {l: [(W1, l), (W2, l)] for l in range(depth)}
    for l in range(n_a):
        send_with[l] += [(GLU, l)]
    for j in range(n_b):
        send_with[n_a + j] += [(O, j)]
        send_with[n_a + j - 1] += [(Q, j)]
    send_with[n_a - 1] += [(KV, 0)]
    d_ln_g = [[None, None] for _ in range(depth)]
    d_mods = [[None, None] for _ in range(depth)]
    d_s5 = [None] * n_a
    dk_acc = dv_acc = None
    half = N_CHIPS // 2

    def tn_grad(name, a, d, into, l, Mr, Nc, lay, s0=0, ns=N_CHIPS):
        return _mm(name, a, _as4(d), mode="tn", M=Mr, N=Nc, K=N, b_lay="cs", out_dtype=BF16, out_lay=lay,
                   out4_shape=into.shape, out_into=into, out_l=l, out_s0=s0, out_ns=ns, tk=2048)

    dy, d_gate1 = _residual_bwd(dh, saved[-1]["gate1"], saved[-1]["y1"], B)
    for l in reversed(range(depth)):
        sv = saved[l]
        dg[W2] = tn_grad("mlp_down_dw", sv["r"], dy, dg[W2], l, FF, D, "rs")
        da = _mm("mlp_down_dx", dy, whole_rows(W2), mode="nt", M=N, N=FF, K=D, b_l=l, out_dtype=BF16,
                 epi=_relu2_bwd, extras=[("mn", sv["r"])])
        dg[W1] = tn_grad("mlp_up_dw", sv["u1"], da, dg[W1], l, D, FF, "cs")
        du = _mm("mlp_up_dx", da, wg[W1], mode="nt", M=N, N=D, K=FF, b_lay="cs", b_l=l, b_ns=N_CHIPS)
        dh, dgv, d_scale1, d_shift1, dy, d_gate0 = _normmod_bwd(du, sv["h1"], ln_g_full[l, 1].reshape(1, D), sv["scale1"],
                                                                dh, B, below=(sv["y0"], sv["gate0"]))
        d_ln_g[l][1] = dgv
        d_mods[l][1] = jnp.concatenate([d_shift1, d_scale1, d_gate1], axis=2)
        if l < n_a:
            bd, cd, pw, seg_f, seg_b, disc_vjp = s5_tabs[l]
            dzz = _glu_bwd(dy, sv["zz"])
            dg[GLU] = tn_grad("glu_proj_dw", sv["z"], dzz, dg[GLU], l, D, 2 * D, "cs")
            dz = _mm("glu_proj_dx", dzz, wg[GLU], mode="nt", M=N, N=D, K=2 * D, b_lay="cs", b_l=l, b_ns=N_CHIPS)
            carry, used = send(send_with[l])
            du, d_bd, d_cd, d_a2, d_dskip, moved = _s5_bwd(sv["u0"], dz, bd, cd, pw, seg_f, seg_b,
                                                           ssm_d_full[l].reshape(1, D), sv["carries"], B, carry)
            landed(recv, used, moved)
            d_are = (d_a2[:, 0, :CHUNK_STATE] + d_a2[:, 0, CHUNK_STATE:]).reshape(-1, SSM_STATE)
            d_aim = (d_a2[:, 1, CHUNK_STATE:] - d_a2[:, 1, :CHUNK_STATE]).reshape(-1, SSM_STATE)
            d_s5[l] = disc_vjp((d_bd, jnp.swapaxes(d_cd, 1, 2), d_are, d_aim)) + (d_dskip,)
        else:
            j = l - n_a
            dg[O] = tn_grad("o_proj_dw", sv["o"].reshape(N, D), dy, dg[O], j, D, D, "rs")
            do = _mm("o_proj_dx", dy, whole_rows(O), mode="nt", M=N, N=D, K=D, b_l=j)
            carry, used = send(send_with[l])
            dq, dk_acc, dv_acc, moved = _attn_bwd(sv["q"], k_all, v_all, sv["o"], sv["lse"], do.reshape(B, S, D),
                                                  dk_acc, dv_acc, B, l == n_a, carry)
            landed(recv, used, moved)
            dq2 = dq.reshape(N, 3 * D)
            dg[Q] = tn_grad("q_proj_dw", sv["u0"], dq2, dg[Q], j, D, 3 * D, "cs")
            du = _mm("q_proj_dx", dq2, wg[Q], mode="nt", M=N, N=D, K=3 * D, b_lay="cs", b_l=j, b_ns=N_CHIPS)
        below = (saved[l - 1]["y1"], saved[l - 1]["gate1"]) if l > 0 else None
        dh, dgv, d_scale0, d_shift0, dy, d_gate1 = _normmod_bwd(du, sv["h0"], ln_g_full[l, 0].reshape(1, D), sv["scale0"],
                                                                dh, B, below=None if l == n_a else below)
        d_ln_g[l][0] = dgv
        d_mods[l][0] = jnp.concatenate([d_shift0, d_scale0, d_gate0], axis=2)
        if l == n_a:
            dk2, dv2 = dk_acc.reshape(N, 3 * D), dv_acc.reshape(N, 3 * D)
            dg[KV] = tn_grad("kv_proj_dw", u_kv, dk2, dg[KV], 0, D, 3 * D, "cs", s0=0, ns=half)
            dg[KV] = tn_grad("kv_proj_dw", u_kv, dv2, dg[KV], 0, D, 3 * D, "cs", s0=half, ns=half)
            du_kv = _mm("kv_proj_dx", dk2, wg[KV], mode="nt", M=N, N=D, K=3 * D, b_lay="cs", b_s0=0, b_ns=half)
            du_kv = _mm("kv_proj_dx", dv2, wg[KV], mode="nt", M=N, N=D, K=3 * D, b_lay="cs", b_s0=half, b_ns=half,
                        epi=_add, extras=[("mn", du_kv)])
            dh, d_kv_g, d_kv_scale, d_kv_shift, dy, d_gate1 = _normmod_bwd(du_kv, h_kv, kv_g.reshape(1, D), kv_scale, dh, B,
                                                                           below=below)
    grad_x = dh.reshape(B, S, D)

    own = [_sum_shards(r.reshape(N_CHIPS, -1, r.shape[-1])) for r in recv]

    d_kvmod = jnp.concatenate([d_kv_shift, d_kv_scale], axis=2).reshape(B, 2 * D)
    d_mod_all = jnp.concatenate([d_mods[l][s].reshape(B, 3 * D) for l in range(depth) for s in range(2)], axis=1)
    small = [
        d_mod_all, d_kvmod,
        jnp.stack([jnp.stack([d_ln_g[l][0].reshape(D), d_ln_g[l][1].reshape(D)]) for l in range(depth)]),
        jnp.stack([d_s5[l][0] for l in range(n_a)]), jnp.stack([d_s5[l][1] for l in range(n_a)]),
        jnp.stack([d_s5[l][2] for l in range(n_a)]),
        jnp.stack([d_s5[l][3] for l in range(n_a)]), jnp.stack([d_s5[l][4] for l in range(n_a)]),
        jnp.stack([d_s5[l][5] for l in range(n_a)]), jnp.stack([d_s5[l][6] for l in range(n_a)]),
        jnp.stack([d_s5[l][7].reshape(D) for l in range(n_a)]),
        d_kv_g.reshape(D), d_final_g.reshape(D),
    ]
    small_pack, small_layout, small_rows = _pack(small, D)
    small_buf, other = _all_gather_small(small_pack, _swap_with_sibling(own))
    small_sum = _sum_shards(small_buf.reshape(N_DEV, small_rows, D))
    red = lambda idx: _unpack(small_sum, small_layout, idx)
    per_dev = lambda idx: jnp.concatenate(
        [_unpack(small_buf[d * small_rows:(d + 1) * small_rows], small_layout, idx) for d in range(N_DEV)], axis=0)

    dm_all = per_dev(0).reshape(n_ex, n_mod, 3 * D)
    dm_cols = lax.dynamic_slice_in_dim(dm_all, chip * ada_cols, ada_cols, axis=2).reshape(n_ex, n_mod * ada_cols)
    g_ada_w = _mm("ada_dw", sc_all, _as4(dm_cols), mode="tn", M=D, N=n_mod * ada_cols, K=n_ex, b_lay="cs",
                  out_lay="cs", out4_shape=(n_mod, 1, D, ada_cols), out_ns=n_mod).reshape(ada_w.shape)
    dkvm_all = per_dev(1)
    dkvm_cols = lax.dynamic_slice_in_dim(dkvm_all, chip * kv_cols, kv_cols, axis=1)
    g_kv_ada_w = _mm("ada_dw", sc_all, _as4(dkvm_cols), mode="tn", M=D, N=kv_cols, K=n_ex, b_lay="cs")
    g_ada_b_full = (red(0)[0] + red(0)[1]).reshape(depth, 2, 3 * D) if B == 2 else jnp.sum(red(0), axis=0).reshape(depth, 2, 3 * D)
    g_ada_b = lax.dynamic_slice_in_dim(g_ada_b_full, chip * ada_cols, ada_cols, axis=2)
    g_kv_ada_b = red(1)[0] + red(1)[1] if B == 2 else jnp.sum(red(1), axis=0)
    g_ln_g = lax.dynamic_slice_in_dim(red(2), chip * (D // N_CHIPS), D // N_CHIPS, axis=2)
    g_ssm_d = lax.dynamic_slice_in_dim(red(10), chip * (D // N_CHIPS), D // N_CHIPS, axis=1)
    small_grads = {
        "ln_g": g_ln_g, "ada_b": g_ada_b, "ssm_lam_re": red(3), "ssm_lam_im": red(4), "ssm_log_dt": red(5),
        "ssm_b_re": red(6), "ssm_b_im": red(7), "ssm_c_re": red(8), "ssm_c_im": red(9), "ssm_d": g_ssm_d,
        "kv_g": red(11), "kv_ada_b": g_kv_ada_b, "final_g": red(12),
    }
    small_w = {"ln_g": (ln_g, m_ln_g, v_ln_g), "ada_b": (ada_b, m_ada_b, v_ada_b),
               "ssm_lam_re": (ssm_lam_re, m_ssm_lam_re, v_ssm_lam_re), "ssm_lam_im": (ssm_lam_im, m_ssm_lam_im, v_ssm_lam_im),
               "ssm_log_dt": (ssm_log_dt, m_ssm_log_dt, v_ssm_log_dt), "ssm_b_re": (ssm_b_re, m_ssm_b_re, v_ssm_b_re),
               "ssm_b_im": (ssm_b_im, m_ssm_b_im, v_ssm_b_im), "ssm_c_re": (ssm_c_re, m_ssm_c_re, v_ssm_c_re),
               "ssm_c_im": (ssm_c_im, m_ssm_c_im, v_ssm_c_im), "ssm_d": (ssm_d, m_ssm_d, v_ssm_d),
               "kv_g": (kv_g, m_kv_g, v_kv_g), "kv_ada_b": (kv_ada_b, m_kv_ada_b, v_kv_ada_b),
               "final_g": (final_g, m_final_g, v_final_g)}
    names = list(small_w)
    wp, lay_w, _ = _pack([small_w[n][0] for n in names], D)
    gp, _, _ = _pack([small_grads[n] for n in names], D)
    mp, _, _ = _pack([small_w[n][1] for n in names], D)
    vp, _, _ = _pack([small_w[n][2] for n in names], D)
    _, d_p, m_p, v_p = _adamw(wp, [gp], mp, vp)
    upd = {n: (small_grads[n].reshape(small_w[n][0].shape), _unpack(d_p, lay_w, i), _unpack(m_p, lay_w, i), _unpack(v_p, lay_w, i))
           for i, n in enumerate(names)}

    def big(w, m, v, g_own, g_other=None):
        C = w.shape[-1]
        gs = [g_own.reshape(-1, C)] + ([g_other.reshape(-1, C)] if g_other is not None else [])
        return tuple(t.reshape(w.shape) for t in _adamw(w.reshape(-1, C), gs, m.reshape(-1, C), v.reshape(-1, C)))

    upd["ssm_w_glu"] = big(ssm_w_glu, m_ssm_w_glu, v_ssm_w_glu, own[0], other[0])
    upd["w_kv"] = big(w_kv, m_w_kv, v_w_kv, own[1], other[1])
    upd["attn_w_q"] = big(attn_w_q, m_attn_w_q, v_attn_w_q, own[2], other[2])
    upd["attn_w_o"] = big(attn_w_o, m_attn_w_o, v_attn_w_o, own[3], other[3])
    upd["mlp_w1"] = big(mlp_w1, m_mlp_w1, v_mlp_w1, own[4], other[4])
    upd["mlp_w2"] = big(mlp_w2, m_mlp_w2, v_mlp_w2, own[5], other[5])
    upd["ada_w"] = big(ada_w, m_ada_w, v_ada_w, g_ada_w)
    upd["kv_ada_w"] = big(kv_ada_w, m_kv_ada_w, v_kv_ada_w, g_kv_ada_w)

    order = ["ln_g", "ada_w", "ada_b", "ssm_lam_re", "ssm_lam_im", "ssm_log_dt", "ssm_b_re", "ssm_b_im", "ssm_c_re",
             "ssm_c_im", "ssm_d", "ssm_w_glu", "kv_g", "kv_ada_w", "kv_ada_b", "w_kv", "attn_w_q", "attn_w_o", "mlp_w1",
             "mlp_w2", "final_g"]
    return (loss, grad_x, *[upd[n][0] for n in order], *[upd[n][1] for n in order], *[upd[n][2] for n in order],
            *[upd[n][3] for n in order])
```

```python
import functools
import math

import jax
import jax.numpy as jnp
from jax import lax
from jax.experimental import pallas as pl
from jax.experimental.pallas import tpu as pltpu

F32 = jnp.float32
BF16 = jnp.bfloat16
MESH = pl.DeviceIdType.MESH

EPS = 1e-6
NEG = -1e30
SSM_GROUP = 16
SSM_STATE = 64
HEAD_DIM = 64
ATTN_BLOCK = 128
DILATIONS = (1, 4, 16)
ADAM_LR, ADAM_B1, ADAM_B2, ADAM_EPS, ADAM_WD, ADAM_STEP = 0.001, 0.9, 0.999, 1e-08, 0.01, 10

LANES = 128
SUBLANES = 8
CHUNK_GROUPS = LANES // SSM_GROUP
CHUNK_STATE = CHUNK_GROUPS * SSM_STATE
VMEM_LIMIT = 56 * 1024 * 1024


def _div(dim, pref, mult):
    t = min(pref, dim) // mult * mult
    while t >= mult:
        if dim % t == 0:
            return t
        t -= mult
    return dim


def _params(*sem):
    return pltpu.CompilerParams(dimension_semantics=sem, vmem_limit_bytes=VMEM_LIMIT)


def _coords():
    return lax.axis_index("x"), lax.axis_index("y"), lax.axis_index("c")


def _other_chips(cx, cy):
    return [(1 - cx, cy), (cx, 1 - cy), (1 - cx, 1 - cy)]


class _Carry:
    def __init__(self, srcs, dsts, plan, n_remote, n_local, onward=None, n_onward=0):
        self.srcs, self.dsts, self.plan, self.n_remote, self.n_local = list(srcs), list(dsts), plan, n_remote, n_local
        self.onward, self.n_onward = onward, n_onward


def _carried_call(body, *, name, grid, in_specs, out_specs, out_shape, scratch_shapes, operands, sem, carry=None):
    if carry is None:
        outs = pl.pallas_call(body, name=name, grid=grid, in_specs=in_specs, out_specs=out_specs, out_shape=out_shape,
                              scratch_shapes=scratch_shapes, compiler_params=_params(*sem))(*operands)
        return list(outs), []
    n_in, n_out, n_scr = len(in_specs), len(out_specs), len(scratch_shapes)
    ns, nd = len(carry.srcs), len(carry.dsts)

    def wrapped(*refs):
        base_in, src_refs = refs[:n_in], refs[n_in:n_in + ns]
        o0 = n_in + ns + nd
        base_out, dst_refs = refs[o0:o0 + n_out], refs[o0 + n_out:o0 + n_out + nd]
        s0 = o0 + n_out + nd
        base_scr = refs[s0:s0 + n_scr]
        send_sems, recv_sems, local_sems = refs[s0 + n_scr:]
        pids = [pl.program_id(a) for a in range(len(grid))]
        first = functools.reduce(jnp.logical_and, [p == 0 for p in pids])
        last = functools.reduce(jnp.logical_and, [p == g - 1 for p, g in zip(pids, grid)])

        def remote_copies(moves, k0):
            return [pltpu.make_async_remote_copy(src_ref=s, dst_ref=d, send_sem=send_sems.at[k0 + i], recv_sem=recv_sems.at[k0 + i],
                                                 device_id=peer, device_id_type=MESH) for i, (s, d, peer) in enumerate(moves)]

        def copies():
            remote, local = carry.plan(src_refs, dst_refs, _coords())
            return remote_copies(remote, 0), [pltpu.make_async_copy(s, d, local_sems.at[i]) for i, (s, d) in enumerate(local)]

        @pl.when(first)
        def _():
            remote, local = copies()
            for cp in local + remote:
                cp.start()

        body(*base_in, *base_out, *base_scr)

        @pl.when(last)
        def _():
            remote, local = copies()
            for cp in remote:
                cp.wait_send()
                cp.wait_recv()
            for cp in local:
                cp.wait()
            if carry.onward is not None:
                second = remote_copies(carry.onward(src_refs, dst_refs, _coords()), carry.n_remote)
                for cp in second:
                    cp.start()
                for cp in second:
                    cp.wait_send()
                    cp.wait_recv()

    anyspec = pl.BlockSpec(memory_space=pl.ANY)
    outs = pl.pallas_call(
        wrapped, name=name, grid=grid, in_specs=list(in_specs) + [anyspec] * (ns + nd),
        out_specs=list(out_specs) + [anyspec] * nd,
        out_shape=list(out_shape) + [jax.ShapeDtypeStruct(d.shape, d.dtype) for d in carry.dsts],
        scratch_shapes=list(scratch_shapes) + [pltpu.SemaphoreType.DMA((carry.n_remote + carry.n_onward,)),
                                               pltpu.SemaphoreType.DMA((carry.n_remote + carry.n_onward,)),
                                               pltpu.SemaphoreType.DMA((max(carry.n_local, 1),))],
        input_output_aliases={n_in + ns + i: n_out + i for i in range(nd)},
        compiler_params=_params(*(["arbitrary"] * len(grid))),
    )(*operands, *carry.srcs, *carry.dsts)
    return list(outs[:n_out]), list(outs[n_out:])


def _mm(name, a, b4, *, mode, M, N, K, b_lay="cs", b_l=0, b_s0=0, b_ns=1, out_dtype=F32, out_lay=None, out4_shape=None,
        out_into=None, out_l=0, out_s0=0, out_ns=1, epi=None, extras=(), rows_per_ex=None, tm=2048, tn=1024, tk=1024,
        carry=None):
    _, _, bR, bC = b4.shape
    tm = _div(M, tm, SUBLANES if M % 16 else 16)
    brows, bcols = (N, K) if mode == "nt" else (K, N)
    if b_lay == "cs":
        assert bR == brows and bC * b_ns == bcols, (name, b4.shape, brows, bcols)
    else:
        assert bC == bcols and bR * b_ns == brows, (name, b4.shape, brows, bcols)
    n_lim = N
    k_lim = K
    if mode == "nt":
        if b_lay == "cs":
            k_lim = bC
        else:
            n_lim = bR
    else:
        if b_lay == "cs":
            n_lim = bC
        else:
            k_lim = bR
    if out_lay == "cs":
        oR, oC = out4_shape[2], out4_shape[3]
        assert oR == M and oC * out_ns == N, (name, out4_shape, M, N)
        n_lim = math.gcd(n_lim, oC)
    elif out_lay == "rs":
        oR, oC = out4_shape[2], out4_shape[3]
        assert oC == N and oR * out_ns == M, (name, out4_shape, M, N)
        tm = _div(oR, tm, SUBLANES)
    tn = _div(n_lim, tn, LANES)
    tk = _div(k_lim, tk, LANES if mode != "tn" else SUBLANES)
    if mode == "tn":
        tk = _div(k_lim, tk, 16) if k_lim % 16 == 0 else tk
    nk = K // tk
    grid = (M // tm, N // tn, nk)

    if mode == "tn":
        a_spec = pl.BlockSpec((tk, tm), lambda i, j, k: (k, i))
    else:
        a_spec = pl.BlockSpec((tm, tk), lambda i, j, k: (i, k))

    def b_index(ri, ci, br, bc):
        if b_lay == "cs":
            per = bC // bc
            return (b_s0 + ci // per, b_l, ri, ci % per)
        per = bR // br
        return (b_s0 + ri // per, b_l, ri % per, ci)

    if mode == "nt":
        b_spec = pl.BlockSpec((None, None, tn, tk), lambda i, j, k: b_index(j, k, tn, tk))
    else:
        b_spec = pl.BlockSpec((None, None, tk, tn), lambda i, j, k: b_index(k, j, tk, tn))

    in_specs = [a_spec, b_spec]
    operands = [a, b4]
    for kind, arr in extras:
        if kind == "mn":
            in_specs.append(pl.BlockSpec((tm, tn), lambda i, j, k: (i, j)))
        elif kind == "ex":
            per_ex = rows_per_ex // tm
            in_specs.append(pl.BlockSpec((None, 1, tn), lambda i, j, k: (i // per_ex, 0, j)))
        else:
            in_specs.append(pl.BlockSpec((1, tn), lambda i, j, k: (0, j)))
        operands.append(arr)
    n_extra = len(extras)

    multi = isinstance(out_dtype, tuple)
    n_out = len(out_dtype) if multi else 1
    if out_lay is None:
        out_shape = [jax.ShapeDtypeStruct((M, N), dt) for dt in (out_dtype if multi else (out_dtype,))]
        out_spec = [pl.BlockSpec((tm, tn), lambda i, j, k: (i, j)) for _ in range(n_out)]
    else:
        out_shape = [jax.ShapeDtypeStruct(tuple(out4_shape), out_dtype)]
        if out_lay == "cs":
            per_o = oC // tn
            out_spec = [pl.BlockSpec((None, None, tm, tn), lambda i, j, k: (out_s0 + j // per_o, out_l, i, j % per_o))]
        else:
            per_o = oR // tm
            out_spec = [pl.BlockSpec((None, None, tm, tn), lambda i, j, k: (out_s0 + i // per_o, out_l, i % per_o, j))]
    aliases = {}
    if out_into is not None:
        in_specs.append(pl.BlockSpec(memory_space=pl.ANY))
        operands.append(out_into)
        aliases = {len(operands) - 1: 0}

    dims = {"nn": (((1,), (0,)), ((), ())), "nt": (((1,), (1,)), ((), ())), "tn": (((0,), (0,)), ((), ()))}[mode]

    def body(a_ref, b_ref, *rest):
        extra_refs = rest[:n_extra]
        o_refs = rest[len(rest) - n_out - (nk > 1):len(rest) - (nk > 1)]

        def finish(r):
            if epi is not None:
                r = epi(r, *[e[...] for e in extra_refs])
            for o_ref, val in zip(o_refs, r if multi else (r,)):
                o_ref[...] = val.astype(o_ref.dtype)

        part = lax.dot_general(a_ref[...].astype(BF16), b_ref[...].astype(BF16), dims, preferred_element_type=F32)
        if nk == 1:
            finish(part)
            return
        acc = rest[-1]
        k = pl.program_id(2)

        @pl.when(k == 0)
        def _():
            acc[...] = part

        @pl.when(k != 0)
        def _():
            acc[...] += part

        @pl.when(k == nk - 1)
        def _():
            finish(acc[...])

    scratch = [pltpu.VMEM((tm, tn), F32)] if nk > 1 else []
    if carry is not None:
        assert out_into is None, name
        outs, moved = _carried_call(body, name=name, grid=grid, in_specs=in_specs, out_specs=out_spec, out_shape=out_shape,
                                    scratch_shapes=scratch, operands=operands, sem=("arbitrary",) * 3, carry=carry)
        return (tuple(outs) if multi else outs[0]), moved
    outs = pl.pallas_call(
        body, name=name, grid=grid, in_specs=in_specs, out_specs=out_spec, out_shape=out_shape,
        scratch_shapes=scratch, input_output_aliases=aliases,
        compiler_params=_params("parallel", "parallel", "arbitrary"),
    )(*operands)
    return tuple(outs) if multi else outs[0]


def _as4(w):
    return w.reshape((1, 1) + w.shape)


def _relu2(acc):
    r = jnp.maximum(acc, 0.0)
    return r * r


def _relu2_bwd(acc, r):
    return acc * (2.0 * jnp.sqrt(r.astype(F32)))


def _add(acc, e):
    return acc + e


def _gated_residual(acc, h, gate):
    return acc, h + gate * acc


def _modulated_norm(x, g, scale, shift):
    rstd = lax.rsqrt(jnp.mean(x * x, axis=-1, keepdims=True) + EPS)
    return ((x * rstd) * g) * (1.0 + scale) + shift


def _gated_residual_norm(acc, h, gate, g, scale, shift):
    h_new = h + gate * acc
    return acc, h_new, _modulated_norm(h_new, g, scale, shift)


def _row_tiles(N, B, pref=256):
    S = N // B
    tm = _div(S, pref, SUBLANES)
    return tm, S // tm


def _normmod(h, g, scale, shift, B):
    N, D = h.shape
    tm, per_ex = _row_tiles(N, B)

    def body(h_ref, g_ref, sc_ref, sh_ref, u_ref):
        u_ref[...] = _modulated_norm(h_ref[...], g_ref[...], sc_ref[...], sh_ref[...]).astype(u_ref.dtype)

    tok = pl.BlockSpec((tm, D), lambda i: (i, 0))
    vec = pl.BlockSpec((1, D), lambda i: (0, 0))
    ex = pl.BlockSpec((None, 1, D), lambda i: (i // per_ex, 0, 0))
    return pl.pallas_call(
        body, name="normmod_fwd", grid=(N // tm,), in_specs=[tok, vec, ex, ex], out_specs=tok,
        out_shape=jax.ShapeDtypeStruct((N, D), BF16), compiler_params=_params("parallel"),
    )(h, g, scale, shift)


def _normmod_bwd(du, h, g, scale, dh_in, B, below=None):
    N, D = h.shape
    tm, per_ex = _row_tiles(N, B)
    fused = below is not None

    def body(*refs):
        du_ref, h_ref, g_ref, sc_ref, dhin_ref = refs[:5]
        dh_ref, dg_ref, dsc_ref, dsh_ref = refs[5 + 2 * fused:9 + 2 * fused]
        i = pl.program_id(0)
        x = h_ref[...]
        gv = g_ref[...]
        d_u = du_ref[...].astype(F32)
        rstd = lax.rsqrt(jnp.mean(x * x, axis=-1, keepdims=True) + EPS)
        xn = x * rstd
        dyg = d_u * (1.0 + sc_ref[...])
        dxn = dyg * gv
        dh = dhin_ref[...] + rstd * (dxn - xn * jnp.mean(dxn * xn, axis=-1, keepdims=True))
        dh_ref[...] = dh
        sums = [(dsc_ref, jnp.sum(d_u * (xn * gv), axis=0, keepdims=True)), (dsh_ref, jnp.sum(d_u, axis=0, keepdims=True))]
        if fused:
            y_ref, gt_ref = refs[5:7]
            dy_ref, dgt_ref = refs[9 + 2 * fused:]
            dy_ref[...] = (gt_ref[...] * dh).astype(dy_ref.dtype)
            sums.append((dgt_ref, jnp.sum(dh * y_ref[...], axis=0, keepdims=True)))
        dg_t = jnp.sum(dyg * xn, axis=0, keepdims=True)

        @pl.when(i % per_ex == 0)
        def _():
            for ref, val in sums:
                ref[...] = val

        @pl.when(i % per_ex != 0)
        def _():
            for ref, val in sums:
                ref[...] += val

        @pl.when(i == 0)
        def _():
            dg_ref[...] = dg_t

        @pl.when(i != 0)
        def _():
            dg_ref[...] += dg_t

    tok = pl.BlockSpec((tm, D), lambda i: (i, 0))
    vec = pl.BlockSpec((1, D), lambda i: (0, 0))
    ex = pl.BlockSpec((None, 1, D), lambda i: (i // per_ex, 0, 0))
    per_ex_shape = jax.ShapeDtypeStruct((B, 1, D), F32)
    outs = pl.pallas_call(
        body, name="normmod_bwd", grid=(N // tm,), in_specs=[tok, tok, vec, ex, tok] + ([tok, ex] if fused else []),
        out_specs=[tok, vec, ex, ex] + ([tok, ex] if fused else []),
        out_shape=[jax.ShapeDtypeStruct((N, D), F32), jax.ShapeDtypeStruct((1, D), F32), per_ex_shape, per_ex_shape]
        + ([jax.ShapeDtypeStruct((N, D), BF16), per_ex_shape] if fused else []),
        compiler_params=_params("arbitrary"),
    )(du, h, g, scale, dh_in, *(below if fused else ()))
    return tuple(outs) if fused else tuple(outs) + (None, None)


def _residual_bwd(dh, gate, y, B):
    N, D = dh.shape
    tm, per_ex = _row_tiles(N, B)

    def body(dh_ref, gt_ref, y_ref, dy_ref, dgt_ref):
        i = pl.program_id(0)
        d = dh_ref[...]
        dy_ref[...] = (gt_ref[...] * d).astype(dy_ref.dtype)
        t = jnp.sum(d * y_ref[...], axis=0, keepdims=True)

        @pl.when(i % per_ex == 0)
        def _():
            dgt_ref[...] = t

        @pl.when(i % per_ex != 0)
        def _():
            dgt_ref[...] += t

    tok = pl.BlockSpec((tm, D), lambda i: (i, 0))
    ex = pl.BlockSpec((None, 1, D), lambda i: (i // per_ex, 0, 0))
    return pl.pallas_call(
        body, name="residual_bwd", grid=(N // tm,), in_specs=[tok, ex, tok], out_specs=[tok, ex],
        out_shape=[jax.ShapeDtypeStruct((N, D), BF16), jax.ShapeDtypeStruct((B, 1, D), F32)],
        compiler_params=_params("arbitrary"),
    )(dh, gate, y)


def _glu_residual_norm(zz, h, gate, g, scale, shift, B):
    N, D2 = zz.shape
    D = D2 // 2
    tm, per_ex = _row_tiles(N, B)

    def body(v_ref, g_ref, h_ref, gt_ref, ng_ref, sc_ref, sh_ref, y_ref, o_ref, u_ref):
        y = v_ref[...] * jax.nn.sigmoid(g_ref[...])
        y_ref[...] = y.astype(y_ref.dtype)
        h_new = h_ref[...] + gt_ref[...] * y
        o_ref[...] = h_new
        u_ref[...] = _modulated_norm(h_new, ng_ref[...], sc_ref[...], sh_ref[...]).astype(u_ref.dtype)

    tok = pl.BlockSpec((tm, D), lambda i: (i, 0))
    vec = pl.BlockSpec((1, D), lambda i: (0, 0))
    ex = pl.BlockSpec((None, 1, D), lambda i: (i // per_ex, 0, 0))
    return pl.pallas_call(
        body, name="glu_fwd", grid=(N // tm,),
        in_specs=[tok, pl.BlockSpec((tm, D), lambda i: (i, 1)), tok, ex, vec, ex, ex], out_specs=[tok, tok, tok],
        out_shape=[jax.ShapeDtypeStruct((N, D), BF16), jax.ShapeDtypeStruct((N, D), F32), jax.ShapeDtypeStruct((N, D), BF16)],
        compiler_params=_params("parallel"),
    )(zz, zz, h, gate, g, scale, shift)


def _glu_bwd(dy, zz):
    N, D2 = zz.shape
    D = D2 // 2
    tm = _div(N, 256, SUBLANES)

    def body(dy_ref, v_ref, g_ref, o_ref):
        d = dy_ref[...].astype(F32)
        s = jax.nn.sigmoid(g_ref[...])
        o_ref[...] = jnp.concatenate([d * s, d * v_ref[...] * s * (1.0 - s)], axis=1).astype(o_ref.dtype)

    return pl.pallas_call(
        body, name="glu_bwd", grid=(N // tm,),
        in_specs=[pl.BlockSpec((tm, D), lambda i: (i, 0)), pl.BlockSpec((tm, D), lambda i: (i, 0)),
                  pl.BlockSpec((tm, D), lambda i: (i, 1))],
        out_specs=pl.BlockSpec((tm, D2), lambda i: (i, 0)), out_shape=jax.ShapeDtypeStruct((N, D2), BF16),
        compiler_params=_params("parallel"),
    )(dy, zz, zz)


def _loss_head(h, g, target):
    N, D = h.shape
    tm = _div(N, 256, SUBLANES)

    def body(h_ref, g_ref, t_ref, loss_ref, dh_ref, dg_ref):
        i = pl.program_id(0)
        x = h_ref[...]
        gv = g_ref[...]
        rstd = lax.rsqrt(jnp.mean(x * x, axis=-1, keepdims=True) + EPS)
        xn = x * rstd
        err = xn * gv - t_ref[...]
        part = 0.5 * jnp.sum(jnp.sum(err * err, axis=-1, keepdims=True) / D, axis=0, keepdims=True)
        dy = err / D
        dxn = dy * gv
        dh_ref[...] = rstd * (dxn - xn * jnp.mean(dxn * xn, axis=-1, keepdims=True))
        dg_t = jnp.sum(dy * xn, axis=0, keepdims=True)
        part = jnp.broadcast_to(part, loss_ref.shape)

        @pl.when(i == 0)
        def _():
            loss_ref[...] = part
            dg_ref[...] = dg_t

        @pl.when(i != 0)
        def _():
            loss_ref[...] += part
            dg_ref[...] += dg_t

    tok = pl.BlockSpec((tm, D), lambda i: (i, 0))
    vec = pl.BlockSpec((1, D), lambda i: (0, 0))
    return pl.pallas_call(
        body, name="loss_head", grid=(N // tm,), in_specs=[tok, vec, tok],
        out_specs=[pl.BlockSpec((SUBLANES, LANES), lambda i: (0, 0)), tok, vec],
        out_shape=[jax.ShapeDtypeStruct((SUBLANES, LANES), F32), jax.ShapeDtypeStruct((N, D), F32),
                   jax.ShapeDtypeStruct((1, D), F32)],
        compiler_params=_params("arbitrary"),
    )(h, g, target)


def _swap_halves(x):
    half = x.shape[-1] // 2
    return jnp.concatenate([x[:, half:], x[:, :half]], axis=1)


def _gelu(y):
    return jax.nn.gelu(y)


def _gelu_grad(y):
    c0 = math.sqrt(2.0 / math.pi)
    inner = c0 * (y + 0.044715 * y * y * y)
    t = jnp.tanh(inner)
    return 0.5 * (1.0 + t) + 0.5 * y * (1.0 - t * t) * c0 * (1.0 + 3.0 * 0.044715 * y * y)


def _s5_discretize(lam_re, lam_im, log_dt, b_re, b_im, c_re, c_im):
    G = lam_re.shape[0]
    nch = G // CHUNK_GROUPS
    dt = jnp.exp(log_dt)[:, None]
    er = jnp.exp(lam_re * dt)
    a_re = er * jnp.cos(lam_im * dt)
    a_im = er * jnp.sin(lam_im * dt)
    den = lam_re * lam_re + lam_im * lam_im
    n_re, n_im = a_re - 1.0, a_im
    f_re = (n_re * lam_re + n_im * lam_im) / den
    f_im = (n_im * lam_re - n_re * lam_im) / den
    bb_re = f_re[..., None] * b_re - f_im[..., None] * b_im
    bb_im = f_re[..., None] * b_im + f_im[..., None] * b_re
    eye = jnp.eye(CHUNK_GROUPS, dtype=F32)

    def pack_b(bb):
        bb = bb.reshape(nch, CHUNK_GROUPS, SSM_STATE, SSM_GROUP)
        return jnp.einsum("jgpc,gh->jgchp", bb, eye).reshape(nch, LANES, CHUNK_STATE)

    def pack_c(cc):
        cc = cc.reshape(nch, CHUNK_GROUPS, SSM_GROUP, SSM_STATE)
        return jnp.einsum("jgcp,gh->jgphc", cc, eye).reshape(nch, CHUNK_STATE, LANES)

    bd = jnp.concatenate([pack_b(bb_re), pack_b(bb_im)], axis=2)
    cd = jnp.concatenate([pack_c(c_re), pack_c(-c_im)], axis=1)
    return bd, cd, a_re, a_im


S5_TILE = 1024
S5_SEG = S5_TILE // SUBLANES
S5_UNROLL = 8


def _s5_scan_coefs(lam_re, lam_im, log_dt, seg):
    G = lam_re.shape[0]
    nch = G // CHUNK_GROUPS
    dt = jnp.exp(log_dt)[:, None]
    rate = (lam_re * dt).reshape(nch, 1, CHUNK_STATE)
    freq = (lam_im * dt).reshape(nch, 1, CHUNK_STATE)

    def powers(ks):
        k = jnp.asarray(ks, F32)[None, :, None]
        er = jnp.exp(k * rate)
        re, im = er * jnp.cos(k * freq), er * jnp.sin(k * freq)
        return jnp.concatenate([re, re], axis=2), jnp.concatenate([-im, im], axis=2)

    pw = jnp.stack(powers(range(1, seg + 1)), axis=1)
    steps = (1, 2, 4)
    re, im = powers([s * seg for s in steps])
    row = jnp.arange(SUBLANES, dtype=jnp.int32)[None, None, :, None]
    shift = jnp.asarray(steps, jnp.int32)[None, :, None, None]

    def table(reverse):
        mask = (row < SUBLANES - shift) if reverse else (row >= shift)
        pair = jnp.stack([jnp.where(mask, re[:, :, None, :], 0.0),
                          jnp.where(mask, (-im if reverse else im)[:, :, None, :], 0.0)], axis=2)
        return pair.reshape(nch, 2 * len(steps), SUBLANES, 2 * CHUNK_STATE)

    return pw, table(False), table(True)


def _to_segments(dst_s, src_ref, seg):
    for j in range(SUBLANES):
        dst_s[pl.ds(j, seg, stride=SUBLANES), :] = src_ref[pl.ds(j * seg, seg), :].astype(F32)


def _from_segments(dst_ref, src_s, seg):
    for j in range(SUBLANES):
        dst_ref[pl.ds(j * seg, seg), :] = src_s[pl.ds(j, seg, stride=SUBLANES), :].astype(dst_ref.dtype)


def _seg_scan(x_ref, pw_ref, seg_ref, carry_ref, c_ref, seg, reverse):
    W = x_ref.shape[-1]
    tm = x_ref.shape[0]
    sgn = -1.0 if reverse else 1.0
    ar = jnp.broadcast_to(pw_ref[0, 0:1, :], (SUBLANES, W))
    ai = sgn * jnp.broadcast_to(pw_ref[1, 0:1, :], (SUBLANES, W))

    def rows(i):
        return pl.ds(pl.multiple_of(i * SUBLANES, SUBLANES), SUBLANES)

    def step(t, prev):
        i = (seg - 2 - t) if reverse else (t + 1)
        x = x_ref[rows(i), :] + ar * prev + ai * _swap_halves(prev)
        x_ref[rows(i), :] = x
        return x

    start = (seg - 1) * SUBLANES if reverse else 0
    edge = lax.fori_loop(0, seg - 1, step, x_ref[start:start + SUBLANES, :], unroll=S5_UNROLL)
    row = lax.broadcasted_iota(jnp.int32, (SUBLANES, W), 0)
    if reverse:
        f = jnp.where(row == SUBLANES - 1, carry_ref[...], pltpu.roll(edge, SUBLANES - 1, 0))
    else:
        f = jnp.where(row == 0, carry_ref[...], pltpu.roll(edge, 1, 0))
    for si, s in enumerate((1, 2, 4)):
        fs = pltpu.roll(f, (SUBLANES - s) if reverse else s, 0)
        f = f + seg_ref[2 * si] * fs + seg_ref[2 * si + 1] * _swap_halves(fs)
    c_ref[...] = f
    fsw = _swap_halves(f)

    def fix(i, _):
        k = (seg - 1 - i) if reverse else i
        x_ref[rows(i), :] = x_ref[rows(i), :] + pw_ref[0, pl.ds(k, 1), :] * f + (sgn * pw_ref[1, pl.ds(k, 1), :]) * fsw
        return 0

    lax.fori_loop(0, seg, fix, 0, unroll=S5_UNROLL)
    leaving = x_ref[0:1, :] if reverse else x_ref[tm - 1:tm, :]
    carry_ref[...] = jnp.broadcast_to(leaving, carry_ref.shape)


def _s5_fwd(u, bd, cd, pw, seg_f, d_skip, B, carry=None):
    N, D = u.shape
    S = N // B
    nch = D // LANES
    W = 2 * CHUNK_STATE
    tm, seg = S5_TILE, S5_SEG
    nt = S // tm

    def body(u_ref, bd_ref, cd_ref, pw_ref, seg_ref, d_ref, z_ref, cin_ref, x_s, carry, c_s, u_s, z_s):
        t = pl.program_id(2)

        @pl.when(t == 0)
        def _():
            carry[...] = jnp.zeros_like(carry)

        cin_ref[...] = carry[...]
        _to_segments(u_s, u_ref, seg)
        uf = u_s[...]
        x_s[...] = jnp.dot(uf.astype(BF16), bd_ref[...], preferred_element_type=F32)
        _seg_scan(x_s, pw_ref, seg_ref, carry, c_s, seg, False)
        y = jnp.dot(x_s[...].astype(BF16), cd_ref[...], preferred_element_type=F32) + d_ref[...] * uf
        z_s[...] = _gelu(y)
        _from_segments(z_ref, z_s, seg)

    (z, carries), moved = _carried_call(
        body, name="s5_fwd", grid=(nch, B, nt),
        in_specs=[pl.BlockSpec((tm, LANES), lambda j, b, t: (b * nt + t, j)),
                  pl.BlockSpec((None, LANES, W), lambda j, b, t: (j, 0, 0)),
                  pl.BlockSpec((None, W, LANES), lambda j, b, t: (j, 0, 0)),
                  pl.BlockSpec((None, 2, seg, W), lambda j, b, t: (j, 0, 0, 0)),
                  pl.BlockSpec((None, 6, SUBLANES, W), lambda j, b, t: (j, 0, 0, 0)),
                  pl.BlockSpec((1, LANES), lambda j, b, t: (0, j))],
        out_specs=[pl.BlockSpec((tm, LANES), lambda j, b, t: (b * nt + t, j)),
                   pl.BlockSpec((None, None, SUBLANES, W), lambda j, b, t: (j, b * nt + t, 0, 0))],
        out_shape=[jax.ShapeDtypeStruct((N, D), BF16), jax.ShapeDtypeStruct((nch, B * nt, SUBLANES, W), F32)],
        scratch_shapes=[pltpu.VMEM((tm, W), F32), pltpu.VMEM((SUBLANES, W), F32), pltpu.VMEM((SUBLANES, W), F32),
                        pltpu.VMEM((tm, LANES), F32), pltpu.VMEM((tm, LANES), F32)],
        operands=(u, bd, cd, pw, seg_f, d_skip), sem=("parallel", "arbitrary", "arbitrary"), carry=carry)
    return z, carries, moved


def _s5_bwd(u, dz, bd, cd, pw, seg_f, seg_b, d_skip, carries, B, carry=None):
    N, D = u.shape
    S = N // B
    nch = D // LANES
    W = 2 * CHUNK_STATE
    tm, seg = S5_TILE, S5_SEG
    nt = S // tm
    tn_dims = (((0,), (0,)), ((), ()))
    nt_dims = (((1,), (1,)), ((), ()))

    def body(u_ref, dz_ref, bd_ref, cd_ref, pw_ref, sf_ref, sb_ref, d_ref, cin_ref,
             du_ref, dbd_ref, dcd_ref, da_ref, dd_ref, x_s, l_s, carry, lcarry, c_s, lc_s, u_s, t_s):
        b = pl.program_id(1)
        t = pl.program_id(2)

        @pl.when((b == 0) & (t == 0))
        def _():
            dbd_ref[...] = jnp.zeros_like(dbd_ref)
            dcd_ref[...] = jnp.zeros_like(dcd_ref)
            da_ref[...] = jnp.zeros_like(da_ref)
            dd_ref[...] = jnp.zeros_like(dd_ref)

        @pl.when(t == 0)
        def _():
            lcarry[...] = jnp.zeros_like(lcarry)

        _to_segments(u_s, u_ref, seg)
        _to_segments(t_s, dz_ref, seg)
        uf = u_s[...]
        uv = uf.astype(BF16)
        carry[...] = cin_ref[...]
        x_s[...] = jnp.dot(uv, bd_ref[...], preferred_element_type=F32)
        _seg_scan(x_s, pw_ref, sf_ref, carry, c_s, seg, False)
        xb = x_s[...].astype(BF16)
        y = jnp.dot(xb, cd_ref[...], preferred_element_type=F32) + d_ref[...] * uf
        dy = t_s[...] * _gelu_grad(y)
        dd_ref[...] += jnp.sum(dy * uf, axis=0, keepdims=True)
        dyb = dy.astype(BF16)
        dcd_ref[...] += lax.dot_general(dyb, xb, tn_dims, preferred_element_type=F32)
        l_s[...] = lax.dot_general(dyb, cd_ref[...], nt_dims, preferred_element_type=F32)
        _seg_scan(l_s, pw_ref, sb_ref, lcarry, lc_s, seg, True)
        lb = l_s[...].astype(BF16)
        dbd_ref[...] += lax.dot_general(uv, lb, tn_dims, preferred_element_type=F32)
        t_s[...] = lax.dot_general(lb, bd_ref[...], nt_dims, preferred_element_type=F32) + d_ref[...] * dy
        _from_segments(du_ref, t_s, seg)
        lam_rest, x_prev = l_s[SUBLANES:, :], x_s[:tm - SUBLANES, :]
        lam_0, c_in = l_s[:SUBLANES, :], c_s[...]
        da_ref[0:1, :] += (jnp.sum(lam_rest * x_prev, axis=0, keepdims=True) + jnp.sum(lam_0 * c_in, axis=0, keepdims=True))
        da_ref[1:2, :] += (jnp.sum(lam_rest * _swap_halves(x_prev), axis=0, keepdims=True)
                           + jnp.sum(lam_0 * _swap_halves(c_in), axis=0, keepdims=True))

    tile = lambda j, b, t: (b * nt + (nt - 1 - t), j)
    chunk3 = lambda j, b, t: (j, 0, 0)
    chunk4 = lambda j, b, t: (j, 0, 0, 0)
    outs, moved = _carried_call(
        body, name="s5_bwd", grid=(nch, B, nt),
        in_specs=[pl.BlockSpec((tm, LANES), tile), pl.BlockSpec((tm, LANES), tile),
                  pl.BlockSpec((None, LANES, W), chunk3), pl.BlockSpec((None, W, LANES), chunk3),
                  pl.BlockSpec((None, 2, seg, W), chunk4), pl.BlockSpec((None, 6, SUBLANES, W), chunk4),
                  pl.BlockSpec((None, 6, SUBLANES, W), chunk4), pl.BlockSpec((1, LANES), lambda j, b, t: (0, j)),
                  pl.BlockSpec((None, None, SUBLANES, W), lambda j, b, t: (j, b * nt + (nt - 1 - t), 0, 0))],
        out_specs=[pl.BlockSpec((tm, LANES), tile), pl.BlockSpec((None, LANES, W), chunk3),
                   pl.BlockSpec((None, LANES, W), chunk3), pl.BlockSpec((None, 2, W), chunk3),
                   pl.BlockSpec((1, LANES), lambda j, b, t: (0, j))],
        out_shape=[jax.ShapeDtypeStruct((N, D), F32), jax.ShapeDtypeStruct((nch, LANES, W), F32),
                   jax.ShapeDtypeStruct((nch, LANES, W), F32), jax.ShapeDtypeStruct((nch, 2, W), F32),
                   jax.ShapeDtypeStruct((1, D), F32)],
        scratch_shapes=[pltpu.VMEM((tm, W), F32), pltpu.VMEM((tm, W), F32)] + [pltpu.VMEM((SUBLANES, W), F32)] * 4
        + [pltpu.VMEM((tm, LANES), F32)] * 2,
        operands=(u, dz, bd, cd, pw, seg_f, seg_b, d_skip, carries), sem=("parallel", "arbitrary", "arbitrary"), carry=carry)
    return (*outs, moved)


ATTN_HEADS = LANES // HEAD_DIM
ATTN_FWD_UNROLL = 4
ATTN_BWD_UNROLL = 4


def _attn_mask(n):
    qi = lax.broadcasted_iota(jnp.int32, (ATTN_HEADS * ATTN_BLOCK, 2 * ATTN_BLOCK), 0) % ATTN_BLOCK
    kj = lax.broadcasted_iota(jnp.int32, (ATTN_HEADS * ATTN_BLOCK, 2 * ATTN_BLOCK), 1)
    prev_ok = (kj < ATTN_BLOCK) & (kj >= qi) & (n > 0)
    return prev_ok | ((kj >= ATTN_BLOCK) & (kj - ATTN_BLOCK <= qi))


def _stack_heads(x):
    return jnp.concatenate([_only_head(x, h) for h in range(ATTN_HEADS)], axis=0)


def _stack_head_columns(x):
    return jnp.concatenate([x[:, h * HEAD_DIM:h * HEAD_DIM + 1] for h in range(ATTN_HEADS)], axis=0)


def _unstack_heads(x):
    return _per_head([x[h * ATTN_BLOCK:(h + 1) * ATTN_BLOCK] for h in range(ATTN_HEADS)])


def _head_lanes(h):
    lane = lax.broadcasted_iota(jnp.int32, (ATTN_BLOCK, LANES), 1)
    return (lane >= h * HEAD_DIM) & (lane < (h + 1) * HEAD_DIM)


def _per_head(cols):
    out = jnp.broadcast_to(cols[-1], (ATTN_BLOCK, LANES))
    for h in range(len(cols) - 2, -1, -1):
        out = jnp.where(_head_lanes(h), jnp.broadcast_to(cols[h], (ATTN_BLOCK, LANES)), out)
    return out


def _only_head(x, h):
    return jnp.where(_head_lanes(h), x, 0.0).astype(BF16)


def _block_rows(tb, dil, nb):
    r = tb // nb
    n = tb % nb
    start = r + dil * ATTN_BLOCK * n
    startp = jnp.where(n > 0, start - dil * ATTN_BLOCK, start)
    return n, pl.ds(start, ATTN_BLOCK, stride=dil), pl.ds(startp, ATTN_BLOCK, stride=dil)


def _attn_fwd(q, k, v, B, carry=None):
    _, S, D3 = q.shape
    D = D3 // 3
    HP = D // LANES
    scale = HEAD_DIM ** -0.5
    n_blocks = S // ATTN_BLOCK
    nbr = len(DILATIONS)
    nt_dims = (((1,), (1,)), ((), ()))

    def branch(dil, q_ref, k_ref, v_ref, acc, m_s, l_s):
        nb = (S // dil) // ATTN_BLOCK

        def blk(tb, _):
            n, rows, rowsp = _block_rows(tb, dil, nb)
            qb = q_ref[rows, :] * scale
            kk = jnp.concatenate([k_ref[rowsp, :], k_ref[rows, :]], axis=0).astype(BF16)
            vv = jnp.concatenate([v_ref[rowsp, :], v_ref[rows, :]], axis=0).astype(BF16)
            s = lax.dot_general(_stack_heads(qb), kk, nt_dims, preferred_element_type=F32)
            s = jnp.where(_attn_mask(n), s, NEG)
            m = jnp.max(s, axis=-1, keepdims=True)
            p = jnp.exp(s - m)
            m_s[rows, :] = _unstack_heads(m)
            l_s[rows, :] = _unstack_heads(jnp.sum(p, axis=-1, keepdims=True))
            acc[rows, :] = _unstack_heads(jnp.dot(p.astype(BF16), vv, preferred_element_type=F32))
            return 0

        lax.fori_loop(0, n_blocks, blk, 0, unroll=ATTN_FWD_UNROLL)

    def body(q_ref, k_ref, v_ref, o_ref, lse_ref, *scratch):
        accs, m_ss, l_ss = scratch[:nbr], scratch[nbr:2 * nbr], scratch[2 * nbr:]
        g = pl.program_id(2)
        for gi, dil in enumerate(DILATIONS):
            pl.when(g == gi)(functools.partial(branch, dil, q_ref, k_ref, v_ref, accs[gi], m_ss[gi], l_ss[gi]))

        @pl.when(g == nbr - 1)
        def _():
            def fin(i, _):
                rows = pl.ds(pl.multiple_of(i * ATTN_BLOCK, ATTN_BLOCK), ATTN_BLOCK)
                ms = [m[rows, :] for m in m_ss]
                m_all = functools.reduce(jnp.maximum, ms)
                ws = [jnp.exp(m - m_all) for m in ms]
                den = sum(w * l[rows, :] for w, l in zip(ws, l_ss))
                o_ref[rows, :] = sum(w * a[rows, :] for w, a in zip(ws, accs)) / den
                lse_ref[rows, :] = m_all + jnp.log(den)
                return 0

            lax.fori_loop(0, n_blocks, fin, 0)

    br = pl.BlockSpec((None, S, LANES), lambda b, hp, g: (b, 0, g * HP + hp))
    hd = pl.BlockSpec((None, S, LANES), lambda b, hp, g: (b, 0, hp))
    (o, lse), moved = _carried_call(
        body, name="attn_fwd", grid=(B, HP, nbr), in_specs=[br, br, br], out_specs=[hd, hd],
        out_shape=[jax.ShapeDtypeStruct((B, S, D), F32), jax.ShapeDtypeStruct((B, S, D), F32)],
        scratch_shapes=[pltpu.VMEM((S, LANES), F32)] * (3 * nbr),
        operands=(q, k, v), sem=("parallel", "parallel", "arbitrary"), carry=carry)
    return o, lse, moved


def _attn_bwd(q, k, v, o, lse, do, dk_prev, dv_prev, B, last, carry=None):
    _, S, D3 = q.shape
    D = D3 // 3
    HP = D // LANES
    scale = HEAD_DIM ** -0.5
    n_blocks = S // ATTN_BLOCK
    has_prev = dk_prev is not None
    nt_dims = (((1,), (1,)), ((), ()))
    tn_dims = (((0,), (0,)), ((), ()))

    def branch(dil, q_ref, k_ref, v_ref, lse_ref, do_ref, dq_s, dk_c, dv_c, delta, dk_p, dv_p):
        nb = (S // dil) // ATTN_BLOCK

        def blk(tb, _):
            n, rows, rowsp = _block_rows(tb, dil, nb)
            qb = q_ref[rows, :] * scale
            dob, lb, db = do_ref[rows, :], lse_ref[rows, :], delta[rows, :]
            kk = jnp.concatenate([k_ref[rowsp, :], k_ref[rows, :]], axis=0).astype(BF16)
            vv = jnp.concatenate([v_ref[rowsp, :], v_ref[rows, :]], axis=0).astype(BF16)
            qs, dos = _stack_heads(qb), _stack_heads(dob)
            s = lax.dot_general(qs, kk, nt_dims, preferred_element_type=F32)
            p = jnp.where(_attn_mask(n), jnp.exp(s - _stack_head_columns(lb)), 0.0)
            dp = lax.dot_general(dos, vv, nt_dims, preferred_element_type=F32)
            ds = (p * (dp - _stack_head_columns(db))).astype(BF16)
            dkk = lax.dot_general(ds, qs, tn_dims, preferred_element_type=F32)
            dvv = lax.dot_general(p.astype(BF16), dos, tn_dims, preferred_element_type=F32)
            dq_s[rows, :] = _unstack_heads(jnp.dot(ds, kk, preferred_element_type=F32)) * scale
            dk_p[rowsp, :] = dkk[:ATTN_BLOCK]
            dv_p[rowsp, :] = dvv[:ATTN_BLOCK]
            dk_c[rows, :] = dkk[ATTN_BLOCK:]
            dv_c[rows, :] = dvv[ATTN_BLOCK:]
            return 0

        lax.fori_loop(0, n_blocks, blk, 0, unroll=ATTN_BWD_UNROLL)

    def body(*refs):
        q_ref, k_ref, v_ref, o_ref, lse_ref, do_ref = refs[:6]
        n_in = 8 if has_prev else 6
        dq_ref, dk_ref, dv_ref, delta, dk_p, dv_p, dq_s, dk_c, dv_c = refs[n_in:n_in + 9]
        g = pl.program_id(2)

        @pl.when(g == 0)
        def _():
            def dl(i, _):
                rows = pl.ds(pl.multiple_of(i * ATTN_BLOCK, ATTN_BLOCK), ATTN_BLOCK)
                prod = do_ref[rows, :] * o_ref[rows, :]
                delta[rows, :] = _per_head([jnp.sum(jnp.where(_head_lanes(h), prod, 0.0), axis=-1, keepdims=True)
                                            for h in range(ATTN_HEADS)])
                return 0

            lax.fori_loop(0, n_blocks, dl, 0)

        dk_p[...] = jnp.zeros_like(dk_p)
        dv_p[...] = jnp.zeros_like(dv_p)
        for gi, dil in enumerate(DILATIONS):
            pl.when(g == gi)(functools.partial(branch, dil, q_ref, k_ref, v_ref, lse_ref, do_ref, dq_s, dk_c, dv_c,
                                               delta, dk_p, dv_p))

        def fin(i, _):
            rows = pl.ds(pl.multiple_of(i * ATTN_BLOCK, ATTN_BLOCK), ATTN_BLOCK)
            dk_t = dk_c[rows, :] + dk_p[rows, :]
            dv_t = dv_c[rows, :] + dv_p[rows, :]
            if has_prev:
                dk_t = dk_t + refs[6][rows, :].astype(F32)
                dv_t = dv_t + refs[7][rows, :].astype(F32)
            dq_ref[rows, :] = dq_s[rows, :].astype(dq_ref.dtype)
            dk_ref[rows, :] = dk_t.astype(dk_ref.dtype)
            dv_ref[rows, :] = dv_t.astype(dv_ref.dtype)
            return 0

        lax.fori_loop(0, n_blocks, fin, 0)

    br = pl.BlockSpec((None, S, LANES), lambda b, hp, g: (b, 0, g * HP + hp))
    hd = pl.BlockSpec((None, S, LANES), lambda b, hp, g: (b, 0, hp))
    ins = [q, k, v, o, lse, do] + ([dk_prev, dv_prev] if has_prev else [])
    kv_dtype = BF16 if last else F32
    (dq, dk, dv), moved = _carried_call(
        body, name="attn_bwd", grid=(B, HP, len(DILATIONS)),
        in_specs=[br, br, br, hd, hd, hd] + ([br, br] if has_prev else []), out_specs=[br, br, br],
        out_shape=[jax.ShapeDtypeStruct(q.shape, BF16), jax.ShapeDtypeStruct(q.shape, kv_dtype),
                   jax.ShapeDtypeStruct(q.shape, kv_dtype)],
        scratch_shapes=[pltpu.VMEM((S, LANES), F32)] * 6,
        operands=ins, sem=("parallel", "parallel", "arbitrary"), carry=carry)
    return dq, dk, dv, moved


def _adamw(w, grads, m, v):
    R, C = w.shape
    tr = _div(R, 256, SUBLANES)
    ng = len(grads)
    c1 = 1.0 - ADAM_B1 ** ADAM_STEP
    c2 = 1.0 - ADAM_B2 ** ADAM_STEP

    def body(*refs):
        w_ref, m_ref, v_ref = refs[0], refs[1 + ng], refs[2 + ng]
        d_ref, mo_ref, vo_ref = refs[3 + ng:6 + ng]
        g = refs[1][...]
        if ng == 2:
            g = g + refs[2][...]
            refs[6 + ng][...] = g
        mn = ADAM_B1 * m_ref[...] + (1.0 - ADAM_B1) * g
        vn = ADAM_B2 * v_ref[...] + (1.0 - ADAM_B2) * (g * g)
        d_ref[...] = -ADAM_LR * ((mn / c1) / (jnp.sqrt(vn / c2) + ADAM_EPS) + ADAM_WD * w_ref[...])
        mo_ref[...] = mn
        vo_ref[...] = vn

    blk = pl.BlockSpec((tr, C), lambda i: (i, 0))
    n_out = 3 + (ng == 2)
    outs = pl.pallas_call(
        body, name="adamw", grid=(R // tr,), in_specs=[blk] * (3 + ng), out_specs=[blk] * n_out,
        out_shape=[jax.ShapeDtypeStruct((R, C), F32)] * n_out, compiler_params=_params("parallel"),
    )(w, *grads, m, v)
    return (outs[3] if ng == 2 else grads[0],) + tuple(outs[:3])


def _sum_shards(recv):
    n, R, C = recv.shape
    tr = _div(R, 256, SUBLANES if recv.dtype == F32 else 2 * SUBLANES)

    def body(r_ref, o_ref):
        s = r_ref[0].astype(F32)
        for i in range(1, n):
            s = s + r_ref[i].astype(F32)
        o_ref[...] = s

    return pl.pallas_call(
        body, name="sum_shards", grid=(R // tr,), in_specs=[pl.BlockSpec((n, tr, C), lambda i: (0, i, 0))],
        out_specs=pl.BlockSpec((tr, C), lambda i: (i, 0)), out_shape=jax.ShapeDtypeStruct((R, C), F32),
        compiler_params=_params("parallel"),
    )(recv)


N_DEV = 8
N_CHIPS = 4


def _all_gather_small(x, carry=None):
    m_per, n = x.shape

    def body(x_ref, out_ref, send_sems, recv_sems, local_sem):
        cx, cy, cc = _coords()
        me, sibling = (cx, cy, cc), (cx, cy, 1 - cc)
        chips = [(1 - cx, cy), (cx, 1 - cy), (1 - cx, 1 - cy)]

        def rows(px, py, pc):
            return out_ref.at[pl.ds((4 * px + 2 * py + pc) * m_per, m_per), :]

        def copy(k, block, to, src=None):
            return pltpu.make_async_remote_copy(
                src_ref=rows(*block) if src is None else src, dst_ref=rows(*block), send_sem=send_sems.at[k],
                recv_sem=recv_sems.at[k], device_id=to, device_id_type=MESH)

        mine = pltpu.make_async_copy(x_ref, rows(*me), local_sem)
        mine.start()
        first = [copy(0, me, sibling, src=x_ref)]
        first += [copy(1 + j, me, (*chip, cc), src=x_ref) for j, chip in enumerate(chips)]
        for cp in first:
            cp.start()
        passed = [copy(4 + j, (*chip, cc), sibling) for j, chip in enumerate(chips)]
        for j, chip in enumerate(chips):
            copy(1 + j, (*chip, cc), me).wait_recv()
            passed[j].start()
        copy(0, sibling, me).wait_recv()
        for j, chip in enumerate(chips):
            copy(4 + j, (*chip, 1 - cc), me).wait_recv()
        for cp in first + passed:
            cp.wait_send()
        mine.wait()

    (out,), moved = _carried_call(
        body, name="all_gather_small", grid=(1,), out_shape=[jax.ShapeDtypeStruct((N_DEV * m_per, n), x.dtype)],
        in_specs=[pl.BlockSpec(memory_space=pltpu.VMEM)], out_specs=[pl.BlockSpec(memory_space=pltpu.VMEM)],
        scratch_shapes=[pltpu.SemaphoreType.DMA((7,)), pltpu.SemaphoreType.DMA((7,)), pltpu.SemaphoreType.DMA],
        operands=(x,), sem=("arbitrary",), carry=carry)
    return out, moved


def _layer_moves(kind, arrays_from, arrays_to, pieces, layer_major=()):
    used = sorted({w for w, _ in pieces})
    pos = {w: i for i, w in enumerate(used)}
    gather = kind == "gather"

    def half(ref, c):
        rows = ref.shape[0] // 2
        return ref.at[pl.ds(c * rows, rows), :]

    def slot(d, w, chip, l):
        return d.at[l, chip] if w in layer_major else d.at[chip, l]

    def plan(src_refs, dst_refs, me):
        cx, cy, cc = me
        mine = 2 * cx + cy
        remote, local = [], []
        for w, l in pieces:
            s, d = src_refs[pos[w]], dst_refs[pos[w]]
            for px, py in _other_chips(cx, cy):
                if gather:
                    remote.append((half(s.at[l], cc), half(slot(d, w, mine, l), cc), (px, py, cc)))
                else:
                    remote.append((s.at[2 * px + py, l], d.at[mine, l], (px, py, cc)))
            local.append((s.at[l], slot(d, w, mine, l)) if gather else (s.at[mine, l], d.at[mine, l]))
        return remote, local

    def onward(src_refs, dst_refs, me):
        cx, cy, cc = me
        moves = []
        for w, l in pieces:
            d = dst_refs[pos[w]]
            for px, py in _other_chips(cx, cy):
                landed = half(slot(d, w, 2 * px + py, l), cc)
                moves.append((landed, landed, (cx, cy, 1 - cc)))
        return moves

    n = 3 * len(pieces)
    carry = _Carry([arrays_from[w] for w in used], [arrays_to[w] for w in used], plan, n, len(pieces),
                   onward if gather else None, n if gather else 0)
    return carry, used


def _swap_with_sibling(sums):
    def plan(src_refs, dst_refs, me):
        cx, cy, cc = me
        return [(s, d, (cx, cy, 1 - cc)) for s, d in zip(src_refs, dst_refs)], []

    return _Carry(sums, [lax.empty(s.shape, s.dtype) for s in sums], plan, len(sums), 0)


def _pack(arrs, width):
    parts, layout, row = [], [], 0
    for a in arrs:
        flat = a.reshape(-1).astype(F32)
        rows = -(-flat.shape[0] // (width * SUBLANES)) * SUBLANES
        parts.append(jnp.pad(flat, (0, rows * width - flat.shape[0])).reshape(rows, width))
        layout.append((row, rows, a.shape))
        row += rows
    pad = -row % (8 * SUBLANES) if row > 8 * SUBLANES else 0
    if pad:
        parts.append(jnp.zeros((pad, width), F32))
    return jnp.concatenate(parts, axis=0), layout, row + pad


def _unpack(buf, layout, idx):
    row, rows, shape = layout[idx]
    size = math.prod(shape)
    return buf[row:row + rows].reshape(-1)[:size].reshape(shape)


def kernel(x, c, ln_g, ada_w, ada_b, ssm_lam_re, ssm_lam_im, ssm_log_dt, ssm_b_re, ssm_b_im, ssm_c_re, ssm_c_im, ssm_d, ssm_w_glu, kv_g, kv_ada_w, kv_ada_b, w_kv, attn_w_q, attn_w_o, mlp_w1, mlp_w2, final_g, loss_target, m_ln_g, m_ada_w, m_ada_b, m_ssm_lam_re, m_ssm_lam_im, m_ssm_log_dt, m_ssm_b_re, m_ssm_b_im, m_ssm_c_re, m_ssm_c_im, m_ssm_d, m_ssm_w_glu, m_kv_g, m_kv_ada_w, m_kv_ada_b, m_w_kv, m_attn_w_q, m_attn_w_o, m_mlp_w1, m_mlp_w2, m_final_g, v_ln_g, v_ada_w, v_ada_b, v_ssm_lam_re, v_ssm_lam_im, v_ssm_log_dt, v_ssm_b_re, v_ssm_b_im, v_ssm_c_re, v_ssm_c_im, v_ssm_d, v_ssm_w_glu, v_kv_g, v_kv_ada_w, v_kv_ada_b, v_w_kv, v_attn_w_q, v_attn_w_o, v_mlp_w1, v_mlp_w2, v_final_g):
    B, S, D = x.shape
    N = B * S
    depth = ln_g.shape[0]
    n_a = ssm_w_glu.shape[0]
    n_b = attn_w_q.shape[0]
    FF = mlp_w1.shape[2] * N_CHIPS
    cx, cy, cc = _coords()
    chip = 2 * cx + cy
    dev = 4 * cx + 2 * cy + cc
    n_ex = N_DEV * B
    ada_cols = ada_w.shape[-1]
    kv_cols = kv_ada_w.shape[-1]

    GLU, KV, Q, O, W1, W2 = range(6)
    shards = [ssm_w_glu.astype(BF16), w_kv.astype(BF16)[None], attn_w_q.astype(BF16), attn_w_o.astype(BF16),
              mlp_w1.astype(BF16), mlp_w2.astype(BF16)]
    row_sharded = (O, W2)
    wg = [lax.empty((s.shape[0], N_CHIPS) + s.shape[1:] if w in row_sharded else (N_CHIPS,) + s.shape, BF16)
          for w, s in enumerate(shards)]

    def whole_rows(w):
        L, _, R, C = wg[w].shape
        return wg[w].reshape(1, L, N_CHIPS * R, C)

    def landed(arrays, used, moved):
        for w, a in zip(used, moved):
            arrays[w] = a

    fetch_with = {}

    def carried_by(kind, l, *pieces):
        fetch_with.setdefault((kind, l), []).extend(pieces)

    assert n_a >= 1 and n_b >= 1, (n_a, n_b)
    carried_by("mixer", 0, *[(GLU, l) for l in range(n_a)], (W1, 0))
    carried_by("glu_proj", 0, (W2, 0))
    carried_by("mlp_up", 0, (Q, 0))
    carried_by("mlp_up", n_a - 1, (O, 0))
    carried_by("mixer", n_a - 1, (KV, 0))
    for l in range(1, depth):
        carried_by("mixer" if l - 1 >= n_a else "mlp_down", l - 1, (W1, l))
        carried_by("mixer", l, (W2, l))
    for j in range(1, n_b):
        carried_by("mixer", n_a + j - 1, (Q, j), (O, j))

    def fetch(kind, l):
        pieces = fetch_with.get((kind, l))
        if not pieces:
            return None, []
        return _layer_moves("gather", shards, wg, pieces, layer_major=row_sharded)

    def mm_carrying(kind, l, *args, **kw):
        carry, used = fetch(kind, l)
        if carry is None:
            return _mm(kind, *args, **kw)
        out, moved = _mm(kind, *args, carry=carry, **kw)
        landed(wg, used, moved)
        return out

    c_pack, c_layout, _ = _pack([c], D)
    c_all_buf, _ = _all_gather_small(c_pack)
    c_rows = c_pack.shape[0]
    c_all = jnp.concatenate([_unpack(c_all_buf[d * c_rows:(d + 1) * c_rows], c_layout, 0) for d in range(N_DEV)], axis=0)
    sc_all = jax.nn.silu(c_all).astype(BF16)
    n_mod = depth * 2
    ada_w8 = ada_w.reshape(n_mod, 1, D, ada_cols)
    ada_b_row = ada_b.reshape(1, n_mod * ada_cols)
    mod_local = _mm("ada_fwd", sc_all, ada_w8, mode="nn", M=n_ex, N=n_mod * ada_cols, K=D, b_lay="cs", b_ns=n_mod,
                    epi=_add, extras=[("n", ada_b_row)])
    kv_ada_b_local = lax.dynamic_slice(kv_ada_b.reshape(N_CHIPS, kv_cols), (chip, 0), (1, kv_cols))
    kvmod_local = _mm("ada_fwd", sc_all, _as4(kv_ada_w), mode="nn", M=n_ex, N=kv_cols, K=D, epi=_add,
                      extras=[("n", kv_ada_b_local)])
    mod_pack, mod_layout, mod_rows = _pack([mod_local, kvmod_local, ln_g, ssm_d], D)
    mod_buf, _ = _all_gather_small(mod_pack)

    def from_chip(j, idx):
        d = 2 * j
        return _unpack(mod_buf[d * mod_rows:(d + 1) * mod_rows], mod_layout, idx)

    my_rows = lambda a: lax.dynamic_slice_in_dim(a, dev * B, B, axis=0)
    mods = jnp.concatenate([my_rows(from_chip(j, 0)).reshape(B, n_mod, ada_cols) for j in range(N_CHIPS)], axis=2)
    kvmod = jnp.concatenate([my_rows(from_chip(j, 1)) for j in range(N_CHIPS)], axis=1)
    ln_g_full = jnp.concatenate([from_chip(j, 2) for j in range(N_CHIPS)], axis=2)
    ssm_d_full = jnp.concatenate([from_chip(j, 3) for j in range(N_CHIPS)], axis=1)

    def mod3(l, s):
        mrow = mods[:, l * 2 + s]
        return [mrow[:, i * D:(i + 1) * D].reshape(B, 1, D) for i in range(3)]

    kv_shift, kv_scale = kvmod[:, :D].reshape(B, 1, D), kvmod[:, D:].reshape(B, 1, D)

    s5_tabs = []
    for l in range(n_a):
        prm = (ssm_lam_re[l], ssm_lam_im[l], ssm_log_dt[l], ssm_b_re[l], ssm_b_im[l], ssm_c_re[l], ssm_c_im[l])
        (bd, cd, _, _), disc_vjp = jax.vjp(_s5_discretize, *prm)
        pw, seg_f, seg_b = _s5_scan_coefs(ssm_lam_re[l], ssm_lam_im[l], ssm_log_dt[l], S5_SEG)
        s5_tabs.append((bd.astype(BF16), cd.astype(BF16), pw, seg_f, seg_b, disc_vjp))

    h = x.reshape(N, D)
    saved = []
    k_all = v_all = None
    shift, scale, gate = mod3(0, 0)
    u = _normmod(h, ln_g_full[0, 0].reshape(1, D), scale, shift, B)
    for l in range(depth):
        sv = {}
        sv["h0"], sv["scale0"], sv["gate0"], sv["u0"] = h, scale, gate, u
        shift1, scale1, gate1 = mod3(l, 1)
        norm1 = [("n", ln_g_full[l, 1].reshape(1, D)), ("ex", scale1), ("ex", shift1)]
        carry, used = fetch("mixer", l)
        if l < n_a:
            bd, cd, pw, seg_f, _, _ = s5_tabs[l]
            z, carries, moved = _s5_fwd(u, bd, cd, pw, seg_f, ssm_d_full[l].reshape(1, D), B, carry)
            landed(wg, used, moved)
            zz = mm_carrying("glu_proj", l, z, wg[GLU], mode="nn", M=N, N=2 * D, K=D, b_lay="cs", b_l=l, b_ns=N_CHIPS)
            y, h, u = _glu_residual_norm(zz, h, gate, ln_g_full[l, 1].reshape(1, D), scale1, shift1, B)
            sv["z"], sv["carries"], sv["zz"] = z, carries, zz
        else:
            j = l - n_a
            q = _mm("q_proj", u, wg[Q], mode="nn", M=N, N=3 * D, K=D, b_lay="cs", b_l=j, b_ns=N_CHIPS)
            q3 = q.reshape(B, S, 3 * D)
            o, lse, moved = _attn_fwd(q3, k_all, v_all, B, carry)
            landed(wg, used, moved)
            o2 = o.reshape(N, D)
            y, h, u = _mm("o_proj", o2, whole_rows(O), mode="nn", M=N, N=D, K=D, b_l=j, tm=1024, out_dtype=(BF16, F32, BF16),
                          epi=_gated_residual_norm, extras=[("mn", h), ("ex", gate)] + norm1, rows_per_ex=S)
            sv["q"], sv["o"], sv["lse"] = q3, o, lse
        sv["y0"] = y
        sv["h1"], sv["scale1"], sv["gate1"], sv["u1"] = h, scale1, gate1, u
        r = mm_carrying("mlp_up", l, u, wg[W1], mode="nn", M=N, N=FF, K=D, b_lay="cs", b_l=l, b_ns=N_CHIPS,
                        out_dtype=BF16, epi=_relu2)
        if l + 1 < depth:
            shift, scale, gate = mod3(l + 1, 0)
            y, h, u = mm_carrying(
                "mlp_down", l, r, whole_rows(W2), mode="nn", M=N, N=D, K=FF, b_l=l, tk=2048, tm=512,
                out_dtype=(BF16, F32, BF16), epi=_gated_residual_norm, rows_per_ex=S,
                extras=[("mn", h), ("ex", gate1), ("n", ln_g_full[l + 1, 0].reshape(1, D)), ("ex", scale), ("ex", shift)])
        else:
            y, h = mm_carrying("mlp_down", l, r, whole_rows(W2), mode="nn", M=N, N=D, K=FF, b_l=l, tk=2048, tm=1024,
                               out_dtype=(BF16, F32), epi=_gated_residual, extras=[("mn", h), ("ex", gate1)], rows_per_ex=S)
        sv["r"], sv["y1"] = r, y
        saved.append(sv)
        if l == n_a - 1:
            h_kv = h
            u_kv = _normmod(h, kv_g.reshape(1, D), kv_scale, kv_shift, B)
            half = N_CHIPS // 2
            k_all = _mm("kv_proj", u_kv, wg[KV], mode="nn", M=N, N=3 * D, K=D, b_lay="cs", b_s0=0, b_ns=half).reshape(B, S, 3 * D)
            v_all = _mm("kv_proj", u_kv, wg[KV], mode="nn", M=N, N=3 * D, K=D, b_lay="cs", b_s0=half, b_ns=half).reshape(B, S, 3 * D)

    loss_buf, dh, d_final_g = _loss_head(h, final_g.reshape(1, D), loss_target.reshape(N, D))
    loss = lax.psum(loss_buf[0, 0], ("x", "y", "c"))

    dg = [lax.empty((N_CHIPS,) + s.shape, BF16) for s in shards]
    recv = [lax.empty((N_CHIPS,) + s.shape, BF16) for s in shards]

    def send(pieces):
        return _layer_moves("scatter", dg, recv, pieces)

    send_with = {l: [(W1, l), (W2, l)] for l in range(depth)}
    for l in range(n_a):
        send_with[l] += [(GLU, l)]
    for j in range(n_b):
        send_with[n_a + j] += [(O, j)]
        send_with[n_a + j - 1] += [(Q, j)]
    send_with[n_a - 1] += [(KV, 0)]
    d_ln_g = [[None, None] for _ in range(depth)]
    d_mods = [[None, None] for _ in range(depth)]
    d_s5 = [None] * n_a
    dk_acc = dv_acc = None
    half = N_CHIPS // 2

    def tn_grad(name, a, d, into, l, Mr, Nc, lay, s0=0, ns=N_CHIPS):
        return _mm(name, a, _as4(d), mode="tn", M=Mr, N=Nc, K=N, b_lay="cs", out_dtype=BF16, out_lay=lay,
                   out4_shape=into.shape, out_into=into, out_l=l, out_s0=s0, out_ns=ns, tk=2048)

    dy, d_gate1 = _residual_bwd(dh, saved[-1]["gate1"], saved[-1]["y1"], B)
    for l in reversed(range(depth)):
        sv = saved[l]
        dg[W2] = tn_grad("mlp_down_dw", sv["r"], dy, dg[W2], l, FF, D, "rs")
        da = _mm("mlp_down_dx", dy, whole_rows(W2), mode="nt", M=N, N=FF, K=D, b_l=l, out_dtype=BF16,
                 epi=_relu2_bwd, extras=[("mn", sv["r"])])
        dg[W1] = tn_grad("mlp_up_dw", sv["u1"], da, dg[W1], l, D, FF, "cs")
        du = _mm("mlp_up_dx", da, wg[W1], mode="nt", M=N, N=D, K=FF, b_lay="cs", b_l=l, b_ns=N_CHIPS)
        dh, dgv, d_scale1, d_shift1, dy, d_gate0 = _normmod_bwd(du, sv["h1"], ln_g_full[l, 1].reshape(1, D), sv["scale1"],
                                                                dh, B, below=(sv["y0"], sv["gate0"]))
        d_ln_g[l][1] = dgv
        d_mods[l][1] = jnp.concatenate([d_shift1, d_scale1, d_gate1], axis=2)
        if l < n_a:
            bd, cd, pw, seg_f, seg_b, disc_vjp = s5_tabs[l]
            dzz = _glu_bwd(dy, sv["zz"])
            dg[GLU] = tn_grad("glu_proj_dw", sv["z"], dzz, dg[GLU], l, D, 2 * D, "cs")
            dz = _mm("glu_proj_dx", dzz, wg[GLU], mode="nt", M=N, N=D, K=2 * D, b_lay="cs", b_l=l, b_ns=N_CHIPS)
            carry, used = send(send_with[l])
            du, d_bd, d_cd, d_a2, d_dskip, moved = _s5_bwd(sv["u0"], dz, bd, cd, pw, seg_f, seg_b,
                                                           ssm_d_full[l].reshape(1, D), sv["carries"], B, carry)
            landed(recv, used, moved)
            d_are = (d_a2[:, 0, :CHUNK_STATE] + d_a2[:, 0, CHUNK_STATE:]).reshape(-1, SSM_STATE)
            d_aim = (d_a2[:, 1, CHUNK_STATE:] - d_a2[:, 1, :CHUNK_STATE]).reshape(-1, SSM_STATE)
            d_s5[l] = disc_vjp((d_bd, jnp.swapaxes(d_cd, 1, 2), d_are, d_aim)) + (d_dskip,)
        else:
            j = l - n_a
            dg[O] = tn_grad("o_proj_dw", sv["o"].reshape(N, D), dy, dg[O], j, D, D, "rs")
            do = _mm("o_proj_dx", dy, whole_rows(O), mode="nt", M=N, N=D, K=D, b_l=j)
            carry, used = send(send_with[l])
            dq, dk_acc, dv_acc, moved = _attn_bwd(sv["q"], k_all, v_all, sv["o"], sv["lse"], do.reshape(B, S, D),
                                                  dk_acc, dv_acc, B, l == n_a, carry)
            landed(recv, used, moved)
            dq2 = dq.reshape(N, 3 * D)
            dg[Q] = tn_grad("q_proj_dw", sv["u0"], dq2, dg[Q], j, D, 3 * D, "cs")
            du = _mm("q_proj_dx", dq2, wg[Q], mode="nt", M=N, N=D, K=3 * D, b_lay="cs", b_l=j, b_ns=N_CHIPS)
        below = (saved[l - 1]["y1"], saved[l - 1]["gate1"]) if l > 0 else None
        dh, dgv, d_scale0, d_shift0, dy, d_gate1 = _normmod_bwd(du, sv["h0"], ln_g_full[l, 0].reshape(1, D), sv["scale0"],
                                                                dh, B, below=None if l == n_a else below)
        d_ln_g[l][0] = dgv
        d_mods[l][0] = jnp.concatenate([d_shift0, d_scale0, d_gate0], axis=2)
        if l == n_a:
            dk2, dv2 = dk_acc.reshape(N, 3 * D), dv_acc.reshape(N, 3 * D)
            dg[KV] = tn_grad("kv_proj_dw", u_kv, dk2, dg[KV], 0, D, 3 * D, "cs", s0=0, ns=half)
            dg[KV] = tn_grad("kv_proj_dw", u_kv, dv2, dg[KV], 0, D, 3 * D, "cs", s0=half, ns=half)
            du_kv = _mm("kv_proj_dx", dk2, wg[KV], mode="nt", M=N, N=D, K=3 * D, b_lay="cs", b_s0=0, b_ns=half)
            du_kv = _mm("kv_proj_dx", dv2, wg[KV], mode="nt", M=N, N=D, K=3 * D, b_lay="cs", b_s0=half, b_ns=half,
                        tm=1024, epi=_add, extras=[("mn", du_kv)])
            dh, d_kv_g, d_kv_scale, d_kv_shift, dy, d_gate1 = _normmod_bwd(du_kv, h_kv, kv_g.reshape(1, D), kv_scale, dh, B,
                                                                           below=below)
    grad_x = dh.reshape(B, S, D)

    own = [_sum_shards(r.reshape(N_CHIPS, -1, r.shape[-1])) for r in recv]

    d_kvmod = jnp.concatenate([d_kv_shift, d_kv_scale], axis=2).reshape(B, 2 * D)
    d_mod_all = jnp.concatenate([d_mods[l][s].reshape(B, 3 * D) for l in range(depth) for s in range(2)], axis=1)
    small = [
        d_mod_all, d_kvmod,
        jnp.stack([jnp.stack([d_ln_g[l][0].reshape(D), d_ln_g[l][1].reshape(D)]) for l in range(depth)]),
        jnp.stack([d_s5[l][0] for l in range(n_a)]), jnp.stack([d_s5[l][1] for l in range(n_a)]),
        jnp.stack([d_s5[l][2] for l in range(n_a)]),
        jnp.stack([d_s5[l][3] for l in range(n_a)]), jnp.stack([d_s5[l][4] for l in range(n_a)]),
        jnp.stack([d_s5[l][5] for l in range(n_a)]), jnp.stack([d_s5[l][6] for l in range(n_a)]),
        jnp.stack([d_s5[l][7].reshape(D) for l in range(n_a)]),
        d_kv_g.reshape(D), d_final_g.reshape(D),
    ]
    small_pack, small_layout, small_rows = _pack(small, D)
    small_buf, other = _all_gather_small(small_pack, _swap_with_sibling(own))
    small_sum = _sum_shards(small_buf.reshape(N_DEV, small_rows, D))
    red = lambda idx: _unpack(small_sum, small_layout, idx)
    per_dev = lambda idx: jnp.concatenate(
        [_unpack(small_buf[d * small_rows:(d + 1) * small_rows], small_layout, idx) for d in range(N_DEV)], axis=0)

    dm_all = per_dev(0).reshape(n_ex, n_mod, 3 * D)
    dm_cols = lax.dynamic_slice_in_dim(dm_all, chip * ada_cols, ada_cols, axis=2).reshape(n_ex, n_mod * ada_cols)
    g_ada_w = _mm("ada_dw", sc_all, _as4(dm_cols), mode="tn", M=D, N=n_mod * ada_cols, K=n_ex, b_lay="cs",
                  out_lay="cs", out4_shape=(n_mod, 1, D, ada_cols), out_ns=n_mod).reshape(ada_w.shape)
    dkvm_all = per_dev(1)
    dkvm_cols = lax.dynamic_slice_in_dim(dkvm_all, chip * kv_cols, kv_cols, axis=1)
    g_kv_ada_w = _mm("ada_dw", sc_all, _as4(dkvm_cols), mode="tn", M=D, N=kv_cols, K=n_ex, b_lay="cs")
    g_ada_b_full = (red(0)[0] + red(0)[1]).reshape(depth, 2, 3 * D) if B == 2 else jnp.sum(red(0), axis=0).reshape(depth, 2, 3 * D)
    g_ada_b = lax.dynamic_slice_in_dim(g_ada_b_full, chip * ada_cols, ada_cols, axis=2)
    g_kv_ada_b = red(1)[0] + red(1)[1] if B == 2 else jnp.sum(red(1), axis=0)
    g_ln_g = lax.dynamic_slice_in_dim(red(2), chip * (D // N_CHIPS), D // N_CHIPS, axis=2)
    g_ssm_d = lax.dynamic_slice_in_dim(red(10), chip * (D // N_CHIPS), D // N_CHIPS, axis=1)
    small_grads = {
        "ln_g": g_ln_g, "ada_b": g_ada_b, "ssm_lam_re": red(3), "ssm_lam_im": red(4), "ssm_log_dt": red(5),
        "ssm_b_re": red(6), "ssm_b_im": red(7), "ssm_c_re": red(8), "ssm_c_im": red(9), "ssm_d": g_ssm_d,
        "kv_g": red(11), "kv_ada_b": g_kv_ada_b, "final_g": red(12),
    }
    small_w = {"ln_g": (ln_g, m_ln_g, v_ln_g), "ada_b": (ada_b, m_ada_b, v_ada_b),
               "ssm_lam_re": (ssm_lam_re, m_ssm_lam_re, v_ssm_lam_re), "ssm_lam_im": (ssm_lam_im, m_ssm_lam_im, v_ssm_lam_im),
               "ssm_log_dt": (ssm_log_dt, m_ssm_log_dt, v_ssm_log_dt), "ssm_b_re": (ssm_b_re, m_ssm_b_re, v_ssm_b_re),
               "ssm_b_im": (ssm_b_im, m_ssm_b_im, v_ssm_b_im), "ssm_c_re": (ssm_c_re, m_ssm_c_re, v_ssm_c_re),
               "ssm_c_im": (ssm_c_im, m_ssm_c_im, v_ssm_c_im), "ssm_d": (ssm_d, m_ssm_d, v_ssm_d),
               "kv_g": (kv_g, m_kv_g, v_kv_g), "kv_ada_b": (kv_ada_b, m_kv_ada_b, v_kv_ada_b),
               "final_g": (final_g, m_final_g, v_final_g)}
    names = list(small_w)
    wp, lay_w, _ = _pack([small_w[n][0] for n in names], D)
    gp, _, _ = _pack([small_grads[n] for n in names], D)
    mp, _, _ = _pack([small_w[n][1] for n in names], D)
    vp, _, _ = _pack([small_w[n][2] for n in names], D)
    _, d_p, m_p, v_p = _adamw(wp, [gp], mp, vp)
    upd = {n: (small_grads[n].reshape(small_w[n][0].shape), _unpack(d_p, lay_w, i), _unpack(m_p, lay_w, i), _unpack(v_p, lay_w, i))
           for i, n in enumerate(names)}

    def big(w, m, v, g_own, g_other=None):
        C = w.shape[-1]
        gs = [g_own.reshape(-1, C)] + ([g_other.reshape(-1, C)] if g_other is not None else [])
        return tuple(t.reshape(w.shape) for t in _adamw(w.reshape(-1, C), gs, m.reshape(-1, C), v.reshape(-1, C)))

    upd["ssm_w_glu"] = big(ssm_w_glu, m_ssm_w_glu, v_ssm_w_glu, own[0], other[0])
    upd["w_kv"] = big(w_kv, m_w_kv, v_w_kv, own[1], other[1])
    upd["attn_w_q"] = big(attn_w_q, m_attn_w_q, v_attn_w_q, own[2], other[2])
    upd["attn_w_o"] = big(attn_w_o, m_attn_w_o, v_attn_w_o, own[3], other[3])
    upd["mlp_w1"] = big(mlp_w1, m_mlp_w1, v_mlp_w1, own[4], other[4])
    upd["mlp_w2"] = big(mlp_w2, m_mlp_w2, v_mlp_w2, own[5], other[5])
    upd["ada_w"] = big(ada_w, m_ada_w, v_ada_w, g_ada_w)
    upd["kv_ada_w"] = big(kv_ada_w, m_kv_ada_w, v_kv_ada_w, g_kv_ada_w)

    order = ["ln_g", "ada_w", "ada_b", "ssm_lam_re", "ssm_lam_im", "ssm_log_dt", "ssm_b_re", "ssm_b_im", "ssm_c_re",
             "ssm_c_im", "ssm_d", "ssm_w_glu", "kv_g", "kv_ada_w", "kv_ada_b", "w_kv", "attn_w_q", "attn_w_o", "mlp_w1",
             "mlp_w2", "final_g"]
    return (loss, grad_x, *[upd[n][0] for n in order], *[upd[n][1] for n in order], *[upd[n][2] for n in order],
            *[upd[n][3] for n in order])
```

```python
import functools
import math

import jax
import jax.numpy as jnp
from jax import lax
from jax.experimental import pallas as pl
from jax.experimental.pallas import tpu as pltpu

F32 = jnp.float32
BF16 = jnp.bfloat16
MESH = pl.DeviceIdType.MESH

EPS = 1e-6
NEG = -1e30
SSM_GROUP = 16
SSM_STATE = 64
HEAD_DIM = 64
ATTN_BLOCK = 128
DILATIONS = (1, 4, 16)
ADAM_LR, ADAM_B1, ADAM_B2, ADAM_EPS, ADAM_WD, ADAM_STEP = 0.001, 0.9, 0.999, 1e-08, 0.01, 10

LANES = 128
SUBLANES = 8
CHUNK_GROUPS = LANES // SSM_GROUP
CHUNK_STATE = CHUNK_GROUPS * SSM_STATE
VMEM_LIMIT = 56 * 1024 * 1024


def _div(dim, pref, mult):
    t = min(pref, dim) // mult * mult
    while t >= mult:
        if dim % t == 0:
            return t
        t -= mult
    return dim


def _params(*sem):
    return pltpu.CompilerParams(dimension_semantics=sem, vmem_limit_bytes=VMEM_LIMIT)


def _coords():
    return lax.axis_index("x"), lax.axis_index("y"), lax.axis_index("c")


def _other_chips(cx, cy):
    return [(1 - cx, cy), (cx, 1 - cy), (1 - cx, 1 - cy)]


class _Carry:
    def __init__(self, srcs, dsts, plan, n_remote, n_local, onward=None, n_onward=0):
        self.srcs, self.dsts, self.plan, self.n_remote, self.n_local = list(srcs), list(dsts), plan, n_remote, n_local
        self.onward, self.n_onward = onward, n_onward


def _carried_call(body, *, name, grid, in_specs, out_specs, out_shape, scratch_shapes, operands, sem, carry=None):
    if carry is None:
        outs = pl.pallas_call(body, name=name, grid=grid, in_specs=in_specs, out_specs=out_specs, out_shape=out_shape,
                              scratch_shapes=scratch_shapes, compiler_params=_params(*sem))(*operands)
        return list(outs), []
    n_in, n_out, n_scr = len(in_specs), len(out_specs), len(scratch_shapes)
    ns, nd = len(carry.srcs), len(carry.dsts)

    def wrapped(*refs):
        base_in, src_refs = refs[:n_in], refs[n_in:n_in + ns]
        o0 = n_in + ns + nd
        base_out, dst_refs = refs[o0:o0 + n_out], refs[o0 + n_out:o0 + n_out + nd]
        s0 = o0 + n_out + nd
        base_scr = refs[s0:s0 + n_scr]
        send_sems, recv_sems, local_sems = refs[s0 + n_scr:]
        pids = [pl.program_id(a) for a in range(len(grid))]
        first = functools.reduce(jnp.logical_and, [p == 0 for p in pids])
        last = functools.reduce(jnp.logical_and, [p == g - 1 for p, g in zip(pids, grid)])

        def remote_copies(moves, k0):
            return [pltpu.make_async_remote_copy(src_ref=s, dst_ref=d, send_sem=send_sems.at[k0 + i], recv_sem=recv_sems.at[k0 + i],
                                                 device_id=peer, device_id_type=MESH) for i, (s, d, peer) in enumerate(moves)]

        def copies():
            remote, local = carry.plan(src_refs, dst_refs, _coords())
            return remote_copies(remote, 0), [pltpu.make_async_copy(s, d, local_sems.at[i]) for i, (s, d) in enumerate(local)]

        @pl.when(first)
        def _():
            remote, local = copies()
            for cp in local + remote:
                cp.start()

        body(*base_in, *base_out, *base_scr)

        @pl.when(last)
        def _():
            remote, local = copies()
            for cp in remote:
                cp.wait_send()
                cp.wait_recv()
            for cp in local:
                cp.wait()
            if carry.onward is not None:
                second = remote_copies(carry.onward(src_refs, dst_refs, _coords()), carry.n_remote)
                for cp in second:
                    cp.start()
                for cp in second:
                    cp.wait_send()
                    cp.wait_recv()

    anyspec = pl.BlockSpec(memory_space=pl.ANY)
    outs = pl.pallas_call(
        wrapped, name=name, grid=grid, in_specs=list(in_specs) + [anyspec] * (ns + nd),
        out_specs=list(out_specs) + [anyspec] * nd,
        out_shape=list(out_shape) + [jax.ShapeDtypeStruct(d.shape, d.dtype) for d in carry.dsts],
        scratch_shapes=list(scratch_shapes) + [pltpu.SemaphoreType.DMA((carry.n_remote + carry.n_onward,)),
                                               pltpu.SemaphoreType.DMA((carry.n_remote + carry.n_onward,)),
                                               pltpu.SemaphoreType.DMA((max(carry.n_local, 1),))],
        input_output_aliases={n_in + ns + i: n_out + i for i in range(nd)},
        compiler_params=_params(*(["arbitrary"] * len(grid))),
    )(*operands, *carry.srcs, *carry.dsts)
    return list(outs[:n_out]), list(outs[n_out:])


def _mm(name, a, b4, *, mode, M, N, K, b_lay="cs", b_l=0, b_s0=0, b_ns=1, out_dtype=F32, out_lay=None, out4_shape=None,
        out_into=None, out_l=0, out_s0=0, out_ns=1, epi=None, extras=(), rows_per_ex=None, tm=2048, tn=1024, tk=1024,
        carry=None):
    _, _, bR, bC = b4.shape
    tm = _div(M, tm, SUBLANES if M % 16 else 16)
    brows, bcols = (N, K) if mode == "nt" else (K, N)
    if b_lay == "cs":
        assert bR == brows and bC * b_ns == bcols, (name, b4.shape, brows, bcols)
    else:
        assert bC == bcols and bR * b_ns == brows, (name, b4.shape, brows, bcols)
    n_lim = N
    k_lim = K
    if mode == "nt":
        if b_lay == "cs":
            k_lim = bC
        else:
            n_lim = bR
    else:
        if b_lay == "cs":
            n_lim = bC
        else:
            k_lim = bR
    if out_lay == "cs":
        oR, oC = out4_shape[2], out4_shape[3]
        assert oR == M and oC * out_ns == N, (name, out4_shape, M, N)
        n_lim = math.gcd(n_lim, oC)
    elif out_lay == "rs":
        oR, oC = out4_shape[2], out4_shape[3]
        assert oC == N and oR * out_ns == M, (name, out4_shape, M, N)
        tm = _div(oR, tm, SUBLANES)
    tn = _div(n_lim, tn, LANES)
    tk = _div(k_lim, tk, LANES if mode != "tn" else SUBLANES)
    if mode == "tn":
        tk = _div(k_lim, tk, 16) if k_lim % 16 == 0 else tk
    nk = K // tk
    grid = (M // tm, N // tn, nk)

    if mode == "tn":
        a_spec = pl.BlockSpec((tk, tm), lambda i, j, k: (k, i))
    else:
        a_spec = pl.BlockSpec((tm, tk), lambda i, j, k: (i, k))

    def b_index(ri, ci, br, bc):
        if b_lay == "cs":
            per = bC // bc
            return (b_s0 + ci // per, b_l, ri, ci % per)
        per = bR // br
        return (b_s0 + ri // per, b_l, ri % per, ci)

    if mode == "nt":
        b_spec = pl.BlockSpec((None, None, tn, tk), lambda i, j, k: b_index(j, k, tn, tk))
    else:
        b_spec = pl.BlockSpec((None, None, tk, tn), lambda i, j, k: b_index(k, j, tk, tn))

    in_specs = [a_spec, b_spec]
    operands = [a, b4]
    for kind, arr in extras:
        if kind == "mn":
            in_specs.append(pl.BlockSpec((tm, tn), lambda i, j, k: (i, j)))
        elif kind == "ex":
            per_ex = rows_per_ex // tm
            in_specs.append(pl.BlockSpec((None, 1, tn), lambda i, j, k: (i // per_ex, 0, j)))
        else:
            in_specs.append(pl.BlockSpec((1, tn), lambda i, j, k: (0, j)))
        operands.append(arr)
    n_extra = len(extras)

    multi = isinstance(out_dtype, tuple)
    n_out = len(out_dtype) if multi else 1
    if out_lay is None:
        out_shape = [jax.ShapeDtypeStruct((M, N), dt) for dt in (out_dtype if multi else (out_dtype,))]
        out_spec = [pl.BlockSpec((tm, tn), lambda i, j, k: (i, j)) for _ in range(n_out)]
    else:
        out_shape = [jax.ShapeDtypeStruct(tuple(out4_shape), out_dtype)]
        if out_lay == "cs":
            per_o = oC // tn
            out_spec = [pl.BlockSpec((None, None, tm, tn), lambda i, j, k: (out_s0 + j // per_o, out_l, i, j % per_o))]
        else:
            per_o = oR // tm
            out_spec = [pl.BlockSpec((None, None, tm, tn), lambda i, j, k: (out_s0 + i // per_o, out_l, i % per_o, j))]
    aliases = {}
    if out_into is not None:
        in_specs.append(pl.BlockSpec(memory_space=pl.ANY))
        operands.append(out_into)
        aliases = {len(operands) - 1: 0}

    dims = {"nn": (((1,), (0,)), ((), ())), "nt": (((1,), (1,)), ((), ())), "tn": (((0,), (0,)), ((), ()))}[mode]

    def body(a_ref, b_ref, *rest):
        extra_refs = rest[:n_extra]
        o_refs = rest[len(rest) - n_out - (nk > 1):len(rest) - (nk > 1)]

        def finish(r):
            if epi is not None:
                r = epi(r, *[e[...] for e in extra_refs])
            for o_ref, val in zip(o_refs, r if multi else (r,)):
                o_ref[...] = val.astype(o_ref.dtype)

        part = lax.dot_general(a_ref[...].astype(BF16), b_ref[...].astype(BF16), dims, preferred_element_type=F32)
        if nk == 1:
            finish(part)
            return
        acc = rest[-1]
        k = pl.program_id(2)

        @pl.when(k == 0)
        def _():
            acc[...] = part

        @pl.when(k != 0)
        def _():
            acc[...] += part

        @pl.when(k == nk - 1)
        def _():
            finish(acc[...])

    scratch = [pltpu.VMEM((tm, tn), F32)] if nk > 1 else []
    if carry is not None:
        assert out_into is None, name
        outs, moved = _carried_call(body, name=name, grid=grid, in_specs=in_specs, out_specs=out_spec, out_shape=out_shape,
                                    scratch_shapes=scratch, operands=operands, sem=("arbitrary",) * 3, carry=carry)
        return (tuple(outs) if multi else outs[0]), moved
    outs = pl.pallas_call(
        body, name=name, grid=grid, in_specs=in_specs, out_specs=out_spec, out_shape=out_shape,
        scratch_shapes=scratch, input_output_aliases=aliases,
        compiler_params=_params("parallel", "parallel", "arbitrary"),
    )(*operands)
    return tuple(outs) if multi else outs[0]


def _as4(w):
    return w.reshape((1, 1) + w.shape)


def _relu2(acc):
    r = jnp.maximum(acc, 0.0)
    return r * r


def _relu2_bwd(acc, r):
    return acc * (2.0 * jnp.sqrt(r.astype(F32)))


def _add(acc, e):
    return acc + e


def _gated_residual(acc, h, gate):
    return acc, h + gate * acc


def _modulated_norm(x, g, scale, shift):
    rstd = lax.rsqrt(jnp.mean(x * x, axis=-1, keepdims=True) + EPS)
    return ((x * rstd) * g) * (1.0 + scale) + shift


def _gated_residual_norm(acc, h, gate, g, scale, shift):
    h_new = h + gate * acc
    return acc, h_new, _modulated_norm(h_new, g, scale, shift)


def _row_tiles(N, B, pref=256):
    S = N // B
    tm = _div(S, pref, SUBLANES)
    return tm, S // tm


def _normmod(h, g, scale, shift, B):
    N, D = h.shape
    tm, per_ex = _row_tiles(N, B)

    def body(h_ref, g_ref, sc_ref, sh_ref, u_ref):
        u_ref[...] = _modulated_norm(h_ref[...], g_ref[...], sc_ref[...], sh_ref[...]).astype(u_ref.dtype)

    tok = pl.BlockSpec((tm, D), lambda i: (i, 0))
    vec = pl.BlockSpec((1, D), lambda i: (0, 0))
    ex = pl.BlockSpec((None, 1, D), lambda i: (i // per_ex, 0, 0))
    return pl.pallas_call(
        body, name="normmod_fwd", grid=(N // tm,), in_specs=[tok, vec, ex, ex], out_specs=tok,
        out_shape=jax.ShapeDtypeStruct((N, D), BF16), compiler_params=_params("parallel"),
    )(h, g, scale, shift)


def _normmod_bwd(du, h, g, scale, dh_in, B, below=None):
    N, D = h.shape
    tm, per_ex = _row_tiles(N, B)
    fused = below is not None

    def body(*refs):
        du_ref, h_ref, g_ref, sc_ref, dhin_ref = refs[:5]
        dh_ref, dg_ref, dsc_ref, dsh_ref = refs[5 + 2 * fused:9 + 2 * fused]
        i = pl.program_id(0)
        x = h_ref[...]
        gv = g_ref[...]
        d_u = du_ref[...].astype(F32)
        rstd = lax.rsqrt(jnp.mean(x * x, axis=-1, keepdims=True) + EPS)
        xn = x * rstd
        dyg = d_u * (1.0 + sc_ref[...])
        dxn = dyg * gv
        dh = dhin_ref[...] + rstd * (dxn - xn * jnp.mean(dxn * xn, axis=-1, keepdims=True))
        dh_ref[...] = dh
        sums = [(dsc_ref, jnp.sum(d_u * (xn * gv), axis=0, keepdims=True)), (dsh_ref, jnp.sum(d_u, axis=0, keepdims=True))]
        if fused:
            y_ref, gt_ref = refs[5:7]
            dy_ref, dgt_ref = refs[9 + 2 * fused:]
            dy_ref[...] = (gt_ref[...] * dh).astype(dy_ref.dtype)
            sums.append((dgt_ref, jnp.sum(dh * y_ref[...], axis=0, keepdims=True)))
        dg_t = jnp.sum(dyg * xn, axis=0, keepdims=True)

        @pl.when(i % per_ex == 0)
        def _():
            for ref, val in sums:
                ref[...] = val

        @pl.when(i % per_ex != 0)
        def _():
            for ref, val in sums:
                ref[...] += val

        @pl.when(i == 0)
        def _():
            dg_ref[...] = dg_t

        @pl.when(i != 0)
        def _():
            dg_ref[...] += dg_t

    tok = pl.BlockSpec((tm, D), lambda i: (i, 0))
    vec = pl.BlockSpec((1, D), lambda i: (0, 0))
    ex = pl.BlockSpec((None, 1, D), lambda i: (i // per_ex, 0, 0))
    per_ex_shape = jax.ShapeDtypeStruct((B, 1, D), F32)
    outs = pl.pallas_call(
        body, name="normmod_bwd", grid=(N // tm,), in_specs=[tok, tok, vec, ex, tok] + ([tok, ex] if fused else []),
        out_specs=[tok, vec, ex, ex] + ([tok, ex] if fused else []),
        out_shape=[jax.ShapeDtypeStruct((N, D), F32), jax.ShapeDtypeStruct((1, D), F32), per_ex_shape, per_ex_shape]
        + ([jax.ShapeDtypeStruct((N, D), BF16), per_ex_shape] if fused else []),
        compiler_params=_params("arbitrary"),
    )(du, h, g, scale, dh_in, *(below if fused else ()))
    return tuple(outs) if fused else tuple(outs) + (None, None)


def _residual_bwd(dh, gate, y, B):
    N, D = dh.shape
    tm, per_ex = _row_tiles(N, B)

    def body(dh_ref, gt_ref, y_ref, dy_ref, dgt_ref):
        i = pl.program_id(0)
        d = dh_ref[...]
        dy_ref[...] = (gt_ref[...] * d).astype(dy_ref.dtype)
        t = jnp.sum(d * y_ref[...], axis=0, keepdims=True)

        @pl.when(i % per_ex == 0)
        def _():
            dgt_ref[...] = t

        @pl.when(i % per_ex != 0)
        def _():
            dgt_ref[...] += t

    tok = pl.BlockSpec((tm, D), lambda i: (i, 0))
    ex = pl.BlockSpec((None, 1, D), lambda i: (i // per_ex, 0, 0))
    return pl.pallas_call(
        body, name="residual_bwd", grid=(N // tm,), in_specs=[tok, ex, tok], out_specs=[tok, ex],
        out_shape=[jax.ShapeDtypeStruct((N, D), BF16), jax.ShapeDtypeStruct((B, 1, D), F32)],
        compiler_params=_params("arbitrary"),
    )(dh, gate, y)


def _glu_residual_norm(zz, h, gate, g, scale, shift, B):
    N, D2 = zz.shape
    D = D2 // 2
    tm, per_ex = _row_tiles(N, B)

    def body(v_ref, g_ref, h_ref, gt_ref, ng_ref, sc_ref, sh_ref, y_ref, o_ref, u_ref):
        y = v_ref[...] * jax.nn.sigmoid(g_ref[...])
        y_ref[...] = y.astype(y_ref.dtype)
        h_new = h_ref[...] + gt_ref[...] * y
        o_ref[...] = h_new
        u_ref[...] = _modulated_norm(h_new, ng_ref[...], sc_ref[...], sh_ref[...]).astype(u_ref.dtype)

    tok = pl.BlockSpec((tm, D), lambda i: (i, 0))
    vec = pl.BlockSpec((1, D), lambda i: (0, 0))
    ex = pl.BlockSpec((None, 1, D), lambda i: (i // per_ex, 0, 0))
    return pl.pallas_call(
        body, name="glu_fwd", grid=(N // tm,),
        in_specs=[tok, pl.BlockSpec((tm, D), lambda i: (i, 1)), tok, ex, vec, ex, ex], out_specs=[tok, tok, tok],
        out_shape=[jax.ShapeDtypeStruct((N, D), BF16), jax.ShapeDtypeStruct((N, D), F32), jax.ShapeDtypeStruct((N, D), BF16)],
        compiler_params=_params("parallel"),
    )(zz, zz, h, gate, g, scale, shift)


def _glu_bwd(dy, zz):
    N, D2 = zz.shape
    D = D2 // 2
    tm = _div(N, 256, SUBLANES)

    def body(dy_ref, v_ref, g_ref, o_ref):
        d = dy_ref[...].astype(F32)
        s = jax.nn.sigmoid(g_ref[...])
        o_ref[...] = jnp.concatenate([d * s, d * v_ref[...] * s * (1.0 - s)], axis=1).astype(o_ref.dtype)

    return pl.pallas_call(
        body, name="glu_bwd", grid=(N // tm,),
        in_specs=[pl.BlockSpec((tm, D), lambda i: (i, 0)), pl.BlockSpec((tm, D), lambda i: (i, 0)),
                  pl.BlockSpec((tm, D), lambda i: (i, 1))],
        out_specs=pl.BlockSpec((tm, D2), lambda i: (i, 0)), out_shape=jax.ShapeDtypeStruct((N, D2), BF16),
        compiler_params=_params("parallel"),
    )(dy, zz, zz)


def _loss_head(h, g, target):
    N, D = h.shape
    tm = _div(N, 256, SUBLANES)

    def body(h_ref, g_ref, t_ref, loss_ref, dh_ref, dg_ref):
        i = pl.program_id(0)
        x = h_ref[...]
        gv = g_ref[...]
        rstd = lax.rsqrt(jnp.mean(x * x, axis=-1, keepdims=True) + EPS)
        xn = x * rstd
        err = xn * gv - t_ref[...]
        part = 0.5 * jnp.sum(jnp.sum(err * err, axis=-1, keepdims=True) / D, axis=0, keepdims=True)
        dy = err / D
        dxn = dy * gv
        dh_ref[...] = rstd * (dxn - xn * jnp.mean(dxn * xn, axis=-1, keepdims=True))
        dg_t = jnp.sum(dy * xn, axis=0, keepdims=True)
        part = jnp.broadcast_to(part, loss_ref.shape)

        @pl.when(i == 0)
        def _():
            loss_ref[...] = part
            dg_ref[...] = dg_t

        @pl.when(i != 0)
        def _():
            loss_ref[...] += part
            dg_ref[...] += dg_t

    tok = pl.BlockSpec((tm, D), lambda i: (i, 0))
    vec = pl.BlockSpec((1, D), lambda i: (0, 0))
    return pl.pallas_call(
        body, name="loss_head", grid=(N // tm,), in_specs=[tok, vec, tok],
        out_specs=[pl.BlockSpec((SUBLANES, LANES), lambda i: (0, 0)), tok, vec],
        out_shape=[jax.ShapeDtypeStruct((SUBLANES, LANES), F32), jax.ShapeDtypeStruct((N, D), F32),
                   jax.ShapeDtypeStruct((1, D), F32)],
        compiler_params=_params("arbitrary"),
    )(h, g, target)


def _swap_halves(x):
    half = x.shape[-1] // 2
    return jnp.concatenate([x[:, half:], x[:, :half]], axis=1)


def _gelu(y):
    return jax.nn.gelu(y)


def _gelu_grad(y):
    c0 = math.sqrt(2.0 / math.pi)
    inner = c0 * (y + 0.044715 * y * y * y)
    t = jnp.tanh(inner)
    return 0.5 * (1.0 + t) + 0.5 * y * (1.0 - t * t) * c0 * (1.0 + 3.0 * 0.044715 * y * y)


def _s5_discretize(lam_re, lam_im, log_dt, b_re, b_im, c_re, c_im):
    G = lam_re.shape[0]
    nch = G // CHUNK_GROUPS
    dt = jnp.exp(log_dt)[:, None]
    er = jnp.exp(lam_re * dt)
    a_re = er * jnp.cos(lam_im * dt)
    a_im = er * jnp.sin(lam_im * dt)
    den = lam_re * lam_re + lam_im * lam_im
    n_re, n_im = a_re - 1.0, a_im
    f_re = (n_re * lam_re + n_im * lam_im) / den
    f_im = (n_im * lam_re - n_re * lam_im) / den
    bb_re = f_re[..., None] * b_re - f_im[..., None] * b_im
    bb_im = f_re[..., None] * b_im + f_im[..., None] * b_re
    eye = jnp.eye(CHUNK_GROUPS, dtype=F32)

    def pack_b(bb):
        bb = bb.reshape(nch, CHUNK_GROUPS, SSM_STATE, SSM_GROUP)
        return jnp.einsum("jgpc,gh->jgchp", bb, eye).reshape(nch, LANES, CHUNK_STATE)

    def pack_c(cc):
        cc = cc.reshape(nch, CHUNK_GROUPS, SSM_GROUP, SSM_STATE)
        return jnp.einsum("jgcp,gh->jgphc", cc, eye).reshape(nch, CHUNK_STATE, LANES)

    bd = jnp.concatenate([pack_b(bb_re), pack_b(bb_im)], axis=2)
    cd = jnp.concatenate([pack_c(c_re), pack_c(-c_im)], axis=1)
    return bd, cd, a_re, a_im


S5_TILE = 1024
S5_SEG = S5_TILE // SUBLANES
S5_UNROLL = 8


def _s5_scan_coefs(lam_re, lam_im, log_dt, seg):
    G = lam_re.shape[0]
    nch = G // CHUNK_GROUPS
    dt = jnp.exp(log_dt)[:, None]
    rate = (lam_re * dt).reshape(nch, 1, CHUNK_STATE)
    freq = (lam_im * dt).reshape(nch, 1, CHUNK_STATE)

    def powers(ks):
        k = jnp.asarray(ks, F32)[None, :, None]
        er = jnp.exp(k * rate)
        re, im = er * jnp.cos(k * freq), er * jnp.sin(k * freq)
        return jnp.concatenate([re, re], axis=2), jnp.concatenate([-im, im], axis=2)

    pw = jnp.stack(powers(range(1, seg + 1)), axis=1)
    steps = (1, 2, 4)
    re, im = powers([s * seg for s in steps])
    row = jnp.arange(SUBLANES, dtype=jnp.int32)[None, None, :, None]
    shift = jnp.asarray(steps, jnp.int32)[None, :, None, None]

    def table(reverse):
        mask = (row < SUBLANES - shift) if reverse else (row >= shift)
        pair = jnp.stack([jnp.where(mask, re[:, :, None, :], 0.0),
                          jnp.where(mask, (-im if reverse else im)[:, :, None, :], 0.0)], axis=2)
        return pair.reshape(nch, 2 * len(steps), SUBLANES, 2 * CHUNK_STATE)

    return pw, table(False), table(True)


def _to_segments(dst_s, src_ref, seg):
    for j in range(SUBLANES):
        dst_s[pl.ds(j, seg, stride=SUBLANES), :] = src_ref[pl.ds(j * seg, seg), :].astype(F32)


def _from_segments(dst_ref, src_s, seg):
    for j in range(SUBLANES):
        dst_ref[pl.ds(j * seg, seg), :] = src_s[pl.ds(j, seg, stride=SUBLANES), :].astype(dst_ref.dtype)


def _seg_scan(x_ref, pw_ref, seg_ref, carry_ref, c_ref, seg, reverse):
    W = x_ref.shape[-1]
    tm = x_ref.shape[0]
    sgn = -1.0 if reverse else 1.0
    ar = jnp.broadcast_to(pw_ref[0, 0:1, :], (SUBLANES, W))
    ai = sgn * jnp.broadcast_to(pw_ref[1, 0:1, :], (SUBLANES, W))

    def rows(i):
        return pl.ds(pl.multiple_of(i * SUBLANES, SUBLANES), SUBLANES)

    def step(t, prev):
        i = (seg - 2 - t) if reverse else (t + 1)
        x = x_ref[rows(i), :] + ar * prev + ai * _swap_halves(prev)
        x_ref[rows(i), :] = x
        return x

    start = (seg - 1) * SUBLANES if reverse else 0
    edge = lax.fori_loop(0, seg - 1, step, x_ref[start:start + SUBLANES, :], unroll=S5_UNROLL)
    row = lax.broadcasted_iota(jnp.int32, (SUBLANES, W), 0)
    if reverse:
        f = jnp.where(row == SUBLANES - 1, carry_ref[...], pltpu.roll(edge, SUBLANES - 1, 0))
    else:
        f = jnp.where(row == 0, carry_ref[...], pltpu.roll(edge, 1, 0))
    for si, s in enumerate((1, 2, 4)):
        fs = pltpu.roll(f, (SUBLANES - s) if reverse else s, 0)
        f = f + seg_ref[2 * si] * fs + seg_ref[2 * si + 1] * _swap_halves(fs)
    c_ref[...] = f
    fsw = _swap_halves(f)

    def fix(i, _):
        k = (seg - 1 - i) if reverse else i
        x_ref[rows(i), :] = x_ref[rows(i), :] + pw_ref[0, pl.ds(k, 1), :] * f + (sgn * pw_ref[1, pl.ds(k, 1), :]) * fsw
        return 0

    lax.fori_loop(0, seg, fix, 0, unroll=S5_UNROLL)
    leaving = x_ref[0:1, :] if reverse else x_ref[tm - 1:tm, :]
    carry_ref[...] = jnp.broadcast_to(leaving, carry_ref.shape)


def _s5_fwd(u, bd, cd, pw, seg_f, d_skip, B, carry=None):
    N, D = u.shape
    S = N // B
    nch = D // LANES
    W = 2 * CHUNK_STATE
    tm, seg = S5_TILE, S5_SEG
    nt = S // tm

    def body(u_ref, bd_ref, cd_ref, pw_ref, seg_ref, d_ref, z_ref, cin_ref, x_s, carry, c_s, u_s, z_s):
        t = pl.program_id(2)

        @pl.when(t == 0)
        def _():
            carry[...] = jnp.zeros_like(carry)

        cin_ref[...] = carry[...]
        _to_segments(u_s, u_ref, seg)
        uf = u_s[...]
        x_s[...] = jnp.dot(uf.astype(BF16), bd_ref[...], preferred_element_type=F32)
        _seg_scan(x_s, pw_ref, seg_ref, carry, c_s, seg, False)
        y = jnp.dot(x_s[...].astype(BF16), cd_ref[...], preferred_element_type=F32) + d_ref[...] * uf
        z_s[...] = _gelu(y)
        _from_segments(z_ref, z_s, seg)

    (z, carries), moved = _carried_call(
        body, name="s5_fwd", grid=(nch, B, nt),
        in_specs=[pl.BlockSpec((tm, LANES), lambda j, b, t: (b * nt + t, j)),
                  pl.BlockSpec((None, LANES, W), lambda j, b, t: (j, 0, 0)),
                  pl.BlockSpec((None, W, LANES), lambda j, b, t: (j, 0, 0)),
                  pl.BlockSpec((None, 2, seg, W), lambda j, b, t: (j, 0, 0, 0)),
                  pl.BlockSpec((None, 6, SUBLANES, W), lambda j, b, t: (j, 0, 0, 0)),
                  pl.BlockSpec((1, LANES), lambda j, b, t: (0, j))],
        out_specs=[pl.BlockSpec((tm, LANES), lambda j, b, t: (b * nt + t, j)),
                   pl.BlockSpec((None, None, SUBLANES, W), lambda j, b, t: (j, b * nt + t, 0, 0))],
        out_shape=[jax.ShapeDtypeStruct((N, D), BF16), jax.ShapeDtypeStruct((nch, B * nt, SUBLANES, W), F32)],
        scratch_shapes=[pltpu.VMEM((tm, W), F32), pltpu.VMEM((SUBLANES, W), F32), pltpu.VMEM((SUBLANES, W), F32),
                        pltpu.VMEM((tm, LANES), F32), pltpu.VMEM((tm, LANES), F32)],
        operands=(u, bd, cd, pw, seg_f, d_skip), sem=("parallel", "arbitrary", "arbitrary"), carry=carry)
    return z, carries, moved


def _s5_bwd(u, dz, bd, cd, pw, seg_f, seg_b, d_skip, carries, B, carry=None):
    N, D = u.shape
    S = N // B
    nch = D // LANES
    W = 2 * CHUNK_STATE
    tm, seg = S5_TILE, S5_SEG
    nt = S // tm
    tn_dims = (((0,), (0,)), ((), ()))
    nt_dims = (((1,), (1,)), ((), ()))

    def body(u_ref, dz_ref, bd_ref, cd_ref, pw_ref, sf_ref, sb_ref, d_ref, cin_ref,
             du_ref, dbd_ref, dcd_ref, da_ref, dd_ref, x_s, l_s, carry, lcarry, c_s, lc_s, u_s, t_s):
        b = pl.program_id(1)
        t = pl.program_id(2)

        @pl.when((b == 0) & (t == 0))
        def _():
            dbd_ref[...] = jnp.zeros_like(dbd_ref)
            dcd_ref[...] = jnp.zeros_like(dcd_ref)
            da_ref[...] = jnp.zeros_like(da_ref)
            dd_ref[...] = jnp.zeros_like(dd_ref)

        @pl.when(t == 0)
        def _():
            lcarry[...] = jnp.zeros_like(lcarry)

        _to_segments(u_s, u_ref, seg)
        _to_segments(t_s, dz_ref, seg)
        uf = u_s[...]
        uv = uf.astype(BF16)
        carry[...] = cin_ref[...]
        x_s[...] = jnp.dot(uv, bd_ref[...], preferred_element_type=F32)
        _seg_scan(x_s, pw_ref, sf_ref, carry, c_s, seg, False)
        xb = x_s[...].astype(BF16)
        y = jnp.dot(xb, cd_ref[...], preferred_element_type=F32) + d_ref[...] * uf
        dy = t_s[...] * _gelu_grad(y)
        dd_ref[...] += jnp.sum(dy * uf, axis=0, keepdims=True)
        dyb = dy.astype(BF16)
        dcd_ref[...] += lax.dot_general(dyb, xb, tn_dims, preferred_element_type=F32)
        l_s[...] = lax.dot_general(dyb, cd_ref[...], nt_dims, preferred_element_type=F32)
        _seg_scan(l_s, pw_ref, sb_ref, lcarry, lc_s, seg, True)
        lb = l_s[...].astype(BF16)
        dbd_ref[...] += lax.dot_general(uv, lb, tn_dims, preferred_element_type=F32)
        t_s[...] = lax.dot_general(lb, bd_ref[...], nt_dims, preferred_element_type=F32) + d_ref[...] * dy
        _from_segments(du_ref, t_s, seg)
        lam_rest, x_prev = l_s[SUBLANES:, :], x_s[:tm - SUBLANES, :]
        lam_0, c_in = l_s[:SUBLANES, :], c_s[...]
        da_ref[0:1, :] += (jnp.sum(lam_rest * x_prev, axis=0, keepdims=True) + jnp.sum(lam_0 * c_in, axis=0, keepdims=True))
        da_ref[1:2, :] += (jnp.sum(lam_rest * _swap_halves(x_prev), axis=0, keepdims=True)
                           + jnp.sum(lam_0 * _swap_halves(c_in), axis=0, keepdims=True))

    tile = lambda j, b, t: (b * nt + (nt - 1 - t), j)
    chunk3 = lambda j, b, t: (j, 0, 0)
    chunk4 = lambda j, b, t: (j, 0, 0, 0)
    outs, moved = _carried_call(
        body, name="s5_bwd", grid=(nch, B, nt),
        in_specs=[pl.BlockSpec((tm, LANES), tile), pl.BlockSpec((tm, LANES), tile),
                  pl.BlockSpec((None, LANES, W), chunk3), pl.BlockSpec((None, W, LANES), chunk3),
                  pl.BlockSpec((None, 2, seg, W), chunk4), pl.BlockSpec((None, 6, SUBLANES, W), chunk4),
                  pl.BlockSpec((None, 6, SUBLANES, W), chunk4), pl.BlockSpec((1, LANES), lambda j, b, t: (0, j)),
                  pl.BlockSpec((None, None, SUBLANES, W), lambda j, b, t: (j, b * nt + (nt - 1 - t), 0, 0))],
        out_specs=[pl.BlockSpec((tm, LANES), tile), pl.BlockSpec((None, LANES, W), chunk3),
                   pl.BlockSpec((None, LANES, W), chunk3), pl.BlockSpec((None, 2, W), chunk3),
                   pl.BlockSpec((1, LANES), lambda j, b, t: (0, j))],
        out_shape=[jax.ShapeDtypeStruct((N, D), F32), jax.ShapeDtypeStruct((nch, LANES, W), F32),
                   jax.ShapeDtypeStruct((nch, LANES, W), F32), jax.ShapeDtypeStruct((nch, 2, W), F32),
                   jax.ShapeDtypeStruct((1, D), F32)],
        scratch_shapes=[pltpu.VMEM((tm, W), F32), pltpu.VMEM((tm, W), F32)] + [pltpu.VMEM((SUBLANES, W), F32)] * 4
        + [pltpu.VMEM((tm, LANES), F32)] * 2,
        operands=(u, dz, bd, cd, pw, seg_f, seg_b, d_skip, carries), sem=("parallel", "arbitrary", "arbitrary"), carry=carry)
    return (*outs, moved)


ATTN_HEADS = LANES // HEAD_DIM
ATTN_FWD_UNROLL = 8
ATTN_BWD_UNROLL = 8


def _attn_mask(n):
    qi = lax.broadcasted_iota(jnp.int32, (ATTN_HEADS * ATTN_BLOCK, 2 * ATTN_BLOCK), 0) % ATTN_BLOCK
    kj = lax.broadcasted_iota(jnp.int32, (ATTN_HEADS * ATTN_BLOCK, 2 * ATTN_BLOCK), 1)
    prev_ok = (kj < ATTN_BLOCK) & (kj >= qi) & (n > 0)
    return prev_ok | ((kj >= ATTN_BLOCK) & (kj - ATTN_BLOCK <= qi))


def _stack_heads(x):
    return jnp.concatenate([_only_head(x, h) for h in range(ATTN_HEADS)], axis=0)


def _stack_head_columns(x):
    return jnp.concatenate([x[:, h * HEAD_DIM:h * HEAD_DIM + 1] for h in range(ATTN_HEADS)], axis=0)


def _unstack_heads(x):
    return _per_head([x[h * ATTN_BLOCK:(h + 1) * ATTN_BLOCK] for h in range(ATTN_HEADS)])


def _head_lanes(h):
    lane = lax.broadcasted_iota(jnp.int32, (ATTN_BLOCK, LANES), 1)
    return (lane >= h * HEAD_DIM) & (lane < (h + 1) * HEAD_DIM)


def _per_head(cols):
    out = jnp.broadcast_to(cols[-1], (ATTN_BLOCK, LANES))
    for h in range(len(cols) - 2, -1, -1):
        out = jnp.where(_head_lanes(h), jnp.broadcast_to(cols[h], (ATTN_BLOCK, LANES)), out)
    return out


def _only_head(x, h):
    return jnp.where(_head_lanes(h), x, 0.0).astype(BF16)


def _block_rows(tb, dil, nb):
    r = tb // nb
    n = tb % nb
    start = r + dil * ATTN_BLOCK * n
    startp = jnp.where(n > 0, start - dil * ATTN_BLOCK, start)
    return n, pl.ds(start, ATTN_BLOCK, stride=dil), pl.ds(startp, ATTN_BLOCK, stride=dil)


def _attn_fwd(q, k, v, B, carry=None):
    _, S, D3 = q.shape
    D = D3 // 3
    HP = D // LANES
    scale = HEAD_DIM ** -0.5
    n_blocks = S // ATTN_BLOCK
    nbr = len(DILATIONS)
    nt_dims = (((1,), (1,)), ((), ()))

    def branch(dil, q_ref, k_ref, v_ref, acc, m_s, l_s):
        nb = (S // dil) // ATTN_BLOCK

        def blk(tb, _):
            n, rows, rowsp = _block_rows(tb, dil, nb)
            qb = q_ref[rows, :] * scale
            kk = jnp.concatenate([k_ref[rowsp, :], k_ref[rows, :]], axis=0).astype(BF16)
            vv = jnp.concatenate([v_ref[rowsp, :], v_ref[rows, :]], axis=0).astype(BF16)
            s = lax.dot_general(_stack_heads(qb), kk, nt_dims, preferred_element_type=F32)
            s = jnp.where(_attn_mask(n), s, NEG)
            m = jnp.max(s, axis=-1, keepdims=True)
            p = jnp.exp(s - m)
            m_s[rows, :] = _unstack_heads(m)
            l_s[rows, :] = _unstack_heads(jnp.sum(p, axis=-1, keepdims=True))
            acc[rows, :] = _unstack_heads(jnp.dot(p.astype(BF16), vv, preferred_element_type=F32))
            return 0

        lax.fori_loop(0, n_blocks, blk, 0, unroll=ATTN_FWD_UNROLL)

    def body(q_ref, k_ref, v_ref, o_ref, lse_ref, *scratch):
        accs, m_ss, l_ss = scratch[:nbr], scratch[nbr:2 * nbr], scratch[2 * nbr:]
        g = pl.program_id(2)
        for gi, dil in enumerate(DILATIONS):
            pl.when(g == gi)(functools.partial(branch, dil, q_ref, k_ref, v_ref, accs[gi], m_ss[gi], l_ss[gi]))

        @pl.when(g == nbr - 1)
        def _():
            def fin(i, _):
                rows = pl.ds(pl.multiple_of(i * ATTN_BLOCK, ATTN_BLOCK), ATTN_BLOCK)
                ms = [m[rows, :] for m in m_ss]
                m_all = functools.reduce(jnp.maximum, ms)
                ws = [jnp.exp(m - m_all) for m in ms]
                den = sum(w * l[rows, :] for w, l in zip(ws, l_ss))
                o_ref[rows, :] = sum(w * a[rows, :] for w, a in zip(ws, accs)) / den
                lse_ref[rows, :] = m_all + jnp.log(den)
                return 0

            lax.fori_loop(0, n_blocks, fin, 0)

    br = pl.BlockSpec((None, S, LANES), lambda b, hp, g: (b, 0, g * HP + hp))
    hd = pl.BlockSpec((None, S, LANES), lambda b, hp, g: (b, 0, hp))
    (o, lse), moved = _carried_call(
        body, name="attn_fwd", grid=(B, HP, nbr), in_specs=[br, br, br], out_specs=[hd, hd],
        out_shape=[jax.ShapeDtypeStruct((B, S, D), F32), jax.ShapeDtypeStruct((B, S, D), F32)],
        scratch_shapes=[pltpu.VMEM((S, LANES), F32)] * (3 * nbr),
        operands=(q, k, v), sem=("parallel", "parallel", "arbitrary"), carry=carry)
    return o, lse, moved


def _attn_bwd(q, k, v, o, lse, do, dk_prev, dv_prev, B, last, carry=None):
    _, S, D3 = q.shape
    D = D3 // 3
    HP = D // LANES
    scale = HEAD_DIM ** -0.5
    n_blocks = S // ATTN_BLOCK
    has_prev = dk_prev is not None
    nt_dims = (((1,), (1,)), ((), ()))
    tn_dims = (((0,), (0,)), ((), ()))

    def branch(dil, q_ref, k_ref, v_ref, lse_ref, do_ref, dq_s, dk_c, dv_c, delta, dk_p, dv_p):
        nb = (S // dil) // ATTN_BLOCK

        def blk(tb, _):
            n, rows, rowsp = _block_rows(tb, dil, nb)
            qb = q_ref[rows, :] * scale
            dob, lb, db = do_ref[rows, :], lse_ref[rows, :], delta[rows, :]
            kk = jnp.concatenate([k_ref[rowsp, :], k_ref[rows, :]], axis=0).astype(BF16)
            vv = jnp.concatenate([v_ref[rowsp, :], v_ref[rows, :]], axis=0).astype(BF16)
            qs, dos = _stack_heads(qb), _stack_heads(dob)
            s = lax.dot_general(qs, kk, nt_dims, preferred_element_type=F32)
            p = jnp.where(_attn_mask(n), jnp.exp(s - _stack_head_columns(lb)), 0.0)
            dp = lax.dot_general(dos, vv, nt_dims, preferred_element_type=F32)
            ds = (p * (dp - _stack_head_columns(db))).astype(BF16)
            dkk = lax.dot_general(ds, qs, tn_dims, preferred_element_type=F32)
            dvv = lax.dot_general(p.astype(BF16), dos, tn_dims, preferred_element_type=F32)
            dq_s[rows, :] = _unstack_heads(jnp.dot(ds, kk, preferred_element_type=F32)) * scale
            dk_p[rowsp, :] = dkk[:ATTN_BLOCK]
            dv_p[rowsp, :] = dvv[:ATTN_BLOCK]
            dk_c[rows, :] = dkk[ATTN_BLOCK:]
            dv_c[rows, :] = dvv[ATTN_BLOCK:]
            return 0

        lax.fori_loop(0, n_blocks, blk, 0, unroll=ATTN_BWD_UNROLL)

    def body(*refs):
        q_ref, k_ref, v_ref, o_ref, lse_ref, do_ref = refs[:6]
        n_in = 8 if has_prev else 6
        dq_ref, dk_ref, dv_ref, delta, dk_p, dv_p, dq_s, dk_c, dv_c = refs[n_in:n_in + 9]
        g = pl.program_id(2)

        @pl.when(g == 0)
        def _():
            def dl(i, _):
                rows = pl.ds(pl.multiple_of(i * ATTN_BLOCK, ATTN_BLOCK), ATTN_BLOCK)
                prod = do_ref[rows, :] * o_ref[rows, :]
                delta[rows, :] = _per_head([jnp.sum(jnp.where(_head_lanes(h), prod, 0.0), axis=-1, keepdims=True)
                                            for h in range(ATTN_HEADS)])
                return 0

            lax.fori_loop(0, n_blocks, dl, 0)

        dk_p[...] = jnp.zeros_like(dk_p)
        dv_p[...] = jnp.zeros_like(dv_p)
        for gi, dil in enumerate(DILATIONS):
            pl.when(g == gi)(functools.partial(branch, dil, q_ref, k_ref, v_ref, lse_ref, do_ref, dq_s, dk_c, dv_c,
                                               delta, dk_p, dv_p))

        def fin(i, _):
            rows = pl.ds(pl.multiple_of(i * ATTN_BLOCK, ATTN_BLOCK), ATTN_BLOCK)
            dk_t = dk_c[rows, :] + dk_p[rows, :]
            dv_t = dv_c[rows, :] + dv_p[rows, :]
            if has_prev:
                dk_t = dk_t + refs[6][rows, :].astype(F32)
                dv_t = dv_t + refs[7][rows, :].astype(F32)
            dq_ref[rows, :] = dq_s[rows, :].astype(dq_ref.dtype)
            dk_ref[rows, :] = dk_t.astype(dk_ref.dtype)
            dv_ref[rows, :] = dv_t.astype(dv_ref.dtype)
            return 0

        lax.fori_loop(0, n_blocks, fin, 0)

    br = pl.BlockSpec((None, S, LANES), lambda b, hp, g: (b, 0, g * HP + hp))
    hd = pl.BlockSpec((None, S, LANES), lambda b, hp, g: (b, 0, hp))
    ins = [q, k, v, o, lse, do] + ([dk_prev, dv_prev] if has_prev else [])
    kv_dtype = BF16 if last else F32
    (dq, dk, dv), moved = _carried_call(
        body, name="attn_bwd", grid=(B, HP, len(DILATIONS)),
        in_specs=[br, br, br, hd, hd, hd] + ([br, br] if has_prev else []), out_specs=[br, br, br],
        out_shape=[jax.ShapeDtypeStruct(q.shape, BF16), jax.ShapeDtypeStruct(q.shape, kv_dtype),
                   jax.ShapeDtypeStruct(q.shape, kv_dtype)],
        scratch_shapes=[pltpu.VMEM((S, LANES), F32)] * 6,
        operands=ins, sem=("parallel", "parallel", "arbitrary"), carry=carry)
    return dq, dk, dv, moved


def _adamw(w, grads, m, v):
    R, C = w.shape
    tr = _div(R, 256, SUBLANES)
    ng = len(grads)
    c1 = 1.0 - ADAM_B1 ** ADAM_STEP
    c2 = 1.0 - ADAM_B2 ** ADAM_STEP

    def body(*refs):
        w_ref, m_ref, v_ref = refs[0], refs[1 + ng], refs[2 + ng]
        d_ref, mo_ref, vo_ref = refs[3 + ng:6 + ng]
        g = refs[1][...]
        if ng == 2:
            g = g + refs[2][...]
            refs[6 + ng][...] = g
        mn = ADAM_B1 * m_ref[...] + (1.0 - ADAM_B1) * g
        vn = ADAM_B2 * v_ref[...] + (1.0 - ADAM_B2) * (g * g)
        d_ref[...] = -ADAM_LR * ((mn / c1) / (jnp.sqrt(vn / c2) + ADAM_EPS) + ADAM_WD * w_ref[...])
        mo_ref[...] = mn
        vo_ref[...] = vn

    blk = pl.BlockSpec((tr, C), lambda i: (i, 0))
    n_out = 3 + (ng == 2)
    outs = pl.pallas_call(
        body, name="adamw", grid=(R // tr,), in_specs=[blk] * (3 + ng), out_specs=[blk] * n_out,
        out_shape=[jax.ShapeDtypeStruct((R, C), F32)] * n_out, compiler_params=_params("parallel"),
    )(w, *grads, m, v)
    return (outs[3] if ng == 2 else grads[0],) + tuple(outs[:3])


def _sum_shards(recv):
    n, R, C = recv.shape
    tr = _div(R, 256, SUBLANES if recv.dtype == F32 else 2 * SUBLANES)

    def body(r_ref, o_ref):
        s = r_ref[0].astype(F32)
        for i in range(1, n):
            s = s + r_ref[i].astype(F32)
        o_ref[...] = s

    return pl.pallas_call(
        body, name="sum_shards", grid=(R // tr,), in_specs=[pl.BlockSpec((n, tr, C), lambda i: (0, i, 0))],
        out_specs=pl.BlockSpec((tr, C), lambda i: (i, 0)), out_shape=jax.ShapeDtypeStruct((R, C), F32),
        compiler_params=_params("parallel"),
    )(recv)


N_DEV = 8
N_CHIPS = 4


def _all_gather_small(x, carry=None):
    m_per, n = x.shape

    def body(x_ref, out_ref, send_sems, recv_sems, local_sem):
        cx, cy, cc = _coords()
        me, sibling = (cx, cy, cc), (cx, cy, 1 - cc)
        chips = [(1 - cx, cy), (cx, 1 - cy), (1 - cx, 1 - cy)]

        def rows(px, py, pc):
            return out_ref.at[pl.ds((4 * px + 2 * py + pc) * m_per, m_per), :]

        def copy(k, block, to, src=None):
            return pltpu.make_async_remote_copy(
                src_ref=rows(*block) if src is None else src, dst_ref=rows(*block), send_sem=send_sems.at[k],
                recv_sem=recv_sems.at[k], device_id=to, device_id_type=MESH)

        mine = pltpu.make_async_copy(x_ref, rows(*me), local_sem)
        mine.start()
        first = [copy(0, me, sibling, src=x_ref)]
        first += [copy(1 + j, me, (*chip, cc), src=x_ref) for j, chip in enumerate(chips)]
        for cp in first:
            cp.start()
        passed = [copy(4 + j, (*chip, cc), sibling) for j, chip in enumerate(chips)]
        for j, chip in enumerate(chips):
            copy(1 + j, (*chip, cc), me).wait_recv()
            passed[j].start()
        copy(0, sibling, me).wait_recv()
        for j, chip in enumerate(chips):
            copy(4 + j, (*chip, 1 - cc), me).wait_recv()
        for cp in first + passed:
            cp.wait_send()
        mine.wait()

    (out,), moved = _carried_call(
        body, name="all_gather_small", grid=(1,), out_shape=[jax.ShapeDtypeStruct((N_DEV * m_per, n), x.dtype)],
        in_specs=[pl.BlockSpec(memory_space=pltpu.VMEM)], out_specs=[pl.BlockSpec(memory_space=pltpu.VMEM)],
        scratch_shapes=[pltpu.SemaphoreType.DMA((7,)), pltpu.SemaphoreType.DMA((7,)), pltpu.SemaphoreType.DMA],
        operands=(x,), sem=("arbitrary",), carry=carry)
    return out, moved


def _layer_moves(kind, arrays_from, arrays_to, pieces, layer_major=()):
    used = sorted({w for w, _ in pieces})
    pos = {w: i for i, w in enumerate(used)}
    gather = kind == "gather"

    def half(ref, c):
        rows = ref.shape[0] // 2
        return ref.at[pl.ds(c * rows, rows), :]

    def slot(d, w, chip, l):
        return d.at[l, chip] if w in layer_major else d.at[chip, l]

    def plan(src_refs, dst_refs, me):
        cx, cy, cc = me
        mine = 2 * cx + cy
        remote, local = [], []
        for w, l in pieces:
            s, d = src_refs[pos[w]], dst_refs[pos[w]]
            for px, py in _other_chips(cx, cy):
                if gather:
                    remote.append((half(s.at[l], cc), half(slot(d, w, mine, l), cc), (px, py, cc)))
                else:
                    remote.append((s.at[2 * px + py, l], d.at[mine, l], (px, py, cc)))
            local.append((s.at[l], slot(d, w, mine, l)) if gather else (s.at[mine, l], d.at[mine, l]))
        return remote, local

    def onward(src_refs, dst_refs, me):
        cx, cy, cc = me
        moves = []
        for w, l in pieces:
            d = dst_refs[pos[w]]
            for px, py in _other_chips(cx, cy):
                landed = half(slot(d, w, 2 * px + py, l), cc)
                moves.append((landed, landed, (cx, cy, 1 - cc)))
        return moves

    n = 3 * len(pieces)
    carry = _Carry([arrays_from[w] for w in used], [arrays_to[w] for w in used], plan, n, len(pieces),
                   onward if gather else None, n if gather else 0)
    return carry, used


def _swap_with_sibling(sums):
    def plan(src_refs, dst_refs, me):
        cx, cy, cc = me
        return [(s, d, (cx, cy, 1 - cc)) for s, d in zip(src_refs, dst_refs)], []

    return _Carry(sums, [lax.empty(s.shape, s.dtype) for s in sums], plan, len(sums), 0)


def _pack(arrs, width):
    parts, layout, row = [], [], 0
    for a in arrs:
        flat = a.reshape(-1).astype(F32)
        rows = -(-flat.shape[0] // (width * SUBLANES)) * SUBLANES
        parts.append(jnp.pad(flat, (0, rows * width - flat.shape[0])).reshape(rows, width))
        layout.append((row, rows, a.shape))
        row += rows
    pad = -row % (8 * SUBLANES) if row > 8 * SUBLANES else 0
    if pad:
        parts.append(jnp.zeros((pad, width), F32))
    return jnp.concatenate(parts, axis=0), layout, row + pad


def _unpack(buf, layout, idx):
    row, rows, shape = layout[idx]
    size = math.prod(shape)
    return buf[row:row + rows].reshape(-1)[:size].reshape(shape)


def kernel(x, c, ln_g, ada_w, ada_b, ssm_lam_re, ssm_lam_im, ssm_log_dt, ssm_b_re, ssm_b_im, ssm_c_re, ssm_c_im, ssm_d, ssm_w_glu, kv_g, kv_ada_w, kv_ada_b, w_kv, attn_w_q, attn_w_o, mlp_w1, mlp_w2, final_g, loss_target, m_ln_g, m_ada_w, m_ada_b, m_ssm_lam_re, m_ssm_lam_im, m_ssm_log_dt, m_ssm_b_re, m_ssm_b_im, m_ssm_c_re, m_ssm_c_im, m_ssm_d, m_ssm_w_glu, m_kv_g, m_kv_ada_w, m_kv_ada_b, m_w_kv, m_attn_w_q, m_attn_w_o, m_mlp_w1, m_mlp_w2, m_final_g, v_ln_g, v_ada_w, v_ada_b, v_ssm_lam_re, v_ssm_lam_im, v_ssm_log_dt, v_ssm_b_re, v_ssm_b_im, v_ssm_c_re, v_ssm_c_im, v_ssm_d, v_ssm_w_glu, v_kv_g, v_kv_ada_w, v_kv_ada_b, v_w_kv, v_attn_w_q, v_attn_w_o, v_mlp_w1, v_mlp_w2, v_final_g):
    B, S, D = x.shape
    N = B * S
    depth = ln_g.shape[0]
    n_a = ssm_w_glu.shape[0]
    n_b = attn_w_q.shape[0]
    FF = mlp_w1.shape[2] * N_CHIPS
    cx, cy, cc = _coords()
    chip = 2 * cx + cy
    dev = 4 * cx + 2 * cy + cc
    n_ex = N_DEV * B
    ada_cols = ada_w.shape[-1]
    kv_cols = kv_ada_w.shape[-1]

    GLU, KV, Q, O, W1, W2 = range(6)
    shards = [ssm_w_glu.astype(BF16), w_kv.astype(BF16)[None], attn_w_q.astype(BF16), attn_w_o.astype(BF16),
              mlp_w1.astype(BF16), mlp_w2.astype(BF16)]
    row_sharded = (O, W2)
    wg = [lax.empty((s.shape[0], N_CHIPS) + s.shape[1:] if w in row_sharded else (N_CHIPS,) + s.shape, BF16)
          for w, s in enumerate(shards)]

    def whole_rows(w):
        L, _, R, C = wg[w].shape
        return wg[w].reshape(1, L, N_CHIPS * R, C)

    def landed(arrays, used, moved):
        for w, a in zip(used, moved):
            arrays[w] = a

    fetch_with = {}

    def carried_by(kind, l, *pieces):
        fetch_with.setdefault((kind, l), []).extend(pieces)

    assert n_a >= 1 and n_b >= 1, (n_a, n_b)
    carried_by("mixer", 0, *[(GLU, l) for l in range(n_a)], (W1, 0))
    carried_by("glu_proj", 0, (W2, 0))
    carried_by("mlp_up", 0, (Q, 0))
    carried_by("mlp_up", n_a - 1, (O, 0))
    carried_by("mixer", n_a - 1, (KV, 0))
    for l in range(1, depth):
        carried_by("mixer" if l - 1 >= n_a else "mlp_down", l - 1, (W1, l))
        carried_by("mixer", l, (W2, l))
    for j in range(1, n_b):
        carried_by("mixer", n_a + j - 1, (Q, j), (O, j))

    def fetch(kind, l):
        pieces = fetch_with.get((kind, l))
        if not pieces:
            return None, []
        return _layer_moves("gather", shards, wg, pieces, layer_major=row_sharded)

    def mm_carrying(kind, l, *args, **kw):
        carry, used = fetch(kind, l)
        if carry is None:
            return _mm(kind, *args, **kw)
        out, moved = _mm(kind, *args, carry=carry, **kw)
        landed(wg, used, moved)
        return out

    c_pack, c_layout, _ = _pack([c], D)
    c_all_buf, _ = _all_gather_small(c_pack)
    c_rows = c_pack.shape[0]
    c_all = jnp.concatenate([_unpack(c_all_buf[d * c_rows:(d + 1) * c_rows], c_layout, 0) for d in range(N_DEV)], axis=0)
    sc_all = jax.nn.silu(c_all).astype(BF16)
    n_mod = depth * 2
    ada_w8 = ada_w.reshape(n_mod, 1, D, ada_cols)
    ada_b_row = ada_b.reshape(1, n_mod * ada_cols)
    mod_local = _mm("ada_fwd", sc_all, ada_w8, mode="nn", M=n_ex, N=n_mod * ada_cols, K=D, b_lay="cs", b_ns=n_mod,
                    epi=_add, extras=[("n", ada_b_row)])
    kv_ada_b_local = lax.dynamic_slice(kv_ada_b.reshape(N_CHIPS, kv_cols), (chip, 0), (1, kv_cols))
    kvmod_local = _mm("ada_fwd", sc_all, _as4(kv_ada_w), mode="nn", M=n_ex, N=kv_cols, K=D, epi=_add,
                      extras=[("n", kv_ada_b_local)])
    mod_pack, mod_layout, mod_rows = _pack([mod_local, kvmod_local, ln_g, ssm_d], D)
    mod_buf, _ = _all_gather_small(mod_pack)

    def from_chip(j, idx):
        d = 2 * j
        return _unpack(mod_buf[d * mod_rows:(d + 1) * mod_rows], mod_layout, idx)

    my_rows = lambda a: lax.dynamic_slice_in_dim(a, dev * B, B, axis=0)
    mods = jnp.concatenate([my_rows(from_chip(j, 0)).reshape(B, n_mod, ada_cols) for j in range(N_CHIPS)], axis=2)
    kvmod = jnp.concatenate([my_rows(from_chip(j, 1)) for j in range(N_CHIPS)], axis=1)
    ln_g_full = jnp.concatenate([from_chip(j, 2) for j in range(N_CHIPS)], axis=2)
    ssm_d_full = jnp.concatenate([from_chip(j, 3) for j in range(N_CHIPS)], axis=1)

    def mod3(l, s):
        mrow = mods[:, l * 2 + s]
        return [mrow[:, i * D:(i + 1) * D].reshape(B, 1, D) for i in range(3)]

    kv_shift, kv_scale = kvmod[:, :D].reshape(B, 1, D), kvmod[:, D:].reshape(B, 1, D)

    s5_tabs = []
    for l in range(n_a):
        prm = (ssm_lam_re[l], ssm_lam_im[l], ssm_log_dt[l], ssm_b_re[l], ssm_b_im[l], ssm_c_re[l], ssm_c_im[l])
        (bd, cd, _, _), disc_vjp = jax.vjp(_s5_discretize, *prm)
        pw, seg_f, seg_b = _s5_scan_coefs(ssm_lam_re[l], ssm_lam_im[l], ssm_log_dt[l], S5_SEG)
        s5_tabs.append((bd.astype(BF16), cd.astype(BF16), pw, seg_f, seg_b, disc_vjp))

    h = x.reshape(N, D)
    saved = []
    k_all = v_all = None
    shift, scale, gate = mod3(0, 0)
    u = _normmod(h, ln_g_full[0, 0].reshape(1, D), scale, shift, B)
    for l in range(depth):
        sv = {}
        sv["h0"], sv["scale0"], sv["gate0"], sv["u0"] = h, scale, gate, u
        shift1, scale1, gate1 = mod3(l, 1)
        norm1 = [("n", ln_g_full[l, 1].reshape(1, D)), ("ex", scale1), ("ex", shift1)]
        carry, used = fetch("mixer", l)
        if l < n_a:
            bd, cd, pw, seg_f, _, _ = s5_tabs[l]
            z, carries, moved = _s5_fwd(u, bd, cd, pw, seg_f, ssm_d_full[l].reshape(1, D), B, carry)
            landed(wg, used, moved)
            zz = mm_carrying("glu_proj", l, z, wg[GLU], mode="nn", M=N, N=2 * D, K=D, b_lay="cs", b_l=l, b_ns=N_CHIPS)
            y, h, u = _glu_residual_norm(zz, h, gate, ln_g_full[l, 1].reshape(1, D), scale1, shift1, B)
            sv["z"], sv["carries"], sv["zz"] = z, carries, zz
        else:
            j = l - n_a
            q = _mm("q_proj", u, wg[Q], mode="nn", M=N, N=3 * D, K=D, b_lay="cs", b_l=j, b_ns=N_CHIPS)
            q3 = q.reshape(B, S, 3 * D)
            o, lse, moved = _attn_fwd(q3, k_all, v_all, B, carry)
            landed(wg, used, moved)
            o2 = o.reshape(N, D)
            y, h, u = _mm("o_proj", o2, whole_rows(O), mode="nn", M=N, N=D, K=D, b_l=j, tm=1024, out_dtype=(BF16, F32, BF16),
                          epi=_gated_residual_norm, extras=[("mn", h), ("ex", gate)] + norm1, rows_per_ex=S)
            sv["q"], sv["o"], sv["lse"] = q3, o, lse
        sv["y0"] = y
        sv["h1"], sv["scale1"], sv["gate1"], sv["u1"] = h, scale1, gate1, u
        r = mm_carrying("mlp_up", l, u, wg[W1], mode="nn", M=N, N=FF, K=D, b_lay="cs", b_l=l, b_ns=N_CHIPS,
                        out_dtype=BF16, epi=_relu2)
        if l + 1 < depth:
            shift, scale, gate = mod3(l + 1, 0)
            y, h, u = mm_carrying(
                "mlp_down", l, r, whole_rows(W2), mode="nn", M=N, N=D, K=FF, b_l=l, tk=2048, tm=512,
                out_dtype=(BF16, F32, BF16), epi=_gated_residual_norm, rows_per_ex=S,
                extras=[("mn", h), ("ex", gate1), ("n", ln_g_full[l + 1, 0].reshape(1, D)), ("ex", scale), ("ex", shift)])
        else:
            y, h = mm_carrying("mlp_down", l, r, whole_rows(W2), mode="nn", M=N, N=D, K=FF, b_l=l, tk=2048, tm=1024,
                               out_dtype=(BF16, F32), epi=_gated_residual, extras=[("mn", h), ("ex", gate1)], rows_per_ex=S)
        sv["r"], sv["y1"] = r, y
        saved.append(sv)
        if l == n_a - 1:
            h_kv = h
            u_kv = _normmod(h, kv_g.reshape(1, D), kv_scale, kv_shift, B)
            half = N_CHIPS // 2
            k_all = _mm("kv_proj", u_kv, wg[KV], mode="nn", M=N, N=3 * D, K=D, b_lay="cs", b_s0=0, b_ns=half).reshape(B, S, 3 * D)
            v_all = _mm("kv_proj", u_kv, wg[KV], mode="nn", M=N, N=3 * D, K=D, b_lay="cs", b_s0=half, b_ns=half).reshape(B, S, 3 * D)

    loss_buf, dh, d_final_g = _loss_head(h, final_g.reshape(1, D), loss_target.reshape(N, D))
    loss = lax.psum(loss_buf[0, 0], ("x", "y", "c"))

    dg = [lax.empty((N_CHIPS,) + s.shape, BF16) for s in shards]
    recv = [lax.empty((N_CHIPS,) + s.shape, BF16) for s in shards]

    def send(pieces):
        return _layer_moves("scatter", dg, recv, pieces)

    send_with = {l: [(W1, l), (W2, l)] for l in range(depth)}
    for l in range(n_a):
        send_with[l] += [(GLU, l)]
    for j in range(n_b):
        send_with[n_a + j] += [(O, j)]
        send_with[n_a + j - 1] += [(Q, j)]
    send_with[n_a - 1] += [(KV, 0)]
    d_ln_g = [[None, None] for _ in range(depth)]
    d_mods = [[None, None] for _ in range(depth)]
    d_s5 = [None] * n_a
    dk_acc = dv_acc = None
    half = N_CHIPS // 2

    def tn_grad(name, a, d, into, l, Mr, Nc, lay, s0=0, ns=N_CHIPS):
        return _mm(name, a, _as4(d), mode="tn", M=Mr, N=Nc, K=N, b_lay="cs", out_dtype=BF16, out_lay=lay,
                   out4_shape=into.shape, out_into=into, out_l=l, out_s0=s0, out_ns=ns, tk=2048)

    dy, d_gate1 = _residual_bwd(dh, saved[-1]["gate1"], saved[-1]["y1"], B)
    for l in reversed(range(depth)):
        sv = saved[l]
        dg[W2] = tn_grad("mlp_down_dw", sv["r"], dy, dg[W2], l, FF, D, "rs")
        da = _mm("mlp_down_dx", dy, whole_rows(W2), mode="nt", M=N, N=FF, K=D, b_l=l, out_dtype=BF16,
                 epi=_relu2_bwd, extras=[("mn", sv["r"])])
        dg[W1] = tn_grad("mlp_up_dw", sv["u1"], da, dg[W1], l, D, FF, "cs")
        du = _mm("mlp_up_dx", da, wg[W1], mode="nt", M=N, N=D, K=FF, b_lay="cs", b_l=l, b_ns=N_CHIPS)
        dh, dgv, d_scale1, d_shift1, dy, d_gate0 = _normmod_bwd(du, sv["h1"], ln_g_full[l, 1].reshape(1, D), sv["scale1"],
                                                                dh, B, below=(sv["y0"], sv["gate0"]))
        d_ln_g[l][1] = dgv
        d_mods[l][1] = jnp.concatenate([d_shift1, d_scale1, d_gate1], axis=2)
        if l < n_a:
            bd, cd, pw, seg_f, seg_b, disc_vjp = s5_tabs[l]
            dzz = _glu_bwd(dy, sv["zz"])
            dg[GLU] = tn_grad("glu_proj_dw", sv["z"], dzz, dg[GLU], l, D, 2 * D, "cs")
            dz = _mm("glu_proj_dx", dzz, wg[GLU], mode="nt", M=N, N=D, K=2 * D, b_lay="cs", b_l=l, b_ns=N_CHIPS)
            carry, used = send(send_with[l])
            du, d_bd, d_cd, d_a2, d_dskip, moved = _s5_bwd(sv["u0"], dz, bd, cd, pw, seg_f, seg_b,
                                                           ssm_d_full[l].reshape(1, D), sv["carries"], B, carry)
            landed(recv, used, moved)
            d_are = (d_a2[:, 0, :CHUNK_STATE] + d_a2[:, 0, CHUNK_STATE:]).reshape(-1, SSM_STATE)
            d_aim = (d_a2[:, 1, CHUNK_STATE:] - d_a2[:, 1, :CHUNK_STATE]).reshape(-1, SSM_STATE)
            d_s5[l] = disc_vjp((d_bd, jnp.swapaxes(d_cd, 1, 2), d_are, d_aim)) + (d_dskip,)
        else:
            j = l - n_a
            dg[O] = tn_grad("o_proj_dw", sv["o"].reshape(N, D), dy, dg[O], j, D, D, "rs")
            do = _mm("o_proj_dx", dy, whole_rows(O), mode="nt", M=N, N=D, K=D, b_l=j)
            carry, used = send(send_with[l])
            dq, dk_acc, dv_acc, moved = _attn_bwd(sv["q"], k_all, v_all, sv["o"], sv["lse"], do.reshape(B, S, D),
                                                  dk_acc, dv_acc, B, l == n_a, carry)
            landed(recv, used, moved)
            dq2 = dq.reshape(N, 3 * D)
            dg[Q] = tn_grad("q_proj_dw", sv["u0"], dq2, dg[Q], j, D, 3 * D, "cs")
            du = _mm("q_proj_dx", dq2, wg[Q], mode="nt", M=N, N=D, K=3 * D, b_lay="cs", b_l=j, b_ns=N_CHIPS)
        below = (saved[l - 1]["y1"], saved[l - 1]["gate1"]) if l > 0 else None
        dh, dgv, d_scale0, d_shift0, dy, d_gate1 = _normmod_bwd(du, sv["h0"], ln_g_full[l, 0].reshape(1, D), sv["scale0"],
                                                                dh, B, below=None if l == n_a else below)
        d_ln_g[l][0] = dgv
        d_mods[l][0] = jnp.concatenate([d_shift0, d_scale0, d_gate0], axis=2)
        if l == n_a:
            dk2, dv2 = dk_acc.reshape(N, 3 * D), dv_acc.reshape(N, 3 * D)
            dg[KV] = tn_grad("kv_proj_dw", u_kv, dk2, dg[KV], 0, D, 3 * D, "cs", s0=0, ns=half)
            dg[KV] = tn_grad("kv_proj_dw", u_kv, dv2, dg[KV], 0, D, 3 * D, "cs", s0=half, ns=half)
            du_kv = _mm("kv_proj_dx", dk2, wg[KV], mode="nt", M=N, N=D, K=3 * D, b_lay="cs", b_s0=0, b_ns=half)
            du_kv = _mm("kv_proj_dx", dv2, wg[KV], mode="nt", M=N, N=D, K=3 * D, b_lay="cs", b_s0=half, b_ns=half,
                        tm=1024, epi=_add, extras=[("mn", du_kv)])
            dh, d_kv_g, d_kv_scale, d_kv_shift, dy, d_gate1 = _normmod_bwd(du_kv, h_kv, kv_g.reshape(1, D), kv_scale, dh, B,
                                                                           below=below)
    grad_x = dh.reshape(B, S, D)

    own = [_sum_shards(r.reshape(N_CHIPS, -1, r.shape[-1])) for r in recv]

    d_kvmod = jnp.concatenate([d_kv_shift, d_kv_scale], axis=2).reshape(B, 2 * D)
    d_mod_all = jnp.concatenate([d_mods[l][s].reshape(B, 3 * D) for l in range(depth) for s in range(2)], axis=1)
    small = [
        d_mod_all, d_kvmod,
        jnp.stack([jnp.stack([d_ln_g[l][0].reshape(D), d_ln_g[l][1].reshape(D)]) for l in range(depth)]),
        jnp.stack([d_s5[l][0] for l in range(n_a)]), jnp.stack([d_s5[l][1] for l in range(n_a)]),
        jnp.stack([d_s5[l][2] for l in range(n_a)]),
        jnp.stack([d_s5[l][3] for l in range(n_a)]), jnp.stack([d_s5[l][4] for l in range(n_a)]),
        jnp.stack([d_s5[l][5] for l in range(n_a)]), jnp.stack([d_s5[l][6] for l in range(n_a)]),
        jnp.stack([d_s5[l][7].reshape(D) for l in range(n_a)]),
        d_kv_g.reshape(D), d_final_g.reshape(D),
    ]
    small_pack, small_layout, small_rows = _pack(small, D)
    small_buf, other = _all_gather_small(small_pack, _swap_with_sibling(own))
    small_sum = _sum_shards(small_buf.reshape(N_DEV, small_rows, D))
    red = lambda idx: _unpack(small_sum, small_layout, idx)
    per_dev = lambda idx: jnp.concatenate(
        [_unpack(small_buf[d * small_rows:(d + 1) * small_rows], small_layout, idx) for d in range(N_DEV)], axis=0)

    dm_all = per_dev(0).reshape(n_ex, n_mod, 3 * D)
    dm_cols = lax.dynamic_slice_in_dim(dm_all, chip * ada_cols, ada_cols, axis=2).reshape(n_ex, n_mod * ada_cols)
    g_ada_w = _mm("ada_dw", sc_all, _as4(dm_cols), mode="tn", M=D, N=n_mod * ada_cols, K=n_ex, b_lay="cs",
                  out_lay="cs", out4_shape=(n_mod, 1, D, ada_cols), out_ns=n_mod).reshape(ada_w.shape)
    dkvm_all = per_dev(1)
    dkvm_cols = lax.dynamic_slice_in_dim(dkvm_all, chip * kv_cols, kv_cols, axis=1)
    g_kv_ada_w = _mm("ada_dw", sc_all, _as4(dkvm_cols), mode="tn", M=D, N=kv_cols, K=n_ex, b_lay="cs")
    g_ada_b_full = (red(0)[0] + red(0)[1]).reshape(depth, 2, 3 * D) if B == 2 else jnp.sum(red(0), axis=0).reshape(depth, 2, 3 * D)
    g_ada_b = lax.dynamic_slice_in_dim(g_ada_b_full, chip * ada_cols, ada_cols, axis=2)
    g_kv_ada_b = red(1)[0] + red(1)[1] if B == 2 else jnp.sum(red(1), axis=0)
    g_ln_g = lax.dynamic_slice_in_dim(red(2), chip * (D // N_CHIPS), D // N_CHIPS, axis=2)
    g_ssm_d = lax.dynamic_slice_in_dim(red(10), chip * (D // N_CHIPS), D // N_CHIPS, axis=1)
    small_grads = {
        "ln_g": g_ln_g, "ada_b": g_ada_b, "ssm_lam_re": red(3), "ssm_lam_im": red(4), "ssm_log_dt": red(5),
        "ssm_b_re": red(6), "ssm_b_im": red(7), "ssm_c_re": red(8), "ssm_c_im": red(9), "ssm_d": g_ssm_d,
        "kv_g": red(11), "kv_ada_b": g_kv_ada_b, "final_g": red(12),
    }
    small_w = {"ln_g": (ln_g, m_ln_g, v_ln_g), "ada_b": (ada_b, m_ada_b, v_ada_b),
               "ssm_lam_re": (ssm_lam_re, m_ssm_lam_re, v_ssm_lam_re), "ssm_lam_im": (ssm_lam_im, m_ssm_lam_im, v_ssm_lam_im),
               "ssm_log_dt": (ssm_log_dt, m_ssm_log_dt, v_ssm_log_dt), "ssm_b_re": (ssm_b_re, m_ssm_b_re, v_ssm_b_re),
               "ssm_b_im": (ssm_b_im, m_ssm_b_im, v_ssm_b_im), "ssm_c_re": (ssm_c_re, m_ssm_c_re, v_ssm_c_re),
               "ssm_c_im": (ssm_c_im, m_ssm_c_im, v_ssm_c_im), "ssm_d": (ssm_d, m_ssm_d, v_ssm_d),
               "kv_g": (kv_g, m_kv_g, v_kv_g), "kv_ada_b": (kv_ada_b, m_kv_ada_b, v_kv_ada_b),
               "final_g": (final_g, m_final_g, v_final_g)}
    names = list(small_w)
    wp, lay_w, _ = _pack([small_w[n][0] for n in names], D)
    gp, _, _ = _pack([small_grads[n] for n in names], D)
    mp, _, _ = _pack([small_w[n][1] for n in names], D)
    vp, _, _ = _pack([small_w[n][2] for n in names], D)
    _, d_p, m_p, v_p = _adamw(wp, [gp], mp, vp)
    upd = {n: (small_grads[n].reshape(small_w[n][0].shape), _unpack(d_p, lay_w, i), _unpack(m_p, lay_w, i), _unpack(v_p, lay_w, i))
           for i, n in enumerate(names)}

    def big(w, m, v, g_own, g_other=None):
        C = w.shape[-1]
        gs = [g_own.reshape(-1, C)] + ([g_other.reshape(-1, C)] if g_other is not None else [])
        return tuple(t.reshape(w.shape) for t in _adamw(w.reshape(-1, C), gs, m.reshape(-1, C), v.reshape(-1, C)))

    upd["ssm_w_glu"] = big(ssm_w_glu, m_ssm_w_glu, v_ssm_w_glu, own[0], other[0])
    upd["w_kv"] = big(w_kv, m_w_kv, v_w_kv, own[1], other[1])
    upd["attn_w_q"] = big(attn_w_q, m_attn_w_q, v_attn_w_q, own[2], other[2])
    upd["attn_w_o"] = big(attn_w_o, m_attn_w_o, v_attn_w_o, own[3], other[3])
    upd["mlp_w1"] = big(mlp_w1, m_mlp_w1, v_mlp_w1, own[4], other[4])
    upd["mlp_w2"] = big(mlp_w2, m_mlp_w2, v_mlp_w2, own[5], other[5])
    upd["ada_w"] = big(ada_w, m_ada_w, v_ada_w, g_ada_w)
    upd["kv_ada_w"] = big(kv_ada_w, m_kv_ada_w, v_kv_ada_w, g_kv_ada_w)

    order = ["ln_g", "ada_w", "ada_b", "ssm_lam_re", "ssm_lam_im", "ssm_log_dt", "ssm_b_re", "ssm_b_im", "ssm_c_re",
             "ssm_c_im", "ssm_d", "ssm_w_glu", "kv_g", "kv_ada_w", "kv_ada_b", "w_kv", "attn_w_q", "attn_w_o", "mlp_w1",
             "mlp_w2", "final_g"]
    return (loss, grad_x, *[upd[n][0] for n in order], *[upd[n][1] for n in order], *[upd[n][2] for n in order],
            *[upd[n][3] for n in order])
```

```python
import functools
import math

import jax
import jax.numpy as jnp
from jax import lax
from jax.experimental import pallas as pl
from jax.experimental.pallas import tpu as pltpu

F32 = jnp.float32
BF16 = jnp.bfloat16
MESH = pl.DeviceIdType.MESH

EPS = 1e-6
NEG = -1e30
SSM_GROUP = 16
SSM_STATE = 64
HEAD_DIM = 64
ATTN_BLOCK = 128
DILATIONS = (1, 4, 16)
ADAM_LR, ADAM_B1, ADAM_B2, ADAM_EPS, ADAM_WD, ADAM_STEP = 0.001, 0.9, 0.999, 1e-08, 0.01, 10

LANES = 128
SUBLANES = 8
CHUNK_GROUPS = LANES // SSM_GROUP
CHUNK_STATE = CHUNK_GROUPS * SSM_STATE
VMEM_LIMIT = 56 * 1024 * 1024


def _div(dim, pref, mult):
    t = min(pref, dim) // mult * mult
    while t >= mult:
        if dim % t == 0:
            return t
        t -= mult
    return dim


def _params(*sem):
    return pltpu.CompilerParams(dimension_semantics=sem, vmem_limit_bytes=VMEM_LIMIT)


def _coords():
    return lax.axis_index("x"), lax.axis_index("y"), lax.axis_index("c")


def _other_chips(cx, cy):
    return [(1 - cx, cy), (cx, 1 - cy), (1 - cx, 1 - cy)]


class _Carry:
    def __init__(self, srcs, dsts, plan, n_remote, n_local, onward=None, n_onward=0):
        self.srcs, self.dsts, self.plan, self.n_remote, self.n_local = list(srcs), list(dsts), plan, n_remote, n_local
        self.onward, self.n_onward = onward, n_onward


def _carried_call(body, *, name, grid, in_specs, out_specs, out_shape, scratch_shapes, operands, sem, carry=None):
    if carry is None:
        outs = pl.pallas_call(body, name=name, grid=grid, in_specs=in_specs, out_specs=out_specs, out_shape=out_shape,
                              scratch_shapes=scratch_shapes, compiler_params=_params(*sem))(*operands)
        return list(outs), []
    n_in, n_out, n_scr = len(in_specs), len(out_specs), len(scratch_shapes)
    ns, nd = len(carry.srcs), len(carry.dsts)

    def wrapped(*refs):
        base_in, src_refs = refs[:n_in], refs[n_in:n_in + ns]
        o0 = n_in + ns + nd
        base_out, dst_refs = refs[o0:o0 + n_out], refs[o0 + n_out:o0 + n_out + nd]
        s0 = o0 + n_out + nd
        base_scr = refs[s0:s0 + n_scr]
        send_sems, recv_sems, local_sems = refs[s0 + n_scr:]
        pids = [pl.program_id(a) for a in range(len(grid))]
        first = functools.reduce(jnp.logical_and, [p == 0 for p in pids])
        last = functools.reduce(jnp.logical_and, [p == g - 1 for p, g in zip(pids, grid)])

        def remote_copies(moves, k0):
            return [pltpu.make_async_remote_copy(src_ref=s, dst_ref=d, send_sem=send_sems.at[k0 + i], recv_sem=recv_sems.at[k0 + i],
                                                 device_id=peer, device_id_type=MESH) for i, (s, d, peer) in enumerate(moves)]

        def copies():
            remote, local = carry.plan(src_refs, dst_refs, _coords())
            return remote_copies(remote, 0), [pltpu.make_async_copy(s, d, local_sems.at[i]) for i, (s, d) in enumerate(local)]

        @pl.when(first)
        def _():
            remote, local = copies()
            for cp in local + remote:
                cp.start()

        body(*base_in, *base_out, *base_scr)

        @pl.when(last)
        def _():
            remote, local = copies()
            for cp in remote:
                cp.wait_send()
                cp.wait_recv()
            for cp in local:
                cp.wait()
            if carry.onward is not None:
                second = remote_copies(carry.onward(src_refs, dst_refs, _coords()), carry.n_remote)
                for cp in second:
                    cp.start()
                for cp in second:
                    cp.wait_send()
                    cp.wait_recv()

    anyspec = pl.BlockSpec(memory_space=pl.ANY)
    outs = pl.pallas_call(
        wrapped, name=name, grid=grid, in_specs=list(in_specs) + [anyspec] * (ns + nd),
        out_specs=list(out_specs) + [anyspec] * nd,
        out_shape=list(out_shape) + [jax.ShapeDtypeStruct(d.shape, d.dtype) for d in carry.dsts],
        scratch_shapes=list(scratch_shapes) + [pltpu.SemaphoreType.DMA((carry.n_remote + carry.n_onward,)),
                                               pltpu.SemaphoreType.DMA((carry.n_remote + carry.n_onward,)),
                                               pltpu.SemaphoreType.DMA((max(carry.n_local, 1),))],
        input_output_aliases={n_in + ns + i: n_out + i for i in range(nd)},
        compiler_params=_params(*(["arbitrary"] * len(grid))),
    )(*operands, *carry.srcs, *carry.dsts)
    return list(outs[:n_out]), list(outs[n_out:])


def _mm(name, a, b4, *, mode, M, N, K, b_lay="cs", b_l=0, b_s0=0, b_ns=1, out_dtype=F32, out_lay=None, out4_shape=None,
        out_into=None, out_l=0, out_s0=0, out_ns=1, epi=None, extras=(), rows_per_ex=None, tm=2048, tn=1024, tk=1024,
        carry=None):
    _, _, bR, bC = b4.shape
    tm = _div(M, tm, SUBLANES if M % 16 else 16)
    brows, bcols = (N, K) if mode == "nt" else (K, N)
    if b_lay == "cs":
        assert bR == brows and bC * b_ns == bcols, (name, b4.shape, brows, bcols)
    else:
        assert bC == bcols and bR * b_ns == brows, (name, b4.shape, brows, bcols)
    n_lim = N
    k_lim = K
    if mode == "nt":
        if b_lay == "cs":
            k_lim = bC
        else:
            n_lim = bR
    else:
        if b_lay == "cs":
            n_lim = bC
        else:
            k_lim = bR
    if out_lay == "cs":
        oR, oC = out4_shape[2], out4_shape[3]
        assert oR == M and oC * out_ns == N, (name, out4_shape, M, N)
        n_lim = math.gcd(n_lim, oC)
    elif out_lay == "rs":
        oR, oC = out4_shape[2], out4_shape[3]
        assert oC == N and oR * out_ns == M, (name, out4_shape, M, N)
        tm = _div(oR, tm, SUBLANES)
    tn = _div(n_lim, tn, LANES)
    tk = _div(k_lim, tk, LANES if mode != "tn" else SUBLANES)
    if mode == "tn":
        tk = _div(k_lim, tk, 16) if k_lim % 16 == 0 else tk
    nk = K // tk
    grid = (M // tm, N // tn, nk)

    if mode == "tn":
        a_spec = pl.BlockSpec((tk, tm), lambda i, j, k: (k, i))
    else:
        a_spec = pl.BlockSpec((tm, tk), lambda i, j, k: (i, k))

    def b_index(ri, ci, br, bc):
        if b_lay == "cs":
            per = bC // bc
            return (b_s0 + ci // per, b_l, ri, ci % per)
        per = bR // br
        return (b_s0 + ri // per, b_l, ri % per, ci)

    if mode == "nt":
        b_spec = pl.BlockSpec((None, None, tn, tk), lambda i, j, k: b_index(j, k, tn, tk))
    else:
        b_spec = pl.BlockSpec((None, None, tk, tn), lambda i, j, k: b_index(k, j, tk, tn))

    in_specs = [a_spec, b_spec]
    operands = [a, b4]
    for kind, arr in extras:
        if kind == "mn":
            in_specs.append(pl.BlockSpec((tm, tn), lambda i, j, k: (i, j)))
        elif kind == "ex":
            per_ex = rows_per_ex // tm
            in_specs.append(pl.BlockSpec((None, 1, tn), lambda i, j, k: (i // per_ex, 0, j)))
        else:
            in_specs.append(pl.BlockSpec((1, tn), lambda i, j, k: (0, j)))
        operands.append(arr)
    n_extra = len(extras)

    multi = isinstance(out_dtype, tuple)
    n_out = len(out_dtype) if multi else 1
    if out_lay is None:
        out_shape = [jax.ShapeDtypeStruct((M, N), dt) for dt in (out_dtype if multi else (out_dtype,))]
        out_spec = [pl.BlockSpec((tm, tn), lambda i, j, k: (i, j)) for _ in range(n_out)]
    else:
        out_shape = [jax.ShapeDtypeStruct(tuple(out4_shape), out_dtype)]
        if out_lay == "cs":
            per_o = oC // tn
            out_spec = [pl.BlockSpec((None, None, tm, tn), lambda i, j, k: (out_s0 + j // per_o, out_l, i, j % per_o))]
        else:
            per_o = oR // tm
            out_spec = [pl.BlockSpec((None, None, tm, tn), lambda i, j, k: (out_s0 + i // per_o, out_l, i % per_o, j))]
    aliases = {}
    if out_into is not None:
        in_specs.append(pl.BlockSpec(memory_space=pl.ANY))
        operands.append(out_into)
        aliases = {len(operands) - 1: 0}

    dims = {"nn": (((1,), (0,)), ((), ())), "nt": (((1,), (1,)), ((), ())), "tn": (((0,), (0,)), ((), ()))}[mode]

    def body(a_ref, b_ref, *rest):
        extra_refs = rest[:n_extra]
        o_refs = rest[len(rest) - n_out - (nk > 1):len(rest) - (nk > 1)]

        def finish(r):
            if epi is not None:
                r = epi(r, *[e[...] for e in extra_refs])
            for o_ref, val in zip(o_refs, r if multi else (r,)):
                o_ref[...] = val.astype(o_ref.dtype)

        part = lax.dot_general(a_ref[...].astype(BF16), b_ref[...].astype(BF16), dims, preferred_element_type=F32)
        if nk == 1:
            finish(part)
            return
        acc = rest[-1]
        k = pl.program_id(2)

        @pl.when(k == 0)
        def _():
            acc[...] = part

        @pl.when(k != 0)
        def _():
            acc[...] += part

        @pl.when(k == nk - 1)
        def _():
            finish(acc[...])

    scratch = [pltpu.VMEM((tm, tn), F32)] if nk > 1 else []
    if carry is not None:
        assert out_into is None, name
        outs, moved = _carried_call(body, name=name, grid=grid, in_specs=in_specs, out_specs=out_spec, out_shape=out_shape,
                                    scratch_shapes=scratch, operands=operands, sem=("arbitrary",) * 3, carry=carry)
        return (tuple(outs) if multi else outs[0]), moved
    outs = pl.pallas_call(
        body, name=name, grid=grid, in_specs=in_specs, out_specs=out_spec, out_shape=out_shape,
        scratch_shapes=scratch, input_output_aliases=aliases,
        compiler_params=_params("parallel", "parallel", "arbitrary"),
    )(*operands)
    return tuple(outs) if multi else outs[0]


def _as4(w):
    return w.reshape((1, 1) + w.shape)


def _relu2(acc):
    r = jnp.maximum(acc, 0.0)
    return r * r


def _relu2_bwd(acc, r):
    return acc * (2.0 * jnp.sqrt(r.astype(F32)))


def _add(acc, e):
    return acc + e


def _gated_residual(acc, h, gate):
    return acc, h + gate * acc


def _modulated_norm(x, g, scale, shift):
    rstd = lax.rsqrt(jnp.mean(x * x, axis=-1, keepdims=True) + EPS)
    return ((x * rstd) * g) * (1.0 + scale) + shift


def _gated_residual_norm(acc, h, gate, g, scale, shift):
    h_new = h + gate * acc
    return acc, h_new, _modulated_norm(h_new, g, scale, shift)


def _row_tiles(N, B, pref=256):
    S = N // B
    tm = _div(S, pref, SUBLANES)
    return tm, S // tm


def _normmod(h, g, scale, shift, B):
    N, D = h.shape
    tm, per_ex = _row_tiles(N, B)

    def body(h_ref, g_ref, sc_ref, sh_ref, u_ref):
        u_ref[...] = _modulated_norm(h_ref[...], g_ref[...], sc_ref[...], sh_ref[...]).astype(u_ref.dtype)

    tok = pl.BlockSpec((tm, D), lambda i: (i, 0))
    vec = pl.BlockSpec((1, D), lambda i: (0, 0))
    ex = pl.BlockSpec((None, 1, D), lambda i: (i // per_ex, 0, 0))
    return pl.pallas_call(
        body, name="normmod_fwd", grid=(N // tm,), in_specs=[tok, vec, ex, ex], out_specs=tok,
        out_shape=jax.ShapeDtypeStruct((N, D), BF16), compiler_params=_params("parallel"),
    )(h, g, scale, shift)


def _normmod_bwd(du, h, g, scale, dh_in, B, below=None):
    N, D = h.shape
    tm, per_ex = _row_tiles(N, B)
    fused = below is not None

    def body(*refs):
        du_ref, h_ref, g_ref, sc_ref, dhin_ref = refs[:5]
        dh_ref, dg_ref, dsc_ref, dsh_ref = refs[5 + 2 * fused:9 + 2 * fused]
        i = pl.program_id(0)
        x = h_ref[...]
        gv = g_ref[...]
        d_u = du_ref[...].astype(F32)
        rstd = lax.rsqrt(jnp.mean(x * x, axis=-1, keepdims=True) + EPS)
        xn = x * rstd
        dyg = d_u * (1.0 + sc_ref[...])
        dxn = dyg * gv
        dh = dhin_ref[...] + rstd * (dxn - xn * jnp.mean(dxn * xn, axis=-1, keepdims=True))
        dh_ref[...] = dh
        sums = [(dsc_ref, jnp.sum(d_u * (xn * gv), axis=0, keepdims=True)), (dsh_ref, jnp.sum(d_u, axis=0, keepdims=True))]
        if fused:
            y_ref, gt_ref = refs[5:7]
            dy_ref, dgt_ref = refs[9 + 2 * fused:]
            dy_ref[...] = (gt_ref[...] * dh).astype(dy_ref.dtype)
            sums.append((dgt_ref, jnp.sum(dh * y_ref[...], axis=0, keepdims=True)))
        dg_t = jnp.sum(dyg * xn, axis=0, keepdims=True)

        @pl.when(i % per_ex == 0)
        def _():
            for ref, val in sums:
                ref[...] = val

        @pl.when(i % per_ex != 0)
        def _():
            for ref, val in sums:
                ref[...] += val

        @pl.when(i == 0)
        def _():
            dg_ref[...] = dg_t

        @pl.when(i != 0)
        def _():
            dg_ref[...] += dg_t

    tok = pl.BlockSpec((tm, D), lambda i: (i, 0))
    vec = pl.BlockSpec((1, D), lambda i: (0, 0))
    ex = pl.BlockSpec((None, 1, D), lambda i: (i // per_ex, 0, 0))
    per_ex_shape = jax.ShapeDtypeStruct((B, 1, D), F32)
    outs = pl.pallas_call(
        body, name="normmod_bwd", grid=(N // tm,), in_specs=[tok, tok, vec, ex, tok] + ([tok, ex] if fused else []),
        out_specs=[tok, vec, ex, ex] + ([tok, ex] if fused else []),
        out_shape=[jax.ShapeDtypeStruct((N, D), F32), jax.ShapeDtypeStruct((1, D), F32), per_ex_shape, per_ex_shape]
        + ([jax.ShapeDtypeStruct((N, D), BF16), per_ex_shape] if fused else []),
        compiler_params=_params("arbitrary"),
    )(du, h, g, scale, dh_in, *(below if fused else ()))
    return tuple(outs) if fused else tuple(outs) + (None, None)


def _residual_bwd(dh, gate, y, B):
    N, D = dh.shape
    tm, per_ex = _row_tiles(N, B)

    def body(dh_ref, gt_ref, y_ref, dy_ref, dgt_ref):
        i = pl.program_id(0)
        d = dh_ref[...]
        dy_ref[...] = (gt_ref[...] * d).astype(dy_ref.dtype)
        t = jnp.sum(d * y_ref[...], axis=0, keepdims=True)

        @pl.when(i % per_ex == 0)
        def _():
            dgt_ref[...] = t

        @pl.when(i % per_ex != 0)
        def _():
            dgt_ref[...] += t

    tok = pl.BlockSpec((tm, D), lambda i: (i, 0))
    ex = pl.BlockSpec((None, 1, D), lambda i: (i // per_ex, 0, 0))
    return pl.pallas_call(
        body, name="residual_bwd", grid=(N // tm,), in_specs=[tok, ex, tok], out_specs=[tok, ex],
        out_shape=[jax.ShapeDtypeStruct((N, D), BF16), jax.ShapeDtypeStruct((B, 1, D), F32)],
        compiler_params=_params("arbitrary"),
    )(dh, gate, y)


def _glu_residual_norm(zz, h, gate, g, scale, shift, B):
    N, D2 = zz.shape
    D = D2 // 2
    tm, per_ex = _row_tiles(N, B)

    def body(v_ref, g_ref, h_ref, gt_ref, ng_ref, sc_ref, sh_ref, y_ref, o_ref, u_ref):
        y = v_ref[...] * jax.nn.sigmoid(g_ref[...])
        y_ref[...] = y.astype(y_ref.dtype)
        h_new = h_ref[...] + gt_ref[...] * y
        o_ref[...] = h_new
        u_ref[...] = _modulated_norm(h_new, ng_ref[...], sc_ref[...], sh_ref[...]).astype(u_ref.dtype)

    tok = pl.BlockSpec((tm, D), lambda i: (i, 0))
    vec = pl.BlockSpec((1, D), lambda i: (0, 0))
    ex = pl.BlockSpec((None, 1, D), lambda i: (i // per_ex, 0, 0))
    return pl.pallas_call(
        body, name="glu_fwd", grid=(N // tm,),
        in_specs=[tok, pl.BlockSpec((tm, D), lambda i: (i, 1)), tok, ex, vec, ex, ex], out_specs=[tok, tok, tok],
        out_shape=[jax.ShapeDtypeStruct((N, D), BF16), jax.ShapeDtypeStruct((N, D), F32), jax.ShapeDtypeStruct((N, D), BF16)],
        compiler_params=_params("parallel"),
    )(zz, zz, h, gate, g, scale, shift)


def _glu_bwd(dy, zz):
    N, D2 = zz.shape
    D = D2 // 2
    tm = _div(N, 256, SUBLANES)

    def body(dy_ref, v_ref, g_ref, o_ref):
        d = dy_ref[...].astype(F32)
        s = jax.nn.sigmoid(g_ref[...])
        o_ref[...] = jnp.concatenate([d * s, d * v_ref[...] * s * (1.0 - s)], axis=1).astype(o_ref.dtype)

    return pl.pallas_call(
        body, name="glu_bwd", grid=(N // tm,),
        in_specs=[pl.BlockSpec((tm, D), lambda i: (i, 0)), pl.BlockSpec((tm, D), lambda i: (i, 0)),
                  pl.BlockSpec((tm, D), lambda i: (i, 1))],
        out_specs=pl.BlockSpec((tm, D2), lambda i: (i, 0)), out_shape=jax.ShapeDtypeStruct((N, D2), BF16),
        compiler_params=_params("parallel"),
    )(dy, zz, zz)


def _loss_head(h, g, target):
    N, D = h.shape
    tm = _div(N, 256, SUBLANES)

    def body(h_ref, g_ref, t_ref, loss_ref, dh_ref, dg_ref):
        i = pl.program_id(0)
        x = h_ref[...]
        gv = g_ref[...]
        rstd = lax.rsqrt(jnp.mean(x * x, axis=-1, keepdims=True) + EPS)
        xn = x * rstd
        err = xn * gv - t_ref[...]
        part = 0.5 * jnp.sum(jnp.sum(err * err, axis=-1, keepdims=True) / D, axis=0, keepdims=True)
        dy = err / D
        dxn = dy * gv
        dh_ref[...] = rstd * (dxn - xn * jnp.mean(dxn * xn, axis=-1, keepdims=True))
        dg_t = jnp.sum(dy * xn, axis=0, keepdims=True)
        part = jnp.broadcast_to(part, loss_ref.shape)

        @pl.when(i == 0)
        def _():
            loss_ref[...] = part
            dg_ref[...] = dg_t

        @pl.when(i != 0)
        def _():
            loss_ref[...] += part
            dg_ref[...] += dg_t

    tok = pl.BlockSpec((tm, D), lambda i: (i, 0))
    vec = pl.BlockSpec((1, D), lambda i: (0, 0))
    return pl.pallas_call(
        body, name="loss_head", grid=(N // tm,), in_specs=[tok, vec, tok],
        out_specs=[pl.BlockSpec((SUBLANES, LANES), lambda i: (0, 0)), tok, vec],
        out_shape=[jax.ShapeDtypeStruct((SUBLANES, LANES), F32), jax.ShapeDtypeStruct((N, D), F32),
                   jax.ShapeDtypeStruct((1, D), F32)],
        compiler_params=_params("arbitrary"),
    )(h, g, target)


def _swap_halves(x):
    half = x.shape[-1] // 2
    return jnp.concatenate([x[:, half:], x[:, :half]], axis=1)


def _gelu(y):
    return jax.nn.gelu(y)


def _gelu_grad(y):
    c0 = math.sqrt(2.0 / math.pi)
    inner = c0 * (y + 0.044715 * y * y * y)
    t = jnp.tanh(inner)
    return 0.5 * (1.0 + t) + 0.5 * y * (1.0 - t * t) * c0 * (1.0 + 3.0 * 0.044715 * y * y)


def _s5_discretize(lam_re, lam_im, log_dt, b_re, b_im, c_re, c_im):
    G = lam_re.shape[0]
    nch = G // CHUNK_GROUPS
    dt = jnp.exp(log_dt)[:, None]
    er = jnp.exp(lam_re * dt)
    a_re = er * jnp.cos(lam_im * dt)
    a_im = er * jnp.sin(lam_im * dt)
    den = lam_re * lam_re + lam_im * lam_im
    n_re, n_im = a_re - 1.0, a_im
    f_re = (n_re * lam_re + n_im * lam_im) / den
    f_im = (n_im * lam_re - n_re * lam_im) / den
    bb_re = f_re[..., None] * b_re - f_im[..., None] * b_im
    bb_im = f_re[..., None] * b_im + f_im[..., None] * b_re
    eye = jnp.eye(CHUNK_GROUPS, dtype=F32)

    def pack_b(bb):
        bb = bb.reshape(nch, CHUNK_GROUPS, SSM_STATE, SSM_GROUP)
        return jnp.einsum("jgpc,gh->jgchp", bb, eye).reshape(nch, LANES, CHUNK_STATE)

    def pack_c(cc):
        cc = cc.reshape(nch, CHUNK_GROUPS, SSM_GROUP, SSM_STATE)
        return jnp.einsum("jgcp,gh->jgphc", cc, eye).reshape(nch, CHUNK_STATE, LANES)

    bd = jnp.concatenate([pack_b(bb_re), pack_b(bb_im)], axis=2)
    cd = jnp.concatenate([pack_c(c_re), pack_c(-c_im)], axis=1)
    return bd, cd, a_re, a_im


S5_TILE = 1024
S5_SEG = S5_TILE // SUBLANES
S5_UNROLL = 8


def _s5_scan_coefs(lam_re, lam_im, log_dt, seg):
    G = lam_re.shape[0]
    nch = G // CHUNK_GROUPS
    dt = jnp.exp(log_dt)[:, None]
    rate = (lam_re * dt).reshape(nch, 1, CHUNK_STATE)
    freq = (lam_im * dt).reshape(nch, 1, CHUNK_STATE)

    def powers(ks):
        k = jnp.asarray(ks, F32)[None, :, None]
        er = jnp.exp(k * rate)
        re, im = er * jnp.cos(k * freq), er * jnp.sin(k * freq)
        return jnp.concatenate([re, re], axis=2), jnp.concatenate([-im, im], axis=2)

    pw = jnp.stack(powers(range(1, seg + 1)), axis=1)
    steps = (1, 2, 4)
    re, im = powers([s * seg for s in steps])
    row = jnp.arange(SUBLANES, dtype=jnp.int32)[None, None, :, None]
    shift = jnp.asarray(steps, jnp.int32)[None, :, None, None]

    def table(reverse):
        mask = (row < SUBLANES - shift) if reverse else (row >= shift)
        pair = jnp.stack([jnp.where(mask, re[:, :, None, :], 0.0),
                          jnp.where(mask, (-im if reverse else im)[:, :, None, :], 0.0)], axis=2)
        return pair.reshape(nch, 2 * len(steps), SUBLANES, 2 * CHUNK_STATE)

    return pw, table(False), table(True)


def _to_segments(dst_s, src_ref, seg):
    for j in range(SUBLANES):
        dst_s[pl.ds(j, seg, stride=SUBLANES), :] = src_ref[pl.ds(j * seg, seg), :].astype(F32)


def _from_segments(dst_ref, src_s, seg):
    for j in range(SUBLANES):
        dst_ref[pl.ds(j * seg, seg), :] = src_s[pl.ds(j, seg, stride=SUBLANES), :].astype(dst_ref.dtype)


def _seg_scan(x_ref, pw_ref, seg_ref, carry_ref, c_ref, seg, reverse):
    W = x_ref.shape[-1]
    tm = x_ref.shape[0]
    sgn = -1.0 if reverse else 1.0
    ar = jnp.broadcast_to(pw_ref[0, 0:1, :], (SUBLANES, W))
    ai = sgn * jnp.broadcast_to(pw_ref[1, 0:1, :], (SUBLANES, W))

    def rows(i):
        return pl.ds(pl.multiple_of(i * SUBLANES, SUBLANES), SUBLANES)

    def step(t, prev):
        i = (seg - 2 - t) if reverse else (t + 1)
        x = x_ref[rows(i), :] + ar * prev + ai * _swap_halves(prev)
        x_ref[rows(i), :] = x
        return x

    start = (seg - 1) * SUBLANES if reverse else 0
    edge = lax.fori_loop(0, seg - 1, step, x_ref[start:start + SUBLANES, :], unroll=S5_UNROLL)
    row = lax.broadcasted_iota(jnp.int32, (SUBLANES, W), 0)
    if reverse:
        f = jnp.where(row == SUBLANES - 1, carry_ref[...], pltpu.roll(edge, SUBLANES - 1, 0))
    else:
        f = jnp.where(row == 0, carry_ref[...], pltpu.roll(edge, 1, 0))
    for si, s in enumerate((1, 2, 4)):
        fs = pltpu.roll(f, (SUBLANES - s) if reverse else s, 0)
        f = f + seg_ref[2 * si] * fs + seg_ref[2 * si + 1] * _swap_halves(fs)
    c_ref[...] = f
    fsw = _swap_halves(f)

    def fix(i, _):
        k = (seg - 1 - i) if reverse else i
        x_ref[rows(i), :] = x_ref[rows(i), :] + pw_ref[0, pl.ds(k, 1), :] * f + (sgn * pw_ref[1, pl.ds(k, 1), :]) * fsw
        return 0

    lax.fori_loop(0, seg, fix, 0, unroll=S5_UNROLL)
    leaving = x_ref[0:1, :] if reverse else x_ref[tm - 1:tm, :]
    carry_ref[...] = jnp.broadcast_to(leaving, carry_ref.shape)


def _s5_fwd(u, bd, cd, pw, seg_f, d_skip, B, carry=None):
    N, D = u.shape
    S = N // B
    nch = D // LANES
    W = 2 * CHUNK_STATE
    tm, seg = S5_TILE, S5_SEG
    nt = S // tm

    def body(u_ref, bd_ref, cd_ref, pw_ref, seg_ref, d_ref, z_ref, cin_ref, x_s, carry, c_s, u_s, z_s):
        t = pl.program_id(2)

        @pl.when(t == 0)
        def _():
            carry[...] = jnp.zeros_like(carry)

        cin_ref[...] = carry[...]
        _to_segments(u_s, u_ref, seg)
        uf = u_s[...]
        x_s[...] = jnp.dot(uf.astype(BF16), bd_ref[...], preferred_element_type=F32)
        _seg_scan(x_s, pw_ref, seg_ref, carry, c_s, seg, False)
        y = jnp.dot(x_s[...].astype(BF16), cd_ref[...], preferred_element_type=F32) + d_ref[...] * uf
        z_s[...] = _gelu(y)
        _from_segments(z_ref, z_s, seg)

    (z, carries), moved = _carried_call(
        body, name="s5_fwd", grid=(nch, B, nt),
        in_specs=[pl.BlockSpec((tm, LANES), lambda j, b, t: (b * nt + t, j)),
                  pl.BlockSpec((None, LANES, W), lambda j, b, t: (j, 0, 0)),
                  pl.BlockSpec((None, W, LANES), lambda j, b, t: (j, 0, 0)),
                  pl.BlockSpec((None, 2, seg, W), lambda j, b, t: (j, 0, 0, 0)),
                  pl.BlockSpec((None, 6, SUBLANES, W), lambda j, b, t: (j, 0, 0, 0)),
                  pl.BlockSpec((1, LANES), lambda j, b, t: (0, j))],
        out_specs=[pl.BlockSpec((tm, LANES), lambda j, b, t: (b * nt + t, j)),
                   pl.BlockSpec((None, None, SUBLANES, W), lambda j, b, t: (j, b * nt + t, 0, 0))],
        out_shape=[jax.ShapeDtypeStruct((N, D), BF16), jax.ShapeDtypeStruct((nch, B * nt, SUBLANES, W), F32)],
        scratch_shapes=[pltpu.VMEM((tm, W), F32), pltpu.VMEM((SUBLANES, W), F32), pltpu.VMEM((SUBLANES, W), F32),
                        pltpu.VMEM((tm, LANES), F32), pltpu.VMEM((tm, LANES), F32)],
        operands=(u, bd, cd, pw, seg_f, d_skip), sem=("parallel", "arbitrary", "arbitrary"), carry=carry)
    return z, carries, moved


def _s5_bwd(u, dz, bd, cd, pw, seg_f, seg_b, d_skip, carries, B, carry=None):
    N, D = u.shape
    S = N // B
    nch = D // LANES
    W = 2 * CHUNK_STATE
    tm, seg = S5_TILE, S5_SEG
    nt = S // tm
    tn_dims = (((0,), (0,)), ((), ()))
    nt_dims = (((1,), (1,)), ((), ()))

    def body(u_ref, dz_ref, bd_ref, cd_ref, pw_ref, sf_ref, sb_ref, d_ref, cin_ref,
             du_ref, dbd_ref, dcd_ref, da_ref, dd_ref, x_s, l_s, carry, lcarry, c_s, lc_s, u_s, t_s):
        b = pl.program_id(1)
        t = pl.program_id(2)

        @pl.when((b == 0) & (t == 0))
        def _():
            dbd_ref[...] = jnp.zeros_like(dbd_ref)
            dcd_ref[...] = jnp.zeros_like(dcd_ref)
            da_ref[...] = jnp.zeros_like(da_ref)
            dd_ref[...] = jnp.zeros_like(dd_ref)

        @pl.when(t == 0)
        def _():
            lcarry[...] = jnp.zeros_like(lcarry)

        _to_segments(u_s, u_ref, seg)
        _to_segments(t_s, dz_ref, seg)
        uf = u_s[...]
        uv = uf.astype(BF16)
        carry[...] = cin_ref[...]
        x_s[...] = jnp.dot(uv, bd_ref[...], preferred_element_type=F32)
        _seg_scan(x_s, pw_ref, sf_ref, carry, c_s, seg, False)
        xb = x_s[...].astype(BF16)
        y = jnp.dot(xb, cd_ref[...], preferred_element_type=F32) + d_ref[...] * uf
        dy = t_s[...] * _gelu_grad(y)
        dd_ref[...] += jnp.sum(dy * uf, axis=0, keepdims=True)
        dyb = dy.astype(BF16)
        dcd_ref[...] += lax.dot_general(dyb, xb, tn_dims, preferred_element_type=F32)
        l_s[...] = lax.dot_general(dyb, cd_ref[...], nt_dims, preferred_element_type=F32)
        _seg_scan(l_s, pw_ref, sb_ref, lcarry, lc_s, seg, True)
        lb = l_s[...].astype(BF16)
        dbd_ref[...] += lax.dot_general(uv, lb, tn_dims, preferred_element_type=F32)
        t_s[...] = lax.dot_general(lb, bd_ref[...], nt_dims, preferred_element_type=F32) + d_ref[...] * dy
        _from_segments(du_ref, t_s, seg)
        lam_rest, x_prev = l_s[SUBLANES:, :], x_s[:tm - SUBLANES, :]
        lam_0, c_in = l_s[:SUBLANES, :], c_s[...]
        da_ref[0:1, :] += (jnp.sum(lam_rest * x_prev, axis=0, keepdims=True) + jnp.sum(lam_0 * c_in, axis=0, keepdims=True))
        da_ref[1:2, :] += (jnp.sum(lam_rest * _swap_halves(x_prev), axis=0, keepdims=True)
                           + jnp.sum(lam_0 * _swap_halves(c_in), axis=0, keepdims=True))

    tile = lambda j, b, t: (b * nt + (nt - 1 - t), j)
    chunk3 = lambda j, b, t: (j, 0, 0)
    chunk4 = lambda j, b, t: (j, 0, 0, 0)
    outs, moved = _carried_call(
        body, name="s5_bwd", grid=(nch, B, nt),
        in_specs=[pl.BlockSpec((tm, LANES), tile), pl.BlockSpec((tm, LANES), tile),
                  pl.BlockSpec((None, LANES, W), chunk3), pl.BlockSpec((None, W, LANES), chunk3),
                  pl.BlockSpec((None, 2, seg, W), chunk4), pl.BlockSpec((None, 6, SUBLANES, W), chunk4),
                  pl.BlockSpec((None, 6, SUBLANES, W), chunk4), pl.BlockSpec((1, LANES), lambda j, b, t: (0, j)),
                  pl.BlockSpec((None, None, SUBLANES, W), lambda j, b, t: (j, b * nt + (nt - 1 - t), 0, 0))],
        out_specs=[pl.BlockSpec((tm, LANES), tile), pl.BlockSpec((None, LANES, W), chunk3),
                   pl.BlockSpec((None, LANES, W), chunk3), pl.BlockSpec((None, 2, W), chunk3),
                   pl.BlockSpec((1, LANES), lambda j, b, t: (0, j))],
        out_shape=[jax.ShapeDtypeStruct((N, D), F32), jax.ShapeDtypeStruct((nch, LANES, W), F32),
                   jax.ShapeDtypeStruct((nch, LANES, W), F32), jax.ShapeDtypeStruct((nch, 2, W), F32),
                   jax.ShapeDtypeStruct((1, D), F32)],
        scratch_shapes=[pltpu.VMEM((tm, W), F32), pltpu.VMEM((tm, W), F32)] + [pltpu.VMEM((SUBLANES, W), F32)] * 4
        + [pltpu.VMEM((tm, LANES), F32)] * 2,
        operands=(u, dz, bd, cd, pw, seg_f, seg_b, d_skip, carries), sem=("parallel", "arbitrary", "arbitrary"), carry=carry)
    return (*outs, moved)


ATTN_HEADS = LANES // HEAD_DIM
ATTN_FWD_UNROLL = 8
ATTN_BWD_UNROLL = 8


def _attn_mask(n):
    qi = lax.broadcasted_iota(jnp.int32, (ATTN_HEADS * ATTN_BLOCK, 2 * ATTN_BLOCK), 0) % ATTN_BLOCK
    kj = lax.broadcasted_iota(jnp.int32, (ATTN_HEADS * ATTN_BLOCK, 2 * ATTN_BLOCK), 1)
    prev_ok = (kj < ATTN_BLOCK) & (kj >= qi) & (n > 0)
    return prev_ok | ((kj >= ATTN_BLOCK) & (kj - ATTN_BLOCK <= qi))


def _stack_heads(x):
    return jnp.concatenate([_only_head(x, h) for h in range(ATTN_HEADS)], axis=0)


def _stack_head_columns(x):
    return jnp.concatenate([x[:, h * HEAD_DIM:h * HEAD_DIM + 1] for h in range(ATTN_HEADS)], axis=0)


def _unstack_heads(x):
    return _per_head([x[h * ATTN_BLOCK:(h + 1) * ATTN_BLOCK] for h in range(ATTN_HEADS)])


def _head_lanes(h):
    lane = lax.broadcasted_iota(jnp.int32, (ATTN_BLOCK, LANES), 1)
    return (lane >= h * HEAD_DIM) & (lane < (h + 1) * HEAD_DIM)


def _per_head(cols):
    out = jnp.broadcast_to(cols[-1], (ATTN_BLOCK, LANES))
    for h in range(len(cols) - 2, -1, -1):
        out = jnp.where(_head_lanes(h), jnp.broadcast_to(cols[h], (ATTN_BLOCK, LANES)), out)
    return out


def _only_head(x, h):
    return jnp.where(_head_lanes(h), x, 0.0).astype(BF16)


def _block_rows(tb, dil, nb):
    r = tb // nb
    n = tb % nb
    start = r + dil * ATTN_BLOCK * n
    startp = jnp.where(n > 0, start - dil * ATTN_BLOCK, start)
    return n, pl.ds(start, ATTN_BLOCK, stride=dil), pl.ds(startp, ATTN_BLOCK, stride=dil)


def _attn_fwd(q, k, v, B, carry=None):
    _, S, D3 = q.shape
    D = D3 // 3
    HP = D // LANES
    scale = HEAD_DIM ** -0.5
    n_blocks = S // ATTN_BLOCK
    nbr = len(DILATIONS)
    nt_dims = (((1,), (1,)), ((), ()))

    def branch(dil, q_ref, k_ref, v_ref, acc, m_s, l_s):
        nb = (S // dil) // ATTN_BLOCK

        def blk(tb, _):
            n, rows, rowsp = _block_rows(tb, dil, nb)
            qb = q_ref[rows, :] * scale
            kk = jnp.concatenate([k_ref[rowsp, :], k_ref[rows, :]], axis=0).astype(BF16)
            vv = jnp.concatenate([v_ref[rowsp, :], v_ref[rows, :]], axis=0).astype(BF16)
            s = lax.dot_general(_stack_heads(qb), kk, nt_dims, preferred_element_type=F32)
            s = jnp.where(_attn_mask(n), s, NEG)
            m = jnp.max(s, axis=-1, keepdims=True)
            p = jnp.exp(s - m)
            m_s[rows, :] = _unstack_heads(m)
            l_s[rows, :] = _unstack_heads(jnp.sum(p, axis=-1, keepdims=True))
            acc[rows, :] = _unstack_heads(jnp.dot(p.astype(BF16), vv, preferred_element_type=F32))
            return 0

        lax.fori_loop(0, n_blocks, blk, 0, unroll=ATTN_FWD_UNROLL)

    def body(q_ref, k_ref, v_ref, o_ref, lse_ref, *scratch):
        accs, m_ss, l_ss = scratch[:nbr], scratch[nbr:2 * nbr], scratch[2 * nbr:]
        g = pl.program_id(2)
        for gi, dil in enumerate(DILATIONS):
            pl.when(g == gi)(functools.partial(branch, dil, q_ref, k_ref, v_ref, accs[gi], m_ss[gi], l_ss[gi]))

        @pl.when(g == nbr - 1)
        def _():
            def fin(i, _):
                rows = pl.ds(pl.multiple_of(i * ATTN_BLOCK, ATTN_BLOCK), ATTN_BLOCK)
                ms = [m[rows, :] for m in m_ss]
                m_all = functools.reduce(jnp.maximum, ms)
                ws = [jnp.exp(m - m_all) for m in ms]
                den = sum(w * l[rows, :] for w, l in zip(ws, l_ss))
                o_ref[rows, :] = sum(w * a[rows, :] for w, a in zip(ws, accs)) / den
                lse_ref[rows, :] = m_all + jnp.log(den)
                return 0

            lax.fori_loop(0, n_blocks, fin, 0)

    br = pl.BlockSpec((None, S, LANES), lambda b, hp, g: (b, 0, g * HP + hp))
    hd = pl.BlockSpec((None, S, LANES), lambda b, hp, g: (b, 0, hp))
    (o, lse), moved = _carried_call(
        body, name="attn_fwd", grid=(B, HP, nbr), in_specs=[br, br, br], out_specs=[hd, hd],
        out_shape=[jax.ShapeDtypeStruct((B, S, D), F32), jax.ShapeDtypeStruct((B, S, D), F32)],
        scratch_shapes=[pltpu.VMEM((S, LANES), F32)] * (3 * nbr),
        operands=(q, k, v), sem=("parallel", "parallel", "arbitrary"), carry=carry)
    return o, lse, moved


def _attn_bwd(q, k, v, o, lse, do, dk_prev, dv_prev, B, last, carry=None):
    _, S, D3 = q.shape
    D = D3 // 3
    HP = D // LANES
    scale = HEAD_DIM ** -0.5
    n_blocks = S // ATTN_BLOCK
    has_prev = dk_prev is not None
    nt_dims = (((1,), (1,)), ((), ()))
    tn_dims = (((0,), (0,)), ((), ()))

    def branch(dil, q_ref, k_ref, v_ref, lse_ref, do_ref, dq_s, dk_c, dv_c, delta, dk_p, dv_p):
        nb = (S // dil) // ATTN_BLOCK

        def blk(tb, _):
            n, rows, rowsp = _block_rows(tb, dil, nb)
            qb = q_ref[rows, :] * scale
            dob, lb, db = do_ref[rows, :], lse_ref[rows, :], delta[rows, :]
            kk = jnp.concatenate([k_ref[rowsp, :], k_ref[rows, :]], axis=0).astype(BF16)
            vv = jnp.concatenate([v_ref[rowsp, :], v_ref[rows, :]], axis=0).astype(BF16)
            qs, dos = _stack_heads(qb), _stack_heads(dob)
            s = lax.dot_general(qs, kk, nt_dims, preferred_element_type=F32)
            p = jnp.where(_attn_mask(n), jnp.exp(s - _stack_head_columns(lb)), 0.0)
            dp = lax.dot_general(dos, vv, nt_dims, preferred_element_type=F32)
            ds = (p * (dp - _stack_head_columns(db))).astype(BF16)
            dkk = lax.dot_general(ds, qs, tn_dims, preferred_element_type=F32)
            dvv = lax.dot_general(p.astype(BF16), dos, tn_dims, preferred_element_type=F32)
            dq_s[rows, :] = _unstack_heads(jnp.dot(ds, kk, preferred_element_type=F32)) * scale
            dk_p[rowsp, :] = dkk[:ATTN_BLOCK]
            dv_p[rowsp, :] = dvv[:ATTN_BLOCK]
            dk_c[rows, :] = dkk[ATTN_BLOCK:]
            dv_c[rows, :] = dvv[ATTN_BLOCK:]
            return 0

        lax.fori_loop(0, n_blocks, blk, 0, unroll=ATTN_BWD_UNROLL)

    def body(*refs):
        q_ref, k_ref, v_ref, o_ref, lse_ref, do_ref = refs[:6]
        n_in = 8 if has_prev else 6
        dq_ref, dk_ref, dv_ref, delta, dk_p, dv_p, dq_s, dk_c, dv_c = refs[n_in:n_in + 9]
        g = pl.program_id(2)

        @pl.when(g == 0)
        def _():
            def dl(i, _):
                rows = pl.ds(pl.multiple_of(i * ATTN_BLOCK, ATTN_BLOCK), ATTN_BLOCK)
                prod = do_ref[rows, :] * o_ref[rows, :]
                delta[rows, :] = _per_head([jnp.sum(jnp.where(_head_lanes(h), prod, 0.0), axis=-1, keepdims=True)
                                            for h in range(ATTN_HEADS)])
                return 0

            lax.fori_loop(0, n_blocks, dl, 0)

        dk_p[...] = jnp.zeros_like(dk_p)
        dv_p[...] = jnp.zeros_like(dv_p)
        for gi, dil in enumerate(DILATIONS):
            pl.when(g == gi)(functools.partial(branch, dil, q_ref, k_ref, v_ref, lse_ref, do_ref, dq_s, dk_c, dv_c,
                                               delta, dk_p, dv_p))

        def fin(i, _):
            rows = pl.ds(pl.multiple_of(i * ATTN_BLOCK, ATTN_BLOCK), ATTN_BLOCK)
            dk_t = dk_c[rows, :] + dk_p[rows, :]
            dv_t = dv_c[rows, :] + dv_p[rows, :]
            if has_prev:
                dk_t = dk_t + refs[6][rows, :].astype(F32)
                dv_t = dv_t + refs[7][rows, :].astype(F32)
            dq_ref[rows, :] = dq_s[rows, :].astype(dq_ref.dtype)
            dk_ref[rows, :] = dk_t.astype(dk_ref.dtype)
            dv_ref[rows, :] = dv_t.astype(dv_ref.dtype)
            return 0

        lax.fori_loop(0, n_blocks, fin, 0)

    br = pl.BlockSpec((None, S, LANES), lambda b, hp, g: (b, 0, g * HP + hp))
    hd = pl.BlockSpec((None, S, LANES), lambda b, hp, g: (b, 0, hp))
    ins = [q, k, v, o, lse, do] + ([dk_prev, dv_prev] if has_prev else [])
    kv_dtype = BF16 if last else F32
    (dq, dk, dv), moved = _carried_call(
        body, name="attn_bwd", grid=(B, HP, len(DILATIONS)),
        in_specs=[br, br, br, hd, hd, hd] + ([br, br] if has_prev else []), out_specs=[br, br, br],
        out_shape=[jax.ShapeDtypeStruct(q.shape, BF16), jax.ShapeDtypeStruct(q.shape, kv_dtype),
                   jax.ShapeDtypeStruct(q.shape, kv_dtype)],
        scratch_shapes=[pltpu.VMEM((S, LANES), F32)] * 6,
        operands=ins, sem=("parallel", "parallel", "arbitrary"), carry=carry)
    return dq, dk, dv, moved


def _adamw(w, grads, m, v):
    R, C = w.shape
    tr = _div(R, 256, SUBLANES)
    ng = len(grads)
    c1 = 1.0 - ADAM_B1 ** ADAM_STEP
    c2 = 1.0 - ADAM_B2 ** ADAM_STEP

    def body(*refs):
        w_ref, m_ref, v_ref = refs[0], refs[1 + ng], refs[2 + ng]
        d_ref, mo_ref, vo_ref = refs[3 + ng:6 + ng]
        g = refs[1][...]
        if ng == 2:
            g = g + refs[2][...]
            refs[6 + ng][...] = g
        mn = ADAM_B1 * m_ref[...] + (1.0 - ADAM_B1) * g
        vn = ADAM_B2 * v_ref[...] + (1.0 - ADAM_B2) * (g * g)
        d_ref[...] = -ADAM_LR * ((mn / c1) / (jnp.sqrt(vn / c2) + ADAM_EPS) + ADAM_WD * w_ref[...])
        mo_ref[...] = mn
        vo_ref[...] = vn

    blk = pl.BlockSpec((tr, C), lambda i: (i, 0))
    n_out = 3 + (ng == 2)
    outs = pl.pallas_call(
        body, name="adamw", grid=(R // tr,), in_specs=[blk] * (3 + ng), out_specs=[blk] * n_out,
        out_shape=[jax.ShapeDtypeStruct((R, C), F32)] * n_out, compiler_params=_params("parallel"),
    )(w, *grads, m, v)
    return (outs[3] if ng == 2 else grads[0],) + tuple(outs[:3])


def _sum_shards(recv):
    n, R, C = recv.shape
    tr = _div(R, 256, SUBLANES if recv.dtype == F32 else 2 * SUBLANES)

    def body(r_ref, o_ref):
        s = r_ref[0].astype(F32)
        for i in range(1, n):
            s = s + r_ref[i].astype(F32)
        o_ref[...] = s

    return pl.pallas_call(
        body, name="sum_shards", grid=(R // tr,), in_specs=[pl.BlockSpec((n, tr, C), lambda i: (0, i, 0))],
        out_specs=pl.BlockSpec((tr, C), lambda i: (i, 0)), out_shape=jax.ShapeDtypeStruct((R, C), F32),
        compiler_params=_params("parallel"),
    )(recv)


N_DEV = 8
N_CHIPS = 4


def _all_gather_small(x, carry=None):
    m_per, n = x.shape

    def body(x_ref, out_ref, send_sems, recv_sems, local_sem):
        cx, cy, cc = _coords()
        me, sibling = (cx, cy, cc), (cx, cy, 1 - cc)
        chips = [(1 - cx, cy), (cx, 1 - cy), (1 - cx, 1 - cy)]

        def rows(px, py, pc):
            return out_ref.at[pl.ds((4 * px + 2 * py + pc) * m_per, m_per), :]

        def copy(k, block, to, src=None):
            return pltpu.make_async_remote_copy(
                src_ref=rows(*block) if src is None else src, dst_ref=rows(*block), send_sem=send_sems.at[k],
                recv_sem=recv_sems.at[k], device_id=to, device_id_type=MESH)

        mine = pltpu.make_async_copy(x_ref, rows(*me), local_sem)
        mine.start()
        first = [copy(0, me, sibling, src=x_ref)]
        first += [copy(1 + j, me, (*chip, cc), src=x_ref) for j, chip in enumerate(chips)]
        for cp in first:
            cp.start()
        passed = [copy(4 + j, (*chip, cc), sibling) for j, chip in enumerate(chips)]
        for j, chip in enumerate(chips):
            copy(1 + j, (*chip, cc), me).wait_recv()
            passed[j].start()
        copy(0, sibling, me).wait_recv()
        for j, chip in enumerate(chips):
            copy(4 + j, (*chip, 1 - cc), me).wait_recv()
        for cp in first + passed:
            cp.wait_send()
        mine.wait()

    (out,), moved = _carried_call(
        body, name="all_gather_small", grid=(1,), out_shape=[jax.ShapeDtypeStruct((N_DEV * m_per, n), x.dtype)],
        in_specs=[pl.BlockSpec(memory_space=pltpu.VMEM)], out_specs=[pl.BlockSpec(memory_space=pltpu.VMEM)],
        scratch_shapes=[pltpu.SemaphoreType.DMA((7,)), pltpu.SemaphoreType.DMA((7,)), pltpu.SemaphoreType.DMA],
        operands=(x,), sem=("arbitrary",), carry=carry)
    return out, moved


def _layer_moves(kind, arrays_from, arrays_to, pieces, layer_major=()):
    used = sorted({w for w, _ in pieces})
    pos = {w: i for i, w in enumerate(used)}
    gather = kind == "gather"

    def half(ref, c):
        rows = ref.shape[0] // 2
        return ref.at[pl.ds(c * rows, rows), :]

    def slot(d, w, chip, l):
        return d.at[l, chip] if w in layer_major else d.at[chip, l]

    def plan(src_refs, dst_refs, me):
        cx, cy, cc = me
        mine = 2 * cx + cy
        remote, local = [], []
        for w, l in pieces:
            s, d = src_refs[pos[w]], dst_refs[pos[w]]
            for px, py in _other_chips(cx, cy):
                if gather:
                    remote.append((half(s.at[l], cc), half(slot(d, w, mine, l), cc), (px, py, cc)))
                else:
                    remote.append((s.at[2 * px + py, l], d.at[mine, l], (px, py, cc)))
            local.append((s.at[l], slot(d, w, mine, l)) if gather else (s.at[mine, l], d.at[mine, l]))
        return remote, local

    def onward(src_refs, dst_refs, me):
        cx, cy, cc = me
        moves = []
        for w, l in pieces:
            d = dst_refs[pos[w]]
            for px, py in _other_chips(cx, cy):
                landed = half(slot(d, w, 2 * px + py, l), cc)
                moves.append((landed, landed, (cx, cy, 1 - cc)))
        return moves

    n = 3 * len(pieces)
    carry = _Carry([arrays_from[w] for w in used], [arrays_to[w] for w in used], plan, n, len(pieces),
                   onward if gather else None, n if gather else 0)
    return carry, used


def _swap_with_sibling(sums):
    def plan(src_refs, dst_refs, me):
        cx, cy, cc = me
        return [(s, d, (cx, cy, 1 - cc)) for s, d in zip(src_refs, dst_refs)], []

    return _Carry(sums, [lax.empty(s.shape, s.dtype) for s in sums], plan, len(sums), 0)


def _pack(arrs, width):
    parts, layout, row = [], [], 0
    for a in arrs:
        flat = a.reshape(-1).astype(F32)
        rows = -(-flat.shape[0] // (width * SUBLANES)) * SUBLANES
        parts.append(jnp.pad(flat, (0, rows * width - flat.shape[0])).reshape(rows, width))
        layout.append((row, rows, a.shape))
        row += rows
    pad = -row % (8 * SUBLANES) if row > 8 * SUBLANES else 0
    if pad:
        parts.append(jnp.zeros((pad, width), F32))
    return jnp.concatenate(parts, axis=0), layout, row + pad


def _unpack(buf, layout, idx):
    row, rows, shape = layout[idx]
    size = math.prod(shape)
    return buf[row:row + rows].reshape(-1)[:size].reshape(shape)


def kernel(x, c, ln_g, ada_w, ada_b, ssm_lam_re, ssm_lam_im, ssm_log_dt, ssm_b_re, ssm_b_im, ssm_c_re, ssm_c_im, ssm_d, ssm_w_glu, kv_g, kv_ada_w, kv_ada_b, w_kv, attn_w_q, attn_w_o, mlp_w1, mlp_w2, final_g, loss_target, m_ln_g, m_ada_w, m_ada_b, m_ssm_lam_re, m_ssm_lam_im, m_ssm_log_dt, m_ssm_b_re, m_ssm_b_im, m_ssm_c_re, m_ssm_c_im, m_ssm_d, m_ssm_w_glu, m_kv_g, m_kv_ada_w, m_kv_ada_b, m_w_kv, m_attn_w_q, m_attn_w_o, m_mlp_w1, m_mlp_w2, m_final_g, v_ln_g, v_ada_w, v_ada_b, v_ssm_lam_re, v_ssm_lam_im, v_ssm_log_dt, v_ssm_b_re, v_ssm_b_im, v_ssm_c_re, v_ssm_c_im, v_ssm_d, v_ssm_w_glu, v_kv_g, v_kv_ada_w, v_kv_ada_b, v_w_kv, v_attn_w_q, v_attn_w_o, v_mlp_w1, v_mlp_w2, v_final_g):
    B, S, D = x.shape
    N = B * S
    depth = ln_g.shape[0]
    n_a = ssm_w_glu.shape[0]
    n_b = attn_w_q.shape[0]
    FF = mlp_w1.shape[2] * N_CHIPS
    cx, cy, cc = _coords()
    chip = 2 * cx + cy
    dev = 4 * cx + 2 * cy + cc
    n_ex = N_DEV * B
    ada_cols = ada_w.shape[-1]
    kv_cols = kv_ada_w.shape[-1]

    GLU, KV, Q, O, W1, W2 = range(6)
    shards = [ssm_w_glu.astype(BF16), w_kv.astype(BF16)[None], attn_w_q.astype(BF16), attn_w_o.astype(BF16),
              mlp_w1.astype(BF16), mlp_w2.astype(BF16)]
    row_sharded = (O, W2)
    wg = [lax.empty((s.shape[0], N_CHIPS) + s.shape[1:] if w in row_sharded else (N_CHIPS,) + s.shape, BF16)
          for w, s in enumerate(shards)]

    def whole_rows(w):
        L, _, R, C = wg[w].shape
        return wg[w].reshape(1, L, N_CHIPS * R, C)

    def landed(arrays, used, moved):
        for w, a in zip(used, moved):
            arrays[w] = a

    fetch_with = {}

    def carried_by(kind, l, *pieces):
        fetch_with.setdefault((kind, l), []).extend(pieces)

    assert n_a >= 1 and n_b >= 1, (n_a, n_b)
    carried_by("mixer", 0, *[(GLU, l) for l in range(n_a)], (W1, 0))
    carried_by("glu_proj", 0, (W2, 0))
    carried_by("mlp_up", 0, (Q, 0))
    carried_by("mlp_up", n_a - 1, (O, 0))
    carried_by("mixer", n_a - 1, (KV, 0))
    for l in range(1, depth):
        if l > n_a:
            carried_by("mixer", l, (W1, l))
        else:
            carried_by("mlp_down", l - 1, (W1, l))
        carried_by("mixer", l, (W2, l))
    for j in range(1, n_b):
        carried_by("mixer", n_a + j - 1, (Q, j), (O, j))

    def fetch(kind, l):
        pieces = fetch_with.get((kind, l))
        if not pieces:
            return None, []
        return _layer_moves("gather", shards, wg, pieces, layer_major=row_sharded)

    def mm_carrying(kind, l, *args, **kw):
        carry, used = fetch(kind, l)
        if carry is None:
            return _mm(kind, *args, **kw)
        out, moved = _mm(kind, *args, carry=carry, **kw)
        landed(wg, used, moved)
        return out

    c_pack, c_layout, _ = _pack([c], D)
    c_all_buf, _ = _all_gather_small(c_pack)
    c_rows = c_pack.shape[0]
    c_all = jnp.concatenate([_unpack(c_all_buf[d * c_rows:(d + 1) * c_rows], c_layout, 0) for d in range(N_DEV)], axis=0)
    sc_all = jax.nn.silu(c_all).astype(BF16)
    n_mod = depth * 2
    ada_w8 = ada_w.reshape(n_mod, 1, D, ada_cols)
    ada_b_row = ada_b.reshape(1, n_mod * ada_cols)
    mod_local = _mm("ada_fwd", sc_all, ada_w8, mode="nn", M=n_ex, N=n_mod * ada_cols, K=D, b_lay="cs", b_ns=n_mod,
                    epi=_add, extras=[("n", ada_b_row)])
    kv_ada_b_local = lax.dynamic_slice(kv_ada_b.reshape(N_CHIPS, kv_cols), (chip, 0), (1, kv_cols))
    kvmod_local = _mm("ada_fwd", sc_all, _as4(kv_ada_w), mode="nn", M=n_ex, N=kv_cols, K=D, epi=_add,
                      extras=[("n", kv_ada_b_local)])
    mod_pack, mod_layout, mod_rows = _pack([mod_local, kvmod_local, ln_g, ssm_d], D)
    mod_buf, _ = _all_gather_small(mod_pack)

    def from_chip(j, idx):
        d = 2 * j
        return _unpack(mod_buf[d * mod_rows:(d + 1) * mod_rows], mod_layout, idx)

    my_rows = lambda a: lax.dynamic_slice_in_dim(a, dev * B, B, axis=0)
    mods = jnp.concatenate([my_rows(from_chip(j, 0)).reshape(B, n_mod, ada_cols) for j in range(N_CHIPS)], axis=2)
    kvmod = jnp.concatenate([my_rows(from_chip(j, 1)) for j in range(N_CHIPS)], axis=1)
    ln_g_full = jnp.concatenate([from_chip(j, 2) for j in range(N_CHIPS)], axis=2)
    ssm_d_full = jnp.concatenate([from_chip(j, 3) for j in range(N_CHIPS)], axis=1)

    def mod3(l, s):
        mrow = mods[:, l * 2 + s]
        return [mrow[:, i * D:(i + 1) * D].reshape(B, 1, D) for i in range(3)]

    kv_shift, kv_scale = kvmod[:, :D].reshape(B, 1, D), kvmod[:, D:].reshape(B, 1, D)

    s5_tabs = []
    for l in range(n_a):
        prm = (ssm_lam_re[l], ssm_lam_im[l], ssm_log_dt[l], ssm_b_re[l], ssm_b_im[l], ssm_c_re[l], ssm_c_im[l])
        (bd, cd, _, _), disc_vjp = jax.vjp(_s5_discretize, *prm)
        pw, seg_f, seg_b = _s5_scan_coefs(ssm_lam_re[l], ssm_lam_im[l], ssm_log_dt[l], S5_SEG)
        s5_tabs.append((bd.astype(BF16), cd.astype(BF16), pw, seg_f, seg_b, disc_vjp))

    h = x.reshape(N, D)
    saved = []
    k_all = v_all = None
    shift, scale, gate = mod3(0, 0)
    u = _normmod(h, ln_g_full[0, 0].reshape(1, D), scale, shift, B)
    for l in range(depth):
        sv = {}
        sv["h0"], sv["scale0"], sv["gate0"], sv["u0"] = h, scale, gate, u
        shift1, scale1, gate1 = mod3(l, 1)
        norm1 = [("n", ln_g_full[l, 1].reshape(1, D)), ("ex", scale1), ("ex", shift1)]
        carry, used = fetch("mixer", l)
        if l < n_a:
            bd, cd, pw, seg_f, _, _ = s5_tabs[l]
            z, carries, moved = _s5_fwd(u, bd, cd, pw, seg_f, ssm_d_full[l].reshape(1, D), B, carry)
            landed(wg, used, moved)
            zz = mm_carrying("glu_proj", l, z, wg[GLU], mode="nn", M=N, N=2 * D, K=D, b_lay="cs", b_l=l, b_ns=N_CHIPS)
            y, h, u = _glu_residual_norm(zz, h, gate, ln_g_full[l, 1].reshape(1, D), scale1, shift1, B)
            sv["z"], sv["carries"], sv["zz"] = z, carries, zz
        else:
            j = l - n_a
            q = _mm("q_proj", u, wg[Q], mode="nn", M=N, N=3 * D, K=D, b_lay="cs", b_l=j, b_ns=N_CHIPS)
            q3 = q.reshape(B, S, 3 * D)
            o, lse, moved = _attn_fwd(q3, k_all, v_all, B, carry)
            landed(wg, used, moved)
            o2 = o.reshape(N, D)
            y, h, u = _mm("o_proj", o2, whole_rows(O), mode="nn", M=N, N=D, K=D, b_l=j, tm=1024, out_dtype=(BF16, F32, BF16),
                          epi=_gated_residual_norm, extras=[("mn", h), ("ex", gate)] + norm1, rows_per_ex=S)
            sv["q"], sv["o"], sv["lse"] = q3, o, lse
        sv["y0"] = y
        sv["h1"], sv["scale1"], sv["gate1"], sv["u1"] = h, scale1, gate1, u
        r = mm_carrying("mlp_up", l, u, wg[W1], mode="nn", M=N, N=FF, K=D, b_lay="cs", b_l=l, b_ns=N_CHIPS,
                        out_dtype=BF16, epi=_relu2)
        if l + 1 < depth:
            shift, scale, gate = mod3(l + 1, 0)
            y, h, u = mm_carrying(
                "mlp_down", l, r, whole_rows(W2), mode="nn", M=N, N=D, K=FF, b_l=l, tk=2048, tm=512,
                out_dtype=(BF16, F32, BF16), epi=_gated_residual_norm, rows_per_ex=S,
                extras=[("mn", h), ("ex", gate1), ("n", ln_g_full[l + 1, 0].reshape(1, D)), ("ex", scale), ("ex", shift)])
        else:
            y, h = mm_carrying("mlp_down", l, r, whole_rows(W2), mode="nn", M=N, N=D, K=FF, b_l=l, tk=2048, tm=1024,
                               out_dtype=(BF16, F32), epi=_gated_residual, extras=[("mn", h), ("ex", gate1)], rows_per_ex=S)
        sv["r"], sv["y1"] = r, y
        saved.append(sv)
        if l == n_a - 1:
            h_kv = h
            u_kv = _normmod(h, kv_g.reshape(1, D), kv_scale, kv_shift, B)
            half = N_CHIPS // 2
            k_all = _mm("kv_proj", u_kv, wg[KV], mode="nn", M=N, N=3 * D, K=D, b_lay="cs", b_s0=0, b_ns=half).reshape(B, S, 3 * D)
            v_all = _mm("kv_proj", u_kv, wg[KV], mode="nn", M=N, N=3 * D, K=D, b_lay="cs", b_s0=half, b_ns=half).reshape(B, S, 3 * D)

    loss_buf, dh, d_final_g = _loss_head(h, final_g.reshape(1, D), loss_target.reshape(N, D))
    loss = lax.psum(loss_buf[0, 0], ("x", "y", "c"))

    dg = [lax.empty((N_CHIPS,) + s.shape, BF16) for s in shards]
    recv = [lax.empty((N_CHIPS,) + s.shape, BF16) for s in shards]

    def send(pieces):
        return _layer_moves("scatter", dg, recv, pieces)

    send_with = {l: [(W1, l), (W2, l)] for l in range(depth)}
    for l in range(n_a):
        send_with[l] += [(GLU, l)]
    for j in range(n_b):
        send_with[n_a + j] += [(O, j)]
        send_with[n_a + j - 1] += [(Q, j)]
    send_with[n_a - 1] += [(KV, 0)]
    d_ln_g = [[None, None] for _ in range(depth)]
    d_mods = [[None, None] for _ in range(depth)]
    d_s5 = [None] * n_a
    dk_acc = dv_acc = None
    half = N_CHIPS // 2

    def tn_grad(name, a, d, into, l, Mr, Nc, lay, s0=0, ns=N_CHIPS):
        return _mm(name, a, _as4(d), mode="tn", M=Mr, N=Nc, K=N, b_lay="cs", out_dtype=BF16, out_lay=lay,
                   out4_shape=into.shape, out_into=into, out_l=l, out_s0=s0, out_ns=ns, tk=2048)

    dy, d_gate1 = _residual_bwd(dh, saved[-1]["gate1"], saved[-1]["y1"], B)
    for l in reversed(range(depth)):
        sv = saved[l]
        dg[W2] = tn_grad("mlp_down_dw", sv["r"], dy, dg[W2], l, FF, D, "rs")
        da = _mm("mlp_down_dx", dy, whole_rows(W2), mode="nt", M=N, N=FF, K=D, b_l=l, out_dtype=BF16,
                 epi=_relu2_bwd, extras=[("mn", sv["r"])])
        dg[W1] = tn_grad("mlp_up_dw", sv["u1"], da, dg[W1], l, D, FF, "cs")
        du = _mm("mlp_up_dx", da, wg[W1], mode="nt", M=N, N=D, K=FF, b_lay="cs", b_l=l, b_ns=N_CHIPS)
        dh, dgv, d_scale1, d_shift1, dy, d_gate0 = _normmod_bwd(du, sv["h1"], ln_g_full[l, 1].reshape(1, D), sv["scale1"],
                                                                dh, B, below=(sv["y0"], sv["gate0"]))
        d_ln_g[l][1] = dgv
        d_mods[l][1] = jnp.concatenate([d_shift1, d_scale1, d_gate1], axis=2)
        if l < n_a:
            bd, cd, pw, seg_f, seg_b, disc_vjp = s5_tabs[l]
            dzz = _glu_bwd(dy, sv["zz"])
            dg[GLU] = tn_grad("glu_proj_dw", sv["z"], dzz, dg[GLU], l, D, 2 * D, "cs")
            dz = _mm("glu_proj_dx", dzz, wg[GLU], mode="nt", M=N, N=D, K=2 * D, b_lay="cs", b_l=l, b_ns=N_CHIPS)
            carry, used = send(send_with[l])
            du, d_bd, d_cd, d_a2, d_dskip, moved = _s5_bwd(sv["u0"], dz, bd, cd, pw, seg_f, seg_b,
                                                           ssm_d_full[l].reshape(1, D), sv["carries"], B, carry)
            landed(recv, used, moved)
            d_are = (d_a2[:, 0, :CHUNK_STATE] + d_a2[:, 0, CHUNK_STATE:]).reshape(-1, SSM_STATE)
            d_aim = (d_a2[:, 1, CHUNK_STATE:] - d_a2[:, 1, :CHUNK_STATE]).reshape(-1, SSM_STATE)
            d_s5[l] = disc_vjp((d_bd, jnp.swapaxes(d_cd, 1, 2), d_are, d_aim)) + (d_dskip,)
        else:
            j = l - n_a
            dg[O] = tn_grad("o_proj_dw", sv["o"].reshape(N, D), dy, dg[O], j, D, D, "rs")
            do = _mm("o_proj_dx", dy, whole_rows(O), mode="nt", M=N, N=D, K=D, b_l=j)
            carry, used = send(send_with[l])
            dq, dk_acc, dv_acc, moved = _attn_bwd(sv["q"], k_all, v_all, sv["o"], sv["lse"], do.reshape(B, S, D),
                                                  dk_acc, dv_acc, B, l == n_a, carry)
            landed(recv, used, moved)
            dq2 = dq.reshape(N, 3 * D)
            dg[Q] = tn_grad("q_proj_dw", sv["u0"], dq2, dg[Q], j, D, 3 * D, "cs")
            du = _mm("q_proj_dx", dq2, wg[Q], mode="nt", M=N, N=D, K=3 * D, b_lay="cs", b_l=j, b_ns=N_CHIPS)
        below = (saved[l - 1]["y1"], saved[l - 1]["gate1"]) if l > 0 else None
        dh, dgv, d_scale0, d_shift0, dy, d_gate1 = _normmod_bwd(du, sv["h0"], ln_g_full[l, 0].reshape(1, D), sv["scale0"],
                                                                dh, B, below=None if l == n_a else below)
        d_ln_g[l][0] = dgv
        d_mods[l][0] = jnp.concatenate([d_shift0, d_scale0, d_gate0], axis=2)
        if l == n_a:
            dk2, dv2 = dk_acc.reshape(N, 3 * D), dv_acc.reshape(N, 3 * D)
            dg[KV] = tn_grad("kv_proj_dw", u_kv, dk2, dg[KV], 0, D, 3 * D, "cs", s0=0, ns=half)
            dg[KV] = tn_grad("kv_proj_dw", u_kv, dv2, dg[KV], 0, D, 3 * D, "cs", s0=half, ns=half)
            du_kv = _mm("kv_proj_dx", dk2, wg[KV], mode="nt", M=N, N=D, K=3 * D, b_lay="cs", b_s0=0, b_ns=half)
            du_kv = _mm("kv_proj_dx", dv2, wg[KV], mode="nt", M=N, N=D, K=3 * D, b_lay="cs", b_s0=half, b_ns=half,
                        tm=1024, epi=_add, extras=[("mn", du_kv)])
            dh, d_kv_g, d_kv_scale, d_kv_shift, dy, d_gate1 = _normmod_bwd(du_kv, h_kv, kv_g.reshape(1, D), kv_scale, dh, B,
                                                                           below=below)
    grad_x = dh.reshape(B, S, D)

    own = [_sum_shards(r.reshape(N_CHIPS, -1, r.shape[-1])) for r in recv]

    d_kvmod = jnp.concatenate([d_kv_shift, d_kv_scale], axis=2).reshape(B, 2 * D)
    d_mod_all = jnp.concatenate([d_mods[l][s].reshape(B, 3 * D) for l in range(depth) for s in range(2)], axis=1)
    small = [
        d_mod_all, d_kvmod,
        jnp.stack([jnp.stack([d_ln_g[l][0].reshape(D), d_ln_g[l][1].reshape(D)]) for l in range(depth)]),
        jnp.stack([d_s5[l][0] for l in range(n_a)]), jnp.stack([d_s5[l][1] for l in range(n_a)]),
        jnp.stack([d_s5[l][2] for l in range(n_a)]),
        jnp.stack([d_s5[l][3] for l in range(n_a)]), jnp.stack([d_s5[l][4] for l in range(n_a)]),
        jnp.stack([d_s5[l][5] for l in range(n_a)]), jnp.stack([d_s5[l][6] for l in range(n_a)]),
        jnp.stack([d_s5[l][7].reshape(D) for l in range(n_a)]),
        d_kv_g.reshape(D), d_final_g.reshape(D),
    ]
    small_pack, small_layout, small_rows = _pack(small, D)
    small_buf, other = _all_gather_small(small_pack, _swap_with_sibling(own))
    small_sum = _sum_shards(small_buf.reshape(N_DEV, small_rows, D))
    red = lambda idx: _unpack(small_sum, small_layout, idx)
    per_dev = lambda idx: jnp.concatenate(
        [_unpack(small_buf[d * small_rows:(d + 1) * small_rows], small_layout, idx) for d in range(N_DEV)], axis=0)

    dm_all = per_dev(0).reshape(n_ex, n_mod, 3 * D)
    dm_cols = lax.dynamic_slice_in_dim(dm_all, chip * ada_cols, ada_cols, axis=2).reshape(n_ex, n_mod * ada_cols)
    g_ada_w = _mm("ada_dw", sc_all, _as4(dm_cols), mode="tn", M=D, N=n_mod * ada_cols, K=n_ex, b_lay="cs",
                  out_lay="cs", out4_shape=(n_mod, 1, D, ada_cols), out_ns=n_mod).reshape(ada_w.shape)
    dkvm_all = per_dev(1)
    dkvm_cols = lax.dynamic_slice_in_dim(dkvm_all, chip * kv_cols, kv_cols, axis=1)
    g_kv_ada_w = _mm("ada_dw", sc_all, _as4(dkvm_cols), mode="tn", M=D, N=kv_cols, K=n_ex, b_lay="cs")
    g_ada_b_full = (red(0)[0] + red(0)[1]).reshape(depth, 2, 3 * D) if B == 2 else jnp.sum(red(0), axis=0).reshape(depth, 2, 3 * D)
    g_ada_b = lax.dynamic_slice_in_dim(g_ada_b_full, chip * ada_cols, ada_cols, axis=2)
    g_kv_ada_b = red(1)[0] + red(1)[1] if B == 2 else jnp.sum(red(1), axis=0)
    g_ln_g = lax.dynamic_slice_in_dim(red(2), chip * (D // N_CHIPS), D // N_CHIPS, axis=2)
    g_ssm_d = lax.dynamic_slice_in_dim(red(10), chip * (D // N_CHIPS), D // N_CHIPS, axis=1)
    small_grads = {
        "ln_g": g_ln_g, "ada_b": g_ada_b, "ssm_lam_re": red(3), "ssm_lam_im": red(4), "ssm_log_dt": red(5),
        "ssm_b_re": red(6), "ssm_b_im": red(7), "ssm_c_re": red(8), "ssm_c_im": red(9), "ssm_d": g_ssm_d,
        "kv_g": red(11), "kv_ada_b": g_kv_ada_b, "final_g": red(12),
    }
    small_w = {"ln_g": (ln_g, m_ln_g, v_ln_g), "ada_b": (ada_b, m_ada_b, v_ada_b),
               "ssm_lam_re": (ssm_lam_re, m_ssm_lam_re, v_ssm_lam_re), "ssm_lam_im": (ssm_lam_im, m_ssm_lam_im, v_ssm_lam_im),
               "ssm_log_dt": (ssm_log_dt, m_ssm_log_dt, v_ssm_log_dt), "ssm_b_re": (ssm_b_re, m_ssm_b_re, v_ssm_b_re),
               "ssm_b_im": (ssm_b_im, m_ssm_b_im, v_ssm_b_im), "ssm_c_re": (ssm_c_re, m_ssm_c_re, v_ssm_c_re),
               "ssm_c_im": (ssm_c_im, m_ssm_c_im, v_ssm_c_im), "ssm_d": (ssm_d, m_ssm_d, v_ssm_d),
               "kv_g": (kv_g, m_kv_g, v_kv_g), "kv_ada_b": (kv_ada_b, m_kv_ada_b, v_kv_ada_b),
               "final_g": (final_g, m_final_g, v_final_g)}
    names = list(small_w)
    wp, lay_w, _ = _pack([small_w[n][0] for n in names], D)
    gp, _, _ = _pack([small_grads[n] for n in names], D)
    mp, _, _ = _pack([small_w[n][1] for n in names], D)
    vp, _, _ = _pack([small_w[n][2] for n in names], D)
    _, d_p, m_p, v_p = _adamw(wp, [gp], mp, vp)
    upd = {n: (small_grads[n].reshape(small_w[n][0].shape), _unpack(d_p, lay_w, i), _unpack(m_p, lay_w, i), _unpack(v_p, lay_w, i))
           for i, n in enumerate(names)}

    def big(w, m, v, g_own, g_other=None):
        C = w.shape[-1]
        gs = [g_own.reshape(-1, C)] + ([g_other.reshape(-1, C)] if g_other is not None else [])
        return tuple(t.reshape(w.shape) for t in _adamw(w.reshape(-1, C), gs, m.reshape(-1, C), v.reshape(-1, C)))

    upd["ssm_w_glu"] = big(ssm_w_glu, m_ssm_w_glu, v_ssm_w_glu, own[0], other[0])
    upd["w_kv"] = big(w_kv, m_w_kv, v_w_kv, own[1], other[1])
    upd["attn_w_q"] = big(attn_w_q, m_attn_w_q, v_attn_w_q, own[2], other[2])
    upd["attn_w_o"] = big(attn_w_o, m_attn_w_o, v_attn_w_o, own[3], other[3])
    upd["mlp_w1"] = big(mlp_w1, m_mlp_w1, v_mlp_w1, own[4], other[4])
    upd["mlp_w2"] = big(mlp_w2, m_mlp_w2, v_mlp_w2, own[5], other[5])
    upd["ada_w"] = big(ada_w, m_ada_w, v_ada_w, g_ada_w)
    upd["kv_ada_w"] = big(kv_ada_w, m_kv_ada_w, v_kv_ada_w, g_kv_ada_w)

    order = ["ln_g", "ada_w", "ada_b", "ssm_lam_re", "ssm_lam_im", "ssm_log_dt", "ssm_b_re", "ssm_b_im", "ssm_c_re",
             "ssm_c_im", "ssm_d", "ssm_w_glu", "kv_g", "kv_ada_w", "kv_ada_b", "w_kv", "attn_w_q", "attn_w_o", "mlp_w1",
             "mlp_w2", "final_g"]
    return (loss, grad_x, *[upd[n][0] for n in order], *[upd[n][1] for n in order], *[upd[n][2] for n in order],
            *[upd[n][3] for n in order])
```

```python
import functools
import math

import jax
import jax.numpy as jnp
from jax import lax
from jax.experimental import pallas as pl
from jax.experimental.pallas import tpu as pltpu

F32 = jnp.float32
BF16 = jnp.bfloat16
MESH = pl.DeviceIdType.MESH

EPS = 1e-6
NEG = -1e30
SSM_GROUP = 16
SSM_STATE = 64
HEAD_DIM = 64
ATTN_BLOCK = 128
DILATIONS = (1, 4, 16)
ADAM_LR, ADAM_B1, ADAM_B2, ADAM_EPS, ADAM_WD, ADAM_STEP = 0.001, 0.9, 0.999, 1e-08, 0.01, 10

LANES = 128
SUBLANES = 8
CHUNK_GROUPS = LANES // SSM_GROUP
CHUNK_STATE = CHUNK_GROUPS * SSM_STATE
VMEM_LIMIT = 56 * 1024 * 1024


def _div(dim, pref, mult):
    t = min(pref, dim) // mult * mult
    while t >= mult:
        if dim % t == 0:
            return t
        t -= mult
    return dim


def _params(*sem):
    return pltpu.CompilerParams(dimension_semantics=sem, vmem_limit_bytes=VMEM_LIMIT)


def _coords():
    return lax.axis_index("x"), lax.axis_index("y"), lax.axis_index("c")


def _other_chips(cx, cy):
    return [(1 - cx, cy), (cx, 1 - cy), (1 - cx, 1 - cy)]


class _Carry:
    def __init__(self, srcs, dsts, plan, n_remote, n_local, onward=None, n_onward=0):
        self.srcs, self.dsts, self.plan, self.n_remote, self.n_local = list(srcs), list(dsts), plan, n_remote, n_local
        self.onward, self.n_onward = onward, n_onward


def _carried_call(body, *, name, grid, in_specs, out_specs, out_shape, scratch_shapes, operands, sem, carry=None):
    if carry is None:
        outs = pl.pallas_call(body, name=name, grid=grid, in_specs=in_specs, out_specs=out_specs, out_shape=out_shape,
                              scratch_shapes=scratch_shapes, compiler_params=_params(*sem))(*operands)
        return list(outs), []
    n_in, n_out, n_scr = len(in_specs), len(out_specs), len(scratch_shapes)
    ns, nd = len(carry.srcs), len(carry.dsts)

    def wrapped(*refs):
        base_in, src_refs = refs[:n_in], refs[n_in:n_in + ns]
        o0 = n_in + ns + nd
        base_out, dst_refs = refs[o0:o0 + n_out], refs[o0 + n_out:o0 + n_out + nd]
        s0 = o0 + n_out + nd
        base_scr = refs[s0:s0 + n_scr]
        send_sems, recv_sems, local_sems = refs[s0 + n_scr:]
        pids = [pl.program_id(a) for a in range(len(grid))]
        first = functools.reduce(jnp.logical_and, [p == 0 for p in pids])
        last = functools.reduce(jnp.logical_and, [p == g - 1 for p, g in zip(pids, grid)])

        def remote_copies(moves, k0):
            return [pltpu.make_async_remote_copy(src_ref=s, dst_ref=d, send_sem=send_sems.at[k0 + i], recv_sem=recv_sems.at[k0 + i],
                                                 device_id=peer, device_id_type=MESH) for i, (s, d, peer) in enumerate(moves)]

        def copies():
            remote, local = carry.plan(src_refs, dst_refs, _coords())
            return remote_copies(remote, 0), [pltpu.make_async_copy(s, d, local_sems.at[i]) for i, (s, d) in enumerate(local)]

        @pl.when(first)
        def _():
            remote, local = copies()
            for cp in local + remote:
                cp.start()

        body(*base_in, *base_out, *base_scr)

        @pl.when(last)
        def _():
            remote, local = copies()
            for cp in remote:
                cp.wait_send()
                cp.wait_recv()
            for cp in local:
                cp.wait()
            if carry.onward is not None:
                second = remote_copies(carry.onward(src_refs, dst_refs, _coords()), carry.n_remote)
                for cp in second:
                    cp.start()
                for cp in second:
                    cp.wait_send()
                    cp.wait_recv()

    anyspec = pl.BlockSpec(memory_space=pl.ANY)
    outs = pl.pallas_call(
        wrapped, name=name, grid=grid, in_specs=list(in_specs) + [anyspec] * (ns + nd),
        out_specs=list(out_specs) + [anyspec] * nd,
        out_shape=list(out_shape) + [jax.ShapeDtypeStruct(d.shape, d.dtype) for d in carry.dsts],
        scratch_shapes=list(scratch_shapes) + [pltpu.SemaphoreType.DMA((carry.n_remote + carry.n_onward,)),
                                               pltpu.SemaphoreType.DMA((carry.n_remote + carry.n_onward,)),
                                               pltpu.SemaphoreType.DMA((max(carry.n_local, 1),))],
        input_output_aliases={n_in + ns + i: n_out + i for i in range(nd)},
        compiler_params=_params(*(["arbitrary"] * len(grid))),
    )(*operands, *carry.srcs, *carry.dsts)
    return list(outs[:n_out]), list(outs[n_out:])


def _mm(name, a, b4, *, mode, M, N, K, b_lay="cs", b_l=0, b_s0=0, b_ns=1, out_dtype=F32, out_lay=None, out4_shape=None,
        out_into=None, out_l=0, out_s0=0, out_ns=1, epi=None, extras=(), rows_per_ex=None, tm=2048, tn=1024, tk=1024,
        carry=None):
    _, _, bR, bC = b4.shape
    tm = _div(M, tm, SUBLANES if M % 16 else 16)
    brows, bcols = (N, K) if mode == "nt" else (K, N)
    if b_lay == "cs":
        assert bR == brows and bC * b_ns == bcols, (name, b4.shape, brows, bcols)
    else:
        assert bC == bcols and bR * b_ns == brows, (name, b4.shape, brows, bcols)
    n_lim = N
    k_lim = K
    if mode == "nt":
        if b_lay == "cs":
            k_lim = bC
        else:
            n_lim = bR
    else:
        if b_lay == "cs":
            n_lim = bC
        else:
            k_lim = bR
    if out_lay == "cs":
        oR, oC = out4_shape[2], out4_shape[3]
        assert oR == M and oC * out_ns == N, (name, out4_shape, M, N)
        n_lim = math.gcd(n_lim, oC)
    elif out_lay == "rs":
        oR, oC = out4_shape[2], out4_shape[3]
        assert oC == N and oR * out_ns == M, (name, out4_shape, M, N)
        tm = _div(oR, tm, SUBLANES)
    tn = _div(n_lim, tn, LANES)
    tk = _div(k_lim, tk, LANES if mode != "tn" else SUBLANES)
    if mode == "tn":
        tk = _div(k_lim, tk, 16) if k_lim % 16 == 0 else tk
    nk = K // tk
    grid = (M // tm, N // tn, nk)

    if mode == "tn":
        a_spec = pl.BlockSpec((tk, tm), lambda i, j, k: (k, i))
    else:
        a_spec = pl.BlockSpec((tm, tk), lambda i, j, k: (i, k))

    def b_index(ri, ci, br, bc):
        if b_lay == "cs":
            per = bC // bc
            return (b_s0 + ci // per, b_l, ri, ci % per)
        per = bR // br
        return (b_s0 + ri // per, b_l, ri % per, ci)

    if mode == "nt":
        b_spec = pl.BlockSpec((None, None, tn, tk), lambda i, j, k: b_index(j, k, tn, tk))
    else:
        b_spec = pl.BlockSpec((None, None, tk, tn), lambda i, j, k: b_index(k, j, tk, tn))

    in_specs = [a_spec, b_spec]
    operands = [a, b4]
    for kind, arr in extras:
        if kind == "mn":
            in_specs.append(pl.BlockSpec((tm, tn), lambda i, j, k: (i, j)))
        elif kind == "ex":
            per_ex = rows_per_ex // tm
            in_specs.append(pl.BlockSpec((None, 1, tn), lambda i, j, k: (i // per_ex, 0, j)))
        else:
            in_specs.append(pl.BlockSpec((1, tn), lambda i, j, k: (0, j)))
        operands.append(arr)
    n_extra = len(extras)

    multi = isinstance(out_dtype, tuple)
    n_out = len(out_dtype) if multi else 1
    if out_lay is None:
        out_shape = [jax.ShapeDtypeStruct((M, N), dt) for dt in (out_dtype if multi else (out_dtype,))]
        out_spec = [pl.BlockSpec((tm, tn), lambda i, j, k: (i, j)) for _ in range(n_out)]
    else:
        out_shape = [jax.ShapeDtypeStruct(tuple(out4_shape), out_dtype)]
        if out_lay == "cs":
            per_o = oC // tn
            out_spec = [pl.BlockSpec((None, None, tm, tn), lambda i, j, k: (out_s0 + j // per_o, out_l, i, j % per_o))]
        else:
            per_o = oR // tm
            out_spec = [pl.BlockSpec((None, None, tm, tn), lambda i, j, k: (out_s0 + i // per_o, out_l, i % per_o, j))]
    aliases = {}
    if out_into is not None:
        in_specs.append(pl.BlockSpec(memory_space=pl.ANY))
        operands.append(out_into)
        aliases = {len(operands) - 1: 0}

    dims = {"nn": (((1,), (0,)), ((), ())), "nt": (((1,), (1,)), ((), ())), "tn": (((0,), (0,)), ((), ()))}[mode]

    def body(a_ref, b_ref, *rest):
        extra_refs = rest[:n_extra]
        o_refs = rest[len(rest) - n_out - (nk > 1):len(rest) - (nk > 1)]

        def finish(r):
            if epi is not None:
                r = epi(r, *[e[...] for e in extra_refs])
            for o_ref, val in zip(o_refs, r if multi else (r,)):
                o_ref[...] = val.astype(o_ref.dtype)

        part = lax.dot_general(a_ref[...].astype(BF16), b_ref[...].astype(BF16), dims, preferred_element_type=F32)
        if nk == 1:
            finish(part)
            return
        acc = rest[-1]
        k = pl.program_id(2)

        @pl.when(k == 0)
        def _():
            acc[...] = part

        @pl.when(k != 0)
        def _():
            acc[...] += part

        @pl.when(k == nk - 1)
        def _():
            finish(acc[...])

    scratch = [pltpu.VMEM((tm, tn), F32)] if nk > 1 else []
    if carry is not None:
        assert out_into is None, name
        outs, moved = _carried_call(body, name=name, grid=grid, in_specs=in_specs, out_specs=out_spec, out_shape=out_shape,
                                    scratch_shapes=scratch, operands=operands, sem=("arbitrary",) * 3, carry=carry)
        return (tuple(outs) if multi else outs[0]), moved
    outs = pl.pallas_call(
        body, name=name, grid=grid, in_specs=in_specs, out_specs=out_spec, out_shape=out_shape,
        scratch_shapes=scratch, input_output_aliases=aliases,
        compiler_params=_params("parallel", "parallel", "arbitrary"),
    )(*operands)
    return tuple(outs) if multi else outs[0]


def _as4(w):
    return w.reshape((1, 1) + w.shape)


def _relu2(acc):
    r = jnp.maximum(acc, 0.0)
    return r * r


def _relu2_bwd(acc, r):
    return acc * (2.0 * jnp.sqrt(r.astype(F32)))


def _add(acc, e):
    return acc + e


def _gated_residual(acc, h, gate):
    return acc, h + gate * acc


def _modulated_norm(x, g, scale, shift):
    rstd = lax.rsqrt(jnp.mean(x * x, axis=-1, keepdims=True) + EPS)
    return ((x * rstd) * g) * (1.0 + scale) + shift


def _gated_residual_norm(acc, h, gate, g, scale, shift):
    h_new = h + gate * acc
    return acc, h_new, _modulated_norm(h_new, g, scale, shift)


def _row_tiles(N, B, pref=256):
    S = N // B
    tm = _div(S, pref, SUBLANES)
    return tm, S // tm


def _normmod(h, g, scale, shift, B):
    N, D = h.shape
    tm, per_ex = _row_tiles(N, B)

    def body(h_ref, g_ref, sc_ref, sh_ref, u_ref):
        u_ref[...] = _modulated_norm(h_ref[...], g_ref[...], sc_ref[...], sh_ref[...]).astype(u_ref.dtype)

    tok = pl.BlockSpec((tm, D), lambda i: (i, 0))
    vec = pl.BlockSpec((1, D), lambda i: (0, 0))
    ex = pl.BlockSpec((None, 1, D), lambda i: (i // per_ex, 0, 0))
    return pl.pallas_call(
        body, name="normmod_fwd", grid=(N // tm,), in_specs=[tok, vec, ex, ex], out_specs=tok,
        out_shape=jax.ShapeDtypeStruct((N, D), BF16), compiler_params=_params("parallel"),
    )(h, g, scale, shift)


def _normmod_bwd(du, h, g, scale, dh_in, B, below=None):
    N, D = h.shape
    tm, per_ex = _row_tiles(N, B)
    fused = below is not None

    def body(*refs):
        du_ref, h_ref, g_ref, sc_ref, dhin_ref = refs[:5]
        dh_ref, dg_ref, dsc_ref, dsh_ref = refs[5 + 2 * fused:9 + 2 * fused]
        i = pl.program_id(0)
        x = h_ref[...]
        gv = g_ref[...]
        d_u = du_ref[...].astype(F32)
        rstd = lax.rsqrt(jnp.mean(x * x, axis=-1, keepdims=True) + EPS)
        xn = x * rstd
        dyg = d_u * (1.0 + sc_ref[...])
        dxn = dyg * gv
        dh = dhin_ref[...] + rstd * (dxn - xn * jnp.mean(dxn * xn, axis=-1, keepdims=True))
        dh_ref[...] = dh
        sums = [(dsc_ref, jnp.sum(d_u * (xn * gv), axis=0, keepdims=True)), (dsh_ref, jnp.sum(d_u, axis=0, keepdims=True))]
        if fused:
            y_ref, gt_ref = refs[5:7]
            dy_ref, dgt_ref = refs[9 + 2 * fused:]
            dy_ref[...] = (gt_ref[...] * dh).astype(dy_ref.dtype)
            sums.append((dgt_ref, jnp.sum(dh * y_ref[...], axis=0, keepdims=True)))
        dg_t = jnp.sum(dyg * xn, axis=0, keepdims=True)

        @pl.when(i % per_ex == 0)
        def _():
            for ref, val in sums:
                ref[...] = val

        @pl.when(i % per_ex != 0)
        def _():
            for ref, val in sums:
                ref[...] += val

        @pl.when(i == 0)
        def _():
            dg_ref[...] = dg_t

        @pl.when(i != 0)
        def _():
            dg_ref[...] += dg_t

    tok = pl.BlockSpec((tm, D), lambda i: (i, 0))
    vec = pl.BlockSpec((1, D), lambda i: (0, 0))
    ex = pl.BlockSpec((None, 1, D), lambda i: (i // per_ex, 0, 0))
    per_ex_shape = jax.ShapeDtypeStruct((B, 1, D), F32)
    outs = pl.pallas_call(
        body, name="normmod_bwd", grid=(N // tm,), in_specs=[tok, tok, vec, ex, tok] + ([tok, ex] if fused else []),
        out_specs=[tok, vec, ex, ex] + ([tok, ex] if fused else []),
        out_shape=[jax.ShapeDtypeStruct((N, D), F32), jax.ShapeDtypeStruct((1, D), F32), per_ex_shape, per_ex_shape]
        + ([jax.ShapeDtypeStruct((N, D), BF16), per_ex_shape] if fused else []),
        compiler_params=_params("arbitrary"),
    )(du, h, g, scale, dh_in, *(below if fused else ()))
    return tuple(outs) if fused else tuple(outs) + (None, None)


def _residual_bwd(dh, gate, y, B):
    N, D = dh.shape
    tm, per_ex = _row_tiles(N, B)

    def body(dh_ref, gt_ref, y_ref, dy_ref, dgt_ref):
        i = pl.program_id(0)
        d = dh_ref[...]
        dy_ref[...] = (gt_ref[...] * d).astype(dy_ref.dtype)
        t = jnp.sum(d * y_ref[...], axis=0, keepdims=True)

        @pl.when(i % per_ex == 0)
        def _():
            dgt_ref[...] = t

        @pl.when(i % per_ex != 0)
        def _():
            dgt_ref[...] += t

    tok = pl.BlockSpec((tm, D), lambda i: (i, 0))
    ex = pl.BlockSpec((None, 1, D), lambda i: (i // per_ex, 0, 0))
    return pl.pallas_call(
        body, name="residual_bwd", grid=(N // tm,), in_specs=[tok, ex, tok], out_specs=[tok, ex],
        out_shape=[jax.ShapeDtypeStruct((N, D), BF16), jax.ShapeDtypeStruct((B, 1, D), F32)],
        compiler_params=_params("arbitrary"),
    )(dh, gate, y)


def _glu_residual_norm(zz, h, gate, g, scale, shift, B):
    N, D2 = zz.shape
    D = D2 // 2
    tm, per_ex = _row_tiles(N, B)

    def body(v_ref, g_ref, h_ref, gt_ref, ng_ref, sc_ref, sh_ref, y_ref, o_ref, u_ref):
        y = v_ref[...] * jax.nn.sigmoid(g_ref[...])
        y_ref[...] = y.astype(y_ref.dtype)
        h_new = h_ref[...] + gt_ref[...] * y
        o_ref[...] = h_new
        u_ref[...] = _modulated_norm(h_new, ng_ref[...], sc_ref[...], sh_ref[...]).astype(u_ref.dtype)

    tok = pl.BlockSpec((tm, D), lambda i: (i, 0))
    vec = pl.BlockSpec((1, D), lambda i: (0, 0))
    ex = pl.BlockSpec((None, 1, D), lambda i: (i // per_ex, 0, 0))
    return pl.pallas_call(
        body, name="glu_fwd", grid=(N // tm,),
        in_specs=[tok, pl.BlockSpec((tm, D), lambda i: (i, 1)), tok, ex, vec, ex, ex], out_specs=[tok, tok, tok],
        out_shape=[jax.ShapeDtypeStruct((N, D), BF16), jax.ShapeDtypeStruct((N, D), F32), jax.ShapeDtypeStruct((N, D), BF16)],
        compiler_params=_params("parallel"),
    )(zz, zz, h, gate, g, scale, shift)


def _glu_bwd(dy, zz):
    N, D2 = zz.shape
    D = D2 // 2
    tm = _div(N, 256, SUBLANES)

    def body(dy_ref, v_ref, g_ref, o_ref):
        d = dy_ref[...].astype(F32)
        s = jax.nn.sigmoid(g_ref[...])
        o_ref[...] = jnp.concatenate([d * s, d * v_ref[...] * s * (1.0 - s)], axis=1).astype(o_ref.dtype)

    return pl.pallas_call(
        body, name="glu_bwd", grid=(N // tm,),
        in_specs=[pl.BlockSpec((tm, D), lambda i: (i, 0)), pl.BlockSpec((tm, D), lambda i: (i, 0)),
                  pl.BlockSpec((tm, D), lambda i: (i, 1))],
        out_specs=pl.BlockSpec((tm, D2), lambda i: (i, 0)), out_shape=jax.ShapeDtypeStruct((N, D2), BF16),
        compiler_params=_params("parallel"),
    )(dy, zz, zz)


def _loss_head(h, g, target):
    N, D = h.shape
    tm = _div(N, 256, SUBLANES)

    def body(h_ref, g_ref, t_ref, loss_ref, dh_ref, dg_ref):
        i = pl.program_id(0)
        x = h_ref[...]
        gv = g_ref[...]
        rstd = lax.rsqrt(jnp.mean(x * x, axis=-1, keepdims=True) + EPS)
        xn = x * rstd
        err = xn * gv - t_ref[...]
        part = 0.5 * jnp.sum(jnp.sum(err * err, axis=-1, keepdims=True) / D, axis=0, keepdims=True)
        dy = err / D
        dxn = dy * gv
        dh_ref[...] = rstd * (dxn - xn * jnp.mean(dxn * xn, axis=-1, keepdims=True))
        dg_t = jnp.sum(dy * xn, axis=0, keepdims=True)
        part = jnp.broadcast_to(part, loss_ref.shape)

        @pl.when(i == 0)
        def _():
            loss_ref[...] = part
            dg_ref[...] = dg_t

        @pl.when(i != 0)
        def _():
            loss_ref[...] += part
            dg_ref[...] += dg_t

    tok = pl.BlockSpec((tm, D), lambda i: (i, 0))
    vec = pl.BlockSpec((1, D), lambda i: (0, 0))
    return pl.pallas_call(
        body, name="loss_head", grid=(N // tm,), in_specs=[tok, vec, tok],
        out_specs=[pl.BlockSpec((SUBLANES, LANES), lambda i: (0, 0)), tok, vec],
        out_shape=[jax.ShapeDtypeStruct((SUBLANES, LANES), F32), jax.ShapeDtypeStruct((N, D), F32),
                   jax.ShapeDtypeStruct((1, D), F32)],
        compiler_params=_params("arbitrary"),
    )(h, g, target)


def _swap_halves(x):
    half = x.shape[-1] // 2
    return jnp.concatenate([x[:, half:], x[:, :half]], axis=1)


def _gelu(y):
    return jax.nn.gelu(y)


def _gelu_grad(y):
    c0 = math.sqrt(2.0 / math.pi)
    inner = c0 * (y + 0.044715 * y * y * y)
    t = jnp.tanh(inner)
    return 0.5 * (1.0 + t) + 0.5 * y * (1.0 - t * t) * c0 * (1.0 + 3.0 * 0.044715 * y * y)


def _s5_discretize(lam_re, lam_im, log_dt, b_re, b_im, c_re, c_im):
    G = lam_re.shape[0]
    nch = G // CHUNK_GROUPS
    dt = jnp.exp(log_dt)[:, None]
    er = jnp.exp(lam_re * dt)
    a_re = er * jnp.cos(lam_im * dt)
    a_im = er * jnp.sin(lam_im * dt)
    den = lam_re * lam_re + lam_im * lam_im
    n_re, n_im = a_re - 1.0, a_im
    f_re = (n_re * lam_re + n_im * lam_im) / den
    f_im = (n_im * lam_re - n_re * lam_im) / den
    bb_re = f_re[..., None] * b_re - f_im[..., None] * b_im
    bb_im = f_re[..., None] * b_im + f_im[..., None] * b_re
    eye = jnp.eye(CHUNK_GROUPS, dtype=F32)

    def pack_b(bb):
        bb = bb.reshape(nch, CHUNK_GROUPS, SSM_STATE, SSM_GROUP)
        return jnp.einsum("jgpc,gh->jgchp", bb, eye).reshape(nch, LANES, CHUNK_STATE)

    def pack_c(cc):
        cc = cc.reshape(nch, CHUNK_GROUPS, SSM_GROUP, SSM_STATE)
        return jnp.einsum("jgcp,gh->jgphc", cc, eye).reshape(nch, CHUNK_STATE, LANES)

    bd = jnp.concatenate([pack_b(bb_re), pack_b(bb_im)], axis=2)
    cd = jnp.concatenate([pack_c(c_re), pack_c(-c_im)], axis=1)
    return bd, cd, a_re, a_im


S5_TILE = 2048
S5_SEG = S5_TILE // SUBLANES
S5_UNROLL = 8


def _s5_scan_coefs(lam_re, lam_im, log_dt, seg):
    G = lam_re.shape[0]
    nch = G // CHUNK_GROUPS
    dt = jnp.exp(log_dt)[:, None]
    rate = (lam_re * dt).reshape(nch, 1, CHUNK_STATE)
    freq = (lam_im * dt).reshape(nch, 1, CHUNK_STATE)

    def powers(ks):
        k = jnp.asarray(ks, F32)[None, :, None]
        er = jnp.exp(k * rate)
        re, im = er * jnp.cos(k * freq), er * jnp.sin(k * freq)
        return jnp.concatenate([re, re], axis=2), jnp.concatenate([-im, im], axis=2)

    pw = jnp.stack(powers(range(1, seg + 1)), axis=1)
    steps = (1, 2, 4)
    re, im = powers([s * seg for s in steps])
    row = jnp.arange(SUBLANES, dtype=jnp.int32)[None, None, :, None]
    shift = jnp.asarray(steps, jnp.int32)[None, :, None, None]

    def table(reverse):
        mask = (row < SUBLANES - shift) if reverse else (row >= shift)
        pair = jnp.stack([jnp.where(mask, re[:, :, None, :], 0.0),
                          jnp.where(mask, (-im if reverse else im)[:, :, None, :], 0.0)], axis=2)
        return pair.reshape(nch, 2 * len(steps), SUBLANES, 2 * CHUNK_STATE)

    return pw, table(False), table(True)


def _to_segments(dst_s, src_ref, seg):
    for j in range(SUBLANES):
        dst_s[pl.ds(j, seg, stride=SUBLANES), :] = src_ref[pl.ds(j * seg, seg), :].astype(F32)


def _from_segments(dst_ref, src_s, seg):
    for j in range(SUBLANES):
        dst_ref[pl.ds(j * seg, seg), :] = src_s[pl.ds(j, seg, stride=SUBLANES), :].astype(dst_ref.dtype)


def _seg_scan(x_ref, pw_ref, seg_ref, carry_ref, c_ref, seg, reverse):
    W = x_ref.shape[-1]
    tm = x_ref.shape[0]
    sgn = -1.0 if reverse else 1.0
    ar = jnp.broadcast_to(pw_ref[0, 0:1, :], (SUBLANES, W))
    ai = sgn * jnp.broadcast_to(pw_ref[1, 0:1, :], (SUBLANES, W))

    def rows(i):
        return pl.ds(pl.multiple_of(i * SUBLANES, SUBLANES), SUBLANES)

    def step(t, prev):
        i = (seg - 2 - t) if reverse else (t + 1)
        x = x_ref[rows(i), :] + ar * prev + ai * _swap_halves(prev)
        x_ref[rows(i), :] = x
        return x

    start = (seg - 1) * SUBLANES if reverse else 0
    edge = lax.fori_loop(0, seg - 1, step, x_ref[start:start + SUBLANES, :], unroll=S5_UNROLL)
    row = lax.broadcasted_iota(jnp.int32, (SUBLANES, W), 0)
    if reverse:
        f = jnp.where(row == SUBLANES - 1, carry_ref[...], pltpu.roll(edge, SUBLANES - 1, 0))
    else:
        f = jnp.where(row == 0, carry_ref[...], pltpu.roll(edge, 1, 0))
    for si, s in enumerate((1, 2, 4)):
        fs = pltpu.roll(f, (SUBLANES - s) if reverse else s, 0)
        f = f + seg_ref[2 * si] * fs + seg_ref[2 * si + 1] * _swap_halves(fs)
    c_ref[...] = f
    fsw = _swap_halves(f)

    def fix(i, _):
        k = (seg - 1 - i) if reverse else i
        x_ref[rows(i), :] = x_ref[rows(i), :] + pw_ref[0, pl.ds(k, 1), :] * f + (sgn * pw_ref[1, pl.ds(k, 1), :]) * fsw
        return 0

    lax.fori_loop(0, seg, fix, 0, unroll=S5_UNROLL)
    leaving = x_ref[0:1, :] if reverse else x_ref[tm - 1:tm, :]
    carry_ref[...] = jnp.broadcast_to(leaving, carry_ref.shape)


def _s5_fwd(u, bd, cd, pw, seg_f, d_skip, B, carry=None):
    N, D = u.shape
    S = N // B
    nch = D // LANES
    W = 2 * CHUNK_STATE
    tm, seg = S5_TILE, S5_SEG
    nt = S // tm

    def body(u_ref, bd_ref, cd_ref, pw_ref, seg_ref, d_ref, z_ref, cin_ref, x_s, carry, c_s, u_s, z_s):
        t = pl.program_id(2)

        @pl.when(t == 0)
        def _():
            carry[...] = jnp.zeros_like(carry)

        cin_ref[...] = carry[...]
        _to_segments(u_s, u_ref, seg)
        uf = u_s[...]
        x_s[...] = jnp.dot(uf.astype(BF16), bd_ref[...], preferred_element_type=F32)
        _seg_scan(x_s, pw_ref, seg_ref, carry, c_s, seg, False)
        y = jnp.dot(x_s[...].astype(BF16), cd_ref[...], preferred_element_type=F32) + d_ref[...] * uf
        z_s[...] = _gelu(y)
        _from_segments(z_ref, z_s, seg)

    (z, carries), moved = _carried_call(
        body, name="s5_fwd", grid=(nch, B, nt),
        in_specs=[pl.BlockSpec((tm, LANES), lambda j, b, t: (b * nt + t, j)),
                  pl.BlockSpec((None, LANES, W), lambda j, b, t: (j, 0, 0)),
                  pl.BlockSpec((None, W, LANES), lambda j, b, t: (j, 0, 0)),
                  pl.BlockSpec((None, 2, seg, W), lambda j, b, t: (j, 0, 0, 0)),
                  pl.BlockSpec((None, 6, SUBLANES, W), lambda j, b, t: (j, 0, 0, 0)),
                  pl.BlockSpec((1, LANES), lambda j, b, t: (0, j))],
        out_specs=[pl.BlockSpec((tm, LANES), lambda j, b, t: (b * nt + t, j)),
                   pl.BlockSpec((None, None, SUBLANES, W), lambda j, b, t: (j, b * nt + t, 0, 0))],
        out_shape=[jax.ShapeDtypeStruct((N, D), BF16), jax.ShapeDtypeStruct((nch, B * nt, SUBLANES, W), F32)],
        scratch_shapes=[pltpu.VMEM((tm, W), F32), pltpu.VMEM((SUBLANES, W), F32), pltpu.VMEM((SUBLANES, W), F32),
                        pltpu.VMEM((tm, LANES), F32), pltpu.VMEM((tm, LANES), F32)],
        operands=(u, bd, cd, pw, seg_f, d_skip), sem=("parallel", "arbitrary", "arbitrary"), carry=carry)
    return z, carries, moved


def _s5_bwd(u, dz, bd, cd, pw, seg_f, seg_b, d_skip, carries, B, carry=None):
    N, D = u.shape
    S = N // B
    nch = D // LANES
    W = 2 * CHUNK_STATE
    tm, seg = S5_TILE, S5_SEG
    nt = S // tm
    tn_dims = (((0,), (0,)), ((), ()))
    nt_dims = (((1,), (1,)), ((), ()))

    def body(u_ref, dz_ref, bd_ref, cd_ref, pw_ref, sf_ref, sb_ref, d_ref, cin_ref,
             du_ref, dbd_ref, dcd_ref, da_ref, dd_ref, x_s, l_s, carry, lcarry, c_s, lc_s, u_s, t_s):
        b = pl.program_id(1)
        t = pl.program_id(2)

        @pl.when((b == 0) & (t == 0))
        def _():
            dbd_ref[...] = jnp.zeros_like(dbd_ref)
            dcd_ref[...] = jnp.zeros_like(dcd_ref)
            da_ref[...] = jnp.zeros_like(da_ref)
            dd_ref[...] = jnp.zeros_like(dd_ref)

        @pl.when(t == 0)
        def _():
            lcarry[...] = jnp.zeros_like(lcarry)

        _to_segments(u_s, u_ref, seg)
        _to_segments(t_s, dz_ref, seg)
        uf = u_s[...]
        uv = uf.astype(BF16)
        carry[...] = cin_ref[...]
        x_s[...] = jnp.dot(uv, bd_ref[...], preferred_element_type=F32)
        _seg_scan(x_s, pw_ref, sf_ref, carry, c_s, seg, False)
        xb = x_s[...].astype(BF16)
        y = jnp.dot(xb, cd_ref[...], preferred_element_type=F32) + d_ref[...] * uf
        dy = t_s[...] * _gelu_grad(y)
        dd_ref[...] += jnp.sum(dy * uf, axis=0, keepdims=True)
        dyb = dy.astype(BF16)
        dcd_ref[...] += lax.dot_general(dyb, xb, tn_dims, preferred_element_type=F32)
        l_s[...] = lax.dot_general(dyb, cd_ref[...], nt_dims, preferred_element_type=F32)
        _seg_scan(l_s, pw_ref, sb_ref, lcarry, lc_s, seg, True)
        lb = l_s[...].astype(BF16)
        dbd_ref[...] += lax.dot_general(uv, lb, tn_dims, preferred_element_type=F32)
        t_s[...] = lax.dot_general(lb, bd_ref[...], nt_dims, preferred_element_type=F32) + d_ref[...] * dy
        _from_segments(du_ref, t_s, seg)
        lam_rest, x_prev = l_s[SUBLANES:, :], x_s[:tm - SUBLANES, :]
        lam_0, c_in = l_s[:SUBLANES, :], c_s[...]
        da_ref[0:1, :] += (jnp.sum(lam_rest * x_prev, axis=0, keepdims=True) + jnp.sum(lam_0 * c_in, axis=0, keepdims=True))
        da_ref[1:2, :] += (jnp.sum(lam_rest * _swap_halves(x_prev), axis=0, keepdims=True)
                           + jnp.sum(lam_0 * _swap_halves(c_in), axis=0, keepdims=True))

    tile = lambda j, b, t: (b * nt + (nt - 1 - t), j)
    chunk3 = lambda j, b, t: (j, 0, 0)
    chunk4 = lambda j, b, t: (j, 0, 0, 0)
    outs, moved = _carried_call(
        body, name="s5_bwd", grid=(nch, B, nt),
        in_specs=[pl.BlockSpec((tm, LANES), tile), pl.BlockSpec((tm, LANES), tile),
                  pl.BlockSpec((None, LANES, W), chunk3), pl.BlockSpec((None, W, LANES), chunk3),
                  pl.BlockSpec((None, 2, seg, W), chunk4), pl.BlockSpec((None, 6, SUBLANES, W), chunk4),
                  pl.BlockSpec((None, 6, SUBLANES, W), chunk4), pl.BlockSpec((1, LANES), lambda j, b, t: (0, j)),
                  pl.BlockSpec((None, None, SUBLANES, W), lambda j, b, t: (j, b * nt + (nt - 1 - t), 0, 0))],
        out_specs=[pl.BlockSpec((tm, LANES), tile), pl.BlockSpec((None, LANES, W), chunk3),
                   pl.BlockSpec((None, LANES, W), chunk3), pl.BlockSpec((None, 2, W), chunk3),
                   pl.BlockSpec((1, LANES), lambda j, b, t: (0, j))],
        out_shape=[jax.ShapeDtypeStruct((N, D), F32), jax.ShapeDtypeStruct((nch, LANES, W), F32),
                   jax.ShapeDtypeStruct((nch, LANES, W), F32), jax.ShapeDtypeStruct((nch, 2, W), F32),
                   jax.ShapeDtypeStruct((1, D), F32)],
        scratch_shapes=[pltpu.VMEM((tm, W), F32), pltpu.VMEM((tm, W), F32)] + [pltpu.VMEM((SUBLANES, W), F32)] * 4
        + [pltpu.VMEM((tm, LANES), F32)] * 2,
        operands=(u, dz, bd, cd, pw, seg_f, seg_b, d_skip, carries), sem=("parallel", "arbitrary", "arbitrary"), carry=carry)
    return (*outs, moved)


ATTN_HEADS = LANES // HEAD_DIM
ATTN_FWD_UNROLL = 8
ATTN_BWD_UNROLL = 8


def _attn_mask(n):
    qi = lax.broadcasted_iota(jnp.int32, (ATTN_HEADS * ATTN_BLOCK, 2 * ATTN_BLOCK), 0) % ATTN_BLOCK
    kj = lax.broadcasted_iota(jnp.int32, (ATTN_HEADS * ATTN_BLOCK, 2 * ATTN_BLOCK), 1)
    prev_ok = (kj < ATTN_BLOCK) & (kj >= qi) & (n > 0)
    return prev_ok | ((kj >= ATTN_BLOCK) & (kj - ATTN_BLOCK <= qi))


def _stack_heads(x):
    return jnp.concatenate([_only_head(x, h) for h in range(ATTN_HEADS)], axis=0)


def _stack_head_columns(x):
    return jnp.concatenate([x[:, h * HEAD_DIM:h * HEAD_DIM + 1] for h in range(ATTN_HEADS)], axis=0)


def _unstack_heads(x):
    return _per_head([x[h * ATTN_BLOCK:(h + 1) * ATTN_BLOCK] for h in range(ATTN_HEADS)])


def _head_lanes(h):
    lane = lax.broadcasted_iota(jnp.int32, (ATTN_BLOCK, LANES), 1)
    return (lane >= h * HEAD_DIM) & (lane < (h + 1) * HEAD_DIM)


def _per_head(cols):
    out = jnp.broadcast_to(cols[-1], (ATTN_BLOCK, LANES))
    for h in range(len(cols) - 2, -1, -1):
        out = jnp.where(_head_lanes(h), jnp.broadcast_to(cols[h], (ATTN_BLOCK, LANES)), out)
    return out


def _only_head(x, h):
    return jnp.where(_head_lanes(h), x, 0.0).astype(BF16)


def _block_rows(tb, dil, nb):
    r = tb // nb
    n = tb % nb
    start = r + dil * ATTN_BLOCK * n
    startp = jnp.where(n > 0, start - dil * ATTN_BLOCK, start)
    return n, pl.ds(start, ATTN_BLOCK, stride=dil), pl.ds(startp, ATTN_BLOCK, stride=dil)


def _attn_fwd(q, k, v, B, carry=None):
    _, S, D3 = q.shape
    D = D3 // 3
    HP = D // LANES
    scale = HEAD_DIM ** -0.5
    n_blocks = S // ATTN_BLOCK
    nbr = len(DILATIONS)
    nt_dims = (((1,), (1,)), ((), ()))

    def branch(dil, q_ref, k_ref, v_ref, acc, m_s, l_s):
        nb = (S // dil) // ATTN_BLOCK

        def blk(tb, _):
            n, rows, rowsp = _block_rows(tb, dil, nb)
            qb = q_ref[rows, :] * scale
            kk = jnp.concatenate([k_ref[rowsp, :], k_ref[rows, :]], axis=0).astype(BF16)
            vv = jnp.concatenate([v_ref[rowsp, :], v_ref[rows, :]], axis=0).astype(BF16)
            s = lax.dot_general(_stack_heads(qb), kk, nt_dims, preferred_element_type=F32)
            s = jnp.where(_attn_mask(n), s, NEG)
            m = jnp.max(s, axis=-1, keepdims=True)
            p = jnp.exp(s - m)
            m_s[rows, :] = _unstack_heads(m)
            l_s[rows, :] = _unstack_heads(jnp.sum(p, axis=-1, keepdims=True))
            acc[rows, :] = _unstack_heads(jnp.dot(p.astype(BF16), vv, preferred_element_type=F32))
            return 0

        lax.fori_loop(0, n_blocks, blk, 0, unroll=ATTN_FWD_UNROLL)

    def body(q_ref, k_ref, v_ref, o_ref, lse_ref, *scratch):
        accs, m_ss, l_ss = scratch[:nbr], scratch[nbr:2 * nbr], scratch[2 * nbr:]
        g = pl.program_id(2)
        for gi, dil in enumerate(DILATIONS):
            pl.when(g == gi)(functools.partial(branch, dil, q_ref, k_ref, v_ref, accs[gi], m_ss[gi], l_ss[gi]))

        @pl.when(g == nbr - 1)
        def _():
            def fin(i, _):
                rows = pl.ds(pl.multiple_of(i * ATTN_BLOCK, ATTN_BLOCK), ATTN_BLOCK)
                ms = [m[rows, :] for m in m_ss]
                m_all = functools.reduce(jnp.maximum, ms)
                ws = [jnp.exp(m - m_all) for m in ms]
                den = sum(w * l[rows, :] for w, l in zip(ws, l_ss))
                o_ref[rows, :] = sum(w * a[rows, :] for w, a in zip(ws, accs)) / den
                lse_ref[rows, :] = m_all + jnp.log(den)
                return 0

            lax.fori_loop(0, n_blocks, fin, 0)

    br = pl.BlockSpec((None, S, LANES), lambda b, hp, g: (b, 0, g * HP + hp))
    hd = pl.BlockSpec((None, S, LANES), lambda b, hp, g: (b, 0, hp))
    (o, lse), moved = _carried_call(
        body, name="attn_fwd", grid=(B, HP, nbr), in_specs=[br, br, br], out_specs=[hd, hd],
        out_shape=[jax.ShapeDtypeStruct((B, S, D), F32), jax.ShapeDtypeStruct((B, S, D), F32)],
        scratch_shapes=[pltpu.VMEM((S, LANES), F32)] * (3 * nbr),
        operands=(q, k, v), sem=("parallel", "parallel", "arbitrary"), carry=carry)
    return o, lse, moved


def _attn_bwd(q, k, v, o, lse, do, dk_prev, dv_prev, B, last, carry=None):
    _, S, D3 = q.shape
    D = D3 // 3
    HP = D // LANES
    scale = HEAD_DIM ** -0.5
    n_blocks = S // ATTN_BLOCK
    has_prev = dk_prev is not None
    nt_dims = (((1,), (1,)), ((), ()))
    tn_dims = (((0,), (0,)), ((), ()))

    def branch(dil, q_ref, k_ref, v_ref, lse_ref, do_ref, dq_s, dk_c, dv_c, delta, dk_p, dv_p):
        nb = (S // dil) // ATTN_BLOCK

        def blk(tb, _):
            n, rows, rowsp = _block_rows(tb, dil, nb)
            qb = q_ref[rows, :] * scale
            dob, lb, db = do_ref[rows, :], lse_ref[rows, :], delta[rows, :]
            kk = jnp.concatenate([k_ref[rowsp, :], k_ref[rows, :]], axis=0).astype(BF16)
            vv = jnp.concatenate([v_ref[rowsp, :], v_ref[rows, :]], axis=0).astype(BF16)
            qs, dos = _stack_heads(qb), _stack_heads(dob)
            s = lax.dot_general(qs, kk, nt_dims, preferred_element_type=F32)
            p = jnp.where(_attn_mask(n), jnp.exp(s - _stack_head_columns(lb)), 0.0)
            dp = lax.dot_general(dos, vv, nt_dims, preferred_element_type=F32)
            ds = (p * (dp - _stack_head_columns(db))).astype(BF16)
            dkk = lax.dot_general(ds, qs, tn_dims, preferred_element_type=F32)
            dvv = lax.dot_general(p.astype(BF16), dos, tn_dims, preferred_element_type=F32)
            dq_s[rows, :] = _unstack_heads(jnp.dot(ds, kk, preferred_element_type=F32)) * scale
            dk_p[rowsp, :] = dkk[:ATTN_BLOCK]
            dv_p[rowsp, :] = dvv[:ATTN_BLOCK]
            dk_c[rows, :] = dkk[ATTN_BLOCK:]
            dv_c[rows, :] = dvv[ATTN_BLOCK:]
            return 0

        lax.fori_loop(0, n_blocks, blk, 0, unroll=ATTN_BWD_UNROLL)

    def body(*refs):
        q_ref, k_ref, v_ref, o_ref, lse_ref, do_ref = refs[:6]
        n_in = 8 if has_prev else 6
        dq_ref, dk_ref, dv_ref, delta, dk_p, dv_p, dq_s, dk_c, dv_c = refs[n_in:n_in + 9]
        g = pl.program_id(2)

        @pl.when(g == 0)
        def _():
            def dl(i, _):
                rows = pl.ds(pl.multiple_of(i * ATTN_BLOCK, ATTN_BLOCK), ATTN_BLOCK)
                prod = do_ref[rows, :] * o_ref[rows, :]
                delta[rows, :] = _per_head([jnp.sum(jnp.where(_head_lanes(h), prod, 0.0), axis=-1, keepdims=True)
                                            for h in range(ATTN_HEADS)])
                return 0

            lax.fori_loop(0, n_blocks, dl, 0)

        dk_p[...] = jnp.zeros_like(dk_p)
        dv_p[...] = jnp.zeros_like(dv_p)
        for gi, dil in enumerate(DILATIONS):
            pl.when(g == gi)(functools.partial(branch, dil, q_ref, k_ref, v_ref, lse_ref, do_ref, dq_s, dk_c, dv_c,
                                               delta, dk_p, dv_p))

        def fin(i, _):
            rows = pl.ds(pl.multiple_of(i * ATTN_BLOCK, ATTN_BLOCK), ATTN_BLOCK)
            dk_t = dk_c[rows, :] + dk_p[rows, :]
            dv_t = dv_c[rows, :] + dv_p[rows, :]
            if has_prev:
                dk_t = dk_t + refs[6][rows, :].astype(F32)
                dv_t = dv_t + refs[7][rows, :].astype(F32)
            dq_ref[rows, :] = dq_s[rows, :].astype(dq_ref.dtype)
            dk_ref[rows, :] = dk_t.astype(dk_ref.dtype)
            dv_ref[rows, :] = dv_t.astype(dv_ref.dtype)
            return 0

        lax.fori_loop(0, n_blocks, fin, 0)

    br = pl.BlockSpec((None, S, LANES), lambda b, hp, g: (b, 0, g * HP + hp))
    hd = pl.BlockSpec((None, S, LANES), lambda b, hp, g: (b, 0, hp))
    ins = [q, k, v, o, lse, do] + ([dk_prev, dv_prev] if has_prev else [])
    kv_dtype = BF16 if last else F32
    (dq, dk, dv), moved = _carried_call(
        body, name="attn_bwd", grid=(B, HP, len(DILATIONS)),
        in_specs=[br, br, br, hd, hd, hd] + ([br, br] if has_prev else []), out_specs=[br, br, br],
        out_shape=[jax.ShapeDtypeStruct(q.shape, BF16), jax.ShapeDtypeStruct(q.shape, kv_dtype),
                   jax.ShapeDtypeStruct(q.shape, kv_dtype)],
        scratch_shapes=[pltpu.VMEM((S, LANES), F32)] * 6,
        operands=ins, sem=("parallel", "parallel", "arbitrary"), carry=carry)
    return dq, dk, dv, moved


def _adamw(w, grads, m, v):
    R, C = w.shape
    tr = _div(R, 256, SUBLANES)
    ng = len(grads)
    c1 = 1.0 - ADAM_B1 ** ADAM_STEP
    c2 = 1.0 - ADAM_B2 ** ADAM_STEP

    def body(*refs):
        w_ref, m_ref, v_ref = refs[0], refs[1 + ng], refs[2 + ng]
        d_ref, mo_ref, vo_ref = refs[3 + ng:6 + ng]
        g = refs[1][...]
        if ng == 2:
            g = g + refs[2][...]
            refs[6 + ng][...] = g
        mn = ADAM_B1 * m_ref[...] + (1.0 - ADAM_B1) * g
        vn = ADAM_B2 * v_ref[...] + (1.0 - ADAM_B2) * (g * g)
        d_ref[...] = -ADAM_LR * ((mn / c1) / (jnp.sqrt(vn / c2) + ADAM_EPS) + ADAM_WD * w_ref[...])
        mo_ref[...] = mn
        vo_ref[...] = vn

    blk = pl.BlockSpec((tr, C), lambda i: (i, 0))
    n_out = 3 + (ng == 2)
    outs = pl.pallas_call(
        body, name="adamw", grid=(R // tr,), in_specs=[blk] * (3 + ng), out_specs=[blk] * n_out,
        out_shape=[jax.ShapeDtypeStruct((R, C), F32)] * n_out, compiler_params=_params("parallel"),
    )(w, *grads, m, v)
    return (outs[3] if ng == 2 else grads[0],) + tuple(outs[:3])


def _sum_shards(recv):
    n, R, C = recv.shape
    tr = _div(R, 256, SUBLANES if recv.dtype == F32 else 2 * SUBLANES)

    def body(r_ref, o_ref):
        s = r_ref[0].astype(F32)
        for i in range(1, n):
            s = s + r_ref[i].astype(F32)
        o_ref[...] = s

    return pl.pallas_call(
        body, name="sum_shards", grid=(R // tr,), in_specs=[pl.BlockSpec((n, tr, C), lambda i: (0, i, 0))],
        out_specs=pl.BlockSpec((tr, C), lambda i: (i, 0)), out_shape=jax.ShapeDtypeStruct((R, C), F32),
        compiler_params=_params("parallel"),
    )(recv)


N_DEV = 8
N_CHIPS = 4


def _all_gather_small(x, carry=None):
    m_per, n = x.shape

    def body(x_ref, out_ref, send_sems, recv_sems, local_sem):
        cx, cy, cc = _coords()
        me, sibling = (cx, cy, cc), (cx, cy, 1 - cc)
        chips = [(1 - cx, cy), (cx, 1 - cy), (1 - cx, 1 - cy)]

        def rows(px, py, pc):
            return out_ref.at[pl.ds((4 * px + 2 * py + pc) * m_per, m_per), :]

        def copy(k, block, to, src=None):
            return pltpu.make_async_remote_copy(
                src_ref=rows(*block) if src is None else src, dst_ref=rows(*block), send_sem=send_sems.at[k],
                recv_sem=recv_sems.at[k], device_id=to, device_id_type=MESH)

        mine = pltpu.make_async_copy(x_ref, rows(*me), local_sem)
        mine.start()
        first = [copy(0, me, sibling, src=x_ref)]
        first += [copy(1 + j, me, (*chip, cc), src=x_ref) for j, chip in enumerate(chips)]
        for cp in first:
            cp.start()
        passed = [copy(4 + j, (*chip, cc), sibling) for j, chip in enumerate(chips)]
        for j, chip in enumerate(chips):
            copy(1 + j, (*chip, cc), me).wait_recv()
            passed[j].start()
        copy(0, sibling, me).wait_recv()
        for j, chip in enumerate(chips):
            copy(4 + j, (*chip, 1 - cc), me).wait_recv()
        for cp in first + passed:
            cp.wait_send()
        mine.wait()

    (out,), moved = _carried_call(
        body, name="all_gather_small", grid=(1,), out_shape=[jax.ShapeDtypeStruct((N_DEV * m_per, n), x.dtype)],
        in_specs=[pl.BlockSpec(memory_space=pltpu.VMEM)], out_specs=[pl.BlockSpec(memory_space=pltpu.VMEM)],
        scratch_shapes=[pltpu.SemaphoreType.DMA((7,)), pltpu.SemaphoreType.DMA((7,)), pltpu.SemaphoreType.DMA],
        operands=(x,), sem=("arbitrary",), carry=carry)
    return out, moved


def _layer_moves(kind, arrays_from, arrays_to, pieces, layer_major=()):
    used = sorted({w for w, _ in pieces})
    pos = {w: i for i, w in enumerate(used)}
    gather = kind == "gather"

    def half(ref, c):
        rows = ref.shape[0] // 2
        return ref.at[pl.ds(c * rows, rows), :]

    def slot(d, w, chip, l):
        return d.at[l, chip] if w in layer_major else d.at[chip, l]

    def plan(src_refs, dst_refs, me):
        cx, cy, cc = me
        mine = 2 * cx + cy
        remote, local = [], []
        for w, l in pieces:
            s, d = src_refs[pos[w]], dst_refs[pos[w]]
            for px, py in _other_chips(cx, cy):
                if gather:
                    remote.append((half(s.at[l], cc), half(slot(d, w, mine, l), cc), (px, py, cc)))
                else:
                    remote.append((s.at[2 * px + py, l], d.at[mine, l], (px, py, cc)))
            local.append((s.at[l], slot(d, w, mine, l)) if gather else (s.at[mine, l], d.at[mine, l]))
        return remote, local

    def onward(src_refs, dst_refs, me):
        cx, cy, cc = me
        moves = []
        for w, l in pieces:
            d = dst_refs[pos[w]]
            for px, py in _other_chips(cx, cy):
                landed = half(slot(d, w, 2 * px + py, l), cc)
                moves.append((landed, landed, (cx, cy, 1 - cc)))
        return moves

    n = 3 * len(pieces)
    carry = _Carry([arrays_from[w] for w in used], [arrays_to[w] for w in used], plan, n, len(pieces),
                   onward if gather else None, n if gather else 0)
    return carry, used


def _swap_with_sibling(sums):
    def plan(src_refs, dst_refs, me):
        cx, cy, cc = me
        return [(s, d, (cx, cy, 1 - cc)) for s, d in zip(src_refs, dst_refs)], []

    return _Carry(sums, [lax.empty(s.shape, s.dtype) for s in sums], plan, len(sums), 0)


def _pack(arrs, width):
    parts, layout, row = [], [], 0
    for a in arrs:
        flat = a.reshape(-1).astype(F32)
        rows = -(-flat.shape[0] // (width * SUBLANES)) * SUBLANES
        parts.append(jnp.pad(flat, (0, rows * width - flat.shape[0])).reshape(rows, width))
        layout.append((row, rows, a.shape))
        row += rows
    pad = -row % (8 * SUBLANES) if row > 8 * SUBLANES else 0
    if pad:
        parts.append(jnp.zeros((pad, width), F32))
    return jnp.concatenate(parts, axis=0), layout, row + pad


def _unpack(buf, layout, idx):
    row, rows, shape = layout[idx]
    size = math.prod(shape)
    return buf[row:row + rows].reshape(-1)[:size].reshape(shape)


def kernel(x, c, ln_g, ada_w, ada_b, ssm_lam_re, ssm_lam_im, ssm_log_dt, ssm_b_re, ssm_b_im, ssm_c_re, ssm_c_im, ssm_d, ssm_w_glu, kv_g, kv_ada_w, kv_ada_b, w_kv, attn_w_q, attn_w_o, mlp_w1, mlp_w2, final_g, loss_target, m_ln_g, m_ada_w, m_ada_b, m_ssm_lam_re, m_ssm_lam_im, m_ssm_log_dt, m_ssm_b_re, m_ssm_b_im, m_ssm_c_re, m_ssm_c_im, m_ssm_d, m_ssm_w_glu, m_kv_g, m_kv_ada_w, m_kv_ada_b, m_w_kv, m_attn_w_q, m_attn_w_o, m_mlp_w1, m_mlp_w2, m_final_g, v_ln_g, v_ada_w, v_ada_b, v_ssm_lam_re, v_ssm_lam_im, v_ssm_log_dt, v_ssm_b_re, v_ssm_b_im, v_ssm_c_re, v_ssm_c_im, v_ssm_d, v_ssm_w_glu, v_kv_g, v_kv_ada_w, v_kv_ada_b, v_w_kv, v_attn_w_q, v_attn_w_o, v_mlp_w1, v_mlp_w2, v_final_g):
    B, S, D = x.shape
    N = B * S
    depth = ln_g.shape[0]
    n_a = ssm_w_glu.shape[0]
    n_b = attn_w_q.shape[0]
    FF = mlp_w1.shape[2] * N_CHIPS
    cx, cy, cc = _coords()
    chip = 2 * cx + cy
    dev = 4 * cx + 2 * cy + cc
    n_ex = N_DEV * B
    ada_cols = ada_w.shape[-1]
    kv_cols = kv_ada_w.shape[-1]

    GLU, KV, Q, O, W1, W2 = range(6)
    shards = [ssm_w_glu.astype(BF16), w_kv.astype(BF16)[None], attn_w_q.astype(BF16), attn_w_o.astype(BF16),
              mlp_w1.astype(BF16), mlp_w2.astype(BF16)]
    row_sharded = (O, W2)
    wg = [lax.empty((s.shape[0], N_CHIPS) + s.shape[1:] if w in row_sharded else (N_CHIPS,) + s.shape, BF16)
          for w, s in enumerate(shards)]

    def whole_rows(w):
        L, _, R, C = wg[w].shape
        return wg[w].reshape(1, L, N_CHIPS * R, C)

    def landed(arrays, used, moved):
        for w, a in zip(used, moved):
            arrays[w] = a

    fetch_with = {}

    def carried_by(kind, l, *pieces):
        fetch_with.setdefault((kind, l), []).extend(pieces)

    assert n_a >= 1 and n_b >= 1, (n_a, n_b)
    carried_by("mixer", 0, *[(GLU, l) for l in range(n_a)], (W1, 0))
    carried_by("glu_proj", 0, (W2, 0))
    carried_by("mlp_up", 0, (Q, 0))
    carried_by("mlp_up", n_a - 1, (O, 0))
    carried_by("mixer", n_a - 1, (KV, 0))
    for l in range(1, depth):
        if l > n_a:
            carried_by("mixer", l, (W1, l))
        else:
            carried_by("mlp_down", l - 1, (W1, l))
        carried_by("mixer", l, (W2, l))
    for j in range(1, n_b):
        carried_by("mixer", n_a + j - 1, (Q, j), (O, j))

    def fetch(kind, l):
        pieces = fetch_with.get((kind, l))
        if not pieces:
            return None, []
        return _layer_moves("gather", shards, wg, pieces, layer_major=row_sharded)

    def mm_carrying(kind, l, *args, **kw):
        carry, used = fetch(kind, l)
        if carry is None:
            return _mm(kind, *args, **kw)
        out, moved = _mm(kind, *args, carry=carry, **kw)
        landed(wg, used, moved)
        return out

    c_pack, c_layout, _ = _pack([c], D)
    c_all_buf, _ = _all_gather_small(c_pack)
    c_rows = c_pack.shape[0]
    c_all = jnp.concatenate([_unpack(c_all_buf[d * c_rows:(d + 1) * c_rows], c_layout, 0) for d in range(N_DEV)], axis=0)
    sc_all = jax.nn.silu(c_all).astype(BF16)
    n_mod = depth * 2
    ada_w8 = ada_w.reshape(n_mod, 1, D, ada_cols)
    ada_b_row = ada_b.reshape(1, n_mod * ada_cols)
    mod_local = _mm("ada_fwd", sc_all, ada_w8, mode="nn", M=n_ex, N=n_mod * ada_cols, K=D, b_lay="cs", b_ns=n_mod,
                    epi=_add, extras=[("n", ada_b_row)])
    kv_ada_b_local = lax.dynamic_slice(kv_ada_b.reshape(N_CHIPS, kv_cols), (chip, 0), (1, kv_cols))
    kvmod_local = _mm("ada_fwd", sc_all, _as4(kv_ada_w), mode="nn", M=n_ex, N=kv_cols, K=D, epi=_add,
                      extras=[("n", kv_ada_b_local)])
    mod_pack, mod_layout, mod_rows = _pack([mod_local, kvmod_local, ln_g, ssm_d], D)
    mod_buf, _ = _all_gather_small(mod_pack)

    def from_chip(j, idx):
        d = 2 * j
        return _unpack(mod_buf[d * mod_rows:(d + 1) * mod_rows], mod_layout, idx)

    my_rows = lambda a: lax.dynamic_slice_in_dim(a, dev * B, B, axis=0)
    mods = jnp.concatenate([my_rows(from_chip(j, 0)).reshape(B, n_mod, ada_cols) for j in range(N_CHIPS)], axis=2)
    kvmod = jnp.concatenate([my_rows(from_chip(j, 1)) for j in range(N_CHIPS)], axis=1)
    ln_g_full = jnp.concatenate([from_chip(j, 2) for j in range(N_CHIPS)], axis=2)
    ssm_d_full = jnp.concatenate([from_chip(j, 3) for j in range(N_CHIPS)], axis=1)

    def mod3(l, s):
        mrow = mods[:, l * 2 + s]
        return [mrow[:, i * D:(i + 1) * D].reshape(B, 1, D) for i in range(3)]

    kv_shift, kv_scale = kvmod[:, :D].reshape(B, 1, D), kvmod[:, D:].reshape(B, 1, D)

    s5_tabs = []
    for l in range(n_a):
        prm = (ssm_lam_re[l], ssm_lam_im[l], ssm_log_dt[l], ssm_b_re[l], ssm_b_im[l], ssm_c_re[l], ssm_c_im[l])
        (bd, cd, _, _), disc_vjp = jax.vjp(_s5_discretize, *prm)
        pw, seg_f, seg_b = _s5_scan_coefs(ssm_lam_re[l], ssm_lam_im[l], ssm_log_dt[l], S5_SEG)
        s5_tabs.append((bd.astype(BF16), cd.astype(BF16), pw, seg_f, seg_b, disc_vjp))

    h = x.reshape(N, D)
    saved = []
    k_all = v_all = None
    shift, scale, gate = mod3(0, 0)
    u = _normmod(h, ln_g_full[0, 0].reshape(1, D), scale, shift, B)
    for l in range(depth):
        sv = {}
        sv["h0"], sv["scale0"], sv["gate0"], sv["u0"] = h, scale, gate, u
        shift1, scale1, gate1 = mod3(l, 1)
        norm1 = [("n", ln_g_full[l, 1].reshape(1, D)), ("ex", scale1), ("ex", shift1)]
        carry, used = fetch("mixer", l)
        if l < n_a:
            bd, cd, pw, seg_f, _, _ = s5_tabs[l]
            z, carries, moved = _s5_fwd(u, bd, cd, pw, seg_f, ssm_d_full[l].reshape(1, D), B, carry)
            landed(wg, used, moved)
            zz = mm_carrying("glu_proj", l, z, wg[GLU], mode="nn", M=N, N=2 * D, K=D, b_lay="cs", b_l=l, b_ns=N_CHIPS)
            y, h, u = _glu_residual_norm(zz, h, gate, ln_g_full[l, 1].reshape(1, D), scale1, shift1, B)
            sv["z"], sv["carries"], sv["zz"] = z, carries, zz
        else:
            j = l - n_a
            q = _mm("q_proj", u, wg[Q], mode="nn", M=N, N=3 * D, K=D, b_lay="cs", b_l=j, b_ns=N_CHIPS)
            q3 = q.reshape(B, S, 3 * D)
            o, lse, moved = _attn_fwd(q3, k_all, v_all, B, carry)
            landed(wg, used, moved)
            o2 = o.reshape(N, D)
            y, h, u = _mm("o_proj", o2, whole_rows(O), mode="nn", M=N, N=D, K=D, b_l=j, tm=1024, out_dtype=(BF16, F32, BF16),
                          epi=_gated_residual_norm, extras=[("mn", h), ("ex", gate)] + norm1, rows_per_ex=S)
            sv["q"], sv["o"], sv["lse"] = q3, o, lse
        sv["y0"] = y
        sv["h1"], sv["scale1"], sv["gate1"], sv["u1"] = h, scale1, gate1, u
        r = mm_carrying("mlp_up", l, u, wg[W1], mode="nn", M=N, N=FF, K=D, b_lay="cs", b_l=l, b_ns=N_CHIPS,
                        out_dtype=BF16, epi=_relu2)
        if l + 1 < depth:
            shift, scale, gate = mod3(l + 1, 0)
            y, h, u = mm_carrying(
                "mlp_down", l, r, whole_rows(W2), mode="nn", M=N, N=D, K=FF, b_l=l, tk=2048, tm=512,
                out_dtype=(BF16, F32, BF16), epi=_gated_residual_norm, rows_per_ex=S,
                extras=[("mn", h), ("ex", gate1), ("n", ln_g_full[l + 1, 0].reshape(1, D)), ("ex", scale), ("ex", shift)])
        else:
            y, h = mm_carrying("mlp_down", l, r, whole_rows(W2), mode="nn", M=N, N=D, K=FF, b_l=l, tk=2048, tm=1024,
                               out_dtype=(BF16, F32), epi=_gated_residual, extras=[("mn", h), ("ex", gate1)], rows_per_ex=S)
        sv["r"], sv["y1"] = r, y
        saved.append(sv)
        if l == n_a - 1:
            h_kv = h
            u_kv = _normmod(h, kv_g.reshape(1, D), kv_scale, kv_shift, B)
            half = N_CHIPS // 2
            k_all = _mm("kv_proj", u_kv, wg[KV], mode="nn", M=N, N=3 * D, K=D, b_lay="cs", b_s0=0, b_ns=half).reshape(B, S, 3 * D)
            v_all = _mm("kv_proj", u_kv, wg[KV], mode="nn", M=N, N=3 * D, K=D, b_lay="cs", b_s0=half, b_ns=half).reshape(B, S, 3 * D)

    loss_buf, dh, d_final_g = _loss_head(h, final_g.reshape(1, D), loss_target.reshape(N, D))
    loss = lax.psum(loss_buf[0, 0], ("x", "y", "c"))

    dg = [lax.empty((N_CHIPS,) + s.shape, BF16) for s in shards]
    recv = [lax.empty((N_CHIPS,) + s.shape, BF16) for s in shards]

    def send(pieces):
        return _layer_moves("scatter", dg, recv, pieces)

    send_with = {l: [(W1, l), (W2, l)] for l in range(depth)}
    for l in range(n_a):
        send_with[l] += [(GLU, l)]
    for j in range(n_b):
        send_with[n_a + j] += [(O, j)]
        send_with[n_a + j - 1] += [(Q, j)]
    send_with[n_a - 1] += [(KV, 0)]
    d_ln_g = [[None, None] for _ in range(depth)]
    d_mods = [[None, None] for _ in range(depth)]
    d_s5 = [None] * n_a
    dk_acc = dv_acc = None
    half = N_CHIPS // 2

    def tn_grad(name, a, d, into, l, Mr, Nc, lay, s0=0, ns=N_CHIPS):
        return _mm(name, a, _as4(d), mode="tn", M=Mr, N=Nc, K=N, b_lay="cs", out_dtype=BF16, out_lay=lay,
                   out4_shape=into.shape, out_into=into, out_l=l, out_s0=s0, out_ns=ns, tk=2048)

    dy, d_gate1 = _residual_bwd(dh, saved[-1]["gate1"], saved[-1]["y1"], B)
    for l in reversed(range(depth)):
        sv = saved[l]
        dg[W2] = tn_grad("mlp_down_dw", sv["r"], dy, dg[W2], l, FF, D, "rs")
        da = _mm("mlp_down_dx", dy, whole_rows(W2), mode="nt", M=N, N=FF, K=D, b_l=l, out_dtype=BF16,
                 epi=_relu2_bwd, extras=[("mn", sv["r"])])
        dg[W1] = tn_grad("mlp_up_dw", sv["u1"], da, dg[W1], l, D, FF, "cs")
        du = _mm("mlp_up_dx", da, wg[W1], mode="nt", M=N, N=D, K=FF, b_lay="cs", b_l=l, b_ns=N_CHIPS)
        dh, dgv, d_scale1, d_shift1, dy, d_gate0 = _normmod_bwd(du, sv["h1"], ln_g_full[l, 1].reshape(1, D), sv["scale1"],
                                                                dh, B, below=(sv["y0"], sv["gate0"]))
        d_ln_g[l][1] = dgv
        d_mods[l][1] = jnp.concatenate([d_shift1, d_scale1, d_gate1], axis=2)
        if l < n_a:
            bd, cd, pw, seg_f, seg_b, disc_vjp = s5_tabs[l]
            dzz = _glu_bwd(dy, sv["zz"])
            dg[GLU] = tn_grad("glu_proj_dw", sv["z"], dzz, dg[GLU], l, D, 2 * D, "cs")
            dz = _mm("glu_proj_dx", dzz, wg[GLU], mode="nt", M=N, N=D, K=2 * D, b_lay="cs", b_l=l, b_ns=N_CHIPS)
            carry, used = send(send_with[l])
            du, d_bd, d_cd, d_a2, d_dskip, moved = _s5_bwd(sv["u0"], dz, bd, cd, pw, seg_f, seg_b,
                                                           ssm_d_full[l].reshape(1, D), sv["carries"], B, carry)
            landed(recv, used, moved)
            d_are = (d_a2[:, 0, :CHUNK_STATE] + d_a2[:, 0, CHUNK_STATE:]).reshape(-1, SSM_STATE)
            d_aim = (d_a2[:, 1, CHUNK_STATE:] - d_a2[:, 1, :CHUNK_STATE]).reshape(-1, SSM_STATE)
            d_s5[l] = disc_vjp((d_bd, jnp.swapaxes(d_cd, 1, 2), d_are, d_aim)) + (d_dskip,)
        else:
            j = l - n_a
            dg[O] = tn_grad("o_proj_dw", sv["o"].reshape(N, D), dy, dg[O], j, D, D, "rs")
            do = _mm("o_proj_dx", dy, whole_rows(O), mode="nt", M=N, N=D, K=D, b_l=j)
            carry, used = send(send_with[l])
            dq, dk_acc, dv_acc, moved = _attn_bwd(sv["q"], k_all, v_all, sv["o"], sv["lse"], do.reshape(B, S, D),
                                                  dk_acc, dv_acc, B, l == n_a, carry)
            landed(recv, used, moved)
            dq2 = dq.reshape(N, 3 * D)
            dg[Q] = tn_grad("q_proj_dw", sv["u0"], dq2, dg[Q], j, D, 3 * D, "cs")
            du = _mm("q_proj_dx", dq2, wg[Q], mode="nt", M=N, N=D, K=3 * D, b_lay="cs", b_l=j, b_ns=N_CHIPS)
        below = (saved[l - 1]["y1"], saved[l - 1]["gate1"]) if l > 0 else None
        dh, dgv, d_scale0, d_shift0, dy, d_gate1 = _normmod_bwd(du, sv["h0"], ln_g_full[l, 0].reshape(1, D), sv["scale0"],
                                                                dh, B, below=None if l == n_a else below)
        d_ln_g[l][0] = dgv
        d_mods[l][0] = jnp.concatenate([d_shift0, d_scale0, d_gate0], axis=2)
        if l == n_a:
            dk2, dv2 = dk_acc.reshape(N, 3 * D), dv_acc.reshape(N, 3 * D)
            dg[KV] = tn_grad("kv_proj_dw", u_kv, dk2, dg[KV], 0, D, 3 * D, "cs", s0=0, ns=half)
            dg[KV] = tn_grad("kv_proj_dw", u_kv, dv2, dg[KV], 0, D, 3 * D, "cs", s0=half, ns=half)
            du_kv = _mm("kv_proj_dx", dk2, wg[KV], mode="nt", M=N, N=D, K=3 * D, b_lay="cs", b_s0=0, b_ns=half)
            du_kv = _mm("kv_proj_dx", dv2, wg[KV], mode="nt", M=N, N=D, K=3 * D, b_lay="cs", b_s0=half, b_ns=half,
                        tm=1024, epi=_add, extras=[("mn", du_kv)])
            dh, d_kv_g, d_kv_scale, d_kv_shift, dy, d_gate1 = _normmod_bwd(du_kv, h_kv, kv_g.reshape(1, D), kv_scale, dh, B,
                                                                           below=below)
    grad_x = dh.reshape(B, S, D)

    own = [_sum_shards(r.reshape(N_CHIPS, -1, r.shape[-1])) for r in recv]

    d_kvmod = jnp.concatenate([d_kv_shift, d_kv_scale], axis=2).reshape(B, 2 * D)
    d_mod_all = jnp.concatenate([d_mods[l][s].reshape(B, 3 * D) for l in range(depth) for s in range(2)], axis=1)
    small = [
        d_mod_all, d_kvmod,
        jnp.stack([jnp.stack([d_ln_g[l][0].reshape(D), d_ln_g[l][1].reshape(D)]) for l in range(depth)]),
        jnp.stack([d_s5[l][0] for l in range(n_a)]), jnp.stack([d_s5[l][1] for l in range(n_a)]),
        jnp.stack([d_s5[l][2] for l in range(n_a)]),
        jnp.stack([d_s5[l][3] for l in range(n_a)]), jnp.stack([d_s5[l][4] for l in range(n_a)]),
        jnp.stack([d_s5[l][5] for l in range(n_a)]), jnp.stack([d_s5[l][6] for l in range(n_a)]),
        jnp.stack([d_s5[l][7].reshape(D) for l in range(n_a)]),
        d_kv_g.reshape(D), d_final_g.reshape(D),
    ]
    small_pack, small_layout, small_rows = _pack(small, D)
    small_buf, other = _all_gather_small(small_pack, _swap_with_sibling(own))
    small_sum = _sum_shards(small_buf.reshape(N_DEV, small_rows, D))
    red = lambda idx: _unpack(small_sum, small_layout, idx)
    per_dev = lambda idx: jnp.concatenate(
        [_unpack(small_buf[d * small_rows:(d + 1) * small_rows], small_layout, idx) for d in range(N_DEV)], axis=0)

    dm_all = per_dev(0).reshape(n_ex, n_mod, 3 * D)
    dm_cols = lax.dynamic_slice_in_dim(dm_all, chip * ada_cols, ada_cols, axis=2).reshape(n_ex, n_mod * ada_cols)
    g_ada_w = _mm("ada_dw", sc_all, _as4(dm_cols), mode="tn", M=D, N=n_mod * ada_cols, K=n_ex, b_lay="cs",
                  out_lay="cs", out4_shape=(n_mod, 1, D, ada_cols), out_ns=n_mod).reshape(ada_w.shape)
    dkvm_all = per_dev(1)
    dkvm_cols = lax.dynamic_slice_in_dim(dkvm_all, chip * kv_cols, kv_cols, axis=1)
    g_kv_ada_w = _mm("ada_dw", sc_all, _as4(dkvm_cols), mode="tn", M=D, N=kv_cols, K=n_ex, b_lay="cs")
    g_ada_b_full = (red(0)[0] + red(0)[1]).reshape(depth, 2, 3 * D) if B == 2 else jnp.sum(red(0), axis=0).reshape(depth, 2, 3 * D)
    g_ada_b = lax.dynamic_slice_in_dim(g_ada_b_full, chip * ada_cols, ada_cols, axis=2)
    g_kv_ada_b = red(1)[0] + red(1)[1] if B == 2 else jnp.sum(red(1), axis=0)
    g_ln_g = lax.dynamic_slice_in_dim(red(2), chip * (D // N_CHIPS), D // N_CHIPS, axis=2)
    g_ssm_d = lax.dynamic_slice_in_dim(red(10), chip * (D // N_CHIPS), D // N_CHIPS, axis=1)
    small_grads = {
        "ln_g": g_ln_g, "ada_b": g_ada_b, "ssm_lam_re": red(3), "ssm_lam_im": red(4), "ssm_log_dt": red(5),
        "ssm_b_re": red(6), "ssm_b_im": red(7), "ssm_c_re": red(8), "ssm_c_im": red(9), "ssm_d": g_ssm_d,
        "kv_g": red(11), "kv_ada_b": g_kv_ada_b, "final_g": red(12),
    }
    small_w = {"ln_g": (ln_g, m_ln_g, v_ln_g), "ada_b": (ada_b, m_ada_b, v_ada_b),
               "ssm_lam_re": (ssm_lam_re, m_ssm_lam_re, v_ssm_lam_re), "ssm_lam_im": (ssm_lam_im, m_ssm_lam_im, v_ssm_lam_im),
               "ssm_log_dt": (ssm_log_dt, m_ssm_log_dt, v_ssm_log_dt), "ssm_b_re": (ssm_b_re, m_ssm_b_re, v_ssm_b_re),
               "ssm_b_im": (ssm_b_im, m_ssm_b_im, v_ssm_b_im), "ssm_c_re": (ssm_c_re, m_ssm_c_re, v_ssm_c_re),
               "ssm_c_im": (ssm_c_im, m_ssm_c_im, v_ssm_c_im), "ssm_d": (ssm_d, m_ssm_d, v_ssm_d),
               "kv_g": (kv_g, m_kv_g, v_kv_g), "kv_ada_b": (kv_ada_b, m_kv_ada_b, v_kv_ada_b),
               "final_g": (final_g, m_final_g, v_final_g)}
    names = list(small_w)
    wp, lay_w, _ = _pack([small_w[n][0] for n in names], D)
    gp, _, _ = _pack([small_grads[n] for n in names], D)
    mp, _, _ = _pack([small_w[n][1] for n in names], D)
    vp, _, _ = _pack([small_w[n][2] for n in names], D)
    _, d_p, m_p, v_p = _adamw(wp, [gp], mp, vp)
    upd = {n: (small_grads[n].reshape(small_w[n][0].shape), _unpack(d_p, lay_w, i), _unpack(m_p, lay_w, i), _unpack(v_p, lay_w, i))
           for i, n in enumerate(names)}

    def big(w, m, v, g_own, g_other=None):
        C = w.shape[-1]
        gs = [g_own.reshape(-1, C)] + ([g_other.reshape(-1, C)] if g_other is not None else [])
        return tuple(t.reshape(w.shape) for t in _adamw(w.reshape(-1, C), gs, m.reshape(-1, C), v.reshape(-1, C)))

    upd["ssm_w_glu"] = big(ssm_w_glu, m_ssm_w_glu, v_ssm_w_glu, own[0], other[0])
    upd["w_kv"] = big(w_kv, m_w_kv, v_w_kv, own[1], other[1])
    upd["attn_w_q"] = big(attn_w_q, m_attn_w_q, v_attn_w_q, own[2], other[2])
    upd["attn_w_o"] = big(attn_w_o, m_attn_w_o, v_attn_w_o, own[3], other[3])
    upd["mlp_w1"] = big(mlp_w1, m_mlp_w1, v_mlp_w1, own[4], other[4])
    upd["mlp_w2"] = big(mlp_w2, m_mlp_w2, v_mlp_w2, own[5], other[5])
    upd["ada_w"] = big(ada_w, m_ada_w, v_ada_w, g_ada_w)
    upd["kv_ada_w"] = big(kv_ada_w, m_kv_ada_w, v_kv_ada_w, g_kv_ada_w)

    order = ["ln_g", "ada_w", "ada_b", "ssm_lam_re", "ssm_lam_im", "ssm_log_dt", "ssm_b_re", "ssm_b_im", "ssm_c_re",
             "ssm_c_im", "ssm_d", "ssm_w_glu", "kv_g", "kv_ada_w", "kv_ada_b", "w_kv", "attn_w_q", "attn_w_o", "mlp_w1",
             "mlp_w2", "final_g"]
    return (loss, grad_x, *[upd[n][0] for n in order], *[upd[n][1] for n in order], *[upd[n][2] for n in order],
            *[upd[n][3] for n in order])
```

```python
import functools
import math

import jax
import jax.numpy as jnp
from jax import lax
from jax.experimental import pallas as pl
from jax.experimental.pallas import tpu as pltpu

F32 = jnp.float32
BF16 = jnp.bfloat16
MESH = pl.DeviceIdType.MESH

EPS = 1e-6
NEG = -1e30
SSM_GROUP = 16
SSM_STATE = 64
HEAD_DIM = 64
ATTN_BLOCK = 128
DILATIONS = (1, 4, 16)
ADAM_LR, ADAM_B1, ADAM_B2, ADAM_EPS, ADAM_WD, ADAM_STEP = 0.001, 0.9, 0.999, 1e-08, 0.01, 10

LANES = 128
SUBLANES = 8
CHUNK_GROUPS = LANES // SSM_GROUP
CHUNK_STATE = CHUNK_GROUPS * SSM_STATE
VMEM_LIMIT = 56 * 1024 * 1024


def _div(dim, pref, mult):
    t = min(pref, dim) // mult * mult
    while t >= mult:
        if dim % t == 0:
            return t
        t -= mult
    return dim


def _params(*sem):
    return pltpu.CompilerParams(dimension_semantics=sem, vmem_limit_bytes=VMEM_LIMIT)


def _coords():
    return lax.axis_index("x"), lax.axis_index("y"), lax.axis_index("c")


def _other_chips(cx, cy):
    return [(1 - cx, cy), (cx, 1 - cy), (1 - cx, 1 - cy)]


class _Carry:
    def __init__(self, srcs, dsts, plan, n_remote, n_local, onward=None, n_onward=0):
        self.srcs, self.dsts, self.plan, self.n_remote, self.n_local = list(srcs), list(dsts), plan, n_remote, n_local
        self.onward, self.n_onward = onward, n_onward


def _carried_call(body, *, name, grid, in_specs, out_specs, out_shape, scratch_shapes, operands, sem, carry=None):
    if carry is None:
        outs = pl.pallas_call(body, name=name, grid=grid, in_specs=in_specs, out_specs=out_specs, out_shape=out_shape,
                              scratch_shapes=scratch_shapes, compiler_params=_params(*sem))(*operands)
        return list(outs), []
    n_in, n_out, n_scr = len(in_specs), len(out_specs), len(scratch_shapes)
    ns, nd = len(carry.srcs), len(carry.dsts)

    def wrapped(*refs):
        base_in, src_refs = refs[:n_in], refs[n_in:n_in + ns]
        o0 = n_in + ns + nd
        base_out, dst_refs = refs[o0:o0 + n_out], refs[o0 + n_out:o0 + n_out + nd]
        s0 = o0 + n_out + nd
        base_scr = refs[s0:s0 + n_scr]
        send_sems, recv_sems, local_sems = refs[s0 + n_scr:]
        pids = [pl.program_id(a) for a in range(len(grid))]
        first = functools.reduce(jnp.logical_and, [p == 0 for p in pids])
        last = functools.reduce(jnp.logical_and, [p == g - 1 for p, g in zip(pids, grid)])

        def remote_copies(moves, k0):
            return [pltpu.make_async_remote_copy(src_ref=s, dst_ref=d, send_sem=send_sems.at[k0 + i], recv_sem=recv_sems.at[k0 + i],
                                                 device_id=peer, device_id_type=MESH) for i, (s, d, peer) in enumerate(moves)]

        def copies():
            remote, local = carry.plan(src_refs, dst_refs, _coords())
            return remote_copies(remote, 0), [pltpu.make_async_copy(s, d, local_sems.at[i]) for i, (s, d) in enumerate(local)]

        @pl.when(first)
        def _():
            remote, local = copies()
            for cp in local + remote:
                cp.start()

        body(*base_in, *base_out, *base_scr)

        @pl.when(last)
        def _():
            remote, local = copies()
            for cp in remote:
                cp.wait_send()
                cp.wait_recv()
            for cp in local:
                cp.wait()
            if carry.onward is not None:
                second = remote_copies(carry.onward(src_refs, dst_refs, _coords()), carry.n_remote)
                for cp in second:
                    cp.start()
                for cp in second:
                    cp.wait_send()
                    cp.wait_recv()

    anyspec = pl.BlockSpec(memory_space=pl.ANY)
    outs = pl.pallas_call(
        wrapped, name=name, grid=grid, in_specs=list(in_specs) + [anyspec] * (ns + nd),
        out_specs=list(out_specs) + [anyspec] * nd,
        out_shape=list(out_shape) + [jax.ShapeDtypeStruct(d.shape, d.dtype) for d in carry.dsts],
        scratch_shapes=list(scratch_shapes) + [pltpu.SemaphoreType.DMA((carry.n_remote + carry.n_onward,)),
                                               pltpu.SemaphoreType.DMA((carry.n_remote + carry.n_onward,)),
                                               pltpu.SemaphoreType.DMA((max(carry.n_local, 1),))],
        input_output_aliases={n_in + ns + i: n_out + i for i in range(nd)},
        compiler_params=_params(*(["arbitrary"] * len(grid))),
    )(*operands, *carry.srcs, *carry.dsts)
    return list(outs[:n_out]), list(outs[n_out:])


def _mm(name, a, b4, *, mode, M, N, K, b_lay="cs", b_l=0, b_s0=0, b_ns=1, out_dtype=F32, out_lay=None, out4_shape=None,
        out_into=None, out_l=0, out_s0=0, out_ns=1, epi=None, extras=(), rows_per_ex=None, tm=2048, tn=1024, tk=1024,
        carry=None):
    _, _, bR, bC = b4.shape
    tm = _div(M, tm, SUBLANES if M % 16 else 16)
    brows, bcols = (N, K) if mode == "nt" else (K, N)
    if b_lay == "cs":
        assert bR == brows and bC * b_ns == bcols, (name, b4.shape, brows, bcols)
    else:
        assert bC == bcols and bR * b_ns == brows, (name, b4.shape, brows, bcols)
    n_lim = N
    k_lim = K
    if mode == "nt":
        if b_lay == "cs":
            k_lim = bC
        else:
            n_lim = bR
    else:
        if b_lay == "cs":
            n_lim = bC
        else:
            k_lim = bR
    if out_lay == "cs":
        oR, oC = out4_shape[2], out4_shape[3]
        assert oR == M and oC * out_ns == N, (name, out4_shape, M, N)
        n_lim = math.gcd(n_lim, oC)
    elif out_lay == "rs":
        oR, oC = out4_shape[2], out4_shape[3]
        assert oC == N and oR * out_ns == M, (name, out4_shape, M, N)
        tm = _div(oR, tm, SUBLANES)
    tn = _div(n_lim, tn, LANES)
    tk = _div(k_lim, tk, LANES if mode != "tn" else SUBLANES)
    if mode == "tn":
        tk = _div(k_lim, tk, 16) if k_lim % 16 == 0 else tk
    nk = K // tk
    grid = (M // tm, N // tn, nk)

    if mode == "tn":
        a_spec = pl.BlockSpec((tk, tm), lambda i, j, k: (k, i))
    else:
        a_spec = pl.BlockSpec((tm, tk), lambda i, j, k: (i, k))

    def b_index(ri, ci, br, bc):
        if b_lay == "cs":
            per = bC // bc
            return (b_s0 + ci // per, b_l, ri, ci % per)
        per = bR // br
        return (b_s0 + ri // per, b_l, ri % per, ci)

    if mode == "nt":
        b_spec = pl.BlockSpec((None, None, tn, tk), lambda i, j, k: b_index(j, k, tn, tk))
    else:
        b_spec = pl.BlockSpec((None, None, tk, tn), lambda i, j, k: b_index(k, j, tk, tn))

    in_specs = [a_spec, b_spec]
    operands = [a, b4]
    for kind, arr in extras:
        if kind == "mn":
            in_specs.append(pl.BlockSpec((tm, tn), lambda i, j, k: (i, j)))
        elif kind == "ex":
            per_ex = rows_per_ex // tm
            in_specs.append(pl.BlockSpec((None, 1, tn), lambda i, j, k: (i // per_ex, 0, j)))
        else:
            in_specs.append(pl.BlockSpec((1, tn), lambda i, j, k: (0, j)))
        operands.append(arr)
    n_extra = len(extras)

    multi = isinstance(out_dtype, tuple)
    n_out = len(out_dtype) if multi else 1
    if out_lay is None:
        out_shape = [jax.ShapeDtypeStruct((M, N), dt) for dt in (out_dtype if multi else (out_dtype,))]
        out_spec = [pl.BlockSpec((tm, tn), lambda i, j, k: (i, j)) for _ in range(n_out)]
    else:
        out_shape = [jax.ShapeDtypeStruct(tuple(out4_shape), out_dtype)]
        if out_lay == "cs":
            per_o = oC // tn
            out_spec = [pl.BlockSpec((None, None, tm, tn), lambda i, j, k: (out_s0 + j // per_o, out_l, i, j % per_o))]
        else:
            per_o = oR // tm
            out_spec = [pl.BlockSpec((None, None, tm, tn), lambda i, j, k: (out_s0 + i // per_o, out_l, i % per_o, j))]
    aliases = {}
    if out_into is not None:
        in_specs.append(pl.BlockSpec(memory_space=pl.ANY))
        operands.append(out_into)
        aliases = {len(operands) - 1: 0}

    dims = {"nn": (((1,), (0,)), ((), ())), "nt": (((1,), (1,)), ((), ())), "tn": (((0,), (0,)), ((), ()))}[mode]

    def body(a_ref, b_ref, *rest):
        extra_refs = rest[:n_extra]
        o_refs = rest[len(rest) - n_out - (nk > 1):len(rest) - (nk > 1)]

        def finish(r):
            if epi is not None:
                r = epi(r, *[e[...] for e in extra_refs])
            for o_ref, val in zip(o_refs, r if multi else (r,)):
                o_ref[...] = val.astype(o_ref.dtype)

        part = lax.dot_general(a_ref[...].astype(BF16), b_ref[...].astype(BF16), dims, preferred_element_type=F32)
        if nk == 1:
            finish(part)
            return
        acc = rest[-1]
        k = pl.program_id(2)

        @pl.when(k == 0)
        def _():
            acc[...] = part

        @pl.when(k != 0)
        def _():
            acc[...] += part

        @pl.when(k == nk - 1)
        def _():
            finish(acc[...])

    scratch = [pltpu.VMEM((tm, tn), F32)] if nk > 1 else []
    if carry is not None:
        assert out_into is None, name
        outs, moved = _carried_call(body, name=name, grid=grid, in_specs=in_specs, out_specs=out_spec, out_shape=out_shape,
                                    scratch_shapes=scratch, operands=operands, sem=("arbitrary",) * 3, carry=carry)
        return (tuple(outs) if multi else outs[0]), moved
    outs = pl.pallas_call(
        body, name=name, grid=grid, in_specs=in_specs, out_specs=out_spec, out_shape=out_shape,
        scratch_shapes=scratch, input_output_aliases=aliases,
        compiler_params=_params("parallel", "parallel", "arbitrary"),
    )(*operands)
    return tuple(outs) if multi else outs[0]


def _as4(w):
    return w.reshape((1, 1) + w.shape)


def _relu2(acc):
    r = jnp.maximum(acc, 0.0)
    return r * r


def _relu2_bwd(acc, r):
    return acc * (2.0 * jnp.sqrt(r.astype(F32)))


def _add(acc, e):
    return acc + e


def _gated_residual(acc, h, gate):
    return acc, h + gate * acc


def _modulated_norm(x, g, scale, shift):
    rstd = lax.rsqrt(jnp.mean(x * x, axis=-1, keepdims=True) + EPS)
    return ((x * rstd) * g) * (1.0 + scale) + shift


def _gated_residual_norm(acc, h, gate, g, scale, shift):
    h_new = h + gate * acc
    return acc, h_new, _modulated_norm(h_new, g, scale, shift)


def _row_tiles(N, B, pref=256):
    S = N // B
    tm = _div(S, pref, SUBLANES)
    return tm, S // tm


def _normmod(h, g, scale, shift, B):
    N, D = h.shape
    tm, per_ex = _row_tiles(N, B)

    def body(h_ref, g_ref, sc_ref, sh_ref, u_ref):
        u_ref[...] = _modulated_norm(h_ref[...], g_ref[...], sc_ref[...], sh_ref[...]).astype(u_ref.dtype)

    tok = pl.BlockSpec((tm, D), lambda i: (i, 0))
    vec = pl.BlockSpec((1, D), lambda i: (0, 0))
    ex = pl.BlockSpec((None, 1, D), lambda i: (i // per_ex, 0, 0))
    return pl.pallas_call(
        body, name="normmod_fwd", grid=(N // tm,), in_specs=[tok, vec, ex, ex], out_specs=tok,
        out_shape=jax.ShapeDtypeStruct((N, D), BF16), compiler_params=_params("parallel"),
    )(h, g, scale, shift)


def _normmod_bwd(du, h, g, scale, dh_in, B, below=None):
    N, D = h.shape
    tm, per_ex = _row_tiles(N, B)
    fused = below is not None

    def body(*refs):
        du_ref, h_ref, g_ref, sc_ref, dhin_ref = refs[:5]
        dh_ref, dg_ref, dsc_ref, dsh_ref = refs[5 + 2 * fused:9 + 2 * fused]
        i = pl.program_id(0)
        x = h_ref[...]
        gv = g_ref[...]
        d_u = du_ref[...].astype(F32)
        rstd = lax.rsqrt(jnp.mean(x * x, axis=-1, keepdims=True) + EPS)
        xn = x * rstd
        dyg = d_u * (1.0 + sc_ref[...])
        dxn = dyg * gv
        dh = dhin_ref[...] + rstd * (dxn - xn * jnp.mean(dxn * xn, axis=-1, keepdims=True))
        dh_ref[...] = dh
        sums = [(dsc_ref, jnp.sum(d_u * (xn * gv), axis=0, keepdims=True)), (dsh_ref, jnp.sum(d_u, axis=0, keepdims=True))]
        if fused:
            y_ref, gt_ref = refs[5:7]
            dy_ref, dgt_ref = refs[9 + 2 * fused:]
            dy_ref[...] = (gt_ref[...] * dh).astype(dy_ref.dtype)
            sums.append((dgt_ref, jnp.sum(dh * y_ref[...], axis=0, keepdims=True)))
        dg_t = jnp.sum(dyg * xn, axis=0, keepdims=True)

        @pl.when(i % per_ex == 0)
        def _():
            for ref, val in sums:
                ref[...] = val

        @pl.when(i % per_ex != 0)
        def _():
            for ref, val in sums:
                ref[...] += val

        @pl.when(i == 0)
        def _():
            dg_ref[...] = dg_t

        @pl.when(i != 0)
        def _():
            dg_ref[...] += dg_t

    tok = pl.BlockSpec((tm, D), lambda i: (i, 0))
    vec = pl.BlockSpec((1, D), lambda i: (0, 0))
    ex = pl.BlockSpec((None, 1, D), lambda i: (i // per_ex, 0, 0))
    per_ex_shape = jax.ShapeDtypeStruct((B, 1, D), F32)
    outs = pl.pallas_call(
        body, name="normmod_bwd", grid=(N // tm,), in_specs=[tok, tok, vec, ex, tok] + ([tok, ex] if fused else []),
        out_specs=[tok, vec, ex, ex] + ([tok, ex] if fused else []),
        out_shape=[jax.ShapeDtypeStruct((N, D), F32), jax.ShapeDtypeStruct((1, D), F32), per_ex_shape, per_ex_shape]
        + ([jax.ShapeDtypeStruct((N, D), BF16), per_ex_shape] if fused else []),
        compiler_params=_params("arbitrary"),
    )(du, h, g, scale, dh_in, *(below if fused else ()))
    return tuple(outs) if fused else tuple(outs) + (None, None)


def _residual_bwd(dh, gate, y, B):
    N, D = dh.shape
    tm, per_ex = _row_tiles(N, B)

    def body(dh_ref, gt_ref, y_ref, dy_ref, dgt_ref):
        i = pl.program_id(0)
        d = dh_ref[...]
        dy_ref[...] = (gt_ref[...] * d).astype(dy_ref.dtype)
        t = jnp.sum(d * y_ref[...], axis=0, keepdims=True)

        @pl.when(i % per_ex == 0)
        def _():
            dgt_ref[...] = t

        @pl.when(i % per_ex != 0)
        def _():
            dgt_ref[...] += t

    tok = pl.BlockSpec((tm, D), lambda i: (i, 0))
    ex = pl.BlockSpec((None, 1, D), lambda i: (i // per_ex, 0, 0))
    return pl.pallas_call(
        body, name="residual_bwd", grid=(N // tm,), in_specs=[tok, ex, tok], out_specs=[tok, ex],
        out_shape=[jax.ShapeDtypeStruct((N, D), BF16), jax.ShapeDtypeStruct((B, 1, D), F32)],
        compiler_params=_params("arbitrary"),
    )(dh, gate, y)


def _glu_residual_norm(zz, h, gate, g, scale, shift, B):
    N, D2 = zz.shape
    D = D2 // 2
    tm, per_ex = _row_tiles(N, B)

    def body(v_ref, g_ref, h_ref, gt_ref, ng_ref, sc_ref, sh_ref, y_ref, o_ref, u_ref):
        y = v_ref[...] * jax.nn.sigmoid(g_ref[...])
        y_ref[...] = y.astype(y_ref.dtype)
        h_new = h_ref[...] + gt_ref[...] * y
        o_ref[...] = h_new
        u_ref[...] = _modulated_norm(h_new, ng_ref[...], sc_ref[...], sh_ref[...]).astype(u_ref.dtype)

    tok = pl.BlockSpec((tm, D), lambda i: (i, 0))
    vec = pl.BlockSpec((1, D), lambda i: (0, 0))
    ex = pl.BlockSpec((None, 1, D), lambda i: (i // per_ex, 0, 0))
    return pl.pallas_call(
        body, name="glu_fwd", grid=(N // tm,),
        in_specs=[tok, pl.BlockSpec((tm, D), lambda i: (i, 1)), tok, ex, vec, ex, ex], out_specs=[tok, tok, tok],
        out_shape=[jax.ShapeDtypeStruct((N, D), BF16), jax.ShapeDtypeStruct((N, D), F32), jax.ShapeDtypeStruct((N, D), BF16)],
        compiler_params=_params("parallel"),
    )(zz, zz, h, gate, g, scale, shift)


def _glu_bwd(dy, zz):
    N, D2 = zz.shape
    D = D2 // 2
    tm = _div(N, 256, SUBLANES)

    def body(dy_ref, v_ref, g_ref, o_ref):
        d = dy_ref[...].astype(F32)
        s = jax.nn.sigmoid(g_ref[...])
        o_ref[...] = jnp.concatenate([d * s, d * v_ref[...] * s * (1.0 - s)], axis=1).astype(o_ref.dtype)

    return pl.pallas_call(
        body, name="glu_bwd", grid=(N // tm,),
        in_specs=[pl.BlockSpec((tm, D), lambda i: (i, 0)), pl.BlockSpec((tm, D), lambda i: (i, 0)),
                  pl.BlockSpec((tm, D), lambda i: (i, 1))],
        out_specs=pl.BlockSpec((tm, D2), lambda i: (i, 0)), out_shape=jax.ShapeDtypeStruct((N, D2), BF16),
        compiler_params=_params("parallel"),
    )(dy, zz, zz)


def _loss_head(h, g, target):
    N, D = h.shape
    tm = _div(N, 256, SUBLANES)

    def body(h_ref, g_ref, t_ref, loss_ref, dh_ref, dg_ref):
        i = pl.program_id(0)
        x = h_ref[...]
        gv = g_ref[...]
        rstd = lax.rsqrt(jnp.mean(x * x, axis=-1, keepdims=True) + EPS)
        xn = x * rstd
        err = xn * gv - t_ref[...]
        part = 0.5 * jnp.sum(jnp.sum(err * err, axis=-1, keepdims=True) / D, axis=0, keepdims=True)
        dy = err / D
        dxn = dy * gv
        dh_ref[...] = rstd * (dxn - xn * jnp.mean(dxn * xn, axis=-1, keepdims=True))
        dg_t = jnp.sum(dy * xn, axis=0, keepdims=True)
        part = jnp.broadcast_to(part, loss_ref.shape)

        @pl.when(i == 0)
        def _():
            loss_ref[...] = part
            dg_ref[...] = dg_t

        @pl.when(i != 0)
        def _():
            loss_ref[...] += part
            dg_ref[...] += dg_t

    tok = pl.BlockSpec((tm, D), lambda i: (i, 0))
    vec = pl.BlockSpec((1, D), lambda i: (0, 0))
    return pl.pallas_call(
        body, name="loss_head", grid=(N // tm,), in_specs=[tok, vec, tok],
        out_specs=[pl.BlockSpec((SUBLANES, LANES), lambda i: (0, 0)), tok, vec],
        out_shape=[jax.ShapeDtypeStruct((SUBLANES, LANES), F32), jax.ShapeDtypeStruct((N, D), F32),
                   jax.ShapeDtypeStruct((1, D), F32)],
        compiler_params=_params("arbitrary"),
    )(h, g, target)


def _swap_halves(x):
    half = x.shape[-1] // 2
    return jnp.concatenate([x[:, half:], x[:, :half]], axis=1)


def _gelu(y):
    return jax.nn.gelu(y)


def _gelu_grad(y):
    c0 = math.sqrt(2.0 / math.pi)
    inner = c0 * (y + 0.044715 * y * y * y)
    t = jnp.tanh(inner)
    return 0.5 * (1.0 + t) + 0.5 * y * (1.0 - t * t) * c0 * (1.0 + 3.0 * 0.044715 * y * y)


def _s5_discretize(lam_re, lam_im, log_dt, b_re, b_im, c_re, c_im):
    G = lam_re.shape[0]
    nch = G // CHUNK_GROUPS
    dt = jnp.exp(log_dt)[:, None]
    er = jnp.exp(lam_re * dt)
    a_re = er * jnp.cos(lam_im * dt)
    a_im = er * jnp.sin(lam_im * dt)
    den = lam_re * lam_re + lam_im * lam_im
    n_re, n_im = a_re - 1.0, a_im
    f_re = (n_re * lam_re + n_im * lam_im) / den
    f_im = (n_im * lam_re - n_re * lam_im) / den
    bb_re = f_re[..., None] * b_re - f_im[..., None] * b_im
    bb_im = f_re[..., None] * b_im + f_im[..., None] * b_re
    eye = jnp.eye(CHUNK_GROUPS, dtype=F32)

    def pack_b(bb):
        bb = bb.reshape(nch, CHUNK_GROUPS, SSM_STATE, SSM_GROUP)
        return jnp.einsum("jgpc,gh->jgchp", bb, eye).reshape(nch, LANES, CHUNK_STATE)

    def pack_c(cc):
        cc = cc.reshape(nch, CHUNK_GROUPS, SSM_GROUP, SSM_STATE)
        return jnp.einsum("jgcp,gh->jgphc", cc, eye).reshape(nch, CHUNK_STATE, LANES)

    bd = jnp.concatenate([pack_b(bb_re), pack_b(bb_im)], axis=2)
    cd = jnp.concatenate([pack_c(c_re), pack_c(-c_im)], axis=1)
    return bd, cd, a_re, a_im


S5_TILE = 1024
S5_SEG = S5_TILE // SUBLANES
S5_UNROLL = 8


def _s5_scan_coefs(lam_re, lam_im, log_dt, seg):
    G = lam_re.shape[0]
    nch = G // CHUNK_GROUPS
    dt = jnp.exp(log_dt)[:, None]
    rate = (lam_re * dt).reshape(nch, 1, CHUNK_STATE)
    freq = (lam_im * dt).reshape(nch, 1, CHUNK_STATE)

    def powers(ks):
        k = jnp.asarray(ks, F32)[None, :, None]
        er = jnp.exp(k * rate)
        re, im = er * jnp.cos(k * freq), er * jnp.sin(k * freq)
        return jnp.concatenate([re, re], axis=2), jnp.concatenate([-im, im], axis=2)

    pw = jnp.stack(powers(range(1, seg + 1)), axis=1)
    steps = (1, 2, 4)
    re, im = powers([s * seg for s in steps])
    row = jnp.arange(SUBLANES, dtype=jnp.int32)[None, None, :, None]
    shift = jnp.asarray(steps, jnp.int32)[None, :, None, None]

    def table(reverse):
        mask = (row < SUBLANES - shift) if reverse else (row >= shift)
        pair = jnp.stack([jnp.where(mask, re[:, :, None, :], 0.0),
                          jnp.where(mask, (-im if reverse else im)[:, :, None, :], 0.0)], axis=2)
        return pair.reshape(nch, 2 * len(steps), SUBLANES, 2 * CHUNK_STATE)

    return pw, table(False), table(True)


def _to_segments(dst_s, src_ref, seg):
    for j in range(SUBLANES):
        dst_s[pl.ds(j, seg, stride=SUBLANES), :] = src_ref[pl.ds(j * seg, seg), :].astype(F32)


def _from_segments(dst_ref, src_s, seg):
    for j in range(SUBLANES):
        dst_ref[pl.ds(j * seg, seg), :] = src_s[pl.ds(j, seg, stride=SUBLANES), :].astype(dst_ref.dtype)


def _seg_scan(x_ref, pw_ref, seg_ref, carry_ref, c_ref, seg, reverse):
    W = x_ref.shape[-1]
    tm = x_ref.shape[0]
    sgn = -1.0 if reverse else 1.0
    ar = jnp.broadcast_to(pw_ref[0, 0:1, :], (SUBLANES, W))
    ai = sgn * jnp.broadcast_to(pw_ref[1, 0:1, :], (SUBLANES, W))

    def rows(i):
        return pl.ds(pl.multiple_of(i * SUBLANES, SUBLANES), SUBLANES)

    def step(t, prev):
        i = (seg - 2 - t) if reverse else (t + 1)
        x = x_ref[rows(i), :] + ar * prev + ai * _swap_halves(prev)
        x_ref[rows(i), :] = x
        return x

    start = (seg - 1) * SUBLANES if reverse else 0
    edge = lax.fori_loop(0, seg - 1, step, x_ref[start:start + SUBLANES, :], unroll=S5_UNROLL)
    row = lax.broadcasted_iota(jnp.int32, (SUBLANES, W), 0)
    if reverse:
        f = jnp.where(row == SUBLANES - 1, carry_ref[...], pltpu.roll(edge, SUBLANES - 1, 0))
    else:
        f = jnp.where(row == 0, carry_ref[...], pltpu.roll(edge, 1, 0))
    for si, s in enumerate((1, 2, 4)):
        fs = pltpu.roll(f, (SUBLANES - s) if reverse else s, 0)
        f = f + seg_ref[2 * si] * fs + seg_ref[2 * si + 1] * _swap_halves(fs)
    c_ref[...] = f
    fsw = _swap_halves(f)

    def fix(i, _):
        k = (seg - 1 - i) if reverse else i
        x_ref[rows(i), :] = x_ref[rows(i), :] + pw_ref[0, pl.ds(k, 1), :] * f + (sgn * pw_ref[1, pl.ds(k, 1), :]) * fsw
        return 0

    lax.fori_loop(0, seg, fix, 0, unroll=S5_UNROLL)
    leaving = x_ref[0:1, :] if reverse else x_ref[tm - 1:tm, :]
    carry_ref[...] = jnp.broadcast_to(leaving, carry_ref.shape)


def _s5_fwd(u, bd, cd, pw, seg_f, d_skip, B, carry=None):
    N, D = u.shape
    S = N // B
    nch = D // LANES
    W = 2 * CHUNK_STATE
    tm, seg = S5_TILE, S5_SEG
    nt = S // tm

    def body(u_ref, bd_ref, cd_ref, pw_ref, seg_ref, d_ref, z_ref, cin_ref, x_s, carry, c_s, u_s, z_s):
        t = pl.program_id(2)

        @pl.when(t == 0)
        def _():
            carry[...] = jnp.zeros_like(carry)

        cin_ref[...] = carry[...]
        _to_segments(u_s, u_ref, seg)
        uf = u_s[...]
        x_s[...] = jnp.dot(uf.astype(BF16), bd_ref[...], preferred_element_type=F32)
        _seg_scan(x_s, pw_ref, seg_ref, carry, c_s, seg, False)
        y = jnp.dot(x_s[...].astype(BF16), cd_ref[...], preferred_element_type=F32) + d_ref[...] * uf
        z_s[...] = _gelu(y)
        _from_segments(z_ref, z_s, seg)

    (z, carries), moved = _carried_call(
        body, name="s5_fwd", grid=(nch, B, nt),
        in_specs=[pl.BlockSpec((tm, LANES), lambda j, b, t: (b * nt + t, j)),
                  pl.BlockSpec((None, LANES, W), lambda j, b, t: (j, 0, 0)),
                  pl.BlockSpec((None, W, LANES), lambda j, b, t: (j, 0, 0)),
                  pl.BlockSpec((None, 2, seg, W), lambda j, b, t: (j, 0, 0, 0)),
                  pl.BlockSpec((None, 6, SUBLANES, W), lambda j, b, t: (j, 0, 0, 0)),
                  pl.BlockSpec((1, LANES), lambda j, b, t: (0, j))],
        out_specs=[pl.BlockSpec((tm, LANES), lambda j, b, t: (b * nt + t, j)),
                   pl.BlockSpec((None, None, SUBLANES, W), lambda j, b, t: (j, b * nt + t, 0, 0))],
        out_shape=[jax.ShapeDtypeStruct((N, D), BF16), jax.ShapeDtypeStruct((nch, B * nt, SUBLANES, W), F32)],
        scratch_shapes=[pltpu.VMEM((tm, W), F32), pltpu.VMEM((SUBLANES, W), F32), pltpu.VMEM((SUBLANES, W), F32),
                        pltpu.VMEM((tm, LANES), F32), pltpu.VMEM((tm, LANES), F32)],
        operands=(u, bd, cd, pw, seg_f, d_skip), sem=("parallel", "arbitrary", "arbitrary"), carry=carry)
    return z, carries, moved


def _s5_bwd(u, dz, bd, cd, pw, seg_f, seg_b, d_skip, carries, B, carry=None):
    N, D = u.shape
    S = N // B
    nch = D // LANES
    W = 2 * CHUNK_STATE
    tm, seg = S5_TILE, S5_SEG
    nt = S // tm
    tn_dims = (((0,), (0,)), ((), ()))
    nt_dims = (((1,), (1,)), ((), ()))

    def body(u_ref, dz_ref, bd_ref, cd_ref, pw_ref, sf_ref, sb_ref, d_ref, cin_ref,
             du_ref, dbd_ref, dcd_ref, da_ref, dd_ref, x_s, l_s, carry, lcarry, c_s, lc_s, u_s, t_s):
        b = pl.program_id(1)
        t = pl.program_id(2)

        @pl.when((b == 0) & (t == 0))
        def _():
            dbd_ref[...] = jnp.zeros_like(dbd_ref)
            dcd_ref[...] = jnp.zeros_like(dcd_ref)
            da_ref[...] = jnp.zeros_like(da_ref)
            dd_ref[...] = jnp.zeros_like(dd_ref)

        @pl.when(t == 0)
        def _():
            lcarry[...] = jnp.zeros_like(lcarry)

        _to_segments(u_s, u_ref, seg)
        _to_segments(t_s, dz_ref, seg)
        uf = u_s[...]
        uv = uf.astype(BF16)
        carry[...] = cin_ref[...]
        x_s[...] = jnp.dot(uv, bd_ref[...], preferred_element_type=F32)
        _seg_scan(x_s, pw_ref, sf_ref, carry, c_s, seg, False)
        xb = x_s[...].astype(BF16)
        y = jnp.dot(xb, cd_ref[...], preferred_element_type=F32) + d_ref[...] * uf
        dy = t_s[...] * _gelu_grad(y)
        dd_ref[...] += jnp.sum(dy * uf, axis=0, keepdims=True)
        dyb = dy.astype(BF16)
        dcd_ref[...] += lax.dot_general(dyb, xb, tn_dims, preferred_element_type=F32)
        l_s[...] = lax.dot_general(dyb, cd_ref[...], nt_dims, preferred_element_type=F32)
        _seg_scan(l_s, pw_ref, sb_ref, lcarry, lc_s, seg, True)
        lb = l_s[...].astype(BF16)
        dbd_ref[...] += lax.dot_general(uv, lb, tn_dims, preferred_element_type=F32)
        t_s[...] = lax.dot_general(lb, bd_ref[...], nt_dims, preferred_element_type=F32) + d_ref[...] * dy
        _from_segments(du_ref, t_s, seg)
        def da_step(i, acc):
            lam = l_s[pl.ds(pl.multiple_of(i * SUBLANES, SUBLANES), SUBLANES), :]
            x_prev = x_s[pl.ds(pl.multiple_of((i - 1) * SUBLANES, SUBLANES), SUBLANES), :]
            return acc[0] + lam * x_prev, acc[1] + lam * _swap_halves(x_prev)

        lam_0, c_in = l_s[:SUBLANES, :], c_s[...]
        acc_re, acc_im = lax.fori_loop(1, seg, da_step, (lam_0 * c_in, lam_0 * _swap_halves(c_in)), unroll=S5_UNROLL)
        da_ref[0:1, :] += jnp.sum(acc_re, axis=0, keepdims=True)
        da_ref[1:2, :] += jnp.sum(acc_im, axis=0, keepdims=True)

    tile = lambda j, b, t: (b * nt + (nt - 1 - t), j)
    chunk3 = lambda j, b, t: (j, 0, 0)
    chunk4 = lambda j, b, t: (j, 0, 0, 0)
    outs, moved = _carried_call(
        body, name="s5_bwd", grid=(nch, B, nt),
        in_specs=[pl.BlockSpec((tm, LANES), tile), pl.BlockSpec((tm, LANES), tile),
                  pl.BlockSpec((None, LANES, W), chunk3), pl.BlockSpec((None, W, LANES), chunk3),
                  pl.BlockSpec((None, 2, seg, W), chunk4), pl.BlockSpec((None, 6, SUBLANES, W), chunk4),
                  pl.BlockSpec((None, 6, SUBLANES, W), chunk4), pl.BlockSpec((1, LANES), lambda j, b, t: (0, j)),
                  pl.BlockSpec((None, None, SUBLANES, W), lambda j, b, t: (j, b * nt + (nt - 1 - t), 0, 0))],
        out_specs=[pl.BlockSpec((tm, LANES), tile), pl.BlockSpec((None, LANES, W), chunk3),
                   pl.BlockSpec((None, LANES, W), chunk3), pl.BlockSpec((None, 2, W), chunk3),
                   pl.BlockSpec((1, LANES), lambda j, b, t: (0, j))],
        out_shape=[jax.ShapeDtypeStruct((N, D), F32), jax.ShapeDtypeStruct((nch, LANES, W), F32),
                   jax.ShapeDtypeStruct((nch, LANES, W), F32), jax.ShapeDtypeStruct((nch, 2, W), F32),
                   jax.ShapeDtypeStruct((1, D), F32)],
        scratch_shapes=[pltpu.VMEM((tm, W), F32), pltpu.VMEM((tm, W), F32)] + [pltpu.VMEM((SUBLANES, W), F32)] * 4
        + [pltpu.VMEM((tm, LANES), F32)] * 2,
        operands=(u, dz, bd, cd, pw, seg_f, seg_b, d_skip, carries), sem=("parallel", "arbitrary", "arbitrary"), carry=carry)
    return (*outs, moved)


ATTN_HEADS = LANES // HEAD_DIM
ATTN_FWD_UNROLL = 8
ATTN_BWD_UNROLL = 8


def _attn_mask(n):
    qi = lax.broadcasted_iota(jnp.int32, (ATTN_HEADS * ATTN_BLOCK, 2 * ATTN_BLOCK), 0) % ATTN_BLOCK
    kj = lax.broadcasted_iota(jnp.int32, (ATTN_HEADS * ATTN_BLOCK, 2 * ATTN_BLOCK), 1)
    prev_ok = (kj < ATTN_BLOCK) & (kj >= qi) & (n > 0)
    return prev_ok | ((kj >= ATTN_BLOCK) & (kj - ATTN_BLOCK <= qi))


def _stack_heads(x):
    return jnp.concatenate([_only_head(x, h) for h in range(ATTN_HEADS)], axis=0)


def _stack_head_columns(x):
    return jnp.concatenate([x[:, h * HEAD_DIM:h * HEAD_DIM + 1] for h in range(ATTN_HEADS)], axis=0)


def _unstack_heads(x):
    return _per_head([x[h * ATTN_BLOCK:(h + 1) * ATTN_BLOCK] for h in range(ATTN_HEADS)])


def _head_lanes(h):
    lane = lax.broadcasted_iota(jnp.int32, (ATTN_BLOCK, LANES), 1)
    return (lane >= h * HEAD_DIM) & (lane < (h + 1) * HEAD_DIM)


def _per_head(cols):
    out = jnp.broadcast_to(cols[-1], (ATTN_BLOCK, LANES))
    for h in range(len(cols) - 2, -1, -1):
        out = jnp.where(_head_lanes(h), jnp.broadcast_to(cols[h], (ATTN_BLOCK, LANES)), out)
    return out


def _only_head(x, h):
    return jnp.where(_head_lanes(h), x, 0.0).astype(BF16)


def _block_rows(tb, dil, nb):
    r = tb // nb
    n = tb % nb
    start = r + dil * ATTN_BLOCK * n
    startp = jnp.where(n > 0, start - dil * ATTN_BLOCK, start)
    return n, pl.ds(start, ATTN_BLOCK, stride=dil), pl.ds(startp, ATTN_BLOCK, stride=dil)


def _attn_fwd(q, k, v, B, carry=None):
    _, S, D3 = q.shape
    D = D3 // 3
    HP = D // LANES
    scale = HEAD_DIM ** -0.5
    n_blocks = S // ATTN_BLOCK
    nbr = len(DILATIONS)
    nt_dims = (((1,), (1,)), ((), ()))

    def branch(dil, q_ref, k_ref, v_ref, acc, m_s, l_s):
        nb = (S // dil) // ATTN_BLOCK

        def blk(tb, _):
            n, rows, rowsp = _block_rows(tb, dil, nb)
            qb = q_ref[rows, :] * scale
            kk = jnp.concatenate([k_ref[rowsp, :], k_ref[rows, :]], axis=0).astype(BF16)
            vv = jnp.concatenate([v_ref[rowsp, :], v_ref[rows, :]], axis=0).astype(BF16)
            s = lax.dot_general(_stack_heads(qb), kk, nt_dims, preferred_element_type=F32)
            s = jnp.where(_attn_mask(n), s, NEG)
            m = jnp.max(s, axis=-1, keepdims=True)
            p = jnp.exp(s - m)
            m_s[rows, :] = _unstack_heads(m)
            l_s[rows, :] = _unstack_heads(jnp.sum(p, axis=-1, keepdims=True))
            acc[rows, :] = _unstack_heads(jnp.dot(p.astype(BF16), vv, preferred_element_type=F32))
            return 0

        lax.fori_loop(0, n_blocks, blk, 0, unroll=ATTN_FWD_UNROLL)

    def body(q_ref, k_ref, v_ref, o_ref, lse_ref, *scratch):
        accs, m_ss, l_ss = scratch[:nbr], scratch[nbr:2 * nbr], scratch[2 * nbr:]
        g = pl.program_id(2)
        for gi, dil in enumerate(DILATIONS):
            pl.when(g == gi)(functools.partial(branch, dil, q_ref, k_ref, v_ref, accs[gi], m_ss[gi], l_ss[gi]))

        @pl.when(g == nbr - 1)
        def _():
            def fin(i, _):
                rows = pl.ds(pl.multiple_of(i * ATTN_BLOCK, ATTN_BLOCK), ATTN_BLOCK)
                ms = [m[rows, :] for m in m_ss]
                m_all = functools.reduce(jnp.maximum, ms)
                ws = [jnp.exp(m - m_all) for m in ms]
                den = sum(w * l[rows, :] for w, l in zip(ws, l_ss))
                o_ref[rows, :] = sum(w * a[rows, :] for w, a in zip(ws, accs)) / den
                lse_ref[rows, :] = m_all + jnp.log(den)
                return 0

            lax.fori_loop(0, n_blocks, fin, 0)

    br = pl.BlockSpec((None, S, LANES), lambda b, hp, g: (b, 0, g * HP + hp))
    hd = pl.BlockSpec((None, S, LANES), lambda b, hp, g: (b, 0, hp))
    (o, lse), moved = _carried_call(
        body, name="attn_fwd", grid=(B, HP, nbr), in_specs=[br, br, br], out_specs=[hd, hd],
        out_shape=[jax.ShapeDtypeStruct((B, S, D), F32), jax.ShapeDtypeStruct((B, S, D), F32)],
        scratch_shapes=[pltpu.VMEM((S, LANES), F32)] * (3 * nbr),
        operands=(q, k, v), sem=("parallel", "parallel", "arbitrary"), carry=carry)
    return o, lse, moved


def _attn_bwd(q, k, v, o, lse, do, dk_prev, dv_prev, B, last, carry=None):
    _, S, D3 = q.shape
    D = D3 // 3
    HP = D // LANES
    scale = HEAD_DIM ** -0.5
    n_blocks = S // ATTN_BLOCK
    has_prev = dk_prev is not None
    nt_dims = (((1,), (1,)), ((), ()))
    tn_dims = (((0,), (0,)), ((), ()))

    def branch(dil, q_ref, k_ref, v_ref, lse_ref, do_ref, dq_s, dk_c, dv_c, delta, dk_p, dv_p):
        nb = (S // dil) // ATTN_BLOCK

        def blk(tb, _):
            n, rows, rowsp = _block_rows(tb, dil, nb)
            qb = q_ref[rows, :] * scale
            dob, lb, db = do_ref[rows, :], lse_ref[rows, :], delta[rows, :]
            kk = jnp.concatenate([k_ref[rowsp, :], k_ref[rows, :]], axis=0).astype(BF16)
            vv = jnp.concatenate([v_ref[rowsp, :], v_ref[rows, :]], axis=0).astype(BF16)
            qs, dos = _stack_heads(qb), _stack_heads(dob)
            s = lax.dot_general(qs, kk, nt_dims, preferred_element_type=F32)
            p = jnp.where(_attn_mask(n), jnp.exp(s - _stack_head_columns(lb)), 0.0)
            dp = lax.dot_general(dos, vv, nt_dims, preferred_element_type=F32)
            ds = (p * (dp - _stack_head_columns(db))).astype(BF16)
            dkk = lax.dot_general(ds, qs, tn_dims, preferred_element_type=F32)
            dvv = lax.dot_general(p.astype(BF16), dos, tn_dims, preferred_element_type=F32)
            dq_s[rows, :] = _unstack_heads(jnp.dot(ds, kk, preferred_element_type=F32)) * scale
            dk_p[rowsp, :] = dkk[:ATTN_BLOCK]
            dv_p[rowsp, :] = dvv[:ATTN_BLOCK]
            dk_c[rows, :] = dkk[ATTN_BLOCK:]
            dv_c[rows, :] = dvv[ATTN_BLOCK:]
            return 0

        lax.fori_loop(0, n_blocks, blk, 0, unroll=ATTN_BWD_UNROLL)

    def body(*refs):
        q_ref, k_ref, v_ref, o_ref, lse_ref, do_ref = refs[:6]
        n_in = 8 if has_prev else 6
        dq_ref, dk_ref, dv_ref, delta, dk_p, dv_p, dq_s, dk_c, dv_c = refs[n_in:n_in + 9]
        g = pl.program_id(2)

        @pl.when(g == 0)
        def _():
            def dl(i, _):
                rows = pl.ds(pl.multiple_of(i * ATTN_BLOCK, ATTN_BLOCK), ATTN_BLOCK)
                prod = do_ref[rows, :] * o_ref[rows, :]
                delta[rows, :] = _per_head([jnp.sum(jnp.where(_head_lanes(h), prod, 0.0), axis=-1, keepdims=True)
                                            for h in range(ATTN_HEADS)])
                return 0

            lax.fori_loop(0, n_blocks, dl, 0)

        dk_p[...] = jnp.zeros_like(dk_p)
        dv_p[...] = jnp.zeros_like(dv_p)
        for gi, dil in enumerate(DILATIONS):
            pl.when(g == gi)(functools.partial(branch, dil, q_ref, k_ref, v_ref, lse_ref, do_ref, dq_s, dk_c, dv_c,
                                               delta, dk_p, dv_p))

        def fin(i, _):
            rows = pl.ds(pl.multiple_of(i * ATTN_BLOCK, ATTN_BLOCK), ATTN_BLOCK)
            dk_t = dk_c[rows, :] + dk_p[rows, :]
            dv_t = dv_c[rows, :] + dv_p[rows, :]
            if has_prev:
                dk_t = dk_t + refs[6][rows, :].astype(F32)
                dv_t = dv_t + refs[7][rows, :].astype(F32)
            dq_ref[rows, :] = dq_s[rows, :].astype(dq_ref.dtype)
            dk_ref[rows, :] = dk_t.astype(dk_ref.dtype)
            dv_ref[rows, :] = dv_t.astype(dv_ref.dtype)
            return 0

        lax.fori_loop(0, n_blocks, fin, 0)

    br = pl.BlockSpec((None, S, LANES), lambda b, hp, g: (b, 0, g * HP + hp))
    hd = pl.BlockSpec((None, S, LANES), lambda b, hp, g: (b, 0, hp))
    ins = [q, k, v, o, lse, do] + ([dk_prev, dv_prev] if has_prev else [])
    kv_dtype = BF16 if last else F32
    (dq, dk, dv), moved = _carried_call(
        body, name="attn_bwd", grid=(B, HP, len(DILATIONS)),
        in_specs=[br, br, br, hd, hd, hd] + ([br, br] if has_prev else []), out_specs=[br, br, br],
        out_shape=[jax.ShapeDtypeStruct(q.shape, BF16), jax.ShapeDtypeStruct(q.shape, kv_dtype),
                   jax.ShapeDtypeStruct(q.shape, kv_dtype)],
        scratch_shapes=[pltpu.VMEM((S, LANES), F32)] * 6,
        operands=ins, sem=("parallel", "parallel", "arbitrary"), carry=carry)
    return dq, dk, dv, moved


def _adamw(w, grads, m, v):
    R, C = w.shape
    tr = _div(R, 256, SUBLANES)
    ng = len(grads)
    c1 = 1.0 - ADAM_B1 ** ADAM_STEP
    c2 = 1.0 - ADAM_B2 ** ADAM_STEP

    def body(*refs):
        w_ref, m_ref, v_ref = refs[0], refs[1 + ng], refs[2 + ng]
        d_ref, mo_ref, vo_ref = refs[3 + ng:6 + ng]
        g = refs[1][...]
        if ng == 2:
            g = g + refs[2][...]
            refs[6 + ng][...] = g
        mn = ADAM_B1 * m_ref[...] + (1.0 - ADAM_B1) * g
        vn = ADAM_B2 * v_ref[...] + (1.0 - ADAM_B2) * (g * g)
        d_ref[...] = -ADAM_LR * ((mn / c1) / (jnp.sqrt(vn / c2) + ADAM_EPS) + ADAM_WD * w_ref[...])
        mo_ref[...] = mn
        vo_ref[...] = vn

    blk = pl.BlockSpec((tr, C), lambda i: (i, 0))
    n_out = 3 + (ng == 2)
    outs = pl.pallas_call(
        body, name="adamw", grid=(R // tr,), in_specs=[blk] * (3 + ng), out_specs=[blk] * n_out,
        out_shape=[jax.ShapeDtypeStruct((R, C), F32)] * n_out, compiler_params=_params("parallel"),
    )(w, *grads, m, v)
    return (outs[3] if ng == 2 else grads[0],) + tuple(outs[:3])


def _sum_shards(recv):
    n, R, C = recv.shape
    tr = _div(R, 256, SUBLANES if recv.dtype == F32 else 2 * SUBLANES)

    def body(r_ref, o_ref):
        s = r_ref[0].astype(F32)
        for i in range(1, n):
            s = s + r_ref[i].astype(F32)
        o_ref[...] = s

    return pl.pallas_call(
        body, name="sum_shards", grid=(R // tr,), in_specs=[pl.BlockSpec((n, tr, C), lambda i: (0, i, 0))],
        out_specs=pl.BlockSpec((tr, C), lambda i: (i, 0)), out_shape=jax.ShapeDtypeStruct((R, C), F32),
        compiler_params=_params("parallel"),
    )(recv)


N_DEV = 8
N_CHIPS = 4


def _all_gather_small(x, carry=None):
    m_per, n = x.shape

    def body(x_ref, out_ref, send_sems, recv_sems, local_sem):
        cx, cy, cc = _coords()
        me, sibling = (cx, cy, cc), (cx, cy, 1 - cc)
        chips = [(1 - cx, cy), (cx, 1 - cy), (1 - cx, 1 - cy)]

        def rows(px, py, pc):
            return out_ref.at[pl.ds((4 * px + 2 * py + pc) * m_per, m_per), :]

        def copy(k, block, to, src=None):
            return pltpu.make_async_remote_copy(
                src_ref=rows(*block) if src is None else src, dst_ref=rows(*block), send_sem=send_sems.at[k],
                recv_sem=recv_sems.at[k], device_id=to, device_id_type=MESH)

        mine = pltpu.make_async_copy(x_ref, rows(*me), local_sem)
        mine.start()
        first = [copy(0, me, sibling, src=x_ref)]
        first += [copy(1 + j, me, (*chip, cc), src=x_ref) for j, chip in enumerate(chips)]
        for cp in first:
            cp.start()
        passed = [copy(4 + j, (*chip, cc), sibling) for j, chip in enumerate(chips)]
        for j, chip in enumerate(chips):
            copy(1 + j, (*chip, cc), me).wait_recv()
            passed[j].start()
        copy(0, sibling, me).wait_recv()
        for j, chip in enumerate(chips):
            copy(4 + j, (*chip, 1 - cc), me).wait_recv()
        for cp in first + passed:
            cp.wait_send()
        mine.wait()

    (out,), moved = _carried_call(
        body, name="all_gather_small", grid=(1,), out_shape=[jax.ShapeDtypeStruct((N_DEV * m_per, n), x.dtype)],
        in_specs=[pl.BlockSpec(memory_space=pltpu.VMEM)], out_specs=[pl.BlockSpec(memory_space=pltpu.VMEM)],
        scratch_shapes=[pltpu.SemaphoreType.DMA((7,)), pltpu.SemaphoreType.DMA((7,)), pltpu.SemaphoreType.DMA],
        operands=(x,), sem=("arbitrary",), carry=carry)
    return out, moved


def _layer_moves(kind, arrays_from, arrays_to, pieces, layer_major=()):
    used = sorted({w for w, _ in pieces})
    pos = {w: i for i, w in enumerate(used)}
    gather = kind == "gather"

    def half(ref, c):
        rows = ref.shape[0] // 2
        return ref.at[pl.ds(c * rows, rows), :]

    def slot(d, w, chip, l):
        return d.at[l, chip] if w in layer_major else d.at[chip, l]

    def plan(src_refs, dst_refs, me):
        cx, cy, cc = me
        mine = 2 * cx + cy
        remote, local = [], []
        for w, l in pieces:
            s, d = src_refs[pos[w]], dst_refs[pos[w]]
            for px, py in _other_chips(cx, cy):
                if gather:
                    remote.append((half(s.at[l], cc), half(slot(d, w, mine, l), cc), (px, py, cc)))
                else:
                    remote.append((s.at[2 * px + py, l], d.at[mine, l], (px, py, cc)))
            local.append((s.at[l], slot(d, w, mine, l)) if gather else (s.at[mine, l], d.at[mine, l]))
        return remote, local

    def onward(src_refs, dst_refs, me):
        cx, cy, cc = me
        moves = []
        for w, l in pieces:
            d = dst_refs[pos[w]]
            for px, py in _other_chips(cx, cy):
                landed = half(slot(d, w, 2 * px + py, l), cc)
                moves.append((landed, landed, (cx, cy, 1 - cc)))
        return moves

    n = 3 * len(pieces)
    carry = _Carry([arrays_from[w] for w in used], [arrays_to[w] for w in used], plan, n, len(pieces),
                   onward if gather else None, n if gather else 0)
    return carry, used


def _swap_with_sibling(sums):
    def plan(src_refs, dst_refs, me):
        cx, cy, cc = me
        return [(s, d, (cx, cy, 1 - cc)) for s, d in zip(src_refs, dst_refs)], []

    return _Carry(sums, [lax.empty(s.shape, s.dtype) for s in sums], plan, len(sums), 0)


def _pack(arrs, width):
    parts, layout, row = [], [], 0
    for a in arrs:
        flat = a.reshape(-1).astype(F32)
        rows = -(-flat.shape[0] // (width * SUBLANES)) * SUBLANES
        parts.append(jnp.pad(flat, (0, rows * width - flat.shape[0])).reshape(rows, width))
        layout.append((row, rows, a.shape))
        row += rows
    pad = -row % (8 * SUBLANES) if row > 8 * SUBLANES else 0
    if pad:
        parts.append(jnp.zeros((pad, width), F32))
    return jnp.concatenate(parts, axis=0), layout, row + pad


def _unpack(buf, layout, idx):
    row, rows, shape = layout[idx]
    size = math.prod(shape)
    return buf[row:row + rows].reshape(-1)[:size].reshape(shape)


def kernel(x, c, ln_g, ada_w, ada_b, ssm_lam_re, ssm_lam_im, ssm_log_dt, ssm_b_re, ssm_b_im, ssm_c_re, ssm_c_im, ssm_d, ssm_w_glu, kv_g, kv_ada_w, kv_ada_b, w_kv, attn_w_q, attn_w_o, mlp_w1, mlp_w2, final_g, loss_target, m_ln_g, m_ada_w, m_ada_b, m_ssm_lam_re, m_ssm_lam_im, m_ssm_log_dt, m_ssm_b_re, m_ssm_b_im, m_ssm_c_re, m_ssm_c_im, m_ssm_d, m_ssm_w_glu, m_kv_g, m_kv_ada_w, m_kv_ada_b, m_w_kv, m_attn_w_q, m_attn_w_o, m_mlp_w1, m_mlp_w2, m_final_g, v_ln_g, v_ada_w, v_ada_b, v_ssm_lam_re, v_ssm_lam_im, v_ssm_log_dt, v_ssm_b_re, v_ssm_b_im, v_ssm_c_re, v_ssm_c_im, v_ssm_d, v_ssm_w_glu, v_kv_g, v_kv_ada_w, v_kv_ada_b, v_w_kv, v_attn_w_q, v_attn_w_o, v_mlp_w1, v_mlp_w2, v_final_g):
    B, S, D = x.shape
    N = B * S
    depth = ln_g.shape[0]
    n_a = ssm_w_glu.shape[0]
    n_b = attn_w_q.shape[0]
    FF = mlp_w1.shape[2] * N_CHIPS
    cx, cy, cc = _coords()
    chip = 2 * cx + cy
    dev = 4 * cx + 2 * cy + cc
    n_ex = N_DEV * B
    ada_cols = ada_w.shape[-1]
    kv_cols = kv_ada_w.shape[-1]

    GLU, KV, Q, O, W1, W2 = range(6)
    shards = [ssm_w_glu.astype(BF16), w_kv.astype(BF16)[None], attn_w_q.astype(BF16), attn_w_o.astype(BF16),
              mlp_w1.astype(BF16), mlp_w2.astype(BF16)]
    row_sharded = (O, W2)
    wg = [lax.empty((s.shape[0], N_CHIPS) + s.shape[1:] if w in row_sharded else (N_CHIPS,) + s.shape, BF16)
          for w, s in enumerate(shards)]

    def whole_rows(w):
        L, _, R, C = wg[w].shape
        return wg[w].reshape(1, L, N_CHIPS * R, C)

    def landed(arrays, used, moved):
        for w, a in zip(used, moved):
            arrays[w] = a

    fetch_with = {}

    def carried_by(kind, l, *pieces):
        fetch_with.setdefault((kind, l), []).extend(pieces)

    assert n_a >= 1 and n_b >= 1, (n_a, n_b)
    carried_by("mixer", 0, *[(GLU, l) for l in range(n_a)], (W1, 0))
    carried_by("glu_proj", 0, (W2, 0))
    carried_by("mlp_up", 0, (Q, 0))
    carried_by("mlp_up", n_a - 1, (O, 0))
    carried_by("mixer", n_a - 1, (KV, 0))
    for l in range(1, depth):
        if l > n_a:
            carried_by("mixer", l, (W1, l))
        else:
            carried_by("mlp_down", l - 1, (W1, l))
        carried_by("mixer", l, (W2, l))
    for j in range(1, n_b):
        carried_by("mixer", n_a + j - 1, (Q, j), (O, j))

    def fetch(kind, l):
        pieces = fetch_with.get((kind, l))
        if not pieces:
            return None, []
        return _layer_moves("gather", shards, wg, pieces, layer_major=row_sharded)

    def mm_carrying(kind, l, *args, **kw):
        carry, used = fetch(kind, l)
        if carry is None:
            return _mm(kind, *args, **kw)
        out, moved = _mm(kind, *args, carry=carry, **kw)
        landed(wg, used, moved)
        return out

    c_pack, c_layout, _ = _pack([c], D)
    c_all_buf, _ = _all_gather_small(c_pack)
    c_rows = c_pack.shape[0]
    c_all = jnp.concatenate([_unpack(c_all_buf[d * c_rows:(d + 1) * c_rows], c_layout, 0) for d in range(N_DEV)], axis=0)
    sc_all = jax.nn.silu(c_all).astype(BF16)
    n_mod = depth * 2
    ada_w8 = ada_w.reshape(n_mod, 1, D, ada_cols)
    ada_b_row = ada_b.reshape(1, n_mod * ada_cols)
    mod_local = _mm("ada_fwd", sc_all, ada_w8, mode="nn", M=n_ex, N=n_mod * ada_cols, K=D, b_lay="cs", b_ns=n_mod,
                    epi=_add, extras=[("n", ada_b_row)])
    kv_ada_b_local = lax.dynamic_slice(kv_ada_b.reshape(N_CHIPS, kv_cols), (chip, 0), (1, kv_cols))
    kvmod_local = _mm("ada_fwd", sc_all, _as4(kv_ada_w), mode="nn", M=n_ex, N=kv_cols, K=D, epi=_add,
                      extras=[("n", kv_ada_b_local)])
    mod_pack, mod_layout, mod_rows = _pack([mod_local, kvmod_local, ln_g, ssm_d], D)
    mod_buf, _ = _all_gather_small(mod_pack)

    def from_chip(j, idx):
        d = 2 * j
        return _unpack(mod_buf[d * mod_rows:(d + 1) * mod_rows], mod_layout, idx)

    my_rows = lambda a: lax.dynamic_slice_in_dim(a, dev * B, B, axis=0)
    mods = jnp.concatenate([my_rows(from_chip(j, 0)).reshape(B, n_mod, ada_cols) for j in range(N_CHIPS)], axis=2)
    kvmod = jnp.concatenate([my_rows(from_chip(j, 1)) for j in range(N_CHIPS)], axis=1)
    ln_g_full = jnp.concatenate([from_chip(j, 2) for j in range(N_CHIPS)], axis=2)
    ssm_d_full = jnp.concatenate([from_chip(j, 3) for j in range(N_CHIPS)], axis=1)

    def mod3(l, s):
        mrow = mods[:, l * 2 + s]
        return [mrow[:, i * D:(i + 1) * D].reshape(B, 1, D) for i in range(3)]

    kv_shift, kv_scale = kvmod[:, :D].reshape(B, 1, D), kvmod[:, D:].reshape(B, 1, D)

    s5_tabs = []
    for l in range(n_a):
        prm = (ssm_lam_re[l], ssm_lam_im[l], ssm_log_dt[l], ssm_b_re[l], ssm_b_im[l], ssm_c_re[l], ssm_c_im[l])
        (bd, cd, _, _), disc_vjp = jax.vjp(_s5_discretize, *prm)
        pw, seg_f, seg_b = _s5_scan_coefs(ssm_lam_re[l], ssm_lam_im[l], ssm_log_dt[l], S5_SEG)
        s5_tabs.append((bd.astype(BF16), cd.astype(BF16), pw, seg_f, seg_b, disc_vjp))

    h = x.reshape(N, D)
    saved = []
    k_all = v_all = None
    shift, scale, gate = mod3(0, 0)
    u = _normmod(h, ln_g_full[0, 0].reshape(1, D), scale, shift, B)
    for l in range(depth):
        sv = {}
        sv["h0"], sv["scale0"], sv["gate0"], sv["u0"] = h, scale, gate, u
        shift1, scale1, gate1 = mod3(l, 1)
        norm1 = [("n", ln_g_full[l, 1].reshape(1, D)), ("ex", scale1), ("ex", shift1)]
        carry, used = fetch("mixer", l)
        if l < n_a:
            bd, cd, pw, seg_f, _, _ = s5_tabs[l]
            z, carries, moved = _s5_fwd(u, bd, cd, pw, seg_f, ssm_d_full[l].reshape(1, D), B, carry)
            landed(wg, used, moved)
            zz = mm_carrying("glu_proj", l, z, wg[GLU], mode="nn", M=N, N=2 * D, K=D, b_lay="cs", b_l=l, b_ns=N_CHIPS)
            y, h, u = _glu_residual_norm(zz, h, gate, ln_g_full[l, 1].reshape(1, D), scale1, shift1, B)
            sv["z"], sv["carries"], sv["zz"] = z, carries, zz
        else:
            j = l - n_a
            q = _mm("q_proj", u, wg[Q], mode="nn", M=N, N=3 * D, K=D, b_lay="cs", b_l=j, b_ns=N_CHIPS)
            q3 = q.reshape(B, S, 3 * D)
            o, lse, moved = _attn_fwd(q3, k_all, v_all, B, carry)
            landed(wg, used, moved)
            o2 = o.reshape(N, D)
            y, h, u = _mm("o_proj", o2, whole_rows(O), mode="nn", M=N, N=D, K=D, b_l=j, tm=1024, out_dtype=(BF16, F32, BF16),
                          epi=_gated_residual_norm, extras=[("mn", h), ("ex", gate)] + norm1, rows_per_ex=S)
            sv["q"], sv["o"], sv["lse"] = q3, o, lse
        sv["y0"] = y
        sv["h1"], sv["scale1"], sv["gate1"], sv["u1"] = h, scale1, gate1, u
        r = mm_carrying("mlp_up", l, u, wg[W1], mode="nn", M=N, N=FF, K=D, b_lay="cs", b_l=l, b_ns=N_CHIPS,
                        out_dtype=BF16, epi=_relu2)
        if l + 1 < depth:
            shift, scale, gate = mod3(l + 1, 0)
            y, h, u = mm_carrying(
                "mlp_down", l, r, whole_rows(W2), mode="nn", M=N, N=D, K=FF, b_l=l, tk=2048, tm=512,
                out_dtype=(BF16, F32, BF16), epi=_gated_residual_norm, rows_per_ex=S,
                extras=[("mn", h), ("ex", gate1), ("n", ln_g_full[l + 1, 0].reshape(1, D)), ("ex", scale), ("ex", shift)])
        else:
            y, h = mm_carrying("mlp_down", l, r, whole_rows(W2), mode="nn", M=N, N=D, K=FF, b_l=l, tk=2048, tm=1024,
                               out_dtype=(BF16, F32), epi=_gated_residual, extras=[("mn", h), ("ex", gate1)], rows_per_ex=S)
        sv["r"], sv["y1"] = r, y
        saved.append(sv)
        if l == n_a - 1:
            h_kv = h
            u_kv = _normmod(h, kv_g.reshape(1, D), kv_scale, kv_shift, B)
            half = N_CHIPS // 2
            k_all = _mm("kv_proj", u_kv, wg[KV], mode="nn", M=N, N=3 * D, K=D, b_lay="cs", b_s0=0, b_ns=half).reshape(B, S, 3 * D)
            v_all = _mm("kv_proj", u_kv, wg[KV], mode="nn", M=N, N=3 * D, K=D, b_lay="cs", b_s0=half, b_ns=half).reshape(B, S, 3 * D)

    loss_buf, dh, d_final_g = _loss_head(h, final_g.reshape(1, D), loss_target.reshape(N, D))
    loss = lax.psum(loss_buf[0, 0], ("x", "y", "c"))

    dg = [lax.empty((N_CHIPS,) + s.shape, BF16) for s in shards]
    recv = [lax.empty((N_CHIPS,) + s.shape, BF16) for s in shards]

    def send(pieces):
        return _layer_moves("scatter", dg, recv, pieces)

    send_with = {l: [(W1, l), (W2, l)] for l in range(depth)}
    for l in range(n_a):
        send_with[l] += [(GLU, l)]
    for j in range(n_b):
        send_with[n_a + j] += [(O, j)]
        send_with[n_a + j - 1] += [(Q, j)]
    send_with[n_a - 1] += [(KV, 0)]
    d_ln_g = [[None, None] for _ in range(depth)]
    d_mods = [[None, None] for _ in range(depth)]
    d_s5 = [None] * n_a
    dk_acc = dv_acc = None
    half = N_CHIPS // 2

    def tn_grad(name, a, d, into, l, Mr, Nc, lay, s0=0, ns=N_CHIPS):
        return _mm(name, a, _as4(d), mode="tn", M=Mr, N=Nc, K=N, b_lay="cs", out_dtype=BF16, out_lay=lay,
                   out4_shape=into.shape, out_into=into, out_l=l, out_s0=s0, out_ns=ns, tk=2048)

    dy, d_gate1 = _residual_bwd(dh, saved[-1]["gate1"], saved[-1]["y1"], B)
    for l in reversed(range(depth)):
        sv = saved[l]
        dg[W2] = tn_grad("mlp_down_dw", sv["r"], dy, dg[W2], l, FF, D, "rs")
        da = _mm("mlp_down_dx", dy, whole_rows(W2), mode="nt", M=N, N=FF, K=D, b_l=l, out_dtype=BF16,
                 epi=_relu2_bwd, extras=[("mn", sv["r"])])
        dg[W1] = tn_grad("mlp_up_dw", sv["u1"], da, dg[W1], l, D, FF, "cs")
        du = _mm("mlp_up_dx", da, wg[W1], mode="nt", M=N, N=D, K=FF, b_lay="cs", b_l=l, b_ns=N_CHIPS)
        dh, dgv, d_scale1, d_shift1, dy, d_gate0 = _normmod_bwd(du, sv["h1"], ln_g_full[l, 1].reshape(1, D), sv["scale1"],
                                                                dh, B, below=(sv["y0"], sv["gate0"]))
        d_ln_g[l][1] = dgv
        d_mods[l][1] = jnp.concatenate([d_shift1, d_scale1, d_gate1], axis=2)
        if l < n_a:
            bd, cd, pw, seg_f, seg_b, disc_vjp = s5_tabs[l]
            dzz = _glu_bwd(dy, sv["zz"])
            dg[GLU] = tn_grad("glu_proj_dw", sv["z"], dzz, dg[GLU], l, D, 2 * D, "cs")
            dz = _mm("glu_proj_dx", dzz, wg[GLU], mode="nt", M=N, N=D, K=2 * D, b_lay="cs", b_l=l, b_ns=N_CHIPS)
            carry, used = send(send_with[l])
            du, d_bd, d_cd, d_a2, d_dskip, moved = _s5_bwd(sv["u0"], dz, bd, cd, pw, seg_f, seg_b,
                                                           ssm_d_full[l].reshape(1, D), sv["carries"], B, carry)
            landed(recv, used, moved)
            d_are = (d_a2[:, 0, :CHUNK_STATE] + d_a2[:, 0, CHUNK_STATE:]).reshape(-1, SSM_STATE)
            d_aim = (d_a2[:, 1, CHUNK_STATE:] - d_a2[:, 1, :CHUNK_STATE]).reshape(-1, SSM_STATE)
            d_s5[l] = disc_vjp((d_bd, jnp.swapaxes(d_cd, 1, 2), d_are, d_aim)) + (d_dskip,)
        else:
            j = l - n_a
            dg[O] = tn_grad("o_proj_dw", sv["o"].reshape(N, D), dy, dg[O], j, D, D, "rs")
            do = _mm("o_proj_dx", dy, whole_rows(O), mode="nt", M=N, N=D, K=D, b_l=j)
            carry, used = send(send_with[l])
            dq, dk_acc, dv_acc, moved = _attn_bwd(sv["q"], k_all, v_all, sv["o"], sv["lse"], do.reshape(B, S, D),
                                                  dk_acc, dv_acc, B, l == n_a, carry)
            landed(recv, used, moved)
            dq2 = dq.reshape(N, 3 * D)
            dg[Q] = tn_grad("q_proj_dw", sv["u0"], dq2, dg[Q], j, D, 3 * D, "cs")
            du = _mm("q_proj_dx", dq2, wg[Q], mode="nt", M=N, N=D, K=3 * D, b_lay="cs", b_l=j, b_ns=N_CHIPS)
        below = (saved[l - 1]["y1"], saved[l - 1]["gate1"]) if l > 0 else None
        dh, dgv, d_scale0, d_shift0, dy, d_gate1 = _normmod_bwd(du, sv["h0"], ln_g_full[l, 0].reshape(1, D), sv["scale0"],
                                                                dh, B, below=None if l == n_a else below)
        d_ln_g[l][0] = dgv
        d_mods[l][0] = jnp.concatenate([d_shift0, d_scale0, d_gate0], axis=2)
        if l == n_a:
            dk2, dv2 = dk_acc.reshape(N, 3 * D), dv_acc.reshape(N, 3 * D)
            dg[KV] = tn_grad("kv_proj_dw", u_kv, dk2, dg[KV], 0, D, 3 * D, "cs", s0=0, ns=half)
            dg[KV] = tn_grad("kv_proj_dw", u_kv, dv2, dg[KV], 0, D, 3 * D, "cs", s0=half, ns=half)
            du_kv = _mm("kv_proj_dx", dk2, wg[KV], mode="nt", M=N, N=D, K=3 * D, b_lay="cs", b_s0=0, b_ns=half)
            du_kv = _mm("kv_proj_dx", dv2, wg[KV], mode="nt", M=N, N=D, K=3 * D, b_lay="cs", b_s0=half, b_ns=half,
                        tm=1024, epi=_add, extras=[("mn", du_kv)])
            dh, d_kv_g, d_kv_scale, d_kv_shift, dy, d_gate1 = _normmod_bwd(du_kv, h_kv, kv_g.reshape(1, D), kv_scale, dh, B,
                                                                           below=below)
    grad_x = dh.reshape(B, S, D)

    own = [_sum_shards(r.reshape(N_CHIPS, -1, r.shape[-1])) for r in recv]

    d_kvmod = jnp.concatenate([d_kv_shift, d_kv_scale], axis=2).reshape(B, 2 * D)
    d_mod_all = jnp.concatenate([d_mods[l][s].reshape(B, 3 * D) for l in range(depth) for s in range(2)], axis=1)
    small = [
        d_mod_all, d_kvmod,
        jnp.stack([jnp.stack([d_ln_g[l][0].reshape(D), d_ln_g[l][1].reshape(D)]) for l in range(depth)]),
        jnp.stack([d_s5[l][0] for l in range(n_a)]), jnp.stack([d_s5[l][1] for l in range(n_a)]),
        jnp.stack([d_s5[l][2] for l in range(n_a)]),
        jnp.stack([d_s5[l][3] for l in range(n_a)]), jnp.stack([d_s5[l][4] for l in range(n_a)]),
        jnp.stack([d_s5[l][5] for l in range(n_a)]), jnp.stack([d_s5[l][6] for l in range(n_a)]),
        jnp.stack([d_s5[l][7].reshape(D) for l in range(n_a)]),
        d_kv_g.reshape(D), d_final_g.reshape(D),
    ]
    small_pack, small_layout, small_rows = _pack(small, D)
    small_buf, other = _all_gather_small(small_pack, _swap_with_sibling(own))
    small_sum = _sum_shards(small_buf.reshape(N_DEV, small_rows, D))
    red = lambda idx: _unpack(small_sum, small_layout, idx)
    per_dev = lambda idx: jnp.concatenate(
        [_unpack(small_buf[d * small_rows:(d + 1) * small_rows], small_layout, idx) for d in range(N_DEV)], axis=0)

    dm_all = per_dev(0).reshape(n_ex, n_mod, 3 * D)
    dm_cols = lax.dynamic_slice_in_dim(dm_all, chip * ada_cols, ada_cols, axis=2).reshape(n_ex, n_mod * ada_cols)
    g_ada_w = _mm("ada_dw", sc_all, _as4(dm_cols), mode="tn", M=D, N=n_mod * ada_cols, K=n_ex, b_lay="cs",
                  out_lay="cs", out4_shape=(n_mod, 1, D, ada_cols), out_ns=n_mod).reshape(ada_w.shape)
    dkvm_all = per_dev(1)
    dkvm_cols = lax.dynamic_slice_in_dim(dkvm_all, chip * kv_cols, kv_cols, axis=1)
    g_kv_ada_w = _mm("ada_dw", sc_all, _as4(dkvm_cols), mode="tn", M=D, N=kv_cols, K=n_ex, b_lay="cs")
    g_ada_b_full = (red(0)[0] + red(0)[1]).reshape(depth, 2, 3 * D) if B == 2 else jnp.sum(red(0), axis=0).reshape(depth, 2, 3 * D)
    g_ada_b = lax.dynamic_slice_in_dim(g_ada_b_full, chip * ada_cols, ada_cols, axis=2)
    g_kv_ada_b = red(1)[0] + red(1)[1] if B == 2 else jnp.sum(red(1), axis=0)
    g_ln_g = lax.dynamic_slice_in_dim(red(2), chip * (D // N_CHIPS), D // N_CHIPS, axis=2)
    g_ssm_d = lax.dynamic_slice_in_dim(red(10), chip * (D // N_CHIPS), D // N_CHIPS, axis=1)
    small_grads = {
        "ln_g": g_ln_g, "ada_b": g_ada_b, "ssm_lam_re": red(3), "ssm_lam_im": red(4), "ssm_log_dt": red(5),
        "ssm_b_re": red(6), "ssm_b_im": red(7), "ssm_c_re": red(8), "ssm_c_im": red(9), "ssm_d": g_ssm_d,
        "kv_g": red(11), "kv_ada_b": g_kv_ada_b, "final_g": red(12),
    }
    small_w = {"ln_g": (ln_g, m_ln_g, v_ln_g), "ada_b": (ada_b, m_ada_b, v_ada_b),
               "ssm_lam_re": (ssm_lam_re, m_ssm_lam_re, v_ssm_lam_re), "ssm_lam_im": (ssm_lam_im, m_ssm_lam_im, v_ssm_lam_im),
               "ssm_log_dt": (ssm_log_dt, m_ssm_log_dt, v_ssm_log_dt), "ssm_b_re": (ssm_b_re, m_ssm_b_re, v_ssm_b_re),
               "ssm_b_im": (ssm_b_im, m_ssm_b_im, v_ssm_b_im), "ssm_c_re": (ssm_c_re, m_ssm_c_re, v_ssm_c_re),
               "ssm_c_im": (ssm_c_im, m_ssm_c_im, v_ssm_c_im), "ssm_d": (ssm_d, m_ssm_d, v_ssm_d),
               "kv_g": (kv_g, m_kv_g, v_kv_g), "kv_ada_b": (kv_ada_b, m_kv_ada_b, v_kv_ada_b),
               "final_g": (final_g, m_final_g, v_final_g)}
    names = list(small_w)
    wp, lay_w, _ = _pack([small_w[n][0] for n in names], D)
    gp, _, _ = _pack([small_grads[n] for n in names], D)
    mp, _, _ = _pack([small_w[n][1] for n in names], D)
    vp, _, _ = _pack([small_w[n][2] for n in names], D)
    _, d_p, m_p, v_p = _adamw(wp, [gp], mp, vp)
    upd = {n: (small_grads[n].reshape(small_w[n][0].shape), _unpack(d_p, lay_w, i), _unpack(m_p, lay_w, i), _unpack(v_p, lay_w, i))
           for i, n in enumerate(names)}

    def big(w, m, v, g_own, g_other=None):
        C = w.shape[-1]
        gs = [g_own.reshape(-1, C)] + ([g_other.reshape(-1, C)] if g_other is not None else [])
        return tuple(t.reshape(w.shape) for t in _adamw(w.reshape(-1, C), gs, m.reshape(-1, C), v.reshape(-1, C)))

    upd["ssm_w_glu"] = big(ssm_w_glu, m_ssm_w_glu, v_ssm_w_glu, own[0], other[0])
    upd["w_kv"] = big(w_kv, m_w_kv, v_w_kv, own[1], other[1])
    upd["attn_w_q"] = big(attn_w_q, m_attn_w_q, v_attn_w_q, own[2], other[2])
    upd["attn_w_o"] = big(attn_w_o, m_attn_w_o, v_attn_w_o, own[3], other[3])
    upd["mlp_w1"] = big(mlp_w1, m_mlp_w1, v_mlp_w1, own[4], other[4])
    upd["mlp_w2"] = big(mlp_w2, m_mlp_w2, v_mlp_w2, own[5], other[5])
    upd["ada_w"] = big(ada_w, m_ada_w, v_ada_w, g_ada_w)
    upd["kv_ada_w"] = big(kv_ada_w, m_kv_ada_w, v_kv_ada_w, g_kv_ada_w)

    order = ["ln_g", "ada_w", "ada_b", "ssm_lam_re", "ssm_lam_im", "ssm_log_dt", "ssm_b_re", "ssm_b_im", "ssm_c_re",
             "ssm_c_im", "ssm_d", "ssm_w_glu", "kv_g", "kv_ada_w", "kv_ada_b", "w_kv", "attn_w_q", "attn_w_o", "mlp_w1",
             "mlp_w2", "final_g"]
    return (loss, grad_x, *[upd[n][0] for n in order], *[upd[n][1] for n in order], *[upd[n][2] for n in order],
            *[upd[n][3] for n in order])
```
